```python
import jax
import jax.numpy as jnp
from jax import lax
import numpy as np


D_MODEL = 2048
BATCH = 4
SEQ = 4096
DEPTH = 4

GRID_W = 64
CTX_LEN = 256
N_MIXERS = 2
N_RWKV = (DEPTH + 1) // 2
N_MLA = DEPTH // 2
EPS = 1e-6

RW_HEAD = 64
RW_HEADS = D_MODEL // RW_HEAD
LORA_DECAY = max(32, int(round(1.8 * D_MODEL ** 0.5 / 32)) * 32)
LORA_A = max(32, int(round(1.8 * D_MODEL ** 0.5 / 32)) * 32)
LORA_MV = max(32, int(round(1.3 * D_MODEL ** 0.5 / 32)) * 32)
LORA_GATE = max(32, int(round(0.6 * D_MODEL ** 0.8 / 32)) * 32)
GN_EPS = 64e-5

MLA_HEADS = 16
QK_NOPE = 128
QK_ROPE = 64
V_HEAD = 128
Q_LORA = D_MODEL // 4
KV_LORA = 512
ROPE_THETA = 10000.0
Q_BLOCK = 128
ATTN_SCALE = (QK_NOPE + QK_ROPE) ** -0.5

PEER_HEADS = 8
N_KEYS = 128
N_EXPERTS = N_KEYS * N_KEYS
PEER_TOPK = 16
D_KEY = 256
PEER_CHUNK = 128

kernel_name = 'hybrid_rwkv7_mla_peer_dit'


def rms_norm(x, g):
    xf = x.astype(jnp.float32)
    y = xf * lax.rsqrt(jnp.mean(xf * xf, axis=-1, keepdims=True) + EPS)
    return (y * g.astype(jnp.float32)).astype(x.dtype)


def modulate(x, g, shift, scale):
    return rms_norm(x, g) * (1.0 + scale) + shift


def split_heads(t, n_heads):
    return t.reshape(t.shape[:-1] + (n_heads, t.shape[-1] // n_heads))


def axial_rope_tables(T):
    rows = T // GRID_W
    row = jnp.repeat(jnp.arange(rows, dtype=jnp.float32), GRID_W)
    col = jnp.tile(jnp.arange(GRID_W, dtype=jnp.float32), rows)
    n_freq = QK_ROPE // 4
    inv_freq = ROPE_THETA ** (-jnp.arange(n_freq, dtype=jnp.float32) / n_freq)
    ang = jnp.concatenate([row[:, None] * inv_freq, col[:, None] * inv_freq], axis=-1)
    return jnp.cos(ang), jnp.sin(ang)


def apply_rope(t, cos, sin):
    t1, t2 = jnp.split(t.astype(jnp.float32), 2, axis=-1)
    cs, sn = cos[None, :, None, :], sin[None, :, None, :]
    return jnp.concatenate([t1 * cs - t2 * sn, t2 * cs + t1 * sn], axis=-1).astype(t.dtype)


def token_shift(x):
    prev = jnp.pad(x[:, :-1], ((0, 0), (1, 0), (0, 0)))
    nxt = jnp.pad(x[:, 1:], ((0, 0), (0, 1), (0, 0)))
    return 0.5 * (prev + nxt) - x


def rwkv7_features(h, v_first, v_lora, mix, w_rkv, w0, w1, w2, a0, a1, a2, g1, g2, k_k, k_a):
    xx = token_shift(h)
    xr, xw, xk, xv, xa, xg = [h + xx * mix[m] for m in range(6)]
    r = xr @ w_rkv[0]
    k = xk @ w_rkv[1]
    v = xv @ w_rkv[2]
    if v_lora is not None:
        v0, v1, v2 = v_lora
        v = v + (v_first - v) * jax.nn.sigmoid(v0 + (xv @ v1) @ v2)
    lora_w = jnp.einsum('zbtr,zrd->zbtd', jnp.tanh(jnp.einsum('btd,zdr->zbtr', xw, w1)), w2)
    w = -jax.nn.softplus(-(w0[:, None, None, :] + lora_w)) - 0.5
    lora_a = jnp.einsum('zbtr,zrd->zbtd', jnp.einsum('btd,zdr->zbtr', xa, a1), a2)
    a = jax.nn.sigmoid(a0[:, None, None, :] + lora_a)
    g = jax.nn.sigmoid(xg @ g1) @ g2
    kk = split_heads((k * k_k).astype(jnp.float32), RW_HEADS)
    kk = kk * lax.rsqrt(jnp.sum(kk * kk, axis=-1, keepdims=True) + 1e-12)
    k_dir = k[None] * (1.0 + (a - 1.0) * k_a)
    return r, w, k_dir, v, a, g, kk


def wkv7_inputs(feats, z):
    r, w, k_dir, v, a, g, kk = feats
    hf = lambda t: split_heads(t.astype(jnp.float32), RW_HEADS)
    decay = jnp.exp(-jnp.exp(hf(w[z])))
    return (hf(r), decay, hf(k_dir[z]), hf(v), -kk, kk * hf(a[z]))


def wkv7_scan(inputs, s0, reverse):
    def step(s, inp):
        r_t, w_t, k_t, v_t, a_t, b_t = inp
        sa = jnp.einsum('bhvk,bhk->bhv', s, a_t)
        s = s * w_t[:, :, None, :] + sa[..., :, None] * b_t[:, :, None, :] + v_t[..., :, None] * k_t[:, :, None, :]
        return s, jnp.einsum('bhvk,bhk->bhv', s, r_t)
    xs = tuple(jnp.moveaxis(t, 1, 0) for t in inputs)
    s_final, out = lax.scan(step, s0, xs, reverse=reverse)
    return jnp.moveaxis(out, 0, 1), s_final


def rwkv7_readout(wkv, feats, r_k, ln_w, ln_b, w_o):
    r, w, k_dir, v, a, g, kk = feats
    mu = jnp.mean(wkv, axis=-1, keepdims=True)
    var = jnp.mean(jnp.square(wkv - mu), axis=-1, keepdims=True)
    y = ((wkv - mu) * lax.rsqrt(var + GN_EPS)).reshape(r.shape) * ln_w + ln_b
    rk = split_heads((r * (k_dir[0] + k_dir[1]) * r_k).astype(jnp.float32), RW_HEADS)
    bonus = jnp.sum(rk, axis=-1, keepdims=True) * split_heads(v.astype(jnp.float32), RW_HEADS)
    y = y + bonus.reshape(r.shape)
    return (y.astype(r.dtype) * g) @ w_o


def rwkv7_mixer(h_c, h_l, v_first, v_lora, mix, w_rkv, w_o, w0, w1, w2, a0, a1, a2, g1, g2,
                k_k, k_a, r_k, ln_w, ln_b, need_ctx_out):
    shared = (mix, w_rkv, w0, w1, w2, a0, a1, a2, g1, g2, k_k, k_a)
    vf_c, vf_l = (None, None) if v_first is None else v_first
    f_c = rwkv7_features(h_c, vf_c, v_lora, *shared)
    f_l = rwkv7_features(h_l, vf_l, v_lora, *shared)
    B = h_l.shape[0]
    s0 = jnp.zeros((B, RW_HEADS, RW_HEAD, RW_HEAD), jnp.float32)
    o_cf, s_f = wkv7_scan(wkv7_inputs(f_c, 0), s0, reverse=False)
    o_cb, s_b = wkv7_scan(wkv7_inputs(f_c, 1), s0, reverse=True)
    o_lf, _ = wkv7_scan(wkv7_inputs(f_l, 0), s_f, reverse=False)
    o_lb, _ = wkv7_scan(wkv7_inputs(f_l, 1), s_b, reverse=True)
    out_l = rwkv7_readout(o_lf + o_lb, f_l, r_k, ln_w, ln_b, w_o)
    out_c = rwkv7_readout(o_cf + o_cb, f_c, r_k, ln_w, ln_b, w_o) if need_ctx_out else None
    return out_c, out_l, (f_c[3], f_l[3])


def mla_project(h, w_in, q_norm, kv_norm, w_uq, w_ukv, g_q, g_k, rope):
    B, T, _ = h.shape
    z = h @ w_in
    c_q, c_kv, k_rot = jnp.split(z, [Q_LORA, Q_LORA + KV_LORA], axis=-1)
    q = (rms_norm(c_q, q_norm) @ w_uq).reshape(B, T, MLA_HEADS, QK_NOPE + QK_ROPE)
    kv = (rms_norm(c_kv, kv_norm) @ w_ukv).reshape(B, T, MLA_HEADS, QK_NOPE + V_HEAD)
    k_nope, v = jnp.split(kv, [QK_NOPE], axis=-1)
    k = jnp.concatenate([k_nope, jnp.broadcast_to(k_rot[:, :, None, :], (B, T, MLA_HEADS, QK_ROPE))], axis=-1)
    q = rms_norm(q, g_q)
    k = rms_norm(k, g_k)
    if rope is not None:
        cos, sin = rope
        q = jnp.concatenate([q[..., :QK_NOPE], apply_rope(q[..., QK_NOPE:], cos, sin)], axis=-1)
        k = jnp.concatenate([k[..., :QK_NOPE], apply_rope(k[..., QK_NOPE:], cos, sin)], axis=-1)
    return q, k, v


def softmax_attend(q, k, v):
    s = jnp.einsum('bqhd,bkhd->bhqk', q, k).astype(jnp.float32) * ATTN_SCALE
    p = jax.nn.softmax(s, axis=-1).astype(v.dtype)
    return jnp.einsum('bhqk,bkhd->bqhd', p, v)


def blocked_attend(q, k, v):
    B, T, H, E = q.shape
    qb = jnp.moveaxis(q.reshape(B, T // Q_BLOCK, Q_BLOCK, H, E), 1, 0)
    ob = lax.map(lambda qi: softmax_attend(qi, k, v), qb)
    return jnp.moveaxis(ob, 0, 1).reshape(B, T, H, v.shape[-1])


def mla_mixer(h_c, h_l, rope, w_in, q_norm, kv_norm, w_uq, w_ukv, g_q, g_k, w_o, need_ctx_out):
    p = (w_in, q_norm, kv_norm, w_uq, w_ukv, g_q, g_k)
    q_c, k_c, v_c = mla_project(h_c, *p, None)
    q_l, k_l, v_l = mla_project(h_l, *p, rope)
    k_all = jnp.concatenate([k_c, k_l], axis=1)
    v_all = jnp.concatenate([v_c, v_l], axis=1)
    B, T, _ = h_l.shape
    out_l = blocked_attend(q_l, k_all, v_all).reshape(B, T, MLA_HEADS * V_HEAD) @ w_o
    out_c = None
    if need_ctx_out:
        out_c = softmax_attend(q_c, k_c, v_c).reshape(B, h_c.shape[1], MLA_HEADS * V_HEAD) @ w_o
    return out_c, out_l


def peer_ffn(h, w_q, q_norm, sub_keys, u, v):
    M, D = h.shape
    half = D_KEY // 2

    def chunk(hc):
        C = hc.shape[0]
        q = rms_norm((hc @ w_q).reshape(C, PEER_HEADS, D_KEY), q_norm)
        s1 = jnp.einsum('chd,nd->chn', q[..., :half], sub_keys[0]).astype(jnp.float32)
        s2 = jnp.einsum('chd,nd->chn', q[..., half:], sub_keys[1]).astype(jnp.float32)
        t1, i1 = lax.top_k(s1, PEER_TOPK)
        t2, i2 = lax.top_k(s2, PEER_TOPK)
        cand = (t1[..., :, None] + t2[..., None, :]).reshape(C, PEER_HEADS, PEER_TOPK * PEER_TOPK)
        cidx = (i1[..., :, None] * N_KEYS + i2[..., None, :]).reshape(C, PEER_HEADS, PEER_TOPK * PEER_TOPK)
        best, pos = lax.top_k(cand, PEER_TOPK)
        eidx = jnp.take_along_axis(cidx, pos, axis=-1)
        gate = jax.nn.softmax(best, axis=-1).astype(hc.dtype)
        act = jax.nn.gelu(jnp.einsum('chkd,cd->chk', u[eidx], hc), approximate=False)
        return jnp.einsum('chk,chkd->cd', gate * act, v[eidx])

    out = lax.map(chunk, h.reshape(M // PEER_CHUNK, PEER_CHUNK, D))
    return out.reshape(M, D)


def setup_inputs(seed: int = 0) -> dict:
    key = jax.random.key(seed)
    keys = jax.random.split(key, 48)
    counter = iter(range(48))

    def nrm(shape, scale=1.0):
        return jax.random.normal(keys[next(counter)], shape, jnp.float32) * scale

    def gain(shape):
        return 1.0 + nrm(shape, 0.02)

    def unif(shape, lo, hi):
        return jax.random.uniform(keys[next(counter)], shape, jnp.float32, lo, hi)

    D = D_MODEL
    qk = QK_NOPE + QK_ROPE
    return dict(
        x=nrm((BATCH, SEQ, D)),
        c=nrm((BATCH, D)),
        ctx=nrm((BATCH, CTX_LEN, D)),
        c_ctx=nrm((D,)),
        w_ada=nrm((DEPTH, D, 6 * D), 0.5 * D ** -0.5),
        b_ada=nrm((DEPTH, 6 * D), 0.02),
        norm1=gain((DEPTH, D)),
        norm2=gain((DEPTH, D)),
        rw_mix=unif((N_RWKV, 6, D), 0.0, 1.0),
        rw_wrkv=nrm((N_RWKV, 3, D, D), D ** -0.5),
        rw_wo=nrm((N_RWKV, D, D), D ** -0.5),
        rw_w0=unif((N_RWKV, 2, D), -5.0, 0.0),
        rw_w1=nrm((N_RWKV, 2, D, LORA_DECAY), D ** -0.5),
        rw_w2=nrm((N_RWKV, 2, LORA_DECAY, D), 0.5 * LORA_DECAY ** -0.5),
        rw_a0=nrm((N_RWKV, 2, D), 0.5),
        rw_a1=nrm((N_RWKV, 2, D, LORA_A), D ** -0.5),
        rw_a2=nrm((N_RWKV, 2, LORA_A, D), 0.5 * LORA_A ** -0.5),
        rw_v0=nrm((N_RWKV - 1, D), 0.5),
        rw_v1=nrm((N_RWKV - 1, D, LORA_MV), D ** -0.5),
        rw_v2=nrm((N_RWKV - 1, LORA_MV, D), 0.5 * LORA_MV ** -0.5),
        rw_g1=nrm((N_RWKV, D, LORA_GATE), D ** -0.5),
        rw_g2=nrm((N_RWKV, LORA_GATE, D), LORA_GATE ** -0.5),
        rw_kk=0.85 + nrm((N_RWKV, D), 0.05),
        rw_ka=1.0 + nrm((N_RWKV, D), 0.05),
        rw_rk=nrm((N_RWKV, D), 0.1),
        rw_lnw=gain((N_RWKV, D)),
        rw_lnb=nrm((N_RWKV, D), 0.02),
        mla_win=nrm((N_MLA, D, Q_LORA + KV_LORA + QK_ROPE), D ** -0.5),
        mla_qnorm=gain((N_MLA, Q_LORA)),
        mla_kvnorm=gain((N_MLA, KV_LORA)),
        mla_wuq=nrm((N_MLA, Q_LORA, MLA_HEADS * qk), Q_LORA ** -0.5),
        mla_wukv=nrm((N_MLA, KV_LORA, MLA_HEADS * (QK_NOPE + V_HEAD)), KV_LORA ** -0.5),
        mla_gq=gain((N_MLA, qk)),
        mla_gk=gain((N_MLA, qk)),
        mla_wo=nrm((N_MLA, MLA_HEADS * V_HEAD, D), (MLA_HEADS * V_HEAD) ** -0.5),
        peer_wq=nrm((DEPTH, D, PEER_HEADS * D_KEY), D ** -0.5),
        peer_qnorm=gain((DEPTH, D_KEY)),
        peer_keys=nrm((DEPTH, 2, N_KEYS, D_KEY // 2), (D_KEY // 2) ** -0.5),
        peer_u=nrm((DEPTH, N_EXPERTS, D), D ** -0.5),
        peer_v=nrm((DEPTH, N_EXPERTS, D), PEER_HEADS ** -0.5),
    )


def reference(x, c, ctx, c_ctx, w_ada, b_ada, norm1, norm2,
              rw_mix, rw_wrkv, rw_wo, rw_w0, rw_w1, rw_w2, rw_a0, rw_a1, rw_a2,
              rw_v0, rw_v1, rw_v2, rw_g1, rw_g2, rw_kk, rw_ka, rw_rk, rw_lnw, rw_lnb,
              mla_win, mla_qnorm, mla_kvnorm, mla_wuq, mla_wukv, mla_gq, mla_gk, mla_wo,
              peer_wq, peer_qnorm, peer_keys, peer_u, peer_v):
    B, T, D = x.shape
    L = ctx.shape[1]
    rope = axial_rope_tables(T)
    s_lat = jax.nn.silu(c)
    s_ctx = jax.nn.silu(c_ctx)
    v_first = None
    for i in range(DEPTH):
        last = i == DEPTH - 1
        j = i // N_MIXERS
        mod_l = jnp.split((s_lat @ w_ada[i] + b_ada[i])[:, None, :], 6, axis=-1)
        mod_c = jnp.split(s_ctx @ w_ada[i] + b_ada[i], 6, axis=-1)
        h_c = modulate(ctx, norm1[i], mod_c[0], mod_c[1])
        h_l = modulate(x, norm1[i], mod_l[0], mod_l[1])
        if i % N_MIXERS == 0:
            v_lora = None if j == 0 else (rw_v0[j - 1], rw_v1[j - 1], rw_v2[j - 1])
            o_c, o_l, v_cur = rwkv7_mixer(h_c, h_l, v_first, v_lora, rw_mix[j], rw_wrkv[j], rw_wo[j],
                                          rw_w0[j], rw_w1[j], rw_w2[j], rw_a0[j], rw_a1[j], rw_a2[j],
                                          rw_g1[j], rw_g2[j], rw_kk[j], rw_ka[j], rw_rk[j],
                                          rw_lnw[j], rw_lnb[j], not last)
            if j == 0:
                v_first = v_cur
        else:
            o_c, o_l = mla_mixer(h_c, h_l, rope, mla_win[j], mla_qnorm[j], mla_kvnorm[j], mla_wuq[j],
                                 mla_wukv[j], mla_gq[j], mla_gk[j], mla_wo[j], not last)
        x = x + mod_l[2] * o_l
        h2_l = modulate(x, norm2[i], mod_l[3], mod_l[4])
        if last:
            y_l = peer_ffn(h2_l.reshape(B * T, D), peer_wq[i], peer_qnorm[i], peer_keys[i], peer_u[i], peer_v[i])
        else:
            ctx = ctx + mod_c[2] * o_c
            h2_c = modulate(ctx, norm2[i], mod_c[3], mod_c[4])
            y = peer_ffn(jnp.concatenate([h2_c.reshape(B * L, D), h2_l.reshape(B * T, D)], axis=0),
                         peer_wq[i], peer_qnorm[i], peer_keys[i], peer_u[i], peer_v[i])
            ctx = ctx + mod_c[5] * y[:B * L].reshape(B, L, D)
            y_l = y[B * L:]
        x = x + mod_l[5] * y_l.reshape(B, T, D)
    return x
```

```python
import functools
import math

import jax
import jax.numpy as jnp
from jax import lax
from jax.experimental import pallas as pl
from jax.experimental.pallas import tpu as pltpu

F32 = jnp.float32
BF16 = jnp.bfloat16

EPS = 1e-6
GN_EPS = 64e-5
RW_HEAD = 64
WKV_CHUNK = 64
MLA_HEADS = 16
QK_NOPE = 128
QK_ROPE = 64
V_HEAD = 128
QK_PAD = 256
ROPE_THETA = 10000.0
GRID_W = 64
ATTN_SCALE = (QK_NOPE + QK_ROPE) ** -0.5
PEER_HEADS = 8
N_KEYS = 128
PEER_TOPK = 16
D_KEY = 256
LORA_PAD = 128

LANES = 128
SUBLANES = 8
VMEM_LIMIT = 56 * 1024 * 1024


def _cparams(*sem):
    return pltpu.CompilerParams(dimension_semantics=sem, vmem_limit_bytes=VMEM_LIMIT)


def _dot(a, b):
    return jnp.dot(a, b, preferred_element_type=F32)


def _dot_nt(a, b):
    return lax.dot_general(a, b, (((1,), (1,)), ((), ())), preferred_element_type=F32)


def _split2(x):
    hi = x.astype(BF16)
    lo = (x - hi.astype(F32)).astype(BF16)
    return hi, lo


def _split3(x):
    hi = x.astype(BF16)
    r1 = x - hi.astype(F32)
    mid = r1.astype(BF16)
    lo = (r1 - mid.astype(F32)).astype(BF16)
    return hi, mid, lo


def _dot3(a, b):
    ah, al = _split2(a)
    bh, bl = _split2(b)
    return _dot(ah, bh) + (_dot(ah, bl) + _dot(al, bh))


def _dot3_nt(a, b):
    ah, al = _split2(a)
    bh, bl = _split2(b)
    return _dot_nt(ah, bh) + (_dot_nt(ah, bl) + _dot_nt(al, bh))


def _dot_exact_lhs(sel, x):
    hi, mid, lo = _split3(x)
    return _dot(sel, hi) + (_dot(sel, mid) + _dot(sel, lo))


def _modulate(x, g, shift, scale):
    ms = jnp.mean(x * x, axis=-1, keepdims=True)
    return (x * lax.rsqrt(ms + EPS) * g) * (1.0 + scale) + shift


def _sigmoid(x):
    return 1.0 / (1.0 + jnp.exp(-x))


def _softplus(y):
    return jnp.maximum(y, 0.0) + jnp.log(1.0 + jnp.exp(-jnp.abs(y)))


def _erf(x):
    return lax.erf(x)


def _gelu(x):
    return 0.5 * x * (1.0 + _erf(x * (2.0 ** -0.5)))


def _ada_kernel(s_ref, w_ref, b_ref, o_ref):
    s = s_ref[...]
    s = s * _sigmoid(s)
    o_ref[...] = _dot3(s, w_ref[...]) + b_ref[...]


def _adaln(cond8, w_ada, b_ada):
    depth, d, n = w_ada.shape
    tn = 1024
    return pl.pallas_call(
        _ada_kernel,
        grid=(depth, n // tn),
        in_specs=[
            pl.BlockSpec((SUBLANES, d), lambda l, j: (0, 0)),
            pl.BlockSpec((None, d, tn), lambda l, j: (l, 0, j)),
            pl.BlockSpec((None, 1, tn), lambda l, j: (l, 0, j)),
        ],
        out_specs=pl.BlockSpec((None, SUBLANES, tn), lambda l, j: (l, 0, j)),
        out_shape=jax.ShapeDtypeStruct((depth, SUBLANES, n), F32),
        compiler_params=_cparams("parallel", "parallel"),
        name="adaln",
    )(cond8, w_ada, b_ada.reshape(depth, 1, n))


class _Geom:
    def __init__(self, batch, ctx_len, seq_len, d_model, tm):
        self.b, self.l, self.t, self.d = batch, ctx_len, seq_len, d_model
        self.s = ctx_len + seq_len
        self.n = batch * self.s
        self.tm = tm
        assert ctx_len % tm == 0 and seq_len % tm == 0
        self.tpb = self.s // tm
        self.nct = ctx_len // tm
        self.ntiles = self.n // tm

    def mod_spec(self, nlead=0):
        tpb, nct = self.tpb, self.nct

        def imap(*ids):
            i = ids[nlead]
            return (i // tpb, ((i % tpb) >= nct).astype(jnp.int32), 0, 0)

        return pl.BlockSpec((None, None, 6, self.d), imap)


def _modmm_kernel(x_ref, g_ref, mod_ref, w_ref, o_ref, *h_ref, which):
    h = _modulate(x_ref[...], g_ref[...], mod_ref[3 * which:3 * which + 1, :],
                  mod_ref[3 * which + 1:3 * which + 2, :])
    if h_ref:
        h_ref[0][...] = h
    o_ref[...] = _dot(h.astype(BF16), w_ref[...])


def _mod_matmul(geo, x, g, mods, w, which, emit_h):
    n, d = x.shape
    nn = w.shape[1]
    tm = geo.tm
    out_shape = [jax.ShapeDtypeStruct((n, nn), F32)]
    out_specs = [pl.BlockSpec((tm, nn), lambda i: (i, 0))]
    if emit_h:
        out_shape.append(jax.ShapeDtypeStruct((n, d), F32))
        out_specs.append(pl.BlockSpec((tm, d), lambda i: (i, 0)))
    res = pl.pallas_call(
        functools.partial(_modmm_kernel, which=which),
        grid=(geo.ntiles,),
        in_specs=[
            pl.BlockSpec((tm, d), lambda i: (i, 0)),
            pl.BlockSpec((1, d), lambda i: (0, 0)),
            geo.mod_spec(),
            pl.BlockSpec((d, nn), lambda i: (0, 0)),
        ],
        out_specs=out_specs,
        out_shape=out_shape,
        compiler_params=_cparams("parallel"),
        name="mod_matmul",
    )(x, g.reshape(1, d), mods, w)
    return res if emit_h else res[0]


def _mmres_kernel(y_ref, w_ref, x_ref, mod_ref, o_ref, *, gidx):
    acc = _dot(y_ref[...], w_ref[...])
    o_ref[...] = x_ref[...] + mod_ref[gidx:gidx + 1, :] * acc


def _matmul_res(geo, y, w, x, mods, gidx):
    n, k = y.shape
    d = x.shape[1]
    tm = geo.tm
    return pl.pallas_call(
        functools.partial(_mmres_kernel, gidx=gidx),
        grid=(geo.ntiles,),
        in_specs=[
            pl.BlockSpec((tm, k), lambda i: (i, 0)),
            pl.BlockSpec((k, d), lambda i: (0, 0)),
            pl.BlockSpec((tm, d), lambda i: (i, 0)),
            geo.mod_spec(),
        ],
        out_specs=pl.BlockSpec((tm, d), lambda i: (i, 0)),
        out_shape=jax.ShapeDtypeStruct((n, d), F32),
        compiler_params=_cparams("parallel"),
        name="matmul_res",
    )(y, w, x, mods)


def _bmm_kernel(x_ref, w_ref, o_ref, *, acts):
    j = pl.program_id(0)
    y = _dot(x_ref[...], w_ref[...])
    out = y
    for jj, a in enumerate(acts):
        if a == "tanh":
            out = jnp.where(j == jj, jnp.tanh(y), out)
        elif a == "sigmoid":
            out = jnp.where(j == jj, _sigmoid(y), out)
    o_ref[...] = out.astype(o_ref.dtype)


def _bmm(x3, w3, src, acts, out_dtype, tm):
    _, n, k = x3.shape
    nj, _, nn = w3.shape
    src = tuple(src)

    def xmap(j, i):
        idx = jnp.int32(src[0])
        for jj in range(1, nj):
            idx = jnp.where(j == jj, jnp.int32(src[jj]), idx)
        return (idx, i, 0)

    return pl.pallas_call(
        functools.partial(_bmm_kernel, acts=tuple(acts)),
        grid=(nj, n // tm),
        in_specs=[
            pl.BlockSpec((None, tm, k), xmap),
            pl.BlockSpec((None, k, nn), lambda j, i: (j, 0, 0)),
        ],
        out_specs=pl.BlockSpec((None, tm, nn), lambda j, i: (j, i, 0)),
        out_shape=jax.ShapeDtypeStruct((nj, n, nn), out_dtype),
        compiler_params=_cparams("parallel", "parallel"),
        name="bmm",
    )(x3, w3)


def _rwmix_kernel(x_ref, xp_ref, xn_ref, g_ref, mod_ref, mix_ref, o_ref, *, tpb, nct):
    i = pl.program_id(0)
    tm = x_ref.shape[0]
    g = g_ref[...]
    shift = mod_ref[0:1, :]
    scale = mod_ref[1:2, :]
    h = _modulate(x_ref[...], g, shift, scale)
    hp = _modulate(xp_ref[...], g, shift, scale)[SUBLANES - 1:SUBLANES, :]
    hn = _modulate(xn_ref[...], g, shift, scale)[0:1, :]
    it = i % tpb
    first = jnp.logical_or(it == 0, it == nct)
    last = jnp.logical_or(it == nct - 1, it == tpb - 1)
    hp = jnp.where(first, 0.0, hp)
    hn = jnp.where(last, 0.0, hn)
    rows = lax.broadcasted_iota(jnp.int32, h.shape, 0)
    prev = jnp.where(rows == 0, hp, pltpu.roll(h, 1, axis=0))
    nxt = jnp.where(rows == tm - 1, hn, pltpu.roll(h, tm - 1, axis=0))
    xx = 0.5 * (prev + nxt) - h
    for m in range(6):
        o_ref[m] = (h + xx * mix_ref[m:m + 1, :]).astype(BF16)


def _rwkv_mix(geo, x, g, mods, mix):
    n, d = x.shape
    tm = geo.tm
    r8 = tm // SUBLANES
    nblk8 = n // SUBLANES
    return pl.pallas_call(
        functools.partial(_rwmix_kernel, tpb=geo.tpb, nct=geo.nct),
        grid=(geo.ntiles,),
        in_specs=[
            pl.BlockSpec((tm, d), lambda i: (i, 0)),
            pl.BlockSpec((SUBLANES, d), lambda i: (jnp.maximum(i * r8 - 1, 0), 0)),
            pl.BlockSpec((SUBLANES, d), lambda i: (jnp.minimum((i + 1) * r8, nblk8 - 1), 0)),
            pl.BlockSpec((1, d), lambda i: (0, 0)),
            geo.mod_spec(),
            pl.BlockSpec((6, d), lambda i: (0, 0)),
        ],
        out_specs=pl.BlockSpec((6, tm, d), lambda i: (0, i, 0)),
        out_shape=jax.ShapeDtypeStruct((6, n, d), BF16),
        compiler_params=_cparams("parallel"),
        name="rwkv_mix",
    )(x, x, x, g.reshape(1, d), mods, mix)


def _head_sum(x, bd):
    hi, lo = _split2(x)
    return _dot(hi, bd) + _dot(lo, bd)


def _rwfeat_kernel(k_ref, v_ref, tl_ref, w2_ref, a2_ref, g2_ref, v2_ref, pv_ref, bd_ref,
                   *rest, has_vlora):
    if has_vlora:
        vf_ref, lw_ref, kd_ref, as_ref, kk_ref, g_ref, vo_ref = rest
    else:
        lw_ref, kd_ref, as_ref, kk_ref, g_ref = rest
    k = k_ref[...]
    tw = tl_ref[0]
    ta = tl_ref[1]
    tg = tl_ref[2]
    w0 = pv_ref[0:2, :]
    a0 = pv_ref[2:4, :]
    k_k = pv_ref[5:6, :]
    k_a = pv_ref[6:7, :]
    for z in range(2):
        sl = slice(LORA_PAD * z, LORA_PAD * (z + 1))
        lora_w = _dot(tw[:, sl], w2_ref[z])
        w = -_softplus(-(w0[z:z + 1, :] + lora_w)) - 0.5
        lw_ref[z] = -jnp.exp(w)
        a_sig = _sigmoid(a0[z:z + 1, :] + _dot(ta[:, sl], a2_ref[z]))
        as_ref[z] = a_sig
        kd_ref[z] = k * (1.0 + (a_sig - 1.0) * k_a)
    g_ref[...] = _dot(tg, g2_ref[...])
    kkr = k * k_k
    ss = _head_sum(kkr * kkr, bd_ref[...])
    kk_ref[...] = kkr * lax.rsqrt(ss + 1e-12)
    if has_vlora:
        v = v_ref[...]
        tv = tl_ref[3]
        gate = _sigmoid(pv_ref[4:5, :] + _dot(tv[:, 0:LORA_PAD], v2_ref[...]))
        vo_ref[...] = v + (vf_ref[...] - v) * gate


def _rwkv_feat(geo, rkv, tl, w2p, a2p, g2, v2p, pvec, vfirst):
    _, n, d = rkv.shape
    tm, tc = geo.tm, 512
    has_vlora = vfirst is not None
    lt = tl.shape[2]
    ii = lax.broadcasted_iota(jnp.int32, (tc, tc), 0) // RW_HEAD
    jj = lax.broadcasted_iota(jnp.int32, (tc, tc), 1) // RW_HEAD
    bd = (ii == jj).astype(BF16)
    row = lambda i, j: (i, j)
    in_specs = [
        pl.BlockSpec((None, tm, tc), lambda i, j: (1, i, j)),
        pl.BlockSpec((None, tm, tc), lambda i, j: (2, i, j)),
        pl.BlockSpec((4, tm, lt), lambda i, j: (0, i, 0)),
        pl.BlockSpec((2, LORA_PAD, tc), lambda i, j: (0, 0, j)),
        pl.BlockSpec((2, LORA_PAD, tc), lambda i, j: (0, 0, j)),
        pl.BlockSpec((lt, tc), lambda i, j: (0, j)),
        pl.BlockSpec((LORA_PAD, tc), lambda i, j: (0, j)),
        pl.BlockSpec((SUBLANES, tc), lambda i, j: (0, j)),
        pl.BlockSpec((tc, tc), lambda i, j: (0, 0)),
    ]
    args = [rkv, rkv, tl, w2p, a2p, g2, v2p, pvec, bd]
    dir_spec = pl.BlockSpec((2, tm, tc), lambda i, j: (0, i, j))
    out_specs = [dir_spec, dir_spec, dir_spec, pl.BlockSpec((tm, tc), row), pl.BlockSpec((tm, tc), row)]
    out_shape = [jax.ShapeDtypeStruct((2, n, d), F32)] * 3 + [jax.ShapeDtypeStruct((n, d), F32)] * 2
    if has_vlora:
        in_specs.append(pl.BlockSpec((tm, tc), row))
        args.append(vfirst)
        out_specs.append(pl.BlockSpec((tm, tc), row))
        out_shape.append(jax.ShapeDtypeStruct((n, d), F32))
    res = pl.pallas_call(
        functools.partial(_rwfeat_kernel, has_vlora=has_vlora),
        grid=(geo.ntiles, d // tc),
        in_specs=in_specs,
        out_specs=out_specs,
        out_shape=out_shape,
        compiler_params=_cparams("parallel", "parallel"),
        name="rwkv_feat",
    )(*args)
    if has_vlora:
        lw, kd, asg, kk, g, v = res
    else:
        lw, kd, asg, kk, g = res
        v = rkv[2]
    return lw, kd, asg, kk, g, v


def _wkv_chunk_kernel(r_ref, v_ref, kk_ref, lw_ref, kd_ref, as_ref,
                      m_ref, ga_ref, rq_ref, o0_ref, pc_ref, *, npairs):
    c = WKV_CHUNK
    sgn = 1 - 2 * pl.program_id(0)
    ri = lax.broadcasted_iota(jnp.int32, (2 * c, 2 * c), 0)
    ci = lax.broadcasted_iota(jnp.int32, (2 * c, 2 * c), 1)
    same = (ri >= c) == (ci >= c)
    tt = jnp.where(ri >= c, ri - c, ri)
    ss = jnp.where(ci >= c, ci - c, ci)
    earlier = (ss - tt) * sgn < 0
    strict = jnp.logical_and(same, earlier)
    incl = jnp.logical_and(same, jnp.logical_or(earlier, ss == tt))
    eye = (ri == ci).astype(F32)
    r64 = lax.broadcasted_iota(jnp.int32, (c, c), 0)
    c64 = lax.broadcasted_iota(jnp.int32, (c, c), 1)
    ltri = jnp.where((c64 - r64) * sgn <= 0, 1.0, 0.0).astype(BF16)
    head0 = lax.broadcasted_iota(jnp.int32, (c, LANES), 1) < RW_HEAD

    def stack(x):
        return jnp.concatenate([jnp.where(head0, x, 0.0), jnp.where(head0, 0.0, x)], axis=0)

    def dup(x):
        return jnp.concatenate([x, x], axis=0)

    def mm(a, b):
        return _dot(a.astype(BF16), b.astype(BF16))

    def mm_nt(a, b):
        return _dot_nt(a.astype(BF16), b.astype(BF16))

    def mm_tn(a, b):
        return _dot(a.T.astype(BF16), b.astype(BF16))

    for p in range(npairs):
        sl = slice(LANES * p, LANES * (p + 1))
        lw = lw_ref[:, sl]
        cum = _dot_exact_lhs(ltri, lw)
        tot = jnp.sum(lw, axis=0, keepdims=True)
        p_t = jnp.exp(cum)
        p_prev = jnp.exp(cum - lw)
        p_inv = jnp.exp(-cum)
        p_end = jnp.exp(tot - cum)
        kk = kk_ref[:, sl]
        b = kk * as_ref[:, sl]
        kd = kd_ref[:, sl]
        a2 = stack(-kk * p_prev)
        r2 = stack(r_ref[:, sl] * p_t)
        b2 = dup(b * p_inv)
        k2 = dup(kd * p_inv)
        bp2 = stack(b * p_end)
        kp2 = stack(kd * p_end)
        vst = stack(v_ref[:, sl])
        nab = jnp.where(strict, mm_nt(a2, b2), 0.0)
        nak = jnp.where(strict, mm_nt(a2, k2), 0.0)
        nrb = jnp.where(incl, mm_nt(r2, b2), 0.0)
        nrk = jnp.where(incl, mm_nt(r2, k2), 0.0)
        tinv = eye + nab
        npow = nab
        for _ in range(int(math.log2(c)) - 1):
            npow = mm(npow, npow)
            tinv = tinv + mm(tinv, npow)
        w2 = mm(tinv, a2)
        u0 = mm(tinv, mm(nak, vst))
        rq = r2 + mm(nrb, w2)
        o0 = mm(nrb, u0) + mm(nrk, vst)
        m_ref[:, sl] = mm_tn(w2, bp2)
        ga_ref[:, sl] = mm_tn(u0, bp2) + mm_tn(vst, kp2)
        rq_ref[:, sl] = rq
        o0_ref[:, sl] = o0[0:c, :] + o0[c:2 * c, :]
        pc_ref[:, sl] = jnp.broadcast_to(jnp.exp(tot), (SUBLANES, LANES))


def _wkv_scan_kernel(m_ref, ga_ref, rq_ref, o0_ref, pc_ref, o_ref, g_scr, *, npairs):
    c = WKV_CHUNK

    @pl.when(pl.program_id(3) == 0)
    def _():
        g_scr[...] = jnp.zeros_like(g_scr)

    for p in range(npairs):
        sl = slice(LANES * p, LANES * (p + 1))
        g = g_scr[p]
        o_st = _dot3_nt(rq_ref[:, sl], g)
        o_ref[:, sl] = o_st[0:c, :] + o_st[c:2 * c, :] + o0_ref[:, sl]
        g_scr[p] = g * pc_ref[0:1, sl] + _dot3(g, m_ref[:, sl]) + ga_ref[:, sl]


def _wkv_bidir(r, v, kk, lw, kd, asg, batch, ctx_len):
    n, d = r.shape
    c = WKV_CHUNK
    s = n // batch
    ncs = s // c
    ncc = ctx_len // c
    nch = n // c
    lanes = 512
    npairs = lanes // LANES
    ngrp = d // lanes
    shared = pl.BlockSpec((c, lanes), lambda z, i, j: (i, j))
    perdir = pl.BlockSpec((None, c, lanes), lambda z, i, j: (z, i, j))
    big = pl.BlockSpec((None, 2 * c, lanes), lambda z, i, j: (z, i, j))
    m_, ga_, rq_, o0_, pc_ = pl.pallas_call(
        functools.partial(_wkv_chunk_kernel, npairs=npairs),
        grid=(2, nch, ngrp),
        in_specs=[shared, shared, shared, perdir, perdir, perdir],
        out_specs=[big, big, big, perdir,
                   pl.BlockSpec((None, SUBLANES, lanes), lambda z, i, j: (z, i, j))],
        out_shape=[jax.ShapeDtypeStruct((2, nch * 2 * c, d), F32)] * 3
        + [jax.ShapeDtypeStruct((2, n, d), F32),
           jax.ShapeDtypeStruct((2, nch * SUBLANES, d), F32)],
        compiler_params=_cparams("parallel", "parallel", "parallel"),
        name="wkv_chunk",
    )(r, v, kk, lw, kd, asg)

    def cmap(z, b, j, cc):
        back = jnp.where(cc < ncc, ncc - 1 - cc, ncs + ncc - 1 - cc)
        return (z, b * ncs + jnp.where(z == 0, cc, back), j)

    out = pl.pallas_call(
        functools.partial(_wkv_scan_kernel, npairs=npairs),
        grid=(2, batch, ngrp, ncs),
        in_specs=[
            pl.BlockSpec((None, 2 * c, lanes), cmap),
            pl.BlockSpec((None, 2 * c, lanes), cmap),
            pl.BlockSpec((None, 2 * c, lanes), cmap),
            pl.BlockSpec((None, c, lanes), cmap),
            pl.BlockSpec((None, SUBLANES, lanes), cmap),
        ],
        out_specs=pl.BlockSpec((None, c, lanes), cmap),
        out_shape=jax.ShapeDtypeStruct((2, n, d), F32),
        scratch_shapes=[pltpu.VMEM((npairs, 2 * c, 2 * c), F32)],
        compiler_params=_cparams("parallel", "parallel", "parallel", "arbitrary"),
        name="wkv_scan",
    )(m_, ga_, rq_, o0_, pc_)
    return out


def _rwread_kernel(o_ref, r_ref, kd_ref, v_ref, g_ref, pv_ref, bd_ref, y_ref):
    bd = bd_ref[...]
    wkv = o_ref[0] + o_ref[1]
    inv = 1.0 / RW_HEAD
    mu = _head_sum(wkv, bd) * inv
    dev = wkv - mu
    var = _head_sum(dev * dev, bd) * inv
    y = dev * lax.rsqrt(var + GN_EPS) * pv_ref[1:2, :] + pv_ref[2:3, :]
    rk = r_ref[...] * (kd_ref[0] + kd_ref[1]) * pv_ref[0:1, :]
    y = y + _head_sum(rk, bd) * v_ref[...]
    y_ref[...] = (y * g_ref[...]).astype(BF16)


def _rwkv_readout(geo, wkv, r, kd, v, g, pvec):
    n, d = r.shape
    tm, tc = geo.tm, 512
    ii = lax.broadcasted_iota(jnp.int32, (tc, tc), 0) // RW_HEAD
    jj = lax.broadcasted_iota(jnp.int32, (tc, tc), 1) // RW_HEAD
    bd = (ii == jj).astype(BF16)
    row = pl.BlockSpec((tm, tc), lambda i, j: (i, j))
    dirs = pl.BlockSpec((2, tm, tc), lambda i, j: (0, i, j))
    return pl.pallas_call(
        _rwread_kernel,
        grid=(geo.ntiles, d // tc),
        in_specs=[dirs, row, dirs, row, row,
                  pl.BlockSpec((SUBLANES, tc), lambda i, j: (0, j)),
                  pl.BlockSpec((tc, tc), lambda i, j: (0, 0))],
        out_specs=row,
        out_shape=jax.ShapeDtypeStruct((n, d), BF16),
        compiler_params=_cparams("parallel", "parallel"),
        name="rwkv_readout",
    )(wkv, r, kd, v, g, pvec, bd)


def _mlaqkv_kernel(z_ref, qn_ref, kvn_ref, wuq_ref, wukv_ref, gq_ref, gk_ref, cos_ref, sin_ref,
                   q_ref, k_ref, v_ref, *, q_lora, kv_lora):
    z = z_ref[...]
    cq = z[:, 0:q_lora]
    ckv = z[:, q_lora:q_lora + kv_lora]
    krot = z[:, q_lora + kv_lora:q_lora + kv_lora + LANES]

    def rms(x, g):
        ms = jnp.mean(x * x, axis=-1, keepdims=True)
        return x * lax.rsqrt(ms + EPS) * g

    qf = _dot(rms(cq, qn_ref[...]).astype(BF16), wuq_ref[...])
    kvf = _dot(rms(ckv, kvn_ref[...]).astype(BF16), wukv_ref[...])
    cos = cos_ref[...]
    sin = sin_ref[...]
    half = QK_ROPE // 2
    lane = lax.broadcasted_iota(jnp.int32, cos.shape, 1)

    def rope(x):
        up = pltpu.roll(x, LANES - half, axis=1)
        dn = pltpu.roll(x, half, axis=1)
        return x * cos + jnp.where(lane < half, up, dn) * sin

    inv_w = 1.0 / (QK_NOPE + QK_ROPE)
    gq = gq_ref[...]
    gk = gk_ref[...]
    kr_ss = jnp.sum(krot * krot, axis=-1, keepdims=True)
    for h in range(MLA_HEADS):
        o = QK_PAD * h
        qh = qf[:, o:o + QK_PAD]
        rs = lax.rsqrt(jnp.sum(qh * qh, axis=-1, keepdims=True) * inv_w + EPS)
        qn = qh * rs * gq * ATTN_SCALE
        q_ref[:, o:o + QK_NOPE] = qn[:, 0:QK_NOPE].astype(BF16)
        q_ref[:, o + QK_NOPE:o + QK_PAD] = rope(qn[:, QK_NOPE:QK_PAD]).astype(BF16)
        kn = kvf[:, o:o + QK_NOPE]
        rsk = lax.rsqrt((jnp.sum(kn * kn, axis=-1, keepdims=True) + kr_ss) * inv_w + EPS)
        k_ref[:, o:o + QK_NOPE] = (kn * rsk * gk[:, 0:QK_NOPE]).astype(BF16)
        k_ref[:, o + QK_NOPE:o + QK_PAD] = rope(krot * rsk * gk[:, QK_NOPE:QK_PAD]).astype(BF16)
        v_ref[:, V_HEAD * h:V_HEAD * (h + 1)] = kvf[:, o + QK_NOPE:o + QK_PAD].astype(BF16)


def _mla_qkv(geo, z, qn, kvn, wuq_p, wukv, gq_p, gk_p, cos_t, sin_t):
    n, zw = z.shape
    tm = geo.tm
    q_lora, kv_lora = qn.shape[0], kvn.shape[0]
    hq = MLA_HEADS * QK_PAD
    tpb = geo.tpb
    full = lambda i: (0, 0)
    rowmap = lambda i: (i, 0)
    return pl.pallas_call(
        functools.partial(_mlaqkv_kernel, q_lora=q_lora, kv_lora=kv_lora),
        grid=(geo.ntiles,),
        in_specs=[
            pl.BlockSpec((tm, zw), rowmap),
            pl.BlockSpec((1, q_lora), full),
            pl.BlockSpec((1, kv_lora), full),
            pl.BlockSpec((q_lora, hq), full),
            pl.BlockSpec((kv_lora, hq), full),
            pl.BlockSpec((1, QK_PAD), full),
            pl.BlockSpec((1, QK_PAD), full),
            pl.BlockSpec((tm, LANES), lambda i: (i % tpb, 0)),
            pl.BlockSpec((tm, LANES), lambda i: (i % tpb, 0)),
        ],
        out_specs=[pl.BlockSpec((tm, hq), rowmap), pl.BlockSpec((tm, hq), rowmap),
                   pl.BlockSpec((tm, MLA_HEADS * V_HEAD), rowmap)],
        out_shape=[jax.ShapeDtypeStruct((n, hq), BF16), jax.ShapeDtypeStruct((n, hq), BF16),
                   jax.ShapeDtypeStruct((n, MLA_HEADS * V_HEAD), BF16)],
        compiler_params=_cparams("parallel"),
        name="mla_qkv",
    )(z, qn.reshape(1, -1), kvn.reshape(1, -1), wuq_p, wukv, gq_p, gk_p, cos_t, sin_t)


def _attn_kernel(q_ref, k_ref, v_ref, o_ref, *, ctx_len, nct):
    i = pl.program_id(2)
    s = _dot_nt(q_ref[...], k_ref[...])
    col = lax.broadcasted_iota(jnp.int32, s.shape, 1)
    s = jnp.where(jnp.logical_and(i < nct, col >= ctx_len), -1e30, s)
    m = jnp.max(s, axis=-1, keepdims=True)
    p = jnp.exp(s - m)
    l = jnp.sum(p, axis=-1, keepdims=True)
    o_ref[...] = (_dot(p.astype(BF16), v_ref[...]) / l).astype(BF16)


def _mla_attention(geo, q, k, v):
    n = q.shape[0]
    b, s, tq, tpb = geo.b, geo.s, geo.tm, geo.tpb
    k3 = k.reshape(b, s, MLA_HEADS * QK_PAD)
    v3 = v.reshape(b, s, MLA_HEADS * V_HEAD)
    return pl.pallas_call(
        functools.partial(_attn_kernel, ctx_len=geo.l, nct=geo.nct),
        grid=(b, MLA_HEADS, tpb),
        in_specs=[
            pl.BlockSpec((tq, QK_PAD), lambda bb, h, i: (bb * tpb + i, h)),
            pl.BlockSpec((None, s, QK_PAD), lambda bb, h, i: (bb, 0, h)),
            pl.BlockSpec((None, s, V_HEAD), lambda bb, h, i: (bb, 0, h)),
        ],
        out_specs=pl.BlockSpec((tq, V_HEAD), lambda bb, h, i: (bb * tpb + i, h)),
        out_shape=jax.ShapeDtypeStruct((n, MLA_HEADS * V_HEAD), BF16),
        compiler_params=_cparams("parallel", "parallel", "parallel"),
        name="mla_attention",
    )(q, k3, v3)


def _topk_rows(s, k, payload=None):
    rows = lax.broadcasted_iota(jnp.int32, s.shape, 0).astype(F32)
    big = float(s.shape[0])
    vals, idxs = [], []
    for _ in range(k):
        m = jnp.max(s, axis=0, keepdims=True)
        idx = jnp.min(jnp.where(s == m, rows, big), axis=0, keepdims=True)
        hit = rows == idx
        vals.append(m)
        if payload is None:
            idxs.append(idx)
        else:
            idxs.append(jnp.sum(jnp.where(hit, payload, 0.0), axis=0, keepdims=True))
        s = jnp.where(hit, -jnp.inf, s)
    return jnp.concatenate(vals, axis=0), jnp.concatenate(idxs, axis=0)


def _peersel_kernel(q_ref, qn_ref, keys_ref, e_ref, g_ref):
    q = q_ref[...]
    ms = jnp.mean(q * q, axis=-1, keepdims=True)
    qn = q * lax.rsqrt(ms + EPS) * qn_ref[...]
    half = D_KEY // 2
    s1 = _dot3_nt(keys_ref[0], qn[:, 0:half])
    s2 = _dot3_nt(keys_ref[1], qn[:, half:D_KEY])
    t1, i1 = _topk_rows(s1, PEER_TOPK)
    t2, i2 = _topk_rows(s2, PEER_TOPK)
    cand = jnp.concatenate([t1[p:p + 1, :] + t2 for p in range(PEER_TOPK)], axis=0)
    cidx = jnp.concatenate([i1[p:p + 1, :] * float(N_KEYS) + i2 for p in range(PEER_TOPK)], axis=0)
    best, eidx = _topk_rows(cand, PEER_TOPK, payload=cidx)
    ex = jnp.exp(best - jnp.max(best, axis=0, keepdims=True))
    g_ref[...] = ex / jnp.sum(ex, axis=0, keepdims=True)
    e_ref[...] = eidx.astype(jnp.int32)


def _peer_select(qp, q_norm, keys, tm):
    n = qp.shape[0]
    return pl.pallas_call(
        _peersel_kernel,
        grid=(n // tm, PEER_HEADS),
        in_specs=[
            pl.BlockSpec((tm, D_KEY), lambda i, h: (i, h)),
            pl.BlockSpec((1, D_KEY), lambda i, h: (0, 0)),
            pl.BlockSpec((2, N_KEYS, D_KEY // 2), lambda i, h: (0, 0, 0)),
        ],
        out_specs=[pl.BlockSpec((PEER_TOPK, tm), lambda i, h: (h, i)),
                   pl.BlockSpec((PEER_TOPK, tm), lambda i, h: (h, i))],
        out_shape=[jax.ShapeDtypeStruct((PEER_HEADS * PEER_TOPK, n), jnp.int32),
                   jax.ShapeDtypeStruct((PEER_HEADS * PEER_TOPK, n), F32)],
        compiler_params=_cparams("parallel", "parallel"),
        name="peer_select",
    )(qp, q_norm.reshape(1, D_KEY), keys)


GATHER_SLOTS = 4
SLAB_PAD = SUBLANES


def _peergather_kernel(idx_ref, gate_ref, h_ref, x_ref, mod_ref, tab_ref, o_ref, buf, sem, *, tb):
    nsel = PEER_HEADS * PEER_TOPK
    nrow = h_ref.shape[1] // LANES
    pitch = 2 * nrow + SLAB_PAD
    slot_rows = nsel * pitch
    ns = GATHER_SLOTS
    lane_t = lax.broadcasted_iota(jnp.int32, (nsel, tb), 1)
    g2 = mod_ref[5:6, :]

    def row_copy(t, j, slot):
        return pltpu.make_async_copy(
            tab_ref.at[idx_ref[j, t]],
            buf.at[pl.ds(slot * slot_rows + j * pitch, 2 * nrow), :],
            sem.at[slot])

    def issue(t, slot):
        for j in range(nsel):
            row_copy(t, j, slot).start()

    def wait(t, slot):
        for j in range(nsel):
            row_copy(t, j, slot).wait()

    def compute(t, slot):
        hrow = h_ref[pl.ds(t, 1), :]
        acc = jnp.zeros((nsel, LANES), F32)
        for s in range(nrow):
            u = buf[pl.ds(slot * slot_rows + s, nsel, stride=pitch), :]
            acc = acc + u * hrow[:, LANES * s:LANES * (s + 1)]
        dots = jnp.sum(acc, axis=-1, keepdims=True)
        gcol = jnp.sum(jnp.where(lane_t == t, gate_ref[...], 0.0), axis=-1, keepdims=True)
        coef = gcol * _gelu(dots)
        outs = []
        for s in range(nrow):
            vv = buf[pl.ds(slot * slot_rows + nrow + s, nsel, stride=pitch), :]
            outs.append(jnp.sum(coef * vv, axis=0, keepdims=True))
        orow = jnp.concatenate(outs, axis=1)
        o_ref[pl.ds(t, 1), :] = x_ref[pl.ds(t, 1), :] + g2 * orow

    for s in range(ns - 1):
        issue(s, s)

    def body(g, carry):
        for s in range(ns):
            t = g * ns + s
            nxt = t + ns - 1

            @pl.when(nxt < tb)
            def _():
                issue(nxt, (s + ns - 1) % ns)

            wait(t, s)
            compute(t, s)
        return carry

    lax.fori_loop(0, tb // ns, body, 0)


def _peer_gather(geo_g, eidx_t, gate_t, h2, x, mods, table):
    n, d = x.shape
    tb = geo_g.tm
    nsel = PEER_HEADS * PEER_TOPK
    pitch = 2 * d // LANES + SLAB_PAD
    return pl.pallas_call(
        functools.partial(_peergather_kernel, tb=tb),
        grid=(n // tb,),
        in_specs=[
            pl.BlockSpec((nsel, tb), lambda i: (0, i), memory_space=pltpu.SMEM),
            pl.BlockSpec((nsel, tb), lambda i: (0, i)),
            pl.BlockSpec((tb, d), lambda i: (i, 0)),
            pl.BlockSpec((tb, d), lambda i: (i, 0)),
            geo_g.mod_spec(),
            pl.BlockSpec(memory_space=pl.ANY),
        ],
        out_specs=pl.BlockSpec((tb, d), lambda i: (i, 0)),
        out_shape=jax.ShapeDtypeStruct((n, d), F32),
        scratch_shapes=[
            pltpu.VMEM((GATHER_SLOTS * nsel * pitch, LANES), F32),
            pltpu.SemaphoreType.DMA((GATHER_SLOTS,)),
        ],
        compiler_params=_cparams("arbitrary"),
        name="peer_gather",
    )(eidx_t, gate_t, h2, x, mods, table)


def _pad_to(x, axis, size):
    pad = [(0, 0)] * x.ndim
    pad[axis] = (0, size - x.shape[axis])
    return jnp.pad(x, pad)


def _rwkv_layer(geo, xs, mods, norm1, mix, w_rkv, w_o, w0, w1, w2, a0, a1, a2, vl, g1, g2,
                k_k, k_a, r_k, ln_w, ln_b, vfirst):
    d = geo.d
    xm = _rwkv_mix(geo, xs, norm1, mods, mix)
    rkv = _bmm(xm, w_rkv.astype(BF16), (0, 2, 3), ("none",) * 3, F32, geo.tm)
    lt = g1.shape[1]
    w1c = jnp.concatenate([_pad_to(w1[0], 1, LORA_PAD), _pad_to(w1[1], 1, LORA_PAD)], axis=1)
    a1c = jnp.concatenate([_pad_to(a1[0], 1, LORA_PAD), _pad_to(a1[1], 1, LORA_PAD)], axis=1)
    if vl is None:
        v1p = jnp.zeros((d, lt), F32)
        v2p = jnp.zeros((LORA_PAD, d), F32)
        v0 = jnp.zeros((d,), F32)
    else:
        v0, v1, v2 = vl
        v1p = _pad_to(v1, 1, lt)
        v2p = _pad_to(v2, 0, LORA_PAD)
    wl1 = jnp.stack([_pad_to(w1c, 1, lt), _pad_to(a1c, 1, lt), g1, v1p]).astype(BF16)
    tl = _bmm(xm, wl1, (1, 4, 5, 3), ("tanh", "none", "sigmoid", "none"), BF16, geo.tm)
    w2p = jnp.stack([_pad_to(w2[0], 0, LORA_PAD), _pad_to(w2[1], 0, LORA_PAD)]).astype(BF16)
    a2p = jnp.stack([_pad_to(a2[0], 0, LORA_PAD), _pad_to(a2[1], 0, LORA_PAD)]).astype(BF16)
    pvec = jnp.stack([w0[0], w0[1], a0[0], a0[1], v0, k_k, k_a, jnp.zeros_like(k_k)])
    lw, kd, asg, kk, g, v = _rwkv_feat(geo, rkv, tl, w2p, a2p, g2.astype(BF16), v2p.astype(BF16),
                                      pvec, vfirst)
    r = rkv[0]
    wkv = _wkv_bidir(r, v, kk, lw, kd, asg, geo.b, geo.l)
    pv2 = _pad_to(jnp.stack([r_k, ln_w, ln_b]), 0, SUBLANES)
    y = _rwkv_readout(geo, wkv, r, kd, v, g, pv2)
    xs = _matmul_res(geo, y, w_o.astype(BF16), xs, mods, 2)
    return xs, v


def _rope_tables(geo):
    t = geo.t
    pos = jnp.arange(t)
    row = (pos // GRID_W).astype(F32)
    col = (pos % GRID_W).astype(F32)
    n_freq = QK_ROPE // 4
    inv_freq = ROPE_THETA ** (-jnp.arange(n_freq, dtype=F32) / n_freq)
    ang = jnp.concatenate([row[:, None] * inv_freq, col[:, None] * inv_freq], axis=-1)
    cos, sin = jnp.cos(ang), jnp.sin(ang)
    pad = LANES - QK_ROPE
    cos_l = jnp.concatenate([cos, cos, jnp.ones((t, pad), F32)], axis=1)
    sin_l = jnp.concatenate([-sin, sin, jnp.zeros((t, pad), F32)], axis=1)
    cos_c = jnp.ones((geo.l, LANES), F32)
    sin_c = jnp.zeros((geo.l, LANES), F32)
    return jnp.concatenate([cos_c, cos_l], axis=0), jnp.concatenate([sin_c, sin_l], axis=0)


def _mla_layer(geo, xs, mods, norm1, rope_t, w_in, q_norm, kv_norm, w_uq, w_ukv, g_q, g_k, w_o):
    q_lora, kv_lora = q_norm.shape[0], kv_norm.shape[0]
    zw = q_lora + kv_lora + LANES
    z = _mod_matmul(geo, xs, norm1, mods, _pad_to(w_in, 1, zw).astype(BF16), 0, False)
    qk = QK_NOPE + QK_ROPE
    wuq_p = _pad_to(w_uq.reshape(q_lora, MLA_HEADS, qk), 2, QK_PAD).reshape(q_lora, -1)
    gq_p = _pad_to(g_q, 0, QK_PAD).reshape(1, QK_PAD)
    gk_p = _pad_to(g_k, 0, QK_PAD).reshape(1, QK_PAD)
    q, k, v = _mla_qkv(geo, z, q_norm, kv_norm, wuq_p.astype(BF16), w_ukv.astype(BF16),
                       gq_p, gk_p, *rope_t)
    o = _mla_attention(geo, q, k, v)
    return _matmul_res(geo, o, w_o.astype(BF16), xs, mods, 2)


def _peer_layer(geo, geo_g, xs, mods, norm2, w_q, q_norm, keys, u, v):
    n, d = xs.shape
    qp, h2 = _mod_matmul(geo, xs, norm2, mods, w_q.astype(BF16), 1, True)
    eidx_t, gate_t = _peer_select(qp, q_norm, keys, geo.tm)
    ne = u.shape[0]
    nrow = d // LANES
    table = jnp.concatenate([u.reshape(ne, nrow, LANES), v.reshape(ne, nrow, LANES)], axis=1)
    return _peer_gather(geo_g, eidx_t, gate_t, h2, xs, mods, table)


def kernel(x, c, ctx, c_ctx, w_ada, b_ada, norm1, norm2, rw_mix, rw_wrkv, rw_wo, rw_w0, rw_w1, rw_w2, rw_a0, rw_a1, rw_a2, rw_v0, rw_v1, rw_v2, rw_g1, rw_g2, rw_kk, rw_ka, rw_rk, rw_lnw, rw_lnb, mla_win, mla_qnorm, mla_kvnorm, mla_wuq, mla_wukv, mla_gq, mla_gk, mla_wo, peer_wq, peer_qnorm, peer_keys, peer_u, peer_v):
    b, t, d = x.shape
    l = ctx.shape[1]
    depth = w_ada.shape[0]
    geo = _Geom(b, l, t, d, min(256, l))
    geo_g = _Geom(b, l, t, d, min(128, l))
    cond8 = _pad_to(jnp.concatenate([c, c_ctx[None, :]], axis=0), 0, SUBLANES)
    ada = _adaln(cond8, w_ada, b_ada).reshape(depth, SUBLANES, 6, d)
    mods_all = jnp.stack([jnp.broadcast_to(ada[:, b:b + 1], (depth, b, 6, d)), ada[:, 0:b]], axis=2)
    xs = jnp.concatenate([ctx, x], axis=1).reshape(b * (l + t), d)
    rope_t = _rope_tables(geo)
    vfirst = None
    for i in range(depth):
        j = i // 2
        mods = mods_all[i]
        if i % 2 == 0:
            vl = None if j == 0 else (rw_v0[j - 1], rw_v1[j - 1], rw_v2[j - 1])
            xs, vcur = _rwkv_layer(geo, xs, mods, norm1[i], rw_mix[j], rw_wrkv[j], rw_wo[j],
                                   rw_w0[j], rw_w1[j], rw_w2[j], rw_a0[j], rw_a1[j], rw_a2[j], vl,
                                   rw_g1[j], rw_g2[j], rw_kk[j], rw_ka[j], rw_rk[j], rw_lnw[j],
                                   rw_lnb[j], vfirst)
            if j == 0:
                vfirst = vcur
        else:
            xs = _mla_layer(geo, xs, mods, norm1[i], rope_t, mla_win[j], mla_qnorm[j],
                            mla_kvnorm[j], mla_wuq[j], mla_wukv[j], mla_gq[j], mla_gk[j], mla_wo[j])
        xs = _peer_layer(geo, geo_g, xs, mods, norm2[i], peer_wq[i], peer_qnorm[i], peer_keys[i],
                         peer_u[i], peer_v[i])
    return xs.reshape(b, l + t, d)[:, l:, :]
```

```python
import functools
import math

import jax
import jax.numpy as jnp
from jax import lax
from jax.experimental import pallas as pl
from jax.experimental.pallas import tpu as pltpu

F32 = jnp.float32
BF16 = jnp.bfloat16

EPS = 1e-6
GN_EPS = 64e-5
RW_HEAD = 64
WKV_CHUNK = 64
MLA_HEADS = 16
QK_NOPE = 128
QK_ROPE = 64
V_HEAD = 128
QK_PAD = 256
ROPE_THETA = 10000.0
GRID_W = 64
ATTN_SCALE = (QK_NOPE + QK_ROPE) ** -0.5
PEER_HEADS = 8
N_KEYS = 128
PEER_TOPK = 16
D_KEY = 256
LORA_PAD = 128

LANES = 128
SUBLANES = 8
VMEM_LIMIT = 56 * 1024 * 1024


def _cparams(*sem):
    return pltpu.CompilerParams(dimension_semantics=sem, vmem_limit_bytes=VMEM_LIMIT)


def _dot(a, b):
    return jnp.dot(a, b, preferred_element_type=F32)


def _dot_nt(a, b):
    return lax.dot_general(a, b, (((1,), (1,)), ((), ())), preferred_element_type=F32)


def _split2(x):
    hi = x.astype(BF16)
    lo = (x - hi.astype(F32)).astype(BF16)
    return hi, lo


def _split3(x):
    hi = x.astype(BF16)
    r1 = x - hi.astype(F32)
    mid = r1.astype(BF16)
    lo = (r1 - mid.astype(F32)).astype(BF16)
    return hi, mid, lo


def _dot3(a, b):
    ah, al = _split2(a)
    bh, bl = _split2(b)
    return _dot(ah, bh) + (_dot(ah, bl) + _dot(al, bh))


def _dot3_nt(a, b):
    ah, al = _split2(a)
    bh, bl = _split2(b)
    return _dot_nt(ah, bh) + (_dot_nt(ah, bl) + _dot_nt(al, bh))


def _dot_exact_lhs(sel, x):
    hi, mid, lo = _split3(x)
    return _dot(sel, hi) + (_dot(sel, mid) + _dot(sel, lo))


def _modulate(x, g, shift, scale):
    ms = jnp.mean(x * x, axis=-1, keepdims=True)
    return (x * lax.rsqrt(ms + EPS) * g) * (1.0 + scale) + shift


def _sigmoid(x):
    return 1.0 / (1.0 + jnp.exp(-x))


def _softplus(y):
    return jnp.maximum(y, 0.0) + jnp.log(1.0 + jnp.exp(-jnp.abs(y)))


def _erf(x):
    return lax.erf(x)


def _gelu(x):
    return 0.5 * x * (1.0 + _erf(x * (2.0 ** -0.5)))


def _ada_kernel(s_ref, w_ref, b_ref, o_ref):
    s = s_ref[...]
    s = s * _sigmoid(s)
    o_ref[...] = _dot3(s, w_ref[...]) + b_ref[...]


def _adaln(cond8, w_ada, b_ada):
    depth, d, n = w_ada.shape
    tn = 1024
    return pl.pallas_call(
        _ada_kernel,
        grid=(depth, n // tn),
        in_specs=[
            pl.BlockSpec((SUBLANES, d), lambda l, j: (0, 0)),
            pl.BlockSpec((None, d, tn), lambda l, j: (l, 0, j)),
            pl.BlockSpec((None, 1, tn), lambda l, j: (l, 0, j)),
        ],
        out_specs=pl.BlockSpec((None, SUBLANES, tn), lambda l, j: (l, 0, j)),
        out_shape=jax.ShapeDtypeStruct((depth, SUBLANES, n), F32),
        compiler_params=_cparams("parallel", "parallel"),
        name="adaln",
    )(cond8, w_ada, b_ada.reshape(depth, 1, n))


class _Geom:
    def __init__(self, batch, ctx_len, seq_len, d_model, tm):
        self.b, self.l, self.t, self.d = batch, ctx_len, seq_len, d_model
        self.s = ctx_len + seq_len
        self.n = batch * self.s
        self.tm = tm
        assert ctx_len % tm == 0 and seq_len % tm == 0
        self.tpb = self.s // tm
        self.nct = ctx_len // tm
        self.ntiles = self.n // tm

    def mod_spec(self, nlead=0):
        tpb, nct = self.tpb, self.nct

        def imap(*ids):
            i = ids[nlead]
            return (i // tpb, ((i % tpb) >= nct).astype(jnp.int32), 0, 0)

        return pl.BlockSpec((None, None, 6, self.d), imap)


def _modmm_kernel(x_ref, g_ref, mod_ref, w_ref, o_ref, *h_ref, which):
    h = _modulate(x_ref[...], g_ref[...], mod_ref[3 * which:3 * which + 1, :],
                  mod_ref[3 * which + 1:3 * which + 2, :])
    if h_ref:
        h_ref[0][...] = h
    o_ref[...] = _dot(h.astype(BF16), w_ref[...])


def _mod_matmul(geo, x, g, mods, w, which, emit_h):
    n, d = x.shape
    nn = w.shape[1]
    tm = geo.tm
    out_shape = [jax.ShapeDtypeStruct((n, nn), F32)]
    out_specs = [pl.BlockSpec((tm, nn), lambda i: (i, 0))]
    if emit_h:
        out_shape.append(jax.ShapeDtypeStruct((n, d), F32))
        out_specs.append(pl.BlockSpec((tm, d), lambda i: (i, 0)))
    res = pl.pallas_call(
        functools.partial(_modmm_kernel, which=which),
        grid=(geo.ntiles,),
        in_specs=[
            pl.BlockSpec((tm, d), lambda i: (i, 0)),
            pl.BlockSpec((1, d), lambda i: (0, 0)),
            geo.mod_spec(),
            pl.BlockSpec((d, nn), lambda i: (0, 0)),
        ],
        out_specs=out_specs,
        out_shape=out_shape,
        compiler_params=_cparams("parallel"),
        name="mod_matmul",
    )(x, g.reshape(1, d), mods, w)
    return res if emit_h else res[0]


def _mmres_kernel(y_ref, w_ref, x_ref, mod_ref, o_ref, *, gidx):
    acc = _dot(y_ref[...], w_ref[...])
    o_ref[...] = x_ref[...] + mod_ref[gidx:gidx + 1, :] * acc


def _matmul_res(geo, y, w, x, mods, gidx):
    n, k = y.shape
    d = x.shape[1]
    tm = geo.tm
    return pl.pallas_call(
        functools.partial(_mmres_kernel, gidx=gidx),
        grid=(geo.ntiles,),
        in_specs=[
            pl.BlockSpec((tm, k), lambda i: (i, 0)),
            pl.BlockSpec((k, d), lambda i: (0, 0)),
            pl.BlockSpec((tm, d), lambda i: (i, 0)),
            geo.mod_spec(),
        ],
        out_specs=pl.BlockSpec((tm, d), lambda i: (i, 0)),
        out_shape=jax.ShapeDtypeStruct((n, d), F32),
        compiler_params=_cparams("parallel"),
        name="matmul_res",
    )(y, w, x, mods)


def _bmm_kernel(x_ref, w_ref, o_ref, *, acts):
    j = pl.program_id(0)
    y = _dot(x_ref[...], w_ref[...])
    out = y
    for jj, a in enumerate(acts):
        if a == "tanh":
            out = jnp.where(j == jj, jnp.tanh(y), out)
        elif a == "sigmoid":
            out = jnp.where(j == jj, _sigmoid(y), out)
    o_ref[...] = out.astype(o_ref.dtype)


def _bmm(x3, w3, src, acts, out_dtype, tm):
    _, n, k = x3.shape
    nj, _, nn = w3.shape
    src = tuple(src)

    def xmap(j, i):
        idx = jnp.int32(src[0])
        for jj in range(1, nj):
            idx = jnp.where(j == jj, jnp.int32(src[jj]), idx)
        return (idx, i, 0)

    return pl.pallas_call(
        functools.partial(_bmm_kernel, acts=tuple(acts)),
        grid=(nj, n // tm),
        in_specs=[
            pl.BlockSpec((None, tm, k), xmap),
            pl.BlockSpec((None, k, nn), lambda j, i: (j, 0, 0)),
        ],
        out_specs=pl.BlockSpec((None, tm, nn), lambda j, i: (j, i, 0)),
        out_shape=jax.ShapeDtypeStruct((nj, n, nn), out_dtype),
        compiler_params=_cparams("parallel", "parallel"),
        name="bmm",
    )(x3, w3)


def _rwmix_kernel(x_ref, xp_ref, xn_ref, g_ref, mod_ref, mix_ref, o_ref, *, tpb, nct):
    i = pl.program_id(0)
    tm = x_ref.shape[0]
    g = g_ref[...]
    shift = mod_ref[0:1, :]
    scale = mod_ref[1:2, :]
    h = _modulate(x_ref[...], g, shift, scale)
    hp = _modulate(xp_ref[...], g, shift, scale)[SUBLANES - 1:SUBLANES, :]
    hn = _modulate(xn_ref[...], g, shift, scale)[0:1, :]
    it = i % tpb
    first = jnp.logical_or(it == 0, it == nct)
    last = jnp.logical_or(it == nct - 1, it == tpb - 1)
    hp = jnp.where(first, 0.0, hp)
    hn = jnp.where(last, 0.0, hn)
    rows = lax.broadcasted_iota(jnp.int32, h.shape, 0)
    prev = jnp.where(rows == 0, hp, pltpu.roll(h, 1, axis=0))
    nxt = jnp.where(rows == tm - 1, hn, pltpu.roll(h, tm - 1, axis=0))
    xx = 0.5 * (prev + nxt) - h
    for m in range(6):
        o_ref[m] = (h + xx * mix_ref[m:m + 1, :]).astype(BF16)


def _rwkv_mix(geo, x, g, mods, mix):
    n, d = x.shape
    tm = geo.tm
    r8 = tm // SUBLANES
    nblk8 = n // SUBLANES
    return pl.pallas_call(
        functools.partial(_rwmix_kernel, tpb=geo.tpb, nct=geo.nct),
        grid=(geo.ntiles,),
        in_specs=[
            pl.BlockSpec((tm, d), lambda i: (i, 0)),
            pl.BlockSpec((SUBLANES, d), lambda i: (jnp.maximum(i * r8 - 1, 0), 0)),
            pl.BlockSpec((SUBLANES, d), lambda i: (jnp.minimum((i + 1) * r8, nblk8 - 1), 0)),
            pl.BlockSpec((1, d), lambda i: (0, 0)),
            geo.mod_spec(),
            pl.BlockSpec((6, d), lambda i: (0, 0)),
        ],
        out_specs=pl.BlockSpec((6, tm, d), lambda i: (0, i, 0)),
        out_shape=jax.ShapeDtypeStruct((6, n, d), BF16),
        compiler_params=_cparams("parallel"),
        name="rwkv_mix",
    )(x, x, x, g.reshape(1, d), mods, mix)


def _head_sum(x, bd):
    hi, lo = _split2(x)
    return _dot(hi, bd) + _dot(lo, bd)


def _rwfeat_kernel(k_ref, v_ref, tl_ref, w2_ref, a2_ref, g2_ref, v2_ref, pv_ref, bd_ref,
                   *rest, has_vlora):
    if has_vlora:
        vf_ref, lw_ref, kd_ref, as_ref, kk_ref, g_ref, vo_ref = rest
    else:
        lw_ref, kd_ref, as_ref, kk_ref, g_ref = rest
    k = k_ref[...]
    tw = tl_ref[0]
    ta = tl_ref[1]
    tg = tl_ref[2]
    w0 = pv_ref[0:2, :]
    a0 = pv_ref[2:4, :]
    k_k = pv_ref[5:6, :]
    k_a = pv_ref[6:7, :]
    for z in range(2):
        sl = slice(LORA_PAD * z, LORA_PAD * (z + 1))
        lora_w = _dot(tw[:, sl], w2_ref[z])
        w = -_softplus(-(w0[z:z + 1, :] + lora_w)) - 0.5
        lw_ref[z] = -jnp.exp(w)
        a_sig = _sigmoid(a0[z:z + 1, :] + _dot(ta[:, sl], a2_ref[z]))
        as_ref[z] = a_sig
        kd_ref[z] = k * (1.0 + (a_sig - 1.0) * k_a)
    g_ref[...] = _dot(tg, g2_ref[...])
    kkr = k * k_k
    ss = _head_sum(kkr * kkr, bd_ref[...])
    kk_ref[...] = kkr * lax.rsqrt(ss + 1e-12)
    if has_vlora:
        v = v_ref[...]
        tv = tl_ref[3]
        gate = _sigmoid(pv_ref[4:5, :] + _dot(tv[:, 0:LORA_PAD], v2_ref[...]))
        vo_ref[...] = v + (vf_ref[...] - v) * gate


def _rwkv_feat(geo, rkv, tl, w2p, a2p, g2, v2p, pvec, vfirst):
    _, n, d = rkv.shape
    tm, tc = geo.tm, 512
    has_vlora = vfirst is not None
    lt = tl.shape[2]
    ii = lax.broadcasted_iota(jnp.int32, (tc, tc), 0) // RW_HEAD
    jj = lax.broadcasted_iota(jnp.int32, (tc, tc), 1) // RW_HEAD
    bd = (ii == jj).astype(BF16)
    row = lambda i, j: (i, j)
    in_specs = [
        pl.BlockSpec((None, tm, tc), lambda i, j: (1, i, j)),
        pl.BlockSpec((None, tm, tc), lambda i, j: (2, i, j)),
        pl.BlockSpec((4, tm, lt), lambda i, j: (0, i, 0)),
        pl.BlockSpec((2, LORA_PAD, tc), lambda i, j: (0, 0, j)),
        pl.BlockSpec((2, LORA_PAD, tc), lambda i, j: (0, 0, j)),
        pl.BlockSpec((lt, tc), lambda i, j: (0, j)),
        pl.BlockSpec((LORA_PAD, tc), lambda i, j: (0, j)),
        pl.BlockSpec((SUBLANES, tc), lambda i, j: (0, j)),
        pl.BlockSpec((tc, tc), lambda i, j: (0, 0)),
    ]
    args = [rkv, rkv, tl, w2p, a2p, g2, v2p, pvec, bd]
    dir_spec = pl.BlockSpec((2, tm, tc), lambda i, j: (0, i, j))
    out_specs = [dir_spec, dir_spec, dir_spec, pl.BlockSpec((tm, tc), row), pl.BlockSpec((tm, tc), row)]
    out_shape = [jax.ShapeDtypeStruct((2, n, d), F32)] * 3 + [jax.ShapeDtypeStruct((n, d), F32)] * 2
    if has_vlora:
        in_specs.append(pl.BlockSpec((tm, tc), row))
        args.append(vfirst)
        out_specs.append(pl.BlockSpec((tm, tc), row))
        out_shape.append(jax.ShapeDtypeStruct((n, d), F32))
    res = pl.pallas_call(
        functools.partial(_rwfeat_kernel, has_vlora=has_vlora),
        grid=(geo.ntiles, d // tc),
        in_specs=in_specs,
        out_specs=out_specs,
        out_shape=out_shape,
        compiler_params=_cparams("parallel", "parallel"),
        name="rwkv_feat",
    )(*args)
    if has_vlora:
        lw, kd, asg, kk, g, v = res
    else:
        lw, kd, asg, kk, g = res
        v = rkv[2]
    return lw, kd, asg, kk, g, v


def _wkv_chunk_kernel(r_ref, v_ref, kk_ref, lw_ref, kd_ref, as_ref,
                      m_ref, ga_ref, rq_ref, o0_ref, pc_ref, *, npairs):
    c = WKV_CHUNK
    c2 = 2 * c
    sgn = 1 - 2 * pl.program_id(0)
    ri = lax.broadcasted_iota(jnp.int32, (c2, c2), 0)
    ci = lax.broadcasted_iota(jnp.int32, (c2, c2), 1)
    same = (ri >= c) == (ci >= c)
    tt = jnp.where(ri >= c, ri - c, ri)
    ss = jnp.where(ci >= c, ci - c, ci)
    earlier = (ss - tt) * sgn < 0
    strict = jnp.logical_and(same, earlier)
    incl = jnp.logical_and(same, jnp.logical_or(earlier, ss == tt))
    eye = (ri == ci).astype(F32)
    r64 = lax.broadcasted_iota(jnp.int32, (c, c), 0)
    c64 = lax.broadcasted_iota(jnp.int32, (c, c), 1)
    ltri = jnp.where((c64 - r64) * sgn <= 0, 1.0, 0.0).astype(BF16)
    head0 = lax.broadcasted_iota(jnp.int32, (c, LANES), 1) < RW_HEAD
    pairs = range(npairs)

    def stack(x):
        return jnp.concatenate([jnp.where(head0, x, 0.0), jnp.where(head0, 0.0, x)], axis=0)

    def dup(x):
        return jnp.concatenate([x, x], axis=0)

    def bf(x):
        return x.astype(BF16)

    lhs, rhs, a2, bp2, kp2, vst, r2 = [], [], [], [], [], [], []
    for p in pairs:
        sl = slice(LANES * p, LANES * (p + 1))
        lw = lw_ref[:, sl]
        cum = _dot_exact_lhs(ltri, lw)
        tot = jnp.sum(lw, axis=0, keepdims=True)
        p_inv = jnp.exp(-cum)
        p_end = jnp.exp(tot - cum)
        kk = kk_ref[:, sl]
        b = kk * as_ref[:, sl]
        kd = kd_ref[:, sl]
        a2p = stack(-kk * jnp.exp(cum - lw))
        r2p = stack(r_ref[:, sl] * jnp.exp(cum))
        a2.append(bf(a2p))
        r2.append(r2p)
        lhs.append(jnp.concatenate([a2[p], bf(r2p)], axis=0))
        rhs.append(jnp.concatenate([dup(bf(b * p_inv)), dup(bf(kd * p_inv))], axis=0))
        bp2.append(bf(stack(b * p_end)))
        kp2.append(bf(stack(kd * p_end)))
        vst.append(bf(stack(v_ref[:, sl])))
        pc_ref[:, sl] = jnp.broadcast_to(jnp.exp(tot), (SUBLANES, LANES))
    gram = [_dot_nt(lhs[p], rhs[p]) for p in pairs]
    nab = [jnp.where(strict, gram[p][0:c2, 0:c2], 0.0) for p in pairs]
    nrb = [bf(jnp.where(incl, gram[p][c2:2 * c2, 0:c2], 0.0)) for p in pairs]
    nk = [bf(jnp.concatenate([jnp.where(strict, gram[p][0:c2, c2:2 * c2], 0.0),
                              jnp.where(incl, gram[p][c2:2 * c2, c2:2 * c2], 0.0)], axis=0))
          for p in pairs]
    xv = [_dot(nk[p], vst[p]) for p in pairs]
    tinv = [eye + nab[p] for p in pairs]
    npow = [bf(nab[p]) for p in pairs]
    for _ in range(int(math.log2(c)) - 1):
        npow = [bf(_dot(npow[p], npow[p])) for p in pairs]
        tinv = [tinv[p] + _dot(bf(tinv[p]), npow[p]) for p in pairs]
    y = [_dot(bf(tinv[p]), jnp.concatenate([a2[p], bf(xv[p][0:c2, :])], axis=1)) for p in pairs]
    yb = [bf(y[p]) for p in pairs]
    z = [_dot(nrb[p], yb[p]) for p in pairs]
    mg = [_dot(bf(y[p].T), bp2[p]) for p in pairs]
    vk = [_dot(bf(vst[p].astype(F32).T), kp2[p]) for p in pairs]
    for p in pairs:
        sl = slice(LANES * p, LANES * (p + 1))
        m_ref[:, sl] = mg[p][0:c2, :]
        ga_ref[:, sl] = mg[p][c2:2 * c2, :] + vk[p]
        rq_ref[:, sl] = r2[p] + z[p][:, 0:c2]
        o0 = z[p][:, c2:2 * c2] + xv[p][c2:2 * c2, :]
        o0_ref[:, sl] = o0[0:c, :] + o0[c:c2, :]


def _wkv_scan_kernel(m_ref, ga_ref, rq_ref, o0_ref, pc_ref, o_ref, g_scr, *, npairs):
    c = WKV_CHUNK

    @pl.when(pl.program_id(2) == 0)
    def _():
        g_scr[...] = jnp.zeros_like(g_scr)

    for p in range(npairs):
        sl = slice(LANES * p, LANES * (p + 1))
        g = g_scr[p]
        o_st = _dot3_nt(rq_ref[:, sl], g)
        o_ref[:, sl] = o_st[0:c, :] + o_st[c:2 * c, :] + o0_ref[:, sl]
        g_scr[p] = g * pc_ref[0:1, sl] + _dot3(g, m_ref[:, sl]) + ga_ref[:, sl]


WKV_CHUNK_LANES = 1024


def _wkv_bidir(r, v, kk, lw, kd, asg, batch, ctx_len):
    n, d = r.shape
    c = WKV_CHUNK
    s = n // batch
    ncs = s // c
    ncc = ctx_len // c
    nch = n // c
    lanes = min(WKV_CHUNK_LANES, d)
    ngrp = d // lanes
    shared = pl.BlockSpec((c, lanes), lambda z, i, j: (i, j))
    perdir = pl.BlockSpec((None, c, lanes), lambda z, i, j: (z, i, j))
    big = pl.BlockSpec((None, 2 * c, lanes), lambda z, i, j: (z, i, j))
    m_, ga_, rq_, o0_, pc_ = pl.pallas_call(
        functools.partial(_wkv_chunk_kernel, npairs=lanes // LANES),
        grid=(2, nch, ngrp),
        in_specs=[shared, shared, shared, perdir, perdir, perdir],
        out_specs=[big, big, big, perdir,
                   pl.BlockSpec((None, SUBLANES, lanes), lambda z, i, j: (z, i, j))],
        out_shape=[jax.ShapeDtypeStruct((2, nch * 2 * c, d), F32)] * 3
        + [jax.ShapeDtypeStruct((2, n, d), F32),
           jax.ShapeDtypeStruct((2, nch * SUBLANES, d), F32)],
        compiler_params=_cparams("parallel", "parallel", "parallel"),
        name="wkv_chunk",
    )(r, v, kk, lw, kd, asg)

    def cmap(z, b, cc):
        back = jnp.where(cc < ncc, ncc - 1 - cc, ncs + ncc - 1 - cc)
        return (z, b * ncs + jnp.where(z == 0, cc, back), 0)

    out = pl.pallas_call(
        functools.partial(_wkv_scan_kernel, npairs=d // LANES),
        grid=(2, batch, ncs),
        in_specs=[
            pl.BlockSpec((None, 2 * c, d), cmap),
            pl.BlockSpec((None, 2 * c, d), cmap),
            pl.BlockSpec((None, 2 * c, d), cmap),
            pl.BlockSpec((None, c, d), cmap),
            pl.BlockSpec((None, SUBLANES, d), cmap),
        ],
        out_specs=pl.BlockSpec((None, c, d), cmap),
        out_shape=jax.ShapeDtypeStruct((2, n, d), F32),
        scratch_shapes=[pltpu.VMEM((d // LANES, 2 * c, 2 * c), F32)],
        compiler_params=_cparams("parallel", "parallel", "arbitrary"),
        name="wkv_scan",
    )(m_, ga_, rq_, o0_, pc_)
    return out


def _rwread_kernel(o_ref, r_ref, kd_ref, v_ref, g_ref, pv_ref, bd_ref, y_ref):
    bd = bd_ref[...]
    wkv = o_ref[0] + o_ref[1]
    inv = 1.0 / RW_HEAD
    mu = _head_sum(wkv, bd) * inv
    dev = wkv - mu
    var = _head_sum(dev * dev, bd) * inv
    y = dev * lax.rsqrt(var + GN_EPS) * pv_ref[1:2, :] + pv_ref[2:3, :]
    rk = r_ref[...] * (kd_ref[0] + kd_ref[1]) * pv_ref[0:1, :]
    y = y + _head_sum(rk, bd) * v_ref[...]
    y_ref[...] = (y * g_ref[...]).astype(BF16)


def _rwkv_readout(geo, wkv, r, kd, v, g, pvec):
    n, d = r.shape
    tm, tc = geo.tm, 512
    ii = lax.broadcasted_iota(jnp.int32, (tc, tc), 0) // RW_HEAD
    jj = lax.broadcasted_iota(jnp.int32, (tc, tc), 1) // RW_HEAD
    bd = (ii == jj).astype(BF16)
    row = pl.BlockSpec((tm, tc), lambda i, j: (i, j))
    dirs = pl.BlockSpec((2, tm, tc), lambda i, j: (0, i, j))
    return pl.pallas_call(
        _rwread_kernel,
        grid=(geo.ntiles, d // tc),
        in_specs=[dirs, row, dirs, row, row,
                  pl.BlockSpec((SUBLANES, tc), lambda i, j: (0, j)),
                  pl.BlockSpec((tc, tc), lambda i, j: (0, 0))],
        out_specs=row,
        out_shape=jax.ShapeDtypeStruct((n, d), BF16),
        compiler_params=_cparams("parallel", "parallel"),
        name="rwkv_readout",
    )(wkv, r, kd, v, g, pvec, bd)


def _mlaqkv_kernel(z_ref, qn_ref, kvn_ref, wuq_ref, wukv_ref, gq_ref, gk_ref, cos_ref, sin_ref,
                   q_ref, k_ref, v_ref, *, q_lora, kv_lora):
    z = z_ref[...]
    cq = z[:, 0:q_lora]
    ckv = z[:, q_lora:q_lora + kv_lora]
    krot = z[:, q_lora + kv_lora:q_lora + kv_lora + LANES]

    def rms(x, g):
        ms = jnp.mean(x * x, axis=-1, keepdims=True)
        return x * lax.rsqrt(ms + EPS) * g

    qf = _dot(rms(cq, qn_ref[...]).astype(BF16), wuq_ref[...])
    kvf = _dot(rms(ckv, kvn_ref[...]).astype(BF16), wukv_ref[...])
    cos = cos_ref[...]
    sin = sin_ref[...]
    half = QK_ROPE // 2
    lane = lax.broadcasted_iota(jnp.int32, cos.shape, 1)

    def rope(x):
        up = pltpu.roll(x, LANES - half, axis=1)
        dn = pltpu.roll(x, half, axis=1)
        return x * cos + jnp.where(lane < half, up, dn) * sin

    inv_w = 1.0 / (QK_NOPE + QK_ROPE)
    gq = gq_ref[...]
    gk = gk_ref[...]
    kr_ss = jnp.sum(krot * krot, axis=-1, keepdims=True)
    for h in range(MLA_HEADS):
        o = QK_PAD * h
        qh = qf[:, o:o + QK_PAD]
        rs = lax.rsqrt(jnp.sum(qh * qh, axis=-1, keepdims=True) * inv_w + EPS)
        qn = qh * rs * gq * ATTN_SCALE
        q_ref[:, o:o + QK_NOPE] = qn[:, 0:QK_NOPE].astype(BF16)
        q_ref[:, o + QK_NOPE:o + QK_PAD] = rope(qn[:, QK_NOPE:QK_PAD]).astype(BF16)
        kn = kvf[:, o:o + QK_NOPE]
        rsk = lax.rsqrt((jnp.sum(kn * kn, axis=-1, keepdims=True) + kr_ss) * inv_w + EPS)
        k_ref[:, o:o + QK_NOPE] = (kn * rsk * gk[:, 0:QK_NOPE]).astype(BF16)
        k_ref[:, o + QK_NOPE:o + QK_PAD] = rope(krot * rsk * gk[:, QK_NOPE:QK_PAD]).astype(BF16)
        v_ref[:, V_HEAD * h:V_HEAD * (h + 1)] = kvf[:, o + QK_NOPE:o + QK_PAD].astype(BF16)


def _mla_qkv(geo, z, qn, kvn, wuq_p, wukv, gq_p, gk_p, cos_t, sin_t):
    n, zw = z.shape
    tm = geo.tm
    q_lora, kv_lora = qn.shape[0], kvn.shape[0]
    hq = MLA_HEADS * QK_PAD
    tpb = geo.tpb
    full = lambda i: (0, 0)
    rowmap = lambda i: (i, 0)
    return pl.pallas_call(
        functools.partial(_mlaqkv_kernel, q_lora=q_lora, kv_lora=kv_lora),
        grid=(geo.ntiles,),
        in_specs=[
            pl.BlockSpec((tm, zw), rowmap),
            pl.BlockSpec((1, q_lora), full),
            pl.BlockSpec((1, kv_lora), full),
            pl.BlockSpec((q_lora, hq), full),
            pl.BlockSpec((kv_lora, hq), full),
            pl.BlockSpec((1, QK_PAD), full),
            pl.BlockSpec((1, QK_PAD), full),
            pl.BlockSpec((tm, LANES), lambda i: (i % tpb, 0)),
            pl.BlockSpec((tm, LANES), lambda i: (i % tpb, 0)),
        ],
        out_specs=[pl.BlockSpec((tm, hq), rowmap), pl.BlockSpec((tm, hq), rowmap),
                   pl.BlockSpec((tm, MLA_HEADS * V_HEAD), rowmap)],
        out_shape=[jax.ShapeDtypeStruct((n, hq), BF16), jax.ShapeDtypeStruct((n, hq), BF16),
                   jax.ShapeDtypeStruct((n, MLA_HEADS * V_HEAD), BF16)],
        compiler_params=_cparams("parallel"),
        name="mla_qkv",
    )(z, qn.reshape(1, -1), kvn.reshape(1, -1), wuq_p, wukv, gq_p, gk_p, cos_t, sin_t)


def _attn_kernel(q_ref, k_ref, v_ref, o_ref, *, ctx_len, nct):
    i = pl.program_id(2)
    s = _dot_nt(q_ref[...], k_ref[...])
    col = lax.broadcasted_iota(jnp.int32, s.shape, 1)
    s = jnp.where(jnp.logical_and(i < nct, col >= ctx_len), -1e30, s)
    m = jnp.max(s, axis=-1, keepdims=True)
    p = jnp.exp(s - m)
    l = jnp.sum(p, axis=-1, keepdims=True)
    o_ref[...] = (_dot(p.astype(BF16), v_ref[...]) / l).astype(BF16)


def _mla_attention(geo, q, k, v):
    n = q.shape[0]
    b, s, tq, tpb = geo.b, geo.s, geo.tm, geo.tpb
    k3 = k.reshape(b, s, MLA_HEADS * QK_PAD)
    v3 = v.reshape(b, s, MLA_HEADS * V_HEAD)
    return pl.pallas_call(
        functools.partial(_attn_kernel, ctx_len=geo.l, nct=geo.nct),
        grid=(b, MLA_HEADS, tpb),
        in_specs=[
            pl.BlockSpec((tq, QK_PAD), lambda bb, h, i: (bb * tpb + i, h)),
            pl.BlockSpec((None, s, QK_PAD), lambda bb, h, i: (bb, 0, h)),
            pl.BlockSpec((None, s, V_HEAD), lambda bb, h, i: (bb, 0, h)),
        ],
        out_specs=pl.BlockSpec((tq, V_HEAD), lambda bb, h, i: (bb * tpb + i, h)),
        out_shape=jax.ShapeDtypeStruct((n, MLA_HEADS * V_HEAD), BF16),
        compiler_params=_cparams("parallel", "parallel", "parallel"),
        name="mla_attention",
    )(q, k3, v3)


def _topk_rows(s, k, payload=None):
    rows = lax.broadcasted_iota(jnp.int32, s.shape, 0).astype(F32)
    big = float(s.shape[0])
    vals, idxs = [], []
    for _ in range(k):
        m = jnp.max(s, axis=0, keepdims=True)
        idx = jnp.min(jnp.where(s == m, rows, big), axis=0, keepdims=True)
        hit = rows == idx
        vals.append(m)
        if payload is None:
            idxs.append(idx)
        else:
            idxs.append(jnp.sum(jnp.where(hit, payload, 0.0), axis=0, keepdims=True))
        s = jnp.where(hit, -jnp.inf, s)
    return jnp.concatenate(vals, axis=0), jnp.concatenate(idxs, axis=0)


def _peersel_kernel(q_ref, qn_ref, keys_ref, e_ref, g_ref):
    q = q_ref[...]
    ms = jnp.mean(q * q, axis=-1, keepdims=True)
    qn = q * lax.rsqrt(ms + EPS) * qn_ref[...]
    half = D_KEY // 2
    s1 = _dot3_nt(keys_ref[0], qn[:, 0:half])
    s2 = _dot3_nt(keys_ref[1], qn[:, half:D_KEY])
    t1, i1 = _topk_rows(s1, PEER_TOPK)
    t2, i2 = _topk_rows(s2, PEER_TOPK)
    cand = jnp.concatenate([t1[p:p + 1, :] + t2 for p in range(PEER_TOPK)], axis=0)
    cidx = jnp.concatenate([i1[p:p + 1, :] * float(N_KEYS) + i2 for p in range(PEER_TOPK)], axis=0)
    best, eidx = _topk_rows(cand, PEER_TOPK, payload=cidx)
    ex = jnp.exp(best - jnp.max(best, axis=0, keepdims=True))
    g_ref[...] = ex / jnp.sum(ex, axis=0, keepdims=True)
    e_ref[...] = eidx.astype(jnp.int32)


def _peer_select(qp, q_norm, keys, tm):
    n = qp.shape[0]
    return pl.pallas_call(
        _peersel_kernel,
        grid=(n // tm, PEER_HEADS),
        in_specs=[
            pl.BlockSpec((tm, D_KEY), lambda i, h: (i, h)),
            pl.BlockSpec((1, D_KEY), lambda i, h: (0, 0)),
            pl.BlockSpec((2, N_KEYS, D_KEY // 2), lambda i, h: (0, 0, 0)),
        ],
        out_specs=[pl.BlockSpec((PEER_TOPK, tm), lambda i, h: (h, i)),
                   pl.BlockSpec((PEER_TOPK, tm), lambda i, h: (h, i))],
        out_shape=[jax.ShapeDtypeStruct((PEER_HEADS * PEER_TOPK, n), jnp.int32),
                   jax.ShapeDtypeStruct((PEER_HEADS * PEER_TOPK, n), F32)],
        compiler_params=_cparams("parallel", "parallel"),
        name="peer_select",
    )(qp, q_norm.reshape(1, D_KEY), keys)


GATHER_SLOTS = 4
SLAB_PAD = SUBLANES


def _pack_expert_table(u, v):
    ne, d = u.shape
    ub = lax.bitcast_convert_type(u.astype(BF16), jnp.uint16).astype(jnp.uint32)
    vb = lax.bitcast_convert_type(v.astype(BF16), jnp.uint16).astype(jnp.uint32)
    return ((vb << 16) | ub).reshape(ne, d // LANES, LANES)


def _peergather_kernel(idx_ref, gate_ref, h_ref, x_ref, mod_ref, tab_ref, o_ref, *scratch, tb):
    nsel = PEER_HEADS * PEER_TOPK
    nrow = h_ref.shape[1] // LANES
    pitch = nrow + SLAB_PAD
    ns = GATHER_SLOTS
    bufs, sem = scratch[:ns], scratch[ns]
    lane_t = lax.broadcasted_iota(jnp.int32, (nsel, tb), 1)
    g2 = mod_ref[5:6, :]

    def row_copy(t, j, slot):
        return pltpu.make_async_copy(
            tab_ref.at[idx_ref[j, t]],
            bufs[slot].at[pl.ds(j * pitch, nrow), :],
            sem.at[slot])

    def issue(t, slot):
        for j in range(nsel):
            row_copy(t, j, slot).start(priority=j % 2)

    def wait(t, slot):
        for j in range(nsel):
            row_copy(t, j, slot).wait()

    def packed(slot, s):
        return bufs[slot][pl.ds(s, nsel, stride=pitch), :]

    def compute(t, slot):
        hrow = h_ref[pl.ds(t, 1), :]
        acc = jnp.zeros((nsel, LANES), F32)
        for s in range(nrow):
            u = lax.bitcast_convert_type(packed(slot, s) << 16, F32)
            acc = acc + u * hrow[:, LANES * s:LANES * (s + 1)]
        dots = jnp.sum(acc, axis=-1, keepdims=True)
        gcol = jnp.sum(jnp.where(lane_t == t, gate_ref[...], 0.0), axis=-1, keepdims=True)
        coef = gcol * _gelu(dots)
        outs = []
        for s in range(nrow):
            vv = lax.bitcast_convert_type(packed(slot, s) & jnp.uint32(0xFFFF0000), F32)
            outs.append(jnp.sum(coef * vv, axis=0, keepdims=True))
        orow = jnp.concatenate(outs, axis=1)
        o_ref[pl.ds(t, 1), :] = x_ref[pl.ds(t, 1), :] + g2 * orow

    def token_step(t, s, prefetch):
        wait(t, s)
        if prefetch:
            issue(t + ns - 1, (s + ns - 1) % ns)
        compute(t, s)

    for s in range(ns - 1):
        issue(s, s)
    ngroups = tb // ns

    def body(g, carry):
        for s in range(ns):
            token_step(g * ns + s, s, True)
        return carry

    lax.fori_loop(0, ngroups - 1, body, 0)
    for s in range(ns):
        token_step((ngroups - 1) * ns + s, s, s == 0)


def _peer_gather(geo_g, eidx_t, gate_t, h2, x, mods, table):
    n, d = x.shape
    tb = geo_g.tm
    nsel = PEER_HEADS * PEER_TOPK
    pitch = d // LANES + SLAB_PAD
    return pl.pallas_call(
        functools.partial(_peergather_kernel, tb=tb),
        grid=(n // tb,),
        in_specs=[
            pl.BlockSpec((nsel, tb), lambda i: (0, i), memory_space=pltpu.SMEM),
            pl.BlockSpec((nsel, tb), lambda i: (0, i)),
            pl.BlockSpec((tb, d), lambda i: (i, 0)),
            pl.BlockSpec((tb, d), lambda i: (i, 0)),
            geo_g.mod_spec(),
            pl.BlockSpec(memory_space=pl.ANY),
        ],
        out_specs=pl.BlockSpec((tb, d), lambda i: (i, 0)),
        out_shape=jax.ShapeDtypeStruct((n, d), F32),
        scratch_shapes=[pltpu.VMEM((nsel * pitch, LANES), jnp.uint32)] * GATHER_SLOTS
        + [pltpu.SemaphoreType.DMA((GATHER_SLOTS,))],
        compiler_params=_cparams("arbitrary"),
        name="peer_gather",
    )(eidx_t, gate_t, h2, x, mods, table)


def _pad_to(x, axis, size):
    pad = [(0, 0)] * x.ndim
    pad[axis] = (0, size - x.shape[axis])
    return jnp.pad(x, pad)


def _rwkv_layer(geo, xs, mods, norm1, mix, w_rkv, w_o, w0, w1, w2, a0, a1, a2, vl, g1, g2,
                k_k, k_a, r_k, ln_w, ln_b, vfirst):
    d = geo.d
    xm = _rwkv_mix(geo, xs, norm1, mods, mix)
    rkv = _bmm(xm, w_rkv.astype(BF16), (0, 2, 3), ("none",) * 3, F32, geo.tm)
    lt = g1.shape[1]
    w1c = jnp.concatenate([_pad_to(w1[0], 1, LORA_PAD), _pad_to(w1[1], 1, LORA_PAD)], axis=1)
    a1c = jnp.concatenate([_pad_to(a1[0], 1, LORA_PAD), _pad_to(a1[1], 1, LORA_PAD)], axis=1)
    if vl is None:
        v1p = jnp.zeros((d, lt), F32)
        v2p = jnp.zeros((LORA_PAD, d), F32)
        v0 = jnp.zeros((d,), F32)
    else:
        v0, v1, v2 = vl
        v1p = _pad_to(v1, 1, lt)
        v2p = _pad_to(v2, 0, LORA_PAD)
    wl1 = jnp.stack([_pad_to(w1c, 1, lt), _pad_to(a1c, 1, lt), g1, v1p]).astype(BF16)
    tl = _bmm(xm, wl1, (1, 4, 5, 3), ("tanh", "none", "sigmoid", "none"), BF16, geo.tm)
    w2p = jnp.stack([_pad_to(w2[0], 0, LORA_PAD), _pad_to(w2[1], 0, LORA_PAD)]).astype(BF16)
    a2p = jnp.stack([_pad_to(a2[0], 0, LORA_PAD), _pad_to(a2[1], 0, LORA_PAD)]).astype(BF16)
    pvec = jnp.stack([w0[0], w0[1], a0[0], a0[1], v0, k_k, k_a, jnp.zeros_like(k_k)])
    lw, kd, asg, kk, g, v = _rwkv_feat(geo, rkv, tl, w2p, a2p, g2.astype(BF16), v2p.astype(BF16),
                                      pvec, vfirst)
    r = rkv[0]
    wkv = _wkv_bidir(r, v, kk, lw, kd, asg, geo.b, geo.l)
    pv2 = _pad_to(jnp.stack([r_k, ln_w, ln_b]), 0, SUBLANES)
    y = _rwkv_readout(geo, wkv, r, kd, v, g, pv2)
    xs = _matmul_res(geo, y, w_o.astype(BF16), xs, mods, 2)
    return xs, v


def _rope_tables(geo):
    t = geo.t
    pos = jnp.arange(t)
    row = (pos // GRID_W).astype(F32)
    col = (pos % GRID_W).astype(F32)
    n_freq = QK_ROPE // 4
    inv_freq = ROPE_THETA ** (-jnp.arange(n_freq, dtype=F32) / n_freq)
    ang = jnp.concatenate([row[:, None] * inv_freq, col[:, None] * inv_freq], axis=-1)
    cos, sin = jnp.cos(ang), jnp.sin(ang)
    pad = LANES - QK_ROPE
    cos_l = jnp.concatenate([cos, cos, jnp.ones((t, pad), F32)], axis=1)
    sin_l = jnp.concatenate([-sin, sin, jnp.zeros((t, pad), F32)], axis=1)
    cos_c = jnp.ones((geo.l, LANES), F32)
    sin_c = jnp.zeros((geo.l, LANES), F32)
    return jnp.concatenate([cos_c, cos_l], axis=0), jnp.concatenate([sin_c, sin_l], axis=0)


def _mla_layer(geo, xs, mods, norm1, rope_t, w_in, q_norm, kv_norm, w_uq, w_ukv, g_q, g_k, w_o):
    q_lora, kv_lora = q_norm.shape[0], kv_norm.shape[0]
    zw = q_lora + kv_lora + LANES
    z = _mod_matmul(geo, xs, norm1, mods, _pad_to(w_in, 1, zw).astype(BF16), 0, False)
    qk = QK_NOPE + QK_ROPE
    wuq_p = _pad_to(w_uq.reshape(q_lora, MLA_HEADS, qk), 2, QK_PAD).reshape(q_lora, -1)
    gq_p = _pad_to(g_q, 0, QK_PAD).reshape(1, QK_PAD)
    gk_p = _pad_to(g_k, 0, QK_PAD).reshape(1, QK_PAD)
    q, k, v = _mla_qkv(geo, z, q_norm, kv_norm, wuq_p.astype(BF16), w_ukv.astype(BF16),
                       gq_p, gk_p, *rope_t)
    o = _mla_attention(geo, q, k, v)
    return _matmul_res(geo, o, w_o.astype(BF16), xs, mods, 2)


def _peer_layer(geo, geo_g, xs, mods, norm2, w_q, q_norm, keys, u, v):
    qp, h2 = _mod_matmul(geo, xs, norm2, mods, w_q.astype(BF16), 1, True)
    eidx_t, gate_t = _peer_select(qp, q_norm, keys, geo.tm)
    return _peer_gather(geo_g, eidx_t, gate_t, h2, xs, mods, _pack_expert_table(u, v))


def kernel(x, c, ctx, c_ctx, w_ada, b_ada, norm1, norm2, rw_mix, rw_wrkv, rw_wo, rw_w0, rw_w1, rw_w2, rw_a0, rw_a1, rw_a2, rw_v0, rw_v1, rw_v2, rw_g1, rw_g2, rw_kk, rw_ka, rw_rk, rw_lnw, rw_lnb, mla_win, mla_qnorm, mla_kvnorm, mla_wuq, mla_wukv, mla_gq, mla_gk, mla_wo, peer_wq, peer_qnorm, peer_keys, peer_u, peer_v):
    b, t, d = x.shape
    l = ctx.shape[1]
    depth = w_ada.shape[0]
    geo = _Geom(b, l, t, d, min(256, l))
    geo_g = _Geom(b, l, t, d, min(128, l))
    cond8 = _pad_to(jnp.concatenate([c, c_ctx[None, :]], axis=0), 0, SUBLANES)
    ada = _adaln(cond8, w_ada, b_ada).reshape(depth, SUBLANES, 6, d)
    mods_all = jnp.stack([jnp.broadcast_to(ada[:, b:b + 1], (depth, b, 6, d)), ada[:, 0:b]], axis=2)
    xs = jnp.concatenate([ctx, x], axis=1).reshape(b * (l + t), d)
    rope_t = _rope_tables(geo)
    vfirst = None
    for i in range(depth):
        j = i // 2
        mods = mods_all[i]
        if i % 2 == 0:
            vl = None if j == 0 else (rw_v0[j - 1], rw_v1[j - 1], rw_v2[j - 1])
            xs, vcur = _rwkv_layer(geo, xs, mods, norm1[i], rw_mix[j], rw_wrkv[j], rw_wo[j],
                                   rw_w0[j], rw_w1[j], rw_w2[j], rw_a0[j], rw_a1[j], rw_a2[j], vl,
                                   rw_g1[j], rw_g2[j], rw_kk[j], rw_ka[j], rw_rk[j], rw_lnw[j],
                                   rw_lnb[j], vfirst)
            if j == 0:
                vfirst = vcur
        else:
            xs = _mla_layer(geo, xs, mods, norm1[i], rope_t, mla_win[j], mla_qnorm[j],
                            mla_kvnorm[j], mla_wuq[j], mla_wukv[j], mla_gq[j], mla_gk[j], mla_wo[j])
        xs = _peer_layer(geo, geo_g, xs, mods, norm2[i], peer_wq[i], peer_qnorm[i], peer_keys[i],
                         peer_u[i], peer_v[i])
    return xs.reshape(b, l + t, d)[:, l:, :]
```

```python
import functools
import math

import jax
import jax.numpy as jnp
from jax import lax
from jax.experimental import pallas as pl
from jax.experimental.pallas import tpu as pltpu

F32 = jnp.float32
BF16 = jnp.bfloat16

EPS = 1e-6
GN_EPS = 64e-5
RW_HEAD = 64
WKV_CHUNK = 64
MLA_HEADS = 16
QK_NOPE = 128
QK_ROPE = 64
V_HEAD = 128
QK_PAD = 256
ROPE_THETA = 10000.0
GRID_W = 64
ATTN_SCALE = (QK_NOPE + QK_ROPE) ** -0.5
PEER_HEADS = 8
N_KEYS = 128
PEER_TOPK = 16
D_KEY = 256
LORA_PAD = 128

LANES = 128
SUBLANES = 8
VMEM_LIMIT = 56 * 1024 * 1024


def _cparams(*sem):
    return pltpu.CompilerParams(dimension_semantics=sem, vmem_limit_bytes=VMEM_LIMIT)


def _dot(a, b):
    return jnp.dot(a, b, preferred_element_type=F32)


def _dot_nt(a, b):
    return lax.dot_general(a, b, (((1,), (1,)), ((), ())), preferred_element_type=F32)


def _split2(x):
    hi = x.astype(BF16)
    lo = (x - hi.astype(F32)).astype(BF16)
    return hi, lo


def _split3(x):
    hi = x.astype(BF16)
    r1 = x - hi.astype(F32)
    mid = r1.astype(BF16)
    lo = (r1 - mid.astype(F32)).astype(BF16)
    return hi, mid, lo


def _dot3(a, b):
    ah, al = _split2(a)
    bh, bl = _split2(b)
    return _dot(ah, bh) + (_dot(ah, bl) + _dot(al, bh))


def _dot3_nt(a, b):
    ah, al = _split2(a)
    bh, bl = _split2(b)
    return _dot_nt(ah, bh) + (_dot_nt(ah, bl) + _dot_nt(al, bh))


def _dot_exact_lhs(sel, x):
    hi, mid, lo = _split3(x)
    return _dot(sel, hi) + (_dot(sel, mid) + _dot(sel, lo))


def _modulate(x, g, shift, scale):
    ms = jnp.mean(x * x, axis=-1, keepdims=True)
    return (x * lax.rsqrt(ms + EPS) * g) * (1.0 + scale) + shift


def _sigmoid(x):
    return 1.0 / (1.0 + jnp.exp(-x))


def _softplus(y):
    return jnp.maximum(y, 0.0) + jnp.log(1.0 + jnp.exp(-jnp.abs(y)))


def _erf(x):
    return lax.erf(x)


def _gelu(x):
    return 0.5 * x * (1.0 + _erf(x * (2.0 ** -0.5)))


def _ada_kernel(s_ref, w_ref, b_ref, o_ref):
    s = s_ref[...]
    s = s * _sigmoid(s)
    o_ref[...] = _dot3(s, w_ref[...]) + b_ref[...]


def _adaln(cond8, w_ada, b_ada):
    depth, d, n = w_ada.shape
    tn = 1024
    return pl.pallas_call(
        _ada_kernel,
        grid=(depth, n // tn),
        in_specs=[
            pl.BlockSpec((SUBLANES, d), lambda l, j: (0, 0)),
            pl.BlockSpec((None, d, tn), lambda l, j: (l, 0, j)),
            pl.BlockSpec((None, 1, tn), lambda l, j: (l, 0, j)),
        ],
        out_specs=pl.BlockSpec((None, SUBLANES, tn), lambda l, j: (l, 0, j)),
        out_shape=jax.ShapeDtypeStruct((depth, SUBLANES, n), F32),
        compiler_params=_cparams("parallel", "parallel"),
        name="adaln",
    )(cond8, w_ada, b_ada.reshape(depth, 1, n))


class _Geom:
    def __init__(self, batch, ctx_len, seq_len, d_model, tm):
        self.b, self.l, self.t, self.d = batch, ctx_len, seq_len, d_model
        self.s = ctx_len + seq_len
        self.n = batch * self.s
        self.tm = tm
        assert ctx_len % tm == 0 and seq_len % tm == 0
        self.tpb = self.s // tm
        self.nct = ctx_len // tm
        self.ntiles = self.n // tm

    def mod_spec(self, nlead=0):
        tpb, nct = self.tpb, self.nct

        def imap(*ids):
            i = ids[nlead]
            return (i // tpb, ((i % tpb) >= nct).astype(jnp.int32), 0, 0)

        return pl.BlockSpec((None, None, 6, self.d), imap)


def _modmm_kernel(x_ref, g_ref, mod_ref, w_ref, o_ref, *h_ref, which):
    h = _modulate(x_ref[...], g_ref[...], mod_ref[3 * which:3 * which + 1, :],
                  mod_ref[3 * which + 1:3 * which + 2, :])
    if h_ref:
        h_ref[0][...] = h
    o_ref[...] = _dot(h.astype(BF16), w_ref[...])


def _mod_matmul(geo, x, g, mods, w, which, emit_h):
    n, d = x.shape
    nn = w.shape[1]
    tm = geo.tm
    out_shape = [jax.ShapeDtypeStruct((n, nn), F32)]
    out_specs = [pl.BlockSpec((tm, nn), lambda i: (i, 0))]
    if emit_h:
        out_shape.append(jax.ShapeDtypeStruct((n, d), F32))
        out_specs.append(pl.BlockSpec((tm, d), lambda i: (i, 0)))
    res = pl.pallas_call(
        functools.partial(_modmm_kernel, which=which),
        grid=(geo.ntiles,),
        in_specs=[
            pl.BlockSpec((tm, d), lambda i: (i, 0)),
            pl.BlockSpec((1, d), lambda i: (0, 0)),
            geo.mod_spec(),
            pl.BlockSpec((d, nn), lambda i: (0, 0)),
        ],
        out_specs=out_specs,
        out_shape=out_shape,
        compiler_params=_cparams("parallel"),
        name="mod_matmul",
    )(x, g.reshape(1, d), mods, w)
    return res if emit_h else res[0]


def _mmres_kernel(y_ref, w_ref, x_ref, mod_ref, o_ref, *, gidx):
    acc = _dot(y_ref[...], w_ref[...])
    o_ref[...] = x_ref[...] + mod_ref[gidx:gidx + 1, :] * acc


def _matmul_res(geo, y, w, x, mods, gidx):
    n, k = y.shape
    d = x.shape[1]
    tm = geo.tm
    return pl.pallas_call(
        functools.partial(_mmres_kernel, gidx=gidx),
        grid=(geo.ntiles,),
        in_specs=[
            pl.BlockSpec((tm, k), lambda i: (i, 0)),
            pl.BlockSpec((k, d), lambda i: (0, 0)),
            pl.BlockSpec((tm, d), lambda i: (i, 0)),
            geo.mod_spec(),
        ],
        out_specs=pl.BlockSpec((tm, d), lambda i: (i, 0)),
        out_shape=jax.ShapeDtypeStruct((n, d), F32),
        compiler_params=_cparams("parallel"),
        name="matmul_res",
    )(y, w, x, mods)


def _bmm_kernel(x_ref, w_ref, o_ref, *, acts):
    j = pl.program_id(0)
    y = _dot(x_ref[...], w_ref[...])
    out = y
    for jj, a in enumerate(acts):
        if a == "tanh":
            out = jnp.where(j == jj, jnp.tanh(y), out)
        elif a == "sigmoid":
            out = jnp.where(j == jj, _sigmoid(y), out)
    o_ref[...] = out.astype(o_ref.dtype)


def _bmm(x3, w3, src, acts, out_dtype, tm):
    _, n, k = x3.shape
    nj, _, nn = w3.shape
    src = tuple(src)

    def xmap(j, i):
        idx = jnp.int32(src[0])
        for jj in range(1, nj):
            idx = jnp.where(j == jj, jnp.int32(src[jj]), idx)
        return (idx, i, 0)

    return pl.pallas_call(
        functools.partial(_bmm_kernel, acts=tuple(acts)),
        grid=(nj, n // tm),
        in_specs=[
            pl.BlockSpec((None, tm, k), xmap),
            pl.BlockSpec((None, k, nn), lambda j, i: (j, 0, 0)),
        ],
        out_specs=pl.BlockSpec((None, tm, nn), lambda j, i: (j, i, 0)),
        out_shape=jax.ShapeDtypeStruct((nj, n, nn), out_dtype),
        compiler_params=_cparams("parallel", "parallel"),
        name="bmm",
    )(x3, w3)


def _rwmix_kernel(x_ref, xp_ref, xn_ref, g_ref, mod_ref, mix_ref, o_ref, *, tpb, nct):
    i = pl.program_id(0)
    tm = x_ref.shape[0]
    g = g_ref[...]
    shift = mod_ref[0:1, :]
    scale = mod_ref[1:2, :]
    h = _modulate(x_ref[...], g, shift, scale)
    hp = _modulate(xp_ref[...], g, shift, scale)[SUBLANES - 1:SUBLANES, :]
    hn = _modulate(xn_ref[...], g, shift, scale)[0:1, :]
    it = i % tpb
    first = jnp.logical_or(it == 0, it == nct)
    last = jnp.logical_or(it == nct - 1, it == tpb - 1)
    hp = jnp.where(first, 0.0, hp)
    hn = jnp.where(last, 0.0, hn)
    rows = lax.broadcasted_iota(jnp.int32, h.shape, 0)
    prev = jnp.where(rows == 0, hp, pltpu.roll(h, 1, axis=0))
    nxt = jnp.where(rows == tm - 1, hn, pltpu.roll(h, tm - 1, axis=0))
    xx = 0.5 * (prev + nxt) - h
    for m in range(6):
        o_ref[m] = (h + xx * mix_ref[m:m + 1, :]).astype(BF16)


def _rwkv_mix(geo, x, g, mods, mix):
    n, d = x.shape
    tm = geo.tm
    r8 = tm // SUBLANES
    nblk8 = n // SUBLANES
    return pl.pallas_call(
        functools.partial(_rwmix_kernel, tpb=geo.tpb, nct=geo.nct),
        grid=(geo.ntiles,),
        in_specs=[
            pl.BlockSpec((tm, d), lambda i: (i, 0)),
            pl.BlockSpec((SUBLANES, d), lambda i: (jnp.maximum(i * r8 - 1, 0), 0)),
            pl.BlockSpec((SUBLANES, d), lambda i: (jnp.minimum((i + 1) * r8, nblk8 - 1), 0)),
            pl.BlockSpec((1, d), lambda i: (0, 0)),
            geo.mod_spec(),
            pl.BlockSpec((6, d), lambda i: (0, 0)),
        ],
        out_specs=pl.BlockSpec((6, tm, d), lambda i: (0, i, 0)),
        out_shape=jax.ShapeDtypeStruct((6, n, d), BF16),
        compiler_params=_cparams("parallel"),
        name="rwkv_mix",
    )(x, x, x, g.reshape(1, d), mods, mix)


def _head_sum(x, bd):
    hi, lo = _split2(x)
    return _dot(hi, bd) + _dot(lo, bd)


def _rwfeat_kernel(k_ref, v_ref, tl_ref, w2_ref, a2_ref, g2_ref, v2_ref, pv_ref, bd_ref,
                   *rest, has_vlora):
    if has_vlora:
        vf_ref, lw_ref, kd_ref, as_ref, kk_ref, g_ref, vo_ref = rest
    else:
        lw_ref, kd_ref, as_ref, kk_ref, g_ref = rest
    k = k_ref[...]
    tw = tl_ref[0]
    ta = tl_ref[1]
    tg = tl_ref[2]
    w0 = pv_ref[0:2, :]
    a0 = pv_ref[2:4, :]
    k_k = pv_ref[5:6, :]
    k_a = pv_ref[6:7, :]
    for z in range(2):
        sl = slice(LORA_PAD * z, LORA_PAD * (z + 1))
        lora_w = _dot(tw[:, sl], w2_ref[z])
        w = -_softplus(-(w0[z:z + 1, :] + lora_w)) - 0.5
        lw_ref[z] = -jnp.exp(w)
        a_sig = _sigmoid(a0[z:z + 1, :] + _dot(ta[:, sl], a2_ref[z]))
        as_ref[z] = a_sig
        kd_ref[z] = k * (1.0 + (a_sig - 1.0) * k_a)
    g_ref[...] = _dot(tg, g2_ref[...])
    kkr = k * k_k
    ss = _head_sum(kkr * kkr, bd_ref[...])
    kk_ref[...] = kkr * lax.rsqrt(ss + 1e-12)
    if has_vlora:
        v = v_ref[...]
        tv = tl_ref[3]
        gate = _sigmoid(pv_ref[4:5, :] + _dot(tv[:, 0:LORA_PAD], v2_ref[...]))
        vo_ref[...] = v + (vf_ref[...] - v) * gate


def _rwkv_feat(geo, rkv, tl, w2p, a2p, g2, v2p, pvec, vfirst):
    _, n, d = rkv.shape
    tm, tc = geo.tm, 512
    has_vlora = vfirst is not None
    lt = tl.shape[2]
    ii = lax.broadcasted_iota(jnp.int32, (tc, tc), 0) // RW_HEAD
    jj = lax.broadcasted_iota(jnp.int32, (tc, tc), 1) // RW_HEAD
    bd = (ii == jj).astype(BF16)
    row = lambda i, j: (i, j)
    in_specs = [
        pl.BlockSpec((None, tm, tc), lambda i, j: (1, i, j)),
        pl.BlockSpec((None, tm, tc), lambda i, j: (2, i, j)),
        pl.BlockSpec((4, tm, lt), lambda i, j: (0, i, 0)),
        pl.BlockSpec((2, LORA_PAD, tc), lambda i, j: (0, 0, j)),
        pl.BlockSpec((2, LORA_PAD, tc), lambda i, j: (0, 0, j)),
        pl.BlockSpec((lt, tc), lambda i, j: (0, j)),
        pl.BlockSpec((LORA_PAD, tc), lambda i, j: (0, j)),
        pl.BlockSpec((SUBLANES, tc), lambda i, j: (0, j)),
        pl.BlockSpec((tc, tc), lambda i, j: (0, 0)),
    ]
    args = [rkv, rkv, tl, w2p, a2p, g2, v2p, pvec, bd]
    dir_spec = pl.BlockSpec((2, tm, tc), lambda i, j: (0, i, j))
    out_specs = [dir_spec, dir_spec, dir_spec, pl.BlockSpec((tm, tc), row), pl.BlockSpec((tm, tc), row)]
    out_shape = [jax.ShapeDtypeStruct((2, n, d), F32)] * 3 + [jax.ShapeDtypeStruct((n, d), F32)] * 2
    if has_vlora:
        in_specs.append(pl.BlockSpec((tm, tc), row))
        args.append(vfirst)
        out_specs.append(pl.BlockSpec((tm, tc), row))
        out_shape.append(jax.ShapeDtypeStruct((n, d), F32))
    res = pl.pallas_call(
        functools.partial(_rwfeat_kernel, has_vlora=has_vlora),
        grid=(geo.ntiles, d // tc),
        in_specs=in_specs,
        out_specs=out_specs,
        out_shape=out_shape,
        compiler_params=_cparams("parallel", "parallel"),
        name="rwkv_feat",
    )(*args)
    if has_vlora:
        lw, kd, asg, kk, g, v = res
    else:
        lw, kd, asg, kk, g = res
        v = rkv[2]
    return lw, kd, asg, kk, g, v


def _wkv_chunk_kernel(r_ref, v_ref, kk_ref, lw_ref, kd_ref, as_ref,
                      m_ref, ga_ref, rq_ref, o0_ref, pc_ref, *, npairs):
    c = WKV_CHUNK
    c2 = 2 * c
    sgn = 1 - 2 * pl.program_id(0)
    ri = lax.broadcasted_iota(jnp.int32, (c2, c2), 0)
    ci = lax.broadcasted_iota(jnp.int32, (c2, c2), 1)
    same = (ri >= c) == (ci >= c)
    tt = jnp.where(ri >= c, ri - c, ri)
    ss = jnp.where(ci >= c, ci - c, ci)
    earlier = (ss - tt) * sgn < 0
    strict = jnp.logical_and(same, earlier)
    incl = jnp.logical_and(same, jnp.logical_or(earlier, ss == tt))
    eye = (ri == ci).astype(F32)
    r64 = lax.broadcasted_iota(jnp.int32, (c, c), 0)
    c64 = lax.broadcasted_iota(jnp.int32, (c, c), 1)
    ltri = jnp.where((c64 - r64) * sgn <= 0, 1.0, 0.0).astype(BF16)
    head0 = lax.broadcasted_iota(jnp.int32, (c, LANES), 1) < RW_HEAD
    pairs = range(npairs)

    def stack(x):
        return jnp.concatenate([jnp.where(head0, x, 0.0), jnp.where(head0, 0.0, x)], axis=0)

    def dup(x):
        return jnp.concatenate([x, x], axis=0)

    def bf(x):
        return x.astype(BF16)

    lhs, rhs, a2, bp2, kp2, vst, r2 = [], [], [], [], [], [], []
    for p in pairs:
        sl = slice(LANES * p, LANES * (p + 1))
        lw = lw_ref[:, sl]
        cum = _dot_exact_lhs(ltri, lw)
        tot = jnp.sum(lw, axis=0, keepdims=True)
        p_inv = jnp.exp(-cum)
        p_end = jnp.exp(tot - cum)
        kk = kk_ref[:, sl]
        b = kk * as_ref[:, sl]
        kd = kd_ref[:, sl]
        a2p = stack(-kk * jnp.exp(cum - lw))
        r2p = stack(r_ref[:, sl] * jnp.exp(cum))
        a2.append(bf(a2p))
        r2.append(r2p)
        lhs.append(jnp.concatenate([a2[p], bf(r2p)], axis=0))
        rhs.append(jnp.concatenate([dup(bf(b * p_inv)), dup(bf(kd * p_inv))], axis=0))
        bp2.append(bf(stack(b * p_end)))
        kp2.append(bf(stack(kd * p_end)))
        vst.append(bf(stack(v_ref[:, sl])))
        pc_ref[:, sl] = jnp.broadcast_to(jnp.exp(tot), (SUBLANES, LANES))
    gram = [_dot_nt(lhs[p], rhs[p]) for p in pairs]
    nab = [jnp.where(strict, gram[p][0:c2, 0:c2], 0.0) for p in pairs]
    nrb = [bf(jnp.where(incl, gram[p][c2:2 * c2, 0:c2], 0.0)) for p in pairs]
    nk = [bf(jnp.concatenate([jnp.where(strict, gram[p][0:c2, c2:2 * c2], 0.0),
                              jnp.where(incl, gram[p][c2:2 * c2, c2:2 * c2], 0.0)], axis=0))
          for p in pairs]
    xv = [_dot(nk[p], vst[p]) for p in pairs]
    tinv = [eye + nab[p] for p in pairs]
    npow = [bf(nab[p]) for p in pairs]
    for _ in range(int(math.log2(c)) - 1):
        npow = [bf(_dot(npow[p], npow[p])) for p in pairs]
        tinv = [tinv[p] + _dot(bf(tinv[p]), npow[p]) for p in pairs]
    y = [_dot(bf(tinv[p]), jnp.concatenate([a2[p], bf(xv[p][0:c2, :])], axis=1)) for p in pairs]
    yb = [bf(y[p]) for p in pairs]
    z = [_dot(nrb[p], yb[p]) for p in pairs]
    mg = [_dot(bf(y[p].T), bp2[p]) for p in pairs]
    vk = [_dot(bf(vst[p].astype(F32).T), kp2[p]) for p in pairs]
    for p in pairs:
        sl = slice(LANES * p, LANES * (p + 1))
        m_ref[:, sl] = mg[p][0:c2, :]
        ga_ref[:, sl] = mg[p][c2:2 * c2, :] + vk[p]
        rq_ref[:, sl] = r2[p] + z[p][:, 0:c2]
        o0 = z[p][:, c2:2 * c2] + xv[p][c2:2 * c2, :]
        o0_ref[:, sl] = o0[0:c, :] + o0[c:c2, :]


def _wkv_scan_kernel(m_ref, ga_ref, rq_ref, o0_ref, pc_ref, o_ref, g_scr, *, npairs):
    c = WKV_CHUNK

    @pl.when(pl.program_id(2) == 0)
    def _():
        g_scr[...] = jnp.zeros_like(g_scr)

    for p in range(npairs):
        sl = slice(LANES * p, LANES * (p + 1))
        g = g_scr[p]
        o_st = _dot3_nt(rq_ref[:, sl], g)
        o_ref[:, sl] = o_st[0:c, :] + o_st[c:2 * c, :] + o0_ref[:, sl]
        g_scr[p] = g * pc_ref[0:1, sl] + _dot3(g, m_ref[:, sl]) + ga_ref[:, sl]


WKV_CHUNK_LANES = 1024


def _wkv_bidir(r, v, kk, lw, kd, asg, batch, ctx_len):
    n, d = r.shape
    c = WKV_CHUNK
    s = n // batch
    ncs = s // c
    ncc = ctx_len // c
    nch = n // c
    lanes = min(WKV_CHUNK_LANES, d)
    ngrp = d // lanes
    shared = pl.BlockSpec((c, lanes), lambda z, i, j: (i, j))
    perdir = pl.BlockSpec((None, c, lanes), lambda z, i, j: (z, i, j))
    big = pl.BlockSpec((None, 2 * c, lanes), lambda z, i, j: (z, i, j))
    m_, ga_, rq_, o0_, pc_ = pl.pallas_call(
        functools.partial(_wkv_chunk_kernel, npairs=lanes // LANES),
        grid=(2, nch, ngrp),
        in_specs=[shared, shared, shared, perdir, perdir, perdir],
        out_specs=[big, big, big, perdir,
                   pl.BlockSpec((None, SUBLANES, lanes), lambda z, i, j: (z, i, j))],
        out_shape=[jax.ShapeDtypeStruct((2, nch * 2 * c, d), F32)] * 3
        + [jax.ShapeDtypeStruct((2, n, d), F32),
           jax.ShapeDtypeStruct((2, nch * SUBLANES, d), F32)],
        compiler_params=_cparams("parallel", "parallel", "parallel"),
        name="wkv_chunk",
    )(r, v, kk, lw, kd, asg)

    def cmap(z, b, cc):
        back = jnp.where(cc < ncc, ncc - 1 - cc, ncs + ncc - 1 - cc)
        return (z, b * ncs + jnp.where(z == 0, cc, back), 0)

    out = pl.pallas_call(
        functools.partial(_wkv_scan_kernel, npairs=d // LANES),
        grid=(2, batch, ncs),
        in_specs=[
            pl.BlockSpec((None, 2 * c, d), cmap),
            pl.BlockSpec((None, 2 * c, d), cmap),
            pl.BlockSpec((None, 2 * c, d), cmap),
            pl.BlockSpec((None, c, d), cmap),
            pl.BlockSpec((None, SUBLANES, d), cmap),
        ],
        out_specs=pl.BlockSpec((None, c, d), cmap),
        out_shape=jax.ShapeDtypeStruct((2, n, d), F32),
        scratch_shapes=[pltpu.VMEM((d // LANES, 2 * c, 2 * c), F32)],
        compiler_params=_cparams("parallel", "parallel", "arbitrary"),
        name="wkv_scan",
    )(m_, ga_, rq_, o0_, pc_)
    return out


def _rwread_kernel(o_ref, r_ref, kd_ref, v_ref, g_ref, pv_ref, bd_ref, y_ref):
    bd = bd_ref[...]
    wkv = o_ref[0] + o_ref[1]
    inv = 1.0 / RW_HEAD
    mu = _head_sum(wkv, bd) * inv
    dev = wkv - mu
    var = _head_sum(dev * dev, bd) * inv
    y = dev * lax.rsqrt(var + GN_EPS) * pv_ref[1:2, :] + pv_ref[2:3, :]
    rk = r_ref[...] * (kd_ref[0] + kd_ref[1]) * pv_ref[0:1, :]
    y = y + _head_sum(rk, bd) * v_ref[...]
    y_ref[...] = (y * g_ref[...]).astype(BF16)


def _rwkv_readout(geo, wkv, r, kd, v, g, pvec):
    n, d = r.shape
    tm, tc = geo.tm, 512
    ii = lax.broadcasted_iota(jnp.int32, (tc, tc), 0) // RW_HEAD
    jj = lax.broadcasted_iota(jnp.int32, (tc, tc), 1) // RW_HEAD
    bd = (ii == jj).astype(BF16)
    row = pl.BlockSpec((tm, tc), lambda i, j: (i, j))
    dirs = pl.BlockSpec((2, tm, tc), lambda i, j: (0, i, j))
    return pl.pallas_call(
        _rwread_kernel,
        grid=(geo.ntiles, d // tc),
        in_specs=[dirs, row, dirs, row, row,
                  pl.BlockSpec((SUBLANES, tc), lambda i, j: (0, j)),
                  pl.BlockSpec((tc, tc), lambda i, j: (0, 0))],
        out_specs=row,
        out_shape=jax.ShapeDtypeStruct((n, d), BF16),
        compiler_params=_cparams("parallel", "parallel"),
        name="rwkv_readout",
    )(wkv, r, kd, v, g, pvec, bd)


def _mlaqkv_kernel(z_ref, qn_ref, kvn_ref, wuq_ref, wukv_ref, gq_ref, gk_ref, cos_ref, sin_ref,
                   q_ref, k_ref, v_ref, *, q_lora, kv_lora):
    z = z_ref[...]
    cq = z[:, 0:q_lora]
    ckv = z[:, q_lora:q_lora + kv_lora]
    krot = z[:, q_lora + kv_lora:q_lora + kv_lora + LANES]

    def rms(x, g):
        ms = jnp.mean(x * x, axis=-1, keepdims=True)
        return x * lax.rsqrt(ms + EPS) * g

    qf = _dot(rms(cq, qn_ref[...]).astype(BF16), wuq_ref[...])
    kvf = _dot(rms(ckv, kvn_ref[...]).astype(BF16), wukv_ref[...])
    cos = cos_ref[...]
    sin = sin_ref[...]
    half = QK_ROPE // 2
    lane = lax.broadcasted_iota(jnp.int32, cos.shape, 1)

    def rope(x):
        up = pltpu.roll(x, LANES - half, axis=1)
        dn = pltpu.roll(x, half, axis=1)
        return x * cos + jnp.where(lane < half, up, dn) * sin

    inv_w = 1.0 / (QK_NOPE + QK_ROPE)
    gq = gq_ref[...]
    gk = gk_ref[...]
    kr_ss = jnp.sum(krot * krot, axis=-1, keepdims=True)
    for h in range(MLA_HEADS):
        o = QK_PAD * h
        qh = qf[:, o:o + QK_PAD]
        rs = lax.rsqrt(jnp.sum(qh * qh, axis=-1, keepdims=True) * inv_w + EPS)
        qn = qh * rs * gq * (ATTN_SCALE * math.log2(math.e))
        q_ref[:, o:o + QK_NOPE] = qn[:, 0:QK_NOPE].astype(BF16)
        q_ref[:, o + QK_NOPE:o + QK_PAD] = rope(qn[:, QK_NOPE:QK_PAD]).astype(BF16)
        kn = kvf[:, o:o + QK_NOPE]
        rsk = lax.rsqrt((jnp.sum(kn * kn, axis=-1, keepdims=True) + kr_ss) * inv_w + EPS)
        k_ref[:, o:o + QK_NOPE] = (kn * rsk * gk[:, 0:QK_NOPE]).astype(BF16)
        k_ref[:, o + QK_NOPE:o + QK_PAD] = rope(krot * rsk * gk[:, QK_NOPE:QK_PAD]).astype(BF16)
        v_ref[:, V_HEAD * h:V_HEAD * (h + 1)] = kvf[:, o + QK_NOPE:o + QK_PAD].astype(BF16)


def _mla_qkv(geo, z, qn, kvn, wuq_p, wukv, gq_p, gk_p, cos_t, sin_t):
    n, zw = z.shape
    tm = geo.tm
    q_lora, kv_lora = qn.shape[0], kvn.shape[0]
    hq = MLA_HEADS * QK_PAD
    tpb = geo.tpb
    full = lambda i: (0, 0)
    rowmap = lambda i: (i, 0)
    return pl.pallas_call(
        functools.partial(_mlaqkv_kernel, q_lora=q_lora, kv_lora=kv_lora),
        grid=(geo.ntiles,),
        in_specs=[
            pl.BlockSpec((tm, zw), rowmap),
            pl.BlockSpec((1, q_lora), full),
            pl.BlockSpec((1, kv_lora), full),
            pl.BlockSpec((q_lora, hq), full),
            pl.BlockSpec((kv_lora, hq), full),
            pl.BlockSpec((1, QK_PAD), full),
            pl.BlockSpec((1, QK_PAD), full),
            pl.BlockSpec((tm, LANES), lambda i: (i % tpb, 0)),
            pl.BlockSpec((tm, LANES), lambda i: (i % tpb, 0)),
        ],
        out_specs=[pl.BlockSpec((tm, hq), rowmap), pl.BlockSpec((tm, hq), rowmap),
                   pl.BlockSpec((tm, MLA_HEADS * V_HEAD), rowmap)],
        out_shape=[jax.ShapeDtypeStruct((n, hq), BF16), jax.ShapeDtypeStruct((n, hq), BF16),
                   jax.ShapeDtypeStruct((n, MLA_HEADS * V_HEAD), BF16)],
        compiler_params=_cparams("parallel"),
        name="mla_qkv",
    )(z, qn.reshape(1, -1), kvn.reshape(1, -1), wuq_p, wukv, gq_p, gk_p, cos_t, sin_t)


ATTN_SUBTILES = 2


def _attn_kernel(q_ref, k_ref, v_ref, o_ref):
    k = k_ref[...]
    v = v_ref[...]
    rows = q_ref.shape[0] // ATTN_SUBTILES
    subs = range(ATTN_SUBTILES)
    s = [_dot_nt(q_ref[rows * i:rows * (i + 1), :], k) for i in subs]
    m = [jnp.max(s[i], axis=-1, keepdims=True) for i in subs]
    p = [jnp.exp2(s[i] - m[i]) for i in subs]
    l = [jnp.sum(p[i], axis=-1, keepdims=True) for i in subs]
    for i in subs:
        o_ref[rows * i:rows * (i + 1), :] = (_dot(p[i].astype(BF16), v) / l[i]).astype(BF16)


def _attn_call(q3, k3, v3, tq, nkeys):
    b, sq, _ = q3.shape
    return pl.pallas_call(
        _attn_kernel,
        grid=(b, MLA_HEADS, sq // tq),
        in_specs=[
            pl.BlockSpec((None, tq, QK_PAD), lambda bb, h, i: (bb, i, h)),
            pl.BlockSpec((None, nkeys, QK_PAD), lambda bb, h, i: (bb, 0, h)),
            pl.BlockSpec((None, nkeys, V_HEAD), lambda bb, h, i: (bb, 0, h)),
        ],
        out_specs=pl.BlockSpec((None, tq, V_HEAD), lambda bb, h, i: (bb, i, h)),
        out_shape=jax.ShapeDtypeStruct((b, sq, MLA_HEADS * V_HEAD), BF16),
        compiler_params=_cparams("parallel", "parallel", "parallel"),
        name="mla_attention",
    )(q3, k3, v3)


def _mla_attention(geo, q, k, v):
    b, s, l = geo.b, geo.s, geo.l
    q3 = q.reshape(b, s, MLA_HEADS * QK_PAD)
    k3 = k.reshape(b, s, MLA_HEADS * QK_PAD)
    v3 = v.reshape(b, s, MLA_HEADS * V_HEAD)
    o_ctx = _attn_call(q3[:, :l], k3, v3, geo.tm, l)
    tq_lat = 2 * geo.tm if geo.t % (2 * geo.tm) == 0 else geo.tm
    o_lat = _attn_call(q3[:, l:], k3, v3, tq_lat, s)
    return jnp.concatenate([o_ctx, o_lat], axis=1).reshape(b * s, MLA_HEADS * V_HEAD)


def _topk_rows(s, k, payload=None):
    rows = lax.broadcasted_iota(jnp.int32, s.shape, 0).astype(F32)
    big = float(s.shape[0])
    vals, idxs = [], []
    for _ in range(k):
        m = jnp.max(s, axis=0, keepdims=True)
        idx = jnp.min(jnp.where(s == m, rows, big), axis=0, keepdims=True)
        hit = rows == idx
        vals.append(m)
        if payload is None:
            idxs.append(idx)
        else:
            idxs.append(jnp.sum(jnp.where(hit, payload, 0.0), axis=0, keepdims=True))
        s = jnp.where(hit, -jnp.inf, s)
    return jnp.concatenate(vals, axis=0), jnp.concatenate(idxs, axis=0)


def _peersel_kernel(q_ref, qn_ref, keys_ref, e_ref, g_ref):
    q = q_ref[...]
    ms = jnp.mean(q * q, axis=-1, keepdims=True)
    qn = q * lax.rsqrt(ms + EPS) * qn_ref[...]
    half = D_KEY // 2
    s1 = _dot3_nt(keys_ref[0], qn[:, 0:half])
    s2 = _dot3_nt(keys_ref[1], qn[:, half:D_KEY])
    t1, i1 = _topk_rows(s1, PEER_TOPK)
    t2, i2 = _topk_rows(s2, PEER_TOPK)
    k = PEER_TOPK
    sub = lax.broadcasted_iota(jnp.int32, (SUBLANES, t1.shape[1]), 0)
    cand = [t1[0:1, :] + t2]
    cidx = [i1[0:1, :] * float(N_KEYS) + i2]
    for p in range(1, k // 2):
        live = sub < k // (p + 1)
        cand.append(jnp.where(live, t1[p:p + 1, :] + t2[0:SUBLANES, :], -jnp.inf))
        cidx.append(i1[p:p + 1, :] * float(N_KEYS) + i2[0:SUBLANES, :])
    cand.append(t1[k // 2:k, :] + t2[0:1, :])
    cidx.append(i1[k // 2:k, :] * float(N_KEYS) + i2[0:1, :])
    best, eidx = _topk_rows(jnp.concatenate(cand, axis=0), k, payload=jnp.concatenate(cidx, axis=0))
    ex = jnp.exp(best - jnp.max(best, axis=0, keepdims=True))
    g_ref[...] = ex / jnp.sum(ex, axis=0, keepdims=True)
    e_ref[...] = eidx.astype(jnp.int32)


def _peer_select(qp, q_norm, keys, tm):
    n = qp.shape[0]
    return pl.pallas_call(
        _peersel_kernel,
        grid=(n // tm, PEER_HEADS),
        in_specs=[
            pl.BlockSpec((tm, D_KEY), lambda i, h: (i, h)),
            pl.BlockSpec((1, D_KEY), lambda i, h: (0, 0)),
            pl.BlockSpec((2, N_KEYS, D_KEY // 2), lambda i, h: (0, 0, 0)),
        ],
        out_specs=[pl.BlockSpec((PEER_TOPK, tm), lambda i, h: (h, i)),
                   pl.BlockSpec((PEER_TOPK, tm), lambda i, h: (h, i))],
        out_shape=[jax.ShapeDtypeStruct((PEER_HEADS * PEER_TOPK, n), jnp.int32),
                   jax.ShapeDtypeStruct((PEER_HEADS * PEER_TOPK, n), F32)],
        compiler_params=_cparams("parallel", "parallel"),
        name="peer_select",
    )(qp, q_norm.reshape(1, D_KEY), keys)


GATHER_SLOTS = 4
SLAB_PAD = SUBLANES


def _pack_expert_table(u, v):
    ne, d = u.shape
    ub = lax.bitcast_convert_type(u.astype(BF16), jnp.uint16).astype(jnp.uint32)
    vb = lax.bitcast_convert_type(v.astype(BF16), jnp.uint16).astype(jnp.uint32)
    return ((vb << 16) | ub).reshape(ne, d // LANES, LANES)


def _peergather_kernel(idx_ref, gate_ref, h_ref, x_ref, mod_ref, tab_ref, o_ref, *scratch, tb):
    nsel = PEER_HEADS * PEER_TOPK
    nrow = h_ref.shape[1] // LANES
    pitch = nrow + SLAB_PAD
    ns = GATHER_SLOTS
    bufs, sem = scratch[:ns], scratch[ns]
    lane_t = lax.broadcasted_iota(jnp.int32, (nsel, tb), 1)
    g2 = mod_ref[5:6, :]

    def row_copy(t, j, slot):
        return pltpu.make_async_copy(
            tab_ref.at[idx_ref[j, t]],
            bufs[slot].at[pl.ds(j * pitch, nrow), :],
            sem.at[slot])

    def issue(t, slot):
        for j in range(nsel):
            row_copy(t, j, slot).start(priority=j % 2)

    def wait(t, slot):
        for j in range(nsel):
            row_copy(t, j, slot).wait()

    def packed(slot, s):
        return bufs[slot][pl.ds(s, nsel, stride=pitch), :]

    def compute(t, slot):
        hrow = h_ref[pl.ds(t, 1), :]
        acc = jnp.zeros((nsel, LANES), F32)
        for s in range(nrow):
            u = lax.bitcast_convert_type(packed(slot, s) << 16, F32)
            acc = acc + u * hrow[:, LANES * s:LANES * (s + 1)]
        dots = jnp.sum(acc, axis=-1, keepdims=True)
        gcol = jnp.sum(jnp.where(lane_t == t, gate_ref[...], 0.0), axis=-1, keepdims=True)
        coef = gcol * _gelu(dots)
        outs = []
        for s in range(nrow):
            vv = lax.bitcast_convert_type(packed(slot, s) & jnp.uint32(0xFFFF0000), F32)
            outs.append(jnp.sum(coef * vv, axis=0, keepdims=True))
        orow = jnp.concatenate(outs, axis=1)
        o_ref[pl.ds(t, 1), :] = x_ref[pl.ds(t, 1), :] + g2 * orow

    def token_step(t, s, prefetch):
        wait(t, s)
        if prefetch:
            issue(t + ns - 1, (s + ns - 1) % ns)
        compute(t, s)

    for s in range(ns - 1):
        issue(s, s)
    ngroups = tb // ns

    def body(g, carry):
        for s in range(ns):
            token_step(g * ns + s, s, True)
        return carry

    lax.fori_loop(0, ngroups - 1, body, 0)
    for s in range(ns):
        token_step((ngroups - 1) * ns + s, s, s == 0)


def _peer_gather(geo_g, eidx_t, gate_t, h2, x, mods, table):
    n, d = x.shape
    tb = geo_g.tm
    nsel = PEER_HEADS * PEER_TOPK
    pitch = d // LANES + SLAB_PAD
    return pl.pallas_call(
        functools.partial(_peergather_kernel, tb=tb),
        grid=(n // tb,),
        in_specs=[
            pl.BlockSpec((nsel, tb), lambda i: (0, i), memory_space=pltpu.SMEM),
            pl.BlockSpec((nsel, tb), lambda i: (0, i)),
            pl.BlockSpec((tb, d), lambda i: (i, 0)),
            pl.BlockSpec((tb, d), lambda i: (i, 0)),
            geo_g.mod_spec(),
            pl.BlockSpec(memory_space=pl.ANY),
        ],
        out_specs=pl.BlockSpec((tb, d), lambda i: (i, 0)),
        out_shape=jax.ShapeDtypeStruct((n, d), F32),
        scratch_shapes=[pltpu.VMEM((nsel * pitch, LANES), jnp.uint32)] * GATHER_SLOTS
        + [pltpu.SemaphoreType.DMA((GATHER_SLOTS,))],
        compiler_params=_cparams("arbitrary"),
        name="peer_gather",
    )(eidx_t, gate_t, h2, x, mods, table)


def _pad_to(x, axis, size):
    pad = [(0, 0)] * x.ndim
    pad[axis] = (0, size - x.shape[axis])
    return jnp.pad(x, pad)


def _rwkv_layer(geo, xs, mods, norm1, mix, w_rkv, w_o, w0, w1, w2, a0, a1, a2, vl, g1, g2,
                k_k, k_a, r_k, ln_w, ln_b, vfirst):
    d = geo.d
    xm = _rwkv_mix(geo, xs, norm1, mods, mix)
    rkv = _bmm(xm, w_rkv.astype(BF16), (0, 2, 3), ("none",) * 3, F32, geo.tm)
    lt = g1.shape[1]
    w1c = jnp.concatenate([_pad_to(w1[0], 1, LORA_PAD), _pad_to(w1[1], 1, LORA_PAD)], axis=1)
    a1c = jnp.concatenate([_pad_to(a1[0], 1, LORA_PAD), _pad_to(a1[1], 1, LORA_PAD)], axis=1)
    if vl is None:
        v1p = jnp.zeros((d, lt), F32)
        v2p = jnp.zeros((LORA_PAD, d), F32)
        v0 = jnp.zeros((d,), F32)
    else:
        v0, v1, v2 = vl
        v1p = _pad_to(v1, 1, lt)
        v2p = _pad_to(v2, 0, LORA_PAD)
    wl1 = jnp.stack([_pad_to(w1c, 1, lt), _pad_to(a1c, 1, lt), g1, v1p]).astype(BF16)
    tl = _bmm(xm, wl1, (1, 4, 5, 3), ("tanh", "none", "sigmoid", "none"), BF16, geo.tm)
    w2p = jnp.stack([_pad_to(w2[0], 0, LORA_PAD), _pad_to(w2[1], 0, LORA_PAD)]).astype(BF16)
    a2p = jnp.stack([_pad_to(a2[0], 0, LORA_PAD), _pad_to(a2[1], 0, LORA_PAD)]).astype(BF16)
    pvec = jnp.stack([w0[0], w0[1], a0[0], a0[1], v0, k_k, k_a, jnp.zeros_like(k_k)])
    lw, kd, asg, kk, g, v = _rwkv_feat(geo, rkv, tl, w2p, a2p, g2.astype(BF16), v2p.astype(BF16),
                                      pvec, vfirst)
    r = rkv[0]
    wkv = _wkv_bidir(r, v, kk, lw, kd, asg, geo.b, geo.l)
    pv2 = _pad_to(jnp.stack([r_k, ln_w, ln_b]), 0, SUBLANES)
    y = _rwkv_readout(geo, wkv, r, kd, v, g, pv2)
    xs = _matmul_res(geo, y, w_o.astype(BF16), xs, mods, 2)
    return xs, v


def _rope_tables(geo):
    t = geo.t
    pos = jnp.arange(t)
    row = (pos // GRID_W).astype(F32)
    col = (pos % GRID_W).astype(F32)
    n_freq = QK_ROPE // 4
    inv_freq = ROPE_THETA ** (-jnp.arange(n_freq, dtype=F32) / n_freq)
    ang = jnp.concatenate([row[:, None] * inv_freq, col[:, None] * inv_freq], axis=-1)
    cos, sin = jnp.cos(ang), jnp.sin(ang)
    pad = LANES - QK_ROPE
    cos_l = jnp.concatenate([cos, cos, jnp.ones((t, pad), F32)], axis=1)
    sin_l = jnp.concatenate([-sin, sin, jnp.zeros((t, pad), F32)], axis=1)
    cos_c = jnp.ones((geo.l, LANES), F32)
    sin_c = jnp.zeros((geo.l, LANES), F32)
    return jnp.concatenate([cos_c, cos_l], axis=0), jnp.concatenate([sin_c, sin_l], axis=0)


def _mla_layer(geo, xs, mods, norm1, rope_t, w_in, q_norm, kv_norm, w_uq, w_ukv, g_q, g_k, w_o):
    q_lora, kv_lora = q_norm.shape[0], kv_norm.shape[0]
    zw = q_lora + kv_lora + LANES
    z = _mod_matmul(geo, xs, norm1, mods, _pad_to(w_in, 1, zw).astype(BF16), 0, False)
    qk = QK_NOPE + QK_ROPE
    wuq_p = _pad_to(w_uq.reshape(q_lora, MLA_HEADS, qk), 2, QK_PAD).reshape(q_lora, -1)
    gq_p = _pad_to(g_q, 0, QK_PAD).reshape(1, QK_PAD)
    gk_p = _pad_to(g_k, 0, QK_PAD).reshape(1, QK_PAD)
    q, k, v = _mla_qkv(geo, z, q_norm, kv_norm, wuq_p.astype(BF16), w_ukv.astype(BF16),
                       gq_p, gk_p, *rope_t)
    o = _mla_attention(geo, q, k, v)
    return _matmul_res(geo, o, w_o.astype(BF16), xs, mods, 2)


def _peer_layer(geo, geo_g, xs, mods, norm2, w_q, q_norm, keys, u, v):
    qp, h2 = _mod_matmul(geo, xs, norm2, mods, w_q.astype(BF16), 1, True)
    eidx_t, gate_t = _peer_select(qp, q_norm, keys, geo.tm)
    return _peer_gather(geo_g, eidx_t, gate_t, h2, xs, mods, _pack_expert_table(u, v))


def kernel(x, c, ctx, c_ctx, w_ada, b_ada, norm1, norm2, rw_mix, rw_wrkv, rw_wo, rw_w0, rw_w1, rw_w2, rw_a0, rw_a1, rw_a2, rw_v0, rw_v1, rw_v2, rw_g1, rw_g2, rw_kk, rw_ka, rw_rk, rw_lnw, rw_lnb, mla_win, mla_qnorm, mla_kvnorm, mla_wuq, mla_wukv, mla_gq, mla_gk, mla_wo, peer_wq, peer_qnorm, peer_keys, peer_u, peer_v):
    b, t, d = x.shape
    l = ctx.shape[1]
    depth = w_ada.shape[0]
    geo = _Geom(b, l, t, d, min(256, l))
    geo_g = _Geom(b, l, t, d, min(128, l))
    cond8 = _pad_to(jnp.concatenate([c, c_ctx[None, :]], axis=0), 0, SUBLANES)
    ada = _adaln(cond8, w_ada, b_ada).reshape(depth, SUBLANES, 6, d)
    mods_all = jnp.stack([jnp.broadcast_to(ada[:, b:b + 1], (depth, b, 6, d)), ada[:, 0:b]], axis=2)
    xs = jnp.concatenate([ctx, x], axis=1).reshape(b * (l + t), d)
    rope_t = _rope_tables(geo)
    vfirst = None
    for i in range(depth):
        j = i // 2
        mods = mods_all[i]
        if i % 2 == 0:
            vl = None if j == 0 else (rw_v0[j - 1], rw_v1[j - 1], rw_v2[j - 1])
            xs, vcur = _rwkv_layer(geo, xs, mods, norm1[i], rw_mix[j], rw_wrkv[j], rw_wo[j],
                                   rw_w0[j], rw_w1[j], rw_w2[j], rw_a0[j], rw_a1[j], rw_a2[j], vl,
                                   rw_g1[j], rw_g2[j], rw_kk[j], rw_ka[j], rw_rk[j], rw_lnw[j],
                                   rw_lnb[j], vfirst)
            if j == 0:
                vfirst = vcur
        else:
            xs = _mla_layer(geo, xs, mods, norm1[i], rope_t, mla_win[j], mla_qnorm[j],
                            mla_kvnorm[j], mla_wuq[j], mla_wukv[j], mla_gq[j], mla_gk[j], mla_wo[j])
        xs = _peer_layer(geo, geo_g, xs, mods, norm2[i], peer_wq[i], peer_qnorm[i], peer_keys[i],
                         peer_u[i], peer_v[i])
    return xs.reshape(b, l + t, d)[:, l:, :]
```

```python
import functools
import math

import jax
import jax.numpy as jnp
from jax import lax
from jax.experimental import pallas as pl
from jax.experimental.pallas import tpu as pltpu

F32 = jnp.float32
BF16 = jnp.bfloat16

EPS = 1e-6
GN_EPS = 64e-5
RW_HEAD = 64
WKV_CHUNK = 64
MLA_HEADS = 16
QK_NOPE = 128
QK_ROPE = 64
V_HEAD = 128
QK_PAD = 256
ROPE_THETA = 10000.0
GRID_W = 64
ATTN_SCALE = (QK_NOPE + QK_ROPE) ** -0.5
PEER_HEADS = 8
N_KEYS = 128
PEER_TOPK = 16
D_KEY = 256
LORA_PAD = 128

LANES = 128
SUBLANES = 8
VMEM_LIMIT = 56 * 1024 * 1024


def _cparams(*sem):
    return pltpu.CompilerParams(dimension_semantics=sem, vmem_limit_bytes=VMEM_LIMIT)


def _dot(a, b):
    return jnp.dot(a, b, preferred_element_type=F32)


def _dot_nt(a, b):
    return lax.dot_general(a, b, (((1,), (1,)), ((), ())), preferred_element_type=F32)


def _split2(x):
    hi = x.astype(BF16)
    lo = (x - hi.astype(F32)).astype(BF16)
    return hi, lo


def _split3(x):
    hi = x.astype(BF16)
    r1 = x - hi.astype(F32)
    mid = r1.astype(BF16)
    lo = (r1 - mid.astype(F32)).astype(BF16)
    return hi, mid, lo


def _dot3(a, b):
    ah, al = _split2(a)
    bh, bl = _split2(b)
    return _dot(ah, bh) + (_dot(ah, bl) + _dot(al, bh))


def _dot3_nt(a, b):
    ah, al = _split2(a)
    bh, bl = _split2(b)
    return _dot_nt(ah, bh) + (_dot_nt(ah, bl) + _dot_nt(al, bh))


def _dot_exact_lhs(sel, x):
    hi, mid, lo = _split3(x)
    return _dot(sel, hi) + (_dot(sel, mid) + _dot(sel, lo))


def _modulate(x, g, shift, scale):
    ms = jnp.mean(x * x, axis=-1, keepdims=True)
    return (x * lax.rsqrt(ms + EPS) * g) * (1.0 + scale) + shift


def _sigmoid(x):
    return 1.0 / (1.0 + jnp.exp(-x))


def _softplus(y):
    return jnp.maximum(y, 0.0) + jnp.log(1.0 + jnp.exp(-jnp.abs(y)))


def _erf(x):
    return lax.erf(x)


def _gelu(x):
    return 0.5 * x * (1.0 + _erf(x * (2.0 ** -0.5)))


def _ada_kernel(s_ref, w_ref, b_ref, o_ref):
    s = s_ref[...]
    s = s * _sigmoid(s)
    o_ref[...] = _dot3(s, w_ref[...]) + b_ref[...]


def _adaln(cond8, w_ada, b_ada):
    depth, d, n = w_ada.shape
    tn = 1024
    return pl.pallas_call(
        _ada_kernel,
        grid=(depth, n // tn),
        in_specs=[
            pl.BlockSpec((SUBLANES, d), lambda l, j: (0, 0)),
            pl.BlockSpec((None, d, tn), lambda l, j: (l, 0, j)),
            pl.BlockSpec((None, 1, tn), lambda l, j: (l, 0, j)),
        ],
        out_specs=pl.BlockSpec((None, SUBLANES, tn), lambda l, j: (l, 0, j)),
        out_shape=jax.ShapeDtypeStruct((depth, SUBLANES, n), F32),
        compiler_params=_cparams("parallel", "parallel"),
        name="adaln",
    )(cond8, w_ada, b_ada.reshape(depth, 1, n))


class _Geom:
    def __init__(self, batch, ctx_len, seq_len, d_model, tm):
        self.b, self.l, self.t, self.d = batch, ctx_len, seq_len, d_model
        self.s = ctx_len + seq_len
        self.n = batch * self.s
        self.tm = tm
        assert ctx_len % tm == 0 and seq_len % tm == 0
        self.tpb = self.s // tm
        self.nct = ctx_len // tm
        self.ntiles = self.n // tm

    def mod_spec(self, nlead=0):
        tpb, nct = self.tpb, self.nct

        def imap(*ids):
            i = ids[nlead]
            return (i // tpb, ((i % tpb) >= nct).astype(jnp.int32), 0, 0)

        return pl.BlockSpec((None, None, 6, self.d), imap)


def _modmm_kernel(x_ref, g_ref, mod_ref, w_ref, o_ref, *h_ref, which):
    h = _modulate(x_ref[...], g_ref[...], mod_ref[3 * which:3 * which + 1, :],
                  mod_ref[3 * which + 1:3 * which + 2, :])
    if h_ref:
        h_ref[0][...] = h
    o_ref[...] = _dot(h.astype(BF16), w_ref[...])


def _mod_matmul(geo, x, g, mods, w, which, emit_h):
    n, d = x.shape
    nn = w.shape[1]
    tm = geo.tm
    out_shape = [jax.ShapeDtypeStruct((n, nn), F32)]
    out_specs = [pl.BlockSpec((tm, nn), lambda i: (i, 0))]
    if emit_h:
        out_shape.append(jax.ShapeDtypeStruct((n, d), F32))
        out_specs.append(pl.BlockSpec((tm, d), lambda i: (i, 0)))
    res = pl.pallas_call(
        functools.partial(_modmm_kernel, which=which),
        grid=(geo.ntiles,),
        in_specs=[
            pl.BlockSpec((tm, d), lambda i: (i, 0)),
            pl.BlockSpec((1, d), lambda i: (0, 0)),
            geo.mod_spec(),
            pl.BlockSpec((d, nn), lambda i: (0, 0)),
        ],
        out_specs=out_specs,
        out_shape=out_shape,
        compiler_params=_cparams("parallel"),
        name="mod_matmul",
    )(x, g.reshape(1, d), mods, w)
    return res if emit_h else res[0]


def _mmres_kernel(y_ref, w_ref, x_ref, mod_ref, o_ref, *, gidx):
    acc = _dot(y_ref[...], w_ref[...])
    o_ref[...] = x_ref[...] + mod_ref[gidx:gidx + 1, :] * acc


def _matmul_res(geo, y, w, x, mods, gidx):
    n, k = y.shape
    d = x.shape[1]
    tm = geo.tm
    return pl.pallas_call(
        functools.partial(_mmres_kernel, gidx=gidx),
        grid=(geo.ntiles,),
        in_specs=[
            pl.BlockSpec((tm, k), lambda i: (i, 0)),
            pl.BlockSpec((k, d), lambda i: (0, 0)),
            pl.BlockSpec((tm, d), lambda i: (i, 0)),
            geo.mod_spec(),
        ],
        out_specs=pl.BlockSpec((tm, d), lambda i: (i, 0)),
        out_shape=jax.ShapeDtypeStruct((n, d), F32),
        compiler_params=_cparams("parallel"),
        name="matmul_res",
    )(y, w, x, mods)


def _bmm_kernel(x_ref, w_ref, o_ref, *, acts):
    j = pl.program_id(0)
    y = _dot(x_ref[...], w_ref[...])
    out = y
    for jj, a in enumerate(acts):
        if a == "tanh":
            out = jnp.where(j == jj, jnp.tanh(y), out)
        elif a == "sigmoid":
            out = jnp.where(j == jj, _sigmoid(y), out)
    o_ref[...] = out.astype(o_ref.dtype)


def _bmm(x3, w3, src, acts, out_dtype, tm):
    _, n, k = x3.shape
    nj, _, nn = w3.shape
    src = tuple(src)

    def xmap(j, i):
        idx = jnp.int32(src[0])
        for jj in range(1, nj):
            idx = jnp.where(j == jj, jnp.int32(src[jj]), idx)
        return (idx, i, 0)

    return pl.pallas_call(
        functools.partial(_bmm_kernel, acts=tuple(acts)),
        grid=(nj, n // tm),
        in_specs=[
            pl.BlockSpec((None, tm, k), xmap),
            pl.BlockSpec((None, k, nn), lambda j, i: (j, 0, 0)),
        ],
        out_specs=pl.BlockSpec((None, tm, nn), lambda j, i: (j, i, 0)),
        out_shape=jax.ShapeDtypeStruct((nj, n, nn), out_dtype),
        compiler_params=_cparams("parallel", "parallel"),
        name="bmm",
    )(x3, w3)


def _rwmix_kernel(x_ref, xp_ref, xn_ref, g_ref, mod_ref, mix_ref, o_ref, *, tpb, nct):
    i = pl.program_id(0)
    tm = x_ref.shape[0]
    g = g_ref[...]
    shift = mod_ref[0:1, :]
    scale = mod_ref[1:2, :]
    h = _modulate(x_ref[...], g, shift, scale)
    hp = _modulate(xp_ref[...], g, shift, scale)[SUBLANES - 1:SUBLANES, :]
    hn = _modulate(xn_ref[...], g, shift, scale)[0:1, :]
    it = i % tpb
    first = jnp.logical_or(it == 0, it == nct)
    last = jnp.logical_or(it == nct - 1, it == tpb - 1)
    hp = jnp.where(first, 0.0, hp)
    hn = jnp.where(last, 0.0, hn)
    rows = lax.broadcasted_iota(jnp.int32, h.shape, 0)
    prev = jnp.where(rows == 0, hp, pltpu.roll(h, 1, axis=0))
    nxt = jnp.where(rows == tm - 1, hn, pltpu.roll(h, tm - 1, axis=0))
    xx = 0.5 * (prev + nxt) - h
    for m in range(6):
        o_ref[m] = (h + xx * mix_ref[m:m + 1, :]).astype(BF16)


def _rwkv_mix(geo, x, g, mods, mix):
    n, d = x.shape
    tm = geo.tm
    r8 = tm // SUBLANES
    nblk8 = n // SUBLANES
    return pl.pallas_call(
        functools.partial(_rwmix_kernel, tpb=geo.tpb, nct=geo.nct),
        grid=(geo.ntiles,),
        in_specs=[
            pl.BlockSpec((tm, d), lambda i: (i, 0)),
            pl.BlockSpec((SUBLANES, d), lambda i: (jnp.maximum(i * r8 - 1, 0), 0)),
            pl.BlockSpec((SUBLANES, d), lambda i: (jnp.minimum((i + 1) * r8, nblk8 - 1), 0)),
            pl.BlockSpec((1, d), lambda i: (0, 0)),
            geo.mod_spec(),
            pl.BlockSpec((6, d), lambda i: (0, 0)),
        ],
        out_specs=pl.BlockSpec((6, tm, d), lambda i: (0, i, 0)),
        out_shape=jax.ShapeDtypeStruct((6, n, d), BF16),
        compiler_params=_cparams("parallel"),
        name="rwkv_mix",
    )(x, x, x, g.reshape(1, d), mods, mix)


def _head_sum(x, bd):
    hi, lo = _split2(x)
    return _dot(hi, bd) + _dot(lo, bd)


def _rwfeat_kernel(k_ref, v_ref, tl_ref, w2_ref, a2_ref, g2_ref, v2_ref, pv_ref, bd_ref,
                   *rest, has_vlora):
    if has_vlora:
        vf_ref, lw_ref, kd_ref, as_ref, kk_ref, g_ref, vo_ref = rest
    else:
        lw_ref, kd_ref, as_ref, kk_ref, g_ref = rest
    k = k_ref[...]
    tw = tl_ref[0]
    ta = tl_ref[1]
    tg = tl_ref[2]
    w0 = pv_ref[0:2, :]
    a0 = pv_ref[2:4, :]
    k_k = pv_ref[5:6, :]
    k_a = pv_ref[6:7, :]
    for z in range(2):
        sl = slice(LORA_PAD * z, LORA_PAD * (z + 1))
        lora_w = _dot(tw[:, sl], w2_ref[z])
        w = -_softplus(-(w0[z:z + 1, :] + lora_w)) - 0.5
        lw_ref[z] = -jnp.exp(w)
        a_sig = _sigmoid(a0[z:z + 1, :] + _dot(ta[:, sl], a2_ref[z]))
        as_ref[z] = a_sig
        kd_ref[z] = k * (1.0 + (a_sig - 1.0) * k_a)
    g_ref[...] = _dot(tg, g2_ref[...])
    kkr = k * k_k
    ss = _head_sum(kkr * kkr, bd_ref[...])
    kk_ref[...] = kkr * lax.rsqrt(ss + 1e-12)
    if has_vlora:
        v = v_ref[...]
        tv = tl_ref[3]
        gate = _sigmoid(pv_ref[4:5, :] + _dot(tv[:, 0:LORA_PAD], v2_ref[...]))
        vo_ref[...] = v + (vf_ref[...] - v) * gate


def _rwkv_feat(geo, rkv, tl, w2p, a2p, g2, v2p, pvec, vfirst):
    _, n, d = rkv.shape
    tm, tc = geo.tm, 512
    has_vlora = vfirst is not None
    lt = tl.shape[2]
    ii = lax.broadcasted_iota(jnp.int32, (tc, tc), 0) // RW_HEAD
    jj = lax.broadcasted_iota(jnp.int32, (tc, tc), 1) // RW_HEAD
    bd = (ii == jj).astype(BF16)
    row = lambda i, j: (i, j)
    in_specs = [
        pl.BlockSpec((None, tm, tc), lambda i, j: (1, i, j)),
        pl.BlockSpec((None, tm, tc), lambda i, j: (2, i, j)),
        pl.BlockSpec((4, tm, lt), lambda i, j: (0, i, 0)),
        pl.BlockSpec((2, LORA_PAD, tc), lambda i, j: (0, 0, j)),
        pl.BlockSpec((2, LORA_PAD, tc), lambda i, j: (0, 0, j)),
        pl.BlockSpec((lt, tc), lambda i, j: (0, j)),
        pl.BlockSpec((LORA_PAD, tc), lambda i, j: (0, j)),
        pl.BlockSpec((SUBLANES, tc), lambda i, j: (0, j)),
        pl.BlockSpec((tc, tc), lambda i, j: (0, 0)),
    ]
    args = [rkv, rkv, tl, w2p, a2p, g2, v2p, pvec, bd]
    dir_spec = pl.BlockSpec((2, tm, tc), lambda i, j: (0, i, j))
    out_specs = [dir_spec, dir_spec, dir_spec, pl.BlockSpec((tm, tc), row), pl.BlockSpec((tm, tc), row)]
    out_shape = [jax.ShapeDtypeStruct((2, n, d), F32)] * 3 + [jax.ShapeDtypeStruct((n, d), F32)] * 2
    if has_vlora:
        in_specs.append(pl.BlockSpec((tm, tc), row))
        args.append(vfirst)
        out_specs.append(pl.BlockSpec((tm, tc), row))
        out_shape.append(jax.ShapeDtypeStruct((n, d), F32))
    res = pl.pallas_call(
        functools.partial(_rwfeat_kernel, has_vlora=has_vlora),
        grid=(geo.ntiles, d // tc),
        in_specs=in_specs,
        out_specs=out_specs,
        out_shape=out_shape,
        compiler_params=_cparams("parallel", "parallel"),
        name="rwkv_feat",
    )(*args)
    if has_vlora:
        lw, kd, asg, kk, g, v = res
    else:
        lw, kd, asg, kk, g = res
        v = rkv[2]
    return lw, kd, asg, kk, g, v


def _wkv_chunk_kernel(r_ref, v_ref, kk_ref, lw_ref, kd_ref, as_ref,
                      m_ref, ga_ref, rq_ref, o0_ref, pc_ref, *, npairs):
    c = WKV_CHUNK
    c2 = 2 * c
    sgn = 1 - 2 * pl.program_id(0)
    ri = lax.broadcasted_iota(jnp.int32, (c2, c2), 0)
    ci = lax.broadcasted_iota(jnp.int32, (c2, c2), 1)
    same = (ri >= c) == (ci >= c)
    tt = jnp.where(ri >= c, ri - c, ri)
    ss = jnp.where(ci >= c, ci - c, ci)
    earlier = (ss - tt) * sgn < 0
    strict = jnp.logical_and(same, earlier)
    incl = jnp.logical_and(same, jnp.logical_or(earlier, ss == tt))
    eye = (ri == ci).astype(F32)
    r64 = lax.broadcasted_iota(jnp.int32, (c, c), 0)
    c64 = lax.broadcasted_iota(jnp.int32, (c, c), 1)
    ltri = jnp.where((c64 - r64) * sgn <= 0, 1.0, 0.0).astype(BF16)
    head0 = lax.broadcasted_iota(jnp.int32, (c, LANES), 1) < RW_HEAD
    pairs = range(npairs)

    def stack(x):
        return jnp.concatenate([jnp.where(head0, x, 0.0), jnp.where(head0, 0.0, x)], axis=0)

    def dup(x):
        return jnp.concatenate([x, x], axis=0)

    def bf(x):
        return x.astype(BF16)

    lhs, rhs, a2, bp2, kp2, vst, r2 = [], [], [], [], [], [], []
    for p in pairs:
        sl = slice(LANES * p, LANES * (p + 1))
        lw = lw_ref[:, sl]
        cum = _dot_exact_lhs(ltri, lw)
        tot = jnp.sum(lw, axis=0, keepdims=True)
        p_inv = jnp.exp(-cum)
        p_end = jnp.exp(tot - cum)
        kk = kk_ref[:, sl]
        b = kk * as_ref[:, sl]
        kd = kd_ref[:, sl]
        a2p = stack(-kk * jnp.exp(cum - lw))
        r2p = stack(r_ref[:, sl] * jnp.exp(cum))
        a2.append(bf(a2p))
        r2.append(r2p)
        lhs.append(jnp.concatenate([a2[p], bf(r2p)], axis=0))
        rhs.append(jnp.concatenate([dup(bf(b * p_inv)), dup(bf(kd * p_inv))], axis=0))
        bp2.append(bf(stack(b * p_end)))
        kp2.append(bf(stack(kd * p_end)))
        vst.append(bf(stack(v_ref[:, sl])))
        pc_ref[:, sl] = jnp.broadcast_to(jnp.exp(tot), (SUBLANES, LANES))
    gram = [_dot_nt(lhs[p], rhs[p]) for p in pairs]
    nab = [jnp.where(strict, gram[p][0:c2, 0:c2], 0.0) for p in pairs]
    nrb = [bf(jnp.where(incl, gram[p][c2:2 * c2, 0:c2], 0.0)) for p in pairs]
    nk = [bf(jnp.concatenate([jnp.where(strict, gram[p][0:c2, c2:2 * c2], 0.0),
                              jnp.where(incl, gram[p][c2:2 * c2, c2:2 * c2], 0.0)], axis=0))
          for p in pairs]
    xv = [_dot(nk[p], vst[p]) for p in pairs]
    tinv = [eye + nab[p] for p in pairs]
    npow = [bf(nab[p]) for p in pairs]
    for _ in range(int(math.log2(c)) - 1):
        npow = [bf(_dot(npow[p], npow[p])) for p in pairs]
        tinv = [tinv[p] + _dot(bf(tinv[p]), npow[p]) for p in pairs]
    y = [_dot(bf(tinv[p]), jnp.concatenate([a2[p], bf(xv[p][0:c2, :])], axis=1)) for p in pairs]
    yb = [bf(y[p]) for p in pairs]
    z = [_dot(nrb[p], yb[p]) for p in pairs]
    mg = [_dot(bf(y[p].T), bp2[p]) for p in pairs]
    vk = [_dot(bf(vst[p].astype(F32).T), kp2[p]) for p in pairs]
    for p in pairs:
        sl = slice(LANES * p, LANES * (p + 1))
        m_ref[:, sl] = mg[p][0:c2, :]
        ga_ref[:, sl] = mg[p][c2:2 * c2, :] + vk[p]
        rq_ref[:, sl] = r2[p] + z[p][:, 0:c2]
        o0 = z[p][:, c2:2 * c2] + xv[p][c2:2 * c2, :]
        o0_ref[:, sl] = o0[0:c, :] + o0[c:c2, :]


def _wkv_scan_kernel(m_ref, ga_ref, rq_ref, o0_ref, pc_ref, o_ref, g_scr, *, npairs):
    c = WKV_CHUNK

    @pl.when(pl.program_id(2) == 0)
    def _():
        g_scr[...] = jnp.zeros_like(g_scr)

    for p in range(npairs):
        sl = slice(LANES * p, LANES * (p + 1))
        g = g_scr[p]
        o_st = _dot3_nt(rq_ref[:, sl], g)
        o_ref[:, sl] = o_st[0:c, :] + o_st[c:2 * c, :] + o0_ref[:, sl]
        g_scr[p] = g * pc_ref[0:1, sl] + _dot3(g, m_ref[:, sl]) + ga_ref[:, sl]


WKV_CHUNK_LANES = 1024


def _wkv_bidir(r, v, kk, lw, kd, asg, batch, ctx_len):
    n, d = r.shape
    c = WKV_CHUNK
    s = n // batch
    ncs = s // c
    ncc = ctx_len // c
    nch = n // c
    lanes = min(WKV_CHUNK_LANES, d)
    ngrp = d // lanes
    shared = pl.BlockSpec((c, lanes), lambda z, i, j: (i, j))
    perdir = pl.BlockSpec((None, c, lanes), lambda z, i, j: (z, i, j))
    big = pl.BlockSpec((None, 2 * c, lanes), lambda z, i, j: (z, i, j))
    m_, ga_, rq_, o0_, pc_ = pl.pallas_call(
        functools.partial(_wkv_chunk_kernel, npairs=lanes // LANES),
        grid=(2, nch, ngrp),
        in_specs=[shared, shared, shared, perdir, perdir, perdir],
        out_specs=[big, big, big, perdir,
                   pl.BlockSpec((None, SUBLANES, lanes), lambda z, i, j: (z, i, j))],
        out_shape=[jax.ShapeDtypeStruct((2, nch * 2 * c, d), F32)] * 3
        + [jax.ShapeDtypeStruct((2, n, d), F32),
           jax.ShapeDtypeStruct((2, nch * SUBLANES, d), F32)],
        compiler_params=_cparams("parallel", "parallel", "parallel"),
        name="wkv_chunk",
    )(r, v, kk, lw, kd, asg)

    def cmap(z, b, cc):
        back = jnp.where(cc < ncc, ncc - 1 - cc, ncs + ncc - 1 - cc)
        return (z, b * ncs + jnp.where(z == 0, cc, back), 0)

    out = pl.pallas_call(
        functools.partial(_wkv_scan_kernel, npairs=d // LANES),
        grid=(2, batch, ncs),
        in_specs=[
            pl.BlockSpec((None, 2 * c, d), cmap),
            pl.BlockSpec((None, 2 * c, d), cmap),
            pl.BlockSpec((None, 2 * c, d), cmap),
            pl.BlockSpec((None, c, d), cmap),
            pl.BlockSpec((None, SUBLANES, d), cmap),
        ],
        out_specs=pl.BlockSpec((None, c, d), cmap),
        out_shape=jax.ShapeDtypeStruct((2, n, d), F32),
        scratch_shapes=[pltpu.VMEM((d // LANES, 2 * c, 2 * c), F32)],
        compiler_params=_cparams("parallel", "parallel", "arbitrary"),
        name="wkv_scan",
    )(m_, ga_, rq_, o0_, pc_)
    return out


def _rwread_kernel(o_ref, r_ref, kd_ref, v_ref, g_ref, pv_ref, bd_ref, y_ref):
    bd = bd_ref[...]
    wkv = o_ref[0] + o_ref[1]
    inv = 1.0 / RW_HEAD
    mu = _head_sum(wkv, bd) * inv
    dev = wkv - mu
    var = _head_sum(dev * dev, bd) * inv
    y = dev * lax.rsqrt(var + GN_EPS) * pv_ref[1:2, :] + pv_ref[2:3, :]
    rk = r_ref[...] * (kd_ref[0] + kd_ref[1]) * pv_ref[0:1, :]
    y = y + _head_sum(rk, bd) * v_ref[...]
    y_ref[...] = (y * g_ref[...]).astype(BF16)


def _rwkv_readout(geo, wkv, r, kd, v, g, pvec):
    n, d = r.shape
    tm, tc = geo.tm, 512
    ii = lax.broadcasted_iota(jnp.int32, (tc, tc), 0) // RW_HEAD
    jj = lax.broadcasted_iota(jnp.int32, (tc, tc), 1) // RW_HEAD
    bd = (ii == jj).astype(BF16)
    row = pl.BlockSpec((tm, tc), lambda i, j: (i, j))
    dirs = pl.BlockSpec((2, tm, tc), lambda i, j: (0, i, j))
    return pl.pallas_call(
        _rwread_kernel,
        grid=(geo.ntiles, d // tc),
        in_specs=[dirs, row, dirs, row, row,
                  pl.BlockSpec((SUBLANES, tc), lambda i, j: (0, j)),
                  pl.BlockSpec((tc, tc), lambda i, j: (0, 0))],
        out_specs=row,
        out_shape=jax.ShapeDtypeStruct((n, d), BF16),
        compiler_params=_cparams("parallel", "parallel"),
        name="rwkv_readout",
    )(wkv, r, kd, v, g, pvec, bd)


def _mlaqkv_kernel(z_ref, qn_ref, kvn_ref, wuq_ref, wukv_ref, gq_ref, gk_ref, cos_ref, sin_ref,
                   q_ref, k_ref, v_ref, *, q_lora, kv_lora):
    z = z_ref[...]
    cq = z[:, 0:q_lora]
    ckv = z[:, q_lora:q_lora + kv_lora]
    krot = z[:, q_lora + kv_lora:q_lora + kv_lora + LANES]

    def rms(x, g):
        ms = jnp.mean(x * x, axis=-1, keepdims=True)
        return x * lax.rsqrt(ms + EPS) * g

    qf = _dot(rms(cq, qn_ref[...]).astype(BF16), wuq_ref[...])
    kvf = _dot(rms(ckv, kvn_ref[...]).astype(BF16), wukv_ref[...])
    cos = cos_ref[...]
    sin = sin_ref[...]
    half = QK_ROPE // 2
    lane = lax.broadcasted_iota(jnp.int32, cos.shape, 1)

    def rope(x):
        up = pltpu.roll(x, LANES - half, axis=1)
        dn = pltpu.roll(x, half, axis=1)
        return x * cos + jnp.where(lane < half, up, dn) * sin

    inv_w = 1.0 / (QK_NOPE + QK_ROPE)
    gq = gq_ref[...]
    gk = gk_ref[...]
    kr_ss = jnp.sum(krot * krot, axis=-1, keepdims=True)
    for h in range(MLA_HEADS):
        o = QK_PAD * h
        qh = qf[:, o:o + QK_PAD]
        rs = lax.rsqrt(jnp.sum(qh * qh, axis=-1, keepdims=True) * inv_w + EPS)
        qn = qh * rs * gq * (ATTN_SCALE * math.log2(math.e))
        q_ref[:, o:o + QK_NOPE] = qn[:, 0:QK_NOPE].astype(BF16)
        q_ref[:, o + QK_NOPE:o + QK_PAD] = rope(qn[:, QK_NOPE:QK_PAD]).astype(BF16)
        kn = kvf[:, o:o + QK_NOPE]
        rsk = lax.rsqrt((jnp.sum(kn * kn, axis=-1, keepdims=True) + kr_ss) * inv_w + EPS)
        k_ref[:, o:o + QK_NOPE] = (kn * rsk * gk[:, 0:QK_NOPE]).astype(BF16)
        k_ref[:, o + QK_NOPE:o + QK_PAD] = rope(krot * rsk * gk[:, QK_NOPE:QK_PAD]).astype(BF16)
        v_ref[:, V_HEAD * h:V_HEAD * (h + 1)] = kvf[:, o + QK_NOPE:o + QK_PAD].astype(BF16)


def _mla_qkv(geo, z, qn, kvn, wuq_p, wukv, gq_p, gk_p, cos_t, sin_t):
    n, zw = z.shape
    tm = geo.tm
    q_lora, kv_lora = qn.shape[0], kvn.shape[0]
    hq = MLA_HEADS * QK_PAD
    tpb = geo.tpb
    full = lambda i: (0, 0)
    rowmap = lambda i: (i, 0)
    return pl.pallas_call(
        functools.partial(_mlaqkv_kernel, q_lora=q_lora, kv_lora=kv_lora),
        grid=(geo.ntiles,),
        in_specs=[
            pl.BlockSpec((tm, zw), rowmap),
            pl.BlockSpec((1, q_lora), full),
            pl.BlockSpec((1, kv_lora), full),
            pl.BlockSpec((q_lora, hq), full),
            pl.BlockSpec((kv_lora, hq), full),
            pl.BlockSpec((1, QK_PAD), full),
            pl.BlockSpec((1, QK_PAD), full),
            pl.BlockSpec((tm, LANES), lambda i: (i % tpb, 0)),
            pl.BlockSpec((tm, LANES), lambda i: (i % tpb, 0)),
        ],
        out_specs=[pl.BlockSpec((tm, hq), rowmap), pl.BlockSpec((tm, hq), rowmap),
                   pl.BlockSpec((tm, MLA_HEADS * V_HEAD), rowmap)],
        out_shape=[jax.ShapeDtypeStruct((n, hq), BF16), jax.ShapeDtypeStruct((n, hq), BF16),
                   jax.ShapeDtypeStruct((n, MLA_HEADS * V_HEAD), BF16)],
        compiler_params=_cparams("parallel"),
        name="mla_qkv",
    )(z, qn.reshape(1, -1), kvn.reshape(1, -1), wuq_p, wukv, gq_p, gk_p, cos_t, sin_t)


ATTN_SUBTILES = 2


def _attn_kernel(q_ref, k_ref, v_ref, o_ref):
    k = k_ref[...]
    v = v_ref[...]
    rows = q_ref.shape[0] // ATTN_SUBTILES
    subs = range(ATTN_SUBTILES)
    s = [_dot_nt(q_ref[rows * i:rows * (i + 1), :], k) for i in subs]
    m = [jnp.max(s[i], axis=-1, keepdims=True) for i in subs]
    p = [jnp.exp2(s[i] - m[i]) for i in subs]
    l = [jnp.sum(p[i], axis=-1, keepdims=True) for i in subs]
    for i in subs:
        o_ref[rows * i:rows * (i + 1), :] = (_dot(p[i].astype(BF16), v) / l[i]).astype(BF16)


def _attn_call(q3, k3, v3, tq, nkeys):
    b, sq, _ = q3.shape
    return pl.pallas_call(
        _attn_kernel,
        grid=(b, MLA_HEADS, sq // tq),
        in_specs=[
            pl.BlockSpec((None, tq, QK_PAD), lambda bb, h, i: (bb, i, h)),
            pl.BlockSpec((None, nkeys, QK_PAD), lambda bb, h, i: (bb, 0, h)),
            pl.BlockSpec((None, nkeys, V_HEAD), lambda bb, h, i: (bb, 0, h)),
        ],
        out_specs=pl.BlockSpec((None, tq, V_HEAD), lambda bb, h, i: (bb, i, h)),
        out_shape=jax.ShapeDtypeStruct((b, sq, MLA_HEADS * V_HEAD), BF16),
        compiler_params=_cparams("parallel", "parallel", "parallel"),
        name="mla_attention",
    )(q3, k3, v3)


def _mla_attention(geo, q, k, v):
    b, s, l = geo.b, geo.s, geo.l
    q3 = q.reshape(b, s, MLA_HEADS * QK_PAD)
    k3 = k.reshape(b, s, MLA_HEADS * QK_PAD)
    v3 = v.reshape(b, s, MLA_HEADS * V_HEAD)
    o_ctx = _attn_call(q3[:, :l], k3, v3, geo.tm, l)
    tq_lat = 2 * geo.tm if geo.t % (2 * geo.tm) == 0 else geo.tm
    o_lat = _attn_call(q3[:, l:], k3, v3, tq_lat, s)
    return jnp.concatenate([o_ctx, o_lat], axis=1).reshape(b * s, MLA_HEADS * V_HEAD)


def _topk_rows(s, k, payload=None):
    rows = lax.broadcasted_iota(jnp.int32, s.shape, 0).astype(F32)
    big = float(s.shape[0])
    vals, idxs = [], []
    for _ in range(k):
        m = jnp.max(s, axis=0, keepdims=True)
        idx = jnp.min(jnp.where(s == m, rows, big), axis=0, keepdims=True)
        hit = rows == idx
        vals.append(m)
        if payload is None:
            idxs.append(idx)
        else:
            idxs.append(jnp.sum(jnp.where(hit, payload, 0.0), axis=0, keepdims=True))
        s = jnp.where(hit, -jnp.inf, s)
    return jnp.concatenate(vals, axis=0), jnp.concatenate(idxs, axis=0)


def _peersel_kernel(q_ref, qn_ref, keys_ref, e_ref, g_ref):
    q = q_ref[...]
    ms = jnp.mean(q * q, axis=-1, keepdims=True)
    qn = q * lax.rsqrt(ms + EPS) * qn_ref[...]
    half = D_KEY // 2
    s1 = _dot3_nt(keys_ref[0], qn[:, 0:half])
    s2 = _dot3_nt(keys_ref[1], qn[:, half:D_KEY])
    t1, i1 = _topk_rows(s1, PEER_TOPK)
    t2, i2 = _topk_rows(s2, PEER_TOPK)
    k = PEER_TOPK
    sub = lax.broadcasted_iota(jnp.int32, (SUBLANES, t1.shape[1]), 0)
    cand = [t1[0:1, :] + t2]
    cidx = [i1[0:1, :] * float(N_KEYS) + i2]
    for p in range(1, k // 2):
        live = sub < k // (p + 1)
        cand.append(jnp.where(live, t1[p:p + 1, :] + t2[0:SUBLANES, :], -jnp.inf))
        cidx.append(i1[p:p + 1, :] * float(N_KEYS) + i2[0:SUBLANES, :])
    cand.append(t1[k // 2:k, :] + t2[0:1, :])
    cidx.append(i1[k // 2:k, :] * float(N_KEYS) + i2[0:1, :])
    best, eidx = _topk_rows(jnp.concatenate(cand, axis=0), k, payload=jnp.concatenate(cidx, axis=0))
    ex = jnp.exp(best - jnp.max(best, axis=0, keepdims=True))
    g_ref[...] = ex / jnp.sum(ex, axis=0, keepdims=True)
    e_ref[...] = eidx.astype(jnp.int32)


def _peer_select(qp, q_norm, keys, tm):
    n = qp.shape[0]
    return pl.pallas_call(
        _peersel_kernel,
        grid=(n // tm, PEER_HEADS),
        in_specs=[
            pl.BlockSpec((tm, D_KEY), lambda i, h: (i, h)),
            pl.BlockSpec((1, D_KEY), lambda i, h: (0, 0)),
            pl.BlockSpec((2, N_KEYS, D_KEY // 2), lambda i, h: (0, 0, 0)),
        ],
        out_specs=[pl.BlockSpec((PEER_TOPK, tm), lambda i, h: (h, i)),
                   pl.BlockSpec((PEER_TOPK, tm), lambda i, h: (h, i))],
        out_shape=[jax.ShapeDtypeStruct((PEER_HEADS * PEER_TOPK, n), jnp.int32),
                   jax.ShapeDtypeStruct((PEER_HEADS * PEER_TOPK, n), F32)],
        compiler_params=_cparams("parallel", "parallel"),
        name="peer_select",
    )(qp, q_norm.reshape(1, D_KEY), keys)


GATHER_SLOTS = 4
SLAB_PAD = 1


def _pack_expert_table(u, v):
    ne, d = u.shape
    ub = lax.bitcast_convert_type(u.astype(BF16), jnp.uint16).astype(jnp.uint32)
    vb = lax.bitcast_convert_type(v.astype(BF16), jnp.uint16).astype(jnp.uint32)
    return ((vb << 16) | ub).reshape(ne, d // LANES, LANES)


def _peergather_kernel(idx_ref, idxn_ref, gate_ref, h_ref, x_ref, mod_ref, tab_ref, o_ref,
                       *scratch, tb):
    nsel = PEER_HEADS * PEER_TOPK
    nrow = h_ref.shape[1] // LANES
    pitch = nrow + SLAB_PAD
    ns = GATHER_SLOTS
    bufs, sem = scratch[:ns], scratch[ns]
    lane_t = lax.broadcasted_iota(jnp.int32, (nsel, tb), 1)
    g2 = mod_ref[5:6, :]
    step = pl.program_id(0)
    nsteps = pl.num_programs(0)

    def row_copy(ids_ref, t, j, slot):
        return pltpu.make_async_copy(
            tab_ref.at[ids_ref[t, j]],
            bufs[slot].at[pl.ds(j * pitch, nrow), :],
            sem.at[slot])

    def issue(ids_ref, t, slot):
        for j in range(nsel):
            row_copy(ids_ref, t, j, slot).start(priority=j % 2)

    def wait(t, slot):
        for j in range(nsel):
            row_copy(idx_ref, t, j, slot).wait()

    def packed(slot, s):
        return bufs[slot][pl.ds(s, nsel, stride=pitch), :]

    def compute(t, slot):
        hrow = h_ref[pl.ds(t, 1), :]
        acc = jnp.zeros((nsel, LANES), F32)
        for s in range(nrow):
            u = lax.bitcast_convert_type(packed(slot, s) << 16, F32)
            acc = acc + u * hrow[:, LANES * s:LANES * (s + 1)]
        dots = jnp.sum(acc, axis=-1, keepdims=True)
        gcol = jnp.sum(jnp.where(lane_t == t, gate_ref[...], 0.0), axis=-1, keepdims=True)
        coef = gcol * _gelu(dots)
        outs = []
        for s in range(nrow):
            vv = lax.bitcast_convert_type(packed(slot, s) & jnp.uint32(0xFFFF0000), F32)
            outs.append(jnp.sum(coef * vv, axis=0, keepdims=True))
        orow = jnp.concatenate(outs, axis=1)
        o_ref[pl.ds(t, 1), :] = x_ref[pl.ds(t, 1), :] + g2 * orow

    @pl.when(step == 0)
    def _():
        for s in range(ns - 1):
            issue(idx_ref, s, s)

    ngroups = tb // ns

    def body(g, carry):
        for s in range(ns):
            t = g * ns + s
            wait(t, s)
            issue(idx_ref, t + ns - 1, (s + ns - 1) % ns)
            compute(t, s)
        return carry

    lax.fori_loop(0, ngroups - 1, body, 0)
    for s in range(ns):
        t = (ngroups - 1) * ns + s
        wait(t, s)
        if s == 0:
            issue(idx_ref, tb - 1, ns - 1)
        else:
            @pl.when(step < nsteps - 1)
            def _():
                issue(idxn_ref, s - 1, s - 1)
        compute(t, s)


def _peer_gather(geo_g, eidx_t, gate_t, h2, x, mods, table):
    n, d = x.shape
    tb = geo_g.tm
    nsel = PEER_HEADS * PEER_TOPK
    pitch = d // LANES + SLAB_PAD
    nsteps = n // tb
    return pl.pallas_call(
        functools.partial(_peergather_kernel, tb=tb),
        grid=(nsteps,),
        in_specs=[
            pl.BlockSpec((tb, nsel), lambda i: (i, 0), memory_space=pltpu.SMEM),
            pl.BlockSpec((tb, nsel), lambda i: (jnp.minimum(i + 1, nsteps - 1), 0),
                         memory_space=pltpu.SMEM),
            pl.BlockSpec((nsel, tb), lambda i: (0, i)),
            pl.BlockSpec((tb, d), lambda i: (i, 0)),
            pl.BlockSpec((tb, d), lambda i: (i, 0)),
            geo_g.mod_spec(),
            pl.BlockSpec(memory_space=pl.ANY),
        ],
        out_specs=pl.BlockSpec((tb, d), lambda i: (i, 0)),
        out_shape=jax.ShapeDtypeStruct((n, d), F32),
        scratch_shapes=[pltpu.VMEM((nsel * pitch, LANES), jnp.uint32)] * GATHER_SLOTS
        + [pltpu.SemaphoreType.DMA((GATHER_SLOTS,))],
        compiler_params=_cparams("arbitrary"),
        name="peer_gather",
    )(eidx_t, eidx_t, gate_t, h2, x, mods, table)


def _pad_to(x, axis, size):
    pad = [(0, 0)] * x.ndim
    pad[axis] = (0, size - x.shape[axis])
    return jnp.pad(x, pad)


def _rwkv_layer(geo, xs, mods, norm1, mix, w_rkv, w_o, w0, w1, w2, a0, a1, a2, vl, g1, g2,
                k_k, k_a, r_k, ln_w, ln_b, vfirst):
    d = geo.d
    xm = _rwkv_mix(geo, xs, norm1, mods, mix)
    rkv = _bmm(xm, w_rkv.astype(BF16), (0, 2, 3), ("none",) * 3, F32, geo.tm)
    lt = g1.shape[1]
    w1c = jnp.concatenate([_pad_to(w1[0], 1, LORA_PAD), _pad_to(w1[1], 1, LORA_PAD)], axis=1)
    a1c = jnp.concatenate([_pad_to(a1[0], 1, LORA_PAD), _pad_to(a1[1], 1, LORA_PAD)], axis=1)
    if vl is None:
        v1p = jnp.zeros((d, lt), F32)
        v2p = jnp.zeros((LORA_PAD, d), F32)
        v0 = jnp.zeros((d,), F32)
    else:
        v0, v1, v2 = vl
        v1p = _pad_to(v1, 1, lt)
        v2p = _pad_to(v2, 0, LORA_PAD)
    wl1 = jnp.stack([_pad_to(w1c, 1, lt), _pad_to(a1c, 1, lt), g1, v1p]).astype(BF16)
    tl = _bmm(xm, wl1, (1, 4, 5, 3), ("tanh", "none", "sigmoid", "none"), BF16, geo.tm)
    w2p = jnp.stack([_pad_to(w2[0], 0, LORA_PAD), _pad_to(w2[1], 0, LORA_PAD)]).astype(BF16)
    a2p = jnp.stack([_pad_to(a2[0], 0, LORA_PAD), _pad_to(a2[1], 0, LORA_PAD)]).astype(BF16)
    pvec = jnp.stack([w0[0], w0[1], a0[0], a0[1], v0, k_k, k_a, jnp.zeros_like(k_k)])
    lw, kd, asg, kk, g, v = _rwkv_feat(geo, rkv, tl, w2p, a2p, g2.astype(BF16), v2p.astype(BF16),
                                      pvec, vfirst)
    r = rkv[0]
    wkv = _wkv_bidir(r, v, kk, lw, kd, asg, geo.b, geo.l)
    pv2 = _pad_to(jnp.stack([r_k, ln_w, ln_b]), 0, SUBLANES)
    y = _rwkv_readout(geo, wkv, r, kd, v, g, pv2)
    xs = _matmul_res(geo, y, w_o.astype(BF16), xs, mods, 2)
    return xs, v


def _rope_tables(geo):
    t = geo.t
    pos = jnp.arange(t)
    row = (pos // GRID_W).astype(F32)
    col = (pos % GRID_W).astype(F32)
    n_freq = QK_ROPE // 4
    inv_freq = ROPE_THETA ** (-jnp.arange(n_freq, dtype=F32) / n_freq)
    ang = jnp.concatenate([row[:, None] * inv_freq, col[:, None] * inv_freq], axis=-1)
    cos, sin = jnp.cos(ang), jnp.sin(ang)
    pad = LANES - QK_ROPE
    cos_l = jnp.concatenate([cos, cos, jnp.ones((t, pad), F32)], axis=1)
    sin_l = jnp.concatenate([-sin, sin, jnp.zeros((t, pad), F32)], axis=1)
    cos_c = jnp.ones((geo.l, LANES), F32)
    sin_c = jnp.zeros((geo.l, LANES), F32)
    return jnp.concatenate([cos_c, cos_l], axis=0), jnp.concatenate([sin_c, sin_l], axis=0)


def _mla_layer(geo, xs, mods, norm1, rope_t, w_in, q_norm, kv_norm, w_uq, w_ukv, g_q, g_k, w_o):
    q_lora, kv_lora = q_norm.shape[0], kv_norm.shape[0]
    zw = q_lora + kv_lora + LANES
    z = _mod_matmul(geo, xs, norm1, mods, _pad_to(w_in, 1, zw).astype(BF16), 0, False)
    qk = QK_NOPE + QK_ROPE
    wuq_p = _pad_to(w_uq.reshape(q_lora, MLA_HEADS, qk), 2, QK_PAD).reshape(q_lora, -1)
    gq_p = _pad_to(g_q, 0, QK_PAD).reshape(1, QK_PAD)
    gk_p = _pad_to(g_k, 0, QK_PAD).reshape(1, QK_PAD)
    q, k, v = _mla_qkv(geo, z, q_norm, kv_norm, wuq_p.astype(BF16), w_ukv.astype(BF16),
                       gq_p, gk_p, *rope_t)
    o = _mla_attention(geo, q, k, v)
    return _matmul_res(geo, o, w_o.astype(BF16), xs, mods, 2)


def _peer_layer(geo, geo_g, xs, mods, norm2, w_q, q_norm, keys, u, v):
    qp, h2 = _mod_matmul(geo, xs, norm2, mods, w_q.astype(BF16), 1, True)
    eidx_t, gate_t = _peer_select(qp, q_norm, keys, geo.tm)
    return _peer_gather(geo_g, eidx_t.T, gate_t, h2, xs, mods, _pack_expert_table(u, v))


def kernel(x, c, ctx, c_ctx, w_ada, b_ada, norm1, norm2, rw_mix, rw_wrkv, rw_wo, rw_w0, rw_w1, rw_w2, rw_a0, rw_a1, rw_a2, rw_v0, rw_v1, rw_v2, rw_g1, rw_g2, rw_kk, rw_ka, rw_rk, rw_lnw, rw_lnb, mla_win, mla_qnorm, mla_kvnorm, mla_wuq, mla_wukv, mla_gq, mla_gk, mla_wo, peer_wq, peer_qnorm, peer_keys, peer_u, peer_v):
    b, t, d = x.shape
    l = ctx.shape[1]
    depth = w_ada.shape[0]
    geo = _Geom(b, l, t, d, min(256, l))
    geo_g = _Geom(b, l, t, d, min(128, l))
    cond8 = _pad_to(jnp.concatenate([c, c_ctx[None, :]], axis=0), 0, SUBLANES)
    ada = _adaln(cond8, w_ada, b_ada).reshape(depth, SUBLANES, 6, d)
    mods_all = jnp.stack([jnp.broadcast_to(ada[:, b:b + 1], (depth, b, 6, d)), ada[:, 0:b]], axis=2)
    xs = jnp.concatenate([ctx, x], axis=1).reshape(b * (l + t), d)
    rope_t = _rope_tables(geo)
    vfirst = None
    for i in range(depth):
        j = i // 2
        mods = mods_all[i]
        if i % 2 == 0:
            vl = None if j == 0 else (rw_v0[j - 1], rw_v1[j - 1], rw_v2[j - 1])
            xs, vcur = _rwkv_layer(geo, xs, mods, norm1[i], rw_mix[j], rw_wrkv[j], rw_wo[j],
                                   rw_w0[j], rw_w1[j], rw_w2[j], rw_a0[j], rw_a1[j], rw_a2[j], vl,
                                   rw_g1[j], rw_g2[j], rw_kk[j], rw_ka[j], rw_rk[j], rw_lnw[j],
                                   rw_lnb[j], vfirst)
            if j == 0:
                vfirst = vcur
        else:
            xs = _mla_layer(geo, xs, mods, norm1[i], rope_t, mla_win[j], mla_qnorm[j],
                            mla_kvnorm[j], mla_wuq[j], mla_wukv[j], mla_gq[j], mla_gk[j], mla_wo[j])
        if i == depth - 1:
            xs = xs.reshape(b, l + t, d)[:, l:, :].reshape(b * t, d)
            geo, geo_g = _Geom(b, 0, t, d, geo.tm), _Geom(b, 0, t, d, geo_g.tm)
        xs = _peer_layer(geo, geo_g, xs, mods, norm2[i], peer_wq[i], peer_qnorm[i], peer_keys[i],
                         peer_u[i], peer_v[i])
    return xs.reshape(b, t, d)
```

```python
import functools
import math

import jax
import jax.numpy as jnp
from jax import lax
from jax.experimental import pallas as pl
from jax.experimental.pallas import tpu as pltpu
from jax.experimental.pallas import tpu_sc as plsc

F32 = jnp.float32
BF16 = jnp.bfloat16

EPS = 1e-6
GN_EPS = 64e-5
RW_HEAD = 64
WKV_CHUNK = 64
MLA_HEADS = 16
QK_NOPE = 128
QK_ROPE = 64
V_HEAD = 128
QK_PAD = 256
ROPE_THETA = 10000.0
GRID_W = 64
ATTN_SCALE = (QK_NOPE + QK_ROPE) ** -0.5
PEER_HEADS = 8
N_KEYS = 128
PEER_TOPK = 16
D_KEY = 256
LORA_PAD = 128

LANES = 128
SUBLANES = 8
VMEM_LIMIT = 56 * 1024 * 1024


def _cparams(*sem):
    return pltpu.CompilerParams(dimension_semantics=sem, vmem_limit_bytes=VMEM_LIMIT)


def _dot(a, b):
    return jnp.dot(a, b, preferred_element_type=F32)


def _dot_nt(a, b):
    return lax.dot_general(a, b, (((1,), (1,)), ((), ())), preferred_element_type=F32)


def _split2(x):
    hi = x.astype(BF16)
    lo = (x - hi.astype(F32)).astype(BF16)
    return hi, lo


def _split3(x):
    hi = x.astype(BF16)
    r1 = x - hi.astype(F32)
    mid = r1.astype(BF16)
    lo = (r1 - mid.astype(F32)).astype(BF16)
    return hi, mid, lo


def _dot3(a, b):
    ah, al = _split2(a)
    bh, bl = _split2(b)
    return _dot(ah, bh) + (_dot(ah, bl) + _dot(al, bh))


def _dot3_nt(a, b):
    ah, al = _split2(a)
    bh, bl = _split2(b)
    return _dot_nt(ah, bh) + (_dot_nt(ah, bl) + _dot_nt(al, bh))


def _dot_exact_lhs(sel, x):
    hi, mid, lo = _split3(x)
    return _dot(sel, hi) + (_dot(sel, mid) + _dot(sel, lo))


def _modulate(x, g, shift, scale):
    ms = jnp.mean(x * x, axis=-1, keepdims=True)
    return (x * lax.rsqrt(ms + EPS) * g) * (1.0 + scale) + shift


def _sigmoid(x):
    return 1.0 / (1.0 + jnp.exp(-x))


def _softplus(y):
    return jnp.maximum(y, 0.0) + jnp.log(1.0 + jnp.exp(-jnp.abs(y)))


def _erf(x):
    return lax.erf(x)


def _gelu(x):
    return 0.5 * x * (1.0 + _erf(x * (2.0 ** -0.5)))


def _ada_kernel(s_ref, w_ref, b_ref, o_ref):
    s = s_ref[...]
    s = s * _sigmoid(s)
    o_ref[...] = _dot3(s, w_ref[...]) + b_ref[...]


def _adaln(cond8, w_ada, b_ada):
    depth, d, n = w_ada.shape
    tn = 1024
    return pl.pallas_call(
        _ada_kernel,
        grid=(depth, n // tn),
        in_specs=[
            pl.BlockSpec((SUBLANES, d), lambda l, j: (0, 0)),
            pl.BlockSpec((None, d, tn), lambda l, j: (l, 0, j)),
            pl.BlockSpec((None, 1, tn), lambda l, j: (l, 0, j)),
        ],
        out_specs=pl.BlockSpec((None, SUBLANES, tn), lambda l, j: (l, 0, j)),
        out_shape=jax.ShapeDtypeStruct((depth, SUBLANES, n), F32),
        compiler_params=_cparams("parallel", "parallel"),
        name="adaln",
    )(cond8, w_ada, b_ada.reshape(depth, 1, n))


class _Geom:
    def __init__(self, batch, ctx_len, seq_len, d_model, tm):
        self.b, self.l, self.t, self.d = batch, ctx_len, seq_len, d_model
        self.s = ctx_len + seq_len
        self.n = batch * self.s
        self.tm = tm
        assert ctx_len % tm == 0 and seq_len % tm == 0
        self.tpb = self.s // tm
        self.nct = ctx_len // tm
        self.ntiles = self.n // tm

    def mod_spec(self, nlead=0):
        tpb, nct = self.tpb, self.nct

        def imap(*ids):
            i = ids[nlead]
            return (i // tpb, ((i % tpb) >= nct).astype(jnp.int32), 0, 0)

        return pl.BlockSpec((None, None, 6, self.d), imap)


def _modmm_kernel(x_ref, g_ref, mod_ref, w_ref, o_ref, *h_ref, which):
    h = _modulate(x_ref[...], g_ref[...], mod_ref[3 * which:3 * which + 1, :],
                  mod_ref[3 * which + 1:3 * which + 2, :])
    if h_ref:
        h_ref[0][...] = h
    o_ref[...] = _dot(h.astype(BF16), w_ref[...])


def _mod_matmul(geo, x, g, mods, w, which, emit_h):
    n, d = x.shape
    nn = w.shape[1]
    tm = geo.tm
    out_shape = [jax.ShapeDtypeStruct((n, nn), F32)]
    out_specs = [pl.BlockSpec((tm, nn), lambda i: (i, 0))]
    if emit_h:
        out_shape.append(jax.ShapeDtypeStruct((n, d), F32))
        out_specs.append(pl.BlockSpec((tm, d), lambda i: (i, 0)))
    res = pl.pallas_call(
        functools.partial(_modmm_kernel, which=which),
        grid=(geo.ntiles,),
        in_specs=[
            pl.BlockSpec((tm, d), lambda i: (i, 0)),
            pl.BlockSpec((1, d), lambda i: (0, 0)),
            geo.mod_spec(),
            pl.BlockSpec((d, nn), lambda i: (0, 0)),
        ],
        out_specs=out_specs,
        out_shape=out_shape,
        compiler_params=_cparams("parallel"),
        name="mod_matmul",
    )(x, g.reshape(1, d), mods, w)
    return res if emit_h else res[0]


def _mmres_kernel(y_ref, w_ref, x_ref, mod_ref, o_ref, *, gidx):
    acc = _dot(y_ref[...], w_ref[...])
    o_ref[...] = x_ref[...] + mod_ref[gidx:gidx + 1, :] * acc


def _matmul_res(geo, y, w, x, mods, gidx):
    n, k = y.shape
    d = x.shape[1]
    tm = geo.tm
    return pl.pallas_call(
        functools.partial(_mmres_kernel, gidx=gidx),
        grid=(geo.ntiles,),
        in_specs=[
            pl.BlockSpec((tm, k), lambda i: (i, 0)),
            pl.BlockSpec((k, d), lambda i: (0, 0)),
            pl.BlockSpec((tm, d), lambda i: (i, 0)),
            geo.mod_spec(),
        ],
        out_specs=pl.BlockSpec((tm, d), lambda i: (i, 0)),
        out_shape=jax.ShapeDtypeStruct((n, d), F32),
        compiler_params=_cparams("parallel"),
        name="matmul_res",
    )(y, w, x, mods)


def _bmm_kernel(x_ref, w_ref, o_ref, *, acts):
    j = pl.program_id(0)
    y = _dot(x_ref[...], w_ref[...])
    out = y
    for jj, a in enumerate(acts):
        if a == "tanh":
            out = jnp.where(j == jj, jnp.tanh(y), out)
        elif a == "sigmoid":
            out = jnp.where(j == jj, _sigmoid(y), out)
    o_ref[...] = out.astype(o_ref.dtype)


def _bmm(x3, w3, src, acts, out_dtype, tm):
    _, n, k = x3.shape
    nj, _, nn = w3.shape
    src = tuple(src)

    def xmap(j, i):
        idx = jnp.int32(src[0])
        for jj in range(1, nj):
            idx = jnp.where(j == jj, jnp.int32(src[jj]), idx)
        return (idx, i, 0)

    return pl.pallas_call(
        functools.partial(_bmm_kernel, acts=tuple(acts)),
        grid=(nj, n // tm),
        in_specs=[
            pl.BlockSpec((None, tm, k), xmap),
            pl.BlockSpec((None, k, nn), lambda j, i: (j, 0, 0)),
        ],
        out_specs=pl.BlockSpec((None, tm, nn), lambda j, i: (j, i, 0)),
        out_shape=jax.ShapeDtypeStruct((nj, n, nn), out_dtype),
        compiler_params=_cparams("parallel", "parallel"),
        name="bmm",
    )(x3, w3)


def _rwmix_kernel(x_ref, xp_ref, xn_ref, g_ref, mod_ref, mix_ref, o_ref, *, tpb, nct):
    i = pl.program_id(0)
    tm = x_ref.shape[0]
    g = g_ref[...]
    shift = mod_ref[0:1, :]
    scale = mod_ref[1:2, :]
    h = _modulate(x_ref[...], g, shift, scale)
    hp = _modulate(xp_ref[...], g, shift, scale)[SUBLANES - 1:SUBLANES, :]
    hn = _modulate(xn_ref[...], g, shift, scale)[0:1, :]
    it = i % tpb
    first = jnp.logical_or(it == 0, it == nct)
    last = jnp.logical_or(it == nct - 1, it == tpb - 1)
    hp = jnp.where(first, 0.0, hp)
    hn = jnp.where(last, 0.0, hn)
    rows = lax.broadcasted_iota(jnp.int32, h.shape, 0)
    prev = jnp.where(rows == 0, hp, pltpu.roll(h, 1, axis=0))
    nxt = jnp.where(rows == tm - 1, hn, pltpu.roll(h, tm - 1, axis=0))
    xx = 0.5 * (prev + nxt) - h
    for m in range(6):
        o_ref[m] = (h + xx * mix_ref[m:m + 1, :]).astype(BF16)


def _rwkv_mix(geo, x, g, mods, mix):
    n, d = x.shape
    tm = geo.tm
    r8 = tm // SUBLANES
    nblk8 = n // SUBLANES
    return pl.pallas_call(
        functools.partial(_rwmix_kernel, tpb=geo.tpb, nct=geo.nct),
        grid=(geo.ntiles,),
        in_specs=[
            pl.BlockSpec((tm, d), lambda i: (i, 0)),
            pl.BlockSpec((SUBLANES, d), lambda i: (jnp.maximum(i * r8 - 1, 0), 0)),
            pl.BlockSpec((SUBLANES, d), lambda i: (jnp.minimum((i + 1) * r8, nblk8 - 1), 0)),
            pl.BlockSpec((1, d), lambda i: (0, 0)),
            geo.mod_spec(),
            pl.BlockSpec((6, d), lambda i: (0, 0)),
        ],
        out_specs=pl.BlockSpec((6, tm, d), lambda i: (0, i, 0)),
        out_shape=jax.ShapeDtypeStruct((6, n, d), BF16),
        compiler_params=_cparams("parallel"),
        name="rwkv_mix",
    )(x, x, x, g.reshape(1, d), mods, mix)


def _head_sum(x, bd):
    hi, lo = _split2(x)
    return _dot(hi, bd) + _dot(lo, bd)


def _rwfeat_kernel(k_ref, v_ref, tl_ref, w2_ref, a2_ref, g2_ref, v2_ref, pv_ref, bd_ref,
                   *rest, has_vlora):
    if has_vlora:
        vf_ref, lw_ref, kd_ref, as_ref, kk_ref, g_ref, vo_ref = rest
    else:
        lw_ref, kd_ref, as_ref, kk_ref, g_ref = rest
    k = k_ref[...]
    tw = tl_ref[0]
    ta = tl_ref[1]
    tg = tl_ref[2]
    w0 = pv_ref[0:2, :]
    a0 = pv_ref[2:4, :]
    k_k = pv_ref[5:6, :]
    k_a = pv_ref[6:7, :]
    for z in range(2):
        sl = slice(LORA_PAD * z, LORA_PAD * (z + 1))
        lora_w = _dot(tw[:, sl], w2_ref[z])
        w = -_softplus(-(w0[z:z + 1, :] + lora_w)) - 0.5
        lw_ref[z] = -jnp.exp(w)
        a_sig = _sigmoid(a0[z:z + 1, :] + _dot(ta[:, sl], a2_ref[z]))
        as_ref[z] = a_sig
        kd_ref[z] = k * (1.0 + (a_sig - 1.0) * k_a)
    g_ref[...] = _dot(tg, g2_ref[...])
    kkr = k * k_k
    ss = _head_sum(kkr * kkr, bd_ref[...])
    kk_ref[...] = kkr * lax.rsqrt(ss + 1e-12)
    if has_vlora:
        v = v_ref[...]
        tv = tl_ref[3]
        gate = _sigmoid(pv_ref[4:5, :] + _dot(tv[:, 0:LORA_PAD], v2_ref[...]))
        vo_ref[...] = v + (vf_ref[...] - v) * gate


def _rwkv_feat(geo, rkv, tl, w2p, a2p, g2, v2p, pvec, vfirst):
    _, n, d = rkv.shape
    tm, tc = geo.tm, 512
    has_vlora = vfirst is not None
    lt = tl.shape[2]
    ii = lax.broadcasted_iota(jnp.int32, (tc, tc), 0) // RW_HEAD
    jj = lax.broadcasted_iota(jnp.int32, (tc, tc), 1) // RW_HEAD
    bd = (ii == jj).astype(BF16)
    row = lambda i, j: (i, j)
    in_specs = [
        pl.BlockSpec((None, tm, tc), lambda i, j: (1, i, j)),
        pl.BlockSpec((None, tm, tc), lambda i, j: (2, i, j)),
        pl.BlockSpec((4, tm, lt), lambda i, j: (0, i, 0)),
        pl.BlockSpec((2, LORA_PAD, tc), lambda i, j: (0, 0, j)),
        pl.BlockSpec((2, LORA_PAD, tc), lambda i, j: (0, 0, j)),
        pl.BlockSpec((lt, tc), lambda i, j: (0, j)),
        pl.BlockSpec((LORA_PAD, tc), lambda i, j: (0, j)),
        pl.BlockSpec((SUBLANES, tc), lambda i, j: (0, j)),
        pl.BlockSpec((tc, tc), lambda i, j: (0, 0)),
    ]
    args = [rkv, rkv, tl, w2p, a2p, g2, v2p, pvec, bd]
    dir_spec = pl.BlockSpec((2, tm, tc), lambda i, j: (0, i, j))
    out_specs = [dir_spec, dir_spec, dir_spec, pl.BlockSpec((tm, tc), row), pl.BlockSpec((tm, tc), row)]
    out_shape = [jax.ShapeDtypeStruct((2, n, d), F32)] * 3 + [jax.ShapeDtypeStruct((n, d), F32)] * 2
    if has_vlora:
        in_specs.append(pl.BlockSpec((tm, tc), row))
        args.append(vfirst)
        out_specs.append(pl.BlockSpec((tm, tc), row))
        out_shape.append(jax.ShapeDtypeStruct((n, d), F32))
    res = pl.pallas_call(
        functools.partial(_rwfeat_kernel, has_vlora=has_vlora),
        grid=(geo.ntiles, d // tc),
        in_specs=in_specs,
        out_specs=out_specs,
        out_shape=out_shape,
        compiler_params=_cparams("parallel", "parallel"),
        name="rwkv_feat",
    )(*args)
    if has_vlora:
        lw, kd, asg, kk, g, v = res
    else:
        lw, kd, asg, kk, g = res
        v = rkv[2]
    return lw, kd, asg, kk, g, v


def _wkv_chunk_kernel(r_ref, v_ref, kk_ref, lw_ref, kd_ref, as_ref,
                      m_ref, ga_ref, rq_ref, o0_ref, pc_ref, *, npairs):
    c = WKV_CHUNK
    c2 = 2 * c
    sgn = 1 - 2 * pl.program_id(0)
    ri = lax.broadcasted_iota(jnp.int32, (c2, c2), 0)
    ci = lax.broadcasted_iota(jnp.int32, (c2, c2), 1)
    same = (ri >= c) == (ci >= c)
    tt = jnp.where(ri >= c, ri - c, ri)
    ss = jnp.where(ci >= c, ci - c, ci)
    earlier = (ss - tt) * sgn < 0
    strict = jnp.logical_and(same, earlier)
    incl = jnp.logical_and(same, jnp.logical_or(earlier, ss == tt))
    eye = (ri == ci).astype(F32)
    r64 = lax.broadcasted_iota(jnp.int32, (c, c), 0)
    c64 = lax.broadcasted_iota(jnp.int32, (c, c), 1)
    ltri = jnp.where((c64 - r64) * sgn <= 0, 1.0, 0.0).astype(BF16)
    head0 = lax.broadcasted_iota(jnp.int32, (c, LANES), 1) < RW_HEAD
    pairs = range(npairs)

    def stack(x):
        return jnp.concatenate([jnp.where(head0, x, 0.0), jnp.where(head0, 0.0, x)], axis=0)

    def dup(x):
        return jnp.concatenate([x, x], axis=0)

    def bf(x):
        return x.astype(BF16)

    lhs, rhs, a2, bp2, kp2, vst, r2 = [], [], [], [], [], [], []
    for p in pairs:
        sl = slice(LANES * p, LANES * (p + 1))
        lw = lw_ref[:, sl]
        cum = _dot_exact_lhs(ltri, lw)
        tot = jnp.sum(lw, axis=0, keepdims=True)
        p_inv = jnp.exp(-cum)
        p_end = jnp.exp(tot - cum)
        kk = kk_ref[:, sl]
        b = kk * as_ref[:, sl]
        kd = kd_ref[:, sl]
        a2p = stack(-kk * jnp.exp(cum - lw))
        r2p = stack(r_ref[:, sl] * jnp.exp(cum))
        a2.append(bf(a2p))
        r2.append(r2p)
        lhs.append(jnp.concatenate([a2[p], bf(r2p)], axis=0))
        rhs.append(jnp.concatenate([dup(bf(b * p_inv)), dup(bf(kd * p_inv))], axis=0))
        bp2.append(bf(stack(b * p_end)))
        kp2.append(bf(stack(kd * p_end)))
        vst.append(bf(stack(v_ref[:, sl])))
        pc_ref[:, sl] = jnp.broadcast_to(jnp.exp(tot), (SUBLANES, LANES))
    gram = [_dot_nt(lhs[p], rhs[p]) for p in pairs]
    nab = [jnp.where(strict, gram[p][0:c2, 0:c2], 0.0) for p in pairs]
    nrb = [bf(jnp.where(incl, gram[p][c2:2 * c2, 0:c2], 0.0)) for p in pairs]
    nk = [bf(jnp.concatenate([jnp.where(strict, gram[p][0:c2, c2:2 * c2], 0.0),
                              jnp.where(incl, gram[p][c2:2 * c2, c2:2 * c2], 0.0)], axis=0))
          for p in pairs]
    xv = [_dot(nk[p], vst[p]) for p in pairs]
    tinv = [eye + nab[p] for p in pairs]
    npow = [bf(nab[p]) for p in pairs]
    for _ in range(int(math.log2(c)) - 1):
        npow = [bf(_dot(npow[p], npow[p])) for p in pairs]
        tinv = [tinv[p] + _dot(bf(tinv[p]), npow[p]) for p in pairs]
    y = [_dot(bf(tinv[p]), jnp.concatenate([a2[p], bf(xv[p][0:c2, :])], axis=1)) for p in pairs]
    yb = [bf(y[p]) for p in pairs]
    z = [_dot(nrb[p], yb[p]) for p in pairs]
    mg = [_dot(bf(y[p].T), bp2[p]) for p in pairs]
    vk = [_dot(bf(vst[p].astype(F32).T), kp2[p]) for p in pairs]
    for p in pairs:
        sl = slice(LANES * p, LANES * (p + 1))
        m_ref[:, sl] = mg[p][0:c2, :]
        ga_ref[:, sl] = mg[p][c2:2 * c2, :] + vk[p]
        rq_ref[:, sl] = r2[p] + z[p][:, 0:c2]
        o0 = z[p][:, c2:2 * c2] + xv[p][c2:2 * c2, :]
        o0_ref[:, sl] = o0[0:c, :] + o0[c:c2, :]


def _wkv_scan_kernel(m_ref, ga_ref, rq_ref, o0_ref, pc_ref, o_ref, g_scr, *, npairs):
    c = WKV_CHUNK

    @pl.when(pl.program_id(2) == 0)
    def _():
        g_scr[...] = jnp.zeros_like(g_scr)

    for p in range(npairs):
        sl = slice(LANES * p, LANES * (p + 1))
        g = g_scr[p]
        o_st = _dot3_nt(rq_ref[:, sl], g)
        o_ref[:, sl] = o_st[0:c, :] + o_st[c:2 * c, :] + o0_ref[:, sl]
        g_scr[p] = g * pc_ref[0:1, sl] + _dot3(g, m_ref[:, sl]) + ga_ref[:, sl]


WKV_CHUNK_LANES = 1024


def _wkv_bidir(r, v, kk, lw, kd, asg, batch, ctx_len):
    n, d = r.shape
    c = WKV_CHUNK
    s = n // batch
    ncs = s // c
    ncc = ctx_len // c
    nch = n // c
    lanes = min(WKV_CHUNK_LANES, d)
    ngrp = d // lanes
    shared = pl.BlockSpec((c, lanes), lambda z, i, j: (i, j))
    perdir = pl.BlockSpec((None, c, lanes), lambda z, i, j: (z, i, j))
    big = pl.BlockSpec((None, 2 * c, lanes), lambda z, i, j: (z, i, j))
    m_, ga_, rq_, o0_, pc_ = pl.pallas_call(
        functools.partial(_wkv_chunk_kernel, npairs=lanes // LANES),
        grid=(2, nch, ngrp),
        in_specs=[shared, shared, shared, perdir, perdir, perdir],
        out_specs=[big, big, big, perdir,
                   pl.BlockSpec((None, SUBLANES, lanes), lambda z, i, j: (z, i, j))],
        out_shape=[jax.ShapeDtypeStruct((2, nch * 2 * c, d), F32)] * 3
        + [jax.ShapeDtypeStruct((2, n, d), F32),
           jax.ShapeDtypeStruct((2, nch * SUBLANES, d), F32)],
        compiler_params=_cparams("parallel", "parallel", "parallel"),
        name="wkv_chunk",
    )(r, v, kk, lw, kd, asg)

    def cmap(z, b, cc):
        back = jnp.where(cc < ncc, ncc - 1 - cc, ncs + ncc - 1 - cc)
        return (z, b * ncs + jnp.where(z == 0, cc, back), 0)

    out = pl.pallas_call(
        functools.partial(_wkv_scan_kernel, npairs=d // LANES),
        grid=(2, batch, ncs),
        in_specs=[
            pl.BlockSpec((None, 2 * c, d), cmap),
            pl.BlockSpec((None, 2 * c, d), cmap),
            pl.BlockSpec((None, 2 * c, d), cmap),
            pl.BlockSpec((None, c, d), cmap),
            pl.BlockSpec((None, SUBLANES, d), cmap),
        ],
        out_specs=pl.BlockSpec((None, c, d), cmap),
        out_shape=jax.ShapeDtypeStruct((2, n, d), F32),
        scratch_shapes=[pltpu.VMEM((d // LANES, 2 * c, 2 * c), F32)],
        compiler_params=_cparams("parallel", "parallel", "arbitrary"),
        name="wkv_scan",
    )(m_, ga_, rq_, o0_, pc_)
    return out


def _rwread_kernel(o_ref, r_ref, kd_ref, v_ref, g_ref, pv_ref, bd_ref, y_ref):
    bd = bd_ref[...]
    wkv = o_ref[0] + o_ref[1]
    inv = 1.0 / RW_HEAD
    mu = _head_sum(wkv, bd) * inv
    dev = wkv - mu
    var = _head_sum(dev * dev, bd) * inv
    y = dev * lax.rsqrt(var + GN_EPS) * pv_ref[1:2, :] + pv_ref[2:3, :]
    rk = r_ref[...] * (kd_ref[0] + kd_ref[1]) * pv_ref[0:1, :]
    y = y + _head_sum(rk, bd) * v_ref[...]
    y_ref[...] = (y * g_ref[...]).astype(BF16)


def _rwkv_readout(geo, wkv, r, kd, v, g, pvec):
    n, d = r.shape
    tm, tc = geo.tm, 512
    ii = lax.broadcasted_iota(jnp.int32, (tc, tc), 0) // RW_HEAD
    jj = lax.broadcasted_iota(jnp.int32, (tc, tc), 1) // RW_HEAD
    bd = (ii == jj).astype(BF16)
    row = pl.BlockSpec((tm, tc), lambda i, j: (i, j))
    dirs = pl.BlockSpec((2, tm, tc), lambda i, j: (0, i, j))
    return pl.pallas_call(
        _rwread_kernel,
        grid=(geo.ntiles, d // tc),
        in_specs=[dirs, row, dirs, row, row,
                  pl.BlockSpec((SUBLANES, tc), lambda i, j: (0, j)),
                  pl.BlockSpec((tc, tc), lambda i, j: (0, 0))],
        out_specs=row,
        out_shape=jax.ShapeDtypeStruct((n, d), BF16),
        compiler_params=_cparams("parallel", "parallel"),
        name="rwkv_readout",
    )(wkv, r, kd, v, g, pvec, bd)


def _mlaqkv_kernel(z_ref, qn_ref, kvn_ref, wuq_ref, wukv_ref, gq_ref, gk_ref, cos_ref, sin_ref,
                   q_ref, k_ref, v_ref, *, q_lora, kv_lora):
    z = z_ref[...]
    cq = z[:, 0:q_lora]
    ckv = z[:, q_lora:q_lora + kv_lora]
    krot = z[:, q_lora + kv_lora:q_lora + kv_lora + LANES]

    def rms(x, g):
        ms = jnp.mean(x * x, axis=-1, keepdims=True)
        return x * lax.rsqrt(ms + EPS) * g

    qf = _dot(rms(cq, qn_ref[...]).astype(BF16), wuq_ref[...])
    kvf = _dot(rms(ckv, kvn_ref[...]).astype(BF16), wukv_ref[...])
    cos = cos_ref[...]
    sin = sin_ref[...]
    half = QK_ROPE // 2
    lane = lax.broadcasted_iota(jnp.int32, cos.shape, 1)

    def rope(x):
        up = pltpu.roll(x, LANES - half, axis=1)
        dn = pltpu.roll(x, half, axis=1)
        return x * cos + jnp.where(lane < half, up, dn) * sin

    inv_w = 1.0 / (QK_NOPE + QK_ROPE)
    gq = gq_ref[...]
    gk = gk_ref[...]
    kr_ss = jnp.sum(krot * krot, axis=-1, keepdims=True)
    for h in range(MLA_HEADS):
        o = QK_PAD * h
        qh = qf[:, o:o + QK_PAD]
        rs = lax.rsqrt(jnp.sum(qh * qh, axis=-1, keepdims=True) * inv_w + EPS)
        qn = qh * rs * gq * (ATTN_SCALE * math.log2(math.e))
        q_ref[:, o:o + QK_NOPE] = qn[:, 0:QK_NOPE].astype(BF16)
        q_ref[:, o + QK_NOPE:o + QK_PAD] = rope(qn[:, QK_NOPE:QK_PAD]).astype(BF16)
        kn = kvf[:, o:o + QK_NOPE]
        rsk = lax.rsqrt((jnp.sum(kn * kn, axis=-1, keepdims=True) + kr_ss) * inv_w + EPS)
        k_ref[:, o:o + QK_NOPE] = (kn * rsk * gk[:, 0:QK_NOPE]).astype(BF16)
        k_ref[:, o + QK_NOPE:o + QK_PAD] = rope(krot * rsk * gk[:, QK_NOPE:QK_PAD]).astype(BF16)
        v_ref[:, V_HEAD * h:V_HEAD * (h + 1)] = kvf[:, o + QK_NOPE:o + QK_PAD].astype(BF16)


def _mla_qkv(geo, z, qn, kvn, wuq_p, wukv, gq_p, gk_p, cos_t, sin_t):
    n, zw = z.shape
    tm = geo.tm
    q_lora, kv_lora = qn.shape[0], kvn.shape[0]
    hq = MLA_HEADS * QK_PAD
    tpb = geo.tpb
    full = lambda i: (0, 0)
    rowmap = lambda i: (i, 0)
    return pl.pallas_call(
        functools.partial(_mlaqkv_kernel, q_lora=q_lora, kv_lora=kv_lora),
        grid=(geo.ntiles,),
        in_specs=[
            pl.BlockSpec((tm, zw), rowmap),
            pl.BlockSpec((1, q_lora), full),
            pl.BlockSpec((1, kv_lora), full),
            pl.BlockSpec((q_lora, hq), full),
            pl.BlockSpec((kv_lora, hq), full),
            pl.BlockSpec((1, QK_PAD), full),
            pl.BlockSpec((1, QK_PAD), full),
            pl.BlockSpec((tm, LANES), lambda i: (i % tpb, 0)),
            pl.BlockSpec((tm, LANES), lambda i: (i % tpb, 0)),
        ],
        out_specs=[pl.BlockSpec((tm, hq), rowmap), pl.BlockSpec((tm, hq), rowmap),
                   pl.BlockSpec((tm, MLA_HEADS * V_HEAD), rowmap)],
        out_shape=[jax.ShapeDtypeStruct((n, hq), BF16), jax.ShapeDtypeStruct((n, hq), BF16),
                   jax.ShapeDtypeStruct((n, MLA_HEADS * V_HEAD), BF16)],
        compiler_params=_cparams("parallel"),
        name="mla_qkv",
    )(z, qn.reshape(1, -1), kvn.reshape(1, -1), wuq_p, wukv, gq_p, gk_p, cos_t, sin_t)


ATTN_SUBTILES = 2


def _attn_kernel(q_ref, k_ref, v_ref, o_ref):
    k = k_ref[...]
    v = v_ref[...]
    rows = q_ref.shape[0] // ATTN_SUBTILES
    subs = range(ATTN_SUBTILES)
    s = [_dot_nt(q_ref[rows * i:rows * (i + 1), :], k) for i in subs]
    m = [jnp.max(s[i], axis=-1, keepdims=True) for i in subs]
    p = [jnp.exp2(s[i] - m[i]) for i in subs]
    l = [jnp.sum(p[i], axis=-1, keepdims=True) for i in subs]
    for i in subs:
        o_ref[rows * i:rows * (i + 1), :] = (_dot(p[i].astype(BF16), v) / l[i]).astype(BF16)


def _attn_call(q3, k3, v3, tq, nkeys):
    b, sq, _ = q3.shape
    return pl.pallas_call(
        _attn_kernel,
        grid=(b, MLA_HEADS, sq // tq),
        in_specs=[
            pl.BlockSpec((None, tq, QK_PAD), lambda bb, h, i: (bb, i, h)),
            pl.BlockSpec((None, nkeys, QK_PAD), lambda bb, h, i: (bb, 0, h)),
            pl.BlockSpec((None, nkeys, V_HEAD), lambda bb, h, i: (bb, 0, h)),
        ],
        out_specs=pl.BlockSpec((None, tq, V_HEAD), lambda bb, h, i: (bb, i, h)),
        out_shape=jax.ShapeDtypeStruct((b, sq, MLA_HEADS * V_HEAD), BF16),
        compiler_params=_cparams("parallel", "parallel", "parallel"),
        name="mla_attention",
    )(q3, k3, v3)


def _mla_attention(geo, q, k, v):
    b, s, l = geo.b, geo.s, geo.l
    q3 = q.reshape(b, s, MLA_HEADS * QK_PAD)
    k3 = k.reshape(b, s, MLA_HEADS * QK_PAD)
    v3 = v.reshape(b, s, MLA_HEADS * V_HEAD)
    o_ctx = _attn_call(q3[:, :l], k3, v3, geo.tm, l)
    tq_lat = 2 * geo.tm if geo.t % (2 * geo.tm) == 0 else geo.tm
    o_lat = _attn_call(q3[:, l:], k3, v3, tq_lat, s)
    return jnp.concatenate([o_ctx, o_lat], axis=1).reshape(b * s, MLA_HEADS * V_HEAD)


def _topk_rows(s, k, payload=None):
    rows = lax.broadcasted_iota(jnp.int32, s.shape, 0).astype(F32)
    big = float(s.shape[0])
    vals, idxs = [], []
    for _ in range(k):
        m = jnp.max(s, axis=0, keepdims=True)
        idx = jnp.min(jnp.where(s == m, rows, big), axis=0, keepdims=True)
        hit = rows == idx
        vals.append(m)
        if payload is None:
            idxs.append(idx)
        else:
            idxs.append(jnp.sum(jnp.where(hit, payload, 0.0), axis=0, keepdims=True))
        s = jnp.where(hit, -jnp.inf, s)
    return jnp.concatenate(vals, axis=0), jnp.concatenate(idxs, axis=0)


def _peersel_kernel(q_ref, qn_ref, keys_ref, e_ref, g_ref):
    q = q_ref[...]
    ms = jnp.mean(q * q, axis=-1, keepdims=True)
    qn = q * lax.rsqrt(ms + EPS) * qn_ref[...]
    half = D_KEY // 2
    s1 = _dot3_nt(keys_ref[0], qn[:, 0:half])
    s2 = _dot3_nt(keys_ref[1], qn[:, half:D_KEY])
    t1, i1 = _topk_rows(s1, PEER_TOPK)
    t2, i2 = _topk_rows(s2, PEER_TOPK)
    k = PEER_TOPK
    sub = lax.broadcasted_iota(jnp.int32, (SUBLANES, t1.shape[1]), 0)
    cand = [t1[0:1, :] + t2]
    cidx = [i1[0:1, :] * float(N_KEYS) + i2]
    for p in range(1, k // 2):
        live = sub < k // (p + 1)
        cand.append(jnp.where(live, t1[p:p + 1, :] + t2[0:SUBLANES, :], -jnp.inf))
        cidx.append(i1[p:p + 1, :] * float(N_KEYS) + i2[0:SUBLANES, :])
    cand.append(t1[k // 2:k, :] + t2[0:1, :])
    cidx.append(i1[k // 2:k, :] * float(N_KEYS) + i2[0:1, :])
    best, eidx = _topk_rows(jnp.concatenate(cand, axis=0), k, payload=jnp.concatenate(cidx, axis=0))
    ex = jnp.exp(best - jnp.max(best, axis=0, keepdims=True))
    g_ref[...] = ex / jnp.sum(ex, axis=0, keepdims=True)
    e_ref[...] = eidx.astype(jnp.int32)


def _peer_select(qp, q_norm, keys, tm):
    n = qp.shape[0]
    return pl.pallas_call(
        _peersel_kernel,
        grid=(n // tm, PEER_HEADS),
        in_specs=[
            pl.BlockSpec((tm, D_KEY), lambda i, h: (i, h)),
            pl.BlockSpec((1, D_KEY), lambda i, h: (0, 0)),
            pl.BlockSpec((2, N_KEYS, D_KEY // 2), lambda i, h: (0, 0, 0)),
        ],
        out_specs=[pl.BlockSpec((PEER_TOPK, tm), lambda i, h: (h, i)),
                   pl.BlockSpec((PEER_TOPK, tm), lambda i, h: (h, i))],
        out_shape=[jax.ShapeDtypeStruct((PEER_HEADS * PEER_TOPK, n), jnp.int32),
                   jax.ShapeDtypeStruct((PEER_HEADS * PEER_TOPK, n), F32)],
        compiler_params=_cparams("parallel", "parallel"),
        name="peer_select",
    )(qp, q_norm.reshape(1, D_KEY), keys)


GATHER_SLOTS = 4
SLAB_PAD = 1


def _pack_expert_table(u, v):
    ne, d = u.shape
    ub = lax.bitcast_convert_type(u.astype(BF16), jnp.uint16).astype(jnp.uint32)
    vb = lax.bitcast_convert_type(v.astype(BF16), jnp.uint16).astype(jnp.uint32)
    return ((vb << 16) | ub).reshape(ne, d // LANES, LANES)


def _peergather_kernel(idx_ref, idxn_ref, gate_ref, h_ref, x_ref, mod_ref, tab_ref, o_ref,
                       *scratch, tb):
    nsel = PEER_HEADS * PEER_TOPK
    nrow = h_ref.shape[1] // LANES
    pitch = nrow + SLAB_PAD
    ns = GATHER_SLOTS
    bufs, sem = scratch[:ns], scratch[ns]
    lane_t = lax.broadcasted_iota(jnp.int32, (nsel, tb), 1)
    g2 = mod_ref[5:6, :]
    step = pl.program_id(0)
    nsteps = pl.num_programs(0)

    def row_copy(ids_ref, t, j, slot):
        return pltpu.make_async_copy(
            tab_ref.at[ids_ref[t, j]],
            bufs[slot].at[pl.ds(j * pitch, nrow), :],
            sem.at[slot])

    def issue(ids_ref, t, slot):
        for j in range(nsel):
            row_copy(ids_ref, t, j, slot).start(priority=j % 2)

    def wait(t, slot):
        for j in range(nsel):
            row_copy(idx_ref, t, j, slot).wait()

    def packed(slot, s):
        return bufs[slot][pl.ds(s, nsel, stride=pitch), :]

    def compute(t, slot):
        hrow = h_ref[pl.ds(t, 1), :]
        acc = jnp.zeros((nsel, LANES), F32)
        for s in range(nrow):
            u = lax.bitcast_convert_type(packed(slot, s) << 16, F32)
            acc = acc + u * hrow[:, LANES * s:LANES * (s + 1)]
        dots = jnp.sum(acc, axis=-1, keepdims=True)
        gcol = jnp.sum(jnp.where(lane_t == t, gate_ref[...], 0.0), axis=-1, keepdims=True)
        coef = gcol * _gelu(dots)
        outs = []
        for s in range(nrow):
            vv = lax.bitcast_convert_type(packed(slot, s) & jnp.uint32(0xFFFF0000), F32)
            outs.append(jnp.sum(coef * vv, axis=0, keepdims=True))
        orow = jnp.concatenate(outs, axis=1)
        o_ref[pl.ds(t, 1), :] = x_ref[pl.ds(t, 1), :] + g2 * orow

    @pl.when(step == 0)
    def _():
        for s in range(ns - 1):
            issue(idx_ref, s, s)

    ngroups = tb // ns

    def body(g, carry):
        for s in range(ns):
            t = g * ns + s
            wait(t, s)
            issue(idx_ref, t + ns - 1, (s + ns - 1) % ns)
            compute(t, s)
        return carry

    lax.fori_loop(0, ngroups - 1, body, 0)
    for s in range(ns):
        t = (ngroups - 1) * ns + s
        wait(t, s)
        if s == 0:
            issue(idx_ref, tb - 1, ns - 1)
        else:
            @pl.when(step < nsteps - 1)
            def _():
                issue(idxn_ref, s - 1, s - 1)
        compute(t, s)


def _peer_gather(geo_g, eidx, gate_t, h2, x, mods, table, n):
    d = x.shape[1]
    tb = geo_g.tm
    nsel = PEER_HEADS * PEER_TOPK
    pitch = d // LANES + SLAB_PAD
    assert n % tb == 0
    nsteps = n // tb
    return pl.pallas_call(
        functools.partial(_peergather_kernel, tb=tb),
        grid=(nsteps,),
        in_specs=[
            pl.BlockSpec((tb, nsel), lambda i: (i, 0), memory_space=pltpu.SMEM),
            pl.BlockSpec((tb, nsel), lambda i: (jnp.minimum(i + 1, nsteps - 1), 0),
                         memory_space=pltpu.SMEM),
            pl.BlockSpec((nsel, tb), lambda i: (0, i)),
            pl.BlockSpec((tb, d), lambda i: (i, 0)),
            pl.BlockSpec((tb, d), lambda i: (i, 0)),
            geo_g.mod_spec(),
            pl.BlockSpec(memory_space=pl.ANY),
        ],
        out_specs=pl.BlockSpec((tb, d), lambda i: (i, 0)),
        out_shape=jax.ShapeDtypeStruct((n, d), F32),
        scratch_shapes=[pltpu.VMEM((nsel * pitch, LANES), jnp.uint32)] * GATHER_SLOTS
        + [pltpu.SemaphoreType.DMA((GATHER_SLOTS,))],
        compiler_params=_cparams("arbitrary"),
        name="peer_gather",
    )(eidx, eidx, gate_t, h2, x, mods, table)


SC_CORES = 2
SC_SUBCORES = 16
SC_LANES = 16
SC_CHUNK = 16
SC_TOKENS = 5120

_ERF_ALPHA = (0.00022905065861350646, 0.0034082910107109506, 0.050955695062380861,
              0.18520832239976145, 1.128379143519084)
_ERF_BETA = (-1.1791602954361697e-7, 0.000023547966471313185, 0.0010179625278914885,
             0.014070470171167667, 0.11098505178285362, 0.49746925110067538, 1.0)
_ERF_CLAMP = 3.832506856900711


def _erf_rational(x):
    x = jnp.minimum(jnp.maximum(x, -_ERF_CLAMP), _ERF_CLAMP)
    x2 = x * x
    p = jnp.full_like(x, _ERF_ALPHA[0])
    for c in _ERF_ALPHA[1:]:
        p = p * x2 + c
    q = jnp.full_like(x, _ERF_BETA[0])
    for c in _ERF_BETA[1:]:
        q = q * x2 + c
    return x * p / q


def _peer_sc(eidx, gate, h, table):
    n, nsel = eidx.shape
    d = h.shape[1]
    nw = SC_CORES * SC_SUBCORES
    assert n % nw == 0 and nsel % SC_CHUNK == 0 and d % SC_LANES == 0
    tpw = n // nw
    nchunk = nsel // SC_CHUNK
    nvec = d // SC_LANES
    mesh = plsc.VectorSubcoreMesh(core_axis_name="c", subcore_axis_name="s",
                                  num_cores=SC_CORES, num_subcores=SC_SUBCORES)

    @functools.partial(
        pl.kernel, mesh=mesh,
        out_type=jax.ShapeDtypeStruct((n, d), F32),
        scratch_types=[
            pltpu.VMEM((nsel,), jnp.int32),
            pltpu.VMEM((nsel,), F32),
            pltpu.VMEM((d,), F32),
            pltpu.VMEM((d,), F32),
            pltpu.VMEM((SC_CHUNK, d), jnp.uint32),
            pltpu.VMEM((SC_CHUNK, d), jnp.uint32),
            pltpu.SemaphoreType.DMA,
            pltpu.SemaphoreType.DMA,
        ],
        compiler_params=pltpu.CompilerParams(needs_layout_passes=False),
        name="peer_sc",
    )
    def sc_kernel(eidx_hbm, gate_hbm, h_hbm, tab_hbm, y_hbm, idx_v, gate_v, h_v, o_v,
                  rows0, rows1, sem0, sem1):
        wid = lax.axis_index("s") * SC_CORES + lax.axis_index("c")
        lanes = lax.iota(jnp.int32, SC_LANES)
        zero = jnp.zeros((SC_LANES,), F32)
        bufs = ((rows0, sem0), (rows1, sem1))

        def gather(c, slot):
            rows, sem = bufs[slot]
            return pltpu.make_async_copy(
                tab_hbm.at[idx_v.at[pl.ds(c * SC_CHUNK, SC_CHUNK)]], rows, sem)

        def chunk_compute(c, rows):
            def dot_body(i, accs):
                hv = h_v[pl.ds(i * SC_LANES, SC_LANES)]
                out = []
                for e in range(SC_CHUNK):
                    w = rows[e, pl.ds(i * SC_LANES, SC_LANES)]
                    out.append(accs[e] + lax.bitcast_convert_type(w << 16, F32) * hv)
                return tuple(out)

            accs = lax.fori_loop(0, nvec, dot_body, (zero,) * SC_CHUNK)
            dots = zero
            for e in range(SC_CHUNK):
                dots = jnp.where(lanes == e, jnp.sum(accs[e]), dots)
            act = 0.5 * dots * (1.0 + _erf_rational(dots * (2.0 ** -0.5)))
            coef = gate_v[pl.ds(c * SC_CHUNK, SC_CHUNK)] * act
            splat = [jnp.full((SC_LANES,), jnp.sum(jnp.where(lanes == e, coef, 0.0)), F32)
                     for e in range(SC_CHUNK)]

            def acc_body(i, carry):
                o = o_v[pl.ds(i * SC_LANES, SC_LANES)]
                for e in range(SC_CHUNK):
                    w = rows[e, pl.ds(i * SC_LANES, SC_LANES)]
                    o = o + splat[e] * lax.bitcast_convert_type(w & jnp.uint32(0xFFFF0000), F32)
                o_v[pl.ds(i * SC_LANES, SC_LANES)] = o
                return carry

            lax.fori_loop(0, nvec, acc_body, 0)

        def token_body(ti, carry):
            tok = wid * tpw + ti
            pltpu.sync_copy(eidx_hbm.at[tok], idx_v)
            pltpu.sync_copy(gate_hbm.at[tok], gate_v)
            pltpu.sync_copy(h_hbm.at[tok], h_v)

            def zero_body(i, c2):
                o_v[pl.ds(i * SC_LANES, SC_LANES)] = zero
                return c2

            lax.fori_loop(0, nvec, zero_body, 0)
            gather(0, 0).start()
            for c in range(nchunk):
                if c + 1 < nchunk:
                    gather(c + 1, (c + 1) % 2).start()
                gather(c, c % 2).wait()
                chunk_compute(c, bufs[c % 2][0])
            pltpu.sync_copy(o_v, y_hbm.at[tok])
            return carry

        lax.fori_loop(0, tpw, token_body, 0)

    return sc_kernel(eidx, gate, h, table)


def _pad_to(x, axis, size):
    pad = [(0, 0)] * x.ndim
    pad[axis] = (0, size - x.shape[axis])
    return jnp.pad(x, pad)


def _rwkv_layer(geo, xs, mods, norm1, mix, w_rkv, w_o, w0, w1, w2, a0, a1, a2, vl, g1, g2,
                k_k, k_a, r_k, ln_w, ln_b, vfirst):
    d = geo.d
    xm = _rwkv_mix(geo, xs, norm1, mods, mix)
    rkv = _bmm(xm, w_rkv.astype(BF16), (0, 2, 3), ("none",) * 3, F32, geo.tm)
    lt = g1.shape[1]
    w1c = jnp.concatenate([_pad_to(w1[0], 1, LORA_PAD), _pad_to(w1[1], 1, LORA_PAD)], axis=1)
    a1c = jnp.concatenate([_pad_to(a1[0], 1, LORA_PAD), _pad_to(a1[1], 1, LORA_PAD)], axis=1)
    if vl is None:
        v1p = jnp.zeros((d, lt), F32)
        v2p = jnp.zeros((LORA_PAD, d), F32)
        v0 = jnp.zeros((d,), F32)
    else:
        v0, v1, v2 = vl
        v1p = _pad_to(v1, 1, lt)
        v2p = _pad_to(v2, 0, LORA_PAD)
    wl1 = jnp.stack([_pad_to(w1c, 1, lt), _pad_to(a1c, 1, lt), g1, v1p]).astype(BF16)
    tl = _bmm(xm, wl1, (1, 4, 5, 3), ("tanh", "none", "sigmoid", "none"), BF16, geo.tm)
    w2p = jnp.stack([_pad_to(w2[0], 0, LORA_PAD), _pad_to(w2[1], 0, LORA_PAD)]).astype(BF16)
    a2p = jnp.stack([_pad_to(a2[0], 0, LORA_PAD), _pad_to(a2[1], 0, LORA_PAD)]).astype(BF16)
    pvec = jnp.stack([w0[0], w0[1], a0[0], a0[1], v0, k_k, k_a, jnp.zeros_like(k_k)])
    lw, kd, asg, kk, g, v = _rwkv_feat(geo, rkv, tl, w2p, a2p, g2.astype(BF16), v2p.astype(BF16),
                                      pvec, vfirst)
    r = rkv[0]
    wkv = _wkv_bidir(r, v, kk, lw, kd, asg, geo.b, geo.l)
    pv2 = _pad_to(jnp.stack([r_k, ln_w, ln_b]), 0, SUBLANES)
    y = _rwkv_readout(geo, wkv, r, kd, v, g, pv2)
    xs = _matmul_res(geo, y, w_o.astype(BF16), xs, mods, 2)
    return xs, v


def _rope_tables(geo):
    t = geo.t
    pos = jnp.arange(t)
    row = (pos // GRID_W).astype(F32)
    col = (pos % GRID_W).astype(F32)
    n_freq = QK_ROPE // 4
    inv_freq = ROPE_THETA ** (-jnp.arange(n_freq, dtype=F32) / n_freq)
    ang = jnp.concatenate([row[:, None] * inv_freq, col[:, None] * inv_freq], axis=-1)
    cos, sin = jnp.cos(ang), jnp.sin(ang)
    pad = LANES - QK_ROPE
    cos_l = jnp.concatenate([cos, cos, jnp.ones((t, pad), F32)], axis=1)
    sin_l = jnp.concatenate([-sin, sin, jnp.zeros((t, pad), F32)], axis=1)
    cos_c = jnp.ones((geo.l, LANES), F32)
    sin_c = jnp.zeros((geo.l, LANES), F32)
    return jnp.concatenate([cos_c, cos_l], axis=0), jnp.concatenate([sin_c, sin_l], axis=0)


def _mla_layer(geo, xs, mods, norm1, rope_t, w_in, q_norm, kv_norm, w_uq, w_ukv, g_q, g_k, w_o):
    q_lora, kv_lora = q_norm.shape[0], kv_norm.shape[0]
    zw = q_lora + kv_lora + LANES
    z = _mod_matmul(geo, xs, norm1, mods, _pad_to(w_in, 1, zw).astype(BF16), 0, False)
    qk = QK_NOPE + QK_ROPE
    wuq_p = _pad_to(w_uq.reshape(q_lora, MLA_HEADS, qk), 2, QK_PAD).reshape(q_lora, -1)
    gq_p = _pad_to(g_q, 0, QK_PAD).reshape(1, QK_PAD)
    gk_p = _pad_to(g_k, 0, QK_PAD).reshape(1, QK_PAD)
    q, k, v = _mla_qkv(geo, z, q_norm, kv_norm, wuq_p.astype(BF16), w_ukv.astype(BF16),
                       gq_p, gk_p, *rope_t)
    o = _mla_attention(geo, q, k, v)
    return _matmul_res(geo, o, w_o.astype(BF16), xs, mods, 2)


def _peer_layer(geo, geo_g, xs, mods, norm2, w_q, q_norm, keys, u, v):
    qp, h2 = _mod_matmul(geo, xs, norm2, mods, w_q.astype(BF16), 1, True)
    eidx_t, gate_t = _peer_select(qp, q_norm, keys, geo.tm)
    eidx = eidx_t.T
    table = _pack_expert_table(u, v)
    n, d = xs.shape
    n_sc = SC_TOKENS if n > SC_TOKENS and (n - SC_TOKENS) % geo_g.tm == 0 else 0
    n_tc = n - n_sc
    out_tc = _peer_gather(geo_g, eidx, gate_t, h2, xs, mods, table, n_tc)
    if n_sc == 0:
        return out_tc
    y_sc = _peer_sc(eidx[n_tc:], gate_t[:, n_tc:].T, h2[n_tc:], table.reshape(table.shape[0], d))
    rows = jnp.arange(n_tc, n)
    seg = ((rows % geo.s) >= geo.l).astype(jnp.int32)
    out_sc = xs[n_tc:] + mods[rows // geo.s, seg, 5] * y_sc
    return jnp.concatenate([out_tc, out_sc], axis=0)


def kernel(x, c, ctx, c_ctx, w_ada, b_ada, norm1, norm2, rw_mix, rw_wrkv, rw_wo, rw_w0, rw_w1, rw_w2, rw_a0, rw_a1, rw_a2, rw_v0, rw_v1, rw_v2, rw_g1, rw_g2, rw_kk, rw_ka, rw_rk, rw_lnw, rw_lnb, mla_win, mla_qnorm, mla_kvnorm, mla_wuq, mla_wukv, mla_gq, mla_gk, mla_wo, peer_wq, peer_qnorm, peer_keys, peer_u, peer_v):
    b, t, d = x.shape
    l = ctx.shape[1]
    depth = w_ada.shape[0]
    geo = _Geom(b, l, t, d, min(256, l))
    geo_g = _Geom(b, l, t, d, min(128, l))
    cond8 = _pad_to(jnp.concatenate([c, c_ctx[None, :]], axis=0), 0, SUBLANES)
    ada = _adaln(cond8, w_ada, b_ada).reshape(depth, SUBLANES, 6, d)
    mods_all = jnp.stack([jnp.broadcast_to(ada[:, b:b + 1], (depth, b, 6, d)), ada[:, 0:b]], axis=2)
    xs = jnp.concatenate([ctx, x], axis=1).reshape(b * (l + t), d)
    rope_t = _rope_tables(geo)
    vfirst = None
    for i in range(depth):
        j = i // 2
        mods = mods_all[i]
        if i % 2 == 0:
            vl = None if j == 0 else (rw_v0[j - 1], rw_v1[j - 1], rw_v2[j - 1])
            xs, vcur = _rwkv_layer(geo, xs, mods, norm1[i], rw_mix[j], rw_wrkv[j], rw_wo[j],
                                   rw_w0[j], rw_w1[j], rw_w2[j], rw_a0[j], rw_a1[j], rw_a2[j], vl,
                                   rw_g1[j], rw_g2[j], rw_kk[j], rw_ka[j], rw_rk[j], rw_lnw[j],
                                   rw_lnb[j], vfirst)
            if j == 0:
                vfirst = vcur
        else:
            xs = _mla_layer(geo, xs, mods, norm1[i], rope_t, mla_win[j], mla_qnorm[j],
                            mla_kvnorm[j], mla_wuq[j], mla_wukv[j], mla_gq[j], mla_gk[j], mla_wo[j])
        if i == depth - 1:
            xs = xs.reshape(b, l + t, d)[:, l:, :].reshape(b * t, d)
            geo, geo_g = _Geom(b, 0, t, d, geo.tm), _Geom(b, 0, t, d, geo_g.tm)
        xs = _peer_layer(geo, geo_g, xs, mods, norm2[i], peer_wq[i], peer_qnorm[i], peer_keys[i],
                         peer_u[i], peer_v[i])
    return xs.reshape(b, t, d)
```

```python
import functools
import math

import jax
import jax.numpy as jnp
from jax import lax
from jax.experimental import pallas as pl
from jax.experimental.pallas import tpu as pltpu
from jax.experimental.pallas import tpu_sc as plsc

F32 = jnp.float32
BF16 = jnp.bfloat16

EPS = 1e-6
GN_EPS = 64e-5
RW_HEAD = 64
WKV_CHUNK = 64
MLA_HEADS = 16
QK_NOPE = 128
QK_ROPE = 64
V_HEAD = 128
QK_PAD = 256
ROPE_THETA = 10000.0
GRID_W = 64
ATTN_SCALE = (QK_NOPE + QK_ROPE) ** -0.5
PEER_HEADS = 8
N_KEYS = 128
PEER_TOPK = 16
D_KEY = 256
LORA_PAD = 128

LANES = 128
SUBLANES = 8
VMEM_LIMIT = 56 * 1024 * 1024


def _cparams(*sem):
    return pltpu.CompilerParams(dimension_semantics=sem, vmem_limit_bytes=VMEM_LIMIT)


def _dot(a, b):
    return jnp.dot(a, b, preferred_element_type=F32)


def _dot_nt(a, b):
    return lax.dot_general(a, b, (((1,), (1,)), ((), ())), preferred_element_type=F32)


def _split2(x):
    hi = x.astype(BF16)
    lo = (x - hi.astype(F32)).astype(BF16)
    return hi, lo


def _split3(x):
    hi = x.astype(BF16)
    r1 = x - hi.astype(F32)
    mid = r1.astype(BF16)
    lo = (r1 - mid.astype(F32)).astype(BF16)
    return hi, mid, lo


def _dot3(a, b):
    ah, al = _split2(a)
    bh, bl = _split2(b)
    return _dot(ah, bh) + (_dot(ah, bl) + _dot(al, bh))


def _dot3_nt(a, b):
    ah, al = _split2(a)
    bh, bl = _split2(b)
    return _dot_nt(ah, bh) + (_dot_nt(ah, bl) + _dot_nt(al, bh))


def _dot_exact_lhs(sel, x):
    hi, mid, lo = _split3(x)
    return _dot(sel, hi) + (_dot(sel, mid) + _dot(sel, lo))


def _modulate(x, g, shift, scale):
    ms = jnp.mean(x * x, axis=-1, keepdims=True)
    return (x * lax.rsqrt(ms + EPS) * g) * (1.0 + scale) + shift


def _sigmoid(x):
    return 1.0 / (1.0 + jnp.exp(-x))


def _softplus(y):
    return jnp.maximum(y, 0.0) + jnp.log(1.0 + jnp.exp(-jnp.abs(y)))


def _erf(x):
    return lax.erf(x)


def _gelu(x):
    return 0.5 * x * (1.0 + _erf(x * (2.0 ** -0.5)))


def _ada_kernel(s_ref, w_ref, b_ref, o_ref):
    s = s_ref[...]
    s = s * _sigmoid(s)
    o_ref[...] = _dot3(s, w_ref[...]) + b_ref[...]


def _adaln(cond8, w_ada, b_ada):
    depth, d, n = w_ada.shape
    tn = 1024
    return pl.pallas_call(
        _ada_kernel,
        grid=(depth, n // tn),
        in_specs=[
            pl.BlockSpec((SUBLANES, d), lambda l, j: (0, 0)),
            pl.BlockSpec((None, d, tn), lambda l, j: (l, 0, j)),
            pl.BlockSpec((None, 1, tn), lambda l, j: (l, 0, j)),
        ],
        out_specs=pl.BlockSpec((None, SUBLANES, tn), lambda l, j: (l, 0, j)),
        out_shape=jax.ShapeDtypeStruct((depth, SUBLANES, n), F32),
        compiler_params=_cparams("parallel", "parallel"),
        name="adaln",
    )(cond8, w_ada, b_ada.reshape(depth, 1, n))


class _Geom:
    def __init__(self, batch, ctx_len, seq_len, d_model, tm):
        self.b, self.l, self.t, self.d = batch, ctx_len, seq_len, d_model
        self.s = ctx_len + seq_len
        self.n = batch * self.s
        self.tm = tm
        assert ctx_len % tm == 0 and seq_len % tm == 0
        self.tpb = self.s // tm
        self.nct = ctx_len // tm
        self.ntiles = self.n // tm

    def mod_spec(self, nlead=0):
        tpb, nct = self.tpb, self.nct

        def imap(*ids):
            i = ids[nlead]
            return (i // tpb, ((i % tpb) >= nct).astype(jnp.int32), 0, 0)

        return pl.BlockSpec((None, None, 6, self.d), imap)


def _modmm_kernel(x_ref, g_ref, mod_ref, w_ref, o_ref, *h_ref, which):
    h = _modulate(x_ref[...], g_ref[...], mod_ref[3 * which:3 * which + 1, :],
                  mod_ref[3 * which + 1:3 * which + 2, :])
    if h_ref:
        h_ref[0][...] = h
    o_ref[...] = _dot(h.astype(BF16), w_ref[...])


def _mod_matmul(geo, x, g, mods, w, which, emit_h):
    n, d = x.shape
    nn = w.shape[1]
    tm = geo.tm
    out_shape = [jax.ShapeDtypeStruct((n, nn), F32)]
    out_specs = [pl.BlockSpec((tm, nn), lambda i: (i, 0))]
    if emit_h:
        out_shape.append(jax.ShapeDtypeStruct((n, d), F32))
        out_specs.append(pl.BlockSpec((tm, d), lambda i: (i, 0)))
    res = pl.pallas_call(
        functools.partial(_modmm_kernel, which=which),
        grid=(geo.ntiles,),
        in_specs=[
            pl.BlockSpec((tm, d), lambda i: (i, 0)),
            pl.BlockSpec((1, d), lambda i: (0, 0)),
            geo.mod_spec(),
            pl.BlockSpec((d, nn), lambda i: (0, 0)),
        ],
        out_specs=out_specs,
        out_shape=out_shape,
        compiler_params=_cparams("parallel"),
        name="mod_matmul",
    )(x, g.reshape(1, d), mods, w)
    return res if emit_h else res[0]


def _mmres_kernel(y_ref, w_ref, x_ref, mod_ref, o_ref, *, gidx):
    acc = _dot(y_ref[...], w_ref[...])
    o_ref[...] = x_ref[...] + mod_ref[gidx:gidx + 1, :] * acc


def _matmul_res(geo, y, w, x, mods, gidx):
    n, k = y.shape
    d = x.shape[1]
    tm = geo.tm
    return pl.pallas_call(
        functools.partial(_mmres_kernel, gidx=gidx),
        grid=(geo.ntiles,),
        in_specs=[
            pl.BlockSpec((tm, k), lambda i: (i, 0)),
            pl.BlockSpec((k, d), lambda i: (0, 0)),
            pl.BlockSpec((tm, d), lambda i: (i, 0)),
            geo.mod_spec(),
        ],
        out_specs=pl.BlockSpec((tm, d), lambda i: (i, 0)),
        out_shape=jax.ShapeDtypeStruct((n, d), F32),
        compiler_params=_cparams("parallel"),
        name="matmul_res",
    )(y, w, x, mods)


def _bmm_kernel(x_ref, w_ref, o_ref, *, acts):
    j = pl.program_id(0)
    y = _dot(x_ref[...], w_ref[...])
    out = y
    for jj, a in enumerate(acts):
        if a == "tanh":
            out = jnp.where(j == jj, jnp.tanh(y), out)
        elif a == "sigmoid":
            out = jnp.where(j == jj, _sigmoid(y), out)
    o_ref[...] = out.astype(o_ref.dtype)


def _bmm(x3, w3, src, acts, out_dtype, tm):
    _, n, k = x3.shape
    nj, _, nn = w3.shape
    src = tuple(src)

    def xmap(j, i):
        idx = jnp.int32(src[0])
        for jj in range(1, nj):
            idx = jnp.where(j == jj, jnp.int32(src[jj]), idx)
        return (idx, i, 0)

    return pl.pallas_call(
        functools.partial(_bmm_kernel, acts=tuple(acts)),
        grid=(nj, n // tm),
        in_specs=[
            pl.BlockSpec((None, tm, k), xmap),
            pl.BlockSpec((None, k, nn), lambda j, i: (j, 0, 0)),
        ],
        out_specs=pl.BlockSpec((None, tm, nn), lambda j, i: (j, i, 0)),
        out_shape=jax.ShapeDtypeStruct((nj, n, nn), out_dtype),
        compiler_params=_cparams("parallel", "parallel"),
        name="bmm",
    )(x3, w3)


def _rwmix_kernel(x_ref, xp_ref, xn_ref, g_ref, mod_ref, mix_ref, o_ref, *, tpb, nct):
    i = pl.program_id(0)
    tm = x_ref.shape[0]
    g = g_ref[...]
    shift = mod_ref[0:1, :]
    scale = mod_ref[1:2, :]
    h = _modulate(x_ref[...], g, shift, scale)
    hp = _modulate(xp_ref[...], g, shift, scale)[SUBLANES - 1:SUBLANES, :]
    hn = _modulate(xn_ref[...], g, shift, scale)[0:1, :]
    it = i % tpb
    first = jnp.logical_or(it == 0, it == nct)
    last = jnp.logical_or(it == nct - 1, it == tpb - 1)
    hp = jnp.where(first, 0.0, hp)
    hn = jnp.where(last, 0.0, hn)
    rows = lax.broadcasted_iota(jnp.int32, h.shape, 0)
    prev = jnp.where(rows == 0, hp, pltpu.roll(h, 1, axis=0))
    nxt = jnp.where(rows == tm - 1, hn, pltpu.roll(h, tm - 1, axis=0))
    xx = 0.5 * (prev + nxt) - h
    for m in range(6):
        o_ref[m] = (h + xx * mix_ref[m:m + 1, :]).astype(BF16)


def _rwkv_mix(geo, x, g, mods, mix):
    n, d = x.shape
    tm = geo.tm
    r8 = tm // SUBLANES
    nblk8 = n // SUBLANES
    return pl.pallas_call(
        functools.partial(_rwmix_kernel, tpb=geo.tpb, nct=geo.nct),
        grid=(geo.ntiles,),
        in_specs=[
            pl.BlockSpec((tm, d), lambda i: (i, 0)),
            pl.BlockSpec((SUBLANES, d), lambda i: (jnp.maximum(i * r8 - 1, 0), 0)),
            pl.BlockSpec((SUBLANES, d), lambda i: (jnp.minimum((i + 1) * r8, nblk8 - 1), 0)),
            pl.BlockSpec((1, d), lambda i: (0, 0)),
            geo.mod_spec(),
            pl.BlockSpec((6, d), lambda i: (0, 0)),
        ],
        out_specs=pl.BlockSpec((6, tm, d), lambda i: (0, i, 0)),
        out_shape=jax.ShapeDtypeStruct((6, n, d), BF16),
        compiler_params=_cparams("parallel"),
        name="rwkv_mix",
    )(x, x, x, g.reshape(1, d), mods, mix)


def _head_sum(x, bd):
    hi, lo = _split2(x)
    return _dot(hi, bd) + _dot(lo, bd)


def _rwfeat_kernel(k_ref, v_ref, tl_ref, w2_ref, a2_ref, g2_ref, v2_ref, pv_ref, bd_ref,
                   *rest, has_vlora):
    if has_vlora:
        vf_ref, lw_ref, kd_ref, as_ref, kk_ref, g_ref, vo_ref = rest
    else:
        lw_ref, kd_ref, as_ref, kk_ref, g_ref = rest
    k = k_ref[...]
    tw = tl_ref[0]
    ta = tl_ref[1]
    tg = tl_ref[2]
    w0 = pv_ref[0:2, :]
    a0 = pv_ref[2:4, :]
    k_k = pv_ref[5:6, :]
    k_a = pv_ref[6:7, :]
    for z in range(2):
        sl = slice(LORA_PAD * z, LORA_PAD * (z + 1))
        lora_w = _dot(tw[:, sl], w2_ref[z])
        w = -_softplus(-(w0[z:z + 1, :] + lora_w)) - 0.5
        lw_ref[z] = -jnp.exp(w)
        a_sig = _sigmoid(a0[z:z + 1, :] + _dot(ta[:, sl], a2_ref[z]))
        as_ref[z] = a_sig
        kd_ref[z] = k * (1.0 + (a_sig - 1.0) * k_a)
    g_ref[...] = _dot(tg, g2_ref[...])
    kkr = k * k_k
    ss = _head_sum(kkr * kkr, bd_ref[...])
    kk_ref[...] = kkr * lax.rsqrt(ss + 1e-12)
    if has_vlora:
        v = v_ref[...]
        tv = tl_ref[3]
        gate = _sigmoid(pv_ref[4:5, :] + _dot(tv[:, 0:LORA_PAD], v2_ref[...]))
        vo_ref[...] = v + (vf_ref[...] - v) * gate


def _rwkv_feat(geo, rkv, tl, w2p, a2p, g2, v2p, pvec, vfirst):
    _, n, d = rkv.shape
    tm, tc = geo.tm, 512
    has_vlora = vfirst is not None
    lt = tl.shape[2]
    ii = lax.broadcasted_iota(jnp.int32, (tc, tc), 0) // RW_HEAD
    jj = lax.broadcasted_iota(jnp.int32, (tc, tc), 1) // RW_HEAD
    bd = (ii == jj).astype(BF16)
    row = lambda i, j: (i, j)
    in_specs = [
        pl.BlockSpec((None, tm, tc), lambda i, j: (1, i, j)),
        pl.BlockSpec((None, tm, tc), lambda i, j: (2, i, j)),
        pl.BlockSpec((4, tm, lt), lambda i, j: (0, i, 0)),
        pl.BlockSpec((2, LORA_PAD, tc), lambda i, j: (0, 0, j)),
        pl.BlockSpec((2, LORA_PAD, tc), lambda i, j: (0, 0, j)),
        pl.BlockSpec((lt, tc), lambda i, j: (0, j)),
        pl.BlockSpec((LORA_PAD, tc), lambda i, j: (0, j)),
        pl.BlockSpec((SUBLANES, tc), lambda i, j: (0, j)),
        pl.BlockSpec((tc, tc), lambda i, j: (0, 0)),
    ]
    args = [rkv, rkv, tl, w2p, a2p, g2, v2p, pvec, bd]
    dir_spec = pl.BlockSpec((2, tm, tc), lambda i, j: (0, i, j))
    out_specs = [dir_spec, dir_spec, dir_spec, pl.BlockSpec((tm, tc), row), pl.BlockSpec((tm, tc), row)]
    out_shape = [jax.ShapeDtypeStruct((2, n, d), F32)] * 3 + [jax.ShapeDtypeStruct((n, d), F32)] * 2
    if has_vlora:
        in_specs.append(pl.BlockSpec((tm, tc), row))
        args.append(vfirst)
        out_specs.append(pl.BlockSpec((tm, tc), row))
        out_shape.append(jax.ShapeDtypeStruct((n, d), F32))
    res = pl.pallas_call(
        functools.partial(_rwfeat_kernel, has_vlora=has_vlora),
        grid=(geo.ntiles, d // tc),
        in_specs=in_specs,
        out_specs=out_specs,
        out_shape=out_shape,
        compiler_params=_cparams("parallel", "parallel"),
        name="rwkv_feat",
    )(*args)
    if has_vlora:
        lw, kd, asg, kk, g, v = res
    else:
        lw, kd, asg, kk, g = res
        v = rkv[2]
    return lw, kd, asg, kk, g, v


def _wkv_chunk_kernel(r_ref, v_ref, kk_ref, lw_ref, kd_ref, as_ref,
                      m_ref, ga_ref, rq_ref, o0_ref, pc_ref, *, npairs):
    c = WKV_CHUNK
    c2 = 2 * c
    sgn = 1 - 2 * pl.program_id(0)
    ri = lax.broadcasted_iota(jnp.int32, (c2, c2), 0)
    ci = lax.broadcasted_iota(jnp.int32, (c2, c2), 1)
    same = (ri >= c) == (ci >= c)
    tt = jnp.where(ri >= c, ri - c, ri)
    ss = jnp.where(ci >= c, ci - c, ci)
    earlier = (ss - tt) * sgn < 0
    strict = jnp.logical_and(same, earlier)
    incl = jnp.logical_and(same, jnp.logical_or(earlier, ss == tt))
    eye = (ri == ci).astype(F32)
    r64 = lax.broadcasted_iota(jnp.int32, (c, c), 0)
    c64 = lax.broadcasted_iota(jnp.int32, (c, c), 1)
    ltri = jnp.where((c64 - r64) * sgn <= 0, 1.0, 0.0).astype(BF16)
    head0 = lax.broadcasted_iota(jnp.int32, (c, LANES), 1) < RW_HEAD
    pairs = range(npairs)

    def stack(x):
        return jnp.concatenate([jnp.where(head0, x, 0.0), jnp.where(head0, 0.0, x)], axis=0)

    def dup(x):
        return jnp.concatenate([x, x], axis=0)

    def bf(x):
        return x.astype(BF16)

    lhs, rhs, a2, bp2, kp2, vst, r2 = [], [], [], [], [], [], []
    lw_all = lw_ref[...]
    cum_all = _dot_exact_lhs(ltri, lw_all)
    for p in pairs:
        sl = slice(LANES * p, LANES * (p + 1))
        lw = lw_all[:, sl]
        cum = cum_all[:, sl]
        tot = jnp.sum(lw, axis=0, keepdims=True)
        p_inv = jnp.exp(-cum)
        p_end = jnp.exp(tot - cum)
        kk = kk_ref[:, sl]
        b = kk * as_ref[:, sl]
        kd = kd_ref[:, sl]
        a2p = stack(-kk * jnp.exp(cum - lw))
        r2p = stack(r_ref[:, sl] * jnp.exp(cum))
        a2.append(bf(a2p))
        r2.append(r2p)
        lhs.append(jnp.concatenate([a2[p], bf(r2p)], axis=0))
        rhs.append(jnp.concatenate([dup(bf(b * p_inv)), dup(bf(kd * p_inv))], axis=0))
        bp2.append(bf(stack(b * p_end)))
        kp2.append(bf(stack(kd * p_end)))
        vst.append(bf(stack(v_ref[:, sl])))
        pc_ref[:, sl] = jnp.broadcast_to(jnp.exp(tot), (SUBLANES, LANES))
    gram = [_dot_nt(lhs[p], rhs[p]) for p in pairs]
    nab = [jnp.where(strict, gram[p][0:c2, 0:c2], 0.0) for p in pairs]
    nrb = [bf(jnp.where(incl, gram[p][c2:2 * c2, 0:c2], 0.0)) for p in pairs]
    nk = [bf(jnp.concatenate([jnp.where(strict, gram[p][0:c2, c2:2 * c2], 0.0),
                              jnp.where(incl, gram[p][c2:2 * c2, c2:2 * c2], 0.0)], axis=0))
          for p in pairs]
    quads = range(npairs // 2)

    def side(x0, x1):
        return jnp.concatenate([x0, x1], axis=1)

    def diag(x0, x1):
        z0 = jnp.zeros_like(x0)
        return jnp.concatenate([side(x0, z0), side(z0, x1)], axis=0)

    def diag_halves(x):
        return diag(x[:, 0:c2], x[:, c2:2 * c2])

    def unside(xs):
        return [xs[p // 2][:, c2 * (p % 2):c2 * (p % 2 + 1)] for p in pairs]

    xv = unside([_dot(side(nk[2 * q], nk[2 * q + 1]), diag(vst[2 * q], vst[2 * q + 1]))
                 for q in quads])
    tinv = [side(eye + nab[2 * q], eye + nab[2 * q + 1]) for q in quads]
    npow = [bf(side(nab[2 * q], nab[2 * q + 1])) for q in quads]
    for _ in range(int(math.log2(c)) - 1):
        npow = [bf(_dot(npow[q], diag_halves(npow[q]))) for q in quads]
        tinv = [tinv[q] + _dot(bf(tinv[q]), diag_halves(npow[q])) for q in quads]
    tinv = unside(tinv)
    y = [_dot(bf(tinv[p]), jnp.concatenate([a2[p], bf(xv[p][0:c2, :])], axis=1)) for p in pairs]
    yb = [bf(y[p]) for p in pairs]
    z = [_dot(nrb[p], yb[p]) for p in pairs]
    mg = unside([_dot(bf(side(y[2 * q].T, y[2 * q + 1].T)), diag(bp2[2 * q], bp2[2 * q + 1]))
                 for q in quads])
    vk = unside([_dot(bf(side(vst[2 * q].astype(F32).T, vst[2 * q + 1].astype(F32).T)),
                      diag(kp2[2 * q], kp2[2 * q + 1])) for q in quads])
    for p in pairs:
        sl = slice(LANES * p, LANES * (p + 1))
        m_ref[:, sl] = mg[p][0:c2, :]
        ga_ref[:, sl] = mg[p][c2:2 * c2, :] + vk[p]
        rq_ref[:, sl] = r2[p] + z[p][:, 0:c2]
        o0 = z[p][:, c2:2 * c2] + xv[p][c2:2 * c2, :]
        o0_ref[:, sl] = o0[0:c, :] + o0[c:c2, :]


def _wkv_scan_kernel(m_ref, ga_ref, rq_ref, o0_ref, pc_ref, o_ref, g_scr, *, npairs):
    c = WKV_CHUNK

    @pl.when(pl.program_id(2) == 0)
    def _():
        g_scr[...] = jnp.zeros_like(g_scr)

    for p in range(npairs):
        sl = slice(LANES * p, LANES * (p + 1))
        g = g_scr[p]
        o_st = _dot3_nt(rq_ref[:, sl], g)
        o_ref[:, sl] = o_st[0:c, :] + o_st[c:2 * c, :] + o0_ref[:, sl]
        g_scr[p] = g * pc_ref[0:1, sl] + _dot3(g, m_ref[:, sl]) + ga_ref[:, sl]


WKV_CHUNK_LANES = 2048


def _wkv_bidir(r, v, kk, lw, kd, asg, batch, ctx_len):
    n, d = r.shape
    c = WKV_CHUNK
    s = n // batch
    ncs = s // c
    ncc = ctx_len // c
    nch = n // c
    lanes = min(WKV_CHUNK_LANES, d)
    ngrp = d // lanes
    shared = pl.BlockSpec((c, lanes), lambda z, i, j: (i, j))
    perdir = pl.BlockSpec((None, c, lanes), lambda z, i, j: (z, i, j))
    big = pl.BlockSpec((None, 2 * c, lanes), lambda z, i, j: (z, i, j))
    m_, ga_, rq_, o0_, pc_ = pl.pallas_call(
        functools.partial(_wkv_chunk_kernel, npairs=lanes // LANES),
        grid=(2, nch, ngrp),
        in_specs=[shared, shared, shared, perdir, perdir, perdir],
        out_specs=[big, big, big, perdir,
                   pl.BlockSpec((None, SUBLANES, lanes), lambda z, i, j: (z, i, j))],
        out_shape=[jax.ShapeDtypeStruct((2, nch * 2 * c, d), F32)] * 3
        + [jax.ShapeDtypeStruct((2, n, d), F32),
           jax.ShapeDtypeStruct((2, nch * SUBLANES, d), F32)],
        compiler_params=_cparams("parallel", "parallel", "parallel"),
        name="wkv_chunk",
    )(r, v, kk, lw, kd, asg)

    def cmap(z, b, cc):
        back = jnp.where(cc < ncc, ncc - 1 - cc, ncs + ncc - 1 - cc)
        return (z, b * ncs + jnp.where(z == 0, cc, back), 0)

    out = pl.pallas_call(
        functools.partial(_wkv_scan_kernel, npairs=d // LANES),
        grid=(2, batch, ncs),
        in_specs=[
            pl.BlockSpec((None, 2 * c, d), cmap),
            pl.BlockSpec((None, 2 * c, d), cmap),
            pl.BlockSpec((None, 2 * c, d), cmap),
            pl.BlockSpec((None, c, d), cmap),
            pl.BlockSpec((None, SUBLANES, d), cmap),
        ],
        out_specs=pl.BlockSpec((None, c, d), cmap),
        out_shape=jax.ShapeDtypeStruct((2, n, d), F32),
        scratch_shapes=[pltpu.VMEM((d // LANES, 2 * c, 2 * c), F32)],
        compiler_params=_cparams("parallel", "parallel", "arbitrary"),
        name="wkv_scan",
    )(m_, ga_, rq_, o0_, pc_)
    return out


def _rwread_kernel(o_ref, r_ref, kd_ref, v_ref, g_ref, pv_ref, bd_ref, y_ref):
    bd = bd_ref[...]
    wkv = o_ref[0] + o_ref[1]
    inv = 1.0 / RW_HEAD
    mu = _head_sum(wkv, bd) * inv
    dev = wkv - mu
    var = _head_sum(dev * dev, bd) * inv
    y = dev * lax.rsqrt(var + GN_EPS) * pv_ref[1:2, :] + pv_ref[2:3, :]
    rk = r_ref[...] * (kd_ref[0] + kd_ref[1]) * pv_ref[0:1, :]
    y = y + _head_sum(rk, bd) * v_ref[...]
    y_ref[...] = (y * g_ref[...]).astype(BF16)


def _rwkv_readout(geo, wkv, r, kd, v, g, pvec):
    n, d = r.shape
    tm, tc = geo.tm, 512
    ii = lax.broadcasted_iota(jnp.int32, (tc, tc), 0) // RW_HEAD
    jj = lax.broadcasted_iota(jnp.int32, (tc, tc), 1) // RW_HEAD
    bd = (ii == jj).astype(BF16)
    row = pl.BlockSpec((tm, tc), lambda i, j: (i, j))
    dirs = pl.BlockSpec((2, tm, tc), lambda i, j: (0, i, j))
    return pl.pallas_call(
        _rwread_kernel,
        grid=(geo.ntiles, d // tc),
        in_specs=[dirs, row, dirs, row, row,
                  pl.BlockSpec((SUBLANES, tc), lambda i, j: (0, j)),
                  pl.BlockSpec((tc, tc), lambda i, j: (0, 0))],
        out_specs=row,
        out_shape=jax.ShapeDtypeStruct((n, d), BF16),
        compiler_params=_cparams("parallel", "parallel"),
        name="rwkv_readout",
    )(wkv, r, kd, v, g, pvec, bd)


def _mlaqkv_kernel(z_ref, qn_ref, kvn_ref, wuq_ref, wukv_ref, gq_ref, gk_ref, cos_ref, sin_ref,
                   q_ref, k_ref, v_ref, *, q_lora, kv_lora):
    z = z_ref[...]
    cq = z[:, 0:q_lora]
    ckv = z[:, q_lora:q_lora + kv_lora]
    krot = z[:, q_lora + kv_lora:q_lora + kv_lora + LANES]

    def rms(x, g):
        ms = jnp.mean(x * x, axis=-1, keepdims=True)
        return x * lax.rsqrt(ms + EPS) * g

    qf = _dot(rms(cq, qn_ref[...]).astype(BF16), wuq_ref[...])
    kvf = _dot(rms(ckv, kvn_ref[...]).astype(BF16), wukv_ref[...])
    cos = cos_ref[...]
    sin = sin_ref[...]
    half = QK_ROPE // 2
    lane = lax.broadcasted_iota(jnp.int32, cos.shape, 1)

    def rope(x):
        up = pltpu.roll(x, LANES - half, axis=1)
        dn = pltpu.roll(x, half, axis=1)
        return x * cos + jnp.where(lane < half, up, dn) * sin

    inv_w = 1.0 / (QK_NOPE + QK_ROPE)
    gq = gq_ref[...]
    gk = gk_ref[...]
    kr_ss = jnp.sum(krot * krot, axis=-1, keepdims=True)
    for h in range(MLA_HEADS):
        o = QK_PAD * h
        qh = qf[:, o:o + QK_PAD]
        rs = lax.rsqrt(jnp.sum(qh * qh, axis=-1, keepdims=True) * inv_w + EPS)
        qn = qh * rs * gq * (ATTN_SCALE * math.log2(math.e))
        q_ref[:, o:o + QK_NOPE] = qn[:, 0:QK_NOPE].astype(BF16)
        q_ref[:, o + QK_NOPE:o + QK_PAD] = rope(qn[:, QK_NOPE:QK_PAD]).astype(BF16)
        kn = kvf[:, o:o + QK_NOPE]
        rsk = lax.rsqrt((jnp.sum(kn * kn, axis=-1, keepdims=True) + kr_ss) * inv_w + EPS)
        k_ref[:, o:o + QK_NOPE] = (kn * rsk * gk[:, 0:QK_NOPE]).astype(BF16)
        k_ref[:, o + QK_NOPE:o + QK_PAD] = rope(krot * rsk * gk[:, QK_NOPE:QK_PAD]).astype(BF16)
        v_ref[:, V_HEAD * h:V_HEAD * (h + 1)] = kvf[:, o + QK_NOPE:o + QK_PAD].astype(BF16)


def _mla_qkv(geo, z, qn, kvn, wuq_p, wukv, gq_p, gk_p, cos_t, sin_t):
    n, zw = z.shape
    tm = geo.tm
    q_lora, kv_lora = qn.shape[0], kvn.shape[0]
    hq = MLA_HEADS * QK_PAD
    tpb = geo.tpb
    full = lambda i: (0, 0)
    rowmap = lambda i: (i, 0)
    return pl.pallas_call(
        functools.partial(_mlaqkv_kernel, q_lora=q_lora, kv_lora=kv_lora),
        grid=(geo.ntiles,),
        in_specs=[
            pl.BlockSpec((tm, zw), rowmap),
            pl.BlockSpec((1, q_lora), full),
            pl.BlockSpec((1, kv_lora), full),
            pl.BlockSpec((q_lora, hq), full),
            pl.BlockSpec((kv_lora, hq), full),
            pl.BlockSpec((1, QK_PAD), full),
            pl.BlockSpec((1, QK_PAD), full),
            pl.BlockSpec((tm, LANES), lambda i: (i % tpb, 0)),
            pl.BlockSpec((tm, LANES), lambda i: (i % tpb, 0)),
        ],
        out_specs=[pl.BlockSpec((tm, hq), rowmap), pl.BlockSpec((tm, hq), rowmap),
                   pl.BlockSpec((tm, MLA_HEADS * V_HEAD), rowmap)],
        out_shape=[jax.ShapeDtypeStruct((n, hq), BF16), jax.ShapeDtypeStruct((n, hq), BF16),
                   jax.ShapeDtypeStruct((n, MLA_HEADS * V_HEAD), BF16)],
        compiler_params=_cparams("parallel"),
        name="mla_qkv",
    )(z, qn.reshape(1, -1), kvn.reshape(1, -1), wuq_p, wukv, gq_p, gk_p, cos_t, sin_t)


ATTN_SUBTILES = 2


def _attn_kernel(q_ref, k_ref, vt_ref, o_ref):
    k = k_ref[...]
    vt = vt_ref[...]
    cols = q_ref.shape[0] // ATTN_SUBTILES
    subs = range(ATTN_SUBTILES)
    s = [_dot_nt(k, q_ref[cols * i:cols * (i + 1), :]) for i in subs]
    m = [jnp.max(s[i], axis=0, keepdims=True) for i in subs]
    p = [jnp.exp2(s[i] - m[i]) for i in subs]
    l = [jnp.sum(p[i], axis=0, keepdims=True) for i in subs]
    for i in subs:
        ot = _dot(vt, p[i].astype(BF16)) / l[i]
        o_ref[cols * i:cols * (i + 1), :] = ot.T.astype(BF16)


def _attn_call(q3, k3, vt4, tq, nkeys):
    b, sq, _ = q3.shape
    return pl.pallas_call(
        _attn_kernel,
        grid=(b, MLA_HEADS, sq // tq),
        in_specs=[
            pl.BlockSpec((None, tq, QK_PAD), lambda bb, h, i: (bb, i, h)),
            pl.BlockSpec((None, nkeys, QK_PAD), lambda bb, h, i: (bb, 0, h)),
            pl.BlockSpec((None, None, V_HEAD, nkeys), lambda bb, h, i: (bb, h, 0, 0)),
        ],
        out_specs=pl.BlockSpec((None, tq, V_HEAD), lambda bb, h, i: (bb, i, h)),
        out_shape=jax.ShapeDtypeStruct((b, sq, MLA_HEADS * V_HEAD), BF16),
        compiler_params=_cparams("parallel", "parallel", "parallel"),
        name="mla_attention",
    )(q3, k3, vt4)


def _mla_attention(geo, q, k, v):
    b, s, l = geo.b, geo.s, geo.l
    q3 = q.reshape(b, s, MLA_HEADS * QK_PAD)
    k3 = k.reshape(b, s, MLA_HEADS * QK_PAD)
    vt4 = v.reshape(b, s, MLA_HEADS, V_HEAD).transpose(0, 2, 3, 1)
    o_ctx = _attn_call(q3[:, :l], k3, vt4, geo.tm, l)
    tq_lat = 2 * geo.tm if geo.t % (2 * geo.tm) == 0 else geo.tm
    o_lat = _attn_call(q3[:, l:], k3, vt4, tq_lat, s)
    return jnp.concatenate([o_ctx, o_lat], axis=1).reshape(b * s, MLA_HEADS * V_HEAD)


def _topk_rows(s, k, payload=None):
    rows = lax.broadcasted_iota(jnp.int32, s.shape, 0).astype(F32)
    big = float(s.shape[0])
    vals, idxs = [], []
    for _ in range(k):
        m = jnp.max(s, axis=0, keepdims=True)
        idx = jnp.min(jnp.where(s == m, rows, big), axis=0, keepdims=True)
        hit = rows == idx
        vals.append(m)
        if payload is None:
            idxs.append(idx)
        else:
            idxs.append(jnp.sum(jnp.where(hit, payload, 0.0), axis=0, keepdims=True))
        s = jnp.where(hit, -jnp.inf, s)
    return jnp.concatenate(vals, axis=0), jnp.concatenate(idxs, axis=0)


def _peersel_kernel(q_ref, qn_ref, keys_ref, e_ref, g_ref):
    q = q_ref[...]
    ms = jnp.mean(q * q, axis=-1, keepdims=True)
    qn = q * lax.rsqrt(ms + EPS) * qn_ref[...]
    half = D_KEY // 2
    s1 = _dot3_nt(keys_ref[0], qn[:, 0:half])
    s2 = _dot3_nt(keys_ref[1], qn[:, half:D_KEY])
    t1, i1 = _topk_rows(s1, PEER_TOPK)
    t2, i2 = _topk_rows(s2, PEER_TOPK)
    k = PEER_TOPK
    sub = lax.broadcasted_iota(jnp.int32, (SUBLANES, t1.shape[1]), 0)
    cand = [t1[0:1, :] + t2]
    cidx = [i1[0:1, :] * float(N_KEYS) + i2]
    for p in range(1, k // 2):
        live = sub < k // (p + 1)
        cand.append(jnp.where(live, t1[p:p + 1, :] + t2[0:SUBLANES, :], -jnp.inf))
        cidx.append(i1[p:p + 1, :] * float(N_KEYS) + i2[0:SUBLANES, :])
    cand.append(t1[k // 2:k, :] + t2[0:1, :])
    cidx.append(i1[k // 2:k, :] * float(N_KEYS) + i2[0:1, :])
    best, eidx = _topk_rows(jnp.concatenate(cand, axis=0), k, payload=jnp.concatenate(cidx, axis=0))
    ex = jnp.exp(best - jnp.max(best, axis=0, keepdims=True))
    g_ref[...] = ex / jnp.sum(ex, axis=0, keepdims=True)
    e_ref[...] = eidx.astype(jnp.int32)


def _peer_select(qp, q_norm, keys, tm):
    n = qp.shape[0]
    return pl.pallas_call(
        _peersel_kernel,
        grid=(n // tm, PEER_HEADS),
        in_specs=[
            pl.BlockSpec((tm, D_KEY), lambda i, h: (i, h)),
            pl.BlockSpec((1, D_KEY), lambda i, h: (0, 0)),
            pl.BlockSpec((2, N_KEYS, D_KEY // 2), lambda i, h: (0, 0, 0)),
        ],
        out_specs=[pl.BlockSpec((PEER_TOPK, tm), lambda i, h: (h, i)),
                   pl.BlockSpec((PEER_TOPK, tm), lambda i, h: (h, i))],
        out_shape=[jax.ShapeDtypeStruct((PEER_HEADS * PEER_TOPK, n), jnp.int32),
                   jax.ShapeDtypeStruct((PEER_HEADS * PEER_TOPK, n), F32)],
        compiler_params=_cparams("parallel", "parallel"),
        name="peer_select",
    )(qp, q_norm.reshape(1, D_KEY), keys)


GATHER_SLOTS = 4
SLAB_PAD = 1


def _pack_expert_table(u, v):
    ne, d = u.shape
    ub = lax.bitcast_convert_type(u.astype(BF16), jnp.uint16).astype(jnp.uint32)
    vb = lax.bitcast_convert_type(v.astype(BF16), jnp.uint16).astype(jnp.uint32)
    return ((vb << 16) | ub).reshape(ne, d // LANES, LANES)


def _peergather_kernel(idx_ref, idxn_ref, gate_ref, h_ref, x_ref, mod_ref, tab_ref, o_ref,
                       *scratch, tb):
    nsel = PEER_HEADS * PEER_TOPK
    nrow = h_ref.shape[1] // LANES
    pitch = nrow + SLAB_PAD
    ns = GATHER_SLOTS
    bufs, sem = scratch[:ns], scratch[ns]
    lane_t = lax.broadcasted_iota(jnp.int32, (nsel, tb), 1)
    g2 = mod_ref[5:6, :]
    step = pl.program_id(0)
    nsteps = pl.num_programs(0)

    def row_copy(ids_ref, t, j, slot):
        return pltpu.make_async_copy(
            tab_ref.at[ids_ref[t, j]],
            bufs[slot].at[pl.ds(j * pitch, nrow), :],
            sem.at[slot])

    def issue(ids_ref, t, slot):
        for j in range(nsel):
            row_copy(ids_ref, t, j, slot).start(priority=j % 2)

    def wait(t, slot):
        for j in range(nsel):
            row_copy(idx_ref, t, j, slot).wait()

    def packed(slot, s):
        return bufs[slot][pl.ds(s, nsel, stride=pitch), :]

    def compute(t, slot):
        hrow = h_ref[pl.ds(t, 1), :]
        acc = jnp.zeros((nsel, LANES), F32)
        for s in range(nrow):
            u = lax.bitcast_convert_type(packed(slot, s) << 16, F32)
            acc = acc + u * hrow[:, LANES * s:LANES * (s + 1)]
        dots = jnp.sum(acc, axis=-1, keepdims=True)
        gcol = jnp.sum(jnp.where(lane_t == t, gate_ref[...], 0.0), axis=-1, keepdims=True)
        coef = gcol * _gelu(dots)
        outs = []
        for s in range(nrow):
            vv = lax.bitcast_convert_type(packed(slot, s) & jnp.uint32(0xFFFF0000), F32)
            outs.append(jnp.sum(coef * vv, axis=0, keepdims=True))
        orow = jnp.concatenate(outs, axis=1)
        o_ref[pl.ds(t, 1), :] = x_ref[pl.ds(t, 1), :] + g2 * orow

    @pl.when(step == 0)
    def _():
        for s in range(ns - 1):
            issue(idx_ref, s, s)

    ngroups = tb // ns

    def body(g, carry):
        for s in range(ns):
            t = g * ns + s
            wait(t, s)
            issue(idx_ref, t + ns - 1, (s + ns - 1) % ns)
            compute(t, s)
        return carry

    lax.fori_loop(0, ngroups - 1, body, 0)
    for s in range(ns):
        t = (ngroups - 1) * ns + s
        wait(t, s)
        if s == 0:
            issue(idx_ref, tb - 1, ns - 1)
        else:
            @pl.when(step < nsteps - 1)
            def _():
                issue(idxn_ref, s - 1, s - 1)
        compute(t, s)


def _peer_gather(geo_g, eidx, gate_t, h2, x, mods, table, n):
    d = x.shape[1]
    tb = geo_g.tm
    nsel = PEER_HEADS * PEER_TOPK
    pitch = d // LANES + SLAB_PAD
    assert n % tb == 0
    nsteps = n // tb
    return pl.pallas_call(
        functools.partial(_peergather_kernel, tb=tb),
        grid=(nsteps,),
        in_specs=[
            pl.BlockSpec((tb, nsel), lambda i: (i, 0), memory_space=pltpu.SMEM),
            pl.BlockSpec((tb, nsel), lambda i: (jnp.minimum(i + 1, nsteps - 1), 0),
                         memory_space=pltpu.SMEM),
            pl.BlockSpec((nsel, tb), lambda i: (0, i)),
            pl.BlockSpec((tb, d), lambda i: (i, 0)),
            pl.BlockSpec((tb, d), lambda i: (i, 0)),
            geo_g.mod_spec(),
            pl.BlockSpec(memory_space=pl.ANY),
        ],
        out_specs=pl.BlockSpec((tb, d), lambda i: (i, 0)),
        out_shape=jax.ShapeDtypeStruct((n, d), F32),
        scratch_shapes=[pltpu.VMEM((nsel * pitch, LANES), jnp.uint32)] * GATHER_SLOTS
        + [pltpu.SemaphoreType.DMA((GATHER_SLOTS,))],
        compiler_params=_cparams("arbitrary"),
        name="peer_gather",
    )(eidx, eidx, gate_t, h2, x, mods, table)


SC_CORES = 2
SC_SUBCORES = 16
SC_LANES = 16
SC_CHUNK = 16
SC_SHARE = 0.3

_ERF_ALPHA = (0.00022905065861350646, 0.0034082910107109506, 0.050955695062380861,
              0.18520832239976145, 1.128379143519084)
_ERF_BETA = (-1.1791602954361697e-7, 0.000023547966471313185, 0.0010179625278914885,
             0.014070470171167667, 0.11098505178285362, 0.49746925110067538, 1.0)
_ERF_CLAMP = 3.832506856900711


def _erf_rational(x):
    x = jnp.minimum(jnp.maximum(x, -_ERF_CLAMP), _ERF_CLAMP)
    x2 = x * x
    p = jnp.full_like(x, _ERF_ALPHA[0])
    for c in _ERF_ALPHA[1:]:
        p = p * x2 + c
    q = jnp.full_like(x, _ERF_BETA[0])
    for c in _ERF_BETA[1:]:
        q = q * x2 + c
    return x * p / q


def _peer_sc(eidx, gate, h, table):
    n, nsel = eidx.shape
    d = h.shape[1]
    nw = SC_CORES * SC_SUBCORES
    assert n % nw == 0 and nsel % SC_CHUNK == 0 and d % SC_LANES == 0
    tpw = n // nw
    nchunk = nsel // SC_CHUNK
    nvec = d // SC_LANES
    mesh = plsc.VectorSubcoreMesh(core_axis_name="c", subcore_axis_name="s",
                                  num_cores=SC_CORES, num_subcores=SC_SUBCORES)

    @functools.partial(
        pl.kernel, mesh=mesh,
        out_type=jax.ShapeDtypeStruct((n, d), F32),
        scratch_types=[
            pltpu.VMEM((nsel,), jnp.int32),
            pltpu.VMEM((nsel,), F32),
            pltpu.VMEM((d,), F32),
            pltpu.VMEM((d,), F32),
            pltpu.VMEM((SC_CHUNK, d), jnp.uint32),
            pltpu.VMEM((SC_CHUNK, d), jnp.uint32),
            pltpu.SemaphoreType.DMA,
            pltpu.SemaphoreType.DMA,
        ],
        compiler_params=pltpu.CompilerParams(needs_layout_passes=False),
        name="peer_sc",
    )
    def sc_kernel(eidx_hbm, gate_hbm, h_hbm, tab_hbm, y_hbm, idx_v, gate_v, h_v, o_v,
                  rows0, rows1, sem0, sem1):
        wid = lax.axis_index("s") * SC_CORES + lax.axis_index("c")
        lanes = lax.iota(jnp.int32, SC_LANES)
        zero = jnp.zeros((SC_LANES,), F32)
        bufs = ((rows0, sem0), (rows1, sem1))

        def gather(c, slot):
            rows, sem = bufs[slot]
            return pltpu.make_async_copy(
                tab_hbm.at[idx_v.at[pl.ds(c * SC_CHUNK, SC_CHUNK)]], rows, sem)

        def chunk_compute(c, rows):
            def dot_body(i, accs):
                hv = h_v[pl.ds(i * SC_LANES, SC_LANES)]
                out = []
                for e in range(SC_CHUNK):
                    w = rows[e, pl.ds(i * SC_LANES, SC_LANES)]
                    out.append(accs[e] + lax.bitcast_convert_type(w << 16, F32) * hv)
                return tuple(out)

            accs = lax.fori_loop(0, nvec, dot_body, (zero,) * SC_CHUNK)
            dots = zero
            for e in range(SC_CHUNK):
                dots = jnp.where(lanes == e, jnp.sum(accs[e]), dots)
            act = 0.5 * dots * (1.0 + _erf_rational(dots * (2.0 ** -0.5)))
            coef = gate_v[pl.ds(c * SC_CHUNK, SC_CHUNK)] * act
            splat = [jnp.full((SC_LANES,), jnp.sum(jnp.where(lanes == e, coef, 0.0)), F32)
                     for e in range(SC_CHUNK)]

            def acc_body(i, carry):
                o = o_v[pl.ds(i * SC_LANES, SC_LANES)]
                for e in range(SC_CHUNK):
                    w = rows[e, pl.ds(i * SC_LANES, SC_LANES)]
                    o = o + splat[e] * lax.bitcast_convert_type(w & jnp.uint32(0xFFFF0000), F32)
                o_v[pl.ds(i * SC_LANES, SC_LANES)] = o
                return carry

            lax.fori_loop(0, nvec, acc_body, 0)

        def token_body(ti, carry):
            tok = wid * tpw + ti
            pltpu.sync_copy(eidx_hbm.at[tok], idx_v)
            pltpu.sync_copy(gate_hbm.at[tok], gate_v)
            pltpu.sync_copy(h_hbm.at[tok], h_v)

            def zero_body(i, c2):
                o_v[pl.ds(i * SC_LANES, SC_LANES)] = zero
                return c2

            lax.fori_loop(0, nvec, zero_body, 0)
            gather(0, 0).start()
            for c in range(nchunk):
                if c + 1 < nchunk:
                    gather(c + 1, (c + 1) % 2).start()
                gather(c, c % 2).wait()
                chunk_compute(c, bufs[c % 2][0])
            pltpu.sync_copy(o_v, y_hbm.at[tok])
            return carry

        lax.fori_loop(0, tpw, token_body, 0)

    return sc_kernel(eidx, gate, h, table)


def _pad_to(x, axis, size):
    pad = [(0, 0)] * x.ndim
    pad[axis] = (0, size - x.shape[axis])
    return jnp.pad(x, pad)


def _rwkv_layer(geo, xs, mods, norm1, mix, w_rkv, w_o, w0, w1, w2, a0, a1, a2, vl, g1, g2,
                k_k, k_a, r_k, ln_w, ln_b, vfirst):
    d = geo.d
    xm = _rwkv_mix(geo, xs, norm1, mods, mix)
    rkv = _bmm(xm, w_rkv.astype(BF16), (0, 2, 3), ("none",) * 3, F32, geo.tm)
    lt = g1.shape[1]
    w1c = jnp.concatenate([_pad_to(w1[0], 1, LORA_PAD), _pad_to(w1[1], 1, LORA_PAD)], axis=1)
    a1c = jnp.concatenate([_pad_to(a1[0], 1, LORA_PAD), _pad_to(a1[1], 1, LORA_PAD)], axis=1)
    if vl is None:
        v1p = jnp.zeros((d, lt), F32)
        v2p = jnp.zeros((LORA_PAD, d), F32)
        v0 = jnp.zeros((d,), F32)
    else:
        v0, v1, v2 = vl
        v1p = _pad_to(v1, 1, lt)
        v2p = _pad_to(v2, 0, LORA_PAD)
    wl1 = jnp.stack([_pad_to(w1c, 1, lt), _pad_to(a1c, 1, lt), g1, v1p]).astype(BF16)
    tl = _bmm(xm, wl1, (1, 4, 5, 3), ("tanh", "none", "sigmoid", "none"), BF16, geo.tm)
    w2p = jnp.stack([_pad_to(w2[0], 0, LORA_PAD), _pad_to(w2[1], 0, LORA_PAD)]).astype(BF16)
    a2p = jnp.stack([_pad_to(a2[0], 0, LORA_PAD), _pad_to(a2[1], 0, LORA_PAD)]).astype(BF16)
    pvec = jnp.stack([w0[0], w0[1], a0[0], a0[1], v0, k_k, k_a, jnp.zeros_like(k_k)])
    lw, kd, asg, kk, g, v = _rwkv_feat(geo, rkv, tl, w2p, a2p, g2.astype(BF16), v2p.astype(BF16),
                                      pvec, vfirst)
    r = rkv[0]
    wkv = _wkv_bidir(r, v, kk, lw, kd, asg, geo.b, geo.l)
    pv2 = _pad_to(jnp.stack([r_k, ln_w, ln_b]), 0, SUBLANES)
    y = _rwkv_readout(geo, wkv, r, kd, v, g, pv2)
    xs = _matmul_res(geo, y, w_o.astype(BF16), xs, mods, 2)
    return xs, v


def _rope_tables(geo):
    t = geo.t
    pos = jnp.arange(t)
    row = (pos // GRID_W).astype(F32)
    col = (pos % GRID_W).astype(F32)
    n_freq = QK_ROPE // 4
    inv_freq = ROPE_THETA ** (-jnp.arange(n_freq, dtype=F32) / n_freq)
    ang = jnp.concatenate([row[:, None] * inv_freq, col[:, None] * inv_freq], axis=-1)
    cos, sin = jnp.cos(ang), jnp.sin(ang)
    pad = LANES - QK_ROPE
    cos_l = jnp.concatenate([cos, cos, jnp.ones((t, pad), F32)], axis=1)
    sin_l = jnp.concatenate([-sin, sin, jnp.zeros((t, pad), F32)], axis=1)
    cos_c = jnp.ones((geo.l, LANES), F32)
    sin_c = jnp.zeros((geo.l, LANES), F32)
    return jnp.concatenate([cos_c, cos_l], axis=0), jnp.concatenate([sin_c, sin_l], axis=0)


def _mla_layer(geo, xs, mods, norm1, rope_t, w_in, q_norm, kv_norm, w_uq, w_ukv, g_q, g_k, w_o):
    q_lora, kv_lora = q_norm.shape[0], kv_norm.shape[0]
    zw = q_lora + kv_lora + LANES
    z = _mod_matmul(geo, xs, norm1, mods, _pad_to(w_in, 1, zw).astype(BF16), 0, False)
    qk = QK_NOPE + QK_ROPE
    wuq_p = _pad_to(w_uq.reshape(q_lora, MLA_HEADS, qk), 2, QK_PAD).reshape(q_lora, -1)
    gq_p = _pad_to(g_q, 0, QK_PAD).reshape(1, QK_PAD)
    gk_p = _pad_to(g_k, 0, QK_PAD).reshape(1, QK_PAD)
    q, k, v = _mla_qkv(geo, z, q_norm, kv_norm, wuq_p.astype(BF16), w_ukv.astype(BF16),
                       gq_p, gk_p, *rope_t)
    o = _mla_attention(geo, q, k, v)
    return _matmul_res(geo, o, w_o.astype(BF16), xs, mods, 2)


def _peer_layer(geo, geo_g, xs, mods, norm2, w_q, q_norm, keys, u, v):
    qp, h2 = _mod_matmul(geo, xs, norm2, mods, w_q.astype(BF16), 1, True)
    eidx_t, gate_t = _peer_select(qp, q_norm, keys, geo.tm)
    eidx = eidx_t.T
    table = _pack_expert_table(u, v)
    n, d = xs.shape
    n_sc = geo_g.tm * round(SC_SHARE * n / geo_g.tm)
    if n_sc % (SC_CORES * SC_SUBCORES) != 0:
        n_sc = 0
    n_tc = n - n_sc
    out_tc = _peer_gather(geo_g, eidx, gate_t, h2, xs, mods, table, n_tc)
    if n_sc == 0:
        return out_tc
    y_sc = _peer_sc(eidx[n_tc:], gate_t[:, n_tc:].T, h2[n_tc:], table.reshape(table.shape[0], d))
    rows = jnp.arange(n_tc, n)
    seg = ((rows % geo.s) >= geo.l).astype(jnp.int32)
    out_sc = xs[n_tc:] + mods[rows // geo.s, seg, 5] * y_sc
    return jnp.concatenate([out_tc, out_sc], axis=0)


def kernel(x, c, ctx, c_ctx, w_ada, b_ada, norm1, norm2, rw_mix, rw_wrkv, rw_wo, rw_w0, rw_w1, rw_w2, rw_a0, rw_a1, rw_a2, rw_v0, rw_v1, rw_v2, rw_g1, rw_g2, rw_kk, rw_ka, rw_rk, rw_lnw, rw_lnb, mla_win, mla_qnorm, mla_kvnorm, mla_wuq, mla_wukv, mla_gq, mla_gk, mla_wo, peer_wq, peer_qnorm, peer_keys, peer_u, peer_v):
    b, t, d = x.shape
    l = ctx.shape[1]
    depth = w_ada.shape[0]
    geo = _Geom(b, l, t, d, min(256, l))
    geo_g = _Geom(b, l, t, d, min(128, l))
    cond8 = _pad_to(jnp.concatenate([c, c_ctx[None, :]], axis=0), 0, SUBLANES)
    ada = _adaln(cond8, w_ada, b_ada).reshape(depth, SUBLANES, 6, d)
    mods_all = jnp.stack([jnp.broadcast_to(ada[:, b:b + 1], (depth, b, 6, d)), ada[:, 0:b]], axis=2)
    xs = jnp.concatenate([ctx, x], axis=1).reshape(b * (l + t), d)
    rope_t = _rope_tables(geo)
    vfirst = None
    for i in range(depth):
        j = i // 2
        mods = mods_all[i]
        if i % 2 == 0:
            vl = None if j == 0 else (rw_v0[j - 1], rw_v1[j - 1], rw_v2[j - 1])
            xs, vcur = _rwkv_layer(geo, xs, mods, norm1[i], rw_mix[j], rw_wrkv[j], rw_wo[j],
                                   rw_w0[j], rw_w1[j], rw_w2[j], rw_a0[j], rw_a1[j], rw_a2[j], vl,
                                   rw_g1[j], rw_g2[j], rw_kk[j], rw_ka[j], rw_rk[j], rw_lnw[j],
                                   rw_lnb[j], vfirst)
            if j == 0:
                vfirst = vcur
        else:
            xs = _mla_layer(geo, xs, mods, norm1[i], rope_t, mla_win[j], mla_qnorm[j],
                            mla_kvnorm[j], mla_wuq[j], mla_wukv[j], mla_gq[j], mla_gk[j], mla_wo[j])
        if i == depth - 1:
            xs = xs.reshape(b, l + t, d)[:, l:, :].reshape(b * t, d)
            geo, geo_g = _Geom(b, 0, t, d, geo.tm), _Geom(b, 0, t, d, geo_g.tm)
        xs = _peer_layer(geo, geo_g, xs, mods, norm2[i], peer_wq[i], peer_qnorm[i], peer_keys[i],
                         peer_u[i], peer_v[i])
    return xs.reshape(b, t, d)
```

```python
import functools
import math

import jax
import jax.numpy as jnp
from jax import lax
from jax.experimental import pallas as pl
from jax.experimental.pallas import tpu as pltpu
from jax.experimental.pallas import tpu_sc as plsc

F32 = jnp.float32
BF16 = jnp.bfloat16

EPS = 1e-6
GN_EPS = 64e-5
RW_HEAD = 64
WKV_CHUNK = 64
MLA_HEADS = 16
QK_NOPE = 128
QK_ROPE = 64
V_HEAD = 128
QK_PAD = 256
ROPE_THETA = 10000.0
GRID_W = 64
ATTN_SCALE = (QK_NOPE + QK_ROPE) ** -0.5
PEER_HEADS = 8
N_KEYS = 128
PEER_TOPK = 16
D_KEY = 256
LORA_PAD = 128

LANES = 128
SUBLANES = 8
VMEM_LIMIT = 56 * 1024 * 1024


def _cparams(*sem):
    return pltpu.CompilerParams(dimension_semantics=sem, vmem_limit_bytes=VMEM_LIMIT)


def _dot(a, b):
    return jnp.dot(a, b, preferred_element_type=F32)


def _dot_nt(a, b):
    return lax.dot_general(a, b, (((1,), (1,)), ((), ())), preferred_element_type=F32)


def _split2(x):
    hi = x.astype(BF16)
    lo = (x - hi.astype(F32)).astype(BF16)
    return hi, lo


def _split3(x):
    hi = x.astype(BF16)
    r1 = x - hi.astype(F32)
    mid = r1.astype(BF16)
    lo = (r1 - mid.astype(F32)).astype(BF16)
    return hi, mid, lo


def _dot3(a, b):
    ah, al = _split2(a)
    bh, bl = _split2(b)
    return _dot(ah, bh) + (_dot(ah, bl) + _dot(al, bh))


def _dot3_nt(a, b):
    ah, al = _split2(a)
    bh, bl = _split2(b)
    return _dot_nt(ah, bh) + (_dot_nt(ah, bl) + _dot_nt(al, bh))


def _dot_exact_lhs(sel, x):
    hi, mid, lo = _split3(x)
    return _dot(sel, hi) + (_dot(sel, mid) + _dot(sel, lo))


def _modulate(x, g, shift, scale):
    ms = jnp.mean(x * x, axis=-1, keepdims=True)
    return (x * lax.rsqrt(ms + EPS) * g) * (1.0 + scale) + shift


def _sigmoid(x):
    return 1.0 / (1.0 + jnp.exp(-x))


def _softplus(y):
    return jnp.maximum(y, 0.0) + jnp.log(1.0 + jnp.exp(-jnp.abs(y)))


def _erf(x):
    return lax.erf(x)


def _gelu(x):
    return 0.5 * x * (1.0 + _erf(x * (2.0 ** -0.5)))


def _ada_kernel(s_ref, w_ref, b_ref, o_ref):
    s = s_ref[...]
    s = s * _sigmoid(s)
    o_ref[...] = _dot3(s, w_ref[...]) + b_ref[...]


def _adaln(cond8, w_ada, b_ada):
    depth, d, n = w_ada.shape
    tn = 1024
    return pl.pallas_call(
        _ada_kernel,
        grid=(depth, n // tn),
        in_specs=[
            pl.BlockSpec((SUBLANES, d), lambda l, j: (0, 0)),
            pl.BlockSpec((None, d, tn), lambda l, j: (l, 0, j)),
            pl.BlockSpec((None, 1, tn), lambda l, j: (l, 0, j)),
        ],
        out_specs=pl.BlockSpec((None, SUBLANES, tn), lambda l, j: (l, 0, j)),
        out_shape=jax.ShapeDtypeStruct((depth, SUBLANES, n), F32),
        compiler_params=_cparams("parallel", "parallel"),
        name="adaln",
    )(cond8, w_ada, b_ada.reshape(depth, 1, n))


class _Geom:
    def __init__(self, batch, ctx_len, seq_len, d_model, tm):
        self.b, self.l, self.t, self.d = batch, ctx_len, seq_len, d_model
        self.s = ctx_len + seq_len
        self.n = batch * self.s
        self.tm = tm
        assert ctx_len % tm == 0 and seq_len % tm == 0
        self.tpb = self.s // tm
        self.nct = ctx_len // tm
        self.ntiles = self.n // tm

    def mod_spec(self, nlead=0):
        tpb, nct = self.tpb, self.nct

        def imap(*ids):
            i = ids[nlead]
            return (i // tpb, ((i % tpb) >= nct).astype(jnp.int32), 0, 0)

        return pl.BlockSpec((None, None, 6, self.d), imap)


def _modmm_kernel(x_ref, g_ref, mod_ref, w_ref, o_ref, *h_ref, which):
    h = _modulate(x_ref[...], g_ref[...], mod_ref[3 * which:3 * which + 1, :],
                  mod_ref[3 * which + 1:3 * which + 2, :])
    if h_ref:
        h_ref[0][...] = h
    o_ref[...] = _dot(h.astype(BF16), w_ref[...])


def _mod_matmul(geo, x, g, mods, w, which, emit_h):
    n, d = x.shape
    nn = w.shape[1]
    tm = geo.tm
    out_shape = [jax.ShapeDtypeStruct((n, nn), F32)]
    out_specs = [pl.BlockSpec((tm, nn), lambda i: (i, 0))]
    if emit_h:
        out_shape.append(jax.ShapeDtypeStruct((n, d), F32))
        out_specs.append(pl.BlockSpec((tm, d), lambda i: (i, 0)))
    res = pl.pallas_call(
        functools.partial(_modmm_kernel, which=which),
        grid=(geo.ntiles,),
        in_specs=[
            pl.BlockSpec((tm, d), lambda i: (i, 0)),
            pl.BlockSpec((1, d), lambda i: (0, 0)),
            geo.mod_spec(),
            pl.BlockSpec((d, nn), lambda i: (0, 0)),
        ],
        out_specs=out_specs,
        out_shape=out_shape,
        compiler_params=_cparams("parallel"),
        name="mod_matmul",
    )(x, g.reshape(1, d), mods, w)
    return res if emit_h else res[0]


def _mmres_kernel(y_ref, w_ref, x_ref, mod_ref, o_ref, *, gidx):
    acc = _dot(y_ref[...], w_ref[...])
    o_ref[...] = x_ref[...] + mod_ref[gidx:gidx + 1, :] * acc


def _matmul_res(geo, y, w, x, mods, gidx):
    n, k = y.shape
    d = x.shape[1]
    tm = geo.tm
    return pl.pallas_call(
        functools.partial(_mmres_kernel, gidx=gidx),
        grid=(geo.ntiles,),
        in_specs=[
            pl.BlockSpec((tm, k), lambda i: (i, 0)),
            pl.BlockSpec((k, d), lambda i: (0, 0)),
            pl.BlockSpec((tm, d), lambda i: (i, 0)),
            geo.mod_spec(),
        ],
        out_specs=pl.BlockSpec((tm, d), lambda i: (i, 0)),
        out_shape=jax.ShapeDtypeStruct((n, d), F32),
        compiler_params=_cparams("parallel"),
        name="matmul_res",
    )(y, w, x, mods)


def _bmm_kernel(x_ref, w_ref, o_ref, *, acts):
    j = pl.program_id(0)
    y = _dot(x_ref[...], w_ref[...])
    out = y
    for jj, a in enumerate(acts):
        if a == "tanh":
            out = jnp.where(j == jj, jnp.tanh(y), out)
        elif a == "sigmoid":
            out = jnp.where(j == jj, _sigmoid(y), out)
    o_ref[...] = out.astype(o_ref.dtype)


def _bmm(x3, w3, src, acts, out_dtype, tm):
    _, n, k = x3.shape
    nj, _, nn = w3.shape
    src = tuple(src)

    def xmap(j, i):
        idx = jnp.int32(src[0])
        for jj in range(1, nj):
            idx = jnp.where(j == jj, jnp.int32(src[jj]), idx)
        return (idx, i, 0)

    return pl.pallas_call(
        functools.partial(_bmm_kernel, acts=tuple(acts)),
        grid=(nj, n // tm),
        in_specs=[
            pl.BlockSpec((None, tm, k), xmap),
            pl.BlockSpec((None, k, nn), lambda j, i: (j, 0, 0)),
        ],
        out_specs=pl.BlockSpec((None, tm, nn), lambda j, i: (j, i, 0)),
        out_shape=jax.ShapeDtypeStruct((nj, n, nn), out_dtype),
        compiler_params=_cparams("parallel", "parallel"),
        name="bmm",
    )(x3, w3)


def _rwmix_kernel(x_ref, xp_ref, xn_ref, g_ref, mod_ref, mix_ref, o_ref, *, tpb, nct):
    i = pl.program_id(0)
    tm = x_ref.shape[0]
    g = g_ref[...]
    shift = mod_ref[0:1, :]
    scale = mod_ref[1:2, :]
    h = _modulate(x_ref[...], g, shift, scale)
    hp = _modulate(xp_ref[...], g, shift, scale)[SUBLANES - 1:SUBLANES, :]
    hn = _modulate(xn_ref[...], g, shift, scale)[0:1, :]
    it = i % tpb
    first = jnp.logical_or(it == 0, it == nct)
    last = jnp.logical_or(it == nct - 1, it == tpb - 1)
    hp = jnp.where(first, 0.0, hp)
    hn = jnp.where(last, 0.0, hn)
    rows = lax.broadcasted_iota(jnp.int32, h.shape, 0)
    prev = jnp.where(rows == 0, hp, pltpu.roll(h, 1, axis=0))
    nxt = jnp.where(rows == tm - 1, hn, pltpu.roll(h, tm - 1, axis=0))
    xx = 0.5 * (prev + nxt) - h
    for m in range(6):
        o_ref[m] = (h + xx * mix_ref[m:m + 1, :]).astype(BF16)


def _rwkv_mix(geo, x, g, mods, mix):
    n, d = x.shape
    tm = geo.tm
    r8 = tm // SUBLANES
    nblk8 = n // SUBLANES
    return pl.pallas_call(
        functools.partial(_rwmix_kernel, tpb=geo.tpb, nct=geo.nct),
        grid=(geo.ntiles,),
        in_specs=[
            pl.BlockSpec((tm, d), lambda i: (i, 0)),
            pl.BlockSpec((SUBLANES, d), lambda i: (jnp.maximum(i * r8 - 1, 0), 0)),
            pl.BlockSpec((SUBLANES, d), lambda i: (jnp.minimum((i + 1) * r8, nblk8 - 1), 0)),
            pl.BlockSpec((1, d), lambda i: (0, 0)),
            geo.mod_spec(),
            pl.BlockSpec((6, d), lambda i: (0, 0)),
        ],
        out_specs=pl.BlockSpec((6, tm, d), lambda i: (0, i, 0)),
        out_shape=jax.ShapeDtypeStruct((6, n, d), BF16),
        compiler_params=_cparams("parallel"),
        name="rwkv_mix",
    )(x, x, x, g.reshape(1, d), mods, mix)


def _head_sum(x, bd):
    hi, lo = _split2(x)
    return _dot(hi, bd) + _dot(lo, bd)


def _rwfeat_kernel(k_ref, v_ref, tl_ref, w2_ref, a2_ref, g2_ref, v2_ref, pv_ref, bd_ref,
                   *rest, has_vlora):
    if has_vlora:
        vf_ref, lw_ref, kd_ref, as_ref, kk_ref, g_ref, vo_ref = rest
    else:
        lw_ref, kd_ref, as_ref, kk_ref, g_ref = rest
    k = k_ref[...]
    tw = tl_ref[0]
    ta = tl_ref[1]
    tg = tl_ref[2]
    w0 = pv_ref[0:2, :]
    a0 = pv_ref[2:4, :]
    k_k = pv_ref[5:6, :]
    k_a = pv_ref[6:7, :]
    for z in range(2):
        sl = slice(LORA_PAD * z, LORA_PAD * (z + 1))
        lora_w = _dot(tw[:, sl], w2_ref[z])
        w = -_softplus(-(w0[z:z + 1, :] + lora_w)) - 0.5
        lw_ref[z] = -jnp.exp(w)
        a_sig = _sigmoid(a0[z:z + 1, :] + _dot(ta[:, sl], a2_ref[z]))
        as_ref[z] = a_sig
        kd_ref[z] = k * (1.0 + (a_sig - 1.0) * k_a)
    g_ref[...] = _dot(tg, g2_ref[...])
    kkr = k * k_k
    ss = _head_sum(kkr * kkr, bd_ref[...])
    kk_ref[...] = kkr * lax.rsqrt(ss + 1e-12)
    if has_vlora:
        v = v_ref[...]
        tv = tl_ref[3]
        gate = _sigmoid(pv_ref[4:5, :] + _dot(tv[:, 0:LORA_PAD], v2_ref[...]))
        vo_ref[...] = v + (vf_ref[...] - v) * gate


def _rwkv_feat(geo, rkv, tl, w2p, a2p, g2, v2p, pvec, vfirst):
    _, n, d = rkv.shape
    tm, tc = geo.tm, 512
    has_vlora = vfirst is not None
    lt = tl.shape[2]
    ii = lax.broadcasted_iota(jnp.int32, (tc, tc), 0) // RW_HEAD
    jj = lax.broadcasted_iota(jnp.int32, (tc, tc), 1) // RW_HEAD
    bd = (ii == jj).astype(BF16)
    row = lambda i, j: (i, j)
    in_specs = [
        pl.BlockSpec((None, tm, tc), lambda i, j: (1, i, j)),
        pl.BlockSpec((None, tm, tc), lambda i, j: (2, i, j)),
        pl.BlockSpec((4, tm, lt), lambda i, j: (0, i, 0)),
        pl.BlockSpec((2, LORA_PAD, tc), lambda i, j: (0, 0, j)),
        pl.BlockSpec((2, LORA_PAD, tc), lambda i, j: (0, 0, j)),
        pl.BlockSpec((lt, tc), lambda i, j: (0, j)),
        pl.BlockSpec((LORA_PAD, tc), lambda i, j: (0, j)),
        pl.BlockSpec((SUBLANES, tc), lambda i, j: (0, j)),
        pl.BlockSpec((tc, tc), lambda i, j: (0, 0)),
    ]
    args = [rkv, rkv, tl, w2p, a2p, g2, v2p, pvec, bd]
    dir_spec = pl.BlockSpec((2, tm, tc), lambda i, j: (0, i, j))
    out_specs = [dir_spec, dir_spec, dir_spec, pl.BlockSpec((tm, tc), row), pl.BlockSpec((tm, tc), row)]
    out_shape = [jax.ShapeDtypeStruct((2, n, d), F32)] * 3 + [jax.ShapeDtypeStruct((n, d), F32)] * 2
    if has_vlora:
        in_specs.append(pl.BlockSpec((tm, tc), row))
        args.append(vfirst)
        out_specs.append(pl.BlockSpec((tm, tc), row))
        out_shape.append(jax.ShapeDtypeStruct((n, d), F32))
    res = pl.pallas_call(
        functools.partial(_rwfeat_kernel, has_vlora=has_vlora),
        grid=(geo.ntiles, d // tc),
        in_specs=in_specs,
        out_specs=out_specs,
        out_shape=out_shape,
        compiler_params=_cparams("parallel", "parallel"),
        name="rwkv_feat",
    )(*args)
    if has_vlora:
        lw, kd, asg, kk, g, v = res
    else:
        lw, kd, asg, kk, g = res
        v = rkv[2]
    return lw, kd, asg, kk, g, v


def _wkv_chunk_kernel(r_ref, v_ref, kk_ref, lw_ref, kd_ref, as_ref,
                      m_ref, ga_ref, rq_ref, o0_ref, pc_ref, *, npairs):
    c = WKV_CHUNK
    c2 = 2 * c
    sgn = 1 - 2 * pl.program_id(0)
    ri = lax.broadcasted_iota(jnp.int32, (c2, c2), 0)
    ci = lax.broadcasted_iota(jnp.int32, (c2, c2), 1)
    same = (ri >= c) == (ci >= c)
    tt = jnp.where(ri >= c, ri - c, ri)
    ss = jnp.where(ci >= c, ci - c, ci)
    earlier = (ss - tt) * sgn < 0
    strict = jnp.logical_and(same, earlier)
    incl = jnp.logical_and(same, jnp.logical_or(earlier, ss == tt))
    eye = (ri == ci).astype(F32)
    r64 = lax.broadcasted_iota(jnp.int32, (c, c), 0)
    c64 = lax.broadcasted_iota(jnp.int32, (c, c), 1)
    ltri = jnp.where((c64 - r64) * sgn <= 0, 1.0, 0.0).astype(BF16)
    head0 = lax.broadcasted_iota(jnp.int32, (c, LANES), 1) < RW_HEAD
    pairs = range(npairs)

    def stack(x):
        return jnp.concatenate([jnp.where(head0, x, 0.0), jnp.where(head0, 0.0, x)], axis=0)

    def dup(x):
        return jnp.concatenate([x, x], axis=0)

    def bf(x):
        return x.astype(BF16)

    lhs, rhs, a2, bp2, kp2, vst, r2 = [], [], [], [], [], [], []
    lw_all = lw_ref[...]
    cum_all = _dot_exact_lhs(ltri, lw_all)
    for p in pairs:
        sl = slice(LANES * p, LANES * (p + 1))
        lw = lw_all[:, sl]
        cum = cum_all[:, sl]
        tot = jnp.sum(lw, axis=0, keepdims=True)
        p_inv = jnp.exp(-cum)
        p_end = jnp.exp(tot - cum)
        kk = kk_ref[:, sl]
        b = kk * as_ref[:, sl]
        kd = kd_ref[:, sl]
        a2p = stack(-kk * jnp.exp(cum - lw))
        r2p = stack(r_ref[:, sl] * jnp.exp(cum))
        a2.append(bf(a2p))
        r2.append(r2p)
        lhs.append(jnp.concatenate([a2[p], bf(r2p)], axis=0))
        rhs.append(jnp.concatenate([dup(bf(b * p_inv)), dup(bf(kd * p_inv))], axis=0))
        bp2.append(bf(stack(b * p_end)))
        kp2.append(bf(stack(kd * p_end)))
        vst.append(bf(stack(v_ref[:, sl])))
        pc_ref[:, sl] = jnp.broadcast_to(jnp.exp(tot), (SUBLANES, LANES))
    gram = [_dot_nt(lhs[p], rhs[p]) for p in pairs]
    nab = [jnp.where(strict, gram[p][0:c2, 0:c2], 0.0) for p in pairs]
    nrb = [bf(jnp.where(incl, gram[p][c2:2 * c2, 0:c2], 0.0)) for p in pairs]
    nk = [bf(jnp.concatenate([jnp.where(strict, gram[p][0:c2, c2:2 * c2], 0.0),
                              jnp.where(incl, gram[p][c2:2 * c2, c2:2 * c2], 0.0)], axis=0))
          for p in pairs]
    quads = range(npairs // 2)

    def side(x0, x1):
        return jnp.concatenate([x0, x1], axis=1)

    def diag(x0, x1):
        z0 = jnp.zeros_like(x0)
        return jnp.concatenate([side(x0, z0), side(z0, x1)], axis=0)

    def diag_halves(x):
        return diag(x[:, 0:c2], x[:, c2:2 * c2])

    def unside(xs):
        return [xs[p // 2][:, c2 * (p % 2):c2 * (p % 2 + 1)] for p in pairs]

    xv = unside([_dot(side(nk[2 * q], nk[2 * q + 1]), diag(vst[2 * q], vst[2 * q + 1]))
                 for q in quads])
    tinv = [side(eye + nab[2 * q], eye + nab[2 * q + 1]) for q in quads]
    npow = [bf(side(nab[2 * q], nab[2 * q + 1])) for q in quads]
    for _ in range(int(math.log2(c)) - 1):
        npow = [bf(_dot(npow[q], diag_halves(npow[q]))) for q in quads]
        tinv = [tinv[q] + _dot(bf(tinv[q]), diag_halves(npow[q])) for q in quads]
    tinv = unside(tinv)
    y = [_dot(bf(tinv[p]), jnp.concatenate([a2[p], bf(xv[p][0:c2, :])], axis=1)) for p in pairs]
    yb = [bf(y[p]) for p in pairs]
    z = [_dot(nrb[p], yb[p]) for p in pairs]
    mg = unside([_dot(bf(side(y[2 * q].T, y[2 * q + 1].T)), diag(bp2[2 * q], bp2[2 * q + 1]))
                 for q in quads])
    vk = unside([_dot(bf(side(vst[2 * q].astype(F32).T, vst[2 * q + 1].astype(F32).T)),
                      diag(kp2[2 * q], kp2[2 * q + 1])) for q in quads])
    for p in pairs:
        sl = slice(LANES * p, LANES * (p + 1))
        m_ref[:, sl] = bf(mg[p][0:c2, :])
        ga_ref[:, sl] = mg[p][c2:2 * c2, :] + vk[p]
        rq_ref[:, sl] = bf(r2[p] + z[p][:, 0:c2])
        o0 = z[p][:, c2:2 * c2] + xv[p][c2:2 * c2, :]
        o0_ref[:, sl] = o0[0:c, :] + o0[c:c2, :]


def _wkv_scan_kernel(m_ref, ga_ref, rq_ref, o0_ref, pc_ref, o_ref, g_scr, *, npairs):
    c = WKV_CHUNK

    @pl.when(pl.program_id(2) == 0)
    def _():
        g_scr[...] = jnp.zeros_like(g_scr)

    for p in range(npairs):
        sl = slice(LANES * p, LANES * (p + 1))
        g = g_scr[p]
        g_hi, g_lo = _split2(g)
        rq = rq_ref[:, sl]
        o_st = _dot_nt(rq, g_hi) + _dot_nt(rq, g_lo)
        o_ref[:, sl] = o_st[0:c, :] + o_st[c:2 * c, :] + o0_ref[:, sl]
        m = m_ref[:, sl]
        g_scr[p] = g * pc_ref[0:1, sl] + (_dot(g_hi, m) + _dot(g_lo, m)) + ga_ref[:, sl]


WKV_CHUNK_LANES = 2048


def _wkv_bidir(r, v, kk, lw, kd, asg, batch, ctx_len):
    n, d = r.shape
    c = WKV_CHUNK
    s = n // batch
    ncs = s // c
    ncc = ctx_len // c
    nch = n // c
    lanes = min(WKV_CHUNK_LANES, d)
    ngrp = d // lanes
    shared = pl.BlockSpec((c, lanes), lambda z, i, j: (i, j))
    perdir = pl.BlockSpec((None, c, lanes), lambda z, i, j: (z, i, j))
    big = pl.BlockSpec((None, 2 * c, lanes), lambda z, i, j: (z, i, j))
    m_, ga_, rq_, o0_, pc_ = pl.pallas_call(
        functools.partial(_wkv_chunk_kernel, npairs=lanes // LANES),
        grid=(2, nch, ngrp),
        in_specs=[shared, shared, shared, perdir, perdir, perdir],
        out_specs=[big, big, big, perdir,
                   pl.BlockSpec((None, SUBLANES, lanes), lambda z, i, j: (z, i, j))],
        out_shape=[jax.ShapeDtypeStruct((2, nch * 2 * c, d), dt) for dt in (BF16, F32, BF16)]
        + [jax.ShapeDtypeStruct((2, n, d), F32),
           jax.ShapeDtypeStruct((2, nch * SUBLANES, d), F32)],
        compiler_params=_cparams("parallel", "parallel", "parallel"),
        name="wkv_chunk",
    )(r, v, kk, lw, kd, asg)

    def cmap(z, b, cc):
        back = jnp.where(cc < ncc, ncc - 1 - cc, ncs + ncc - 1 - cc)
        return (z, b * ncs + jnp.where(z == 0, cc, back), 0)

    out = pl.pallas_call(
        functools.partial(_wkv_scan_kernel, npairs=d // LANES),
        grid=(2, batch, ncs),
        in_specs=[
            pl.BlockSpec((None, 2 * c, d), cmap),
            pl.BlockSpec((None, 2 * c, d), cmap),
            pl.BlockSpec((None, 2 * c, d), cmap),
            pl.BlockSpec((None, c, d), cmap),
            pl.BlockSpec((None, SUBLANES, d), cmap),
        ],
        out_specs=pl.BlockSpec((None, c, d), cmap),
        out_shape=jax.ShapeDtypeStruct((2, n, d), F32),
        scratch_shapes=[pltpu.VMEM((d // LANES, 2 * c, 2 * c), F32)],
        compiler_params=_cparams("parallel", "parallel", "arbitrary"),
        name="wkv_scan",
    )(m_, ga_, rq_, o0_, pc_)
    return out


def _rwread_kernel(o_ref, r_ref, kd_ref, v_ref, g_ref, pv_ref, bd_ref, y_ref):
    bd = bd_ref[...]
    wkv = o_ref[0] + o_ref[1]
    inv = 1.0 / RW_HEAD
    mu = _head_sum(wkv, bd) * inv
    dev = wkv - mu
    var = _head_sum(dev * dev, bd) * inv
    y = dev * lax.rsqrt(var + GN_EPS) * pv_ref[1:2, :] + pv_ref[2:3, :]
    rk = r_ref[...] * (kd_ref[0] + kd_ref[1]) * pv_ref[0:1, :]
    y = y + _head_sum(rk, bd) * v_ref[...]
    y_ref[...] = (y * g_ref[...]).astype(BF16)


def _rwkv_readout(geo, wkv, r, kd, v, g, pvec):
    n, d = r.shape
    tm, tc = geo.tm, 512
    ii = lax.broadcasted_iota(jnp.int32, (tc, tc), 0) // RW_HEAD
    jj = lax.broadcasted_iota(jnp.int32, (tc, tc), 1) // RW_HEAD
    bd = (ii == jj).astype(BF16)
    row = pl.BlockSpec((tm, tc), lambda i, j: (i, j))
    dirs = pl.BlockSpec((2, tm, tc), lambda i, j: (0, i, j))
    return pl.pallas_call(
        _rwread_kernel,
        grid=(geo.ntiles, d // tc),
        in_specs=[dirs, row, dirs, row, row,
                  pl.BlockSpec((SUBLANES, tc), lambda i, j: (0, j)),
                  pl.BlockSpec((tc, tc), lambda i, j: (0, 0))],
        out_specs=row,
        out_shape=jax.ShapeDtypeStruct((n, d), BF16),
        compiler_params=_cparams("parallel", "parallel"),
        name="rwkv_readout",
    )(wkv, r, kd, v, g, pvec, bd)


def _mlaqkv_kernel(z_ref, qn_ref, kvn_ref, wuq_ref, wukv_ref, gq_ref, gk_ref, cos_ref, sin_ref,
                   q_ref, k_ref, v_ref, *, q_lora, kv_lora):
    z = z_ref[...]
    cq = z[:, 0:q_lora]
    ckv = z[:, q_lora:q_lora + kv_lora]
    krot = z[:, q_lora + kv_lora:q_lora + kv_lora + LANES]

    def rms(x, g):
        ms = jnp.mean(x * x, axis=-1, keepdims=True)
        return x * lax.rsqrt(ms + EPS) * g

    qf = _dot(rms(cq, qn_ref[...]).astype(BF16), wuq_ref[...])
    kvf = _dot(rms(ckv, kvn_ref[...]).astype(BF16), wukv_ref[...])
    cos = cos_ref[...]
    sin = sin_ref[...]
    half = QK_ROPE // 2
    lane = lax.broadcasted_iota(jnp.int32, cos.shape, 1)

    def rope(x):
        up = pltpu.roll(x, LANES - half, axis=1)
        dn = pltpu.roll(x, half, axis=1)
        return x * cos + jnp.where(lane < half, up, dn) * sin

    inv_w = 1.0 / (QK_NOPE + QK_ROPE)
    gq = gq_ref[...]
    gk = gk_ref[...]
    kr_ss = jnp.sum(krot * krot, axis=-1, keepdims=True)
    for h in range(MLA_HEADS):
        o = QK_PAD * h
        qh = qf[:, o:o + QK_PAD]
        rs = lax.rsqrt(jnp.sum(qh * qh, axis=-1, keepdims=True) * inv_w + EPS)
        qn = qh * rs * gq * (ATTN_SCALE * math.log2(math.e))
        q_ref[:, o:o + QK_NOPE] = qn[:, 0:QK_NOPE].astype(BF16)
        q_ref[:, o + QK_NOPE:o + QK_PAD] = rope(qn[:, QK_NOPE:QK_PAD]).astype(BF16)
        kn = kvf[:, o:o + QK_NOPE]
        rsk = lax.rsqrt((jnp.sum(kn * kn, axis=-1, keepdims=True) + kr_ss) * inv_w + EPS)
        k_ref[:, o:o + QK_NOPE] = (kn * rsk * gk[:, 0:QK_NOPE]).astype(BF16)
        k_ref[:, o + QK_NOPE:o + QK_PAD] = rope(krot * rsk * gk[:, QK_NOPE:QK_PAD]).astype(BF16)
        v_ref[:, V_HEAD * h:V_HEAD * (h + 1)] = kvf[:, o + QK_NOPE:o + QK_PAD].astype(BF16)


def _mla_qkv(geo, z, qn, kvn, wuq_p, wukv, gq_p, gk_p, cos_t, sin_t):
    n, zw = z.shape
    tm = geo.tm
    q_lora, kv_lora = qn.shape[0], kvn.shape[0]
    hq = MLA_HEADS * QK_PAD
    tpb = geo.tpb
    full = lambda i: (0, 0)
    rowmap = lambda i: (i, 0)
    return pl.pallas_call(
        functools.partial(_mlaqkv_kernel, q_lora=q_lora, kv_lora=kv_lora),
        grid=(geo.ntiles,),
        in_specs=[
            pl.BlockSpec((tm, zw), rowmap),
            pl.BlockSpec((1, q_lora), full),
            pl.BlockSpec((1, kv_lora), full),
            pl.BlockSpec((q_lora, hq), full),
            pl.BlockSpec((kv_lora, hq), full),
            pl.BlockSpec((1, QK_PAD), full),
            pl.BlockSpec((1, QK_PAD), full),
            pl.BlockSpec((tm, LANES), lambda i: (i % tpb, 0)),
            pl.BlockSpec((tm, LANES), lambda i: (i % tpb, 0)),
        ],
        out_specs=[pl.BlockSpec((tm, hq), rowmap), pl.BlockSpec((tm, hq), rowmap),
                   pl.BlockSpec((tm, MLA_HEADS * V_HEAD), rowmap)],
        out_shape=[jax.ShapeDtypeStruct((n, hq), BF16), jax.ShapeDtypeStruct((n, hq), BF16),
                   jax.ShapeDtypeStruct((n, MLA_HEADS * V_HEAD), BF16)],
        compiler_params=_cparams("parallel"),
        name="mla_qkv",
    )(z, qn.reshape(1, -1), kvn.reshape(1, -1), wuq_p, wukv, gq_p, gk_p, cos_t, sin_t)


ATTN_SUBTILES = 2


def _attn_kernel(q_ref, k_ref, vt_ref, o_ref):
    k = k_ref[...]
    vt = vt_ref[...]
    cols = q_ref.shape[0] // ATTN_SUBTILES
    subs = range(ATTN_SUBTILES)
    s = [_dot_nt(k, q_ref[cols * i:cols * (i + 1), :]) for i in subs]
    m = [jnp.max(s[i], axis=0, keepdims=True) for i in subs]
    p = [jnp.exp2(s[i] - m[i]) for i in subs]
    l = [jnp.sum(p[i], axis=0, keepdims=True) for i in subs]
    for i in subs:
        ot = _dot(vt, p[i].astype(BF16)) / l[i]
        o_ref[cols * i:cols * (i + 1), :] = ot.T.astype(BF16)


def _attn_call(q3, k3, vt4, tq, nkeys):
    b, sq, _ = q3.shape
    return pl.pallas_call(
        _attn_kernel,
        grid=(b, MLA_HEADS, sq // tq),
        in_specs=[
            pl.BlockSpec((None, tq, QK_PAD), lambda bb, h, i: (bb, i, h)),
            pl.BlockSpec((None, nkeys, QK_PAD), lambda bb, h, i: (bb, 0, h)),
            pl.BlockSpec((None, None, V_HEAD, nkeys), lambda bb, h, i: (bb, h, 0, 0)),
        ],
        out_specs=pl.BlockSpec((None, tq, V_HEAD), lambda bb, h, i: (bb, i, h)),
        out_shape=jax.ShapeDtypeStruct((b, sq, MLA_HEADS * V_HEAD), BF16),
        compiler_params=_cparams("parallel", "parallel", "parallel"),
        name="mla_attention",
    )(q3, k3, vt4)


def _mla_attention(geo, q, k, v):
    b, s, l = geo.b, geo.s, geo.l
    q3 = q.reshape(b, s, MLA_HEADS * QK_PAD)
    k3 = k.reshape(b, s, MLA_HEADS * QK_PAD)
    vt4 = v.reshape(b, s, MLA_HEADS, V_HEAD).transpose(0, 2, 3, 1)
    o_ctx = _attn_call(q3[:, :l], k3, vt4, geo.tm, l)
    tq_lat = 2 * geo.tm if geo.t % (2 * geo.tm) == 0 else geo.tm
    o_lat = _attn_call(q3[:, l:], k3, vt4, tq_lat, s)
    return jnp.concatenate([o_ctx, o_lat], axis=1).reshape(b * s, MLA_HEADS * V_HEAD)


def _topk_rows(s, k, payload=None):
    rows = lax.broadcasted_iota(jnp.int32, s.shape, 0).astype(F32)
    big = float(s.shape[0])
    vals, idxs = [], []
    for _ in range(k):
        m = jnp.max(s, axis=0, keepdims=True)
        idx = jnp.min(jnp.where(s == m, rows, big), axis=0, keepdims=True)
        hit = rows == idx
        vals.append(m)
        if payload is None:
            idxs.append(idx)
        else:
            idxs.append(jnp.sum(jnp.where(hit, payload, 0.0), axis=0, keepdims=True))
        s = jnp.where(hit, -jnp.inf, s)
    return jnp.concatenate(vals, axis=0), jnp.concatenate(idxs, axis=0)


def _peersel_kernel(q_ref, qn_ref, keys_ref, e_ref, g_ref):
    q = q_ref[...]
    ms = jnp.mean(q * q, axis=-1, keepdims=True)
    qn = q * lax.rsqrt(ms + EPS) * qn_ref[...]
    half = D_KEY // 2
    s1 = _dot3_nt(keys_ref[0], qn[:, 0:half])
    s2 = _dot3_nt(keys_ref[1], qn[:, half:D_KEY])
    t1, i1 = _topk_rows(s1, PEER_TOPK)
    t2, i2 = _topk_rows(s2, PEER_TOPK)
    k = PEER_TOPK
    sub = lax.broadcasted_iota(jnp.int32, (SUBLANES, t1.shape[1]), 0)
    cand = [t1[0:1, :] + t2]
    cidx = [i1[0:1, :] * float(N_KEYS) + i2]
    for p in range(1, k // 2):
        live = sub < k // (p + 1)
        cand.append(jnp.where(live, t1[p:p + 1, :] + t2[0:SUBLANES, :], -jnp.inf))
        cidx.append(i1[p:p + 1, :] * float(N_KEYS) + i2[0:SUBLANES, :])
    cand.append(t1[k // 2:k, :] + t2[0:1, :])
    cidx.append(i1[k // 2:k, :] * float(N_KEYS) + i2[0:1, :])
    best, eidx = _topk_rows(jnp.concatenate(cand, axis=0), k, payload=jnp.concatenate(cidx, axis=0))
    ex = jnp.exp(best - jnp.max(best, axis=0, keepdims=True))
    g_ref[...] = ex / jnp.sum(ex, axis=0, keepdims=True)
    e_ref[...] = eidx.astype(jnp.int32)


def _peer_select(qp, q_norm, keys, tm):
    n = qp.shape[0]
    return pl.pallas_call(
        _peersel_kernel,
        grid=(n // tm, PEER_HEADS),
        in_specs=[
            pl.BlockSpec((tm, D_KEY), lambda i, h: (i, h)),
            pl.BlockSpec((1, D_KEY), lambda i, h: (0, 0)),
            pl.BlockSpec((2, N_KEYS, D_KEY // 2), lambda i, h: (0, 0, 0)),
        ],
        out_specs=[pl.BlockSpec((PEER_TOPK, tm), lambda i, h: (h, i)),
                   pl.BlockSpec((PEER_TOPK, tm), lambda i, h: (h, i))],
        out_shape=[jax.ShapeDtypeStruct((PEER_HEADS * PEER_TOPK, n), jnp.int32),
                   jax.ShapeDtypeStruct((PEER_HEADS * PEER_TOPK, n), F32)],
        compiler_params=_cparams("parallel", "parallel"),
        name="peer_select",
    )(qp, q_norm.reshape(1, D_KEY), keys)


GATHER_SLOTS = 4
SLAB_PAD = 1


def _pack_expert_table(u, v):
    ne, d = u.shape
    ub = lax.bitcast_convert_type(u.astype(BF16), jnp.uint16).astype(jnp.uint32)
    vb = lax.bitcast_convert_type(v.astype(BF16), jnp.uint16).astype(jnp.uint32)
    return ((vb << 16) | ub).reshape(ne, d // LANES, LANES)


def _peergather_kernel(idx_ref, idxn_ref, gate_ref, h_ref, x_ref, mod_ref, tab_ref, o_ref,
                       *scratch, tb):
    nsel = PEER_HEADS * PEER_TOPK
    nrow = h_ref.shape[1] // LANES
    pitch = nrow + SLAB_PAD
    ns = GATHER_SLOTS
    bufs, sem = scratch[:ns], scratch[ns]
    lane_t = lax.broadcasted_iota(jnp.int32, (nsel, tb), 1)
    g2 = mod_ref[5:6, :]
    step = pl.program_id(0)
    nsteps = pl.num_programs(0)

    def row_copy(ids_ref, t, j, slot):
        return pltpu.make_async_copy(
            tab_ref.at[ids_ref[t, j]],
            bufs[slot].at[pl.ds(j * pitch, nrow), :],
            sem.at[slot])

    def issue(ids_ref, t, slot):
        for j in range(nsel):
            row_copy(ids_ref, t, j, slot).start(priority=j % 2)

    def wait(t, slot):
        for j in range(nsel):
            row_copy(idx_ref, t, j, slot).wait()

    def packed(slot, s):
        return bufs[slot][pl.ds(s, nsel, stride=pitch), :]

    def compute(t, slot):
        hrow = h_ref[pl.ds(t, 1), :]
        acc = jnp.zeros((nsel, LANES), F32)
        for s in range(nrow):
            u = lax.bitcast_convert_type(packed(slot, s) << 16, F32)
            acc = acc + u * hrow[:, LANES * s:LANES * (s + 1)]
        dots = jnp.sum(acc, axis=-1, keepdims=True)
        gcol = jnp.sum(jnp.where(lane_t == t, gate_ref[...], 0.0), axis=-1, keepdims=True)
        coef = gcol * _gelu(dots)
        outs = []
        for s in range(nrow):
            vv = lax.bitcast_convert_type(packed(slot, s) & jnp.uint32(0xFFFF0000), F32)
            outs.append(jnp.sum(coef * vv, axis=0, keepdims=True))
        orow = jnp.concatenate(outs, axis=1)
        o_ref[pl.ds(t, 1), :] = x_ref[pl.ds(t, 1), :] + g2 * orow

    @pl.when(step == 0)
    def _():
        for s in range(ns - 1):
            issue(idx_ref, s, s)

    ngroups = tb // ns

    def body(g, carry):
        for s in range(ns):
            t = g * ns + s
            wait(t, s)
            issue(idx_ref, t + ns - 1, (s + ns - 1) % ns)
            compute(t, s)
        return carry

    lax.fori_loop(0, ngroups - 1, body, 0)
    for s in range(ns):
        t = (ngroups - 1) * ns + s
        wait(t, s)
        if s == 0:
            issue(idx_ref, tb - 1, ns - 1)
        else:
            @pl.when(step < nsteps - 1)
            def _():
                issue(idxn_ref, s - 1, s - 1)
        compute(t, s)


def _peer_gather(geo_g, eidx, gate_t, h2, x, mods, table, n):
    d = x.shape[1]
    tb = geo_g.tm
    nsel = PEER_HEADS * PEER_TOPK
    pitch = d // LANES + SLAB_PAD
    assert n % tb == 0
    nsteps = n // tb
    return pl.pallas_call(
        functools.partial(_peergather_kernel, tb=tb),
        grid=(nsteps,),
        in_specs=[
            pl.BlockSpec((tb, nsel), lambda i: (i, 0), memory_space=pltpu.SMEM),
            pl.BlockSpec((tb, nsel), lambda i: (jnp.minimum(i + 1, nsteps - 1), 0),
                         memory_space=pltpu.SMEM),
            pl.BlockSpec((nsel, tb), lambda i: (0, i)),
            pl.BlockSpec((tb, d), lambda i: (i, 0)),
            pl.BlockSpec((tb, d), lambda i: (i, 0)),
            geo_g.mod_spec(),
            pl.BlockSpec(memory_space=pl.ANY),
        ],
        out_specs=pl.BlockSpec((tb, d), lambda i: (i, 0)),
        out_shape=jax.ShapeDtypeStruct((n, d), F32),
        scratch_shapes=[pltpu.VMEM((nsel * pitch, LANES), jnp.uint32)] * GATHER_SLOTS
        + [pltpu.SemaphoreType.DMA((GATHER_SLOTS,))],
        compiler_params=_cparams("arbitrary"),
        name="peer_gather",
    )(eidx, eidx, gate_t, h2, x, mods, table)


SC_CORES = 2
SC_SUBCORES = 16
SC_LANES = 16
SC_CHUNK = 16
SC_SHARE = 0.3

_ERF_ALPHA = (0.00022905065861350646, 0.0034082910107109506, 0.050955695062380861,
              0.18520832239976145, 1.128379143519084)
_ERF_BETA = (-1.1791602954361697e-7, 0.000023547966471313185, 0.0010179625278914885,
             0.014070470171167667, 0.11098505178285362, 0.49746925110067538, 1.0)
_ERF_CLAMP = 3.832506856900711


def _erf_rational(x):
    x = jnp.minimum(jnp.maximum(x, -_ERF_CLAMP), _ERF_CLAMP)
    x2 = x * x
    p = jnp.full_like(x, _ERF_ALPHA[0])
    for c in _ERF_ALPHA[1:]:
        p = p * x2 + c
    q = jnp.full_like(x, _ERF_BETA[0])
    for c in _ERF_BETA[1:]:
        q = q * x2 + c
    return x * p / q


def _peer_sc(eidx, gate, h, table):
    n, nsel = eidx.shape
    d = h.shape[1]
    nw = SC_CORES * SC_SUBCORES
    assert n % (2 * nw) == 0 and nsel % SC_CHUNK == 0 and d % SC_LANES == 0
    tpw = n // nw
    nchunk = nsel // SC_CHUNK
    nvec = d // SC_LANES
    mesh = plsc.VectorSubcoreMesh(core_axis_name="c", subcore_axis_name="s",
                                  num_cores=SC_CORES, num_subcores=SC_SUBCORES)

    @functools.partial(
        pl.kernel, mesh=mesh,
        out_type=jax.ShapeDtypeStruct((n, d), F32),
        scratch_types=[
            pltpu.VMEM((nsel,), jnp.int32), pltpu.VMEM((nsel,), jnp.int32),
            pltpu.VMEM((nsel,), F32), pltpu.VMEM((nsel,), F32),
            pltpu.VMEM((d,), F32), pltpu.VMEM((d,), F32),
            pltpu.VMEM((d,), F32),
            pltpu.VMEM((SC_CHUNK, d), jnp.uint32),
            pltpu.VMEM((SC_CHUNK, d), jnp.uint32),
            pltpu.SemaphoreType.DMA, pltpu.SemaphoreType.DMA,
            pltpu.SemaphoreType.DMA, pltpu.SemaphoreType.DMA,
        ],
        compiler_params=pltpu.CompilerParams(needs_layout_passes=False),
        name="peer_sc",
    )
    def sc_kernel(eidx_hbm, gate_hbm, h_hbm, tab_hbm, y_hbm, idx_a, idx_b, gate_a, gate_b, h_a, h_b,
                  o_v, rows0, rows1, sem0, sem1, msem_a, msem_b):
        wid = lax.axis_index("s") * SC_CORES + lax.axis_index("c")
        lanes = lax.iota(jnp.int32, SC_LANES)
        zero = jnp.zeros((SC_LANES,), F32)
        bufs = ((rows0, sem0), (rows1, sem1))
        meta = ((idx_a, gate_a, h_a, msem_a), (idx_b, gate_b, h_b, msem_b))

        def meta_copies(tok, s):
            idx_v, gate_v, h_v, msem = meta[s]
            return (pltpu.make_async_copy(eidx_hbm.at[tok], idx_v, msem),
                    pltpu.make_async_copy(gate_hbm.at[tok], gate_v, msem),
                    pltpu.make_async_copy(h_hbm.at[tok], h_v, msem))

        def gather(s, c, slot):
            rows, sem = bufs[slot]
            return pltpu.make_async_copy(
                tab_hbm.at[meta[s][0].at[pl.ds(c * SC_CHUNK, SC_CHUNK)]], rows, sem)

        def chunk_compute(s, c, rows):
            _, gate_v, h_v, _ = meta[s]

            def dot_body(i, accs):
                hv = h_v[pl.ds(i * SC_LANES, SC_LANES)]
                out = []
                for e in range(SC_CHUNK):
                    w = rows[e, pl.ds(i * SC_LANES, SC_LANES)]
                    out.append(accs[e] + lax.bitcast_convert_type(w << 16, F32) * hv)
                return tuple(out)

            accs = lax.fori_loop(0, nvec, dot_body, (zero,) * SC_CHUNK)
            dots = zero
            for e in range(SC_CHUNK):
                dots = jnp.where(lanes == e, jnp.sum(accs[e]), dots)
            act = 0.5 * dots * (1.0 + _erf_rational(dots * (2.0 ** -0.5)))
            coef = gate_v[pl.ds(c * SC_CHUNK, SC_CHUNK)] * act
            splat = [jnp.full((SC_LANES,), jnp.sum(jnp.where(lanes == e, coef, 0.0)), F32)
                     for e in range(SC_CHUNK)]

            def acc_body(i, carry):
                o = o_v[pl.ds(i * SC_LANES, SC_LANES)]
                for e in range(SC_CHUNK):
                    w = rows[e, pl.ds(i * SC_LANES, SC_LANES)]
                    o = o + splat[e] * lax.bitcast_convert_type(w & jnp.uint32(0xFFFF0000), F32)
                o_v[pl.ds(i * SC_LANES, SC_LANES)] = o
                return carry

            lax.fori_loop(0, nvec, acc_body, 0)

        def zero_body(i, c2):
            o_v[pl.ds(i * SC_LANES, SC_LANES)] = zero
            return c2

        def token(tok, s, has_next):
            def when_next(fn):
                if isinstance(has_next, bool):
                    if has_next:
                        fn()
                else:
                    pl.when(has_next)(fn)

            def load_next():
                for cp in meta_copies(tok + 1, 1 - s):
                    cp.start()

            when_next(load_next)
            lax.fori_loop(0, nvec, zero_body, 0)
            for c in range(nchunk):
                if c + 1 < nchunk:
                    gather(s, c + 1, (c + 1) % 2).start()
                gather(s, c, c % 2).wait()
                chunk_compute(s, c, bufs[c % 2][0])

            def prefetch_next():
                for cp in meta_copies(tok + 1, 1 - s):
                    cp.wait()
                gather(1 - s, 0, 0).start()

            when_next(prefetch_next)
            pltpu.sync_copy(o_v, y_hbm.at[tok])

        base = wid * tpw
        for cp in meta_copies(base, 0):
            cp.start()
        for cp in meta_copies(base, 0):
            cp.wait()
        gather(0, 0, 0).start()

        def pair_body(g, carry):
            token(base + 2 * g, 0, True)
            token(base + 2 * g + 1, 1, g + 1 < tpw // 2)
            return carry

        lax.fori_loop(0, tpw // 2, pair_body, 0)

    return sc_kernel(eidx, gate, h, table)


def _pad_to(x, axis, size):
    pad = [(0, 0)] * x.ndim
    pad[axis] = (0, size - x.shape[axis])
    return jnp.pad(x, pad)


def _rwkv_layer(geo, xs, mods, norm1, mix, w_rkv, w_o, w0, w1, w2, a0, a1, a2, vl, g1, g2,
                k_k, k_a, r_k, ln_w, ln_b, vfirst):
    d = geo.d
    xm = _rwkv_mix(geo, xs, norm1, mods, mix)
    rkv = _bmm(xm, w_rkv.astype(BF16), (0, 2, 3), ("none",) * 3, F32, geo.tm)
    lt = g1.shape[1]
    w1c = jnp.concatenate([_pad_to(w1[0], 1, LORA_PAD), _pad_to(w1[1], 1, LORA_PAD)], axis=1)
    a1c = jnp.concatenate([_pad_to(a1[0], 1, LORA_PAD), _pad_to(a1[1], 1, LORA_PAD)], axis=1)
    if vl is None:
        v1p = jnp.zeros((d, lt), F32)
        v2p = jnp.zeros((LORA_PAD, d), F32)
        v0 = jnp.zeros((d,), F32)
    else:
        v0, v1, v2 = vl
        v1p = _pad_to(v1, 1, lt)
        v2p = _pad_to(v2, 0, LORA_PAD)
    wl1 = jnp.stack([_pad_to(w1c, 1, lt), _pad_to(a1c, 1, lt), g1, v1p]).astype(BF16)
    tl = _bmm(xm, wl1, (1, 4, 5, 3), ("tanh", "none", "sigmoid", "none"), BF16, geo.tm)
    w2p = jnp.stack([_pad_to(w2[0], 0, LORA_PAD), _pad_to(w2[1], 0, LORA_PAD)]).astype(BF16)
    a2p = jnp.stack([_pad_to(a2[0], 0, LORA_PAD), _pad_to(a2[1], 0, LORA_PAD)]).astype(BF16)
    pvec = jnp.stack([w0[0], w0[1], a0[0], a0[1], v0, k_k, k_a, jnp.zeros_like(k_k)])
    lw, kd, asg, kk, g, v = _rwkv_feat(geo, rkv, tl, w2p, a2p, g2.astype(BF16), v2p.astype(BF16),
                                      pvec, vfirst)
    r = rkv[0]
    wkv = _wkv_bidir(r, v, kk, lw, kd, asg, geo.b, geo.l)
    pv2 = _pad_to(jnp.stack([r_k, ln_w, ln_b]), 0, SUBLANES)
    y = _rwkv_readout(geo, wkv, r, kd, v, g, pv2)
    xs = _matmul_res(geo, y, w_o.astype(BF16), xs, mods, 2)
    return xs, v


def _rope_tables(geo):
    t = geo.t
    pos = jnp.arange(t)
    row = (pos // GRID_W).astype(F32)
    col = (pos % GRID_W).astype(F32)
    n_freq = QK_ROPE // 4
    inv_freq = ROPE_THETA ** (-jnp.arange(n_freq, dtype=F32) / n_freq)
    ang = jnp.concatenate([row[:, None] * inv_freq, col[:, None] * inv_freq], axis=-1)
    cos, sin = jnp.cos(ang), jnp.sin(ang)
    pad = LANES - QK_ROPE
    cos_l = jnp.concatenate([cos, cos, jnp.ones((t, pad), F32)], axis=1)
    sin_l = jnp.concatenate([-sin, sin, jnp.zeros((t, pad), F32)], axis=1)
    cos_c = jnp.ones((geo.l, LANES), F32)
    sin_c = jnp.zeros((geo.l, LANES), F32)
    return jnp.concatenate([cos_c, cos_l], axis=0), jnp.concatenate([sin_c, sin_l], axis=0)


def _mla_layer(geo, xs, mods, norm1, rope_t, w_in, q_norm, kv_norm, w_uq, w_ukv, g_q, g_k, w_o):
    q_lora, kv_lora = q_norm.shape[0], kv_norm.shape[0]
    zw = q_lora + kv_lora + LANES
    z = _mod_matmul(geo, xs, norm1, mods, _pad_to(w_in, 1, zw).astype(BF16), 0, False)
    qk = QK_NOPE + QK_ROPE
    wuq_p = _pad_to(w_uq.reshape(q_lora, MLA_HEADS, qk), 2, QK_PAD).reshape(q_lora, -1)
    gq_p = _pad_to(g_q, 0, QK_PAD).reshape(1, QK_PAD)
    gk_p = _pad_to(g_k, 0, QK_PAD).reshape(1, QK_PAD)
    q, k, v = _mla_qkv(geo, z, q_norm, kv_norm, wuq_p.astype(BF16), w_ukv.astype(BF16),
                       gq_p, gk_p, *rope_t)
    o = _mla_attention(geo, q, k, v)
    return _matmul_res(geo, o, w_o.astype(BF16), xs, mods, 2)


def _peer_layer(geo, geo_g, xs, mods, norm2, w_q, q_norm, keys, u, v):
    qp, h2 = _mod_matmul(geo, xs, norm2, mods, w_q.astype(BF16), 1, True)
    eidx_t, gate_t = _peer_select(qp, q_norm, keys, geo.tm)
    eidx = eidx_t.T
    table = _pack_expert_table(u, v)
    n, d = xs.shape
    n_sc = geo_g.tm * round(SC_SHARE * n / geo_g.tm)
    if n_sc % (2 * SC_CORES * SC_SUBCORES) != 0:
        n_sc = 0
    n_tc = n - n_sc
    out_tc = _peer_gather(geo_g, eidx, gate_t, h2, xs, mods, table, n_tc)
    if n_sc == 0:
        return out_tc
    y_sc = _peer_sc(eidx[n_tc:], gate_t[:, n_tc:].T, h2[n_tc:], table.reshape(table.shape[0], d))
    rows = jnp.arange(n_tc, n)
    seg = ((rows % geo.s) >= geo.l).astype(jnp.int32)
    out_sc = xs[n_tc:] + mods[rows // geo.s, seg, 5] * y_sc
    return jnp.concatenate([out_tc, out_sc], axis=0)


def kernel(x, c, ctx, c_ctx, w_ada, b_ada, norm1, norm2, rw_mix, rw_wrkv, rw_wo, rw_w0, rw_w1, rw_w2, rw_a0, rw_a1, rw_a2, rw_v0, rw_v1, rw_v2, rw_g1, rw_g2, rw_kk, rw_ka, rw_rk, rw_lnw, rw_lnb, mla_win, mla_qnorm, mla_kvnorm, mla_wuq, mla_wukv, mla_gq, mla_gk, mla_wo, peer_wq, peer_qnorm, peer_keys, peer_u, peer_v):
    b, t, d = x.shape
    l = ctx.shape[1]
    depth = w_ada.shape[0]
    geo = _Geom(b, l, t, d, min(256, l))
    geo_g = _Geom(b, l, t, d, min(128, l))
    cond8 = _pad_to(jnp.concatenate([c, c_ctx[None, :]], axis=0), 0, SUBLANES)
    ada = _adaln(cond8, w_ada, b_ada).reshape(depth, SUBLANES, 6, d)
    mods_all = jnp.stack([jnp.broadcast_to(ada[:, b:b + 1], (depth, b, 6, d)), ada[:, 0:b]], axis=2)
    xs = jnp.concatenate([ctx, x], axis=1).reshape(b * (l + t), d)
    rope_t = _rope_tables(geo)
    vfirst = None
    for i in range(depth):
        j = i // 2
        mods = mods_all[i]
        if i % 2 == 0:
            vl = None if j == 0 else (rw_v0[j - 1], rw_v1[j - 1], rw_v2[j - 1])
            xs, vcur = _rwkv_layer(geo, xs, mods, norm1[i], rw_mix[j], rw_wrkv[j], rw_wo[j],
                                   rw_w0[j], rw_w1[j], rw_w2[j], rw_a0[j], rw_a1[j], rw_a2[j], vl,
                                   rw_g1[j], rw_g2[j], rw_kk[j], rw_ka[j], rw_rk[j], rw_lnw[j],
                                   rw_lnb[j], vfirst)
            if j == 0:
                vfirst = vcur
        else:
            xs = _mla_layer(geo, xs, mods, norm1[i], rope_t, mla_win[j], mla_qnorm[j],
                            mla_kvnorm[j], mla_wuq[j], mla_wukv[j], mla_gq[j], mla_gk[j], mla_wo[j])
        if i == depth - 1:
            xs = xs.reshape(b, l + t, d)[:, l:, :].reshape(b * t, d)
            geo, geo_g = _Geom(b, 0, t, d, geo.tm), _Geom(b, 0, t, d, geo_g.tm)
        xs = _peer_layer(geo, geo_g, xs, mods, norm2[i], peer_wq[i], peer_qnorm[i], peer_keys[i],
                         peer_u[i], peer_v[i])
    return xs.reshape(b, t, d)
```

```python
import functools
import math

import jax
import jax.numpy as jnp
from jax import lax
from jax.experimental import pallas as pl
from jax.experimental.pallas import tpu as pltpu
from jax.experimental.pallas import tpu_sc as plsc

F32 = jnp.float32
BF16 = jnp.bfloat16

EPS = 1e-6
GN_EPS = 64e-5
RW_HEAD = 64
WKV_CHUNK = 64
MLA_HEADS = 16
QK_NOPE = 128
QK_ROPE = 64
V_HEAD = 128
QK_PAD = 256
ROPE_THETA = 10000.0
GRID_W = 64
ATTN_SCALE = (QK_NOPE + QK_ROPE) ** -0.5
PEER_HEADS = 8
N_KEYS = 128
PEER_TOPK = 16
D_KEY = 256
LORA_PAD = 128

LANES = 128
SUBLANES = 8
VMEM_LIMIT = 56 * 1024 * 1024


def _cparams(*sem):
    return pltpu.CompilerParams(dimension_semantics=sem, vmem_limit_bytes=VMEM_LIMIT)


def _dot(a, b):
    return jnp.dot(a, b, preferred_element_type=F32)


def _dot_nt(a, b):
    return lax.dot_general(a, b, (((1,), (1,)), ((), ())), preferred_element_type=F32)


def _split2(x):
    hi = x.astype(BF16)
    lo = (x - hi.astype(F32)).astype(BF16)
    return hi, lo


def _split3(x):
    hi = x.astype(BF16)
    r1 = x - hi.astype(F32)
    mid = r1.astype(BF16)
    lo = (r1 - mid.astype(F32)).astype(BF16)
    return hi, mid, lo


def _dot3(a, b):
    ah, al = _split2(a)
    bh, bl = _split2(b)
    return _dot(ah, bh) + (_dot(ah, bl) + _dot(al, bh))


def _dot3_nt(a, b):
    ah, al = _split2(a)
    bh, bl = _split2(b)
    return _dot_nt(ah, bh) + (_dot_nt(ah, bl) + _dot_nt(al, bh))


def _dot_exact_lhs(sel, x):
    hi, mid, lo = _split3(x)
    return _dot(sel, hi) + (_dot(sel, mid) + _dot(sel, lo))


def _modulate(x, g, shift, scale):
    ms = jnp.mean(x * x, axis=-1, keepdims=True)
    return (x * lax.rsqrt(ms + EPS) * g) * (1.0 + scale) + shift


def _sigmoid(x):
    return 1.0 / (1.0 + jnp.exp(-x))


def _softplus(y):
    return jnp.maximum(y, 0.0) + jnp.log(1.0 + jnp.exp(-jnp.abs(y)))


def _erf(x):
    return lax.erf(x)


def _gelu(x):
    return 0.5 * x * (1.0 + _erf(x * (2.0 ** -0.5)))


def _ada_kernel(s_ref, w_ref, b_ref, o_ref):
    s = s_ref[...]
    s = s * _sigmoid(s)
    o_ref[...] = _dot3(s, w_ref[...]) + b_ref[...]


def _adaln(cond8, w_ada, b_ada):
    depth, d, n = w_ada.shape
    tn = 1024
    return pl.pallas_call(
        _ada_kernel,
        grid=(depth, n // tn),
        in_specs=[
            pl.BlockSpec((SUBLANES, d), lambda l, j: (0, 0)),
            pl.BlockSpec((None, d, tn), lambda l, j: (l, 0, j)),
            pl.BlockSpec((None, 1, tn), lambda l, j: (l, 0, j)),
        ],
        out_specs=pl.BlockSpec((None, SUBLANES, tn), lambda l, j: (l, 0, j)),
        out_shape=jax.ShapeDtypeStruct((depth, SUBLANES, n), F32),
        compiler_params=_cparams("parallel", "parallel"),
        name="adaln",
    )(cond8, w_ada, b_ada.reshape(depth, 1, n))


class _Geom:
    def __init__(self, batch, ctx_len, seq_len, d_model, tm):
        self.b, self.l, self.t, self.d = batch, ctx_len, seq_len, d_model
        self.s = ctx_len + seq_len
        self.n = batch * self.s
        self.tm = tm
        assert ctx_len % tm == 0 and seq_len % tm == 0
        self.tpb = self.s // tm
        self.nct = ctx_len // tm
        self.ntiles = self.n // tm

    def mod_spec(self, nlead=0):
        tpb, nct = self.tpb, self.nct

        def imap(*ids):
            i = ids[nlead]
            return (i // tpb, ((i % tpb) >= nct).astype(jnp.int32), 0, 0)

        return pl.BlockSpec((None, None, 6, self.d), imap)


def _modmm_kernel(x_ref, g_ref, mod_ref, w_ref, o_ref, *h_ref, which):
    h = _modulate(x_ref[...], g_ref[...], mod_ref[3 * which:3 * which + 1, :],
                  mod_ref[3 * which + 1:3 * which + 2, :])
    if h_ref:
        h_ref[0][...] = h
    o_ref[...] = _dot(h.astype(BF16), w_ref[...])


def _mod_matmul(geo, x, g, mods, w, which, emit_h):
    n, d = x.shape
    nn = w.shape[1]
    tm = geo.tm
    out_shape = [jax.ShapeDtypeStruct((n, nn), F32)]
    out_specs = [pl.BlockSpec((tm, nn), lambda i: (i, 0))]
    if emit_h:
        out_shape.append(jax.ShapeDtypeStruct((n, d), F32))
        out_specs.append(pl.BlockSpec((tm, d), lambda i: (i, 0)))
    res = pl.pallas_call(
        functools.partial(_modmm_kernel, which=which),
        grid=(geo.ntiles,),
        in_specs=[
            pl.BlockSpec((tm, d), lambda i: (i, 0)),
            pl.BlockSpec((1, d), lambda i: (0, 0)),
            geo.mod_spec(),
            pl.BlockSpec((d, nn), lambda i: (0, 0)),
        ],
        out_specs=out_specs,
        out_shape=out_shape,
        compiler_params=_cparams("parallel"),
        name="mod_matmul",
    )(x, g.reshape(1, d), mods, w)
    return res if emit_h else res[0]


def _mmres_kernel(y_ref, w_ref, x_ref, mod_ref, o_ref, *, gidx):
    acc = _dot(y_ref[...], w_ref[...])
    o_ref[...] = x_ref[...] + mod_ref[gidx:gidx + 1, :] * acc


def _matmul_res(geo, y, w, x, mods, gidx):
    n, k = y.shape
    d = x.shape[1]
    tm = geo.tm
    return pl.pallas_call(
        functools.partial(_mmres_kernel, gidx=gidx),
        grid=(geo.ntiles,),
        in_specs=[
            pl.BlockSpec((tm, k), lambda i: (i, 0)),
            pl.BlockSpec((k, d), lambda i: (0, 0)),
            pl.BlockSpec((tm, d), lambda i: (i, 0)),
            geo.mod_spec(),
        ],
        out_specs=pl.BlockSpec((tm, d), lambda i: (i, 0)),
        out_shape=jax.ShapeDtypeStruct((n, d), F32),
        compiler_params=_cparams("parallel"),
        name="matmul_res",
    )(y, w, x, mods)


def _bmm_kernel(x_ref, w_ref, o_ref, *, acts):
    j = pl.program_id(0)
    y = _dot(x_ref[...], w_ref[...])
    out = y
    for jj, a in enumerate(acts):
        if a == "tanh":
            out = jnp.where(j == jj, jnp.tanh(y), out)
        elif a == "sigmoid":
            out = jnp.where(j == jj, _sigmoid(y), out)
    o_ref[...] = out.astype(o_ref.dtype)


def _bmm(x3, w3, src, acts, out_dtype, tm):
    _, n, k = x3.shape
    nj, _, nn = w3.shape
    src = tuple(src)

    def xmap(j, i):
        idx = jnp.int32(src[0])
        for jj in range(1, nj):
            idx = jnp.where(j == jj, jnp.int32(src[jj]), idx)
        return (idx, i, 0)

    return pl.pallas_call(
        functools.partial(_bmm_kernel, acts=tuple(acts)),
        grid=(nj, n // tm),
        in_specs=[
            pl.BlockSpec((None, tm, k), xmap),
            pl.BlockSpec((None, k, nn), lambda j, i: (j, 0, 0)),
        ],
        out_specs=pl.BlockSpec((None, tm, nn), lambda j, i: (j, i, 0)),
        out_shape=jax.ShapeDtypeStruct((nj, n, nn), out_dtype),
        compiler_params=_cparams("parallel", "parallel"),
        name="bmm",
    )(x3, w3)


def _rwmix_kernel(x_ref, xp_ref, xn_ref, g_ref, mod_ref, mix_ref, o_ref, *, tpb, nct):
    i = pl.program_id(0)
    tm = x_ref.shape[0]
    g = g_ref[...]
    shift = mod_ref[0:1, :]
    scale = mod_ref[1:2, :]
    h = _modulate(x_ref[...], g, shift, scale)
    hp = _modulate(xp_ref[...], g, shift, scale)[SUBLANES - 1:SUBLANES, :]
    hn = _modulate(xn_ref[...], g, shift, scale)[0:1, :]
    it = i % tpb
    first = jnp.logical_or(it == 0, it == nct)
    last = jnp.logical_or(it == nct - 1, it == tpb - 1)
    hp = jnp.where(first, 0.0, hp)
    hn = jnp.where(last, 0.0, hn)
    rows = lax.broadcasted_iota(jnp.int32, h.shape, 0)
    prev = jnp.where(rows == 0, hp, pltpu.roll(h, 1, axis=0))
    nxt = jnp.where(rows == tm - 1, hn, pltpu.roll(h, tm - 1, axis=0))
    xx = 0.5 * (prev + nxt) - h
    for m in range(6):
        o_ref[m] = (h + xx * mix_ref[m:m + 1, :]).astype(BF16)


def _rwkv_mix(geo, x, g, mods, mix):
    n, d = x.shape
    tm = geo.tm
    r8 = tm // SUBLANES
    nblk8 = n // SUBLANES
    return pl.pallas_call(
        functools.partial(_rwmix_kernel, tpb=geo.tpb, nct=geo.nct),
        grid=(geo.ntiles,),
        in_specs=[
            pl.BlockSpec((tm, d), lambda i: (i, 0)),
            pl.BlockSpec((SUBLANES, d), lambda i: (jnp.maximum(i * r8 - 1, 0), 0)),
            pl.BlockSpec((SUBLANES, d), lambda i: (jnp.minimum((i + 1) * r8, nblk8 - 1), 0)),
            pl.BlockSpec((1, d), lambda i: (0, 0)),
            geo.mod_spec(),
            pl.BlockSpec((6, d), lambda i: (0, 0)),
        ],
        out_specs=pl.BlockSpec((6, tm, d), lambda i: (0, i, 0)),
        out_shape=jax.ShapeDtypeStruct((6, n, d), BF16),
        compiler_params=_cparams("parallel"),
        name="rwkv_mix",
    )(x, x, x, g.reshape(1, d), mods, mix)


def _head_sum(x, bd):
    hi, lo = _split2(x)
    return _dot(hi, bd) + _dot(lo, bd)


def _rwfeat_kernel(k_ref, v_ref, tl_ref, w2_ref, a2_ref, g2_ref, v2_ref, pv_ref, bd_ref,
                   *rest, has_vlora):
    if has_vlora:
        vf_ref, lw_ref, kd_ref, as_ref, kk_ref, g_ref, vo_ref = rest
    else:
        lw_ref, kd_ref, as_ref, kk_ref, g_ref = rest
    k = k_ref[...]
    tw = tl_ref[0]
    ta = tl_ref[1]
    tg = tl_ref[2]
    w0 = pv_ref[0:2, :]
    a0 = pv_ref[2:4, :]
    k_k = pv_ref[5:6, :]
    k_a = pv_ref[6:7, :]
    for z in range(2):
        sl = slice(LORA_PAD * z, LORA_PAD * (z + 1))
        lora_w = _dot(tw[:, sl], w2_ref[z])
        w = -_softplus(-(w0[z:z + 1, :] + lora_w)) - 0.5
        lw_ref[z] = -jnp.exp(w)
        a_sig = _sigmoid(a0[z:z + 1, :] + _dot(ta[:, sl], a2_ref[z]))
        as_ref[z] = a_sig
        kd_ref[z] = k * (1.0 + (a_sig - 1.0) * k_a)
    g_ref[...] = _dot(tg, g2_ref[...])
    kkr = k * k_k
    ss = _head_sum(kkr * kkr, bd_ref[...])
    kk_ref[...] = kkr * lax.rsqrt(ss + 1e-12)
    if has_vlora:
        v = v_ref[...]
        tv = tl_ref[3]
        gate = _sigmoid(pv_ref[4:5, :] + _dot(tv[:, 0:LORA_PAD], v2_ref[...]))
        vo_ref[...] = v + (vf_ref[...] - v) * gate


def _rwkv_feat(geo, rkv, tl, w2p, a2p, g2, v2p, pvec, vfirst):
    _, n, d = rkv.shape
    tm, tc = geo.tm, 512
    has_vlora = vfirst is not None
    lt = tl.shape[2]
    ii = lax.broadcasted_iota(jnp.int32, (tc, tc), 0) // RW_HEAD
    jj = lax.broadcasted_iota(jnp.int32, (tc, tc), 1) // RW_HEAD
    bd = (ii == jj).astype(BF16)
    row = lambda i, j: (i, j)
    in_specs = [
        pl.BlockSpec((None, tm, tc), lambda i, j: (1, i, j)),
        pl.BlockSpec((None, tm, tc), lambda i, j: (2, i, j)),
        pl.BlockSpec((4, tm, lt), lambda i, j: (0, i, 0)),
        pl.BlockSpec((2, LORA_PAD, tc), lambda i, j: (0, 0, j)),
        pl.BlockSpec((2, LORA_PAD, tc), lambda i, j: (0, 0, j)),
        pl.BlockSpec((lt, tc), lambda i, j: (0, j)),
        pl.BlockSpec((LORA_PAD, tc), lambda i, j: (0, j)),
        pl.BlockSpec((SUBLANES, tc), lambda i, j: (0, j)),
        pl.BlockSpec((tc, tc), lambda i, j: (0, 0)),
    ]
    args = [rkv, rkv, tl, w2p, a2p, g2, v2p, pvec, bd]
    dir_spec = pl.BlockSpec((2, tm, tc), lambda i, j: (0, i, j))
    out_specs = [dir_spec, dir_spec, dir_spec, pl.BlockSpec((tm, tc), row), pl.BlockSpec((tm, tc), row)]
    out_shape = [jax.ShapeDtypeStruct((2, n, d), F32)] * 3 + [jax.ShapeDtypeStruct((n, d), F32)] * 2
    if has_vlora:
        in_specs.append(pl.BlockSpec((tm, tc), row))
        args.append(vfirst)
        out_specs.append(pl.BlockSpec((tm, tc), row))
        out_shape.append(jax.ShapeDtypeStruct((n, d), F32))
    res = pl.pallas_call(
        functools.partial(_rwfeat_kernel, has_vlora=has_vlora),
        grid=(geo.ntiles, d // tc),
        in_specs=in_specs,
        out_specs=out_specs,
        out_shape=out_shape,
        compiler_params=_cparams("parallel", "parallel"),
        name="rwkv_feat",
    )(*args)
    if has_vlora:
        lw, kd, asg, kk, g, v = res
    else:
        lw, kd, asg, kk, g = res
        v = rkv[2]
    return lw, kd, asg, kk, g, v


def _wkv_chunk_kernel(r_ref, v_ref, kk_ref, lw_ref, kd_ref, as_ref,
                      m_ref, ga_ref, rq_ref, o0_ref, pc_ref, *, npairs):
    c = WKV_CHUNK
    c2 = 2 * c
    sgn = 1 - 2 * pl.program_id(0)
    ri = lax.broadcasted_iota(jnp.int32, (c2, c2), 0)
    ci = lax.broadcasted_iota(jnp.int32, (c2, c2), 1)
    same = (ri >= c) == (ci >= c)
    tt = jnp.where(ri >= c, ri - c, ri)
    ss = jnp.where(ci >= c, ci - c, ci)
    earlier = (ss - tt) * sgn < 0
    strict = jnp.logical_and(same, earlier)
    incl = jnp.logical_and(same, jnp.logical_or(earlier, ss == tt))
    eye = (ri == ci).astype(F32)
    r64 = lax.broadcasted_iota(jnp.int32, (c, c), 0)
    c64 = lax.broadcasted_iota(jnp.int32, (c, c), 1)
    ltri = jnp.where((c64 - r64) * sgn <= 0, 1.0, 0.0).astype(BF16)
    head0 = lax.broadcasted_iota(jnp.int32, (c, LANES), 1) < RW_HEAD
    pairs = range(npairs)

    def stack(x):
        return jnp.concatenate([jnp.where(head0, x, 0.0), jnp.where(head0, 0.0, x)], axis=0)

    def dup(x):
        return jnp.concatenate([x, x], axis=0)

    def bf(x):
        return x.astype(BF16)

    lhs, rhs, a2, bp2, kp2, vst, r2 = [], [], [], [], [], [], []
    lw_all = lw_ref[...]
    cum_all = _dot_exact_lhs(ltri, lw_all)
    for p in pairs:
        sl = slice(LANES * p, LANES * (p + 1))
        lw = lw_all[:, sl]
        cum = cum_all[:, sl]
        tot = jnp.sum(lw, axis=0, keepdims=True)
        p_inv = jnp.exp(-cum)
        p_end = jnp.exp(tot - cum)
        kk = kk_ref[:, sl]
        b = kk * as_ref[:, sl]
        kd = kd_ref[:, sl]
        a2p = stack(-kk * jnp.exp(cum - lw))
        r2p = stack(r_ref[:, sl] * jnp.exp(cum))
        a2.append(bf(a2p))
        r2.append(r2p)
        lhs.append(jnp.concatenate([a2[p], bf(r2p)], axis=0))
        rhs.append(jnp.concatenate([dup(bf(b * p_inv)), dup(bf(kd * p_inv))], axis=0))
        bp2.append(bf(stack(b * p_end)))
        kp2.append(bf(stack(kd * p_end)))
        vst.append(bf(stack(v_ref[:, sl])))
        pc_ref[:, sl] = jnp.broadcast_to(jnp.exp(tot), (SUBLANES, LANES))
    gram = [_dot_nt(lhs[p], rhs[p]) for p in pairs]
    nab = [jnp.where(strict, gram[p][0:c2, 0:c2], 0.0) for p in pairs]
    nrb = [bf(jnp.where(incl, gram[p][c2:2 * c2, 0:c2], 0.0)) for p in pairs]
    nk = [bf(jnp.concatenate([jnp.where(strict, gram[p][0:c2, c2:2 * c2], 0.0),
                              jnp.where(incl, gram[p][c2:2 * c2, c2:2 * c2], 0.0)], axis=0))
          for p in pairs]
    quads = range(npairs // 2)

    def side(x0, x1):
        return jnp.concatenate([x0, x1], axis=1)

    def diag(x0, x1):
        z0 = jnp.zeros_like(x0)
        return jnp.concatenate([side(x0, z0), side(z0, x1)], axis=0)

    def diag_halves(x):
        return diag(x[:, 0:c2], x[:, c2:2 * c2])

    def unside(xs):
        return [xs[p // 2][:, c2 * (p % 2):c2 * (p % 2 + 1)] for p in pairs]

    xv = unside([_dot(side(nk[2 * q], nk[2 * q + 1]), diag(vst[2 * q], vst[2 * q + 1]))
                 for q in quads])
    tinv = [side(eye + nab[2 * q], eye + nab[2 * q + 1]) for q in quads]
    npow = [bf(side(nab[2 * q], nab[2 * q + 1])) for q in quads]
    for _ in range(int(math.log2(c)) - 1):
        npow = [bf(_dot(npow[q], diag_halves(npow[q]))) for q in quads]
        tinv = [tinv[q] + _dot(bf(tinv[q]), diag_halves(npow[q])) for q in quads]
    tinv = unside(tinv)
    y = [_dot(bf(tinv[p]), jnp.concatenate([a2[p], bf(xv[p][0:c2, :])], axis=1)) for p in pairs]
    yb = [bf(y[p]) for p in pairs]
    z = [_dot(nrb[p], yb[p]) for p in pairs]
    mg = unside([_dot(bf(side(y[2 * q].T, y[2 * q + 1].T)), diag(bp2[2 * q], bp2[2 * q + 1]))
                 for q in quads])
    vk = unside([_dot(bf(side(vst[2 * q].astype(F32).T, vst[2 * q + 1].astype(F32).T)),
                      diag(kp2[2 * q], kp2[2 * q + 1])) for q in quads])
    for p in pairs:
        sl = slice(LANES * p, LANES * (p + 1))
        m_ref[:, sl] = bf(mg[p][0:c2, :])
        ga_ref[:, sl] = mg[p][c2:2 * c2, :] + vk[p]
        rq_ref[:, sl] = bf(r2[p] + z[p][:, 0:c2])
        o0 = z[p][:, c2:2 * c2] + xv[p][c2:2 * c2, :]
        o0_ref[:, sl] = o0[0:c, :] + o0[c:c2, :]


def _wkv_scan_kernel(m_ref, ga_ref, rq_ref, o0_ref, pc_ref, o_ref, g_scr, *, npairs):
    c = WKV_CHUNK

    @pl.when(pl.program_id(2) == 0)
    def _():
        g_scr[...] = jnp.zeros_like(g_scr)

    for p in range(npairs):
        sl = slice(LANES * p, LANES * (p + 1))
        g = g_scr[p]
        g_hi, g_lo = _split2(g)
        rq = rq_ref[:, sl]
        o_st = _dot_nt(rq, g_hi) + _dot_nt(rq, g_lo)
        o_ref[:, sl] = o_st[0:c, :] + o_st[c:2 * c, :] + o0_ref[:, sl]
        m = m_ref[:, sl]
        g_scr[p] = g * pc_ref[0:1, sl] + (_dot(g_hi, m) + _dot(g_lo, m)) + ga_ref[:, sl]


WKV_CHUNK_LANES = 2048


def _wkv_bidir(r, v, kk, lw, kd, asg, batch, ctx_len):
    n, d = r.shape
    c = WKV_CHUNK
    s = n // batch
    ncs = s // c
    ncc = ctx_len // c
    nch = n // c
    lanes = min(WKV_CHUNK_LANES, d)
    ngrp = d // lanes
    shared = pl.BlockSpec((c, lanes), lambda z, i, j: (i, j))
    perdir = pl.BlockSpec((None, c, lanes), lambda z, i, j: (z, i, j))
    big = pl.BlockSpec((None, 2 * c, lanes), lambda z, i, j: (z, i, j))
    m_, ga_, rq_, o0_, pc_ = pl.pallas_call(
        functools.partial(_wkv_chunk_kernel, npairs=lanes // LANES),
        grid=(2, nch, ngrp),
        in_specs=[shared, shared, shared, perdir, perdir, perdir],
        out_specs=[big, big, big, perdir,
                   pl.BlockSpec((None, SUBLANES, lanes), lambda z, i, j: (z, i, j))],
        out_shape=[jax.ShapeDtypeStruct((2, nch * 2 * c, d), dt) for dt in (BF16, F32, BF16)]
        + [jax.ShapeDtypeStruct((2, n, d), F32),
           jax.ShapeDtypeStruct((2, nch * SUBLANES, d), F32)],
        compiler_params=_cparams("parallel", "parallel", "parallel"),
        name="wkv_chunk",
    )(r, v, kk, lw, kd, asg)

    def cmap(z, b, cc):
        back = jnp.where(cc < ncc, ncc - 1 - cc, ncs + ncc - 1 - cc)
        return (z, b * ncs + jnp.where(z == 0, cc, back), 0)

    out = pl.pallas_call(
        functools.partial(_wkv_scan_kernel, npairs=d // LANES),
        grid=(2, batch, ncs),
        in_specs=[
            pl.BlockSpec((None, 2 * c, d), cmap),
            pl.BlockSpec((None, 2 * c, d), cmap),
            pl.BlockSpec((None, 2 * c, d), cmap),
            pl.BlockSpec((None, c, d), cmap),
            pl.BlockSpec((None, SUBLANES, d), cmap),
        ],
        out_specs=pl.BlockSpec((None, c, d), cmap),
        out_shape=jax.ShapeDtypeStruct((2, n, d), F32),
        scratch_shapes=[pltpu.VMEM((d // LANES, 2 * c, 2 * c), F32)],
        compiler_params=_cparams("parallel", "parallel", "arbitrary"),
        name="wkv_scan",
    )(m_, ga_, rq_, o0_, pc_)
    return out


def _rwread_kernel(o_ref, r_ref, kd_ref, v_ref, g_ref, pv_ref, bd_ref, y_ref):
    bd = bd_ref[...]
    wkv = o_ref[0] + o_ref[1]
    inv = 1.0 / RW_HEAD
    mu = _head_sum(wkv, bd) * inv
    dev = wkv - mu
    var = _head_sum(dev * dev, bd) * inv
    y = dev * lax.rsqrt(var + GN_EPS) * pv_ref[1:2, :] + pv_ref[2:3, :]
    rk = r_ref[...] * (kd_ref[0] + kd_ref[1]) * pv_ref[0:1, :]
    y = y + _head_sum(rk, bd) * v_ref[...]
    y_ref[...] = (y * g_ref[...]).astype(BF16)


def _rwkv_readout(geo, wkv, r, kd, v, g, pvec):
    n, d = r.shape
    tm, tc = geo.tm, 512
    ii = lax.broadcasted_iota(jnp.int32, (tc, tc), 0) // RW_HEAD
    jj = lax.broadcasted_iota(jnp.int32, (tc, tc), 1) // RW_HEAD
    bd = (ii == jj).astype(BF16)
    row = pl.BlockSpec((tm, tc), lambda i, j: (i, j))
    dirs = pl.BlockSpec((2, tm, tc), lambda i, j: (0, i, j))
    return pl.pallas_call(
        _rwread_kernel,
        grid=(geo.ntiles, d // tc),
        in_specs=[dirs, row, dirs, row, row,
                  pl.BlockSpec((SUBLANES, tc), lambda i, j: (0, j)),
                  pl.BlockSpec((tc, tc), lambda i, j: (0, 0))],
        out_specs=row,
        out_shape=jax.ShapeDtypeStruct((n, d), BF16),
        compiler_params=_cparams("parallel", "parallel"),
        name="rwkv_readout",
    )(wkv, r, kd, v, g, pvec, bd)


def _mlaqkv_kernel(z_ref, qn_ref, kvn_ref, wuq_ref, wukv_ref, gq_ref, gk_ref, cos_ref, sin_ref,
                   q_ref, k_ref, v_ref, *, q_lora, kv_lora):
    z = z_ref[...]
    cq = z[:, 0:q_lora]
    ckv = z[:, q_lora:q_lora + kv_lora]
    krot = z[:, q_lora + kv_lora:q_lora + kv_lora + LANES]

    def rms(x, g):
        ms = jnp.mean(x * x, axis=-1, keepdims=True)
        return x * lax.rsqrt(ms + EPS) * g

    qf = _dot(rms(cq, qn_ref[...]).astype(BF16), wuq_ref[...])
    kvf = _dot(rms(ckv, kvn_ref[...]).astype(BF16), wukv_ref[...])
    cos = cos_ref[...]
    sin = sin_ref[...]
    half = QK_ROPE // 2
    lane = lax.broadcasted_iota(jnp.int32, cos.shape, 1)

    def rope(x):
        up = pltpu.roll(x, LANES - half, axis=1)
        dn = pltpu.roll(x, half, axis=1)
        return x * cos + jnp.where(lane < half, up, dn) * sin

    inv_w = 1.0 / (QK_NOPE + QK_ROPE)
    gq = gq_ref[...]
    gk = gk_ref[...]
    kr_ss = jnp.sum(krot * krot, axis=-1, keepdims=True)
    for h in range(MLA_HEADS):
        o = QK_PAD * h
        qh = qf[:, o:o + QK_PAD]
        rs = lax.rsqrt(jnp.sum(qh * qh, axis=-1, keepdims=True) * inv_w + EPS)
        qn = qh * rs * gq * (ATTN_SCALE * math.log2(math.e))
        q_ref[:, o:o + QK_NOPE] = qn[:, 0:QK_NOPE].astype(BF16)
        q_ref[:, o + QK_NOPE:o + QK_PAD] = rope(qn[:, QK_NOPE:QK_PAD]).astype(BF16)
        kn = kvf[:, o:o + QK_NOPE]
        rsk = lax.rsqrt((jnp.sum(kn * kn, axis=-1, keepdims=True) + kr_ss) * inv_w + EPS)
        k_ref[:, o:o + QK_NOPE] = (kn * rsk * gk[:, 0:QK_NOPE]).astype(BF16)
        k_ref[:, o + QK_NOPE:o + QK_PAD] = rope(krot * rsk * gk[:, QK_NOPE:QK_PAD]).astype(BF16)
        v_ref[:, V_HEAD * h:V_HEAD * (h + 1)] = kvf[:, o + QK_NOPE:o + QK_PAD].astype(BF16)


def _mla_qkv(geo, z, qn, kvn, wuq_p, wukv, gq_p, gk_p, cos_t, sin_t):
    n, zw = z.shape
    tm = geo.tm
    q_lora, kv_lora = qn.shape[0], kvn.shape[0]
    hq = MLA_HEADS * QK_PAD
    tpb = geo.tpb
    full = lambda i: (0, 0)
    rowmap = lambda i: (i, 0)
    return pl.pallas_call(
        functools.partial(_mlaqkv_kernel, q_lora=q_lora, kv_lora=kv_lora),
        grid=(geo.ntiles,),
        in_specs=[
            pl.BlockSpec((tm, zw), rowmap),
            pl.BlockSpec((1, q_lora), full),
            pl.BlockSpec((1, kv_lora), full),
            pl.BlockSpec((q_lora, hq), full),
            pl.BlockSpec((kv_lora, hq), full),
            pl.BlockSpec((1, QK_PAD), full),
            pl.BlockSpec((1, QK_PAD), full),
            pl.BlockSpec((tm, LANES), lambda i: (i % tpb, 0)),
            pl.BlockSpec((tm, LANES), lambda i: (i % tpb, 0)),
        ],
        out_specs=[pl.BlockSpec((tm, hq), rowmap), pl.BlockSpec((tm, hq), rowmap),
                   pl.BlockSpec((tm, MLA_HEADS * V_HEAD), rowmap)],
        out_shape=[jax.ShapeDtypeStruct((n, hq), BF16), jax.ShapeDtypeStruct((n, hq), BF16),
                   jax.ShapeDtypeStruct((n, MLA_HEADS * V_HEAD), BF16)],
        compiler_params=_cparams("parallel"),
        name="mla_qkv",
    )(z, qn.reshape(1, -1), kvn.reshape(1, -1), wuq_p, wukv, gq_p, gk_p, cos_t, sin_t)


ATTN_SUBTILES = 2


def _attn_kernel(q_ref, k_ref, vt_ref, o_ref):
    k = k_ref[...]
    vt = vt_ref[...]
    cols = q_ref.shape[0] // ATTN_SUBTILES
    subs = range(ATTN_SUBTILES)
    s = [_dot_nt(k, q_ref[cols * i:cols * (i + 1), :]) for i in subs]
    m = [jnp.max(s[i], axis=0, keepdims=True) for i in subs]
    p = [jnp.exp2(s[i] - m[i]) for i in subs]
    l = [jnp.sum(p[i], axis=0, keepdims=True) for i in subs]
    for i in subs:
        ot = _dot(vt, p[i].astype(BF16)) / l[i]
        o_ref[cols * i:cols * (i + 1), :] = ot.T.astype(BF16)


def _attn_call(q3, k3, vt4, tq, nkeys):
    b, sq, _ = q3.shape
    return pl.pallas_call(
        _attn_kernel,
        grid=(b, MLA_HEADS, sq // tq),
        in_specs=[
            pl.BlockSpec((None, tq, QK_PAD), lambda bb, h, i: (bb, i, h)),
            pl.BlockSpec((None, nkeys, QK_PAD), lambda bb, h, i: (bb, 0, h)),
            pl.BlockSpec((None, None, V_HEAD, nkeys), lambda bb, h, i: (bb, h, 0, 0)),
        ],
        out_specs=pl.BlockSpec((None, tq, V_HEAD), lambda bb, h, i: (bb, i, h)),
        out_shape=jax.ShapeDtypeStruct((b, sq, MLA_HEADS * V_HEAD), BF16),
        compiler_params=_cparams("parallel", "parallel", "parallel"),
        name="mla_attention",
    )(q3, k3, vt4)


def _mla_attention(geo, q, k, v):
    b, s, l = geo.b, geo.s, geo.l
    q3 = q.reshape(b, s, MLA_HEADS * QK_PAD)
    k3 = k.reshape(b, s, MLA_HEADS * QK_PAD)
    vt4 = v.reshape(b, s, MLA_HEADS, V_HEAD).transpose(0, 2, 3, 1)
    o_ctx = _attn_call(q3[:, :l], k3, vt4, geo.tm, l)
    tq_lat = 2 * geo.tm if geo.t % (2 * geo.tm) == 0 else geo.tm
    o_lat = _attn_call(q3[:, l:], k3, vt4, tq_lat, s)
    return jnp.concatenate([o_ctx, o_lat], axis=1).reshape(b * s, MLA_HEADS * V_HEAD)


def _topk_rows(s, k, payload=None):
    rows = lax.broadcasted_iota(jnp.int32, s.shape, 0).astype(F32)
    big = float(s.shape[0])
    vals, idxs = [], []
    for _ in range(k):
        m = jnp.max(s, axis=0, keepdims=True)
        idx = jnp.min(jnp.where(s == m, rows, big), axis=0, keepdims=True)
        hit = rows == idx
        vals.append(m)
        if payload is None:
            idxs.append(idx)
        else:
            idxs.append(jnp.sum(jnp.where(hit, payload, 0.0), axis=0, keepdims=True))
        s = jnp.where(hit, -jnp.inf, s)
    return jnp.concatenate(vals, axis=0), jnp.concatenate(idxs, axis=0)


def _peersel_kernel(q_ref, qn_ref, keys_ref, e_ref, g_ref):
    q = q_ref[...]
    ms = jnp.mean(q * q, axis=-1, keepdims=True)
    qn = q * lax.rsqrt(ms + EPS) * qn_ref[...]
    half = D_KEY // 2
    s1 = _dot3_nt(keys_ref[0], qn[:, 0:half])
    s2 = _dot3_nt(keys_ref[1], qn[:, half:D_KEY])
    t1, i1 = _topk_rows(s1, PEER_TOPK)
    t2, i2 = _topk_rows(s2, PEER_TOPK)
    k = PEER_TOPK
    sub = lax.broadcasted_iota(jnp.int32, (SUBLANES, t1.shape[1]), 0)
    cand = [t1[0:1, :] + t2]
    cidx = [i1[0:1, :] * float(N_KEYS) + i2]
    for p in range(1, k // 2):
        live = sub < k // (p + 1)
        cand.append(jnp.where(live, t1[p:p + 1, :] + t2[0:SUBLANES, :], -jnp.inf))
        cidx.append(i1[p:p + 1, :] * float(N_KEYS) + i2[0:SUBLANES, :])
    cand.append(t1[k // 2:k, :] + t2[0:1, :])
    cidx.append(i1[k // 2:k, :] * float(N_KEYS) + i2[0:1, :])
    best, eidx = _topk_rows(jnp.concatenate(cand, axis=0), k, payload=jnp.concatenate(cidx, axis=0))
    ex = jnp.exp(best - jnp.max(best, axis=0, keepdims=True))
    g_ref[...] = ex / jnp.sum(ex, axis=0, keepdims=True)
    e_ref[...] = eidx.astype(jnp.int32)


def _peer_select(qp, q_norm, keys, tm, row0, n):
    assert row0 % tm == 0 and n % tm == 0
    tile0 = row0 // tm
    return pl.pallas_call(
        _peersel_kernel,
        grid=(n // tm, PEER_HEADS),
        in_specs=[
            pl.BlockSpec((tm, D_KEY), lambda i, h: (tile0 + i, h)),
            pl.BlockSpec((1, D_KEY), lambda i, h: (0, 0)),
            pl.BlockSpec((2, N_KEYS, D_KEY // 2), lambda i, h: (0, 0, 0)),
        ],
        out_specs=[pl.BlockSpec((PEER_TOPK, tm), lambda i, h: (h, i)),
                   pl.BlockSpec((PEER_TOPK, tm), lambda i, h: (h, i))],
        out_shape=[jax.ShapeDtypeStruct((PEER_HEADS * PEER_TOPK, n), jnp.int32),
                   jax.ShapeDtypeStruct((PEER_HEADS * PEER_TOPK, n), F32)],
        compiler_params=_cparams("parallel", "parallel"),
        name="peer_select",
    )(qp, q_norm.reshape(1, D_KEY), keys)


GATHER_SLOTS = 4
SLAB_PAD = 1


def _pack_expert_table(u, v):
    ne, d = u.shape
    ub = lax.bitcast_convert_type(u.astype(BF16), jnp.uint16).astype(jnp.uint32)
    vb = lax.bitcast_convert_type(v.astype(BF16), jnp.uint16).astype(jnp.uint32)
    return ((vb << 16) | ub).reshape(ne, d // LANES, LANES)


def _peergather_kernel(idx_ref, idxn_ref, gate_ref, h_ref, x_ref, mod_ref, tab_ref, o_ref,
                       *scratch, tb):
    nsel = PEER_HEADS * PEER_TOPK
    nrow = h_ref.shape[1] // LANES
    pitch = nrow + SLAB_PAD
    ns = GATHER_SLOTS
    bufs, sem = scratch[:ns], scratch[ns]
    lane_t = lax.broadcasted_iota(jnp.int32, (nsel, tb), 1)
    g2 = mod_ref[5:6, :]
    step = pl.program_id(0)
    nsteps = pl.num_programs(0)

    def row_copy(ids_ref, t, j, slot):
        return pltpu.make_async_copy(
            tab_ref.at[ids_ref[t, j]],
            bufs[slot].at[pl.ds(j * pitch, nrow), :],
            sem.at[slot])

    def issue(ids_ref, t, slot):
        for j in range(nsel):
            row_copy(ids_ref, t, j, slot).start(priority=j % 2)

    def wait(t, slot):
        for j in range(nsel):
            row_copy(idx_ref, t, j, slot).wait()

    def packed(slot, s):
        return bufs[slot][pl.ds(s, nsel, stride=pitch), :]

    def compute(t, slot):
        hrow = h_ref[pl.ds(t, 1), :]
        acc = jnp.zeros((nsel, LANES), F32)
        for s in range(nrow):
            u = lax.bitcast_convert_type(packed(slot, s) << 16, F32)
            acc = acc + u * hrow[:, LANES * s:LANES * (s + 1)]
        dots = jnp.sum(acc, axis=-1, keepdims=True)
        gcol = jnp.sum(jnp.where(lane_t == t, gate_ref[...], 0.0), axis=-1, keepdims=True)
        coef = gcol * _gelu(dots)
        outs = []
        for s in range(nrow):
            vv = lax.bitcast_convert_type(packed(slot, s) & jnp.uint32(0xFFFF0000), F32)
            outs.append(jnp.sum(coef * vv, axis=0, keepdims=True))
        orow = jnp.concatenate(outs, axis=1)
        o_ref[pl.ds(t, 1), :] = x_ref[pl.ds(t, 1), :] + g2 * orow

    @pl.when(step == 0)
    def _():
        for s in range(ns - 1):
            issue(idx_ref, s, s)

    ngroups = tb // ns

    def body(g, carry):
        for s in range(ns):
            t = g * ns + s
            wait(t, s)
            issue(idx_ref, t + ns - 1, (s + ns - 1) % ns)
            compute(t, s)
        return carry

    lax.fori_loop(0, ngroups - 1, body, 0)
    for s in range(ns):
        t = (ngroups - 1) * ns + s
        wait(t, s)
        if s == 0:
            issue(idx_ref, tb - 1, ns - 1)
        else:
            @pl.when(step < nsteps - 1)
            def _():
                issue(idxn_ref, s - 1, s - 1)
        compute(t, s)


def _peer_gather(geo_g, eidx, gate_t, h2, x, mods, table, n):
    d = x.shape[1]
    tb = geo_g.tm
    nsel = PEER_HEADS * PEER_TOPK
    pitch = d // LANES + SLAB_PAD
    assert n % tb == 0
    nsteps = n // tb
    return pl.pallas_call(
        functools.partial(_peergather_kernel, tb=tb),
        grid=(nsteps,),
        in_specs=[
            pl.BlockSpec((tb, nsel), lambda i: (i, 0), memory_space=pltpu.SMEM),
            pl.BlockSpec((tb, nsel), lambda i: (jnp.minimum(i + 1, nsteps - 1), 0),
                         memory_space=pltpu.SMEM),
            pl.BlockSpec((nsel, tb), lambda i: (0, i)),
            pl.BlockSpec((tb, d), lambda i: (i, 0)),
            pl.BlockSpec((tb, d), lambda i: (i, 0)),
            geo_g.mod_spec(),
            pl.BlockSpec(memory_space=pl.ANY),
        ],
        out_specs=pl.BlockSpec((tb, d), lambda i: (i, 0)),
        out_shape=jax.ShapeDtypeStruct((n, d), F32),
        scratch_shapes=[pltpu.VMEM((nsel * pitch, LANES), jnp.uint32)] * GATHER_SLOTS
        + [pltpu.SemaphoreType.DMA((GATHER_SLOTS,))],
        compiler_params=_cparams("arbitrary"),
        name="peer_gather",
    )(eidx, eidx, gate_t, h2, x, mods, table)


SC_CORES = 2
SC_SUBCORES = 16
SC_LANES = 16
SC_CHUNK = 16
SC_SHARE = 0.34

_ERF_ALPHA = (0.00022905065861350646, 0.0034082910107109506, 0.050955695062380861,
              0.18520832239976145, 1.128379143519084)
_ERF_BETA = (-1.1791602954361697e-7, 0.000023547966471313185, 0.0010179625278914885,
             0.014070470171167667, 0.11098505178285362, 0.49746925110067538, 1.0)
_ERF_CLAMP = 3.832506856900711


def _erf_rational(x):
    x = jnp.minimum(jnp.maximum(x, -_ERF_CLAMP), _ERF_CLAMP)
    x2 = x * x
    p = jnp.full_like(x, _ERF_ALPHA[0])
    for c in _ERF_ALPHA[1:]:
        p = p * x2 + c
    q = jnp.full_like(x, _ERF_BETA[0])
    for c in _ERF_BETA[1:]:
        q = q * x2 + c
    return x * p / q


def _peer_sc(eidx, gate, h, table, row0):
    n, nsel = eidx.shape
    d = h.shape[1]
    nw = SC_CORES * SC_SUBCORES
    assert n % (2 * nw) == 0 and nsel % SC_CHUNK == 0 and d % SC_LANES == 0
    tpw = n // nw
    nchunk = nsel // SC_CHUNK
    nvec = d // SC_LANES
    mesh = plsc.VectorSubcoreMesh(core_axis_name="c", subcore_axis_name="s",
                                  num_cores=SC_CORES, num_subcores=SC_SUBCORES)

    @functools.partial(
        pl.kernel, mesh=mesh,
        out_type=jax.ShapeDtypeStruct((n, d), F32),
        scratch_types=[
            pltpu.VMEM((nsel,), jnp.int32), pltpu.VMEM((nsel,), jnp.int32),
            pltpu.VMEM((nsel,), F32), pltpu.VMEM((nsel,), F32),
            pltpu.VMEM((d,), F32), pltpu.VMEM((d,), F32),
            pltpu.VMEM((d,), F32),
            pltpu.VMEM((SC_CHUNK, d), jnp.uint32),
            pltpu.VMEM((SC_CHUNK, d), jnp.uint32),
            pltpu.SemaphoreType.DMA, pltpu.SemaphoreType.DMA,
            pltpu.SemaphoreType.DMA, pltpu.SemaphoreType.DMA,
        ],
        compiler_params=pltpu.CompilerParams(needs_layout_passes=False),
        name="peer_sc",
    )
    def sc_kernel(eidx_hbm, gate_hbm, h_hbm, tab_hbm, y_hbm, idx_a, idx_b, gate_a, gate_b, h_a, h_b,
                  o_v, rows0, rows1, sem0, sem1, msem_a, msem_b):
        wid = lax.axis_index("s") * SC_CORES + lax.axis_index("c")
        lanes = lax.iota(jnp.int32, SC_LANES)
        zero = jnp.zeros((SC_LANES,), F32)
        bufs = ((rows0, sem0), (rows1, sem1))
        meta = ((idx_a, gate_a, h_a, msem_a), (idx_b, gate_b, h_b, msem_b))

        def meta_copies(tok, s):
            idx_v, gate_v, h_v, msem = meta[s]
            return (pltpu.make_async_copy(eidx_hbm.at[tok], idx_v, msem),
                    pltpu.make_async_copy(gate_hbm.at[tok], gate_v, msem),
                    pltpu.make_async_copy(h_hbm.at[row0 + tok], h_v, msem))

        def gather(s, c, slot):
            rows, sem = bufs[slot]
            return pltpu.make_async_copy(
                tab_hbm.at[meta[s][0].at[pl.ds(c * SC_CHUNK, SC_CHUNK)]], rows, sem)

        def chunk_compute(s, c, rows):
            _, gate_v, h_v, _ = meta[s]

            def dot_body(i, accs):
                hv = h_v[pl.ds(i * SC_LANES, SC_LANES)]
                out = []
                for e in range(SC_CHUNK):
                    w = rows[e, pl.ds(i * SC_LANES, SC_LANES)]
                    out.append(accs[e] + lax.bitcast_convert_type(w << 16, F32) * hv)
                return tuple(out)

            accs = lax.fori_loop(0, nvec, dot_body, (zero,) * SC_CHUNK)
            dots = zero
            for e in range(SC_CHUNK):
                dots = jnp.where(lanes == e, jnp.sum(accs[e]), dots)
            act = 0.5 * dots * (1.0 + _erf_rational(dots * (2.0 ** -0.5)))
            coef = gate_v[pl.ds(c * SC_CHUNK, SC_CHUNK)] * act
            splat = [jnp.full((SC_LANES,), jnp.sum(jnp.where(lanes == e, coef, 0.0)), F32)
                     for e in range(SC_CHUNK)]

            def acc_body(i, carry):
                o = o_v[pl.ds(i * SC_LANES, SC_LANES)]
                for e in range(SC_CHUNK):
                    w = rows[e, pl.ds(i * SC_LANES, SC_LANES)]
                    o = o + splat[e] * lax.bitcast_convert_type(w & jnp.uint32(0xFFFF0000), F32)
                o_v[pl.ds(i * SC_LANES, SC_LANES)] = o
                return carry

            lax.fori_loop(0, nvec, acc_body, 0)

        def zero_body(i, c2):
            o_v[pl.ds(i * SC_LANES, SC_LANES)] = zero
            return c2

        def token(tok, s, has_next):
            def when_next(fn):
                if isinstance(has_next, bool):
                    if has_next:
                        fn()
                else:
                    pl.when(has_next)(fn)

            def load_next():
                for cp in meta_copies(tok + 1, 1 - s):
                    cp.start()

            when_next(load_next)
            lax.fori_loop(0, nvec, zero_body, 0)
            for c in range(nchunk):
                if c + 1 < nchunk:
                    gather(s, c + 1, (c + 1) % 2).start()
                gather(s, c, c % 2).wait()
                chunk_compute(s, c, bufs[c % 2][0])

            def prefetch_next():
                for cp in meta_copies(tok + 1, 1 - s):
                    cp.wait()
                gather(1 - s, 0, 0).start()

            when_next(prefetch_next)
            pltpu.sync_copy(o_v, y_hbm.at[tok])

        base = wid * tpw
        for cp in meta_copies(base, 0):
            cp.start()
        for cp in meta_copies(base, 0):
            cp.wait()
        gather(0, 0, 0).start()

        def pair_body(g, carry):
            token(base + 2 * g, 0, True)
            token(base + 2 * g + 1, 1, g + 1 < tpw // 2)
            return carry

        lax.fori_loop(0, tpw // 2, pair_body, 0)

    return sc_kernel(eidx, gate, h, table)


def _pad_to(x, axis, size):
    pad = [(0, 0)] * x.ndim
    pad[axis] = (0, size - x.shape[axis])
    return jnp.pad(x, pad)


def _rwkv_layer(geo, xs, mods, norm1, mix, w_rkv, w_o, w0, w1, w2, a0, a1, a2, vl, g1, g2,
                k_k, k_a, r_k, ln_w, ln_b, vfirst):
    d = geo.d
    xm = _rwkv_mix(geo, xs, norm1, mods, mix)
    rkv = _bmm(xm, w_rkv.astype(BF16), (0, 2, 3), ("none",) * 3, F32, geo.tm)
    lt = g1.shape[1]
    w1c = jnp.concatenate([_pad_to(w1[0], 1, LORA_PAD), _pad_to(w1[1], 1, LORA_PAD)], axis=1)
    a1c = jnp.concatenate([_pad_to(a1[0], 1, LORA_PAD), _pad_to(a1[1], 1, LORA_PAD)], axis=1)
    if vl is None:
        v1p = jnp.zeros((d, lt), F32)
        v2p = jnp.zeros((LORA_PAD, d), F32)
        v0 = jnp.zeros((d,), F32)
    else:
        v0, v1, v2 = vl
        v1p = _pad_to(v1, 1, lt)
        v2p = _pad_to(v2, 0, LORA_PAD)
    wl1 = jnp.stack([_pad_to(w1c, 1, lt), _pad_to(a1c, 1, lt), g1, v1p]).astype(BF16)
    tl = _bmm(xm, wl1, (1, 4, 5, 3), ("tanh", "none", "sigmoid", "none"), BF16, geo.tm)
    w2p = jnp.stack([_pad_to(w2[0], 0, LORA_PAD), _pad_to(w2[1], 0, LORA_PAD)]).astype(BF16)
    a2p = jnp.stack([_pad_to(a2[0], 0, LORA_PAD), _pad_to(a2[1], 0, LORA_PAD)]).astype(BF16)
    pvec = jnp.stack([w0[0], w0[1], a0[0], a0[1], v0, k_k, k_a, jnp.zeros_like(k_k)])
    lw, kd, asg, kk, g, v = _rwkv_feat(geo, rkv, tl, w2p, a2p, g2.astype(BF16), v2p.astype(BF16),
                                      pvec, vfirst)
    r = rkv[0]
    wkv = _wkv_bidir(r, v, kk, lw, kd, asg, geo.b, geo.l)
    pv2 = _pad_to(jnp.stack([r_k, ln_w, ln_b]), 0, SUBLANES)
    y = _rwkv_readout(geo, wkv, r, kd, v, g, pv2)
    xs = _matmul_res(geo, y, w_o.astype(BF16), xs, mods, 2)
    return xs, v


def _rope_tables(geo):
    t = geo.t
    pos = jnp.arange(t)
    row = (pos // GRID_W).astype(F32)
    col = (pos % GRID_W).astype(F32)
    n_freq = QK_ROPE // 4
    inv_freq = ROPE_THETA ** (-jnp.arange(n_freq, dtype=F32) / n_freq)
    ang = jnp.concatenate([row[:, None] * inv_freq, col[:, None] * inv_freq], axis=-1)
    cos, sin = jnp.cos(ang), jnp.sin(ang)
    pad = LANES - QK_ROPE
    cos_l = jnp.concatenate([cos, cos, jnp.ones((t, pad), F32)], axis=1)
    sin_l = jnp.concatenate([-sin, sin, jnp.zeros((t, pad), F32)], axis=1)
    cos_c = jnp.ones((geo.l, LANES), F32)
    sin_c = jnp.zeros((geo.l, LANES), F32)
    return jnp.concatenate([cos_c, cos_l], axis=0), jnp.concatenate([sin_c, sin_l], axis=0)


def _mla_layer(geo, xs, mods, norm1, rope_t, w_in, q_norm, kv_norm, w_uq, w_ukv, g_q, g_k, w_o):
    q_lora, kv_lora = q_norm.shape[0], kv_norm.shape[0]
    zw = q_lora + kv_lora + LANES
    z = _mod_matmul(geo, xs, norm1, mods, _pad_to(w_in, 1, zw).astype(BF16), 0, False)
    qk = QK_NOPE + QK_ROPE
    wuq_p = _pad_to(w_uq.reshape(q_lora, MLA_HEADS, qk), 2, QK_PAD).reshape(q_lora, -1)
    gq_p = _pad_to(g_q, 0, QK_PAD).reshape(1, QK_PAD)
    gk_p = _pad_to(g_k, 0, QK_PAD).reshape(1, QK_PAD)
    q, k, v = _mla_qkv(geo, z, q_norm, kv_norm, wuq_p.astype(BF16), w_ukv.astype(BF16),
                       gq_p, gk_p, *rope_t)
    o = _mla_attention(geo, q, k, v)
    return _matmul_res(geo, o, w_o.astype(BF16), xs, mods, 2)


def _peer_layer(geo, geo_g, xs, mods, norm2, w_q, q_norm, keys, u, v):
    qp, h2 = _mod_matmul(geo, xs, norm2, mods, w_q.astype(BF16), 1, True)
    table = _pack_expert_table(u, v)
    n, d = xs.shape
    n_sc = geo.tm * round(SC_SHARE * n / geo.tm)
    if n_sc % (2 * SC_CORES * SC_SUBCORES) != 0:
        n_sc = 0
    n_tc = n - n_sc
    if n_sc:
        eidx_sc, gate_sc = _peer_select(qp, q_norm, keys, geo.tm, n_tc, n_sc)
        y_sc = _peer_sc(eidx_sc.T, gate_sc.T, h2, table.reshape(table.shape[0], d), n_tc)
    eidx_t, gate_t = _peer_select(qp, q_norm, keys, geo.tm, 0, n_tc)
    out_tc = _peer_gather(geo_g, eidx_t.T, gate_t, h2, xs, mods, table, n_tc)
    if n_sc == 0:
        return out_tc
    g2, r = [], n_tc
    while r < n:
        bi, pos = divmod(r, geo.s)
        seg = int(pos >= geo.l)
        stop = min(n, bi * geo.s + (geo.s if seg else geo.l))
        g2.append(jnp.broadcast_to(mods[bi, seg, 5], (stop - r, d)))
        r = stop
    out_sc = xs[n_tc:] + jnp.concatenate(g2, axis=0) * y_sc
    return jnp.concatenate([out_tc, out_sc], axis=0)


def kernel(x, c, ctx, c_ctx, w_ada, b_ada, norm1, norm2, rw_mix, rw_wrkv, rw_wo, rw_w0, rw_w1, rw_w2, rw_a0, rw_a1, rw_a2, rw_v0, rw_v1, rw_v2, rw_g1, rw_g2, rw_kk, rw_ka, rw_rk, rw_lnw, rw_lnb, mla_win, mla_qnorm, mla_kvnorm, mla_wuq, mla_wukv, mla_gq, mla_gk, mla_wo, peer_wq, peer_qnorm, peer_keys, peer_u, peer_v):
    b, t, d = x.shape
    l = ctx.shape[1]
    depth = w_ada.shape[0]
    geo = _Geom(b, l, t, d, min(256, l))
    geo_g = _Geom(b, l, t, d, min(128, l))
    cond8 = _pad_to(jnp.concatenate([c, c_ctx[None, :]], axis=0), 0, SUBLANES)
    ada = _adaln(cond8, w_ada, b_ada).reshape(depth, SUBLANES, 6, d)
    mods_all = jnp.stack([jnp.broadcast_to(ada[:, b:b + 1], (depth, b, 6, d)), ada[:, 0:b]], axis=2)
    xs = jnp.concatenate([ctx, x], axis=1).reshape(b * (l + t), d)
    rope_t = _rope_tables(geo)
    vfirst = None
    for i in range(depth):
        j = i // 2
        mods = mods_all[i]
        if i % 2 == 0:
            vl = None if j == 0 else (rw_v0[j - 1], rw_v1[j - 1], rw_v2[j - 1])
            xs, vcur = _rwkv_layer(geo, xs, mods, norm1[i], rw_mix[j], rw_wrkv[j], rw_wo[j],
                                   rw_w0[j], rw_w1[j], rw_w2[j], rw_a0[j], rw_a1[j], rw_a2[j], vl,
                                   rw_g1[j], rw_g2[j], rw_kk[j], rw_ka[j], rw_rk[j], rw_lnw[j],
                                   rw_lnb[j], vfirst)
            if j == 0:
                vfirst = vcur
        else:
            xs = _mla_layer(geo, xs, mods, norm1[i], rope_t, mla_win[j], mla_qnorm[j],
                            mla_kvnorm[j], mla_wuq[j], mla_wukv[j], mla_gq[j], mla_gk[j], mla_wo[j])
        if i == depth - 1:
            xs = xs.reshape(b, l + t, d)[:, l:, :].reshape(b * t, d)
            geo, geo_g = _Geom(b, 0, t, d, geo.tm), _Geom(b, 0, t, d, geo_g.tm)
        xs = _peer_layer(geo, geo_g, xs, mods, norm2[i], peer_wq[i], peer_qnorm[i], peer_keys[i],
                         peer_u[i], peer_v[i])
    return xs.reshape(b, t, d)
```

```python
import functools
import math

import jax
import jax.numpy as jnp
from jax import lax
from jax.experimental import pallas as pl
from jax.experimental.pallas import tpu as pltpu
from jax.experimental.pallas import tpu_sc as plsc

F32 = jnp.float32
BF16 = jnp.bfloat16

EPS = 1e-6
GN_EPS = 64e-5
RW_HEAD = 64
WKV_CHUNK = 64
MLA_HEADS = 16
QK_NOPE = 128
QK_ROPE = 64
V_HEAD = 128
QK_PAD = 256
ROPE_THETA = 10000.0
GRID_W = 64
ATTN_SCALE = (QK_NOPE + QK_ROPE) ** -0.5
PEER_HEADS = 8
N_KEYS = 128
PEER_TOPK = 16
D_KEY = 256
LORA_PAD = 128

LANES = 128
SUBLANES = 8
VMEM_LIMIT = 56 * 1024 * 1024


def _cparams(*sem):
    return pltpu.CompilerParams(dimension_semantics=sem, vmem_limit_bytes=VMEM_LIMIT)


def _dot(a, b):
    return jnp.dot(a, b, preferred_element_type=F32)


def _dot_nt(a, b):
    return lax.dot_general(a, b, (((1,), (1,)), ((), ())), preferred_element_type=F32)


def _split2(x):
    hi = x.astype(BF16)
    lo = (x - hi.astype(F32)).astype(BF16)
    return hi, lo


def _split3(x):
    hi = x.astype(BF16)
    r1 = x - hi.astype(F32)
    mid = r1.astype(BF16)
    lo = (r1 - mid.astype(F32)).astype(BF16)
    return hi, mid, lo


def _dot3(a, b):
    ah, al = _split2(a)
    bh, bl = _split2(b)
    return _dot(ah, bh) + (_dot(ah, bl) + _dot(al, bh))


def _dot3_nt(a, b):
    ah, al = _split2(a)
    bh, bl = _split2(b)
    return _dot_nt(ah, bh) + (_dot_nt(ah, bl) + _dot_nt(al, bh))


def _dot_exact_lhs(sel, x):
    hi, mid, lo = _split3(x)
    return _dot(sel, hi) + (_dot(sel, mid) + _dot(sel, lo))


def _modulate(x, g, shift, scale):
    ms = jnp.mean(x * x, axis=-1, keepdims=True)
    return (x * lax.rsqrt(ms + EPS) * g) * (1.0 + scale) + shift


def _sigmoid(x):
    return 1.0 / (1.0 + jnp.exp(-x))


def _softplus(y):
    return jnp.maximum(y, 0.0) + jnp.log(1.0 + jnp.exp(-jnp.abs(y)))


def _erf(x):
    return lax.erf(x)


def _gelu(x):
    return 0.5 * x * (1.0 + _erf(x * (2.0 ** -0.5)))


def _ada_kernel(s_ref, w_ref, b_ref, o_ref):
    s = s_ref[...]
    s = s * _sigmoid(s)
    o_ref[...] = _dot3(s, w_ref[...]) + b_ref[...]


def _adaln(cond8, w_ada, b_ada):
    depth, d, n = w_ada.shape
    tn = 1024
    return pl.pallas_call(
        _ada_kernel,
        grid=(depth, n // tn),
        in_specs=[
            pl.BlockSpec((SUBLANES, d), lambda l, j: (0, 0)),
            pl.BlockSpec((None, d, tn), lambda l, j: (l, 0, j)),
            pl.BlockSpec((None, 1, tn), lambda l, j: (l, 0, j)),
        ],
        out_specs=pl.BlockSpec((None, SUBLANES, tn), lambda l, j: (l, 0, j)),
        out_shape=jax.ShapeDtypeStruct((depth, SUBLANES, n), F32),
        compiler_params=_cparams("parallel", "parallel"),
        name="adaln",
    )(cond8, w_ada, b_ada.reshape(depth, 1, n))


class _Geom:
    def __init__(self, batch, ctx_len, seq_len, d_model, tm):
        self.b, self.l, self.t, self.d = batch, ctx_len, seq_len, d_model
        self.s = ctx_len + seq_len
        self.n = batch * self.s
        self.tm = tm
        assert ctx_len % tm == 0 and seq_len % tm == 0
        self.tpb = self.s // tm
        self.nct = ctx_len // tm
        self.ntiles = self.n // tm

    def mod_spec(self, nlead=0):
        tpb, nct = self.tpb, self.nct

        def imap(*ids):
            i = ids[nlead]
            return (i // tpb, ((i % tpb) >= nct).astype(jnp.int32), 0, 0)

        return pl.BlockSpec((None, None, 6, self.d), imap)


def _modmm_kernel(x_ref, g_ref, mod_ref, w_ref, o_ref, *h_ref, which):
    h = _modulate(x_ref[...], g_ref[...], mod_ref[3 * which:3 * which + 1, :],
                  mod_ref[3 * which + 1:3 * which + 2, :])
    if h_ref:
        h_ref[0][...] = h
    o_ref[...] = _dot(h.astype(BF16), w_ref[...])


def _mod_matmul(geo, x, g, mods, w, which, emit_h):
    n, d = x.shape
    nn = w.shape[1]
    tm = geo.tm
    out_shape = [jax.ShapeDtypeStruct((n, nn), F32)]
    out_specs = [pl.BlockSpec((tm, nn), lambda i: (i, 0))]
    if emit_h:
        out_shape.append(jax.ShapeDtypeStruct((n, d), F32))
        out_specs.append(pl.BlockSpec((tm, d), lambda i: (i, 0)))
    res = pl.pallas_call(
        functools.partial(_modmm_kernel, which=which),
        grid=(geo.ntiles,),
        in_specs=[
            pl.BlockSpec((tm, d), lambda i: (i, 0)),
            pl.BlockSpec((1, d), lambda i: (0, 0)),
            geo.mod_spec(),
            pl.BlockSpec((d, nn), lambda i: (0, 0)),
        ],
        out_specs=out_specs,
        out_shape=out_shape,
        compiler_params=_cparams("parallel"),
        name="mod_matmul",
    )(x, g.reshape(1, d), mods, w)
    return res if emit_h else res[0]


def _mmres_kernel(y_ref, w_ref, x_ref, mod_ref, o_ref, *, gidx):
    acc = _dot(y_ref[...], w_ref[...])
    o_ref[...] = x_ref[...] + mod_ref[gidx:gidx + 1, :] * acc


def _matmul_res(geo, y, w, x, mods, gidx):
    n, k = y.shape
    d = x.shape[1]
    tm = geo.tm
    return pl.pallas_call(
        functools.partial(_mmres_kernel, gidx=gidx),
        grid=(geo.ntiles,),
        in_specs=[
            pl.BlockSpec((tm, k), lambda i: (i, 0)),
            pl.BlockSpec((k, d), lambda i: (0, 0)),
            pl.BlockSpec((tm, d), lambda i: (i, 0)),
            geo.mod_spec(),
        ],
        out_specs=pl.BlockSpec((tm, d), lambda i: (i, 0)),
        out_shape=jax.ShapeDtypeStruct((n, d), F32),
        compiler_params=_cparams("parallel"),
        name="matmul_res",
    )(y, w, x, mods)


def _bmm_kernel(x_ref, w_ref, o_ref, *, acts):
    j = pl.program_id(0)
    y = _dot(x_ref[...], w_ref[...])
    out = y
    for jj, a in enumerate(acts):
        if a == "tanh":
            out = jnp.where(j == jj, jnp.tanh(y), out)
        elif a == "sigmoid":
            out = jnp.where(j == jj, _sigmoid(y), out)
    o_ref[...] = out.astype(o_ref.dtype)


def _bmm(x3, w3, src, acts, out_dtype, tm):
    _, n, k = x3.shape
    nj, _, nn = w3.shape
    src = tuple(src)

    def xmap(j, i):
        idx = jnp.int32(src[0])
        for jj in range(1, nj):
            idx = jnp.where(j == jj, jnp.int32(src[jj]), idx)
        return (idx, i, 0)

    return pl.pallas_call(
        functools.partial(_bmm_kernel, acts=tuple(acts)),
        grid=(nj, n // tm),
        in_specs=[
            pl.BlockSpec((None, tm, k), xmap),
            pl.BlockSpec((None, k, nn), lambda j, i: (j, 0, 0)),
        ],
        out_specs=pl.BlockSpec((None, tm, nn), lambda j, i: (j, i, 0)),
        out_shape=jax.ShapeDtypeStruct((nj, n, nn), out_dtype),
        compiler_params=_cparams("parallel", "parallel"),
        name="bmm",
    )(x3, w3)


def _rwmix_kernel(x_ref, xp_ref, xn_ref, g_ref, mod_ref, mix_ref, o_ref, *, tpb, nct):
    i = pl.program_id(0)
    tm = x_ref.shape[0]
    g = g_ref[...]
    shift = mod_ref[0:1, :]
    scale = mod_ref[1:2, :]
    h = _modulate(x_ref[...], g, shift, scale)
    hp = _modulate(xp_ref[...], g, shift, scale)[SUBLANES - 1:SUBLANES, :]
    hn = _modulate(xn_ref[...], g, shift, scale)[0:1, :]
    it = i % tpb
    first = jnp.logical_or(it == 0, it == nct)
    last = jnp.logical_or(it == nct - 1, it == tpb - 1)
    hp = jnp.where(first, 0.0, hp)
    hn = jnp.where(last, 0.0, hn)
    rows = lax.broadcasted_iota(jnp.int32, h.shape, 0)
    prev = jnp.where(rows == 0, hp, pltpu.roll(h, 1, axis=0))
    nxt = jnp.where(rows == tm - 1, hn, pltpu.roll(h, tm - 1, axis=0))
    xx = 0.5 * (prev + nxt) - h
    for m in range(6):
        o_ref[m] = (h + xx * mix_ref[m:m + 1, :]).astype(BF16)


def _rwkv_mix(geo, x, g, mods, mix):
    n, d = x.shape
    tm = geo.tm
    r8 = tm // SUBLANES
    nblk8 = n // SUBLANES
    return pl.pallas_call(
        functools.partial(_rwmix_kernel, tpb=geo.tpb, nct=geo.nct),
        grid=(geo.ntiles,),
        in_specs=[
            pl.BlockSpec((tm, d), lambda i: (i, 0)),
            pl.BlockSpec((SUBLANES, d), lambda i: (jnp.maximum(i * r8 - 1, 0), 0)),
            pl.BlockSpec((SUBLANES, d), lambda i: (jnp.minimum((i + 1) * r8, nblk8 - 1), 0)),
            pl.BlockSpec((1, d), lambda i: (0, 0)),
            geo.mod_spec(),
            pl.BlockSpec((6, d), lambda i: (0, 0)),
        ],
        out_specs=pl.BlockSpec((6, tm, d), lambda i: (0, i, 0)),
        out_shape=jax.ShapeDtypeStruct((6, n, d), BF16),
        compiler_params=_cparams("parallel"),
        name="rwkv_mix",
    )(x, x, x, g.reshape(1, d), mods, mix)


def _head_sum(x, bd):
    hi, lo = _split2(x)
    return _dot(hi, bd) + _dot(lo, bd)


def _rwfeat_kernel(k_ref, v_ref, tl_ref, w2_ref, a2_ref, g2_ref, v2_ref, pv_ref, bd_ref,
                   *rest, has_vlora):
    if has_vlora:
        vf_ref, lw_ref, kd_ref, as_ref, kk_ref, g_ref, vo_ref = rest
    else:
        lw_ref, kd_ref, as_ref, kk_ref, g_ref = rest
    k = k_ref[...]
    tw = tl_ref[0]
    ta = tl_ref[1]
    tg = tl_ref[2]
    w0 = pv_ref[0:2, :]
    a0 = pv_ref[2:4, :]
    k_k = pv_ref[5:6, :]
    k_a = pv_ref[6:7, :]
    for z in range(2):
        sl = slice(LORA_PAD * z, LORA_PAD * (z + 1))
        lora_w = _dot(tw[:, sl], w2_ref[z])
        w = -_softplus(-(w0[z:z + 1, :] + lora_w)) - 0.5
        lw_ref[z] = -jnp.exp(w)
        a_sig = _sigmoid(a0[z:z + 1, :] + _dot(ta[:, sl], a2_ref[z]))
        as_ref[z] = a_sig
        kd_ref[z] = k * (1.0 + (a_sig - 1.0) * k_a)
    g_ref[...] = _dot(tg, g2_ref[...])
    kkr = k * k_k
    ss = _head_sum(kkr * kkr, bd_ref[...])
    kk_ref[...] = kkr * lax.rsqrt(ss + 1e-12)
    if has_vlora:
        v = v_ref[...]
        tv = tl_ref[3]
        gate = _sigmoid(pv_ref[4:5, :] + _dot(tv[:, 0:LORA_PAD], v2_ref[...]))
        vo_ref[...] = v + (vf_ref[...] - v) * gate


def _rwkv_feat(geo, rkv, tl, w2p, a2p, g2, v2p, pvec, vfirst):
    _, n, d = rkv.shape
    tm, tc = geo.tm, 512
    has_vlora = vfirst is not None
    lt = tl.shape[2]
    ii = lax.broadcasted_iota(jnp.int32, (tc, tc), 0) // RW_HEAD
    jj = lax.broadcasted_iota(jnp.int32, (tc, tc), 1) // RW_HEAD
    bd = (ii == jj).astype(BF16)
    row = lambda i, j: (i, j)
    in_specs = [
        pl.BlockSpec((None, tm, tc), lambda i, j: (1, i, j)),
        pl.BlockSpec((None, tm, tc), lambda i, j: (2, i, j)),
        pl.BlockSpec((4, tm, lt), lambda i, j: (0, i, 0)),
        pl.BlockSpec((2, LORA_PAD, tc), lambda i, j: (0, 0, j)),
        pl.BlockSpec((2, LORA_PAD, tc), lambda i, j: (0, 0, j)),
        pl.BlockSpec((lt, tc), lambda i, j: (0, j)),
        pl.BlockSpec((LORA_PAD, tc), lambda i, j: (0, j)),
        pl.BlockSpec((SUBLANES, tc), lambda i, j: (0, j)),
        pl.BlockSpec((tc, tc), lambda i, j: (0, 0)),
    ]
    args = [rkv, rkv, tl, w2p, a2p, g2, v2p, pvec, bd]
    dir_spec = pl.BlockSpec((2, tm, tc), lambda i, j: (0, i, j))
    out_specs = [dir_spec, dir_spec, dir_spec, pl.BlockSpec((tm, tc), row), pl.BlockSpec((tm, tc), row)]
    out_shape = [jax.ShapeDtypeStruct((2, n, d), F32)] * 3 + [jax.ShapeDtypeStruct((n, d), F32)] * 2
    if has_vlora:
        in_specs.append(pl.BlockSpec((tm, tc), row))
        args.append(vfirst)
        out_specs.append(pl.BlockSpec((tm, tc), row))
        out_shape.append(jax.ShapeDtypeStruct((n, d), F32))
    res = pl.pallas_call(
        functools.partial(_rwfeat_kernel, has_vlora=has_vlora),
        grid=(geo.ntiles, d // tc),
        in_specs=in_specs,
        out_specs=out_specs,
        out_shape=out_shape,
        compiler_params=_cparams("parallel", "parallel"),
        name="rwkv_feat",
    )(*args)
    if has_vlora:
        lw, kd, asg, kk, g, v = res
    else:
        lw, kd, asg, kk, g = res
        v = rkv[2]
    return lw, kd, asg, kk, g, v


def _wkv_chunk_kernel(r_ref, v_ref, kk_ref, lw_ref, kd_ref, as_ref,
                      m_ref, ga_ref, rq_ref, o0_ref, pc_ref, *, npairs):
    c = WKV_CHUNK
    c2 = 2 * c
    sgn = 1 - 2 * pl.program_id(0)
    ri = lax.broadcasted_iota(jnp.int32, (c2, c2), 0)
    ci = lax.broadcasted_iota(jnp.int32, (c2, c2), 1)
    same = (ri >= c) == (ci >= c)
    tt = jnp.where(ri >= c, ri - c, ri)
    ss = jnp.where(ci >= c, ci - c, ci)
    earlier = (ss - tt) * sgn < 0
    strict = jnp.logical_and(same, earlier)
    incl = jnp.logical_and(same, jnp.logical_or(earlier, ss == tt))
    eye = (ri == ci).astype(F32)
    r64 = lax.broadcasted_iota(jnp.int32, (c, c), 0)
    c64 = lax.broadcasted_iota(jnp.int32, (c, c), 1)
    ltri = jnp.where((c64 - r64) * sgn <= 0, 1.0, 0.0).astype(BF16)
    head0 = lax.broadcasted_iota(jnp.int32, (c, LANES), 1) < RW_HEAD
    pairs = range(npairs)

    def stack(x):
        return jnp.concatenate([jnp.where(head0, x, 0.0), jnp.where(head0, 0.0, x)], axis=0)

    def dup(x):
        return jnp.concatenate([x, x], axis=0)

    def bf(x):
        return x.astype(BF16)

    lhs, rhs, a2, bp2, kp2, vst, r2 = [], [], [], [], [], [], []
    lw_all = lw_ref[...]
    cum_all = _dot_exact_lhs(ltri, lw_all)
    for p in pairs:
        sl = slice(LANES * p, LANES * (p + 1))
        lw = lw_all[:, sl]
        cum = cum_all[:, sl]
        tot = jnp.sum(lw, axis=0, keepdims=True)
        p_inv = jnp.exp(-cum)
        p_end = jnp.exp(tot - cum)
        kk = kk_ref[:, sl]
        b = kk * as_ref[:, sl]
        kd = kd_ref[:, sl]
        a2p = stack(-kk * jnp.exp(cum - lw))
        r2p = stack(r_ref[:, sl] * jnp.exp(cum))
        a2.append(bf(a2p))
        r2.append(r2p)
        lhs.append(jnp.concatenate([a2[p], bf(r2p)], axis=0))
        rhs.append(jnp.concatenate([dup(bf(b * p_inv)), dup(bf(kd * p_inv))], axis=0))
        bp2.append(bf(stack(b * p_end)))
        kp2.append(bf(stack(kd * p_end)))
        vst.append(bf(stack(v_ref[:, sl])))
        pc_ref[:, sl] = jnp.broadcast_to(jnp.exp(tot), (SUBLANES, LANES))
    gram = [_dot_nt(lhs[p], rhs[p]) for p in pairs]
    nab = [jnp.where(strict, gram[p][0:c2, 0:c2], 0.0) for p in pairs]
    nrb = [bf(jnp.where(incl, gram[p][c2:2 * c2, 0:c2], 0.0)) for p in pairs]
    nk = [bf(jnp.concatenate([jnp.where(strict, gram[p][0:c2, c2:2 * c2], 0.0),
                              jnp.where(incl, gram[p][c2:2 * c2, c2:2 * c2], 0.0)], axis=0))
          for p in pairs]
    quads = range(npairs // 2)

    def side(x0, x1):
        return jnp.concatenate([x0, x1], axis=1)

    def diag(x0, x1):
        z0 = jnp.zeros_like(x0)
        return jnp.concatenate([side(x0, z0), side(z0, x1)], axis=0)

    def diag_halves(x):
        return diag(x[:, 0:c2], x[:, c2:2 * c2])

    def unside(xs):
        return [xs[p // 2][:, c2 * (p % 2):c2 * (p % 2 + 1)] for p in pairs]

    xv = unside([_dot(side(nk[2 * q], nk[2 * q + 1]), diag(vst[2 * q], vst[2 * q + 1]))
                 for q in quads])
    tinv = [side(eye + nab[2 * q], eye + nab[2 * q + 1]) for q in quads]
    npow = [bf(side(nab[2 * q], nab[2 * q + 1])) for q in quads]
    for _ in range(int(math.log2(c)) - 1):
        npow = [bf(_dot(npow[q], diag_halves(npow[q]))) for q in quads]
        tinv = [tinv[q] + _dot(bf(tinv[q]), diag_halves(npow[q])) for q in quads]
    tinv = unside(tinv)
    y = [_dot(bf(tinv[p]), jnp.concatenate([a2[p], bf(xv[p][0:c2, :])], axis=1)) for p in pairs]
    yb = [bf(y[p]) for p in pairs]
    z = [_dot(nrb[p], yb[p]) for p in pairs]
    mg = unside([_dot(bf(side(y[2 * q].T, y[2 * q + 1].T)), diag(bp2[2 * q], bp2[2 * q + 1]))
                 for q in quads])
    vk = unside([_dot(bf(side(vst[2 * q].astype(F32).T, vst[2 * q + 1].astype(F32).T)),
                      diag(kp2[2 * q], kp2[2 * q + 1])) for q in quads])
    for p in pairs:
        sl = slice(LANES * p, LANES * (p + 1))
        m_ref[:, sl] = bf(mg[p][0:c2, :])
        ga_ref[:, sl] = mg[p][c2:2 * c2, :] + vk[p]
        rq_ref[:, sl] = bf(r2[p] + z[p][:, 0:c2])
        o0 = z[p][:, c2:2 * c2] + xv[p][c2:2 * c2, :]
        o0_ref[:, sl] = o0[0:c, :] + o0[c:c2, :]


def _wkv_scan_kernel(m_ref, ga_ref, rq_ref, o0_ref, pc_ref, o_ref, g_scr, *, npairs):
    c = WKV_CHUNK

    @pl.when(pl.program_id(2) == 0)
    def _():
        g_scr[...] = jnp.zeros_like(g_scr)

    for p in range(npairs):
        sl = slice(LANES * p, LANES * (p + 1))
        g = g_scr[p]
        g_hi, g_lo = _split2(g)
        rq = rq_ref[:, sl]
        o_st = _dot_nt(rq, g_hi) + _dot_nt(rq, g_lo)
        o_ref[:, sl] = o_st[0:c, :] + o_st[c:2 * c, :] + o0_ref[:, sl]
        m = m_ref[:, sl]
        g_scr[p] = g * pc_ref[0:1, sl] + (_dot(g_hi, m) + _dot(g_lo, m)) + ga_ref[:, sl]


WKV_CHUNK_LANES = 2048


def _wkv_bidir(r, v, kk, lw, kd, asg, batch, ctx_len):
    n, d = r.shape
    c = WKV_CHUNK
    s = n // batch
    ncs = s // c
    ncc = ctx_len // c
    nch = n // c
    lanes = min(WKV_CHUNK_LANES, d)
    ngrp = d // lanes
    shared = pl.BlockSpec((c, lanes), lambda z, i, j: (i, j))
    perdir = pl.BlockSpec((None, c, lanes), lambda z, i, j: (z, i, j))
    big = pl.BlockSpec((None, 2 * c, lanes), lambda z, i, j: (z, i, j))
    m_, ga_, rq_, o0_, pc_ = pl.pallas_call(
        functools.partial(_wkv_chunk_kernel, npairs=lanes // LANES),
        grid=(2, nch, ngrp),
        in_specs=[shared, shared, shared, perdir, perdir, perdir],
        out_specs=[big, big, big, perdir,
                   pl.BlockSpec((None, SUBLANES, lanes), lambda z, i, j: (z, i, j))],
        out_shape=[jax.ShapeDtypeStruct((2, nch * 2 * c, d), dt) for dt in (BF16, F32, BF16)]
        + [jax.ShapeDtypeStruct((2, n, d), F32),
           jax.ShapeDtypeStruct((2, nch * SUBLANES, d), F32)],
        compiler_params=_cparams("parallel", "parallel", "parallel"),
        name="wkv_chunk",
    )(r, v, kk, lw, kd, asg)

    def cmap(z, b, cc):
        back = jnp.where(cc < ncc, ncc - 1 - cc, ncs + ncc - 1 - cc)
        return (z, b * ncs + jnp.where(z == 0, cc, back), 0)

    out = pl.pallas_call(
        functools.partial(_wkv_scan_kernel, npairs=d // LANES),
        grid=(2, batch, ncs),
        in_specs=[
            pl.BlockSpec((None, 2 * c, d), cmap),
            pl.BlockSpec((None, 2 * c, d), cmap),
            pl.BlockSpec((None, 2 * c, d), cmap),
            pl.BlockSpec((None, c, d), cmap),
            pl.BlockSpec((None, SUBLANES, d), cmap),
        ],
        out_specs=pl.BlockSpec((None, c, d), cmap),
        out_shape=jax.ShapeDtypeStruct((2, n, d), F32),
        scratch_shapes=[pltpu.VMEM((d // LANES, 2 * c, 2 * c), F32)],
        compiler_params=_cparams("parallel", "parallel", "arbitrary"),
        name="wkv_scan",
    )(m_, ga_, rq_, o0_, pc_)
    return out


def _rwread_kernel(o_ref, r_ref, kd_ref, v_ref, g_ref, pv_ref, bd_ref, y_ref):
    bd = bd_ref[...]
    wkv = o_ref[0] + o_ref[1]
    inv = 1.0 / RW_HEAD
    mu = _head_sum(wkv, bd) * inv
    dev = wkv - mu
    var = _head_sum(dev * dev, bd) * inv
    y = dev * lax.rsqrt(var + GN_EPS) * pv_ref[1:2, :] + pv_ref[2:3, :]
    rk = r_ref[...] * (kd_ref[0] + kd_ref[1]) * pv_ref[0:1, :]
    y = y + _head_sum(rk, bd) * v_ref[...]
    y_ref[...] = (y * g_ref[...]).astype(BF16)


def _rwkv_readout(geo, wkv, r, kd, v, g, pvec):
    n, d = r.shape
    tm, tc = geo.tm, 512
    ii = lax.broadcasted_iota(jnp.int32, (tc, tc), 0) // RW_HEAD
    jj = lax.broadcasted_iota(jnp.int32, (tc, tc), 1) // RW_HEAD
    bd = (ii == jj).astype(BF16)
    row = pl.BlockSpec((tm, tc), lambda i, j: (i, j))
    dirs = pl.BlockSpec((2, tm, tc), lambda i, j: (0, i, j))
    return pl.pallas_call(
        _rwread_kernel,
        grid=(geo.ntiles, d // tc),
        in_specs=[dirs, row, dirs, row, row,
                  pl.BlockSpec((SUBLANES, tc), lambda i, j: (0, j)),
                  pl.BlockSpec((tc, tc), lambda i, j: (0, 0))],
        out_specs=row,
        out_shape=jax.ShapeDtypeStruct((n, d), BF16),
        compiler_params=_cparams("parallel", "parallel"),
        name="rwkv_readout",
    )(wkv, r, kd, v, g, pvec, bd)


def _mlaqkv_kernel(z_ref, qn_ref, kvn_ref, wuq_ref, wukv_ref, gq_ref, gk_ref, cos_ref, sin_ref,
                   q_ref, k_ref, v_ref, *, q_lora, kv_lora):
    z = z_ref[...]
    cq = z[:, 0:q_lora]
    ckv = z[:, q_lora:q_lora + kv_lora]
    krot = z[:, q_lora + kv_lora:q_lora + kv_lora + LANES]

    def rms(x, g):
        ms = jnp.mean(x * x, axis=-1, keepdims=True)
        return x * lax.rsqrt(ms + EPS) * g

    qf = _dot(rms(cq, qn_ref[...]).astype(BF16), wuq_ref[...])
    kvf = _dot(rms(ckv, kvn_ref[...]).astype(BF16), wukv_ref[...])
    cos = cos_ref[...]
    sin = sin_ref[...]
    half = QK_ROPE // 2
    lane = lax.broadcasted_iota(jnp.int32, cos.shape, 1)

    def rope(x):
        up = pltpu.roll(x, LANES - half, axis=1)
        dn = pltpu.roll(x, half, axis=1)
        return x * cos + jnp.where(lane < half, up, dn) * sin

    inv_w = 1.0 / (QK_NOPE + QK_ROPE)
    gq = gq_ref[...]
    gk = gk_ref[...]
    kr_ss = jnp.sum(krot * krot, axis=-1, keepdims=True)
    for h in range(MLA_HEADS):
        o = QK_PAD * h
        qh = qf[:, o:o + QK_PAD]
        rs = lax.rsqrt(jnp.sum(qh * qh, axis=-1, keepdims=True) * inv_w + EPS)
        qn = qh * rs * gq * (ATTN_SCALE * math.log2(math.e))
        q_ref[:, o:o + QK_NOPE] = qn[:, 0:QK_NOPE].astype(BF16)
        q_ref[:, o + QK_NOPE:o + QK_PAD] = rope(qn[:, QK_NOPE:QK_PAD]).astype(BF16)
        kn = kvf[:, o:o + QK_NOPE]
        rsk = lax.rsqrt((jnp.sum(kn * kn, axis=-1, keepdims=True) + kr_ss) * inv_w + EPS)
        k_ref[:, o:o + QK_NOPE] = (kn * rsk * gk[:, 0:QK_NOPE]).astype(BF16)
        k_ref[:, o + QK_NOPE:o + QK_PAD] = rope(krot * rsk * gk[:, QK_NOPE:QK_PAD]).astype(BF16)
        v_ref[:, V_HEAD * h:V_HEAD * (h + 1)] = kvf[:, o + QK_NOPE:o + QK_PAD].astype(BF16)


def _mla_qkv(geo, z, qn, kvn, wuq_p, wukv, gq_p, gk_p, cos_t, sin_t):
    n, zw = z.shape
    tm = geo.tm
    q_lora, kv_lora = qn.shape[0], kvn.shape[0]
    hq = MLA_HEADS * QK_PAD
    tpb = geo.tpb
    full = lambda i: (0, 0)
    rowmap = lambda i: (i, 0)
    return pl.pallas_call(
        functools.partial(_mlaqkv_kernel, q_lora=q_lora, kv_lora=kv_lora),
        grid=(geo.ntiles,),
        in_specs=[
            pl.BlockSpec((tm, zw), rowmap),
            pl.BlockSpec((1, q_lora), full),
            pl.BlockSpec((1, kv_lora), full),
            pl.BlockSpec((q_lora, hq), full),
            pl.BlockSpec((kv_lora, hq), full),
            pl.BlockSpec((1, QK_PAD), full),
            pl.BlockSpec((1, QK_PAD), full),
            pl.BlockSpec((tm, LANES), lambda i: (i % tpb, 0)),
            pl.BlockSpec((tm, LANES), lambda i: (i % tpb, 0)),
        ],
        out_specs=[pl.BlockSpec((tm, hq), rowmap), pl.BlockSpec((tm, hq), rowmap),
                   pl.BlockSpec((tm, MLA_HEADS * V_HEAD), rowmap)],
        out_shape=[jax.ShapeDtypeStruct((n, hq), BF16), jax.ShapeDtypeStruct((n, hq), BF16),
                   jax.ShapeDtypeStruct((n, MLA_HEADS * V_HEAD), BF16)],
        compiler_params=_cparams("parallel"),
        name="mla_qkv",
    )(z, qn.reshape(1, -1), kvn.reshape(1, -1), wuq_p, wukv, gq_p, gk_p, cos_t, sin_t)


ATTN_SUBTILES = 2


def _attn_kernel(q_ref, k_ref, vt_ref, o_ref):
    k = k_ref[...]
    vt = vt_ref[...]
    cols = q_ref.shape[0] // ATTN_SUBTILES
    subs = range(ATTN_SUBTILES)
    s = [_dot_nt(k, q_ref[cols * i:cols * (i + 1), :]) for i in subs]
    m = [jnp.max(s[i], axis=0, keepdims=True) for i in subs]
    p = [jnp.exp2(s[i] - m[i]) for i in subs]
    l = [jnp.sum(p[i], axis=0, keepdims=True) for i in subs]
    for i in subs:
        ot = _dot(vt, p[i].astype(BF16)) / l[i]
        o_ref[cols * i:cols * (i + 1), :] = ot.T.astype(BF16)


def _attn_call(q3, k3, vt4, tq, nkeys):
    b, sq, _ = q3.shape
    return pl.pallas_call(
        _attn_kernel,
        grid=(b, MLA_HEADS, sq // tq),
        in_specs=[
            pl.BlockSpec((None, tq, QK_PAD), lambda bb, h, i: (bb, i, h)),
            pl.BlockSpec((None, nkeys, QK_PAD), lambda bb, h, i: (bb, 0, h)),
            pl.BlockSpec((None, None, V_HEAD, nkeys), lambda bb, h, i: (bb, h, 0, 0)),
        ],
        out_specs=pl.BlockSpec((None, tq, V_HEAD), lambda bb, h, i: (bb, i, h)),
        out_shape=jax.ShapeDtypeStruct((b, sq, MLA_HEADS * V_HEAD), BF16),
        compiler_params=_cparams("parallel", "parallel", "parallel"),
        name="mla_attention",
    )(q3, k3, vt4)


def _mla_attention(geo, q, k, v):
    b, s, l = geo.b, geo.s, geo.l
    q3 = q.reshape(b, s, MLA_HEADS * QK_PAD)
    k3 = k.reshape(b, s, MLA_HEADS * QK_PAD)
    vt4 = v.reshape(b, s, MLA_HEADS, V_HEAD).transpose(0, 2, 3, 1)
    o_ctx = _attn_call(q3[:, :l], k3, vt4, geo.tm, l)
    tq_lat = 2 * geo.tm if geo.t % (2 * geo.tm) == 0 else geo.tm
    o_lat = _attn_call(q3[:, l:], k3, vt4, tq_lat, s)
    return jnp.concatenate([o_ctx, o_lat], axis=1).reshape(b * s, MLA_HEADS * V_HEAD)


def _topk_rows(s, k, payload=None):
    rows = lax.broadcasted_iota(jnp.int32, s.shape, 0).astype(F32)
    big = float(s.shape[0])
    vals, idxs = [], []
    for _ in range(k):
        m = jnp.max(s, axis=0, keepdims=True)
        idx = jnp.min(jnp.where(s == m, rows, big), axis=0, keepdims=True)
        hit = rows == idx
        vals.append(m)
        if payload is None:
            idxs.append(idx)
        else:
            idxs.append(jnp.sum(jnp.where(hit, payload, 0.0), axis=0, keepdims=True))
        s = jnp.where(hit, -jnp.inf, s)
    return jnp.concatenate(vals, axis=0), jnp.concatenate(idxs, axis=0)


def _peersel_kernel(q_ref, qn_ref, keys_ref, e_ref, g_ref):
    q = q_ref[...]
    ms = jnp.mean(q * q, axis=-1, keepdims=True)
    qn = q * lax.rsqrt(ms + EPS) * qn_ref[...]
    half = D_KEY // 2
    s1 = _dot3_nt(keys_ref[0], qn[:, 0:half])
    s2 = _dot3_nt(keys_ref[1], qn[:, half:D_KEY])
    t1, i1 = _topk_rows(s1, PEER_TOPK)
    t2, i2 = _topk_rows(s2, PEER_TOPK)
    k = PEER_TOPK
    sub = lax.broadcasted_iota(jnp.int32, (SUBLANES, t1.shape[1]), 0)
    cand = [t1[0:1, :] + t2]
    cidx = [i1[0:1, :] * float(N_KEYS) + i2]
    for p in range(1, k // 2):
        live = sub < k // (p + 1)
        cand.append(jnp.where(live, t1[p:p + 1, :] + t2[0:SUBLANES, :], -jnp.inf))
        cidx.append(i1[p:p + 1, :] * float(N_KEYS) + i2[0:SUBLANES, :])
    cand.append(t1[k // 2:k, :] + t2[0:1, :])
    cidx.append(i1[k // 2:k, :] * float(N_KEYS) + i2[0:1, :])
    best, eidx = _topk_rows(jnp.concatenate(cand, axis=0), k, payload=jnp.concatenate(cidx, axis=0))
    ex = jnp.exp(best - jnp.max(best, axis=0, keepdims=True))
    g_ref[...] = ex / jnp.sum(ex, axis=0, keepdims=True)
    e_ref[...] = eidx.astype(jnp.int32)


def _peer_select(qp, q_norm, keys, tm, row0, n):
    assert row0 % tm == 0 and n % tm == 0
    tile0 = row0 // tm
    return pl.pallas_call(
        _peersel_kernel,
        grid=(n // tm, PEER_HEADS),
        in_specs=[
            pl.BlockSpec((tm, D_KEY), lambda i, h: (tile0 + i, h)),
            pl.BlockSpec((1, D_KEY), lambda i, h: (0, 0)),
            pl.BlockSpec((2, N_KEYS, D_KEY // 2), lambda i, h: (0, 0, 0)),
        ],
        out_specs=[pl.BlockSpec((PEER_TOPK, tm), lambda i, h: (h, i)),
                   pl.BlockSpec((PEER_TOPK, tm), lambda i, h: (h, i))],
        out_shape=[jax.ShapeDtypeStruct((PEER_HEADS * PEER_TOPK, n), jnp.int32),
                   jax.ShapeDtypeStruct((PEER_HEADS * PEER_TOPK, n), F32)],
        compiler_params=_cparams("parallel", "parallel"),
        name="peer_select",
    )(qp, q_norm.reshape(1, D_KEY), keys)


GATHER_SLOTS = 4
SLAB_PAD = 1


def _pack_expert_table(u, v):
    ne, d = u.shape
    ub = lax.bitcast_convert_type(u.astype(BF16), jnp.uint16).astype(jnp.uint32)
    vb = lax.bitcast_convert_type(v.astype(BF16), jnp.uint16).astype(jnp.uint32)
    return ((vb << 16) | ub).reshape(ne, d // LANES, LANES)


def _peergather_kernel(idx_ref, idxn_ref, gate_ref, h_ref, x_ref, mod_ref, tab_ref, o_ref,
                       *scratch, tb):
    nsel = PEER_HEADS * PEER_TOPK
    nrow = h_ref.shape[1] // LANES
    pitch = nrow + SLAB_PAD
    ns = GATHER_SLOTS
    bufs, sem = scratch[:ns], scratch[ns]
    lane_t = lax.broadcasted_iota(jnp.int32, (nsel, tb), 1)
    g2 = mod_ref[5:6, :]
    step = pl.program_id(0)
    nsteps = pl.num_programs(0)

    def row_copy(ids_ref, t, j, slot):
        return pltpu.make_async_copy(
            tab_ref.at[ids_ref[t, j]],
            bufs[slot].at[pl.ds(j * pitch, nrow), :],
            sem.at[slot])

    def issue(ids_ref, t, slot):
        for j in range(nsel):
            row_copy(ids_ref, t, j, slot).start(priority=j % 2)

    def wait(t, slot):
        for j in range(nsel):
            row_copy(idx_ref, t, j, slot).wait()

    def packed(slot, s):
        return bufs[slot][pl.ds(s, nsel, stride=pitch), :]

    def compute(t, slot):
        hrow = h_ref[pl.ds(t, 1), :]
        acc = jnp.zeros((nsel, LANES), F32)
        for s in range(nrow):
            u = lax.bitcast_convert_type(packed(slot, s) << 16, F32)
            acc = acc + u * hrow[:, LANES * s:LANES * (s + 1)]
        dots = jnp.sum(acc, axis=-1, keepdims=True)
        gcol = jnp.sum(jnp.where(lane_t == t, gate_ref[...], 0.0), axis=-1, keepdims=True)
        coef = gcol * _gelu(dots)
        outs = []
        for s in range(nrow):
            vv = lax.bitcast_convert_type(packed(slot, s) & jnp.uint32(0xFFFF0000), F32)
            outs.append(jnp.sum(coef * vv, axis=0, keepdims=True))
        orow = jnp.concatenate(outs, axis=1)
        o_ref[pl.ds(t, 1), :] = x_ref[pl.ds(t, 1), :] + g2 * orow

    @pl.when(step == 0)
    def _():
        for s in range(ns - 1):
            issue(idx_ref, s, s)

    ngroups = tb // ns

    def body(g, carry):
        for s in range(ns):
            t = g * ns + s
            wait(t, s)
            issue(idx_ref, t + ns - 1, (s + ns - 1) % ns)
            compute(t, s)
        return carry

    lax.fori_loop(0, ngroups - 1, body, 0)
    for s in range(ns):
        t = (ngroups - 1) * ns + s
        wait(t, s)
        if s == 0:
            issue(idx_ref, tb - 1, ns - 1)
        else:
            @pl.when(step < nsteps - 1)
            def _():
                issue(idxn_ref, s - 1, s - 1)
        compute(t, s)


def _peer_gather(geo_g, eidx, gate_t, h2, x, mods, table, n):
    d = x.shape[1]
    tb = geo_g.tm
    nsel = PEER_HEADS * PEER_TOPK
    pitch = d // LANES + SLAB_PAD
    assert n % tb == 0
    nsteps = n // tb
    return pl.pallas_call(
        functools.partial(_peergather_kernel, tb=tb),
        grid=(nsteps,),
        in_specs=[
            pl.BlockSpec((tb, nsel), lambda i: (i, 0), memory_space=pltpu.SMEM),
            pl.BlockSpec((tb, nsel), lambda i: (jnp.minimum(i + 1, nsteps - 1), 0),
                         memory_space=pltpu.SMEM),
            pl.BlockSpec((nsel, tb), lambda i: (0, i)),
            pl.BlockSpec((tb, d), lambda i: (i, 0)),
            pl.BlockSpec((tb, d), lambda i: (i, 0)),
            geo_g.mod_spec(),
            pl.BlockSpec(memory_space=pl.ANY),
        ],
        out_specs=pl.BlockSpec((tb, d), lambda i: (i, 0)),
        out_shape=jax.ShapeDtypeStruct((n, d), F32),
        scratch_shapes=[pltpu.VMEM((nsel * pitch, LANES), jnp.uint32)] * GATHER_SLOTS
        + [pltpu.SemaphoreType.DMA((GATHER_SLOTS,))],
        compiler_params=_cparams("arbitrary"),
        name="peer_gather",
    )(eidx, eidx, gate_t, h2, x, mods, table)


SC_CORES = 2
SC_SUBCORES = 16
SC_LANES = 16
SC_CHUNK = 16
SC_SHARE = 0.34

_ERF_ALPHA = (0.00022905065861350646, 0.0034082910107109506, 0.050955695062380861,
              0.18520832239976145, 1.128379143519084)
_ERF_BETA = (-1.1791602954361697e-7, 0.000023547966471313185, 0.0010179625278914885,
             0.014070470171167667, 0.11098505178285362, 0.49746925110067538, 1.0)
_ERF_CLAMP = 3.832506856900711


def _erf_rational(x):
    x = jnp.minimum(jnp.maximum(x, -_ERF_CLAMP), _ERF_CLAMP)
    x2 = x * x
    p = jnp.full_like(x, _ERF_ALPHA[0])
    for c in _ERF_ALPHA[1:]:
        p = p * x2 + c
    q = jnp.full_like(x, _ERF_BETA[0])
    for c in _ERF_BETA[1:]:
        q = q * x2 + c
    return x * p / q


def _peer_sc(eidx, gate, h, table, row0):
    n, nsel = eidx.shape
    d = h.shape[1]
    nw = SC_CORES * SC_SUBCORES
    assert n % (2 * nw) == 0 and nsel % SC_CHUNK == 0 and d % SC_LANES == 0
    tpw = n // nw
    nchunk = nsel // SC_CHUNK
    nvec = d // SC_LANES
    mesh = plsc.VectorSubcoreMesh(core_axis_name="c", subcore_axis_name="s",
                                  num_cores=SC_CORES, num_subcores=SC_SUBCORES)

    @functools.partial(
        pl.kernel, mesh=mesh,
        out_type=jax.ShapeDtypeStruct((n, d), F32),
        scratch_types=[
            pltpu.VMEM((nsel,), jnp.int32), pltpu.VMEM((nsel,), jnp.int32),
            pltpu.VMEM((nsel,), F32), pltpu.VMEM((nsel,), F32),
            pltpu.VMEM((d,), F32), pltpu.VMEM((d,), F32),
            pltpu.VMEM((d,), F32),
            pltpu.VMEM((SC_CHUNK, d), jnp.uint32),
            pltpu.VMEM((SC_CHUNK, d), jnp.uint32),
            pltpu.SemaphoreType.DMA, pltpu.SemaphoreType.DMA,
            pltpu.SemaphoreType.DMA, pltpu.SemaphoreType.DMA,
        ],
        compiler_params=pltpu.CompilerParams(needs_layout_passes=False),
        name="peer_sc",
    )
    def sc_kernel(eidx_hbm, gate_hbm, h_hbm, tab_hbm, y_hbm, idx_a, idx_b, gate_a, gate_b, h_a, h_b,
                  o_v, rows0, rows1, sem0, sem1, msem_a, msem_b):
        wid = lax.axis_index("s") * SC_CORES + lax.axis_index("c")
        lanes = lax.iota(jnp.int32, SC_LANES)
        zero = jnp.zeros((SC_LANES,), F32)
        bufs = ((rows0, sem0), (rows1, sem1))
        meta = ((idx_a, gate_a, h_a, msem_a), (idx_b, gate_b, h_b, msem_b))

        def meta_copies(tok, s):
            idx_v, gate_v, h_v, msem = meta[s]
            return (pltpu.make_async_copy(eidx_hbm.at[tok], idx_v, msem),
                    pltpu.make_async_copy(gate_hbm.at[tok], gate_v, msem),
                    pltpu.make_async_copy(h_hbm.at[row0 + tok], h_v, msem))

        def gather(s, c, slot):
            rows, sem = bufs[slot]
            return pltpu.make_async_copy(
                tab_hbm.at[meta[s][0].at[pl.ds(c * SC_CHUNK, SC_CHUNK)]], rows, sem)

        def chunk_compute(s, c, rows):
            _, gate_v, h_v, _ = meta[s]

            def dot_body(i, accs):
                hv = h_v[pl.ds(i * SC_LANES, SC_LANES)]
                out = []
                for e in range(SC_CHUNK):
                    w = rows[e, pl.ds(i * SC_LANES, SC_LANES)]
                    out.append(accs[e] + lax.bitcast_convert_type(w << 16, F32) * hv)
                return tuple(out)

            accs = lax.fori_loop(0, nvec, dot_body, (zero,) * SC_CHUNK)
            dots = zero
            for e in range(SC_CHUNK):
                dots = jnp.where(lanes == e, jnp.sum(accs[e]), dots)
            act = 0.5 * dots * (1.0 + _erf_rational(dots * (2.0 ** -0.5)))
            coef = gate_v[pl.ds(c * SC_CHUNK, SC_CHUNK)] * act
            splat = [jnp.full((SC_LANES,), jnp.sum(jnp.where(lanes == e, coef, 0.0)), F32)
                     for e in range(SC_CHUNK)]

            def acc_body(i, carry):
                o = o_v[pl.ds(i * SC_LANES, SC_LANES)]
                for e in range(SC_CHUNK):
                    w = rows[e, pl.ds(i * SC_LANES, SC_LANES)]
                    o = o + splat[e] * lax.bitcast_convert_type(w & jnp.uint32(0xFFFF0000), F32)
                o_v[pl.ds(i * SC_LANES, SC_LANES)] = o
                return carry

            lax.fori_loop(0, nvec, acc_body, 0)

        def zero_body(i, c2):
            o_v[pl.ds(i * SC_LANES, SC_LANES)] = zero
            return c2

        def token(tok, s, has_next):
            def when_next(fn):
                if isinstance(has_next, bool):
                    if has_next:
                        fn()
                else:
                    pl.when(has_next)(fn)

            def load_next():
                for cp in meta_copies(tok + 1, 1 - s):
                    cp.start()

            when_next(load_next)
            lax.fori_loop(0, nvec, zero_body, 0)
            for c in range(nchunk):
                if c + 1 < nchunk:
                    gather(s, c + 1, (c + 1) % 2).start()
                gather(s, c, c % 2).wait()
                chunk_compute(s, c, bufs[c % 2][0])

            def prefetch_next():
                for cp in meta_copies(tok + 1, 1 - s):
                    cp.wait()
                gather(1 - s, 0, 0).start()

            when_next(prefetch_next)
            pltpu.sync_copy(o_v, y_hbm.at[tok])

        base = wid * tpw
        for cp in meta_copies(base, 0):
            cp.start()
        for cp in meta_copies(base, 0):
            cp.wait()
        gather(0, 0, 0).start()

        def pair_body(g, carry):
            token(base + 2 * g, 0, True)
            token(base + 2 * g + 1, 1, g + 1 < tpw // 2)
            return carry

        lax.fori_loop(0, tpw // 2, pair_body, 0)

    return sc_kernel(eidx, gate, h, table)


def _pad_to(x, axis, size):
    pad = [(0, 0)] * x.ndim
    pad[axis] = (0, size - x.shape[axis])
    return jnp.pad(x, pad)


def _rwkv_layer(geo, xs, mods, norm1, mix, w_rkv, w_o, w0, w1, w2, a0, a1, a2, vl, g1, g2,
                k_k, k_a, r_k, ln_w, ln_b, vfirst):
    d = geo.d
    xm = _rwkv_mix(geo, xs, norm1, mods, mix)
    rkv = _bmm(xm, w_rkv.astype(BF16), (0, 2, 3), ("none",) * 3, F32, geo.tm)
    lt = g1.shape[1]
    w1c = jnp.concatenate([_pad_to(w1[0], 1, LORA_PAD), _pad_to(w1[1], 1, LORA_PAD)], axis=1)
    a1c = jnp.concatenate([_pad_to(a1[0], 1, LORA_PAD), _pad_to(a1[1], 1, LORA_PAD)], axis=1)
    if vl is None:
        v1p = jnp.zeros((d, lt), F32)
        v2p = jnp.zeros((LORA_PAD, d), F32)
        v0 = jnp.zeros((d,), F32)
    else:
        v0, v1, v2 = vl
        v1p = _pad_to(v1, 1, lt)
        v2p = _pad_to(v2, 0, LORA_PAD)
    wl1 = jnp.stack([_pad_to(w1c, 1, lt), _pad_to(a1c, 1, lt), g1, v1p]).astype(BF16)
    tl = _bmm(xm, wl1, (1, 4, 5, 3), ("tanh", "none", "sigmoid", "none"), BF16, geo.tm)
    w2p = jnp.stack([_pad_to(w2[0], 0, LORA_PAD), _pad_to(w2[1], 0, LORA_PAD)]).astype(BF16)
    a2p = jnp.stack([_pad_to(a2[0], 0, LORA_PAD), _pad_to(a2[1], 0, LORA_PAD)]).astype(BF16)
    pvec = jnp.stack([w0[0], w0[1], a0[0], a0[1], v0, k_k, k_a, jnp.zeros_like(k_k)])
    lw, kd, asg, kk, g, v = _rwkv_feat(geo, rkv, tl, w2p, a2p, g2.astype(BF16), v2p.astype(BF16),
                                      pvec, vfirst)
    r = rkv[0]
    wkv = _wkv_bidir(r, v, kk, lw, kd, asg, geo.b, geo.l)
    pv2 = _pad_to(jnp.stack([r_k, ln_w, ln_b]), 0, SUBLANES)
    y = _rwkv_readout(geo, wkv, r, kd, v, g, pv2)
    xs = _matmul_res(geo, y, w_o.astype(BF16), xs, mods, 2)
    return xs, v


def _rope_tables(geo):
    t = geo.t
    pos = jnp.arange(t)
    row = (pos // GRID_W).astype(F32)
    col = (pos % GRID_W).astype(F32)
    n_freq = QK_ROPE // 4
    inv_freq = ROPE_THETA ** (-jnp.arange(n_freq, dtype=F32) / n_freq)
    ang = jnp.concatenate([row[:, None] * inv_freq, col[:, None] * inv_freq], axis=-1)
    cos, sin = jnp.cos(ang), jnp.sin(ang)
    pad = LANES - QK_ROPE
    cos_l = jnp.concatenate([cos, cos, jnp.ones((t, pad), F32)], axis=1)
    sin_l = jnp.concatenate([-sin, sin, jnp.zeros((t, pad), F32)], axis=1)
    cos_c = jnp.ones((geo.l, LANES), F32)
    sin_c = jnp.zeros((geo.l, LANES), F32)
    return jnp.concatenate([cos_c, cos_l], axis=0), jnp.concatenate([sin_c, sin_l], axis=0)


def _mla_layer(geo, xs, mods, norm1, rope_t, w_in, q_norm, kv_norm, w_uq, w_ukv, g_q, g_k, w_o):
    q_lora, kv_lora = q_norm.shape[0], kv_norm.shape[0]
    zw = q_lora + kv_lora + LANES
    z = _mod_matmul(geo, xs, norm1, mods, _pad_to(w_in, 1, zw).astype(BF16), 0, False)
    qk = QK_NOPE + QK_ROPE
    wuq_p = _pad_to(w_uq.reshape(q_lora, MLA_HEADS, qk), 2, QK_PAD).reshape(q_lora, -1)
    gq_p = _pad_to(g_q, 0, QK_PAD).reshape(1, QK_PAD)
    gk_p = _pad_to(g_k, 0, QK_PAD).reshape(1, QK_PAD)
    q, k, v = _mla_qkv(geo, z, q_norm, kv_norm, wuq_p.astype(BF16), w_ukv.astype(BF16),
                       gq_p, gk_p, *rope_t)
    o = _mla_attention(geo, q, k, v)
    return _matmul_res(geo, o, w_o.astype(BF16), xs, mods, 2)


def _peer_layer(geo, geo_g, xs, mods, norm2, w_q, q_norm, keys, u, v):
    qp, h2 = _mod_matmul(geo, xs, norm2, mods, w_q.astype(BF16), 1, True)
    table = _pack_expert_table(u, v)
    n, d = xs.shape
    n_sc = geo.tm * round(SC_SHARE * n / geo.tm)
    if n_sc % (2 * SC_CORES * SC_SUBCORES) != 0:
        n_sc = 0
    n_tc = n - n_sc
    if n_sc:
        eidx_sc, gate_sc = _peer_select(qp, q_norm, keys, geo.tm, n_tc, n_sc)
        y_sc = _peer_sc(eidx_sc.T, gate_sc.T, h2, table.reshape(table.shape[0], d), n_tc)
        q_norm = q_norm + 0.0 * gate_sc[0, 0]
    eidx_t, gate_t = _peer_select(qp, q_norm, keys, geo.tm, 0, n_tc)
    out_tc = _peer_gather(geo_g, eidx_t.T, gate_t, h2, xs, mods, table, n_tc)
    if n_sc == 0:
        return out_tc
    g2, r = [], n_tc
    while r < n:
        bi, pos = divmod(r, geo.s)
        seg = int(pos >= geo.l)
        stop = min(n, bi * geo.s + (geo.s if seg else geo.l))
        g2.append(jnp.broadcast_to(mods[bi, seg, 5], (stop - r, d)))
        r = stop
    out_sc = xs[n_tc:] + jnp.concatenate(g2, axis=0) * y_sc
    return jnp.concatenate([out_tc, out_sc], axis=0)


def kernel(x, c, ctx, c_ctx, w_ada, b_ada, norm1, norm2, rw_mix, rw_wrkv, rw_wo, rw_w0, rw_w1, rw_w2, rw_a0, rw_a1, rw_a2, rw_v0, rw_v1, rw_v2, rw_g1, rw_g2, rw_kk, rw_ka, rw_rk, rw_lnw, rw_lnb, mla_win, mla_qnorm, mla_kvnorm, mla_wuq, mla_wukv, mla_gq, mla_gk, mla_wo, peer_wq, peer_qnorm, peer_keys, peer_u, peer_v):
    b, t, d = x.shape
    l = ctx.shape[1]
    depth = w_ada.shape[0]
    geo = _Geom(b, l, t, d, min(256, l))
    geo_g = _Geom(b, l, t, d, min(128, l))
    cond8 = _pad_to(jnp.concatenate([c, c_ctx[None, :]], axis=0), 0, SUBLANES)
    ada = _adaln(cond8, w_ada, b_ada).reshape(depth, SUBLANES, 6, d)
    mods_all = jnp.stack([jnp.broadcast_to(ada[:, b:b + 1], (depth, b, 6, d)), ada[:, 0:b]], axis=2)
    xs = jnp.concatenate([ctx, x], axis=1).reshape(b * (l + t), d)
    rope_t = _rope_tables(geo)
    vfirst = None
    for i in range(depth):
        j = i // 2
        mods = mods_all[i]
        if i % 2 == 0:
            vl = None if j == 0 else (rw_v0[j - 1], rw_v1[j - 1], rw_v2[j - 1])
            xs, vcur = _rwkv_layer(geo, xs, mods, norm1[i], rw_mix[j], rw_wrkv[j], rw_wo[j],
                                   rw_w0[j], rw_w1[j], rw_w2[j], rw_a0[j], rw_a1[j], rw_a2[j], vl,
                                   rw_g1[j], rw_g2[j], rw_kk[j], rw_ka[j], rw_rk[j], rw_lnw[j],
                                   rw_lnb[j], vfirst)
            if j == 0:
                vfirst = vcur
        else:
            xs = _mla_layer(geo, xs, mods, norm1[i], rope_t, mla_win[j], mla_qnorm[j],
                            mla_kvnorm[j], mla_wuq[j], mla_wukv[j], mla_gq[j], mla_gk[j], mla_wo[j])
        if i == depth - 1:
            xs = xs.reshape(b, l + t, d)[:, l:, :].reshape(b * t, d)
            geo, geo_g = _Geom(b, 0, t, d, geo.tm), _Geom(b, 0, t, d, geo_g.tm)
        xs = _peer_layer(geo, geo_g, xs, mods, norm2[i], peer_wq[i], peer_qnorm[i], peer_keys[i],
                         peer_u[i], peer_v[i])
    return xs.reshape(b, t, d)
```

```python
import functools
import math

import jax
import jax.numpy as jnp
from jax import lax
from jax.experimental import pallas as pl
from jax.experimental.pallas import tpu as pltpu
from jax.experimental.pallas import tpu_sc as plsc

F32 = jnp.float32
BF16 = jnp.bfloat16

EPS = 1e-6
GN_EPS = 64e-5
RW_HEAD = 64
WKV_CHUNK = 64
MLA_HEADS = 16
QK_NOPE = 128
QK_ROPE = 64
V_HEAD = 128
QK_PAD = 256
ROPE_THETA = 10000.0
GRID_W = 64
ATTN_SCALE = (QK_NOPE + QK_ROPE) ** -0.5
PEER_HEADS = 8
N_KEYS = 128
PEER_TOPK = 16
D_KEY = 256
LORA_PAD = 128

LANES = 128
SUBLANES = 8
VMEM_LIMIT = 56 * 1024 * 1024


def _cparams(*sem):
    return pltpu.CompilerParams(dimension_semantics=sem, vmem_limit_bytes=VMEM_LIMIT)


def _dot(a, b):
    return jnp.dot(a, b, preferred_element_type=F32)


def _dot_nt(a, b):
    return lax.dot_general(a, b, (((1,), (1,)), ((), ())), preferred_element_type=F32)


def _split2(x):
    hi = x.astype(BF16)
    lo = (x - hi.astype(F32)).astype(BF16)
    return hi, lo


def _split3(x):
    hi = x.astype(BF16)
    r1 = x - hi.astype(F32)
    mid = r1.astype(BF16)
    lo = (r1 - mid.astype(F32)).astype(BF16)
    return hi, mid, lo


def _dot3(a, b):
    ah, al = _split2(a)
    bh, bl = _split2(b)
    return _dot(ah, bh) + (_dot(ah, bl) + _dot(al, bh))


def _dot3_nt(a, b):
    ah, al = _split2(a)
    bh, bl = _split2(b)
    return _dot_nt(ah, bh) + (_dot_nt(ah, bl) + _dot_nt(al, bh))


def _dot_exact_lhs(sel, x):
    hi, mid, lo = _split3(x)
    return _dot(sel, hi) + (_dot(sel, mid) + _dot(sel, lo))


def _modulate(x, g, shift, scale):
    ms = jnp.mean(x * x, axis=-1, keepdims=True)
    return (x * lax.rsqrt(ms + EPS) * g) * (1.0 + scale) + shift


def _sigmoid(x):
    return 1.0 / (1.0 + jnp.exp(-x))


def _softplus(y):
    return jnp.maximum(y, 0.0) + jnp.log(1.0 + jnp.exp(-jnp.abs(y)))


def _erf(x):
    return lax.erf(x)


def _gelu(x):
    return 0.5 * x * (1.0 + _erf(x * (2.0 ** -0.5)))


def _ada_kernel(s_ref, w_ref, b_ref, o_ref):
    s = s_ref[...]
    s = s * _sigmoid(s)
    o_ref[...] = _dot3(s, w_ref[...]) + b_ref[...]


def _adaln(cond8, w_ada, b_ada):
    depth, d, n = w_ada.shape
    tn = 1024
    return pl.pallas_call(
        _ada_kernel,
        grid=(depth, n // tn),
        in_specs=[
            pl.BlockSpec((SUBLANES, d), lambda l, j: (0, 0)),
            pl.BlockSpec((None, d, tn), lambda l, j: (l, 0, j)),
            pl.BlockSpec((None, 1, tn), lambda l, j: (l, 0, j)),
        ],
        out_specs=pl.BlockSpec((None, SUBLANES, tn), lambda l, j: (l, 0, j)),
        out_shape=jax.ShapeDtypeStruct((depth, SUBLANES, n), F32),
        compiler_params=_cparams("parallel", "parallel"),
        name="adaln",
    )(cond8, w_ada, b_ada.reshape(depth, 1, n))


class _Geom:
    def __init__(self, batch, ctx_len, seq_len, d_model, tm):
        self.b, self.l, self.t, self.d = batch, ctx_len, seq_len, d_model
        self.s = ctx_len + seq_len
        self.n = batch * self.s
        self.tm = tm
        assert ctx_len % tm == 0 and seq_len % tm == 0
        self.tpb = self.s // tm
        self.nct = ctx_len // tm
        self.ntiles = self.n // tm

    def mod_spec(self, nlead=0):
        tpb, nct = self.tpb, self.nct

        def imap(*ids):
            i = ids[nlead]
            return (i // tpb, ((i % tpb) >= nct).astype(jnp.int32), 0, 0)

        return pl.BlockSpec((None, None, 6, self.d), imap)


def _modmm_kernel(x_ref, g_ref, mod_ref, w_ref, o_ref, *h_ref, which):
    h = _modulate(x_ref[...], g_ref[...], mod_ref[3 * which:3 * which + 1, :],
                  mod_ref[3 * which + 1:3 * which + 2, :])
    if h_ref:
        h_ref[0][...] = h
    o_ref[...] = _dot(h.astype(BF16), w_ref[...])


def _mod_matmul(geo, x, g, mods, w, which, emit_h):
    n, d = x.shape
    nn = w.shape[1]
    tm = geo.tm
    out_shape = [jax.ShapeDtypeStruct((n, nn), F32)]
    out_specs = [pl.BlockSpec((tm, nn), lambda i: (i, 0))]
    if emit_h:
        out_shape.append(jax.ShapeDtypeStruct((n, d), F32))
        out_specs.append(pl.BlockSpec((tm, d), lambda i: (i, 0)))
    res = pl.pallas_call(
        functools.partial(_modmm_kernel, which=which),
        grid=(geo.ntiles,),
        in_specs=[
            pl.BlockSpec((tm, d), lambda i: (i, 0)),
            pl.BlockSpec((1, d), lambda i: (0, 0)),
            geo.mod_spec(),
            pl.BlockSpec((d, nn), lambda i: (0, 0)),
        ],
        out_specs=out_specs,
        out_shape=out_shape,
        compiler_params=_cparams("parallel"),
        name="mod_matmul",
    )(x, g.reshape(1, d), mods, w)
    return res if emit_h else res[0]


def _mmres_kernel(y_ref, w_ref, x_ref, mod_ref, o_ref, *, gidx):
    acc = _dot(y_ref[...], w_ref[...])
    o_ref[...] = x_ref[...] + mod_ref[gidx:gidx + 1, :] * acc


def _matmul_res(geo, y, w, x, mods, gidx):
    n, k = y.shape
    d = x.shape[1]
    tm = geo.tm
    return pl.pallas_call(
        functools.partial(_mmres_kernel, gidx=gidx),
        grid=(geo.ntiles,),
        in_specs=[
            pl.BlockSpec((tm, k), lambda i: (i, 0)),
            pl.BlockSpec((k, d), lambda i: (0, 0)),
            pl.BlockSpec((tm, d), lambda i: (i, 0)),
            geo.mod_spec(),
        ],
        out_specs=pl.BlockSpec((tm, d), lambda i: (i, 0)),
        out_shape=jax.ShapeDtypeStruct((n, d), F32),
        compiler_params=_cparams("parallel"),
        name="matmul_res",
    )(y, w, x, mods)


def _bmm_kernel(x_ref, w_ref, o_ref, *, acts):
    j = pl.program_id(0)
    y = _dot(x_ref[...], w_ref[...])
    out = y
    for jj, a in enumerate(acts):
        if a == "tanh":
            out = jnp.where(j == jj, jnp.tanh(y), out)
        elif a == "sigmoid":
            out = jnp.where(j == jj, _sigmoid(y), out)
    o_ref[...] = out.astype(o_ref.dtype)


def _bmm(x3, w3, src, acts, out_dtype, tm):
    _, n, k = x3.shape
    nj, _, nn = w3.shape
    src = tuple(src)

    def xmap(j, i):
        idx = jnp.int32(src[0])
        for jj in range(1, nj):
            idx = jnp.where(j == jj, jnp.int32(src[jj]), idx)
        return (idx, i, 0)

    return pl.pallas_call(
        functools.partial(_bmm_kernel, acts=tuple(acts)),
        grid=(nj, n // tm),
        in_specs=[
            pl.BlockSpec((None, tm, k), xmap),
            pl.BlockSpec((None, k, nn), lambda j, i: (j, 0, 0)),
        ],
        out_specs=pl.BlockSpec((None, tm, nn), lambda j, i: (j, i, 0)),
        out_shape=jax.ShapeDtypeStruct((nj, n, nn), out_dtype),
        compiler_params=_cparams("parallel", "parallel"),
        name="bmm",
    )(x3, w3)


def _rwmix_kernel(x_ref, xp_ref, xn_ref, g_ref, mod_ref, mix_ref, o_ref, *, tpb, nct):
    i = pl.program_id(0)
    tm = x_ref.shape[0]
    g = g_ref[...]
    shift = mod_ref[0:1, :]
    scale = mod_ref[1:2, :]
    h = _modulate(x_ref[...], g, shift, scale)
    hp = _modulate(xp_ref[...], g, shift, scale)[SUBLANES - 1:SUBLANES, :]
    hn = _modulate(xn_ref[...], g, shift, scale)[0:1, :]
    it = i % tpb
    first = jnp.logical_or(it == 0, it == nct)
    last = jnp.logical_or(it == nct - 1, it == tpb - 1)
    hp = jnp.where(first, 0.0, hp)
    hn = jnp.where(last, 0.0, hn)
    rows = lax.broadcasted_iota(jnp.int32, h.shape, 0)
    prev = jnp.where(rows == 0, hp, pltpu.roll(h, 1, axis=0))
    nxt = jnp.where(rows == tm - 1, hn, pltpu.roll(h, tm - 1, axis=0))
    xx = 0.5 * (prev + nxt) - h
    for m in range(6):
        o_ref[m] = (h + xx * mix_ref[m:m + 1, :]).astype(BF16)


def _rwkv_mix(geo, x, g, mods, mix):
    n, d = x.shape
    tm = geo.tm
    r8 = tm // SUBLANES
    nblk8 = n // SUBLANES
    return pl.pallas_call(
        functools.partial(_rwmix_kernel, tpb=geo.tpb, nct=geo.nct),
        grid=(geo.ntiles,),
        in_specs=[
            pl.BlockSpec((tm, d), lambda i: (i, 0)),
            pl.BlockSpec((SUBLANES, d), lambda i: (jnp.maximum(i * r8 - 1, 0), 0)),
            pl.BlockSpec((SUBLANES, d), lambda i: (jnp.minimum((i + 1) * r8, nblk8 - 1), 0)),
            pl.BlockSpec((1, d), lambda i: (0, 0)),
            geo.mod_spec(),
            pl.BlockSpec((6, d), lambda i: (0, 0)),
        ],
        out_specs=pl.BlockSpec((6, tm, d), lambda i: (0, i, 0)),
        out_shape=jax.ShapeDtypeStruct((6, n, d), BF16),
        compiler_params=_cparams("parallel"),
        name="rwkv_mix",
    )(x, x, x, g.reshape(1, d), mods, mix)


def _head_sum(x, bd):
    hi, lo = _split2(x)
    return _dot(hi, bd) + _dot(lo, bd)


def _rwfeat_kernel(k_ref, v_ref, tl_ref, w2_ref, a2_ref, g2_ref, v2_ref, pv_ref, bd_ref,
                   *rest, has_vlora):
    if has_vlora:
        vf_ref, lw_ref, kd_ref, as_ref, kk_ref, g_ref, vo_ref = rest
    else:
        lw_ref, kd_ref, as_ref, kk_ref, g_ref = rest
    k = k_ref[...]
    tw = tl_ref[0]
    ta = tl_ref[1]
    tg = tl_ref[2]
    w0 = pv_ref[0:2, :]
    a0 = pv_ref[2:4, :]
    k_k = pv_ref[5:6, :]
    k_a = pv_ref[6:7, :]
    for z in range(2):
        sl = slice(LORA_PAD * z, LORA_PAD * (z + 1))
        lora_w = _dot(tw[:, sl], w2_ref[z])
        w = -_softplus(-(w0[z:z + 1, :] + lora_w)) - 0.5
        lw_ref[z] = -jnp.exp(w)
        a_sig = _sigmoid(a0[z:z + 1, :] + _dot(ta[:, sl], a2_ref[z]))
        as_ref[z] = a_sig
        kd_ref[z] = k * (1.0 + (a_sig - 1.0) * k_a)
    g_ref[...] = _dot(tg, g2_ref[...])
    kkr = k * k_k
    ss = _head_sum(kkr * kkr, bd_ref[...])
    kk_ref[...] = kkr * lax.rsqrt(ss + 1e-12)
    if has_vlora:
        v = v_ref[...]
        tv = tl_ref[3]
        gate = _sigmoid(pv_ref[4:5, :] + _dot(tv[:, 0:LORA_PAD], v2_ref[...]))
        vo_ref[...] = v + (vf_ref[...] - v) * gate


def _rwkv_feat(geo, rkv, tl, w2p, a2p, g2, v2p, pvec, vfirst):
    _, n, d = rkv.shape
    tm, tc = geo.tm, 512
    has_vlora = vfirst is not None
    lt = tl.shape[2]
    ii = lax.broadcasted_iota(jnp.int32, (tc, tc), 0) // RW_HEAD
    jj = lax.broadcasted_iota(jnp.int32, (tc, tc), 1) // RW_HEAD
    bd = (ii == jj).astype(BF16)
    row = lambda i, j: (i, j)
    in_specs = [
        pl.BlockSpec((None, tm, tc), lambda i, j: (1, i, j)),
        pl.BlockSpec((None, tm, tc), lambda i, j: (2, i, j)),
        pl.BlockSpec((4, tm, lt), lambda i, j: (0, i, 0)),
        pl.BlockSpec((2, LORA_PAD, tc), lambda i, j: (0, 0, j)),
        pl.BlockSpec((2, LORA_PAD, tc), lambda i, j: (0, 0, j)),
        pl.BlockSpec((lt, tc), lambda i, j: (0, j)),
        pl.BlockSpec((LORA_PAD, tc), lambda i, j: (0, j)),
        pl.BlockSpec((SUBLANES, tc), lambda i, j: (0, j)),
        pl.BlockSpec((tc, tc), lambda i, j: (0, 0)),
    ]
    args = [rkv, rkv, tl, w2p, a2p, g2, v2p, pvec, bd]
    dir_spec = pl.BlockSpec((2, tm, tc), lambda i, j: (0, i, j))
    out_specs = [dir_spec, dir_spec, dir_spec, pl.BlockSpec((tm, tc), row), pl.BlockSpec((tm, tc), row)]
    out_shape = [jax.ShapeDtypeStruct((2, n, d), F32)] * 3 + [jax.ShapeDtypeStruct((n, d), F32)] * 2
    if has_vlora:
        in_specs.append(pl.BlockSpec((tm, tc), row))
        args.append(vfirst)
        out_specs.append(pl.BlockSpec((tm, tc), row))
        out_shape.append(jax.ShapeDtypeStruct((n, d), F32))
    res = pl.pallas_call(
        functools.partial(_rwfeat_kernel, has_vlora=has_vlora),
        grid=(geo.ntiles, d // tc),
        in_specs=in_specs,
        out_specs=out_specs,
        out_shape=out_shape,
        compiler_params=_cparams("parallel", "parallel"),
        name="rwkv_feat",
    )(*args)
    if has_vlora:
        lw, kd, asg, kk, g, v = res
    else:
        lw, kd, asg, kk, g = res
        v = rkv[2]
    return lw, kd, asg, kk, g, v


def _wkv_chunk_kernel(r_ref, v_ref, kk_ref, lw_ref, kd_ref, as_ref,
                      m_ref, ga_ref, rq_ref, o0_ref, pc_ref, *, npairs):
    c = WKV_CHUNK
    c2 = 2 * c
    sgn = 1 - 2 * pl.program_id(0)
    ri = lax.broadcasted_iota(jnp.int32, (c2, c2), 0)
    ci = lax.broadcasted_iota(jnp.int32, (c2, c2), 1)
    same = (ri >= c) == (ci >= c)
    tt = jnp.where(ri >= c, ri - c, ri)
    ss = jnp.where(ci >= c, ci - c, ci)
    earlier = (ss - tt) * sgn < 0
    strict = jnp.logical_and(same, earlier)
    incl = jnp.logical_and(same, jnp.logical_or(earlier, ss == tt))
    eye = (ri == ci).astype(F32)
    r64 = lax.broadcasted_iota(jnp.int32, (c, c), 0)
    c64 = lax.broadcasted_iota(jnp.int32, (c, c), 1)
    ltri = jnp.where((c64 - r64) * sgn <= 0, 1.0, 0.0).astype(BF16)
    head0 = lax.broadcasted_iota(jnp.int32, (c, LANES), 1) < RW_HEAD
    pairs = range(npairs)

    def stack(x):
        return jnp.concatenate([jnp.where(head0, x, 0.0), jnp.where(head0, 0.0, x)], axis=0)

    def dup(x):
        return jnp.concatenate([x, x], axis=0)

    def bf(x):
        return x.astype(BF16)

    lhs, rhs, a2, bp2, kp2, vst, r2 = [], [], [], [], [], [], []
    lw_all = lw_ref[...]
    cum_all = _dot_exact_lhs(ltri, lw_all)
    for p in pairs:
        sl = slice(LANES * p, LANES * (p + 1))
        lw = lw_all[:, sl]
        cum = cum_all[:, sl]
        tot = jnp.sum(lw, axis=0, keepdims=True)
        p_inv = jnp.exp(-cum)
        p_end = jnp.exp(tot - cum)
        kk = kk_ref[:, sl]
        b = kk * as_ref[:, sl]
        kd = kd_ref[:, sl]
        a2p = stack(-kk * jnp.exp(cum - lw))
        r2p = stack(r_ref[:, sl] * jnp.exp(cum))
        a2.append(bf(a2p))
        r2.append(r2p)
        lhs.append(jnp.concatenate([a2[p], bf(r2p)], axis=0))
        rhs.append(jnp.concatenate([dup(bf(b * p_inv)), dup(bf(kd * p_inv))], axis=0))
        bp2.append(bf(stack(b * p_end)))
        kp2.append(bf(stack(kd * p_end)))
        vst.append(bf(stack(v_ref[:, sl])))
        pc_ref[:, sl] = jnp.broadcast_to(jnp.exp(tot), (SUBLANES, LANES))
    gram = [_dot_nt(lhs[p], rhs[p]) for p in pairs]
    nab = [jnp.where(strict, gram[p][0:c2, 0:c2], 0.0) for p in pairs]
    nrb = [bf(jnp.where(incl, gram[p][c2:2 * c2, 0:c2], 0.0)) for p in pairs]
    nk = [bf(jnp.concatenate([jnp.where(strict, gram[p][0:c2, c2:2 * c2], 0.0),
                              jnp.where(incl, gram[p][c2:2 * c2, c2:2 * c2], 0.0)], axis=0))
          for p in pairs]
    quads = range(npairs // 2)

    def side(x0, x1):
        return jnp.concatenate([x0, x1], axis=1)

    def diag(x0, x1):
        z0 = jnp.zeros_like(x0)
        return jnp.concatenate([side(x0, z0), side(z0, x1)], axis=0)

    def diag_halves(x):
        return diag(x[:, 0:c2], x[:, c2:2 * c2])

    def unside(xs):
        return [xs[p // 2][:, c2 * (p % 2):c2 * (p % 2 + 1)] for p in pairs]

    xv = unside([_dot(side(nk[2 * q], nk[2 * q + 1]), diag(vst[2 * q], vst[2 * q + 1]))
                 for q in quads])
    tinv = [side(eye + nab[2 * q], eye + nab[2 * q + 1]) for q in quads]
    npow = [bf(side(nab[2 * q], nab[2 * q + 1])) for q in quads]
    for _ in range(int(math.log2(c)) - 1):
        npow = [bf(_dot(npow[q], diag_halves(npow[q]))) for q in quads]
        tinv = [tinv[q] + _dot(bf(tinv[q]), diag_halves(npow[q])) for q in quads]
    tinv = unside(tinv)
    y = [_dot(bf(tinv[p]), jnp.concatenate([a2[p], bf(xv[p][0:c2, :])], axis=1)) for p in pairs]
    yb = [bf(y[p]) for p in pairs]
    z = [_dot(nrb[p], yb[p]) for p in pairs]
    mg = unside([_dot(bf(side(y[2 * q].T, y[2 * q + 1].T)), diag(bp2[2 * q], bp2[2 * q + 1]))
                 for q in quads])
    vk = unside([_dot(bf(side(vst[2 * q].astype(F32).T, vst[2 * q + 1].astype(F32).T)),
                      diag(kp2[2 * q], kp2[2 * q + 1])) for q in quads])
    for p in pairs:
        sl = slice(LANES * p, LANES * (p + 1))
        m_ref[:, sl] = bf(mg[p][0:c2, :])
        ga_ref[:, sl] = mg[p][c2:2 * c2, :] + vk[p]
        rq_ref[:, sl] = bf(r2[p] + z[p][:, 0:c2])
        o0 = z[p][:, c2:2 * c2] + xv[p][c2:2 * c2, :]
        o0_ref[:, sl] = o0[0:c, :] + o0[c:c2, :]


def _wkv_scan_kernel(m_ref, ga_ref, rq_ref, o0_ref, pc_ref, o_ref, g_scr, *, npairs):
    c = WKV_CHUNK

    @pl.when(pl.program_id(2) == 0)
    def _():
        g_scr[...] = jnp.zeros_like(g_scr)

    for p in range(npairs):
        sl = slice(LANES * p, LANES * (p + 1))
        g = g_scr[p]
        g_hi, g_lo = _split2(g)
        rq = rq_ref[:, sl]
        o_st = _dot_nt(rq, g_hi) + _dot_nt(rq, g_lo)
        o_ref[:, sl] = o_st[0:c, :] + o_st[c:2 * c, :] + o0_ref[:, sl]
        m = m_ref[:, sl]
        g_scr[p] = g * pc_ref[0:1, sl] + (_dot(g_hi, m) + _dot(g_lo, m)) + ga_ref[:, sl]


WKV_CHUNK_LANES = 2048


def _wkv_bidir(r, v, kk, lw, kd, asg, batch, ctx_len):
    n, d = r.shape
    c = WKV_CHUNK
    s = n // batch
    ncs = s // c
    ncc = ctx_len // c
    nch = n // c
    lanes = min(WKV_CHUNK_LANES, d)
    ngrp = d // lanes
    shared = pl.BlockSpec((c, lanes), lambda z, i, j: (i, j))
    perdir = pl.BlockSpec((None, c, lanes), lambda z, i, j: (z, i, j))
    big = pl.BlockSpec((None, 2 * c, lanes), lambda z, i, j: (z, i, j))
    m_, ga_, rq_, o0_, pc_ = pl.pallas_call(
        functools.partial(_wkv_chunk_kernel, npairs=lanes // LANES),
        grid=(2, nch, ngrp),
        in_specs=[shared, shared, shared, perdir, perdir, perdir],
        out_specs=[big, big, big, perdir,
                   pl.BlockSpec((None, SUBLANES, lanes), lambda z, i, j: (z, i, j))],
        out_shape=[jax.ShapeDtypeStruct((2, nch * 2 * c, d), dt) for dt in (BF16, F32, BF16)]
        + [jax.ShapeDtypeStruct((2, n, d), F32),
           jax.ShapeDtypeStruct((2, nch * SUBLANES, d), F32)],
        compiler_params=_cparams("parallel", "parallel", "parallel"),
        name="wkv_chunk",
    )(r, v, kk, lw, kd, asg)

    def cmap(z, b, cc):
        back = jnp.where(cc < ncc, ncc - 1 - cc, ncs + ncc - 1 - cc)
        return (z, b * ncs + jnp.where(z == 0, cc, back), 0)

    out = pl.pallas_call(
        functools.partial(_wkv_scan_kernel, npairs=d // LANES),
        grid=(2, batch, ncs),
        in_specs=[
            pl.BlockSpec((None, 2 * c, d), cmap),
            pl.BlockSpec((None, 2 * c, d), cmap),
            pl.BlockSpec((None, 2 * c, d), cmap),
            pl.BlockSpec((None, c, d), cmap),
            pl.BlockSpec((None, SUBLANES, d), cmap),
        ],
        out_specs=pl.BlockSpec((None, c, d), cmap),
        out_shape=jax.ShapeDtypeStruct((2, n, d), F32),
        scratch_shapes=[pltpu.VMEM((d // LANES, 2 * c, 2 * c), F32)],
        compiler_params=_cparams("parallel", "parallel", "arbitrary"),
        name="wkv_scan",
    )(m_, ga_, rq_, o0_, pc_)
    return out


def _rwread_kernel(o_ref, r_ref, kd_ref, v_ref, g_ref, pv_ref, bd_ref, y_ref):
    bd = bd_ref[...]
    wkv = o_ref[0] + o_ref[1]
    inv = 1.0 / RW_HEAD
    mu = _head_sum(wkv, bd) * inv
    dev = wkv - mu
    var = _head_sum(dev * dev, bd) * inv
    y = dev * lax.rsqrt(var + GN_EPS) * pv_ref[1:2, :] + pv_ref[2:3, :]
    rk = r_ref[...] * (kd_ref[0] + kd_ref[1]) * pv_ref[0:1, :]
    y = y + _head_sum(rk, bd) * v_ref[...]
    y_ref[...] = (y * g_ref[...]).astype(BF16)


def _rwkv_readout(geo, wkv, r, kd, v, g, pvec):
    n, d = r.shape
    tm, tc = geo.tm, 512
    ii = lax.broadcasted_iota(jnp.int32, (tc, tc), 0) // RW_HEAD
    jj = lax.broadcasted_iota(jnp.int32, (tc, tc), 1) // RW_HEAD
    bd = (ii == jj).astype(BF16)
    row = pl.BlockSpec((tm, tc), lambda i, j: (i, j))
    dirs = pl.BlockSpec((2, tm, tc), lambda i, j: (0, i, j))
    return pl.pallas_call(
        _rwread_kernel,
        grid=(geo.ntiles, d // tc),
        in_specs=[dirs, row, dirs, row, row,
                  pl.BlockSpec((SUBLANES, tc), lambda i, j: (0, j)),
                  pl.BlockSpec((tc, tc), lambda i, j: (0, 0))],
        out_specs=row,
        out_shape=jax.ShapeDtypeStruct((n, d), BF16),
        compiler_params=_cparams("parallel", "parallel"),
        name="rwkv_readout",
    )(wkv, r, kd, v, g, pvec, bd)


def _mlaqkv_kernel(z_ref, qn_ref, kvn_ref, wuq_ref, wukv_ref, gq_ref, gk_ref, cos_ref, sin_ref,
                   q_ref, k_ref, v_ref, *, q_lora, kv_lora):
    z = z_ref[...]
    cq = z[:, 0:q_lora]
    ckv = z[:, q_lora:q_lora + kv_lora]
    krot = z[:, q_lora + kv_lora:q_lora + kv_lora + LANES]

    def rms(x, g):
        ms = jnp.mean(x * x, axis=-1, keepdims=True)
        return x * lax.rsqrt(ms + EPS) * g

    qf = _dot(rms(cq, qn_ref[...]).astype(BF16), wuq_ref[...])
    kvf = _dot(rms(ckv, kvn_ref[...]).astype(BF16), wukv_ref[...])
    cos = cos_ref[...]
    sin = sin_ref[...]
    half = QK_ROPE // 2
    lane = lax.broadcasted_iota(jnp.int32, cos.shape, 1)

    def rope(x):
        up = pltpu.roll(x, LANES - half, axis=1)
        dn = pltpu.roll(x, half, axis=1)
        return x * cos + jnp.where(lane < half, up, dn) * sin

    inv_w = 1.0 / (QK_NOPE + QK_ROPE)
    gq = gq_ref[...]
    gk = gk_ref[...]
    kr_ss = jnp.sum(krot * krot, axis=-1, keepdims=True)
    for h in range(MLA_HEADS):
        o = QK_PAD * h
        qh = qf[:, o:o + QK_PAD]
        rs = lax.rsqrt(jnp.sum(qh * qh, axis=-1, keepdims=True) * inv_w + EPS)
        qn = qh * rs * gq * (ATTN_SCALE * math.log2(math.e))
        q_ref[:, o:o + QK_NOPE] = qn[:, 0:QK_NOPE].astype(BF16)
        q_ref[:, o + QK_NOPE:o + QK_PAD] = rope(qn[:, QK_NOPE:QK_PAD]).astype(BF16)
        kn = kvf[:, o:o + QK_NOPE]
        rsk = lax.rsqrt((jnp.sum(kn * kn, axis=-1, keepdims=True) + kr_ss) * inv_w + EPS)
        k_ref[:, o:o + QK_NOPE] = (kn * rsk * gk[:, 0:QK_NOPE]).astype(BF16)
        k_ref[:, o + QK_NOPE:o + QK_PAD] = rope(krot * rsk * gk[:, QK_NOPE:QK_PAD]).astype(BF16)
        v_ref[:, V_HEAD * h:V_HEAD * (h + 1)] = kvf[:, o + QK_NOPE:o + QK_PAD].astype(BF16)


def _mla_qkv(geo, z, qn, kvn, wuq_p, wukv, gq_p, gk_p, cos_t, sin_t):
    n, zw = z.shape
    tm = geo.tm
    q_lora, kv_lora = qn.shape[0], kvn.shape[0]
    hq = MLA_HEADS * QK_PAD
    tpb = geo.tpb
    full = lambda i: (0, 0)
    rowmap = lambda i: (i, 0)
    return pl.pallas_call(
        functools.partial(_mlaqkv_kernel, q_lora=q_lora, kv_lora=kv_lora),
        grid=(geo.ntiles,),
        in_specs=[
            pl.BlockSpec((tm, zw), rowmap),
            pl.BlockSpec((1, q_lora), full),
            pl.BlockSpec((1, kv_lora), full),
            pl.BlockSpec((q_lora, hq), full),
            pl.BlockSpec((kv_lora, hq), full),
            pl.BlockSpec((1, QK_PAD), full),
            pl.BlockSpec((1, QK_PAD), full),
            pl.BlockSpec((tm, LANES), lambda i: (i % tpb, 0)),
            pl.BlockSpec((tm, LANES), lambda i: (i % tpb, 0)),
        ],
        out_specs=[pl.BlockSpec((tm, hq), rowmap), pl.BlockSpec((tm, hq), rowmap),
                   pl.BlockSpec((tm, MLA_HEADS * V_HEAD), rowmap)],
        out_shape=[jax.ShapeDtypeStruct((n, hq), BF16), jax.ShapeDtypeStruct((n, hq), BF16),
                   jax.ShapeDtypeStruct((n, MLA_HEADS * V_HEAD), BF16)],
        compiler_params=_cparams("parallel"),
        name="mla_qkv",
    )(z, qn.reshape(1, -1), kvn.reshape(1, -1), wuq_p, wukv, gq_p, gk_p, cos_t, sin_t)


ATTN_KEY_CHUNKS = 4


def _attn_kernel(q_ref, k_ref, vt_ref, o_ref):
    q = q_ref[...]
    nkeys = k_ref.shape[0]
    unit = LANES if nkeys % LANES == 0 else nkeys
    tiles = nkeys // unit
    nchunks = min(ATTN_KEY_CHUNKS, tiles)
    sizes = [(tiles // nchunks + (1 if j < tiles % nchunks else 0)) * unit for j in range(nchunks)]
    starts = [sum(sizes[:j]) for j in range(nchunks)]
    def scores(j):
        return _dot_nt(k_ref[starts[j]:starts[j] + sizes[j], :], q)

    m = jnp.full((1, q.shape[0]), -1e30, F32)
    l = jnp.zeros((1, q.shape[0]), F32)
    acc = jnp.zeros((vt_ref.shape[0], q.shape[0]), F32)
    s_next = scores(0)
    for j in range(nchunks):
        s_cur = s_next
        if j + 1 < nchunks:
            s_next = scores(j + 1)
        m_new = jnp.maximum(m, jnp.max(s_cur, axis=0, keepdims=True))
        alpha = jnp.exp2(m - m_new)
        p = jnp.exp2(s_cur - m_new)
        l = l * alpha + jnp.sum(p, axis=0, keepdims=True)
        acc = acc * alpha + _dot(vt_ref[:, starts[j]:starts[j] + sizes[j]], p.astype(BF16))
        m = m_new
    o_ref[...] = (acc / l).T.astype(BF16)


def _attn_call(q3, k3, vt4, tq, nkeys):
    b, sq, _ = q3.shape
    return pl.pallas_call(
        _attn_kernel,
        grid=(b, MLA_HEADS, sq // tq),
        in_specs=[
            pl.BlockSpec((None, tq, QK_PAD), lambda bb, h, i: (bb, i, h)),
            pl.BlockSpec((None, nkeys, QK_PAD), lambda bb, h, i: (bb, 0, h)),
            pl.BlockSpec((None, None, V_HEAD, nkeys), lambda bb, h, i: (bb, h, 0, 0)),
        ],
        out_specs=pl.BlockSpec((None, tq, V_HEAD), lambda bb, h, i: (bb, i, h)),
        out_shape=jax.ShapeDtypeStruct((b, sq, MLA_HEADS * V_HEAD), BF16),
        compiler_params=_cparams("parallel", "parallel", "parallel"),
        name="mla_attention",
    )(q3, k3, vt4)


def _mla_attention(geo, q, k, v):
    b, s, l = geo.b, geo.s, geo.l
    q3 = q.reshape(b, s, MLA_HEADS * QK_PAD)
    k3 = k.reshape(b, s, MLA_HEADS * QK_PAD)
    vt4 = v.reshape(b, s, MLA_HEADS, V_HEAD).transpose(0, 2, 3, 1)
    o_ctx = _attn_call(q3[:, :l], k3, vt4, geo.tm, l)
    tq_lat = 2 * geo.tm if geo.t % (2 * geo.tm) == 0 else geo.tm
    o_lat = _attn_call(q3[:, l:], k3, vt4, tq_lat, s)
    return jnp.concatenate([o_ctx, o_lat], axis=1).reshape(b * s, MLA_HEADS * V_HEAD)


def _topk_rows(s, k, payload=None):
    rows = lax.broadcasted_iota(jnp.int32, s.shape, 0).astype(F32)
    big = float(s.shape[0])
    vals, idxs = [], []
    for _ in range(k):
        m = jnp.max(s, axis=0, keepdims=True)
        idx = jnp.min(jnp.where(s == m, rows, big), axis=0, keepdims=True)
        hit = rows == idx
        vals.append(m)
        if payload is None:
            idxs.append(idx)
        else:
            idxs.append(jnp.sum(jnp.where(hit, payload, 0.0), axis=0, keepdims=True))
        s = jnp.where(hit, -jnp.inf, s)
    return jnp.concatenate(vals, axis=0), jnp.concatenate(idxs, axis=0)


def _peersel_kernel(q_ref, qn_ref, keys_ref, e_ref, g_ref):
    q = q_ref[...]
    ms = jnp.mean(q * q, axis=-1, keepdims=True)
    qn = q * lax.rsqrt(ms + EPS) * qn_ref[...]
    half = D_KEY // 2
    s1 = _dot3_nt(keys_ref[0], qn[:, 0:half])
    s2 = _dot3_nt(keys_ref[1], qn[:, half:D_KEY])
    t1, i1 = _topk_rows(s1, PEER_TOPK)
    t2, i2 = _topk_rows(s2, PEER_TOPK)
    k = PEER_TOPK
    sub = lax.broadcasted_iota(jnp.int32, (SUBLANES, t1.shape[1]), 0)
    cand = [t1[0:1, :] + t2]
    cidx = [i1[0:1, :] * float(N_KEYS) + i2]
    for p in range(1, k // 2):
        live = sub < k // (p + 1)
        cand.append(jnp.where(live, t1[p:p + 1, :] + t2[0:SUBLANES, :], -jnp.inf))
        cidx.append(i1[p:p + 1, :] * float(N_KEYS) + i2[0:SUBLANES, :])
    cand.append(t1[k // 2:k, :] + t2[0:1, :])
    cidx.append(i1[k // 2:k, :] * float(N_KEYS) + i2[0:1, :])
    best, eidx = _topk_rows(jnp.concatenate(cand, axis=0), k, payload=jnp.concatenate(cidx, axis=0))
    ex = jnp.exp(best - jnp.max(best, axis=0, keepdims=True))
    g_ref[...] = ex / jnp.sum(ex, axis=0, keepdims=True)
    e_ref[...] = eidx.astype(jnp.int32)


def _peer_select(qp, q_norm, keys, tm, row0, n):
    assert row0 % tm == 0 and n % tm == 0
    tile0 = row0 // tm
    return pl.pallas_call(
        _peersel_kernel,
        grid=(n // tm, PEER_HEADS),
        in_specs=[
            pl.BlockSpec((tm, D_KEY), lambda i, h: (tile0 + i, h)),
            pl.BlockSpec((1, D_KEY), lambda i, h: (0, 0)),
            pl.BlockSpec((2, N_KEYS, D_KEY // 2), lambda i, h: (0, 0, 0)),
        ],
        out_specs=[pl.BlockSpec((PEER_TOPK, tm), lambda i, h: (h, i)),
                   pl.BlockSpec((PEER_TOPK, tm), lambda i, h: (h, i))],
        out_shape=[jax.ShapeDtypeStruct((PEER_HEADS * PEER_TOPK, n), jnp.int32),
                   jax.ShapeDtypeStruct((PEER_HEADS * PEER_TOPK, n), F32)],
        compiler_params=_cparams("parallel", "parallel"),
        name="peer_select",
    )(qp, q_norm.reshape(1, D_KEY), keys)


GATHER_SLOTS = 4
SLAB_PAD = 1


def _pack_expert_table(u, v):
    ne, d = u.shape
    ub = lax.bitcast_convert_type(u.astype(BF16), jnp.uint16).astype(jnp.uint32)
    vb = lax.bitcast_convert_type(v.astype(BF16), jnp.uint16).astype(jnp.uint32)
    return ((vb << 16) | ub).reshape(ne, d // LANES, LANES)


def _peergather_kernel(idx_ref, idxn_ref, gate_ref, h_ref, x_ref, mod_ref, tab_ref, o_ref,
                       *scratch, tb):
    nsel = PEER_HEADS * PEER_TOPK
    nrow = h_ref.shape[1] // LANES
    pitch = nrow + SLAB_PAD
    ns = GATHER_SLOTS
    bufs, sem = scratch[:ns], scratch[ns]
    lane_t = lax.broadcasted_iota(jnp.int32, (nsel, tb), 1)
    g2 = mod_ref[5:6, :]
    step = pl.program_id(0)
    nsteps = pl.num_programs(0)

    def row_copy(ids_ref, t, j, slot):
        return pltpu.make_async_copy(
            tab_ref.at[ids_ref[t, j]],
            bufs[slot].at[pl.ds(j * pitch, nrow), :],
            sem.at[slot])

    def issue(ids_ref, t, slot):
        for j in range(nsel):
            row_copy(ids_ref, t, j, slot).start(priority=j % 2)

    def wait(t, slot):
        for j in range(nsel):
            row_copy(idx_ref, t, j, slot).wait()

    def packed(slot, s):
        return bufs[slot][pl.ds(s, nsel, stride=pitch), :]

    def compute(t, slot):
        hrow = h_ref[pl.ds(t, 1), :]
        acc = jnp.zeros((nsel, LANES), F32)
        for s in range(nrow):
            u = lax.bitcast_convert_type(packed(slot, s) << 16, F32)
            acc = acc + u * hrow[:, LANES * s:LANES * (s + 1)]
        dots = jnp.sum(acc, axis=-1, keepdims=True)
        gcol = jnp.sum(jnp.where(lane_t == t, gate_ref[...], 0.0), axis=-1, keepdims=True)
        coef = gcol * _gelu(dots)
        outs = []
        for s in range(nrow):
            vv = lax.bitcast_convert_type(packed(slot, s) & jnp.uint32(0xFFFF0000), F32)
            outs.append(jnp.sum(coef * vv, axis=0, keepdims=True))
        orow = jnp.concatenate(outs, axis=1)
        o_ref[pl.ds(t, 1), :] = x_ref[pl.ds(t, 1), :] + g2 * orow

    @pl.when(step == 0)
    def _():
        for s in range(ns - 1):
            issue(idx_ref, s, s)

    ngroups = tb // ns

    def body(g, carry):
        for s in range(ns):
            t = g * ns + s
            wait(t, s)
            issue(idx_ref, t + ns - 1, (s + ns - 1) % ns)
            compute(t, s)
        return carry

    lax.fori_loop(0, ngroups - 1, body, 0)
    for s in range(ns):
        t = (ngroups - 1) * ns + s
        wait(t, s)
        if s == 0:
            issue(idx_ref, tb - 1, ns - 1)
        else:
            @pl.when(step < nsteps - 1)
            def _():
                issue(idxn_ref, s - 1, s - 1)
        compute(t, s)


def _peer_gather(geo_g, eidx, gate_t, h2, x, mods, table, n):
    d = x.shape[1]
    tb = geo_g.tm
    nsel = PEER_HEADS * PEER_TOPK
    pitch = d // LANES + SLAB_PAD
    assert n % tb == 0
    nsteps = n // tb
    return pl.pallas_call(
        functools.partial(_peergather_kernel, tb=tb),
        grid=(nsteps,),
        in_specs=[
            pl.BlockSpec((tb, nsel), lambda i: (i, 0), memory_space=pltpu.SMEM),
            pl.BlockSpec((tb, nsel), lambda i: (jnp.minimum(i + 1, nsteps - 1), 0),
                         memory_space=pltpu.SMEM),
            pl.BlockSpec((nsel, tb), lambda i: (0, i)),
            pl.BlockSpec((tb, d), lambda i: (i, 0)),
            pl.BlockSpec((tb, d), lambda i: (i, 0)),
            geo_g.mod_spec(),
            pl.BlockSpec(memory_space=pl.ANY),
        ],
        out_specs=pl.BlockSpec((tb, d), lambda i: (i, 0)),
        out_shape=jax.ShapeDtypeStruct((n, d), F32),
        scratch_shapes=[pltpu.VMEM((nsel * pitch, LANES), jnp.uint32)] * GATHER_SLOTS
        + [pltpu.SemaphoreType.DMA((GATHER_SLOTS,))],
        compiler_params=_cparams("arbitrary"),
        name="peer_gather",
    )(eidx, eidx, gate_t, h2, x, mods, table)


SC_CORES = 2
SC_SUBCORES = 16
SC_LANES = 16
SC_CHUNK = 16
SC_SHARE = 0.3235

_ERF_ALPHA = (0.00022905065861350646, 0.0034082910107109506, 0.050955695062380861,
              0.18520832239976145, 1.128379143519084)
_ERF_BETA = (-1.1791602954361697e-7, 0.000023547966471313185, 0.0010179625278914885,
             0.014070470171167667, 0.11098505178285362, 0.49746925110067538, 1.0)
_ERF_CLAMP = 3.832506856900711


def _erf_rational(x):
    x = jnp.minimum(jnp.maximum(x, -_ERF_CLAMP), _ERF_CLAMP)
    x2 = x * x
    p = jnp.full_like(x, _ERF_ALPHA[0])
    for c in _ERF_ALPHA[1:]:
        p = p * x2 + c
    q = jnp.full_like(x, _ERF_BETA[0])
    for c in _ERF_BETA[1:]:
        q = q * x2 + c
    return x * p / q


def _peer_sc(eidx, gate, h, table, row0):
    n, nsel = eidx.shape
    d = h.shape[1]
    nw = SC_CORES * SC_SUBCORES
    assert n % (2 * nw) == 0 and nsel % SC_CHUNK == 0 and d % SC_LANES == 0
    tpw = n // nw
    nchunk = nsel // SC_CHUNK
    nvec = d // SC_LANES
    mesh = plsc.VectorSubcoreMesh(core_axis_name="c", subcore_axis_name="s",
                                  num_cores=SC_CORES, num_subcores=SC_SUBCORES)

    @functools.partial(
        pl.kernel, mesh=mesh,
        out_type=jax.ShapeDtypeStruct((n, d), F32),
        scratch_types=[
            pltpu.VMEM((nsel,), jnp.int32), pltpu.VMEM((nsel,), jnp.int32),
            pltpu.VMEM((nsel,), F32), pltpu.VMEM((nsel,), F32),
            pltpu.VMEM((d,), F32), pltpu.VMEM((d,), F32),
            pltpu.VMEM((d,), F32),
            pltpu.VMEM((SC_CHUNK, d), jnp.uint32),
            pltpu.VMEM((SC_CHUNK, d), jnp.uint32),
            pltpu.SemaphoreType.DMA, pltpu.SemaphoreType.DMA,
            pltpu.SemaphoreType.DMA, pltpu.SemaphoreType.DMA,
        ],
        compiler_params=pltpu.CompilerParams(needs_layout_passes=False),
        name="peer_sc",
    )
    def sc_kernel(eidx_hbm, gate_hbm, h_hbm, tab_hbm, y_hbm, idx_a, idx_b, gate_a, gate_b, h_a, h_b,
                  o_v, rows0, rows1, sem0, sem1, msem_a, msem_b):
        wid = lax.axis_index("s") * SC_CORES + lax.axis_index("c")
        lanes = lax.iota(jnp.int32, SC_LANES)
        zero = jnp.zeros((SC_LANES,), F32)
        bufs = ((rows0, sem0), (rows1, sem1))
        meta = ((idx_a, gate_a, h_a, msem_a), (idx_b, gate_b, h_b, msem_b))

        def meta_copies(tok, s):
            idx_v, gate_v, h_v, msem = meta[s]
            return (pltpu.make_async_copy(eidx_hbm.at[tok], idx_v, msem),
                    pltpu.make_async_copy(gate_hbm.at[tok], gate_v, msem),
                    pltpu.make_async_copy(h_hbm.at[row0 + tok], h_v, msem))

        def gather(s, c, slot):
            rows, sem = bufs[slot]
            return pltpu.make_async_copy(
                tab_hbm.at[meta[s][0].at[pl.ds(c * SC_CHUNK, SC_CHUNK)]], rows, sem)

        def chunk_compute(s, c, rows):
            _, gate_v, h_v, _ = meta[s]

            def dot_body(i, accs):
                hv = h_v[pl.ds(i * SC_LANES, SC_LANES)]
                out = []
                for e in range(SC_CHUNK):
                    w = rows[e, pl.ds(i * SC_LANES, SC_LANES)]
                    out.append(accs[e] + lax.bitcast_convert_type(w << 16, F32) * hv)
                return tuple(out)

            accs = lax.fori_loop(0, nvec, dot_body, (zero,) * SC_CHUNK)
            dots = zero
            for e in range(SC_CHUNK):
                dots = jnp.where(lanes == e, jnp.sum(accs[e]), dots)
            act = 0.5 * dots * (1.0 + _erf_rational(dots * (2.0 ** -0.5)))
            coef = gate_v[pl.ds(c * SC_CHUNK, SC_CHUNK)] * act
            splat = [jnp.full((SC_LANES,), jnp.sum(jnp.where(lanes == e, coef, 0.0)), F32)
                     for e in range(SC_CHUNK)]

            def acc_body(i, carry):
                o = o_v[pl.ds(i * SC_LANES, SC_LANES)]
                for e in range(SC_CHUNK):
                    w = rows[e, pl.ds(i * SC_LANES, SC_LANES)]
                    o = o + splat[e] * lax.bitcast_convert_type(w & jnp.uint32(0xFFFF0000), F32)
                o_v[pl.ds(i * SC_LANES, SC_LANES)] = o
                return carry

            lax.fori_loop(0, nvec, acc_body, 0)

        def zero_body(i, c2):
            o_v[pl.ds(i * SC_LANES, SC_LANES)] = zero
            return c2

        def token(tok, s, has_next):
            def when_next(fn):
                if isinstance(has_next, bool):
                    if has_next:
                        fn()
                else:
                    pl.when(has_next)(fn)

            def load_next():
                for cp in meta_copies(tok + 1, 1 - s):
                    cp.start()

            when_next(load_next)
            lax.fori_loop(0, nvec, zero_body, 0)
            for c in range(nchunk):
                if c + 1 < nchunk:
                    gather(s, c + 1, (c + 1) % 2).start()
                gather(s, c, c % 2).wait()
                chunk_compute(s, c, bufs[c % 2][0])

            def prefetch_next():
                for cp in meta_copies(tok + 1, 1 - s):
                    cp.wait()
                gather(1 - s, 0, 0).start()

            when_next(prefetch_next)
            pltpu.sync_copy(o_v, y_hbm.at[tok])

        base = wid * tpw
        for cp in meta_copies(base, 0):
            cp.start()
        for cp in meta_copies(base, 0):
            cp.wait()
        gather(0, 0, 0).start()

        def pair_body(g, carry):
            token(base + 2 * g, 0, True)
            token(base + 2 * g + 1, 1, g + 1 < tpw // 2)
            return carry

        lax.fori_loop(0, tpw // 2, pair_body, 0)

    return sc_kernel(eidx, gate, h, table)


def _pad_to(x, axis, size):
    pad = [(0, 0)] * x.ndim
    pad[axis] = (0, size - x.shape[axis])
    return jnp.pad(x, pad)


def _rwkv_layer(geo, xs, mods, norm1, mix, w_rkv, w_o, w0, w1, w2, a0, a1, a2, vl, g1, g2,
                k_k, k_a, r_k, ln_w, ln_b, vfirst):
    d = geo.d
    xm = _rwkv_mix(geo, xs, norm1, mods, mix)
    rkv = _bmm(xm, w_rkv.astype(BF16), (0, 2, 3), ("none",) * 3, F32, geo.tm)
    lt = g1.shape[1]
    w1c = jnp.concatenate([_pad_to(w1[0], 1, LORA_PAD), _pad_to(w1[1], 1, LORA_PAD)], axis=1)
    a1c = jnp.concatenate([_pad_to(a1[0], 1, LORA_PAD), _pad_to(a1[1], 1, LORA_PAD)], axis=1)
    if vl is None:
        v1p = jnp.zeros((d, lt), F32)
        v2p = jnp.zeros((LORA_PAD, d), F32)
        v0 = jnp.zeros((d,), F32)
    else:
        v0, v1, v2 = vl
        v1p = _pad_to(v1, 1, lt)
        v2p = _pad_to(v2, 0, LORA_PAD)
    wl1 = jnp.stack([_pad_to(w1c, 1, lt), _pad_to(a1c, 1, lt), g1, v1p]).astype(BF16)
    tl = _bmm(xm, wl1, (1, 4, 5, 3), ("tanh", "none", "sigmoid", "none"), BF16, geo.tm)
    w2p = jnp.stack([_pad_to(w2[0], 0, LORA_PAD), _pad_to(w2[1], 0, LORA_PAD)]).astype(BF16)
    a2p = jnp.stack([_pad_to(a2[0], 0, LORA_PAD), _pad_to(a2[1], 0, LORA_PAD)]).astype(BF16)
    pvec = jnp.stack([w0[0], w0[1], a0[0], a0[1], v0, k_k, k_a, jnp.zeros_like(k_k)])
    lw, kd, asg, kk, g, v = _rwkv_feat(geo, rkv, tl, w2p, a2p, g2.astype(BF16), v2p.astype(BF16),
                                      pvec, vfirst)
    r = rkv[0]
    wkv = _wkv_bidir(r, v, kk, lw, kd, asg, geo.b, geo.l)
    pv2 = _pad_to(jnp.stack([r_k, ln_w, ln_b]), 0, SUBLANES)
    y = _rwkv_readout(geo, wkv, r, kd, v, g, pv2)
    xs = _matmul_res(geo, y, w_o.astype(BF16), xs, mods, 2)
    return xs, v


def _rope_tables(geo):
    t = geo.t
    pos = jnp.arange(t)
    row = (pos // GRID_W).astype(F32)
    col = (pos % GRID_W).astype(F32)
    n_freq = QK_ROPE // 4
    inv_freq = ROPE_THETA ** (-jnp.arange(n_freq, dtype=F32) / n_freq)
    ang = jnp.concatenate([row[:, None] * inv_freq, col[:, None] * inv_freq], axis=-1)
    cos, sin = jnp.cos(ang), jnp.sin(ang)
    pad = LANES - QK_ROPE
    cos_l = jnp.concatenate([cos, cos, jnp.ones((t, pad), F32)], axis=1)
    sin_l = jnp.concatenate([-sin, sin, jnp.zeros((t, pad), F32)], axis=1)
    cos_c = jnp.ones((geo.l, LANES), F32)
    sin_c = jnp.zeros((geo.l, LANES), F32)
    return jnp.concatenate([cos_c, cos_l], axis=0), jnp.concatenate([sin_c, sin_l], axis=0)


def _mla_layer(geo, xs, mods, norm1, rope_t, w_in, q_norm, kv_norm, w_uq, w_ukv, g_q, g_k, w_o):
    q_lora, kv_lora = q_norm.shape[0], kv_norm.shape[0]
    zw = q_lora + kv_lora + LANES
    z = _mod_matmul(geo, xs, norm1, mods, _pad_to(w_in, 1, zw).astype(BF16), 0, False)
    qk = QK_NOPE + QK_ROPE
    wuq_p = _pad_to(w_uq.reshape(q_lora, MLA_HEADS, qk), 2, QK_PAD).reshape(q_lora, -1)
    gq_p = _pad_to(g_q, 0, QK_PAD).reshape(1, QK_PAD)
    gk_p = _pad_to(g_k, 0, QK_PAD).reshape(1, QK_PAD)
    q, k, v = _mla_qkv(geo, z, q_norm, kv_norm, wuq_p.astype(BF16), w_ukv.astype(BF16),
                       gq_p, gk_p, *rope_t)
    o = _mla_attention(geo, q, k, v)
    return _matmul_res(geo, o, w_o.astype(BF16), xs, mods, 2)


def _peer_layer(geo, geo_g, xs, mods, norm2, w_q, q_norm, keys, u, v):
    qp, h2 = _mod_matmul(geo, xs, norm2, mods, w_q.astype(BF16), 1, True)
    table = _pack_expert_table(u, v)
    n, d = xs.shape
    n_sc = geo.tm * round(SC_SHARE * n / geo.tm)
    if n_sc % (2 * SC_CORES * SC_SUBCORES) != 0:
        n_sc = 0
    n_tc = n - n_sc
    if n_sc:
        eidx_sc, gate_sc = _peer_select(qp, q_norm, keys, geo.tm, n_tc, n_sc)
        y_sc = _peer_sc(eidx_sc.T, gate_sc.T, h2, table.reshape(table.shape[0], d), n_tc)
    eidx_t, gate_t = _peer_select(qp, q_norm, keys, geo.tm, 0, n_tc)
    out_tc = _peer_gather(geo_g, eidx_t.T, gate_t, h2, xs, mods, table, n_tc)
    if n_sc == 0:
        return out_tc
    g2, r = [], n_tc
    while r < n:
        bi, pos = divmod(r, geo.s)
        seg = int(pos >= geo.l)
        stop = min(n, bi * geo.s + (geo.s if seg else geo.l))
        g2.append(jnp.broadcast_to(mods[bi, seg, 5], (stop - r, d)))
        r = stop
    out_sc = xs[n_tc:] + jnp.concatenate(g2, axis=0) * y_sc
    return jnp.concatenate([out_tc, out_sc], axis=0)


def kernel(x, c, ctx, c_ctx, w_ada, b_ada, norm1, norm2, rw_mix, rw_wrkv, rw_wo, rw_w0, rw_w1, rw_w2, rw_a0, rw_a1, rw_a2, rw_v0, rw_v1, rw_v2, rw_g1, rw_g2, rw_kk, rw_ka, rw_rk, rw_lnw, rw_lnb, mla_win, mla_qnorm, mla_kvnorm, mla_wuq, mla_wukv, mla_gq, mla_gk, mla_wo, peer_wq, peer_qnorm, peer_keys, peer_u, peer_v):
    b, t, d = x.shape
    l = ctx.shape[1]
    depth = w_ada.shape[0]
    geo = _Geom(b, l, t, d, min(256, l))
    geo_g = _Geom(b, l, t, d, min(128, l))
    cond8 = _pad_to(jnp.concatenate([c, c_ctx[None, :]], axis=0), 0, SUBLANES)
    ada = _adaln(cond8, w_ada, b_ada).reshape(depth, SUBLANES, 6, d)
    mods_all = jnp.stack([jnp.broadcast_to(ada[:, b:b + 1], (depth, b, 6, d)), ada[:, 0:b]], axis=2)
    xs = jnp.concatenate([ctx, x], axis=1).reshape(b * (l + t), d)
    rope_t = _rope_tables(geo)
    vfirst = None
    for i in range(depth):
        j = i // 2
        mods = mods_all[i]
        if i % 2 == 0:
            vl = None if j == 0 else (rw_v0[j - 1], rw_v1[j - 1], rw_v2[j - 1])
            xs, vcur = _rwkv_layer(geo, xs, mods, norm1[i], rw_mix[j], rw_wrkv[j], rw_wo[j],
                                   rw_w0[j], rw_w1[j], rw_w2[j], rw_a0[j], rw_a1[j], rw_a2[j], vl,
                                   rw_g1[j], rw_g2[j], rw_kk[j], rw_ka[j], rw_rk[j], rw_lnw[j],
                                   rw_lnb[j], vfirst)
            if j == 0:
                vfirst = vcur
        else:
            xs = _mla_layer(geo, xs, mods, norm1[i], rope_t, mla_win[j], mla_qnorm[j],
                            mla_kvnorm[j], mla_wuq[j], mla_wukv[j], mla_gq[j], mla_gk[j], mla_wo[j])
        if i == depth - 1:
            xs = xs.reshape(b, l + t, d)[:, l:, :].reshape(b * t, d)
            geo, geo_g = _Geom(b, 0, t, d, geo.tm), _Geom(b, 0, t, d, geo_g.tm)
        xs = _peer_layer(geo, geo_g, xs, mods, norm2[i], peer_wq[i], peer_qnorm[i], peer_keys[i],
                         peer_u[i], peer_v[i])
    return xs.reshape(b, t, d)
```

```python
import functools
import math

import jax
import jax.numpy as jnp
from jax import lax
from jax.experimental import pallas as pl
from jax.experimental.pallas import tpu as pltpu
from jax.experimental.pallas import tpu_sc as plsc

F32 = jnp.float32
BF16 = jnp.bfloat16

EPS = 1e-6
GN_EPS = 64e-5
RW_HEAD = 64
WKV_CHUNK = 64
MLA_HEADS = 16
QK_NOPE = 128
QK_ROPE = 64
V_HEAD = 128
QK_PAD = 256
ROPE_THETA = 10000.0
GRID_W = 64
ATTN_SCALE = (QK_NOPE + QK_ROPE) ** -0.5
PEER_HEADS = 8
N_KEYS = 128
PEER_TOPK = 16
D_KEY = 256
LORA_PAD = 128

LANES = 128
SUBLANES = 8
VMEM_LIMIT = 56 * 1024 * 1024


def _cparams(*sem):
    return pltpu.CompilerParams(dimension_semantics=sem, vmem_limit_bytes=VMEM_LIMIT)


def _dot(a, b):
    return jnp.dot(a, b, preferred_element_type=F32)


def _dot_nt(a, b):
    return lax.dot_general(a, b, (((1,), (1,)), ((), ())), preferred_element_type=F32)


def _split2(x):
    hi = x.astype(BF16)
    lo = (x - hi.astype(F32)).astype(BF16)
    return hi, lo


def _split3(x):
    hi = x.astype(BF16)
    r1 = x - hi.astype(F32)
    mid = r1.astype(BF16)
    lo = (r1 - mid.astype(F32)).astype(BF16)
    return hi, mid, lo


def _dot3(a, b):
    ah, al = _split2(a)
    bh, bl = _split2(b)
    return _dot(ah, bh) + (_dot(ah, bl) + _dot(al, bh))


def _dot3_nt(a, b):
    ah, al = _split2(a)
    bh, bl = _split2(b)
    return _dot_nt(ah, bh) + (_dot_nt(ah, bl) + _dot_nt(al, bh))


def _dot_exact_lhs(sel, x):
    hi, mid, lo = _split3(x)
    return _dot(sel, hi) + (_dot(sel, mid) + _dot(sel, lo))


def _modulate(x, g, shift, scale):
    ms = jnp.mean(x * x, axis=-1, keepdims=True)
    return (x * lax.rsqrt(ms + EPS) * g) * (1.0 + scale) + shift


def _sigmoid(x):
    return 1.0 / (1.0 + jnp.exp(-x))


def _softplus(y):
    return jnp.maximum(y, 0.0) + jnp.log(1.0 + jnp.exp(-jnp.abs(y)))


def _erf(x):
    return lax.erf(x)


def _gelu(x):
    return 0.5 * x * (1.0 + _erf(x * (2.0 ** -0.5)))


def _ada_kernel(s_ref, w_ref, b_ref, o_ref):
    s = s_ref[...]
    s = s * _sigmoid(s)
    o_ref[...] = _dot3(s, w_ref[...]) + b_ref[...]


def _adaln(cond8, w_ada, b_ada):
    depth, d, n = w_ada.shape
    tn = 1024
    return pl.pallas_call(
        _ada_kernel,
        grid=(depth, n // tn),
        in_specs=[
            pl.BlockSpec((SUBLANES, d), lambda l, j: (0, 0)),
            pl.BlockSpec((None, d, tn), lambda l, j: (l, 0, j)),
            pl.BlockSpec((None, 1, tn), lambda l, j: (l, 0, j)),
        ],
        out_specs=pl.BlockSpec((None, SUBLANES, tn), lambda l, j: (l, 0, j)),
        out_shape=jax.ShapeDtypeStruct((depth, SUBLANES, n), F32),
        compiler_params=_cparams("parallel", "parallel"),
        name="adaln",
    )(cond8, w_ada, b_ada.reshape(depth, 1, n))


class _Geom:
    def __init__(self, batch, ctx_len, seq_len, d_model, tm):
        self.b, self.l, self.t, self.d = batch, ctx_len, seq_len, d_model
        self.s = ctx_len + seq_len
        self.n = batch * self.s
        self.tm = tm
        assert ctx_len % tm == 0 and seq_len % tm == 0
        self.tpb = self.s // tm
        self.nct = ctx_len // tm
        self.ntiles = self.n // tm

    def mod_spec(self, nlead=0):
        tpb, nct = self.tpb, self.nct

        def imap(*ids):
            i = ids[nlead]
            return (i // tpb, ((i % tpb) >= nct).astype(jnp.int32), 0, 0)

        return pl.BlockSpec((None, None, 6, self.d), imap)


def _modmm_kernel(x_ref, g_ref, mod_ref, w_ref, o_ref, *h_ref, which):
    h = _modulate(x_ref[...], g_ref[...], mod_ref[3 * which:3 * which + 1, :],
                  mod_ref[3 * which + 1:3 * which + 2, :])
    if h_ref:
        h_ref[0][...] = h
    o_ref[...] = _dot(h.astype(BF16), w_ref[...])


def _mod_matmul(geo, x, g, mods, w, which, emit_h):
    n, d = x.shape
    nn = w.shape[1]
    tm = geo.tm
    out_shape = [jax.ShapeDtypeStruct((n, nn), F32)]
    out_specs = [pl.BlockSpec((tm, nn), lambda i: (i, 0))]
    if emit_h:
        out_shape.append(jax.ShapeDtypeStruct((n, d), F32))
        out_specs.append(pl.BlockSpec((tm, d), lambda i: (i, 0)))
    res = pl.pallas_call(
        functools.partial(_modmm_kernel, which=which),
        grid=(geo.ntiles,),
        in_specs=[
            pl.BlockSpec((tm, d), lambda i: (i, 0)),
            pl.BlockSpec((1, d), lambda i: (0, 0)),
            geo.mod_spec(),
            pl.BlockSpec((d, nn), lambda i: (0, 0)),
        ],
        out_specs=out_specs,
        out_shape=out_shape,
        compiler_params=_cparams("parallel"),
        name="mod_matmul",
    )(x, g.reshape(1, d), mods, w)
    return res if emit_h else res[0]


def _mmres_kernel(y_ref, w_ref, x_ref, mod_ref, o_ref, *, gidx):
    acc = _dot(y_ref[...], w_ref[...])
    o_ref[...] = x_ref[...] + mod_ref[gidx:gidx + 1, :] * acc


def _matmul_res(geo, y, w, x, mods, gidx):
    n, k = y.shape
    d = x.shape[1]
    tm = geo.tm
    return pl.pallas_call(
        functools.partial(_mmres_kernel, gidx=gidx),
        grid=(geo.ntiles,),
        in_specs=[
            pl.BlockSpec((tm, k), lambda i: (i, 0)),
            pl.BlockSpec((k, d), lambda i: (0, 0)),
            pl.BlockSpec((tm, d), lambda i: (i, 0)),
            geo.mod_spec(),
        ],
        out_specs=pl.BlockSpec((tm, d), lambda i: (i, 0)),
        out_shape=jax.ShapeDtypeStruct((n, d), F32),
        compiler_params=_cparams("parallel"),
        name="matmul_res",
    )(y, w, x, mods)


def _bmm_kernel(x_ref, w_ref, o_ref, *, acts):
    j = pl.program_id(0)
    y = _dot(x_ref[...], w_ref[...])
    out = y
    for jj, a in enumerate(acts):
        if a == "tanh":
            out = jnp.where(j == jj, jnp.tanh(y), out)
        elif a == "sigmoid":
            out = jnp.where(j == jj, _sigmoid(y), out)
    o_ref[...] = out.astype(o_ref.dtype)


def _bmm(x3, w3, src, acts, out_dtype, tm):
    _, n, k = x3.shape
    nj, _, nn = w3.shape
    src = tuple(src)

    def xmap(j, i):
        idx = jnp.int32(src[0])
        for jj in range(1, nj):
            idx = jnp.where(j == jj, jnp.int32(src[jj]), idx)
        return (idx, i, 0)

    return pl.pallas_call(
        functools.partial(_bmm_kernel, acts=tuple(acts)),
        grid=(nj, n // tm),
        in_specs=[
            pl.BlockSpec((None, tm, k), xmap),
            pl.BlockSpec((None, k, nn), lambda j, i: (j, 0, 0)),
        ],
        out_specs=pl.BlockSpec((None, tm, nn), lambda j, i: (j, i, 0)),
        out_shape=jax.ShapeDtypeStruct((nj, n, nn), out_dtype),
        compiler_params=_cparams("parallel", "parallel"),
        name="bmm",
    )(x3, w3)


def _rwmix_kernel(x_ref, xp_ref, xn_ref, g_ref, mod_ref, mix_ref, o_ref, *, tpb, nct):
    i = pl.program_id(0)
    tm = x_ref.shape[0]
    g = g_ref[...]
    shift = mod_ref[0:1, :]
    scale = mod_ref[1:2, :]
    h = _modulate(x_ref[...], g, shift, scale)
    hp = _modulate(xp_ref[...], g, shift, scale)[SUBLANES - 1:SUBLANES, :]
    hn = _modulate(xn_ref[...], g, shift, scale)[0:1, :]
    it = i % tpb
    first = jnp.logical_or(it == 0, it == nct)
    last = jnp.logical_or(it == nct - 1, it == tpb - 1)
    hp = jnp.where(first, 0.0, hp)
    hn = jnp.where(last, 0.0, hn)
    rows = lax.broadcasted_iota(jnp.int32, h.shape, 0)
    prev = jnp.where(rows == 0, hp, pltpu.roll(h, 1, axis=0))
    nxt = jnp.where(rows == tm - 1, hn, pltpu.roll(h, tm - 1, axis=0))
    xx = 0.5 * (prev + nxt) - h
    for m in range(6):
        o_ref[m] = (h + xx * mix_ref[m:m + 1, :]).astype(BF16)


def _rwkv_mix(geo, x, g, mods, mix):
    n, d = x.shape
    tm = geo.tm
    r8 = tm // SUBLANES
    nblk8 = n // SUBLANES
    return pl.pallas_call(
        functools.partial(_rwmix_kernel, tpb=geo.tpb, nct=geo.nct),
        grid=(geo.ntiles,),
        in_specs=[
            pl.BlockSpec((tm, d), lambda i: (i, 0)),
            pl.BlockSpec((SUBLANES, d), lambda i: (jnp.maximum(i * r8 - 1, 0), 0)),
            pl.BlockSpec((SUBLANES, d), lambda i: (jnp.minimum((i + 1) * r8, nblk8 - 1), 0)),
            pl.BlockSpec((1, d), lambda i: (0, 0)),
            geo.mod_spec(),
            pl.BlockSpec((6, d), lambda i: (0, 0)),
        ],
        out_specs=pl.BlockSpec((6, tm, d), lambda i: (0, i, 0)),
        out_shape=jax.ShapeDtypeStruct((6, n, d), BF16),
        compiler_params=_cparams("parallel"),
        name="rwkv_mix",
    )(x, x, x, g.reshape(1, d), mods, mix)


def _head_sum(x, bd):
    hi, lo = _split2(x)
    return _dot(hi, bd) + _dot(lo, bd)


def _rwfeat_kernel(k_ref, v_ref, tl_ref, w2_ref, a2_ref, g2_ref, v2_ref, pv_ref, bd_ref,
                   *rest, has_vlora):
    if has_vlora:
        vf_ref, lw_ref, kd_ref, as_ref, kk_ref, g_ref, vo_ref = rest
    else:
        lw_ref, kd_ref, as_ref, kk_ref, g_ref = rest
    k = k_ref[...]
    tw = tl_ref[0]
    ta = tl_ref[1]
    tg = tl_ref[2]
    w0 = pv_ref[0:2, :]
    a0 = pv_ref[2:4, :]
    k_k = pv_ref[5:6, :]
    k_a = pv_ref[6:7, :]
    for z in range(2):
        sl = slice(LORA_PAD * z, LORA_PAD * (z + 1))
        lora_w = _dot(tw[:, sl], w2_ref[z])
        w = -_softplus(-(w0[z:z + 1, :] + lora_w)) - 0.5
        lw_ref[z] = -jnp.exp(w)
        a_sig = _sigmoid(a0[z:z + 1, :] + _dot(ta[:, sl], a2_ref[z]))
        as_ref[z] = a_sig.astype(BF16)
        kd_ref[z] = k * (1.0 + (a_sig - 1.0) * k_a)
    g_ref[...] = _dot(tg, g2_ref[...]).astype(BF16)
    kkr = k * k_k
    ss = _head_sum(kkr * kkr, bd_ref[...])
    kk_ref[...] = (kkr * lax.rsqrt(ss + 1e-12)).astype(BF16)
    if has_vlora:
        v = v_ref[...]
        tv = tl_ref[3]
        gate = _sigmoid(pv_ref[4:5, :] + _dot(tv[:, 0:LORA_PAD], v2_ref[...]))
        vo_ref[...] = v + (vf_ref[...] - v) * gate


def _rwkv_feat(geo, rkv, tl, w2p, a2p, g2, v2p, pvec, vfirst):
    _, n, d = rkv.shape
    tm, tc = geo.tm, 512
    has_vlora = vfirst is not None
    lt = tl.shape[2]
    ii = lax.broadcasted_iota(jnp.int32, (tc, tc), 0) // RW_HEAD
    jj = lax.broadcasted_iota(jnp.int32, (tc, tc), 1) // RW_HEAD
    bd = (ii == jj).astype(BF16)
    row = lambda i, j: (i, j)
    in_specs = [
        pl.BlockSpec((None, tm, tc), lambda i, j: (1, i, j)),
        pl.BlockSpec((None, tm, tc), lambda i, j: (2, i, j)),
        pl.BlockSpec((4, tm, lt), lambda i, j: (0, i, 0)),
        pl.BlockSpec((2, LORA_PAD, tc), lambda i, j: (0, 0, j)),
        pl.BlockSpec((2, LORA_PAD, tc), lambda i, j: (0, 0, j)),
        pl.BlockSpec((lt, tc), lambda i, j: (0, j)),
        pl.BlockSpec((LORA_PAD, tc), lambda i, j: (0, j)),
        pl.BlockSpec((SUBLANES, tc), lambda i, j: (0, j)),
        pl.BlockSpec((tc, tc), lambda i, j: (0, 0)),
    ]
    args = [rkv, rkv, tl, w2p, a2p, g2, v2p, pvec, bd]
    dir_spec = pl.BlockSpec((2, tm, tc), lambda i, j: (0, i, j))
    out_specs = [dir_spec, dir_spec, dir_spec, pl.BlockSpec((tm, tc), row), pl.BlockSpec((tm, tc), row)]
    out_shape = [jax.ShapeDtypeStruct((2, n, d), dt) for dt in (F32, F32, BF16)]
    out_shape += [jax.ShapeDtypeStruct((n, d), BF16)] * 2
    if has_vlora:
        in_specs.append(pl.BlockSpec((tm, tc), row))
        args.append(vfirst)
        out_specs.append(pl.BlockSpec((tm, tc), row))
        out_shape.append(jax.ShapeDtypeStruct((n, d), F32))
    res = pl.pallas_call(
        functools.partial(_rwfeat_kernel, has_vlora=has_vlora),
        grid=(geo.ntiles, d // tc),
        in_specs=in_specs,
        out_specs=out_specs,
        out_shape=out_shape,
        compiler_params=_cparams("parallel", "parallel"),
        name="rwkv_feat",
    )(*args)
    if has_vlora:
        lw, kd, asg, kk, g, v = res
    else:
        lw, kd, asg, kk, g = res
        v = rkv[2]
    return lw, kd, asg, kk, g, v


def _wkv_chunk_kernel(r_ref, v_ref, kk_ref, lw_ref, kd_ref, as_ref,
                      m_ref, ga_ref, rq_ref, o0_ref, pc_ref, *, npairs):
    c = WKV_CHUNK
    c2 = 2 * c
    sgn = 1 - 2 * pl.program_id(0)
    ri = lax.broadcasted_iota(jnp.int32, (c2, c2), 0)
    ci = lax.broadcasted_iota(jnp.int32, (c2, c2), 1)
    same = (ri >= c) == (ci >= c)
    tt = jnp.where(ri >= c, ri - c, ri)
    ss = jnp.where(ci >= c, ci - c, ci)
    earlier = (ss - tt) * sgn < 0
    strict = jnp.logical_and(same, earlier)
    incl = jnp.logical_and(same, jnp.logical_or(earlier, ss == tt))
    eye = (ri == ci).astype(F32)
    r64 = lax.broadcasted_iota(jnp.int32, (c, c), 0)
    c64 = lax.broadcasted_iota(jnp.int32, (c, c), 1)
    ltri = jnp.where((c64 - r64) * sgn <= 0, 1.0, 0.0).astype(BF16)
    head0 = lax.broadcasted_iota(jnp.int32, (c, LANES), 1) < RW_HEAD
    pairs = range(npairs)

    def stack(x):
        return jnp.concatenate([jnp.where(head0, x, 0.0), jnp.where(head0, 0.0, x)], axis=0)

    def dup(x):
        return jnp.concatenate([x, x], axis=0)

    def bf(x):
        return x.astype(BF16)

    lhs, rhs, a2, bp2, kp2, vst, r2 = [], [], [], [], [], [], []
    lw_all = lw_ref[...]
    cum_all = _dot_exact_lhs(ltri, lw_all)
    for p in pairs:
        sl = slice(LANES * p, LANES * (p + 1))
        lw = lw_all[:, sl]
        cum = cum_all[:, sl]
        tot = jnp.sum(lw, axis=0, keepdims=True)
        p_inv = jnp.exp(-cum)
        p_end = jnp.exp(tot - cum)
        kk = kk_ref[:, sl]
        b = kk * as_ref[:, sl]
        kd = kd_ref[:, sl]
        a2p = stack(-kk * jnp.exp(cum - lw))
        r2p = stack(r_ref[:, sl] * jnp.exp(cum))
        a2.append(bf(a2p))
        r2.append(r2p)
        lhs.append(jnp.concatenate([a2[p], bf(r2p)], axis=0))
        rhs.append(jnp.concatenate([dup(bf(b * p_inv)), dup(bf(kd * p_inv))], axis=0))
        bp2.append(bf(stack(b * p_end)))
        kp2.append(bf(stack(kd * p_end)))
        vst.append(bf(stack(v_ref[:, sl])))
        pc_ref[:, sl] = jnp.broadcast_to(jnp.exp(tot), (SUBLANES, LANES))
    gram = [_dot_nt(lhs[p], rhs[p]) for p in pairs]
    nab = [jnp.where(strict, gram[p][0:c2, 0:c2], 0.0) for p in pairs]
    nrb = [bf(jnp.where(incl, gram[p][c2:2 * c2, 0:c2], 0.0)) for p in pairs]
    nk = [bf(jnp.concatenate([jnp.where(strict, gram[p][0:c2, c2:2 * c2], 0.0),
                              jnp.where(incl, gram[p][c2:2 * c2, c2:2 * c2], 0.0)], axis=0))
          for p in pairs]
    quads = range(npairs // 2)

    def side(x0, x1):
        return jnp.concatenate([x0, x1], axis=1)

    def diag(x0, x1):
        z0 = jnp.zeros_like(x0)
        return jnp.concatenate([side(x0, z0), side(z0, x1)], axis=0)

    def diag_halves(x):
        return diag(x[:, 0:c2], x[:, c2:2 * c2])

    def unside(xs):
        return [xs[p // 2][:, c2 * (p % 2):c2 * (p % 2 + 1)] for p in pairs]

    xv = unside([_dot(side(nk[2 * q], nk[2 * q + 1]), diag(vst[2 * q], vst[2 * q + 1]))
                 for q in quads])
    tinv = [side(eye + nab[2 * q], eye + nab[2 * q + 1]) for q in quads]
    npow = [bf(side(nab[2 * q], nab[2 * q + 1])) for q in quads]
    for _ in range(int(math.log2(c)) - 1):
        npow = [bf(_dot(npow[q], diag_halves(npow[q]))) for q in quads]
        tinv = [tinv[q] + _dot(bf(tinv[q]), diag_halves(npow[q])) for q in quads]
    tinv = unside(tinv)
    y = [_dot(bf(tinv[p]), jnp.concatenate([a2[p], bf(xv[p][0:c2, :])], axis=1)) for p in pairs]
    yb = [bf(y[p]) for p in pairs]
    z = [_dot(nrb[p], yb[p]) for p in pairs]
    mg = unside([_dot(bf(side(y[2 * q].T, y[2 * q + 1].T)), diag(bp2[2 * q], bp2[2 * q + 1]))
                 for q in quads])
    vk = unside([_dot(bf(side(vst[2 * q].astype(F32).T, vst[2 * q + 1].astype(F32).T)),
                      diag(kp2[2 * q], kp2[2 * q + 1])) for q in quads])
    for p in pairs:
        sl = slice(LANES * p, LANES * (p + 1))
        m_ref[:, sl] = bf(mg[p][0:c2, :])
        ga_ref[:, sl] = mg[p][c2:2 * c2, :] + vk[p]
        rq_ref[:, sl] = bf(r2[p] + z[p][:, 0:c2])
        o0 = z[p][:, c2:2 * c2] + xv[p][c2:2 * c2, :]
        o0_ref[:, sl] = o0[0:c, :] + o0[c:c2, :]


def _wkv_scan_kernel(m_ref, ga_ref, rq_ref, o0_ref, pc_ref, o_ref, g_scr, *, npairs):
    c = WKV_CHUNK

    @pl.when(pl.program_id(2) == 0)
    def _():
        g_scr[...] = jnp.zeros_like(g_scr)

    for p in range(npairs):
        sl = slice(LANES * p, LANES * (p + 1))
        g = g_scr[p]
        g_hi, g_lo = _split2(g)
        rq = rq_ref[:, sl]
        o_st = _dot_nt(rq, g_hi) + _dot_nt(rq, g_lo)
        o_ref[:, sl] = o_st[0:c, :] + o_st[c:2 * c, :] + o0_ref[:, sl]
        m = m_ref[:, sl]
        g_scr[p] = g * pc_ref[0:1, sl] + (_dot(g_hi, m) + _dot(g_lo, m)) + ga_ref[:, sl]


WKV_CHUNK_LANES = 2048


def _wkv_bidir(r, v, kk, lw, kd, asg, batch, ctx_len):
    n, d = r.shape
    c = WKV_CHUNK
    s = n // batch
    ncs = s // c
    ncc = ctx_len // c
    nch = n // c
    lanes = min(WKV_CHUNK_LANES, d)
    ngrp = d // lanes
    shared = pl.BlockSpec((c, lanes), lambda z, i, j: (i, j))
    perdir = pl.BlockSpec((None, c, lanes), lambda z, i, j: (z, i, j))
    big = pl.BlockSpec((None, 2 * c, lanes), lambda z, i, j: (z, i, j))
    m_, ga_, rq_, o0_, pc_ = pl.pallas_call(
        functools.partial(_wkv_chunk_kernel, npairs=lanes // LANES),
        grid=(2, nch, ngrp),
        in_specs=[shared, shared, shared, perdir, perdir, perdir],
        out_specs=[big, big, big, perdir,
                   pl.BlockSpec((None, SUBLANES, lanes), lambda z, i, j: (z, i, j))],
        out_shape=[jax.ShapeDtypeStruct((2, nch * 2 * c, d), dt) for dt in (BF16, F32, BF16)]
        + [jax.ShapeDtypeStruct((2, n, d), F32),
           jax.ShapeDtypeStruct((2, nch * SUBLANES, d), F32)],
        compiler_params=_cparams("parallel", "parallel", "parallel"),
        name="wkv_chunk",
    )(r, v, kk, lw, kd, asg)

    def cmap(z, b, cc):
        back = jnp.where(cc < ncc, ncc - 1 - cc, ncs + ncc - 1 - cc)
        return (z, b * ncs + jnp.where(z == 0, cc, back), 0)

    out = pl.pallas_call(
        functools.partial(_wkv_scan_kernel, npairs=d // LANES),
        grid=(2, batch, ncs),
        in_specs=[
            pl.BlockSpec((None, 2 * c, d), cmap),
            pl.BlockSpec((None, 2 * c, d), cmap),
            pl.BlockSpec((None, 2 * c, d), cmap),
            pl.BlockSpec((None, c, d), cmap),
            pl.BlockSpec((None, SUBLANES, d), cmap),
        ],
        out_specs=pl.BlockSpec((None, c, d), cmap),
        out_shape=jax.ShapeDtypeStruct((2, n, d), F32),
        scratch_shapes=[pltpu.VMEM((d // LANES, 2 * c, 2 * c), F32)],
        compiler_params=_cparams("parallel", "parallel", "arbitrary"),
        name="wkv_scan",
    )(m_, ga_, rq_, o0_, pc_)
    return out


def _rwread_kernel(o_ref, r_ref, kd_ref, v_ref, g_ref, pv_ref, bd_ref, y_ref):
    bd = bd_ref[...]
    wkv = o_ref[0] + o_ref[1]
    inv = 1.0 / RW_HEAD
    mu = _head_sum(wkv, bd) * inv
    dev = wkv - mu
    var = _head_sum(dev * dev, bd) * inv
    y = dev * lax.rsqrt(var + GN_EPS) * pv_ref[1:2, :] + pv_ref[2:3, :]
    rk = r_ref[...] * (kd_ref[0] + kd_ref[1]) * pv_ref[0:1, :]
    y = y + _head_sum(rk, bd) * v_ref[...]
    y_ref[...] = (y * g_ref[...]).astype(BF16)


def _rwkv_readout(geo, wkv, r, kd, v, g, pvec):
    n, d = r.shape
    tm, tc = geo.tm, 512
    ii = lax.broadcasted_iota(jnp.int32, (tc, tc), 0) // RW_HEAD
    jj = lax.broadcasted_iota(jnp.int32, (tc, tc), 1) // RW_HEAD
    bd = (ii == jj).astype(BF16)
    row = pl.BlockSpec((tm, tc), lambda i, j: (i, j))
    dirs = pl.BlockSpec((2, tm, tc), lambda i, j: (0, i, j))
    return pl.pallas_call(
        _rwread_kernel,
        grid=(geo.ntiles, d // tc),
        in_specs=[dirs, row, dirs, row, row,
                  pl.BlockSpec((SUBLANES, tc), lambda i, j: (0, j)),
                  pl.BlockSpec((tc, tc), lambda i, j: (0, 0))],
        out_specs=row,
        out_shape=jax.ShapeDtypeStruct((n, d), BF16),
        compiler_params=_cparams("parallel", "parallel"),
        name="rwkv_readout",
    )(wkv, r, kd, v, g, pvec, bd)


def _mlaqkv_kernel(z_ref, qn_ref, kvn_ref, wuq_ref, wukv_ref, gq_ref, gk_ref, cos_ref, sin_ref,
                   q_ref, k_ref, v_ref, *, q_lora, kv_lora):
    z = z_ref[...]
    cq = z[:, 0:q_lora]
    ckv = z[:, q_lora:q_lora + kv_lora]
    krot = z[:, q_lora + kv_lora:q_lora + kv_lora + LANES]

    def rms(x, g):
        ms = jnp.mean(x * x, axis=-1, keepdims=True)
        return x * lax.rsqrt(ms + EPS) * g

    qf = _dot(rms(cq, qn_ref[...]).astype(BF16), wuq_ref[...])
    kvf = _dot(rms(ckv, kvn_ref[...]).astype(BF16), wukv_ref[...])
    cos = cos_ref[...]
    sin = sin_ref[...]
    half = QK_ROPE // 2
    lane = lax.broadcasted_iota(jnp.int32, cos.shape, 1)

    def rope(x):
        up = pltpu.roll(x, LANES - half, axis=1)
        dn = pltpu.roll(x, half, axis=1)
        return x * cos + jnp.where(lane < half, up, dn) * sin

    inv_w = 1.0 / (QK_NOPE + QK_ROPE)
    gq = gq_ref[...]
    gk = gk_ref[...]
    kr_ss = jnp.sum(krot * krot, axis=-1, keepdims=True)
    for h in range(MLA_HEADS):
        o = QK_PAD * h
        qh = qf[:, o:o + QK_PAD]
        rs = lax.rsqrt(jnp.sum(qh * qh, axis=-1, keepdims=True) * inv_w + EPS)
        qn = qh * rs * gq * (ATTN_SCALE * math.log2(math.e))
        q_ref[:, o:o + QK_NOPE] = qn[:, 0:QK_NOPE].astype(BF16)
        q_ref[:, o + QK_NOPE:o + QK_PAD] = rope(qn[:, QK_NOPE:QK_PAD]).astype(BF16)
        kn = kvf[:, o:o + QK_NOPE]
        rsk = lax.rsqrt((jnp.sum(kn * kn, axis=-1, keepdims=True) + kr_ss) * inv_w + EPS)
        k_ref[:, o:o + QK_NOPE] = (kn * rsk * gk[:, 0:QK_NOPE]).astype(BF16)
        k_ref[:, o + QK_NOPE:o + QK_PAD] = rope(krot * rsk * gk[:, QK_NOPE:QK_PAD]).astype(BF16)
        v_ref[:, V_HEAD * h:V_HEAD * (h + 1)] = kvf[:, o + QK_NOPE:o + QK_PAD].astype(BF16)


def _mla_qkv(geo, z, qn, kvn, wuq_p, wukv, gq_p, gk_p, cos_t, sin_t):
    n, zw = z.shape
    tm = geo.tm
    q_lora, kv_lora = qn.shape[0], kvn.shape[0]
    hq = MLA_HEADS * QK_PAD
    tpb = geo.tpb
    full = lambda i: (0, 0)
    rowmap = lambda i: (i, 0)
    return pl.pallas_call(
        functools.partial(_mlaqkv_kernel, q_lora=q_lora, kv_lora=kv_lora),
        grid=(geo.ntiles,),
        in_specs=[
            pl.BlockSpec((tm, zw), rowmap),
            pl.BlockSpec((1, q_lora), full),
            pl.BlockSpec((1, kv_lora), full),
            pl.BlockSpec((q_lora, hq), full),
            pl.BlockSpec((kv_lora, hq), full),
            pl.BlockSpec((1, QK_PAD), full),
            pl.BlockSpec((1, QK_PAD), full),
            pl.BlockSpec((tm, LANES), lambda i: (i % tpb, 0)),
            pl.BlockSpec((tm, LANES), lambda i: (i % tpb, 0)),
        ],
        out_specs=[pl.BlockSpec((tm, hq), rowmap), pl.BlockSpec((tm, hq), rowmap),
                   pl.BlockSpec((tm, MLA_HEADS * V_HEAD), rowmap)],
        out_shape=[jax.ShapeDtypeStruct((n, hq), BF16), jax.ShapeDtypeStruct((n, hq), BF16),
                   jax.ShapeDtypeStruct((n, MLA_HEADS * V_HEAD), BF16)],
        compiler_params=_cparams("parallel"),
        name="mla_qkv",
    )(z, qn.reshape(1, -1), kvn.reshape(1, -1), wuq_p, wukv, gq_p, gk_p, cos_t, sin_t)


ATTN_SUBTILES = 2


def _attn_kernel(q_ref, k_ref, vt_ref, o_ref):
    k = k_ref[...]
    vt = vt_ref[...]
    cols = q_ref.shape[0] // ATTN_SUBTILES
    subs = range(ATTN_SUBTILES)
    s = [_dot_nt(k, q_ref[cols * i:cols * (i + 1), :]) for i in subs]
    m = [jnp.max(s[i], axis=0, keepdims=True) for i in subs]
    p = [jnp.exp2(s[i] - m[i]) for i in subs]
    l = [jnp.sum(p[i], axis=0, keepdims=True) for i in subs]
    for i in subs:
        ot = _dot(vt, p[i].astype(BF16)) / l[i]
        o_ref[cols * i:cols * (i + 1), :] = ot.T.astype(BF16)


def _attn_call(q3, k3, vt4, tq, nkeys):
    b, sq, _ = q3.shape
    return pl.pallas_call(
        _attn_kernel,
        grid=(b, MLA_HEADS, sq // tq),
        in_specs=[
            pl.BlockSpec((None, tq, QK_PAD), lambda bb, h, i: (bb, i, h)),
            pl.BlockSpec((None, nkeys, QK_PAD), lambda bb, h, i: (bb, 0, h)),
            pl.BlockSpec((None, None, V_HEAD, nkeys), lambda bb, h, i: (bb, h, 0, 0)),
        ],
        out_specs=pl.BlockSpec((None, tq, V_HEAD), lambda bb, h, i: (bb, i, h)),
        out_shape=jax.ShapeDtypeStruct((b, sq, MLA_HEADS * V_HEAD), BF16),
        compiler_params=_cparams("parallel", "parallel", "parallel"),
        name="mla_attention",
    )(q3, k3, vt4)


def _mla_attention(geo, q, k, v):
    b, s, l = geo.b, geo.s, geo.l
    q3 = q.reshape(b, s, MLA_HEADS * QK_PAD)
    k3 = k.reshape(b, s, MLA_HEADS * QK_PAD)
    vt4 = v.reshape(b, s, MLA_HEADS, V_HEAD).transpose(0, 2, 3, 1)
    o_ctx = _attn_call(q3[:, :l], k3, vt4, geo.tm, l)
    tq_lat = 2 * geo.tm if geo.t % (2 * geo.tm) == 0 else geo.tm
    o_lat = _attn_call(q3[:, l:], k3, vt4, tq_lat, s)
    return jnp.concatenate([o_ctx, o_lat], axis=1).reshape(b * s, MLA_HEADS * V_HEAD)


def _topk_rows(s, k, payload=None):
    rows = lax.broadcasted_iota(jnp.int32, s.shape, 0).astype(F32)
    big = float(s.shape[0])
    vals, idxs = [], []
    for _ in range(k):
        m = jnp.max(s, axis=0, keepdims=True)
        idx = jnp.min(jnp.where(s == m, rows, big), axis=0, keepdims=True)
        hit = rows == idx
        vals.append(m)
        if payload is None:
            idxs.append(idx)
        else:
            idxs.append(jnp.sum(jnp.where(hit, payload, 0.0), axis=0, keepdims=True))
        s = jnp.where(hit, -jnp.inf, s)
    return jnp.concatenate(vals, axis=0), jnp.concatenate(idxs, axis=0)


def _peersel_kernel(q_ref, qn_ref, keys_ref, e_ref, g_ref):
    q = q_ref[...]
    ms = jnp.mean(q * q, axis=-1, keepdims=True)
    qn = q * lax.rsqrt(ms + EPS) * qn_ref[...]
    half = D_KEY // 2
    s1 = _dot3_nt(keys_ref[0], qn[:, 0:half])
    s2 = _dot3_nt(keys_ref[1], qn[:, half:D_KEY])
    t1, i1 = _topk_rows(s1, PEER_TOPK)
    t2, i2 = _topk_rows(s2, PEER_TOPK)
    k = PEER_TOPK
    sub = lax.broadcasted_iota(jnp.int32, (SUBLANES, t1.shape[1]), 0)
    cand = [t1[0:1, :] + t2]
    cidx = [i1[0:1, :] * float(N_KEYS) + i2]
    for p in range(1, k // 2):
        live = sub < k // (p + 1)
        cand.append(jnp.where(live, t1[p:p + 1, :] + t2[0:SUBLANES, :], -jnp.inf))
        cidx.append(i1[p:p + 1, :] * float(N_KEYS) + i2[0:SUBLANES, :])
    cand.append(t1[k // 2:k, :] + t2[0:1, :])
    cidx.append(i1[k // 2:k, :] * float(N_KEYS) + i2[0:1, :])
    best, eidx = _topk_rows(jnp.concatenate(cand, axis=0), k, payload=jnp.concatenate(cidx, axis=0))
    ex = jnp.exp(best - jnp.max(best, axis=0, keepdims=True))
    g_ref[...] = ex / jnp.sum(ex, axis=0, keepdims=True)
    e_ref[...] = eidx.astype(jnp.int32)


def _peer_select(qp, q_norm, keys, tm, row0, n):
    assert row0 % tm == 0 and n % tm == 0
    tile0 = row0 // tm
    return pl.pallas_call(
        _peersel_kernel,
        grid=(n // tm, PEER_HEADS),
        in_specs=[
            pl.BlockSpec((tm, D_KEY), lambda i, h: (tile0 + i, h)),
            pl.BlockSpec((1, D_KEY), lambda i, h: (0, 0)),
            pl.BlockSpec((2, N_KEYS, D_KEY // 2), lambda i, h: (0, 0, 0)),
        ],
        out_specs=[pl.BlockSpec((PEER_TOPK, tm), lambda i, h: (h, i)),
                   pl.BlockSpec((PEER_TOPK, tm), lambda i, h: (h, i))],
        out_shape=[jax.ShapeDtypeStruct((PEER_HEADS * PEER_TOPK, n), jnp.int32),
                   jax.ShapeDtypeStruct((PEER_HEADS * PEER_TOPK, n), F32)],
        compiler_params=_cparams("parallel", "parallel"),
        name="peer_select",
    )(qp, q_norm.reshape(1, D_KEY), keys)


GATHER_SLOTS = 4
SLAB_PAD = 1


def _pack_expert_table(u, v):
    ne, d = u.shape
    ub = lax.bitcast_convert_type(u.astype(BF16), jnp.uint16).astype(jnp.uint32)
    vb = lax.bitcast_convert_type(v.astype(BF16), jnp.uint16).astype(jnp.uint32)
    return ((vb << 16) | ub).reshape(ne, d // LANES, LANES)


def _peergather_kernel(idx_ref, idxn_ref, gate_ref, h_ref, x_ref, mod_ref, tab_ref, o_ref,
                       *scratch, tb):
    nsel = PEER_HEADS * PEER_TOPK
    nrow = h_ref.shape[1] // LANES
    pitch = nrow + SLAB_PAD
    ns = GATHER_SLOTS
    bufs, sem = scratch[:ns], scratch[ns]
    lane_t = lax.broadcasted_iota(jnp.int32, (nsel, tb), 1)
    g2 = mod_ref[5:6, :]
    step = pl.program_id(0)
    nsteps = pl.num_programs(0)

    def row_copy(ids_ref, t, j, slot):
        return pltpu.make_async_copy(
            tab_ref.at[ids_ref[t, j]],
            bufs[slot].at[pl.ds(j * pitch, nrow), :],
            sem.at[slot])

    def issue(ids_ref, t, slot):
        for j in range(nsel):
            row_copy(ids_ref, t, j, slot).start(priority=j % 2)

    def wait(t, slot):
        for j in range(nsel):
            row_copy(idx_ref, t, j, slot).wait()

    def packed(slot, s):
        return bufs[slot][pl.ds(s, nsel, stride=pitch), :]

    def compute(t, slot):
        hrow = h_ref[pl.ds(t, 1), :]
        acc = jnp.zeros((nsel, LANES), F32)
        for s in range(nrow):
            u = lax.bitcast_convert_type(packed(slot, s) << 16, F32)
            acc = acc + u * hrow[:, LANES * s:LANES * (s + 1)]
        dots = jnp.sum(acc, axis=-1, keepdims=True)
        gcol = jnp.sum(jnp.where(lane_t == t, gate_ref[...], 0.0), axis=-1, keepdims=True)
        coef = gcol * _gelu(dots)
        outs = []
        for s in range(nrow):
            vv = lax.bitcast_convert_type(packed(slot, s) & jnp.uint32(0xFFFF0000), F32)
            outs.append(jnp.sum(coef * vv, axis=0, keepdims=True))
        orow = jnp.concatenate(outs, axis=1)
        o_ref[pl.ds(t, 1), :] = x_ref[pl.ds(t, 1), :] + g2 * orow

    @pl.when(step == 0)
    def _():
        for s in range(ns - 1):
            issue(idx_ref, s, s)

    ngroups = tb // ns

    def body(g, carry):
        for s in range(ns):
            t = g * ns + s
            wait(t, s)
            issue(idx_ref, t + ns - 1, (s + ns - 1) % ns)
            compute(t, s)
        return carry

    lax.fori_loop(0, ngroups - 1, body, 0)
    for s in range(ns):
        t = (ngroups - 1) * ns + s
        wait(t, s)
        if s == 0:
            issue(idx_ref, tb - 1, ns - 1)
        else:
            @pl.when(step < nsteps - 1)
            def _():
                issue(idxn_ref, s - 1, s - 1)
        compute(t, s)


def _peer_gather(geo_g, eidx, gate_t, h2, x, mods, table, n):
    d = x.shape[1]
    tb = geo_g.tm
    nsel = PEER_HEADS * PEER_TOPK
    pitch = d // LANES + SLAB_PAD
    assert n % tb == 0
    nsteps = n // tb
    return pl.pallas_call(
        functools.partial(_peergather_kernel, tb=tb),
        grid=(nsteps,),
        in_specs=[
            pl.BlockSpec((tb, nsel), lambda i: (i, 0), memory_space=pltpu.SMEM),
            pl.BlockSpec((tb, nsel), lambda i: (jnp.minimum(i + 1, nsteps - 1), 0),
                         memory_space=pltpu.SMEM),
            pl.BlockSpec((nsel, tb), lambda i: (0, i)),
            pl.BlockSpec((tb, d), lambda i: (i, 0)),
            pl.BlockSpec((tb, d), lambda i: (i, 0)),
            geo_g.mod_spec(),
            pl.BlockSpec(memory_space=pl.ANY),
        ],
        out_specs=pl.BlockSpec((tb, d), lambda i: (i, 0)),
        out_shape=jax.ShapeDtypeStruct((n, d), F32),
        scratch_shapes=[pltpu.VMEM((nsel * pitch, LANES), jnp.uint32)] * GATHER_SLOTS
        + [pltpu.SemaphoreType.DMA((GATHER_SLOTS,))],
        compiler_params=_cparams("arbitrary"),
        name="peer_gather",
    )(eidx, eidx, gate_t, h2, x, mods, table)


SC_CORES = 2
SC_SUBCORES = 16
SC_LANES = 16
SC_CHUNK = 16
SC_SHARE = 0.3235

_ERF_ALPHA = (0.00022905065861350646, 0.0034082910107109506, 0.050955695062380861,
              0.18520832239976145, 1.128379143519084)
_ERF_BETA = (-1.1791602954361697e-7, 0.000023547966471313185, 0.0010179625278914885,
             0.014070470171167667, 0.11098505178285362, 0.49746925110067538, 1.0)
_ERF_CLAMP = 3.832506856900711


def _erf_rational(x):
    x = jnp.minimum(jnp.maximum(x, -_ERF_CLAMP), _ERF_CLAMP)
    x2 = x * x
    p = jnp.full_like(x, _ERF_ALPHA[0])
    for c in _ERF_ALPHA[1:]:
        p = p * x2 + c
    q = jnp.full_like(x, _ERF_BETA[0])
    for c in _ERF_BETA[1:]:
        q = q * x2 + c
    return x * p / q


def _peer_sc(eidx, gate, h, table, row0):
    n, nsel = eidx.shape
    d = h.shape[1]
    nw = SC_CORES * SC_SUBCORES
    assert n % (2 * nw) == 0 and nsel % SC_CHUNK == 0 and d % SC_LANES == 0
    tpw = n // nw
    nchunk = nsel // SC_CHUNK
    nvec = d // SC_LANES
    mesh = plsc.VectorSubcoreMesh(core_axis_name="c", subcore_axis_name="s",
                                  num_cores=SC_CORES, num_subcores=SC_SUBCORES)

    @functools.partial(
        pl.kernel, mesh=mesh,
        out_type=jax.ShapeDtypeStruct((n, d), F32),
        scratch_types=[
            pltpu.VMEM((nsel,), jnp.int32), pltpu.VMEM((nsel,), jnp.int32),
            pltpu.VMEM((nsel,), F32), pltpu.VMEM((nsel,), F32),
            pltpu.VMEM((d,), F32), pltpu.VMEM((d,), F32),
            pltpu.VMEM((d,), F32),
            pltpu.VMEM((SC_CHUNK, d), jnp.uint32),
            pltpu.VMEM((SC_CHUNK, d), jnp.uint32),
            pltpu.SemaphoreType.DMA, pltpu.SemaphoreType.DMA,
            pltpu.SemaphoreType.DMA, pltpu.SemaphoreType.DMA,
        ],
        compiler_params=pltpu.CompilerParams(needs_layout_passes=False),
        name="peer_sc",
    )
    def sc_kernel(eidx_hbm, gate_hbm, h_hbm, tab_hbm, y_hbm, idx_a, idx_b, gate_a, gate_b, h_a, h_b,
                  o_v, rows0, rows1, sem0, sem1, msem_a, msem_b):
        wid = lax.axis_index("s") * SC_CORES + lax.axis_index("c")
        lanes = lax.iota(jnp.int32, SC_LANES)
        zero = jnp.zeros((SC_LANES,), F32)
        bufs = ((rows0, sem0), (rows1, sem1))
        meta = ((idx_a, gate_a, h_a, msem_a), (idx_b, gate_b, h_b, msem_b))

        def meta_copies(tok, s):
            idx_v, gate_v, h_v, msem = meta[s]
            return (pltpu.make_async_copy(eidx_hbm.at[tok], idx_v, msem),
                    pltpu.make_async_copy(gate_hbm.at[tok], gate_v, msem),
                    pltpu.make_async_copy(h_hbm.at[row0 + tok], h_v, msem))

        def gather(s, c, slot):
            rows, sem = bufs[slot]
            return pltpu.make_async_copy(
                tab_hbm.at[meta[s][0].at[pl.ds(c * SC_CHUNK, SC_CHUNK)]], rows, sem)

        def chunk_compute(s, c, rows):
            _, gate_v, h_v, _ = meta[s]

            def dot_body(i, accs):
                hv = h_v[pl.ds(i * SC_LANES, SC_LANES)]
                out = []
                for e in range(SC_CHUNK):
                    w = rows[e, pl.ds(i * SC_LANES, SC_LANES)]
                    out.append(accs[e] + lax.bitcast_convert_type(w << 16, F32) * hv)
                return tuple(out)

            accs = lax.fori_loop(0, nvec, dot_body, (zero,) * SC_CHUNK)
            dots = zero
            for e in range(SC_CHUNK):
                dots = jnp.where(lanes == e, jnp.sum(accs[e]), dots)
            act = 0.5 * dots * (1.0 + _erf_rational(dots * (2.0 ** -0.5)))
            coef = gate_v[pl.ds(c * SC_CHUNK, SC_CHUNK)] * act
            splat = [jnp.full((SC_LANES,), jnp.sum(jnp.where(lanes == e, coef, 0.0)), F32)
                     for e in range(SC_CHUNK)]

            def acc_body(i, carry):
                o = o_v[pl.ds(i * SC_LANES, SC_LANES)]
                for e in range(SC_CHUNK):
                    w = rows[e, pl.ds(i * SC_LANES, SC_LANES)]
                    o = o + splat[e] * lax.bitcast_convert_type(w & jnp.uint32(0xFFFF0000), F32)
                o_v[pl.ds(i * SC_LANES, SC_LANES)] = o
                return carry

            lax.fori_loop(0, nvec, acc_body, 0)

        def zero_body(i, c2):
            o_v[pl.ds(i * SC_LANES, SC_LANES)] = zero
            return c2

        def token(tok, s, has_next):
            def when_next(fn):
                if isinstance(has_next, bool):
                    if has_next:
                        fn()
                else:
                    pl.when(has_next)(fn)

            def load_next():
                for cp in meta_copies(tok + 1, 1 - s):
                    cp.start()

            when_next(load_next)
            lax.fori_loop(0, nvec, zero_body, 0)
            for c in range(nchunk):
                if c + 1 < nchunk:
                    gather(s, c + 1, (c + 1) % 2).start()
                gather(s, c, c % 2).wait()
                chunk_compute(s, c, bufs[c % 2][0])

            def prefetch_next():
                for cp in meta_copies(tok + 1, 1 - s):
                    cp.wait()
                gather(1 - s, 0, 0).start()

            when_next(prefetch_next)
            pltpu.sync_copy(o_v, y_hbm.at[tok])

        base = wid * tpw
        for cp in meta_copies(base, 0):
            cp.start()
        for cp in meta_copies(base, 0):
            cp.wait()
        gather(0, 0, 0).start()

        def pair_body(g, carry):
            token(base + 2 * g, 0, True)
            token(base + 2 * g + 1, 1, g + 1 < tpw // 2)
            return carry

        lax.fori_loop(0, tpw // 2, pair_body, 0)

    return sc_kernel(eidx, gate, h, table)


def _pad_to(x, axis, size):
    pad = [(0, 0)] * x.ndim
    pad[axis] = (0, size - x.shape[axis])
    return jnp.pad(x, pad)


def _rwkv_layer(geo, xs, mods, norm1, mix, w_rkv, w_o, w0, w1, w2, a0, a1, a2, vl, g1, g2,
                k_k, k_a, r_k, ln_w, ln_b, vfirst):
    d = geo.d
    xm = _rwkv_mix(geo, xs, norm1, mods, mix)
    rkv = _bmm(xm, w_rkv.astype(BF16), (0, 2, 3), ("none",) * 3, F32, geo.tm)
    lt = g1.shape[1]
    w1c = jnp.concatenate([_pad_to(w1[0], 1, LORA_PAD), _pad_to(w1[1], 1, LORA_PAD)], axis=1)
    a1c = jnp.concatenate([_pad_to(a1[0], 1, LORA_PAD), _pad_to(a1[1], 1, LORA_PAD)], axis=1)
    if vl is None:
        v1p = jnp.zeros((d, lt), F32)
        v2p = jnp.zeros((LORA_PAD, d), F32)
        v0 = jnp.zeros((d,), F32)
    else:
        v0, v1, v2 = vl
        v1p = _pad_to(v1, 1, lt)
        v2p = _pad_to(v2, 0, LORA_PAD)
    wl1 = jnp.stack([_pad_to(w1c, 1, lt), _pad_to(a1c, 1, lt), g1, v1p]).astype(BF16)
    tl = _bmm(xm, wl1, (1, 4, 5, 3), ("tanh", "none", "sigmoid", "none"), BF16, geo.tm)
    w2p = jnp.stack([_pad_to(w2[0], 0, LORA_PAD), _pad_to(w2[1], 0, LORA_PAD)]).astype(BF16)
    a2p = jnp.stack([_pad_to(a2[0], 0, LORA_PAD), _pad_to(a2[1], 0, LORA_PAD)]).astype(BF16)
    pvec = jnp.stack([w0[0], w0[1], a0[0], a0[1], v0, k_k, k_a, jnp.zeros_like(k_k)])
    lw, kd, asg, kk, g, v = _rwkv_feat(geo, rkv, tl, w2p, a2p, g2.astype(BF16), v2p.astype(BF16),
                                      pvec, vfirst)
    r = rkv[0]
    wkv = _wkv_bidir(r, v, kk, lw, kd, asg, geo.b, geo.l)
    pv2 = _pad_to(jnp.stack([r_k, ln_w, ln_b]), 0, SUBLANES)
    y = _rwkv_readout(geo, wkv, r, kd, v, g, pv2)
    xs = _matmul_res(geo, y, w_o.astype(BF16), xs, mods, 2)
    return xs, v


def _rope_tables(geo):
    t = geo.t
    pos = jnp.arange(t)
    row = (pos // GRID_W).astype(F32)
    col = (pos % GRID_W).astype(F32)
    n_freq = QK_ROPE // 4
    inv_freq = ROPE_THETA ** (-jnp.arange(n_freq, dtype=F32) / n_freq)
    ang = jnp.concatenate([row[:, None] * inv_freq, col[:, None] * inv_freq], axis=-1)
    cos, sin = jnp.cos(ang), jnp.sin(ang)
    pad = LANES - QK_ROPE
    cos_l = jnp.concatenate([cos, cos, jnp.ones((t, pad), F32)], axis=1)
    sin_l = jnp.concatenate([-sin, sin, jnp.zeros((t, pad), F32)], axis=1)
    cos_c = jnp.ones((geo.l, LANES), F32)
    sin_c = jnp.zeros((geo.l, LANES), F32)
    return jnp.concatenate([cos_c, cos_l], axis=0), jnp.concatenate([sin_c, sin_l], axis=0)


def _mla_layer(geo, xs, mods, norm1, rope_t, w_in, q_norm, kv_norm, w_uq, w_ukv, g_q, g_k, w_o):
    q_lora, kv_lora = q_norm.shape[0], kv_norm.shape[0]
    zw = q_lora + kv_lora + LANES
    z = _mod_matmul(geo, xs, norm1, mods, _pad_to(w_in, 1, zw).astype(BF16), 0, False)
    qk = QK_NOPE + QK_ROPE
    wuq_p = _pad_to(w_uq.reshape(q_lora, MLA_HEADS, qk), 2, QK_PAD).reshape(q_lora, -1)
    gq_p = _pad_to(g_q, 0, QK_PAD).reshape(1, QK_PAD)
    gk_p = _pad_to(g_k, 0, QK_PAD).reshape(1, QK_PAD)
    q, k, v = _mla_qkv(geo, z, q_norm, kv_norm, wuq_p.astype(BF16), w_ukv.astype(BF16),
                       gq_p, gk_p, *rope_t)
    o = _mla_attention(geo, q, k, v)
    return _matmul_res(geo, o, w_o.astype(BF16), xs, mods, 2)


def _peer_layer(geo, geo_g, xs, mods, norm2, w_q, q_norm, keys, u, v):
    qp, h2 = _mod_matmul(geo, xs, norm2, mods, w_q.astype(BF16), 1, True)
    table = _pack_expert_table(u, v)
    n, d = xs.shape
    n_sc = geo.tm * round(SC_SHARE * n / geo.tm)
    if n_sc % (2 * SC_CORES * SC_SUBCORES) != 0:
        n_sc = 0
    n_tc = n - n_sc
    if n_sc:
        eidx_sc, gate_sc = _peer_select(qp, q_norm, keys, geo.tm, n_tc, n_sc)
        y_sc = _peer_sc(eidx_sc.T, gate_sc.T, h2, table.reshape(table.shape[0], d), n_tc)
    eidx_t, gate_t = _peer_select(qp, q_norm, keys, geo.tm, 0, n_tc)
    out_tc = _peer_gather(geo_g, eidx_t.T, gate_t, h2, xs, mods, table, n_tc)
    if n_sc == 0:
        return out_tc
    g2, r = [], n_tc
    while r < n:
        bi, pos = divmod(r, geo.s)
        seg = int(pos >= geo.l)
        stop = min(n, bi * geo.s + (geo.s if seg else geo.l))
        g2.append(jnp.broadcast_to(mods[bi, seg, 5], (stop - r, d)))
        r = stop
    out_sc = xs[n_tc:] + jnp.concatenate(g2, axis=0) * y_sc
    return jnp.concatenate([out_tc, out_sc], axis=0)


def kernel(x, c, ctx, c_ctx, w_ada, b_ada, norm1, norm2, rw_mix, rw_wrkv, rw_wo, rw_w0, rw_w1, rw_w2, rw_a0, rw_a1, rw_a2, rw_v0, rw_v1, rw_v2, rw_g1, rw_g2, rw_kk, rw_ka, rw_rk, rw_lnw, rw_lnb, mla_win, mla_qnorm, mla_kvnorm, mla_wuq, mla_wukv, mla_gq, mla_gk, mla_wo, peer_wq, peer_qnorm, peer_keys, peer_u, peer_v):
    b, t, d = x.shape
    l = ctx.shape[1]
    depth = w_ada.shape[0]
    geo = _Geom(b, l, t, d, min(256, l))
    geo_g = _Geom(b, l, t, d, min(128, l))
    cond8 = _pad_to(jnp.concatenate([c, c_ctx[None, :]], axis=0), 0, SUBLANES)
    ada = _adaln(cond8, w_ada, b_ada).reshape(depth, SUBLANES, 6, d)
    mods_all = jnp.stack([jnp.broadcast_to(ada[:, b:b + 1], (depth, b, 6, d)), ada[:, 0:b]], axis=2)
    xs = jnp.concatenate([ctx, x], axis=1).reshape(b * (l + t), d)
    rope_t = _rope_tables(geo)
    vfirst = None
    for i in range(depth):
        j = i // 2
        mods = mods_all[i]
        if i % 2 == 0:
            vl = None if j == 0 else (rw_v0[j - 1], rw_v1[j - 1], rw_v2[j - 1])
            xs, vcur = _rwkv_layer(geo, xs, mods, norm1[i], rw_mix[j], rw_wrkv[j], rw_wo[j],
                                   rw_w0[j], rw_w1[j], rw_w2[j], rw_a0[j], rw_a1[j], rw_a2[j], vl,
                                   rw_g1[j], rw_g2[j], rw_kk[j], rw_ka[j], rw_rk[j], rw_lnw[j],
                                   rw_lnb[j], vfirst)
            if j == 0:
                vfirst = vcur
        else:
            xs = _mla_layer(geo, xs, mods, norm1[i], rope_t, mla_win[j], mla_qnorm[j],
                            mla_kvnorm[j], mla_wuq[j], mla_wukv[j], mla_gq[j], mla_gk[j], mla_wo[j])
        if i == depth - 1:
            xs = xs.reshape(b, l + t, d)[:, l:, :].reshape(b * t, d)
            geo, geo_g = _Geom(b, 0, t, d, geo.tm), _Geom(b, 0, t, d, geo_g.tm)
        xs = _peer_layer(geo, geo_g, xs, mods, norm2[i], peer_wq[i], peer_qnorm[i], peer_keys[i],
                         peer_u[i], peer_v[i])
    return xs.reshape(b, t, d)
```

```python
import functools
import math

import jax
import jax.numpy as jnp
from jax import lax
from jax.experimental import pallas as pl
from jax.experimental.pallas import tpu as pltpu
from jax.experimental.pallas import tpu_sc as plsc

F32 = jnp.float32
BF16 = jnp.bfloat16

EPS = 1e-6
GN_EPS = 64e-5
RW_HEAD = 64
WKV_CHUNK = 64
MLA_HEADS = 16
QK_NOPE = 128
QK_ROPE = 64
V_HEAD = 128
QK_PAD = 256
ROPE_THETA = 10000.0
GRID_W = 64
ATTN_SCALE = (QK_NOPE + QK_ROPE) ** -0.5
PEER_HEADS = 8
N_KEYS = 128
PEER_TOPK = 16
D_KEY = 256
LORA_PAD = 128

LANES = 128
SUBLANES = 8
VMEM_LIMIT = 56 * 1024 * 1024


def _cparams(*sem):
    return pltpu.CompilerParams(dimension_semantics=sem, vmem_limit_bytes=VMEM_LIMIT)


def _dot(a, b):
    return jnp.dot(a, b, preferred_element_type=F32)


def _dot_nt(a, b):
    return lax.dot_general(a, b, (((1,), (1,)), ((), ())), preferred_element_type=F32)


def _split2(x):
    hi = x.astype(BF16)
    lo = (x - hi.astype(F32)).astype(BF16)
    return hi, lo


def _split3(x):
    hi = x.astype(BF16)
    r1 = x - hi.astype(F32)
    mid = r1.astype(BF16)
    lo = (r1 - mid.astype(F32)).astype(BF16)
    return hi, mid, lo


def _dot3(a, b):
    ah, al = _split2(a)
    bh, bl = _split2(b)
    return _dot(ah, bh) + (_dot(ah, bl) + _dot(al, bh))


def _dot3_nt(a, b):
    ah, al = _split2(a)
    bh, bl = _split2(b)
    return _dot_nt(ah, bh) + (_dot_nt(ah, bl) + _dot_nt(al, bh))


def _dot_exact_lhs(sel, x):
    hi, mid, lo = _split3(x)
    return _dot(sel, hi) + (_dot(sel, mid) + _dot(sel, lo))


def _modulate(x, g, shift, scale):
    ms = jnp.mean(x * x, axis=-1, keepdims=True)
    return (x * lax.rsqrt(ms + EPS) * g) * (1.0 + scale) + shift


def _sigmoid(x):
    return 1.0 / (1.0 + jnp.exp(-x))


def _softplus(y):
    return jnp.maximum(y, 0.0) + jnp.log(1.0 + jnp.exp(-jnp.abs(y)))


def _erf(x):
    return lax.erf(x)


def _gelu(x):
    return 0.5 * x * (1.0 + _erf(x * (2.0 ** -0.5)))


def _ada_kernel(s_ref, w_ref, b_ref, o_ref):
    s = s_ref[...]
    s = s * _sigmoid(s)
    o_ref[...] = _dot3(s, w_ref[...]) + b_ref[...]


def _adaln(cond8, w_ada, b_ada):
    depth, d, n = w_ada.shape
    tn = 1024
    return pl.pallas_call(
        _ada_kernel,
        grid=(depth, n // tn),
        in_specs=[
            pl.BlockSpec((SUBLANES, d), lambda l, j: (0, 0)),
            pl.BlockSpec((None, d, tn), lambda l, j: (l, 0, j)),
            pl.BlockSpec((None, 1, tn), lambda l, j: (l, 0, j)),
        ],
        out_specs=pl.BlockSpec((None, SUBLANES, tn), lambda l, j: (l, 0, j)),
        out_shape=jax.ShapeDtypeStruct((depth, SUBLANES, n), F32),
        compiler_params=_cparams("parallel", "parallel"),
        name="adaln",
    )(cond8, w_ada, b_ada.reshape(depth, 1, n))


class _Geom:
    def __init__(self, batch, ctx_len, seq_len, d_model, tm):
        self.b, self.l, self.t, self.d = batch, ctx_len, seq_len, d_model
        self.s = ctx_len + seq_len
        self.n = batch * self.s
        self.tm = tm
        assert ctx_len % tm == 0 and seq_len % tm == 0
        self.tpb = self.s // tm
        self.nct = ctx_len // tm
        self.ntiles = self.n // tm

    def mod_spec(self, nlead=0):
        tpb, nct = self.tpb, self.nct

        def imap(*ids):
            i = ids[nlead]
            return (i // tpb, ((i % tpb) >= nct).astype(jnp.int32), 0, 0)

        return pl.BlockSpec((None, None, 6, self.d), imap)


def _modmm_kernel(x_ref, g_ref, mod_ref, w_ref, o_ref, *h_ref, which):
    h = _modulate(x_ref[...], g_ref[...], mod_ref[3 * which:3 * which + 1, :],
                  mod_ref[3 * which + 1:3 * which + 2, :])
    if h_ref:
        h_ref[0][...] = h
    o_ref[...] = _dot(h.astype(BF16), w_ref[...])


def _mod_matmul(geo, x, g, mods, w, which, emit_h):
    n, d = x.shape
    nn = w.shape[1]
    tm = geo.tm
    out_shape = [jax.ShapeDtypeStruct((n, nn), F32)]
    out_specs = [pl.BlockSpec((tm, nn), lambda i: (i, 0))]
    if emit_h:
        out_shape.append(jax.ShapeDtypeStruct((n, d), F32))
        out_specs.append(pl.BlockSpec((tm, d), lambda i: (i, 0)))
    res = pl.pallas_call(
        functools.partial(_modmm_kernel, which=which),
        grid=(geo.ntiles,),
        in_specs=[
            pl.BlockSpec((tm, d), lambda i: (i, 0)),
            pl.BlockSpec((1, d), lambda i: (0, 0)),
            geo.mod_spec(),
            pl.BlockSpec((d, nn), lambda i: (0, 0)),
        ],
        out_specs=out_specs,
        out_shape=out_shape,
        compiler_params=_cparams("parallel"),
        name="mod_matmul",
    )(x, g.reshape(1, d), mods, w)
    return res if emit_h else res[0]


def _mmres_kernel(y_ref, w_ref, x_ref, mod_ref, o_ref, *, gidx):
    acc = _dot(y_ref[...], w_ref[...])
    o_ref[...] = x_ref[...] + mod_ref[gidx:gidx + 1, :] * acc


def _matmul_res(geo, y, w, x, mods, gidx):
    n, k = y.shape
    d = x.shape[1]
    tm = geo.tm
    return pl.pallas_call(
        functools.partial(_mmres_kernel, gidx=gidx),
        grid=(geo.ntiles,),
        in_specs=[
            pl.BlockSpec((tm, k), lambda i: (i, 0)),
            pl.BlockSpec((k, d), lambda i: (0, 0)),
            pl.BlockSpec((tm, d), lambda i: (i, 0)),
            geo.mod_spec(),
        ],
        out_specs=pl.BlockSpec((tm, d), lambda i: (i, 0)),
        out_shape=jax.ShapeDtypeStruct((n, d), F32),
        compiler_params=_cparams("parallel"),
        name="matmul_res",
    )(y, w, x, mods)


def _bmm_kernel(x_ref, w_ref, o_ref, *, acts):
    j = pl.program_id(0)
    y = _dot(x_ref[...], w_ref[...])
    out = y
    for jj, a in enumerate(acts):
        if a == "tanh":
            out = jnp.where(j == jj, jnp.tanh(y), out)
        elif a == "sigmoid":
            out = jnp.where(j == jj, _sigmoid(y), out)
    o_ref[...] = out.astype(o_ref.dtype)


def _bmm(x3, w3, src, acts, out_dtype, tm):
    _, n, k = x3.shape
    nj, _, nn = w3.shape
    src = tuple(src)

    def xmap(j, i):
        idx = jnp.int32(src[0])
        for jj in range(1, nj):
            idx = jnp.where(j == jj, jnp.int32(src[jj]), idx)
        return (idx, i, 0)

    return pl.pallas_call(
        functools.partial(_bmm_kernel, acts=tuple(acts)),
        grid=(nj, n // tm),
        in_specs=[
            pl.BlockSpec((None, tm, k), xmap),
            pl.BlockSpec((None, k, nn), lambda j, i: (j, 0, 0)),
        ],
        out_specs=pl.BlockSpec((None, tm, nn), lambda j, i: (j, i, 0)),
        out_shape=jax.ShapeDtypeStruct((nj, n, nn), out_dtype),
        compiler_params=_cparams("parallel", "parallel"),
        name="bmm",
    )(x3, w3)


def _rwmix_kernel(x_ref, xp_ref, xn_ref, g_ref, mod_ref, mix_ref, o_ref, *, tpb, nct):
    i = pl.program_id(0)
    tm = x_ref.shape[0]
    g = g_ref[...]
    shift = mod_ref[0:1, :]
    scale = mod_ref[1:2, :]
    h = _modulate(x_ref[...], g, shift, scale)
    hp = _modulate(xp_ref[...], g, shift, scale)[SUBLANES - 1:SUBLANES, :]
    hn = _modulate(xn_ref[...], g, shift, scale)[0:1, :]
    it = i % tpb
    first = jnp.logical_or(it == 0, it == nct)
    last = jnp.logical_or(it == nct - 1, it == tpb - 1)
    hp = jnp.where(first, 0.0, hp)
    hn = jnp.where(last, 0.0, hn)
    rows = lax.broadcasted_iota(jnp.int32, h.shape, 0)
    prev = jnp.where(rows == 0, hp, pltpu.roll(h, 1, axis=0))
    nxt = jnp.where(rows == tm - 1, hn, pltpu.roll(h, tm - 1, axis=0))
    xx = 0.5 * (prev + nxt) - h
    for m in range(6):
        o_ref[m] = (h + xx * mix_ref[m:m + 1, :]).astype(BF16)


def _rwkv_mix(geo, x, g, mods, mix):
    n, d = x.shape
    tm = geo.tm
    r8 = tm // SUBLANES
    nblk8 = n // SUBLANES
    return pl.pallas_call(
        functools.partial(_rwmix_kernel, tpb=geo.tpb, nct=geo.nct),
        grid=(geo.ntiles,),
        in_specs=[
            pl.BlockSpec((tm, d), lambda i: (i, 0)),
            pl.BlockSpec((SUBLANES, d), lambda i: (jnp.maximum(i * r8 - 1, 0), 0)),
            pl.BlockSpec((SUBLANES, d), lambda i: (jnp.minimum((i + 1) * r8, nblk8 - 1), 0)),
            pl.BlockSpec((1, d), lambda i: (0, 0)),
            geo.mod_spec(),
            pl.BlockSpec((6, d), lambda i: (0, 0)),
        ],
        out_specs=pl.BlockSpec((6, tm, d), lambda i: (0, i, 0)),
        out_shape=jax.ShapeDtypeStruct((6, n, d), BF16),
        compiler_params=_cparams("parallel"),
        name="rwkv_mix",
    )(x, x, x, g.reshape(1, d), mods, mix)


def _head_sum(x, bd):
    hi, lo = _split2(x)
    return _dot(hi, bd) + _dot(lo, bd)


def _rwfeat_kernel(k_ref, v_ref, tl_ref, w2_ref, a2_ref, g2_ref, v2_ref, pv_ref, bd_ref,
                   *rest, has_vlora):
    if has_vlora:
        vf_ref, lw_ref, kd_ref, as_ref, kk_ref, g_ref, vo_ref = rest
    else:
        lw_ref, kd_ref, as_ref, kk_ref, g_ref = rest
    k = k_ref[...]
    tw = tl_ref[0]
    ta = tl_ref[1]
    tg = tl_ref[2]
    w0 = pv_ref[0:2, :]
    a0 = pv_ref[2:4, :]
    k_k = pv_ref[5:6, :]
    k_a = pv_ref[6:7, :]
    for z in range(2):
        sl = slice(LORA_PAD * z, LORA_PAD * (z + 1))
        lora_w = _dot(tw[:, sl], w2_ref[z])
        w = -_softplus(-(w0[z:z + 1, :] + lora_w)) - 0.5
        lw_ref[z] = -jnp.exp(w)
        a_sig = _sigmoid(a0[z:z + 1, :] + _dot(ta[:, sl], a2_ref[z]))
        as_ref[z] = a_sig.astype(BF16)
        kd_ref[z] = k * (1.0 + (a_sig - 1.0) * k_a)
    g_ref[...] = _dot(tg, g2_ref[...]).astype(BF16)
    kkr = k * k_k
    ss = _head_sum(kkr * kkr, bd_ref[...])
    kk_ref[...] = (kkr * lax.rsqrt(ss + 1e-12)).astype(BF16)
    if has_vlora:
        v = v_ref[...]
        tv = tl_ref[3]
        gate = _sigmoid(pv_ref[4:5, :] + _dot(tv[:, 0:LORA_PAD], v2_ref[...]))
        vo_ref[...] = v + (vf_ref[...] - v) * gate


def _rwkv_feat(geo, rkv, tl, w2p, a2p, g2, v2p, pvec, vfirst):
    _, n, d = rkv.shape
    tm, tc = geo.tm, 512
    has_vlora = vfirst is not None
    lt = tl.shape[2]
    ii = lax.broadcasted_iota(jnp.int32, (tc, tc), 0) // RW_HEAD
    jj = lax.broadcasted_iota(jnp.int32, (tc, tc), 1) // RW_HEAD
    bd = (ii == jj).astype(BF16)
    row = lambda i, j: (i, j)
    in_specs = [
        pl.BlockSpec((None, tm, tc), lambda i, j: (1, i, j)),
        pl.BlockSpec((None, tm, tc), lambda i, j: (2, i, j)),
        pl.BlockSpec((4, tm, lt), lambda i, j: (0, i, 0)),
        pl.BlockSpec((2, LORA_PAD, tc), lambda i, j: (0, 0, j)),
        pl.BlockSpec((2, LORA_PAD, tc), lambda i, j: (0, 0, j)),
        pl.BlockSpec((lt, tc), lambda i, j: (0, j)),
        pl.BlockSpec((LORA_PAD, tc), lambda i, j: (0, j)),
        pl.BlockSpec((SUBLANES, tc), lambda i, j: (0, j)),
        pl.BlockSpec((tc, tc), lambda i, j: (0, 0)),
    ]
    args = [rkv, rkv, tl, w2p, a2p, g2, v2p, pvec, bd]
    dir_spec = pl.BlockSpec((2, tm, tc), lambda i, j: (0, i, j))
    out_specs = [dir_spec, dir_spec, dir_spec, pl.BlockSpec((tm, tc), row), pl.BlockSpec((tm, tc), row)]
    out_shape = [jax.ShapeDtypeStruct((2, n, d), dt) for dt in (F32, F32, BF16)]
    out_shape += [jax.ShapeDtypeStruct((n, d), BF16)] * 2
    if has_vlora:
        in_specs.append(pl.BlockSpec((tm, tc), row))
        args.append(vfirst)
        out_specs.append(pl.BlockSpec((tm, tc), row))
        out_shape.append(jax.ShapeDtypeStruct((n, d), F32))
    res = pl.pallas_call(
        functools.partial(_rwfeat_kernel, has_vlora=has_vlora),
        grid=(geo.ntiles, d // tc),
        in_specs=in_specs,
        out_specs=out_specs,
        out_shape=out_shape,
        compiler_params=_cparams("parallel", "parallel"),
        name="rwkv_feat",
    )(*args)
    if has_vlora:
        lw, kd, asg, kk, g, v = res
    else:
        lw, kd, asg, kk, g = res
        v = rkv[2]
    return lw, kd, asg, kk, g, v


def _wkv_chunk_kernel(r_ref, v_ref, kk_ref, lw_ref, kd_ref, as_ref,
                      m_ref, ga_ref, rq_ref, o0_ref, pc_ref, *, npairs):
    c = WKV_CHUNK
    c2 = 2 * c
    sgn = 1 - 2 * pl.program_id(0)
    ri = lax.broadcasted_iota(jnp.int32, (c2, c2), 0)
    ci = lax.broadcasted_iota(jnp.int32, (c2, c2), 1)
    same = (ri >= c) == (ci >= c)
    tt = jnp.where(ri >= c, ri - c, ri)
    ss = jnp.where(ci >= c, ci - c, ci)
    earlier = (ss - tt) * sgn < 0
    strict = jnp.logical_and(same, earlier)
    incl = jnp.logical_and(same, jnp.logical_or(earlier, ss == tt))
    eye = (ri == ci).astype(F32)
    r64 = lax.broadcasted_iota(jnp.int32, (c, c), 0)
    c64 = lax.broadcasted_iota(jnp.int32, (c, c), 1)
    ltri = jnp.where((c64 - r64) * sgn <= 0, 1.0, 0.0).astype(BF16)
    head0 = lax.broadcasted_iota(jnp.int32, (c, LANES), 1) < RW_HEAD
    pairs = range(npairs)

    def stack(x):
        return jnp.concatenate([jnp.where(head0, x, 0.0), jnp.where(head0, 0.0, x)], axis=0)

    def dup(x):
        return jnp.concatenate([x, x], axis=0)

    def bf(x):
        return x.astype(BF16)

    lhs, rhs, a2, bp2, kp2, vst, r2 = [], [], [], [], [], [], []
    lw_all = lw_ref[...]
    cum_all = _dot_exact_lhs(ltri, lw_all)
    for p in pairs:
        sl = slice(LANES * p, LANES * (p + 1))
        lw = lw_all[:, sl]
        cum = cum_all[:, sl]
        tot = jnp.sum(lw, axis=0, keepdims=True)
        p_inv = jnp.exp(-cum)
        p_end = jnp.exp(tot - cum)
        kk = kk_ref[:, sl]
        b = kk * as_ref[:, sl]
        kd = kd_ref[:, sl]
        a2p = stack(-kk * jnp.exp(cum - lw))
        r2p = stack(r_ref[:, sl] * jnp.exp(cum))
        a2.append(bf(a2p))
        r2.append(r2p)
        lhs.append(jnp.concatenate([a2[p], bf(r2p)], axis=0))
        rhs.append(jnp.concatenate([dup(bf(b * p_inv)), dup(bf(kd * p_inv))], axis=0))
        bp2.append(bf(stack(b * p_end)))
        kp2.append(bf(stack(kd * p_end)))
        vst.append(bf(stack(v_ref[:, sl])))
        pc_ref[:, sl] = jnp.broadcast_to(jnp.exp(tot), (SUBLANES, LANES))
    gram = [_dot_nt(lhs[p], rhs[p]) for p in pairs]
    nab = [jnp.where(strict, gram[p][0:c2, 0:c2], 0.0) for p in pairs]
    nrb = [bf(jnp.where(incl, gram[p][c2:2 * c2, 0:c2], 0.0)) for p in pairs]
    nk = [bf(jnp.concatenate([jnp.where(strict, gram[p][0:c2, c2:2 * c2], 0.0),
                              jnp.where(incl, gram[p][c2:2 * c2, c2:2 * c2], 0.0)], axis=0))
          for p in pairs]
    quads = range(npairs // 2)

    def side(x0, x1):
        return jnp.concatenate([x0, x1], axis=1)

    def diag(x0, x1):
        z0 = jnp.zeros_like(x0)
        return jnp.concatenate([side(x0, z0), side(z0, x1)], axis=0)

    def diag_halves(x):
        return diag(x[:, 0:c2], x[:, c2:2 * c2])

    def unside(xs):
        return [xs[p // 2][:, c2 * (p % 2):c2 * (p % 2 + 1)] for p in pairs]

    xv = unside([_dot(side(nk[2 * q], nk[2 * q + 1]), diag(vst[2 * q], vst[2 * q + 1]))
                 for q in quads])
    tinv = [side(eye + nab[2 * q], eye + nab[2 * q + 1]) for q in quads]
    npow = [bf(side(nab[2 * q], nab[2 * q + 1])) for q in quads]
    for _ in range(int(math.log2(c)) - 1):
        npow = [bf(_dot(npow[q], diag_halves(npow[q]))) for q in quads]
        tinv = [tinv[q] + _dot(bf(tinv[q]), diag_halves(npow[q])) for q in quads]
    tinv = unside(tinv)
    y = [_dot(bf(tinv[p]), jnp.concatenate([a2[p], bf(xv[p][0:c2, :])], axis=1)) for p in pairs]
    yb = [bf(y[p]) for p in pairs]
    z = [_dot(nrb[p], yb[p]) for p in pairs]
    mg = unside([_dot(bf(side(y[2 * q].T, y[2 * q + 1].T)), diag(bp2[2 * q], bp2[2 * q + 1]))
                 for q in quads])
    vk = unside([_dot(bf(side(vst[2 * q].astype(F32).T, vst[2 * q + 1].astype(F32).T)),
                      diag(kp2[2 * q], kp2[2 * q + 1])) for q in quads])
    for p in pairs:
        sl = slice(LANES * p, LANES * (p + 1))
        m_ref[:, sl] = bf(mg[p][0:c2, :])
        ga_ref[:, sl] = mg[p][c2:2 * c2, :] + vk[p]
        rq_ref[:, sl] = bf(r2[p] + z[p][:, 0:c2])
        o0 = z[p][:, c2:2 * c2] + xv[p][c2:2 * c2, :]
        o0_ref[:, sl] = o0[0:c, :] + o0[c:c2, :]


def _wkv_scan_kernel(m_ref, ga_ref, rq_ref, o0_ref, pc_ref, o_ref, g_scr, *, npairs):
    c = WKV_CHUNK

    @pl.when(pl.program_id(2) == 0)
    def _():
        g_scr[...] = jnp.zeros_like(g_scr)

    for p in range(npairs):
        sl = slice(LANES * p, LANES * (p + 1))
        g = g_scr[p]
        g_hi, g_lo = _split2(g)
        rq = rq_ref[:, sl]
        o_st = _dot_nt(rq, g_hi) + _dot_nt(rq, g_lo)
        o_ref[:, sl] = o_st[0:c, :] + o_st[c:2 * c, :] + o0_ref[:, sl]
        m = m_ref[:, sl]
        g_scr[p] = g * pc_ref[0:1, sl] + (_dot(g_hi, m) + _dot(g_lo, m)) + ga_ref[:, sl]


WKV_CHUNK_LANES = 2048


def _wkv_bidir(r, v, kk, lw, kd, asg, batch, ctx_len):
    n, d = r.shape
    c = WKV_CHUNK
    s = n // batch
    ncs = s // c
    ncc = ctx_len // c
    nch = n // c
    lanes = min(WKV_CHUNK_LANES, d)
    ngrp = d // lanes
    shared = pl.BlockSpec((c, lanes), lambda z, i, j: (i, j))
    perdir = pl.BlockSpec((None, c, lanes), lambda z, i, j: (z, i, j))
    big = pl.BlockSpec((None, 2 * c, lanes), lambda z, i, j: (z, i, j))
    m_, ga_, rq_, o0_, pc_ = pl.pallas_call(
        functools.partial(_wkv_chunk_kernel, npairs=lanes // LANES),
        grid=(2, nch, ngrp),
        in_specs=[shared, shared, shared, perdir, perdir, perdir],
        out_specs=[big, big, big, perdir,
                   pl.BlockSpec((None, SUBLANES, lanes), lambda z, i, j: (z, i, j))],
        out_shape=[jax.ShapeDtypeStruct((2, nch * 2 * c, d), dt) for dt in (BF16, F32, BF16)]
        + [jax.ShapeDtypeStruct((2, n, d), F32),
           jax.ShapeDtypeStruct((2, nch * SUBLANES, d), F32)],
        compiler_params=_cparams("parallel", "parallel", "parallel"),
        name="wkv_chunk",
    )(r, v, kk, lw, kd, asg)

    def cmap(z, b, cc):
        back = jnp.where(cc < ncc, ncc - 1 - cc, ncs + ncc - 1 - cc)
        return (z, b * ncs + jnp.where(z == 0, cc, back), 0)

    out = pl.pallas_call(
        functools.partial(_wkv_scan_kernel, npairs=d // LANES),
        grid=(2, batch, ncs),
        in_specs=[
            pl.BlockSpec((None, 2 * c, d), cmap),
            pl.BlockSpec((None, 2 * c, d), cmap),
            pl.BlockSpec((None, 2 * c, d), cmap),
            pl.BlockSpec((None, c, d), cmap),
            pl.BlockSpec((None, SUBLANES, d), cmap),
        ],
        out_specs=pl.BlockSpec((None, c, d), cmap),
        out_shape=jax.ShapeDtypeStruct((2, n, d), F32),
        scratch_shapes=[pltpu.VMEM((d // LANES, 2 * c, 2 * c), F32)],
        compiler_params=_cparams("parallel", "parallel", "arbitrary"),
        name="wkv_scan",
    )(m_, ga_, rq_, o0_, pc_)
    return out


def _rwread_kernel(o_ref, r_ref, kd_ref, v_ref, g_ref, pv_ref, bd_ref, y_ref):
    bd = bd_ref[...]
    wkv = o_ref[0] + o_ref[1]
    inv = 1.0 / RW_HEAD
    mu = _head_sum(wkv, bd) * inv
    dev = wkv - mu
    var = _head_sum(dev * dev, bd) * inv
    y = dev * lax.rsqrt(var + GN_EPS) * pv_ref[1:2, :] + pv_ref[2:3, :]
    rk = r_ref[...] * (kd_ref[0] + kd_ref[1]) * pv_ref[0:1, :]
    y = y + _head_sum(rk, bd) * v_ref[...]
    y_ref[...] = (y * g_ref[...]).astype(BF16)


def _rwkv_readout(geo, wkv, r, kd, v, g, pvec):
    n, d = r.shape
    tm, tc = geo.tm, 512
    ii = lax.broadcasted_iota(jnp.int32, (tc, tc), 0) // RW_HEAD
    jj = lax.broadcasted_iota(jnp.int32, (tc, tc), 1) // RW_HEAD
    bd = (ii == jj).astype(BF16)
    row = pl.BlockSpec((tm, tc), lambda i, j: (i, j))
    dirs = pl.BlockSpec((2, tm, tc), lambda i, j: (0, i, j))
    return pl.pallas_call(
        _rwread_kernel,
        grid=(geo.ntiles, d // tc),
        in_specs=[dirs, row, dirs, row, row,
                  pl.BlockSpec((SUBLANES, tc), lambda i, j: (0, j)),
                  pl.BlockSpec((tc, tc), lambda i, j: (0, 0))],
        out_specs=row,
        out_shape=jax.ShapeDtypeStruct((n, d), BF16),
        compiler_params=_cparams("parallel", "parallel"),
        name="rwkv_readout",
    )(wkv, r, kd, v, g, pvec, bd)


def _mlaqkv_kernel(z_ref, qn_ref, kvn_ref, wuq_ref, wukv_ref, gq_ref, gk_ref, cos_ref, sin_ref,
                   q_ref, k_ref, v_ref, *, q_lora, kv_lora):
    z = z_ref[...]
    cq = z[:, 0:q_lora]
    ckv = z[:, q_lora:q_lora + kv_lora]
    krot = z[:, q_lora + kv_lora:q_lora + kv_lora + LANES]

    def rms(x, g):
        ms = jnp.mean(x * x, axis=-1, keepdims=True)
        return x * lax.rsqrt(ms + EPS) * g

    qf = _dot(rms(cq, qn_ref[...]).astype(BF16), wuq_ref[...])
    kvf = _dot(rms(ckv, kvn_ref[...]).astype(BF16), wukv_ref[...])
    cos = cos_ref[...]
    sin = sin_ref[...]
    half = QK_ROPE // 2
    lane = lax.broadcasted_iota(jnp.int32, cos.shape, 1)

    def rope(x):
        up = pltpu.roll(x, LANES - half, axis=1)
        dn = pltpu.roll(x, half, axis=1)
        return x * cos + jnp.where(lane < half, up, dn) * sin

    inv_w = 1.0 / (QK_NOPE + QK_ROPE)
    gq = gq_ref[...]
    gk = gk_ref[...]
    kr_ss = jnp.sum(krot * krot, axis=-1, keepdims=True)
    for h in range(MLA_HEADS):
        o = QK_PAD * h
        qh = qf[:, o:o + QK_PAD]
        rs = lax.rsqrt(jnp.sum(qh * qh, axis=-1, keepdims=True) * inv_w + EPS)
        qn = qh * rs * gq * (ATTN_SCALE * math.log2(math.e))
        q_ref[:, o:o + QK_NOPE] = qn[:, 0:QK_NOPE].astype(BF16)
        q_ref[:, o + QK_NOPE:o + QK_PAD] = rope(qn[:, QK_NOPE:QK_PAD]).astype(BF16)
        kn = kvf[:, o:o + QK_NOPE]
        rsk = lax.rsqrt((jnp.sum(kn * kn, axis=-1, keepdims=True) + kr_ss) * inv_w + EPS)
        k_ref[:, o:o + QK_NOPE] = (kn * rsk * gk[:, 0:QK_NOPE]).astype(BF16)
        k_ref[:, o + QK_NOPE:o + QK_PAD] = rope(krot * rsk * gk[:, QK_NOPE:QK_PAD]).astype(BF16)
        v_ref[:, V_HEAD * h:V_HEAD * (h + 1)] = kvf[:, o + QK_NOPE:o + QK_PAD].astype(BF16)


def _mla_qkv(geo, z, qn, kvn, wuq_p, wukv, gq_p, gk_p, cos_t, sin_t):
    n, zw = z.shape
    tm = geo.tm
    q_lora, kv_lora = qn.shape[0], kvn.shape[0]
    hq = MLA_HEADS * QK_PAD
    tpb = geo.tpb
    full = lambda i: (0, 0)
    rowmap = lambda i: (i, 0)
    return pl.pallas_call(
        functools.partial(_mlaqkv_kernel, q_lora=q_lora, kv_lora=kv_lora),
        grid=(geo.ntiles,),
        in_specs=[
            pl.BlockSpec((tm, zw), rowmap),
            pl.BlockSpec((1, q_lora), full),
            pl.BlockSpec((1, kv_lora), full),
            pl.BlockSpec((q_lora, hq), full),
            pl.BlockSpec((kv_lora, hq), full),
            pl.BlockSpec((1, QK_PAD), full),
            pl.BlockSpec((1, QK_PAD), full),
            pl.BlockSpec((tm, LANES), lambda i: (i % tpb, 0)),
            pl.BlockSpec((tm, LANES), lambda i: (i % tpb, 0)),
        ],
        out_specs=[pl.BlockSpec((tm, hq), rowmap), pl.BlockSpec((tm, hq), rowmap),
                   pl.BlockSpec((tm, MLA_HEADS * V_HEAD), rowmap)],
        out_shape=[jax.ShapeDtypeStruct((n, hq), BF16), jax.ShapeDtypeStruct((n, hq), BF16),
                   jax.ShapeDtypeStruct((n, MLA_HEADS * V_HEAD), BF16)],
        compiler_params=_cparams("parallel"),
        name="mla_qkv",
    )(z, qn.reshape(1, -1), kvn.reshape(1, -1), wuq_p, wukv, gq_p, gk_p, cos_t, sin_t)


ATTN_SUBTILES = 2


def _attn_kernel(q_ref, k_ref, vt_ref, o_ref):
    k = k_ref[...]
    vt = vt_ref[...]
    cols = q_ref.shape[0] // ATTN_SUBTILES
    subs = range(ATTN_SUBTILES)
    s = [_dot_nt(k, q_ref[cols * i:cols * (i + 1), :]) for i in subs]
    m = [jnp.max(s[i], axis=0, keepdims=True) for i in subs]
    p = [jnp.exp2(s[i] - m[i]) for i in subs]
    l = [jnp.sum(p[i], axis=0, keepdims=True) for i in subs]
    for i in subs:
        ot = _dot(vt, p[i].astype(BF16)) / l[i]
        o_ref[cols * i:cols * (i + 1), :] = ot.T.astype(BF16)


def _attn_call(q3, k3, vt4, tq, nkeys):
    b, sq, _ = q3.shape
    return pl.pallas_call(
        _attn_kernel,
        grid=(b, MLA_HEADS, sq // tq),
        in_specs=[
            pl.BlockSpec((None, tq, QK_PAD), lambda bb, h, i: (bb, i, h)),
            pl.BlockSpec((None, nkeys, QK_PAD), lambda bb, h, i: (bb, 0, h)),
            pl.BlockSpec((None, None, V_HEAD, nkeys), lambda bb, h, i: (bb, h, 0, 0)),
        ],
        out_specs=pl.BlockSpec((None, tq, V_HEAD), lambda bb, h, i: (bb, i, h)),
        out_shape=jax.ShapeDtypeStruct((b, sq, MLA_HEADS * V_HEAD), BF16),
        compiler_params=_cparams("parallel", "parallel", "parallel"),
        name="mla_attention",
    )(q3, k3, vt4)


def _mla_attention(geo, q, k, v):
    b, s, l = geo.b, geo.s, geo.l
    q3 = q.reshape(b, s, MLA_HEADS * QK_PAD)
    k3 = k.reshape(b, s, MLA_HEADS * QK_PAD)
    vt4 = v.reshape(b, s, MLA_HEADS, V_HEAD).transpose(0, 2, 3, 1)
    o_ctx = _attn_call(q3[:, :l], k3, vt4, geo.tm, l)
    tq_lat = 2 * geo.tm if geo.t % (2 * geo.tm) == 0 else geo.tm
    o_lat = _attn_call(q3[:, l:], k3, vt4, tq_lat, s)
    return jnp.concatenate([o_ctx, o_lat], axis=1).reshape(b * s, MLA_HEADS * V_HEAD)


def _topk_rows(s, k, payload=None):
    rows = lax.broadcasted_iota(jnp.int32, s.shape, 0).astype(F32)
    big = float(s.shape[0])
    vals, idxs = [], []
    for _ in range(k):
        m = jnp.max(s, axis=0, keepdims=True)
        idx = jnp.min(jnp.where(s == m, rows, big), axis=0, keepdims=True)
        hit = rows == idx
        vals.append(m)
        if payload is None:
            idxs.append(idx)
        else:
            idxs.append(jnp.sum(jnp.where(hit, payload, 0.0), axis=0, keepdims=True))
        s = jnp.where(hit, -jnp.inf, s)
    return jnp.concatenate(vals, axis=0), jnp.concatenate(idxs, axis=0)


def _peersel_kernel(q_ref, qn_ref, keys_ref, e_ref, g_ref):
    q = q_ref[...]
    ms = jnp.mean(q * q, axis=-1, keepdims=True)
    qn = q * lax.rsqrt(ms + EPS) * qn_ref[...]
    half = D_KEY // 2
    s1 = _dot3_nt(keys_ref[0], qn[:, 0:half])
    s2 = _dot3_nt(keys_ref[1], qn[:, half:D_KEY])
    t1, i1 = _topk_rows(s1, PEER_TOPK)
    t2, i2 = _topk_rows(s2, PEER_TOPK)
    k = PEER_TOPK
    sub = lax.broadcasted_iota(jnp.int32, (SUBLANES, t1.shape[1]), 0)
    cand = [t1[0:1, :] + t2]
    cidx = [i1[0:1, :] * float(N_KEYS) + i2]
    for p in range(1, k // 2):
        live = sub < k // (p + 1)
        cand.append(jnp.where(live, t1[p:p + 1, :] + t2[0:SUBLANES, :], -jnp.inf))
        cidx.append(i1[p:p + 1, :] * float(N_KEYS) + i2[0:SUBLANES, :])
    cand.append(t1[k // 2:k, :] + t2[0:1, :])
    cidx.append(i1[k // 2:k, :] * float(N_KEYS) + i2[0:1, :])
    best, eidx = _topk_rows(jnp.concatenate(cand, axis=0), k, payload=jnp.concatenate(cidx, axis=0))
    ex = jnp.exp(best - jnp.max(best, axis=0, keepdims=True))
    g_ref[...] = ex / jnp.sum(ex, axis=0, keepdims=True)
    e_ref[...] = eidx.astype(jnp.int32)


def _peer_select(qp, q_norm, keys, tm, row0, n):
    assert row0 % tm == 0 and n % tm == 0
    tile0 = row0 // tm
    return pl.pallas_call(
        _peersel_kernel,
        grid=(n // tm, PEER_HEADS),
        in_specs=[
            pl.BlockSpec((tm, D_KEY), lambda i, h: (tile0 + i, h)),
            pl.BlockSpec((1, D_KEY), lambda i, h: (0, 0)),
            pl.BlockSpec((2, N_KEYS, D_KEY // 2), lambda i, h: (0, 0, 0)),
        ],
        out_specs=[pl.BlockSpec((PEER_TOPK, tm), lambda i, h: (h, i)),
                   pl.BlockSpec((PEER_TOPK, tm), lambda i, h: (h, i))],
        out_shape=[jax.ShapeDtypeStruct((PEER_HEADS * PEER_TOPK, n), jnp.int32),
                   jax.ShapeDtypeStruct((PEER_HEADS * PEER_TOPK, n), F32)],
        compiler_params=_cparams("parallel", "parallel"),
        name="peer_select",
    )(qp, q_norm.reshape(1, D_KEY), keys)


GATHER_SLOTS = 4
SLAB_PAD = 1


def _pack_expert_table(u, v):
    ne, d = u.shape
    ub = lax.bitcast_convert_type(u.astype(BF16), jnp.uint16).astype(jnp.uint32)
    vb = lax.bitcast_convert_type(v.astype(BF16), jnp.uint16).astype(jnp.uint32)
    return ((vb << 16) | ub).reshape(ne, d // LANES, LANES)


def _peergather_kernel(idx_ref, idxn_ref, gate_ref, h_ref, x_ref, mod_ref, tab_ref, o_ref,
                       *scratch, tb):
    nsel = PEER_HEADS * PEER_TOPK
    nrow = h_ref.shape[1] // LANES
    pitch = nrow + SLAB_PAD
    ns = GATHER_SLOTS
    bufs, sem = scratch[:ns], scratch[ns]
    lane_t = lax.broadcasted_iota(jnp.int32, (nsel, tb), 1)
    g2 = mod_ref[5:6, :]
    step = pl.program_id(0)
    nsteps = pl.num_programs(0)

    def row_copy(ids_ref, t, j, slot):
        return pltpu.make_async_copy(
            tab_ref.at[ids_ref[t, j]],
            bufs[slot].at[pl.ds(j * pitch, nrow), :],
            sem.at[slot])

    def issue(ids_ref, t, slot):
        for j in range(nsel):
            row_copy(ids_ref, t, j, slot).start(priority=j % 2)

    def wait(t, slot):
        for j in range(nsel):
            row_copy(idx_ref, t, j, slot).wait()

    def packed(slot, s):
        return bufs[slot][pl.ds(s, nsel, stride=pitch), :]

    def compute(t, slot):
        hrow = h_ref[pl.ds(t, 1), :]
        acc = jnp.zeros((nsel, LANES), F32)
        for s in range(nrow):
            u = lax.bitcast_convert_type(packed(slot, s) << 16, F32)
            acc = acc + u * hrow[:, LANES * s:LANES * (s + 1)]
        dots = jnp.sum(acc, axis=-1, keepdims=True)
        gcol = jnp.sum(jnp.where(lane_t == t, gate_ref[...], 0.0), axis=-1, keepdims=True)
        coef = gcol * _gelu(dots)
        outs = []
        for s in range(nrow):
            vv = lax.bitcast_convert_type(packed(slot, s) & jnp.uint32(0xFFFF0000), F32)
            outs.append(jnp.sum(coef * vv, axis=0, keepdims=True))
        orow = jnp.concatenate(outs, axis=1)
        o_ref[pl.ds(t, 1), :] = x_ref[pl.ds(t, 1), :] + g2 * orow

    @pl.when(step == 0)
    def _():
        for s in range(ns - 1):
            issue(idx_ref, s, s)

    ngroups = tb // ns

    def body(g, carry):
        for s in range(ns):
            t = g * ns + s
            wait(t, s)
            issue(idx_ref, t + ns - 1, (s + ns - 1) % ns)
            compute(t, s)
        return carry

    lax.fori_loop(0, ngroups - 1, body, 0)
    for s in range(ns):
        t = (ngroups - 1) * ns + s
        wait(t, s)
        if s == 0:
            issue(idx_ref, tb - 1, ns - 1)
        else:
            @pl.when(step < nsteps - 1)
            def _():
                issue(idxn_ref, s - 1, s - 1)
        compute(t, s)


def _peer_gather(geo_g, eidx, gate_t, h2, x, mods, table, n):
    d = x.shape[1]
    tb = geo_g.tm
    nsel = PEER_HEADS * PEER_TOPK
    pitch = d // LANES + SLAB_PAD
    assert n % tb == 0
    nsteps = n // tb
    return pl.pallas_call(
        functools.partial(_peergather_kernel, tb=tb),
        grid=(nsteps,),
        in_specs=[
            pl.BlockSpec((tb, nsel), lambda i: (i, 0), memory_space=pltpu.SMEM),
            pl.BlockSpec((tb, nsel), lambda i: (jnp.minimum(i + 1, nsteps - 1), 0),
                         memory_space=pltpu.SMEM),
            pl.BlockSpec((nsel, tb), lambda i: (0, i)),
            pl.BlockSpec((tb, d), lambda i: (i, 0)),
            pl.BlockSpec((tb, d), lambda i: (i, 0)),
            geo_g.mod_spec(),
            pl.BlockSpec(memory_space=pl.ANY),
        ],
        out_specs=pl.BlockSpec((tb, d), lambda i: (i, 0)),
        out_shape=jax.ShapeDtypeStruct((n, d), F32),
        scratch_shapes=[pltpu.VMEM((nsel * pitch, LANES), jnp.uint32)] * GATHER_SLOTS
        + [pltpu.SemaphoreType.DMA((GATHER_SLOTS,))],
        compiler_params=_cparams("arbitrary"),
        name="peer_gather",
    )(eidx, eidx, gate_t, h2, x, mods, table)


SC_CORES = 2
SC_SUBCORES = 16
SC_LANES = 16
SC_CHUNK = 16
SC_SHARE = 0.3235

_ERF_ALPHA = (0.00022905065861350646, 0.0034082910107109506, 0.050955695062380861,
              0.18520832239976145, 1.128379143519084)
_ERF_BETA = (-1.1791602954361697e-7, 0.000023547966471313185, 0.0010179625278914885,
             0.014070470171167667, 0.11098505178285362, 0.49746925110067538, 1.0)
_ERF_CLAMP = 3.832506856900711


def _erf_rational(x):
    x = jnp.minimum(jnp.maximum(x, -_ERF_CLAMP), _ERF_CLAMP)
    x2 = x * x
    p = jnp.full_like(x, _ERF_ALPHA[0])
    for c in _ERF_ALPHA[1:]:
        p = p * x2 + c
    q = jnp.full_like(x, _ERF_BETA[0])
    for c in _ERF_BETA[1:]:
        q = q * x2 + c
    return x * p / q


def _peer_sc(eidx, gate, h, table, row0):
    n, nsel = eidx.shape
    d = h.shape[1]
    nw = SC_CORES * SC_SUBCORES
    assert n % (2 * nw) == 0 and nsel % SC_CHUNK == 0 and d % SC_LANES == 0
    tpw = n // nw
    nchunk = nsel // SC_CHUNK
    nvec = d // SC_LANES
    mesh = plsc.VectorSubcoreMesh(core_axis_name="c", subcore_axis_name="s",
                                  num_cores=SC_CORES, num_subcores=SC_SUBCORES)

    @functools.partial(
        pl.kernel, mesh=mesh,
        out_type=jax.ShapeDtypeStruct((n, d), F32),
        scratch_types=[
            pltpu.VMEM((nsel,), jnp.int32), pltpu.VMEM((nsel,), jnp.int32),
            pltpu.VMEM((nsel,), F32), pltpu.VMEM((nsel,), F32),
            pltpu.VMEM((d,), F32), pltpu.VMEM((d,), F32),
            pltpu.VMEM((d,), F32),
            pltpu.VMEM((SC_CHUNK, d), jnp.uint32),
            pltpu.VMEM((SC_CHUNK, d), jnp.uint32),
            pltpu.SemaphoreType.DMA, pltpu.SemaphoreType.DMA,
            pltpu.SemaphoreType.DMA, pltpu.SemaphoreType.DMA,
        ],
        compiler_params=pltpu.CompilerParams(needs_layout_passes=False),
        name="peer_sc",
    )
    def sc_kernel(eidx_hbm, gate_hbm, h_hbm, tab_hbm, y_hbm, idx_a, idx_b, gate_a, gate_b, h_a, h_b,
                  o_v, rows0, rows1, sem0, sem1, msem_a, msem_b):
        wid = lax.axis_index("s") * SC_CORES + lax.axis_index("c")
        lanes = lax.iota(jnp.int32, SC_LANES)
        zero = jnp.zeros((SC_LANES,), F32)
        bufs = ((rows0, sem0), (rows1, sem1))
        meta = ((idx_a, gate_a, h_a, msem_a), (idx_b, gate_b, h_b, msem_b))

        def meta_copies(tok, s):
            idx_v, gate_v, h_v, msem = meta[s]
            return (pltpu.make_async_copy(eidx_hbm.at[tok], idx_v, msem),
                    pltpu.make_async_copy(gate_hbm.at[tok], gate_v, msem),
                    pltpu.make_async_copy(h_hbm.at[row0 + tok], h_v, msem))

        def gather(s, c, slot):
            rows, sem = bufs[slot]
            return pltpu.make_async_copy(
                tab_hbm.at[meta[s][0].at[pl.ds(c * SC_CHUNK, SC_CHUNK)]], rows, sem)

        def chunk_compute(s, c, rows):
            _, gate_v, h_v, _ = meta[s]

            def dot_body(i, accs):
                hv = h_v[pl.ds(i * SC_LANES, SC_LANES)]
                out = []
                for e in range(SC_CHUNK):
                    w = rows[e, pl.ds(i * SC_LANES, SC_LANES)]
                    out.append(accs[e] + lax.bitcast_convert_type(w << 16, F32) * hv)
                return tuple(out)

            accs = lax.fori_loop(0, nvec, dot_body, (zero,) * SC_CHUNK)
            dots = zero
            for e in range(SC_CHUNK):
                dots = jnp.where(lanes == e, jnp.sum(accs[e]), dots)
            act = 0.5 * dots * (1.0 + _erf_rational(dots * (2.0 ** -0.5)))
            coef = gate_v[pl.ds(c * SC_CHUNK, SC_CHUNK)] * act
            splat = [jnp.full((SC_LANES,), jnp.sum(jnp.where(lanes == e, coef, 0.0)), F32)
                     for e in range(SC_CHUNK)]

            def acc_body(i, carry):
                o = o_v[pl.ds(i * SC_LANES, SC_LANES)]
                for e in range(SC_CHUNK):
                    w = rows[e, pl.ds(i * SC_LANES, SC_LANES)]
                    o = o + splat[e] * lax.bitcast_convert_type(w & jnp.uint32(0xFFFF0000), F32)
                o_v[pl.ds(i * SC_LANES, SC_LANES)] = o
                return carry

            lax.fori_loop(0, nvec, acc_body, 0)

        def zero_body(i, c2):
            o_v[pl.ds(i * SC_LANES, SC_LANES)] = zero
            return c2

        def token(tok, s, has_next):
            def when_next(fn):
                if isinstance(has_next, bool):
                    if has_next:
                        fn()
                else:
                    pl.when(has_next)(fn)

            def load_next():
                for cp in meta_copies(tok + 1, 1 - s):
                    cp.start()

            when_next(load_next)
            lax.fori_loop(0, nvec, zero_body, 0)
            for c in range(nchunk):
                if c + 1 < nchunk:
                    gather(s, c + 1, (c + 1) % 2).start()
                gather(s, c, c % 2).wait()
                chunk_compute(s, c, bufs[c % 2][0])

            def prefetch_next():
                for cp in meta_copies(tok + 1, 1 - s):
                    cp.wait()
                gather(1 - s, 0, 0).start()

            when_next(prefetch_next)
            pltpu.sync_copy(o_v, y_hbm.at[tok])

        base = wid * tpw
        for cp in meta_copies(base, 0):
            cp.start()
        for cp in meta_copies(base, 0):
            cp.wait()
        gather(0, 0, 0).start()

        def pair_body(g, carry):
            token(base + 2 * g, 0, True)
            token(base + 2 * g + 1, 1, g + 1 < tpw // 2)
            return carry

        lax.fori_loop(0, tpw // 2, pair_body, 0)

    return sc_kernel(eidx, gate, h, table)


def _pad_to(x, axis, size):
    pad = [(0, 0)] * x.ndim
    pad[axis] = (0, size - x.shape[axis])
    return jnp.pad(x, pad)


def _rwkv_layer(geo, xs, mods, norm1, mix, w_rkv, w_o, w0, w1, w2, a0, a1, a2, vl, g1, g2,
                k_k, k_a, r_k, ln_w, ln_b, vfirst):
    d = geo.d
    xm = _rwkv_mix(geo, xs, norm1, mods, mix)
    rkv = _bmm(xm, w_rkv.astype(BF16), (0, 2, 3), ("none",) * 3, F32, geo.tm)
    lt = g1.shape[1]
    w1c = jnp.concatenate([_pad_to(w1[0], 1, LORA_PAD), _pad_to(w1[1], 1, LORA_PAD)], axis=1)
    a1c = jnp.concatenate([_pad_to(a1[0], 1, LORA_PAD), _pad_to(a1[1], 1, LORA_PAD)], axis=1)
    if vl is None:
        v1p = jnp.zeros((d, lt), F32)
        v2p = jnp.zeros((LORA_PAD, d), F32)
        v0 = jnp.zeros((d,), F32)
    else:
        v0, v1, v2 = vl
        v1p = _pad_to(v1, 1, lt)
        v2p = _pad_to(v2, 0, LORA_PAD)
    wl1 = jnp.stack([_pad_to(w1c, 1, lt), _pad_to(a1c, 1, lt), g1, v1p]).astype(BF16)
    tl = _bmm(xm, wl1, (1, 4, 5, 3), ("tanh", "none", "sigmoid", "none"), BF16, geo.tm)
    w2p = jnp.stack([_pad_to(w2[0], 0, LORA_PAD), _pad_to(w2[1], 0, LORA_PAD)]).astype(BF16)
    a2p = jnp.stack([_pad_to(a2[0], 0, LORA_PAD), _pad_to(a2[1], 0, LORA_PAD)]).astype(BF16)
    pvec = jnp.stack([w0[0], w0[1], a0[0], a0[1], v0, k_k, k_a, jnp.zeros_like(k_k)])
    lw, kd, asg, kk, g, v = _rwkv_feat(geo, rkv, tl, w2p, a2p, g2.astype(BF16), v2p.astype(BF16),
                                      pvec, vfirst)
    r = rkv[0]
    wkv = _wkv_bidir(r, v, kk, lw, kd, asg, geo.b, geo.l)
    pv2 = _pad_to(jnp.stack([r_k, ln_w, ln_b]), 0, SUBLANES)
    y = _rwkv_readout(geo, wkv, r, kd, v, g, pv2)
    xs = _matmul_res(geo, y, w_o.astype(BF16), xs, mods, 2)
    return xs, v


def _rope_tables(geo):
    t = geo.t
    pos = jnp.arange(t)
    row = (pos // GRID_W).astype(F32)
    col = (pos % GRID_W).astype(F32)
    n_freq = QK_ROPE // 4
    inv_freq = ROPE_THETA ** (-jnp.arange(n_freq, dtype=F32) / n_freq)
    ang = jnp.concatenate([row[:, None] * inv_freq, col[:, None] * inv_freq], axis=-1)
    cos, sin = jnp.cos(ang), jnp.sin(ang)
    pad = LANES - QK_ROPE
    cos_l = jnp.concatenate([cos, cos, jnp.ones((t, pad), F32)], axis=1)
    sin_l = jnp.concatenate([-sin, sin, jnp.zeros((t, pad), F32)], axis=1)
    cos_c = jnp.ones((geo.l, LANES), F32)
    sin_c = jnp.zeros((geo.l, LANES), F32)
    return jnp.concatenate([cos_c, cos_l], axis=0), jnp.concatenate([sin_c, sin_l], axis=0)


def _mla_layer(geo, xs, mods, norm1, rope_t, w_in, q_norm, kv_norm, w_uq, w_ukv, g_q, g_k, w_o):
    q_lora, kv_lora = q_norm.shape[0], kv_norm.shape[0]
    zw = q_lora + kv_lora + LANES
    z = _mod_matmul(geo, xs, norm1, mods, _pad_to(w_in, 1, zw).astype(BF16), 0, False)
    qk = QK_NOPE + QK_ROPE
    wuq_p = _pad_to(w_uq.reshape(q_lora, MLA_HEADS, qk), 2, QK_PAD).reshape(q_lora, -1)
    gq_p = _pad_to(g_q, 0, QK_PAD).reshape(1, QK_PAD)
    gk_p = _pad_to(g_k, 0, QK_PAD).reshape(1, QK_PAD)
    q, k, v = _mla_qkv(geo, z, q_norm, kv_norm, wuq_p.astype(BF16), w_ukv.astype(BF16),
                       gq_p, gk_p, *rope_t)
    o = _mla_attention(geo, q, k, v)
    return _matmul_res(geo, o, w_o.astype(BF16), xs, mods, 2)


def _peer_layer(geo, geo_g, xs, mods, norm2, w_q, q_norm, keys, u, v):
    qp, h2 = _mod_matmul(geo, xs, norm2, mods, w_q.astype(BF16), 1, True)
    table = _pack_expert_table(u, v)
    n, d = xs.shape
    n_sc = geo.tm * round(SC_SHARE * n / geo.tm)
    if n_sc % (2 * SC_CORES * SC_SUBCORES) != 0:
        n_sc = 0
    n_tc = n - n_sc

    def select(row0, nrows):
        wide = 2 * geo.tm
        tsel = wide if row0 % wide == 0 and nrows % wide == 0 else geo.tm
        return _peer_select(qp, q_norm, keys, tsel, row0, nrows)

    if n_sc:
        eidx_sc, gate_sc = select(n_tc, n_sc)
        y_sc = _peer_sc(eidx_sc.T, gate_sc.T, h2, table.reshape(table.shape[0], d), n_tc)
    eidx_t, gate_t = select(0, n_tc)
    out_tc = _peer_gather(geo_g, eidx_t.T, gate_t, h2, xs, mods, table, n_tc)
    if n_sc == 0:
        return out_tc
    g2, r = [], n_tc
    while r < n:
        bi, pos = divmod(r, geo.s)
        seg = int(pos >= geo.l)
        stop = min(n, bi * geo.s + (geo.s if seg else geo.l))
        g2.append(jnp.broadcast_to(mods[bi, seg, 5], (stop - r, d)))
        r = stop
    out_sc = xs[n_tc:] + jnp.concatenate(g2, axis=0) * y_sc
    return jnp.concatenate([out_tc, out_sc], axis=0)


def kernel(x, c, ctx, c_ctx, w_ada, b_ada, norm1, norm2, rw_mix, rw_wrkv, rw_wo, rw_w0, rw_w1, rw_w2, rw_a0, rw_a1, rw_a2, rw_v0, rw_v1, rw_v2, rw_g1, rw_g2, rw_kk, rw_ka, rw_rk, rw_lnw, rw_lnb, mla_win, mla_qnorm, mla_kvnorm, mla_wuq, mla_wukv, mla_gq, mla_gk, mla_wo, peer_wq, peer_qnorm, peer_keys, peer_u, peer_v):
    b, t, d = x.shape
    l = ctx.shape[1]
    depth = w_ada.shape[0]
    geo = _Geom(b, l, t, d, min(256, l))
    geo_g = _Geom(b, l, t, d, min(128, l))
    cond8 = _pad_to(jnp.concatenate([c, c_ctx[None, :]], axis=0), 0, SUBLANES)
    ada = _adaln(cond8, w_ada, b_ada).reshape(depth, SUBLANES, 6, d)
    mods_all = jnp.stack([jnp.broadcast_to(ada[:, b:b + 1], (depth, b, 6, d)), ada[:, 0:b]], axis=2)
    xs = jnp.concatenate([ctx, x], axis=1).reshape(b * (l + t), d)
    rope_t = _rope_tables(geo)
    vfirst = None
    for i in range(depth):
        j = i // 2
        mods = mods_all[i]
        if i % 2 == 0:
            vl = None if j == 0 else (rw_v0[j - 1], rw_v1[j - 1], rw_v2[j - 1])
            xs, vcur = _rwkv_layer(geo, xs, mods, norm1[i], rw_mix[j], rw_wrkv[j], rw_wo[j],
                                   rw_w0[j], rw_w1[j], rw_w2[j], rw_a0[j], rw_a1[j], rw_a2[j], vl,
                                   rw_g1[j], rw_g2[j], rw_kk[j], rw_ka[j], rw_rk[j], rw_lnw[j],
                                   rw_lnb[j], vfirst)
            if j == 0:
                vfirst = vcur
        else:
            xs = _mla_layer(geo, xs, mods, norm1[i], rope_t, mla_win[j], mla_qnorm[j],
                            mla_kvnorm[j], mla_wuq[j], mla_wukv[j], mla_gq[j], mla_gk[j], mla_wo[j])
        if i == depth - 1:
            xs = xs.reshape(b, l + t, d)[:, l:, :].reshape(b * t, d)
            geo, geo_g = _Geom(b, 0, t, d, geo.tm), _Geom(b, 0, t, d, geo_g.tm)
        xs = _peer_layer(geo, geo_g, xs, mods, norm2[i], peer_wq[i], peer_qnorm[i], peer_keys[i],
                         peer_u[i], peer_v[i])
    return xs.reshape(b, t, d)
```

```python
import functools
import math

import jax
import jax.numpy as jnp
from jax import lax
from jax.experimental import pallas as pl
from jax.experimental.pallas import tpu as pltpu
from jax.experimental.pallas import tpu_sc as plsc

F32 = jnp.float32
BF16 = jnp.bfloat16

EPS = 1e-6
GN_EPS = 64e-5
RW_HEAD = 64
WKV_CHUNK = 64
MLA_HEADS = 16
QK_NOPE = 128
QK_ROPE = 64
V_HEAD = 128
QK_PAD = 256
ROPE_THETA = 10000.0
GRID_W = 64
ATTN_SCALE = (QK_NOPE + QK_ROPE) ** -0.5
PEER_HEADS = 8
N_KEYS = 128
PEER_TOPK = 16
D_KEY = 256
LORA_PAD = 128

LANES = 128
SUBLANES = 8
VMEM_LIMIT = 56 * 1024 * 1024


def _cparams(*sem):
    return pltpu.CompilerParams(dimension_semantics=sem, vmem_limit_bytes=VMEM_LIMIT)


def _dot(a, b):
    return jnp.dot(a, b, preferred_element_type=F32)


def _dot_nt(a, b):
    return lax.dot_general(a, b, (((1,), (1,)), ((), ())), preferred_element_type=F32)


def _split2(x):
    hi = x.astype(BF16)
    lo = (x - hi.astype(F32)).astype(BF16)
    return hi, lo


def _split3(x):
    hi = x.astype(BF16)
    r1 = x - hi.astype(F32)
    mid = r1.astype(BF16)
    lo = (r1 - mid.astype(F32)).astype(BF16)
    return hi, mid, lo


def _dot3(a, b):
    ah, al = _split2(a)
    bh, bl = _split2(b)
    return _dot(ah, bh) + (_dot(ah, bl) + _dot(al, bh))


def _dot3_nt(a, b):
    ah, al = _split2(a)
    bh, bl = _split2(b)
    return _dot_nt(ah, bh) + (_dot_nt(ah, bl) + _dot_nt(al, bh))


def _dot_exact_lhs(sel, x):
    hi, mid, lo = _split3(x)
    return _dot(sel, hi) + (_dot(sel, mid) + _dot(sel, lo))


def _modulate(x, g, shift, scale):
    ms = jnp.mean(x * x, axis=-1, keepdims=True)
    return (x * lax.rsqrt(ms + EPS) * g) * (1.0 + scale) + shift


def _sigmoid(x):
    return 1.0 / (1.0 + jnp.exp(-x))


def _softplus(y):
    return jnp.maximum(y, 0.0) + jnp.log(1.0 + jnp.exp(-jnp.abs(y)))


def _erf(x):
    return lax.erf(x)


def _gelu(x):
    return 0.5 * x * (1.0 + _erf(x * (2.0 ** -0.5)))


def _ada_kernel(s_ref, w_ref, b_ref, o_ref):
    s = s_ref[...]
    s = s * _sigmoid(s)
    o_ref[...] = _dot3(s, w_ref[...]) + b_ref[...]


def _adaln(cond8, w_ada, b_ada):
    depth, d, n = w_ada.shape
    tn = 1024
    return pl.pallas_call(
        _ada_kernel,
        grid=(depth, n // tn),
        in_specs=[
            pl.BlockSpec((SUBLANES, d), lambda l, j: (0, 0)),
            pl.BlockSpec((None, d, tn), lambda l, j: (l, 0, j)),
            pl.BlockSpec((None, 1, tn), lambda l, j: (l, 0, j)),
        ],
        out_specs=pl.BlockSpec((None, SUBLANES, tn), lambda l, j: (l, 0, j)),
        out_shape=jax.ShapeDtypeStruct((depth, SUBLANES, n), F32),
        compiler_params=_cparams("parallel", "parallel"),
        name="adaln",
    )(cond8, w_ada, b_ada.reshape(depth, 1, n))


class _Geom:
    def __init__(self, batch, ctx_len, seq_len, d_model, tm):
        self.b, self.l, self.t, self.d = batch, ctx_len, seq_len, d_model
        self.s = ctx_len + seq_len
        self.n = batch * self.s
        self.tm = tm
        assert ctx_len % tm == 0 and seq_len % tm == 0
        self.tpb = self.s // tm
        self.nct = ctx_len // tm
        self.ntiles = self.n // tm

    def mod_spec(self, nlead=0):
        tpb, nct = self.tpb, self.nct

        def imap(*ids):
            i = ids[nlead]
            return (i // tpb, ((i % tpb) >= nct).astype(jnp.int32), 0, 0)

        return pl.BlockSpec((None, None, 6, self.d), imap)


def _modmm_kernel(x_ref, g_ref, mod_ref, w_ref, o_ref, *h_ref, which):
    h = _modulate(x_ref[...], g_ref[...], mod_ref[3 * which:3 * which + 1, :],
                  mod_ref[3 * which + 1:3 * which + 2, :])
    if h_ref:
        h_ref[0][...] = h
    o_ref[...] = _dot(h.astype(BF16), w_ref[...])


def _mod_matmul(geo, x, g, mods, w, which, emit_h):
    n, d = x.shape
    nn = w.shape[1]
    tm = geo.tm
    out_shape = [jax.ShapeDtypeStruct((n, nn), F32)]
    out_specs = [pl.BlockSpec((tm, nn), lambda i: (i, 0))]
    if emit_h:
        out_shape.append(jax.ShapeDtypeStruct((n, d), F32))
        out_specs.append(pl.BlockSpec((tm, d), lambda i: (i, 0)))
    res = pl.pallas_call(
        functools.partial(_modmm_kernel, which=which),
        grid=(geo.ntiles,),
        in_specs=[
            pl.BlockSpec((tm, d), lambda i: (i, 0)),
            pl.BlockSpec((1, d), lambda i: (0, 0)),
            geo.mod_spec(),
            pl.BlockSpec((d, nn), lambda i: (0, 0)),
        ],
        out_specs=out_specs,
        out_shape=out_shape,
        compiler_params=_cparams("parallel"),
        name="mod_matmul",
    )(x, g.reshape(1, d), mods, w)
    return res if emit_h else res[0]


def _mmres_kernel(y_ref, w_ref, x_ref, mod_ref, o_ref, *, gidx):
    acc = _dot(y_ref[...], w_ref[...])
    o_ref[...] = x_ref[...] + mod_ref[gidx:gidx + 1, :] * acc


def _matmul_res(geo, y, w, x, mods, gidx):
    n, k = y.shape
    d = x.shape[1]
    tm = geo.tm
    return pl.pallas_call(
        functools.partial(_mmres_kernel, gidx=gidx),
        grid=(geo.ntiles,),
        in_specs=[
            pl.BlockSpec((tm, k), lambda i: (i, 0)),
            pl.BlockSpec((k, d), lambda i: (0, 0)),
            pl.BlockSpec((tm, d), lambda i: (i, 0)),
            geo.mod_spec(),
        ],
        out_specs=pl.BlockSpec((tm, d), lambda i: (i, 0)),
        out_shape=jax.ShapeDtypeStruct((n, d), F32),
        compiler_params=_cparams("parallel"),
        name="matmul_res",
    )(y, w, x, mods)


def _bmm_kernel(x_ref, w_ref, o_ref, *, acts):
    j = pl.program_id(0)
    y = _dot(x_ref[...], w_ref[...])
    out = y
    for jj, a in enumerate(acts):
        if a == "tanh":
            out = jnp.where(j == jj, jnp.tanh(y), out)
        elif a == "sigmoid":
            out = jnp.where(j == jj, _sigmoid(y), out)
    o_ref[...] = out.astype(o_ref.dtype)


def _bmm(x3, w3, src, acts, out_dtype, tm):
    _, n, k = x3.shape
    nj, _, nn = w3.shape
    src = tuple(src)

    def xmap(j, i):
        idx = jnp.int32(src[0])
        for jj in range(1, nj):
            idx = jnp.where(j == jj, jnp.int32(src[jj]), idx)
        return (idx, i, 0)

    return pl.pallas_call(
        functools.partial(_bmm_kernel, acts=tuple(acts)),
        grid=(nj, n // tm),
        in_specs=[
            pl.BlockSpec((None, tm, k), xmap),
            pl.BlockSpec((None, k, nn), lambda j, i: (j, 0, 0)),
        ],
        out_specs=pl.BlockSpec((None, tm, nn), lambda j, i: (j, i, 0)),
        out_shape=jax.ShapeDtypeStruct((nj, n, nn), out_dtype),
        compiler_params=_cparams("parallel", "parallel"),
        name="bmm",
    )(x3, w3)


def _rwmix_kernel(x_ref, xp_ref, xn_ref, g_ref, mod_ref, mix_ref, o_ref, *, tpb, nct):
    i = pl.program_id(0)
    tm = x_ref.shape[0]
    g = g_ref[...]
    shift = mod_ref[0:1, :]
    scale = mod_ref[1:2, :]
    h = _modulate(x_ref[...], g, shift, scale)
    hp = _modulate(xp_ref[...], g, shift, scale)[SUBLANES - 1:SUBLANES, :]
    hn = _modulate(xn_ref[...], g, shift, scale)[0:1, :]
    it = i % tpb
    first = jnp.logical_or(it == 0, it == nct)
    last = jnp.logical_or(it == nct - 1, it == tpb - 1)
    hp = jnp.where(first, 0.0, hp)
    hn = jnp.where(last, 0.0, hn)
    rows = lax.broadcasted_iota(jnp.int32, h.shape, 0)
    prev = jnp.where(rows == 0, hp, pltpu.roll(h, 1, axis=0))
    nxt = jnp.where(rows == tm - 1, hn, pltpu.roll(h, tm - 1, axis=0))
    xx = 0.5 * (prev + nxt) - h
    for m in range(6):
        o_ref[m] = (h + xx * mix_ref[m:m + 1, :]).astype(BF16)


def _rwkv_mix(geo, x, g, mods, mix):
    n, d = x.shape
    tm = geo.tm
    r8 = tm // SUBLANES
    nblk8 = n // SUBLANES
    return pl.pallas_call(
        functools.partial(_rwmix_kernel, tpb=geo.tpb, nct=geo.nct),
        grid=(geo.ntiles,),
        in_specs=[
            pl.BlockSpec((tm, d), lambda i: (i, 0)),
            pl.BlockSpec((SUBLANES, d), lambda i: (jnp.maximum(i * r8 - 1, 0), 0)),
            pl.BlockSpec((SUBLANES, d), lambda i: (jnp.minimum((i + 1) * r8, nblk8 - 1), 0)),
            pl.BlockSpec((1, d), lambda i: (0, 0)),
            geo.mod_spec(),
            pl.BlockSpec((6, d), lambda i: (0, 0)),
        ],
        out_specs=pl.BlockSpec((6, tm, d), lambda i: (0, i, 0)),
        out_shape=jax.ShapeDtypeStruct((6, n, d), BF16),
        compiler_params=_cparams("parallel"),
        name="rwkv_mix",
    )(x, x, x, g.reshape(1, d), mods, mix)


def _head_sum(x, bd):
    hi, lo = _split2(x)
    return _dot(hi, bd) + _dot(lo, bd)


def _rwfeat_kernel(k_ref, v_ref, tl_ref, w2_ref, a2_ref, g2_ref, v2_ref, pv_ref, bd_ref,
                   *rest, has_vlora):
    if has_vlora:
        vf_ref, lw_ref, kd_ref, as_ref, kk_ref, g_ref, vo_ref = rest
    else:
        lw_ref, kd_ref, as_ref, kk_ref, g_ref = rest
    k = k_ref[...]
    tw = tl_ref[0]
    ta = tl_ref[1]
    tg = tl_ref[2]
    w0 = pv_ref[0:2, :]
    a0 = pv_ref[2:4, :]
    k_k = pv_ref[5:6, :]
    k_a = pv_ref[6:7, :]
    for z in range(2):
        sl = slice(LORA_PAD * z, LORA_PAD * (z + 1))
        lora_w = _dot(tw[:, sl], w2_ref[z])
        w = -_softplus(-(w0[z:z + 1, :] + lora_w)) - 0.5
        lw_ref[z] = -jnp.exp(w)
        a_sig = _sigmoid(a0[z:z + 1, :] + _dot(ta[:, sl], a2_ref[z]))
        as_ref[z] = a_sig.astype(BF16)
        kd_ref[z] = k * (1.0 + (a_sig - 1.0) * k_a)
    g_ref[...] = _dot(tg, g2_ref[...]).astype(BF16)
    kkr = k * k_k
    ss = _head_sum(kkr * kkr, bd_ref[...])
    kk_ref[...] = (kkr * lax.rsqrt(ss + 1e-12)).astype(BF16)
    if has_vlora:
        v = v_ref[...]
        tv = tl_ref[3]
        gate = _sigmoid(pv_ref[4:5, :] + _dot(tv[:, 0:LORA_PAD], v2_ref[...]))
        vo_ref[...] = v + (vf_ref[...] - v) * gate


def _rwkv_feat(geo, rkv, tl, w2p, a2p, g2, v2p, pvec, vfirst):
    _, n, d = rkv.shape
    tm, tc = geo.tm, 512
    has_vlora = vfirst is not None
    lt = tl.shape[2]
    ii = lax.broadcasted_iota(jnp.int32, (tc, tc), 0) // RW_HEAD
    jj = lax.broadcasted_iota(jnp.int32, (tc, tc), 1) // RW_HEAD
    bd = (ii == jj).astype(BF16)
    row = lambda i, j: (i, j)
    in_specs = [
        pl.BlockSpec((None, tm, tc), lambda i, j: (1, i, j)),
        pl.BlockSpec((None, tm, tc), lambda i, j: (2, i, j)),
        pl.BlockSpec((4, tm, lt), lambda i, j: (0, i, 0)),
        pl.BlockSpec((2, LORA_PAD, tc), lambda i, j: (0, 0, j)),
        pl.BlockSpec((2, LORA_PAD, tc), lambda i, j: (0, 0, j)),
        pl.BlockSpec((lt, tc), lambda i, j: (0, j)),
        pl.BlockSpec((LORA_PAD, tc), lambda i, j: (0, j)),
        pl.BlockSpec((SUBLANES, tc), lambda i, j: (0, j)),
        pl.BlockSpec((tc, tc), lambda i, j: (0, 0)),
    ]
    args = [rkv, rkv, tl, w2p, a2p, g2, v2p, pvec, bd]
    dir_spec = pl.BlockSpec((2, tm, tc), lambda i, j: (0, i, j))
    out_specs = [dir_spec, dir_spec, dir_spec, pl.BlockSpec((tm, tc), row), pl.BlockSpec((tm, tc), row)]
    out_shape = [jax.ShapeDtypeStruct((2, n, d), dt) for dt in (F32, F32, BF16)]
    out_shape += [jax.ShapeDtypeStruct((n, d), BF16)] * 2
    if has_vlora:
        in_specs.append(pl.BlockSpec((tm, tc), row))
        args.append(vfirst)
        out_specs.append(pl.BlockSpec((tm, tc), row))
        out_shape.append(jax.ShapeDtypeStruct((n, d), F32))
    res = pl.pallas_call(
        functools.partial(_rwfeat_kernel, has_vlora=has_vlora),
        grid=(geo.ntiles, d // tc),
        in_specs=in_specs,
        out_specs=out_specs,
        out_shape=out_shape,
        compiler_params=_cparams("parallel", "parallel"),
        name="rwkv_feat",
    )(*args)
    if has_vlora:
        lw, kd, asg, kk, g, v = res
    else:
        lw, kd, asg, kk, g = res
        v = rkv[2]
    return lw, kd, asg, kk, g, v


def _wkv_chunk_kernel(r_ref, v_ref, kk_ref, lw_ref, kd_ref, as_ref,
                      m_ref, ga_ref, rq_ref, o0_ref, pc_ref, *, npairs):
    c = WKV_CHUNK
    c2 = 2 * c
    sgn = 1 - 2 * pl.program_id(0)
    ri = lax.broadcasted_iota(jnp.int32, (c2, c2), 0)
    ci = lax.broadcasted_iota(jnp.int32, (c2, c2), 1)
    same = (ri >= c) == (ci >= c)
    tt = jnp.where(ri >= c, ri - c, ri)
    ss = jnp.where(ci >= c, ci - c, ci)
    earlier = (ss - tt) * sgn < 0
    strict = jnp.logical_and(same, earlier)
    incl = jnp.logical_and(same, jnp.logical_or(earlier, ss == tt))
    eye = (ri == ci).astype(F32)
    r64 = lax.broadcasted_iota(jnp.int32, (c, c), 0)
    c64 = lax.broadcasted_iota(jnp.int32, (c, c), 1)
    ltri = jnp.where((c64 - r64) * sgn <= 0, 1.0, 0.0).astype(BF16)
    head0 = lax.broadcasted_iota(jnp.int32, (c, LANES), 1) < RW_HEAD
    pairs = range(npairs)

    def stack(x):
        return jnp.concatenate([jnp.where(head0, x, 0.0), jnp.where(head0, 0.0, x)], axis=0)

    def dup(x):
        return jnp.concatenate([x, x], axis=0)

    def bf(x):
        return x.astype(BF16)

    lhs, rhs, a2, bp2, kp2, vst, r2 = [], [], [], [], [], [], []
    lw_all = lw_ref[...]
    cum_all = _dot_exact_lhs(ltri, lw_all)
    for p in pairs:
        sl = slice(LANES * p, LANES * (p + 1))
        lw = lw_all[:, sl]
        cum = cum_all[:, sl]
        tot = jnp.sum(lw, axis=0, keepdims=True)
        p_inv = jnp.exp(-cum)
        p_end = jnp.exp(tot - cum)
        kk = kk_ref[:, sl]
        b = kk * as_ref[:, sl]
        kd = kd_ref[:, sl]
        a2p = stack(-kk * jnp.exp(cum - lw))
        r2p = stack(r_ref[:, sl] * jnp.exp(cum))
        a2.append(bf(a2p))
        r2.append(r2p)
        lhs.append(jnp.concatenate([a2[p], bf(r2p)], axis=0))
        rhs.append(jnp.concatenate([dup(bf(b * p_inv)), dup(bf(kd * p_inv))], axis=0))
        bp2.append(bf(stack(b * p_end)))
        kp2.append(bf(stack(kd * p_end)))
        vst.append(bf(stack(v_ref[:, sl])))
        pc_ref[:, sl] = jnp.broadcast_to(jnp.exp(tot), (SUBLANES, LANES))
    gram = [_dot_nt(lhs[p], rhs[p]) for p in pairs]
    nab = [jnp.where(strict, gram[p][0:c2, 0:c2], 0.0) for p in pairs]
    nrb = [bf(jnp.where(incl, gram[p][c2:2 * c2, 0:c2], 0.0)) for p in pairs]
    nk = [bf(jnp.concatenate([jnp.where(strict, gram[p][0:c2, c2:2 * c2], 0.0),
                              jnp.where(incl, gram[p][c2:2 * c2, c2:2 * c2], 0.0)], axis=0))
          for p in pairs]
    quads = range(npairs // 2)

    def side(x0, x1):
        return jnp.concatenate([x0, x1], axis=1)

    def diag(x0, x1):
        z0 = jnp.zeros_like(x0)
        return jnp.concatenate([side(x0, z0), side(z0, x1)], axis=0)

    def diag_halves(x):
        return diag(x[:, 0:c2], x[:, c2:2 * c2])

    def unside(xs):
        return [xs[p // 2][:, c2 * (p % 2):c2 * (p % 2 + 1)] for p in pairs]

    xv = unside([_dot(side(nk[2 * q], nk[2 * q + 1]), diag(vst[2 * q], vst[2 * q + 1]))
                 for q in quads])
    tinv = [side(eye + nab[2 * q], eye + nab[2 * q + 1]) for q in quads]
    npow = [bf(side(nab[2 * q], nab[2 * q + 1])) for q in quads]
    for _ in range(int(math.log2(c)) - 1):
        npow = [bf(_dot(npow[q], diag_halves(npow[q]))) for q in quads]
        tinv = [tinv[q] + _dot(bf(tinv[q]), diag_halves(npow[q])) for q in quads]
    tinv = unside(tinv)
    y = [_dot(bf(tinv[p]), jnp.concatenate([a2[p], bf(xv[p][0:c2, :])], axis=1)) for p in pairs]
    yb = [bf(y[p]) for p in pairs]
    z = [_dot(nrb[p], yb[p]) for p in pairs]
    mg = unside([_dot(bf(side(y[2 * q].T, y[2 * q + 1].T)), diag(bp2[2 * q], bp2[2 * q + 1]))
                 for q in quads])
    vk = unside([_dot(bf(side(vst[2 * q].astype(F32).T, vst[2 * q + 1].astype(F32).T)),
                      diag(kp2[2 * q], kp2[2 * q + 1])) for q in quads])
    for p in pairs:
        sl = slice(LANES * p, LANES * (p + 1))
        m_ref[:, sl] = bf(mg[p][0:c2, :])
        ga_ref[:, sl] = mg[p][c2:2 * c2, :] + vk[p]
        rq_ref[:, sl] = bf(r2[p] + z[p][:, 0:c2])
        o0 = z[p][:, c2:2 * c2] + xv[p][c2:2 * c2, :]
        o0_ref[:, sl] = o0[0:c, :] + o0[c:c2, :]


def _wkv_scan_kernel(m_ref, ga_ref, rq_ref, o0_ref, pc_ref, o_ref, g_scr, *, npairs):
    c = WKV_CHUNK

    @pl.when(pl.program_id(2) == 0)
    def _():
        g_scr[...] = jnp.zeros_like(g_scr)

    for p in range(npairs):
        sl = slice(LANES * p, LANES * (p + 1))
        g = g_scr[p]
        g_hi, g_lo = _split2(g)
        rq = rq_ref[:, sl]
        o_st = _dot_nt(rq, g_hi) + _dot_nt(rq, g_lo)
        o_ref[:, sl] = o_st[0:c, :] + o_st[c:2 * c, :] + o0_ref[:, sl]
        m = m_ref[:, sl]
        g_scr[p] = g * pc_ref[0:1, sl] + (_dot(g_hi, m) + _dot(g_lo, m)) + ga_ref[:, sl]


WKV_CHUNK_LANES = 2048


def _wkv_bidir(r, v, kk, lw, kd, asg, batch, ctx_len):
    n, d = r.shape
    c = WKV_CHUNK
    s = n // batch
    ncs = s // c
    ncc = ctx_len // c
    nch = n // c
    lanes = min(WKV_CHUNK_LANES, d)
    ngrp = d // lanes
    shared = pl.BlockSpec((c, lanes), lambda z, i, j: (i, j))
    perdir = pl.BlockSpec((None, c, lanes), lambda z, i, j: (z, i, j))
    big = pl.BlockSpec((None, 2 * c, lanes), lambda z, i, j: (z, i, j))
    m_, ga_, rq_, o0_, pc_ = pl.pallas_call(
        functools.partial(_wkv_chunk_kernel, npairs=lanes // LANES),
        grid=(2, nch, ngrp),
        in_specs=[shared, shared, shared, perdir, perdir, perdir],
        out_specs=[big, big, big, perdir,
                   pl.BlockSpec((None, SUBLANES, lanes), lambda z, i, j: (z, i, j))],
        out_shape=[jax.ShapeDtypeStruct((2, nch * 2 * c, d), dt) for dt in (BF16, F32, BF16)]
        + [jax.ShapeDtypeStruct((2, n, d), F32),
           jax.ShapeDtypeStruct((2, nch * SUBLANES, d), F32)],
        compiler_params=_cparams("parallel", "parallel", "parallel"),
        name="wkv_chunk",
    )(r, v, kk, lw, kd, asg)

    def cmap(z, b, cc):
        back = jnp.where(cc < ncc, ncc - 1 - cc, ncs + ncc - 1 - cc)
        return (z, b * ncs + jnp.where(z == 0, cc, back), 0)

    out = pl.pallas_call(
        functools.partial(_wkv_scan_kernel, npairs=d // LANES),
        grid=(2, batch, ncs),
        in_specs=[
            pl.BlockSpec((None, 2 * c, d), cmap),
            pl.BlockSpec((None, 2 * c, d), cmap),
            pl.BlockSpec((None, 2 * c, d), cmap),
            pl.BlockSpec((None, c, d), cmap),
            pl.BlockSpec((None, SUBLANES, d), cmap),
        ],
        out_specs=pl.BlockSpec((None, c, d), cmap),
        out_shape=jax.ShapeDtypeStruct((2, n, d), F32),
        scratch_shapes=[pltpu.VMEM((d // LANES, 2 * c, 2 * c), F32)],
        compiler_params=_cparams("parallel", "parallel", "arbitrary"),
        name="wkv_scan",
    )(m_, ga_, rq_, o0_, pc_)
    return out


def _rwread_kernel(o_ref, r_ref, kd_ref, v_ref, g_ref, pv_ref, bd_ref, y_ref):
    bd = bd_ref[...]
    wkv = o_ref[0] + o_ref[1]
    inv = 1.0 / RW_HEAD
    mu = _head_sum(wkv, bd) * inv
    dev = wkv - mu
    var = _head_sum(dev * dev, bd) * inv
    y = dev * lax.rsqrt(var + GN_EPS) * pv_ref[1:2, :] + pv_ref[2:3, :]
    rk = r_ref[...] * (kd_ref[0] + kd_ref[1]) * pv_ref[0:1, :]
    y = y + _head_sum(rk, bd) * v_ref[...]
    y_ref[...] = (y * g_ref[...]).astype(BF16)


def _rwkv_readout(geo, wkv, r, kd, v, g, pvec):
    n, d = r.shape
    tm, tc = geo.tm, 512
    ii = lax.broadcasted_iota(jnp.int32, (tc, tc), 0) // RW_HEAD
    jj = lax.broadcasted_iota(jnp.int32, (tc, tc), 1) // RW_HEAD
    bd = (ii == jj).astype(BF16)
    row = pl.BlockSpec((tm, tc), lambda i, j: (i, j))
    dirs = pl.BlockSpec((2, tm, tc), lambda i, j: (0, i, j))
    return pl.pallas_call(
        _rwread_kernel,
        grid=(geo.ntiles, d // tc),
        in_specs=[dirs, row, dirs, row, row,
                  pl.BlockSpec((SUBLANES, tc), lambda i, j: (0, j)),
                  pl.BlockSpec((tc, tc), lambda i, j: (0, 0))],
        out_specs=row,
        out_shape=jax.ShapeDtypeStruct((n, d), BF16),
        compiler_params=_cparams("parallel", "parallel"),
        name="rwkv_readout",
    )(wkv, r, kd, v, g, pvec, bd)


def _mlaqkv_kernel(z_ref, qn_ref, kvn_ref, wuq_ref, wukv_ref, gq_ref, gk_ref, cos_ref, sin_ref,
                   q_ref, k_ref, v_ref, *, q_lora, kv_lora):
    z = z_ref[...]
    cq = z[:, 0:q_lora]
    ckv = z[:, q_lora:q_lora + kv_lora]
    krot = z[:, q_lora + kv_lora:q_lora + kv_lora + LANES]

    def rms(x, g):
        ms = jnp.mean(x * x, axis=-1, keepdims=True)
        return x * lax.rsqrt(ms + EPS) * g

    qf = _dot(rms(cq, qn_ref[...]).astype(BF16), wuq_ref[...])
    kvf = _dot(rms(ckv, kvn_ref[...]).astype(BF16), wukv_ref[...])
    cos = cos_ref[...]
    sin = sin_ref[...]
    half = QK_ROPE // 2
    lane = lax.broadcasted_iota(jnp.int32, cos.shape, 1)

    def rope(x):
        up = pltpu.roll(x, LANES - half, axis=1)
        dn = pltpu.roll(x, half, axis=1)
        return x * cos + jnp.where(lane < half, up, dn) * sin

    inv_w = 1.0 / (QK_NOPE + QK_ROPE)
    gq = gq_ref[...]
    gk = gk_ref[...]
    kr_ss = jnp.sum(krot * krot, axis=-1, keepdims=True)
    for h in range(MLA_HEADS):
        o = QK_PAD * h
        qh = qf[:, o:o + QK_PAD]
        rs = lax.rsqrt(jnp.sum(qh * qh, axis=-1, keepdims=True) * inv_w + EPS)
        qn = qh * rs * gq * (ATTN_SCALE * math.log2(math.e))
        q_ref[:, o:o + QK_NOPE] = qn[:, 0:QK_NOPE].astype(BF16)
        q_ref[:, o + QK_NOPE:o + QK_PAD] = rope(qn[:, QK_NOPE:QK_PAD]).astype(BF16)
        kn = kvf[:, o:o + QK_NOPE]
        rsk = lax.rsqrt((jnp.sum(kn * kn, axis=-1, keepdims=True) + kr_ss) * inv_w + EPS)
        k_ref[:, o:o + QK_NOPE] = (kn * rsk * gk[:, 0:QK_NOPE]).astype(BF16)
        k_ref[:, o + QK_NOPE:o + QK_PAD] = rope(krot * rsk * gk[:, QK_NOPE:QK_PAD]).astype(BF16)
        v_ref[:, V_HEAD * h:V_HEAD * (h + 1)] = kvf[:, o + QK_NOPE:o + QK_PAD].astype(BF16)


def _mla_qkv(geo, z, qn, kvn, wuq_p, wukv, gq_p, gk_p, cos_t, sin_t):
    n, zw = z.shape
    tm = geo.tm
    q_lora, kv_lora = qn.shape[0], kvn.shape[0]
    hq = MLA_HEADS * QK_PAD
    tpb = geo.tpb
    full = lambda i: (0, 0)
    rowmap = lambda i: (i, 0)
    return pl.pallas_call(
        functools.partial(_mlaqkv_kernel, q_lora=q_lora, kv_lora=kv_lora),
        grid=(geo.ntiles,),
        in_specs=[
            pl.BlockSpec((tm, zw), rowmap),
            pl.BlockSpec((1, q_lora), full),
            pl.BlockSpec((1, kv_lora), full),
            pl.BlockSpec((q_lora, hq), full),
            pl.BlockSpec((kv_lora, hq), full),
            pl.BlockSpec((1, QK_PAD), full),
            pl.BlockSpec((1, QK_PAD), full),
            pl.BlockSpec((tm, LANES), lambda i: (i % tpb, 0)),
            pl.BlockSpec((tm, LANES), lambda i: (i % tpb, 0)),
        ],
        out_specs=[pl.BlockSpec((tm, hq), rowmap), pl.BlockSpec((tm, hq), rowmap),
                   pl.BlockSpec((tm, MLA_HEADS * V_HEAD), rowmap)],
        out_shape=[jax.ShapeDtypeStruct((n, hq), BF16), jax.ShapeDtypeStruct((n, hq), BF16),
                   jax.ShapeDtypeStruct((n, MLA_HEADS * V_HEAD), BF16)],
        compiler_params=_cparams("parallel"),
        name="mla_qkv",
    )(z, qn.reshape(1, -1), kvn.reshape(1, -1), wuq_p, wukv, gq_p, gk_p, cos_t, sin_t)


ATTN_SUBTILES = 2


def _attn_kernel(q_ref, k_ref, vt_ref, o_ref):
    k = k_ref[...]
    vt = vt_ref[...]
    cols = q_ref.shape[0] // ATTN_SUBTILES
    subs = range(ATTN_SUBTILES)
    s = [_dot_nt(k, q_ref[cols * i:cols * (i + 1), :]) for i in subs]
    m = [jnp.max(s[i], axis=0, keepdims=True) for i in subs]
    p = [jnp.exp2(s[i] - m[i]) for i in subs]
    l = [jnp.sum(p[i], axis=0, keepdims=True) for i in subs]
    for i in subs:
        ot = _dot(vt, p[i].astype(BF16)) / l[i]
        o_ref[cols * i:cols * (i + 1), :] = ot.T.astype(BF16)


def _attn_call(q3, k3, vt4, tq, nkeys):
    b, sq, _ = q3.shape
    return pl.pallas_call(
        _attn_kernel,
        grid=(b, MLA_HEADS, sq // tq),
        in_specs=[
            pl.BlockSpec((None, tq, QK_PAD), lambda bb, h, i: (bb, i, h)),
            pl.BlockSpec((None, nkeys, QK_PAD), lambda bb, h, i: (bb, 0, h)),
            pl.BlockSpec((None, None, V_HEAD, nkeys), lambda bb, h, i: (bb, h, 0, 0)),
        ],
        out_specs=pl.BlockSpec((None, tq, V_HEAD), lambda bb, h, i: (bb, i, h)),
        out_shape=jax.ShapeDtypeStruct((b, sq, MLA_HEADS * V_HEAD), BF16),
        compiler_params=_cparams("parallel", "parallel", "parallel"),
        name="mla_attention",
    )(q3, k3, vt4)


def _mla_attention(geo, q, k, v):
    b, s, l = geo.b, geo.s, geo.l
    q3 = q.reshape(b, s, MLA_HEADS * QK_PAD)
    k3 = k.reshape(b, s, MLA_HEADS * QK_PAD)
    vt4 = v.reshape(b, s, MLA_HEADS, V_HEAD).transpose(0, 2, 3, 1)
    o_ctx = _attn_call(q3[:, :l], k3, vt4, geo.tm, l)
    tq_lat = 2 * geo.tm if geo.t % (2 * geo.tm) == 0 else geo.tm
    o_lat = _attn_call(q3[:, l:], k3, vt4, tq_lat, s)
    return jnp.concatenate([o_ctx, o_lat], axis=1).reshape(b * s, MLA_HEADS * V_HEAD)


def _topk_rows(s, k, payload=None):
    rows = lax.broadcasted_iota(jnp.int32, s.shape, 0).astype(F32)
    big = float(s.shape[0])
    vals, idxs = [], []
    for _ in range(k):
        m = jnp.max(s, axis=0, keepdims=True)
        idx = jnp.min(jnp.where(s == m, rows, big), axis=0, keepdims=True)
        hit = rows == idx
        vals.append(m)
        if payload is None:
            idxs.append(idx)
        else:
            idxs.append(jnp.sum(jnp.where(hit, payload, 0.0), axis=0, keepdims=True))
        s = jnp.where(hit, -jnp.inf, s)
    return jnp.concatenate(vals, axis=0), jnp.concatenate(idxs, axis=0)


def _peersel_kernel(q_ref, qn_ref, keys_ref, e_ref, g_ref):
    q = q_ref[...]
    ms = jnp.mean(q * q, axis=-1, keepdims=True)
    qn = q * lax.rsqrt(ms + EPS) * qn_ref[...]
    half = D_KEY // 2
    s1 = _dot3_nt(keys_ref[0], qn[:, 0:half])
    s2 = _dot3_nt(keys_ref[1], qn[:, half:D_KEY])
    t1, i1 = _topk_rows(s1, PEER_TOPK)
    t2, i2 = _topk_rows(s2, PEER_TOPK)
    k = PEER_TOPK
    sub = lax.broadcasted_iota(jnp.int32, (SUBLANES, t1.shape[1]), 0)
    cand = [t1[0:1, :] + t2]
    cidx = [i1[0:1, :] * float(N_KEYS) + i2]
    for p in range(1, k // 2):
        live = sub < k // (p + 1)
        cand.append(jnp.where(live, t1[p:p + 1, :] + t2[0:SUBLANES, :], -jnp.inf))
        cidx.append(i1[p:p + 1, :] * float(N_KEYS) + i2[0:SUBLANES, :])
    cand.append(t1[k // 2:k, :] + t2[0:1, :])
    cidx.append(i1[k // 2:k, :] * float(N_KEYS) + i2[0:1, :])
    best, eidx = _topk_rows(jnp.concatenate(cand, axis=0), k, payload=jnp.concatenate(cidx, axis=0))
    ex = jnp.exp(best - jnp.max(best, axis=0, keepdims=True))
    g_ref[...] = ex / jnp.sum(ex, axis=0, keepdims=True)
    e_ref[...] = eidx.astype(jnp.int32)


def _peer_select(qp, q_norm, keys, tm, row0, n):
    assert row0 % tm == 0 and n % tm == 0
    tile0 = row0 // tm
    return pl.pallas_call(
        _peersel_kernel,
        grid=(n // tm, PEER_HEADS),
        in_specs=[
            pl.BlockSpec((tm, D_KEY), lambda i, h: (tile0 + i, h)),
            pl.BlockSpec((1, D_KEY), lambda i, h: (0, 0)),
            pl.BlockSpec((2, N_KEYS, D_KEY // 2), lambda i, h: (0, 0, 0)),
        ],
        out_specs=[pl.BlockSpec((PEER_TOPK, tm), lambda i, h: (h, i)),
                   pl.BlockSpec((PEER_TOPK, tm), lambda i, h: (h, i))],
        out_shape=[jax.ShapeDtypeStruct((PEER_HEADS * PEER_TOPK, n), jnp.int32),
                   jax.ShapeDtypeStruct((PEER_HEADS * PEER_TOPK, n), F32)],
        compiler_params=_cparams("parallel", "parallel"),
        name="peer_select",
    )(qp, q_norm.reshape(1, D_KEY), keys)


GATHER_SLOTS = 4
SLAB_PAD = 1


def _pack_expert_table(u, v):
    ne, d = u.shape
    ub = lax.bitcast_convert_type(u.astype(BF16), jnp.uint16).astype(jnp.uint32)
    vb = lax.bitcast_convert_type(v.astype(BF16), jnp.uint16).astype(jnp.uint32)
    return ((vb << 16) | ub).reshape(ne, d // LANES, LANES)


def _peergather_kernel(idx_ref, idxn_ref, gate_ref, h_ref, x_ref, mod_ref, tab_ref, o_ref,
                       *scratch, tb):
    nsel = PEER_HEADS * PEER_TOPK
    nrow = h_ref.shape[1] // LANES
    pitch = nrow + SLAB_PAD
    ns = GATHER_SLOTS
    bufs, sem = scratch[:ns], scratch[ns]
    lane_t = lax.broadcasted_iota(jnp.int32, (nsel, tb), 1)
    g2 = mod_ref[5:6, :]
    step = pl.program_id(0)
    nsteps = pl.num_programs(0)

    def row_copy(ids_ref, t, j, slot):
        return pltpu.make_async_copy(
            tab_ref.at[ids_ref[t, j]],
            bufs[slot].at[pl.ds(j * pitch, nrow), :],
            sem.at[slot])

    def issue(ids_ref, t, slot):
        for j in range(nsel):
            row_copy(ids_ref, t, j, slot).start(priority=j % 2)

    def wait(t, slot):
        for j in range(nsel):
            row_copy(idx_ref, t, j, slot).wait()

    def packed(slot, s):
        return bufs[slot][pl.ds(s, nsel, stride=pitch), :]

    def compute(t, slot):
        hrow = h_ref[pl.ds(t, 1), :]
        acc = jnp.zeros((nsel, LANES), F32)
        for s in range(nrow):
            u = lax.bitcast_convert_type(packed(slot, s) << 16, F32)
            acc = acc + u * hrow[:, LANES * s:LANES * (s + 1)]
        dots = jnp.sum(acc, axis=-1, keepdims=True)
        gcol = jnp.sum(jnp.where(lane_t == t, gate_ref[...], 0.0), axis=-1, keepdims=True)
        coef = gcol * _gelu(dots)
        outs = []
        for s in range(nrow):
            vv = lax.bitcast_convert_type(packed(slot, s) & jnp.uint32(0xFFFF0000), F32)
            outs.append(jnp.sum(coef * vv, axis=0, keepdims=True))
        orow = jnp.concatenate(outs, axis=1)
        o_ref[pl.ds(t, 1), :] = x_ref[pl.ds(t, 1), :] + g2 * orow

    @pl.when(step == 0)
    def _():
        for s in range(ns - 1):
            issue(idx_ref, s, s)

    ngroups = tb // ns

    def body(g, carry):
        for s in range(ns):
            t = g * ns + s
            wait(t, s)
            issue(idx_ref, t + ns - 1, (s + ns - 1) % ns)
            compute(t, s)
        return carry

    lax.fori_loop(0, ngroups - 1, body, 0)
    for s in range(ns):
        t = (ngroups - 1) * ns + s
        wait(t, s)
        if s == 0:
            issue(idx_ref, tb - 1, ns - 1)
        else:
            @pl.when(step < nsteps - 1)
            def _():
                issue(idxn_ref, s - 1, s - 1)
        compute(t, s)


def _peer_gather(geo_g, eidx, gate_t, h2, x, mods, table, n):
    d = x.shape[1]
    tb = geo_g.tm
    nsel = PEER_HEADS * PEER_TOPK
    pitch = d // LANES + SLAB_PAD
    assert n % tb == 0
    nsteps = n // tb
    return pl.pallas_call(
        functools.partial(_peergather_kernel, tb=tb),
        grid=(nsteps,),
        in_specs=[
            pl.BlockSpec((tb, nsel), lambda i: (i, 0), memory_space=pltpu.SMEM),
            pl.BlockSpec((tb, nsel), lambda i: (jnp.minimum(i + 1, nsteps - 1), 0),
                         memory_space=pltpu.SMEM),
            pl.BlockSpec((nsel, tb), lambda i: (0, i)),
            pl.BlockSpec((tb, d), lambda i: (i, 0)),
            pl.BlockSpec((tb, d), lambda i: (i, 0)),
            geo_g.mod_spec(),
            pl.BlockSpec(memory_space=pl.ANY),
        ],
        out_specs=pl.BlockSpec((tb, d), lambda i: (i, 0)),
        out_shape=jax.ShapeDtypeStruct((n, d), F32),
        scratch_shapes=[pltpu.VMEM((nsel * pitch, LANES), jnp.uint32)] * GATHER_SLOTS
        + [pltpu.SemaphoreType.DMA((GATHER_SLOTS,))],
        compiler_params=_cparams("arbitrary"),
        name="peer_gather",
    )(eidx, eidx, gate_t, h2, x, mods, table)


SC_CORES = 2
SC_SUBCORES = 16
SC_LANES = 16
SC_CHUNK = 16
SC_SHARE = 0.3235

_ERF_ALPHA = (0.00022905065861350646, 0.0034082910107109506, 0.050955695062380861,
              0.18520832239976145, 1.128379143519084)
_ERF_BETA = (-1.1791602954361697e-7, 0.000023547966471313185, 0.0010179625278914885,
             0.014070470171167667, 0.11098505178285362, 0.49746925110067538, 1.0)
_ERF_CLAMP = 3.832506856900711


def _erf_rational(x):
    x = jnp.minimum(jnp.maximum(x, -_ERF_CLAMP), _ERF_CLAMP)
    x2 = x * x
    p = jnp.full_like(x, _ERF_ALPHA[0])
    for c in _ERF_ALPHA[1:]:
        p = p * x2 + c
    q = jnp.full_like(x, _ERF_BETA[0])
    for c in _ERF_BETA[1:]:
        q = q * x2 + c
    return x * p / q


def _peer_sc(eidx, gate, h, table, row0):
    n, nsel = eidx.shape
    d = h.shape[1]
    nw = SC_CORES * SC_SUBCORES
    assert n % (2 * nw) == 0 and nsel % SC_CHUNK == 0 and d % SC_LANES == 0
    tpw = n // nw
    nchunk = nsel // SC_CHUNK
    nvec = d // SC_LANES
    mesh = plsc.VectorSubcoreMesh(core_axis_name="c", subcore_axis_name="s",
                                  num_cores=SC_CORES, num_subcores=SC_SUBCORES)

    @functools.partial(
        pl.kernel, mesh=mesh,
        out_type=jax.ShapeDtypeStruct((n, d), F32),
        scratch_types=[
            pltpu.VMEM((nsel,), jnp.int32), pltpu.VMEM((nsel,), jnp.int32),
            pltpu.VMEM((nsel,), F32), pltpu.VMEM((nsel,), F32),
            pltpu.VMEM((d,), F32), pltpu.VMEM((d,), F32),
            pltpu.VMEM((d,), F32),
            pltpu.VMEM((SC_CHUNK, d), jnp.uint32),
            pltpu.VMEM((SC_CHUNK, d), jnp.uint32),
            pltpu.SemaphoreType.DMA, pltpu.SemaphoreType.DMA,
            pltpu.SemaphoreType.DMA, pltpu.SemaphoreType.DMA,
        ],
        compiler_params=pltpu.CompilerParams(needs_layout_passes=False),
        cost_estimate=pl.CostEstimate(flops=4 * n * nsel * d, transcendentals=0,
                                      bytes_accessed=4 * n * nsel * d),
        name="peer_sc",
    )
    def sc_kernel(eidx_hbm, gate_hbm, h_hbm, tab_hbm, y_hbm, idx_a, idx_b, gate_a, gate_b, h_a, h_b,
                  o_v, rows0, rows1, sem0, sem1, msem_a, msem_b):
        wid = lax.axis_index("s") * SC_CORES + lax.axis_index("c")
        lanes = lax.iota(jnp.int32, SC_LANES)
        zero = jnp.zeros((SC_LANES,), F32)
        bufs = ((rows0, sem0), (rows1, sem1))
        meta = ((idx_a, gate_a, h_a, msem_a), (idx_b, gate_b, h_b, msem_b))

        def meta_copies(tok, s):
            idx_v, gate_v, h_v, msem = meta[s]
            return (pltpu.make_async_copy(eidx_hbm.at[tok], idx_v, msem),
                    pltpu.make_async_copy(gate_hbm.at[tok], gate_v, msem),
                    pltpu.make_async_copy(h_hbm.at[row0 + tok], h_v, msem))

        def gather(s, c, slot):
            rows, sem = bufs[slot]
            return pltpu.make_async_copy(
                tab_hbm.at[meta[s][0].at[pl.ds(c * SC_CHUNK, SC_CHUNK)]], rows, sem)

        def chunk_compute(s, c, rows):
            _, gate_v, h_v, _ = meta[s]

            def dot_body(i, accs):
                hv = h_v[pl.ds(i * SC_LANES, SC_LANES)]
                out = []
                for e in range(SC_CHUNK):
                    w = rows[e, pl.ds(i * SC_LANES, SC_LANES)]
                    out.append(accs[e] + lax.bitcast_convert_type(w << 16, F32) * hv)
                return tuple(out)

            accs = lax.fori_loop(0, nvec, dot_body, (zero,) * SC_CHUNK)
            dots = zero
            for e in range(SC_CHUNK):
                dots = jnp.where(lanes == e, jnp.sum(accs[e]), dots)
            act = 0.5 * dots * (1.0 + _erf_rational(dots * (2.0 ** -0.5)))
            coef = gate_v[pl.ds(c * SC_CHUNK, SC_CHUNK)] * act
            splat = [jnp.full((SC_LANES,), jnp.sum(jnp.where(lanes == e, coef, 0.0)), F32)
                     for e in range(SC_CHUNK)]

            def acc_body(i, carry):
                o = o_v[pl.ds(i * SC_LANES, SC_LANES)]
                for e in range(SC_CHUNK):
                    w = rows[e, pl.ds(i * SC_LANES, SC_LANES)]
                    o = o + splat[e] * lax.bitcast_convert_type(w & jnp.uint32(0xFFFF0000), F32)
                o_v[pl.ds(i * SC_LANES, SC_LANES)] = o
                return carry

            lax.fori_loop(0, nvec, acc_body, 0)

        def zero_body(i, c2):
            o_v[pl.ds(i * SC_LANES, SC_LANES)] = zero
            return c2

        def token(tok, s, has_next):
            def when_next(fn):
                if isinstance(has_next, bool):
                    if has_next:
                        fn()
                else:
                    pl.when(has_next)(fn)

            def load_next():
                for cp in meta_copies(tok + 1, 1 - s):
                    cp.start()

            when_next(load_next)
            lax.fori_loop(0, nvec, zero_body, 0)
            for c in range(nchunk):
                if c + 1 < nchunk:
                    gather(s, c + 1, (c + 1) % 2).start()
                gather(s, c, c % 2).wait()
                chunk_compute(s, c, bufs[c % 2][0])

            def prefetch_next():
                for cp in meta_copies(tok + 1, 1 - s):
                    cp.wait()
                gather(1 - s, 0, 0).start()

            when_next(prefetch_next)
            pltpu.sync_copy(o_v, y_hbm.at[tok])

        base = wid * tpw
        for cp in meta_copies(base, 0):
            cp.start()
        for cp in meta_copies(base, 0):
            cp.wait()
        gather(0, 0, 0).start()

        def pair_body(g, carry):
            token(base + 2 * g, 0, True)
            token(base + 2 * g + 1, 1, g + 1 < tpw // 2)
            return carry

        lax.fori_loop(0, tpw // 2, pair_body, 0)

    return sc_kernel(eidx, gate, h, table)


def _pad_to(x, axis, size):
    pad = [(0, 0)] * x.ndim
    pad[axis] = (0, size - x.shape[axis])
    return jnp.pad(x, pad)


def _rwkv_layer(geo, xs, mods, norm1, mix, w_rkv, w_o, w0, w1, w2, a0, a1, a2, vl, g1, g2,
                k_k, k_a, r_k, ln_w, ln_b, vfirst):
    d = geo.d
    xm = _rwkv_mix(geo, xs, norm1, mods, mix)
    rkv = _bmm(xm, w_rkv.astype(BF16), (0, 2, 3), ("none",) * 3, F32, geo.tm)
    lt = g1.shape[1]
    w1c = jnp.concatenate([_pad_to(w1[0], 1, LORA_PAD), _pad_to(w1[1], 1, LORA_PAD)], axis=1)
    a1c = jnp.concatenate([_pad_to(a1[0], 1, LORA_PAD), _pad_to(a1[1], 1, LORA_PAD)], axis=1)
    if vl is None:
        v1p = jnp.zeros((d, lt), F32)
        v2p = jnp.zeros((LORA_PAD, d), F32)
        v0 = jnp.zeros((d,), F32)
    else:
        v0, v1, v2 = vl
        v1p = _pad_to(v1, 1, lt)
        v2p = _pad_to(v2, 0, LORA_PAD)
    wl1 = jnp.stack([_pad_to(w1c, 1, lt), _pad_to(a1c, 1, lt), g1, v1p]).astype(BF16)
    tl = _bmm(xm, wl1, (1, 4, 5, 3), ("tanh", "none", "sigmoid", "none"), BF16, geo.tm)
    w2p = jnp.stack([_pad_to(w2[0], 0, LORA_PAD), _pad_to(w2[1], 0, LORA_PAD)]).astype(BF16)
    a2p = jnp.stack([_pad_to(a2[0], 0, LORA_PAD), _pad_to(a2[1], 0, LORA_PAD)]).astype(BF16)
    pvec = jnp.stack([w0[0], w0[1], a0[0], a0[1], v0, k_k, k_a, jnp.zeros_like(k_k)])
    lw, kd, asg, kk, g, v = _rwkv_feat(geo, rkv, tl, w2p, a2p, g2.astype(BF16), v2p.astype(BF16),
                                      pvec, vfirst)
    r = rkv[0]
    wkv = _wkv_bidir(r, v, kk, lw, kd, asg, geo.b, geo.l)
    pv2 = _pad_to(jnp.stack([r_k, ln_w, ln_b]), 0, SUBLANES)
    y = _rwkv_readout(geo, wkv, r, kd, v, g, pv2)
    xs = _matmul_res(geo, y, w_o.astype(BF16), xs, mods, 2)
    return xs, v


def _rope_tables(geo):
    t = geo.t
    pos = jnp.arange(t)
    row = (pos // GRID_W).astype(F32)
    col = (pos % GRID_W).astype(F32)
    n_freq = QK_ROPE // 4
    inv_freq = ROPE_THETA ** (-jnp.arange(n_freq, dtype=F32) / n_freq)
    ang = jnp.concatenate([row[:, None] * inv_freq, col[:, None] * inv_freq], axis=-1)
    cos, sin = jnp.cos(ang), jnp.sin(ang)
    pad = LANES - QK_ROPE
    cos_l = jnp.concatenate([cos, cos, jnp.ones((t, pad), F32)], axis=1)
    sin_l = jnp.concatenate([-sin, sin, jnp.zeros((t, pad), F32)], axis=1)
    cos_c = jnp.ones((geo.l, LANES), F32)
    sin_c = jnp.zeros((geo.l, LANES), F32)
    return jnp.concatenate([cos_c, cos_l], axis=0), jnp.concatenate([sin_c, sin_l], axis=0)


def _mla_layer(geo, xs, mods, norm1, rope_t, w_in, q_norm, kv_norm, w_uq, w_ukv, g_q, g_k, w_o):
    q_lora, kv_lora = q_norm.shape[0], kv_norm.shape[0]
    zw = q_lora + kv_lora + LANES
    z = _mod_matmul(geo, xs, norm1, mods, _pad_to(w_in, 1, zw).astype(BF16), 0, False)
    qk = QK_NOPE + QK_ROPE
    wuq_p = _pad_to(w_uq.reshape(q_lora, MLA_HEADS, qk), 2, QK_PAD).reshape(q_lora, -1)
    gq_p = _pad_to(g_q, 0, QK_PAD).reshape(1, QK_PAD)
    gk_p = _pad_to(g_k, 0, QK_PAD).reshape(1, QK_PAD)
    q, k, v = _mla_qkv(geo, z, q_norm, kv_norm, wuq_p.astype(BF16), w_ukv.astype(BF16),
                       gq_p, gk_p, *rope_t)
    o = _mla_attention(geo, q, k, v)
    return _matmul_res(geo, o, w_o.astype(BF16), xs, mods, 2)


def _peer_layer(geo, geo_g, xs, mods, norm2, w_q, q_norm, keys, u, v):
    qp, h2 = _mod_matmul(geo, xs, norm2, mods, w_q.astype(BF16), 1, True)
    table = _pack_expert_table(u, v)
    n, d = xs.shape
    n_sc = geo.tm * round(SC_SHARE * n / geo.tm)
    if n_sc % (2 * SC_CORES * SC_SUBCORES) != 0:
        n_sc = 0
    n_tc = n - n_sc

    def select(row0, nrows):
        wide = 2 * geo.tm
        tsel = wide if row0 % wide == 0 and nrows % wide == 0 else geo.tm
        return _peer_select(qp, q_norm, keys, tsel, row0, nrows)

    if n_sc:
        eidx_sc, gate_sc = select(n_tc, n_sc)
        y_sc = _peer_sc(eidx_sc.T, gate_sc.T, h2, table.reshape(table.shape[0], d), n_tc)
    eidx_t, gate_t = select(0, n_tc)
    out_tc = _peer_gather(geo_g, eidx_t.T, gate_t, h2, xs, mods, table, n_tc)
    if n_sc == 0:
        return out_tc
    g2, r = [], n_tc
    while r < n:
        bi, pos = divmod(r, geo.s)
        seg = int(pos >= geo.l)
        stop = min(n, bi * geo.s + (geo.s if seg else geo.l))
        g2.append(jnp.broadcast_to(mods[bi, seg, 5], (stop - r, d)))
        r = stop
    out_sc = xs[n_tc:] + jnp.concatenate(g2, axis=0) * y_sc
    return jnp.concatenate([out_tc, out_sc], axis=0)


def kernel(x, c, ctx, c_ctx, w_ada, b_ada, norm1, norm2, rw_mix, rw_wrkv, rw_wo, rw_w0, rw_w1, rw_w2, rw_a0, rw_a1, rw_a2, rw_v0, rw_v1, rw_v2, rw_g1, rw_g2, rw_kk, rw_ka, rw_rk, rw_lnw, rw_lnb, mla_win, mla_qnorm, mla_kvnorm, mla_wuq, mla_wukv, mla_gq, mla_gk, mla_wo, peer_wq, peer_qnorm, peer_keys, peer_u, peer_v):
    b, t, d = x.shape
    l = ctx.shape[1]
    depth = w_ada.shape[0]
    geo = _Geom(b, l, t, d, min(256, l))
    geo_g = _Geom(b, l, t, d, min(128, l))
    cond8 = _pad_to(jnp.concatenate([c, c_ctx[None, :]], axis=0), 0, SUBLANES)
    ada = _adaln(cond8, w_ada, b_ada).reshape(depth, SUBLANES, 6, d)
    mods_all = jnp.stack([jnp.broadcast_to(ada[:, b:b + 1], (depth, b, 6, d)), ada[:, 0:b]], axis=2)
    xs = jnp.concatenate([ctx, x], axis=1).reshape(b * (l + t), d)
    rope_t = _rope_tables(geo)
    vfirst = None
    for i in range(depth):
        j = i // 2
        mods = mods_all[i]
        if i % 2 == 0:
            vl = None if j == 0 else (rw_v0[j - 1], rw_v1[j - 1], rw_v2[j - 1])
            xs, vcur = _rwkv_layer(geo, xs, mods, norm1[i], rw_mix[j], rw_wrkv[j], rw_wo[j],
                                   rw_w0[j], rw_w1[j], rw_w2[j], rw_a0[j], rw_a1[j], rw_a2[j], vl,
                                   rw_g1[j], rw_g2[j], rw_kk[j], rw_ka[j], rw_rk[j], rw_lnw[j],
                                   rw_lnb[j], vfirst)
            if j == 0:
                vfirst = vcur
        else:
            xs = _mla_layer(geo, xs, mods, norm1[i], rope_t, mla_win[j], mla_qnorm[j],
                            mla_kvnorm[j], mla_wuq[j], mla_wukv[j], mla_gq[j], mla_gk[j], mla_wo[j])
        if i == depth - 1:
            xs = xs.reshape(b, l + t, d)[:, l:, :].reshape(b * t, d)
            geo, geo_g = _Geom(b, 0, t, d, geo.tm), _Geom(b, 0, t, d, geo_g.tm)
        xs = _peer_layer(geo, geo_g, xs, mods, norm2[i], peer_wq[i], peer_qnorm[i], peer_keys[i],
                         peer_u[i], peer_v[i])
    return xs.reshape(b, t, d)
```

```python
import functools
import math

import jax
import jax.numpy as jnp
from jax import lax
from jax.experimental import pallas as pl
from jax.experimental.pallas import tpu as pltpu
from jax.experimental.pallas import tpu_sc as plsc

F32 = jnp.float32
BF16 = jnp.bfloat16

EPS = 1e-6
GN_EPS = 64e-5
RW_HEAD = 64
WKV_CHUNK = 64
MLA_HEADS = 16
QK_NOPE = 128
QK_ROPE = 64
V_HEAD = 128
QK_PAD = 256
ROPE_THETA = 10000.0
GRID_W = 64
ATTN_SCALE = (QK_NOPE + QK_ROPE) ** -0.5
PEER_HEADS = 8
N_KEYS = 128
PEER_TOPK = 16
D_KEY = 256
LORA_PAD = 128

LANES = 128
SUBLANES = 8
VMEM_LIMIT = 56 * 1024 * 1024


def _cparams(*sem):
    return pltpu.CompilerParams(dimension_semantics=sem, vmem_limit_bytes=VMEM_LIMIT)


def _dot(a, b):
    return jnp.dot(a, b, preferred_element_type=F32)


def _dot_nt(a, b):
    return lax.dot_general(a, b, (((1,), (1,)), ((), ())), preferred_element_type=F32)


def _split2(x):
    hi = x.astype(BF16)
    lo = (x - hi.astype(F32)).astype(BF16)
    return hi, lo


def _split3(x):
    hi = x.astype(BF16)
    r1 = x - hi.astype(F32)
    mid = r1.astype(BF16)
    lo = (r1 - mid.astype(F32)).astype(BF16)
    return hi, mid, lo


def _dot3(a, b):
    ah, al = _split2(a)
    bh, bl = _split2(b)
    return _dot(ah, bh) + (_dot(ah, bl) + _dot(al, bh))


def _dot3_nt(a, b):
    ah, al = _split2(a)
    bh, bl = _split2(b)
    return _dot_nt(ah, bh) + (_dot_nt(ah, bl) + _dot_nt(al, bh))


def _dot_exact_lhs(sel, x):
    hi, mid, lo = _split3(x)
    return _dot(sel, hi) + (_dot(sel, mid) + _dot(sel, lo))


def _modulate(x, g, shift, scale):
    ms = jnp.mean(x * x, axis=-1, keepdims=True)
    return (x * lax.rsqrt(ms + EPS) * g) * (1.0 + scale) + shift


def _sigmoid(x):
    return 1.0 / (1.0 + jnp.exp(-x))


def _softplus(y):
    return jnp.maximum(y, 0.0) + jnp.log(1.0 + jnp.exp(-jnp.abs(y)))


def _erf(x):
    return lax.erf(x)


def _gelu(x):
    return 0.5 * x * (1.0 + _erf(x * (2.0 ** -0.5)))


def _ada_kernel(s_ref, w_ref, b_ref, o_ref):
    s = s_ref[...]
    s = s * _sigmoid(s)
    o_ref[...] = _dot3(s, w_ref[...]) + b_ref[...]


def _adaln(cond8, w_ada, b_ada):
    depth, d, n = w_ada.shape
    tn = 1024
    return pl.pallas_call(
        _ada_kernel,
        grid=(depth, n // tn),
        in_specs=[
            pl.BlockSpec((SUBLANES, d), lambda l, j: (0, 0)),
            pl.BlockSpec((None, d, tn), lambda l, j: (l, 0, j)),
            pl.BlockSpec((None, 1, tn), lambda l, j: (l, 0, j)),
        ],
        out_specs=pl.BlockSpec((None, SUBLANES, tn), lambda l, j: (l, 0, j)),
        out_shape=jax.ShapeDtypeStruct((depth, SUBLANES, n), F32),
        compiler_params=_cparams("parallel", "parallel"),
        name="adaln",
    )(cond8, w_ada, b_ada.reshape(depth, 1, n))


class _Geom:
    def __init__(self, batch, ctx_len, seq_len, d_model, tm):
        self.b, self.l, self.t, self.d = batch, ctx_len, seq_len, d_model
        self.s = ctx_len + seq_len
        self.n = batch * self.s
        self.tm = tm
        assert ctx_len % tm == 0 and seq_len % tm == 0
        self.tpb = self.s // tm
        self.nct = ctx_len // tm
        self.ntiles = self.n // tm

    def mod_spec(self, nlead=0):
        tpb, nct = self.tpb, self.nct

        def imap(*ids):
            i = ids[nlead]
            return (i // tpb, ((i % tpb) >= nct).astype(jnp.int32), 0, 0)

        return pl.BlockSpec((None, None, 6, self.d), imap)


def _modmm_kernel(x_ref, g_ref, mod_ref, w_ref, o_ref, *h_ref, which):
    h = _modulate(x_ref[...], g_ref[...], mod_ref[3 * which:3 * which + 1, :],
                  mod_ref[3 * which + 1:3 * which + 2, :])
    if h_ref:
        h_ref[0][...] = h
    o_ref[...] = _dot(h.astype(BF16), w_ref[...])


def _mod_matmul(geo, x, g, mods, w, which, emit_h):
    n, d = x.shape
    nn = w.shape[1]
    tm = geo.tm
    out_shape = [jax.ShapeDtypeStruct((n, nn), F32)]
    out_specs = [pl.BlockSpec((tm, nn), lambda i: (i, 0))]
    if emit_h:
        out_shape.append(jax.ShapeDtypeStruct((n, d), F32))
        out_specs.append(pl.BlockSpec((tm, d), lambda i: (i, 0)))
    res = pl.pallas_call(
        functools.partial(_modmm_kernel, which=which),
        grid=(geo.ntiles,),
        in_specs=[
            pl.BlockSpec((tm, d), lambda i: (i, 0)),
            pl.BlockSpec((1, d), lambda i: (0, 0)),
            geo.mod_spec(),
            pl.BlockSpec((d, nn), lambda i: (0, 0)),
        ],
        out_specs=out_specs,
        out_shape=out_shape,
        compiler_params=_cparams("parallel"),
        name="mod_matmul",
    )(x, g.reshape(1, d), mods, w)
    return res if emit_h else res[0]


def _mmres_kernel(y_ref, w_ref, x_ref, mod_ref, o_ref, *, gidx):
    acc = _dot(y_ref[...], w_ref[...])
    o_ref[...] = x_ref[...] + mod_ref[gidx:gidx + 1, :] * acc


def _matmul_res(geo, y, w, x, mods, gidx):
    n, k = y.shape
    d = x.shape[1]
    tm = geo.tm
    return pl.pallas_call(
        functools.partial(_mmres_kernel, gidx=gidx),
        grid=(geo.ntiles,),
        in_specs=[
            pl.BlockSpec((tm, k), lambda i: (i, 0)),
            pl.BlockSpec((k, d), lambda i: (0, 0)),
            pl.BlockSpec((tm, d), lambda i: (i, 0)),
            geo.mod_spec(),
        ],
        out_specs=pl.BlockSpec((tm, d), lambda i: (i, 0)),
        out_shape=jax.ShapeDtypeStruct((n, d), F32),
        compiler_params=_cparams("parallel"),
        name="matmul_res",
    )(y, w, x, mods)


def _bmm_kernel(x_ref, w_ref, o_ref, *, acts):
    j = pl.program_id(0)
    y = _dot(x_ref[...], w_ref[...])
    out = y
    for jj, a in enumerate(acts):
        if a == "tanh":
            out = jnp.where(j == jj, jnp.tanh(y), out)
        elif a == "sigmoid":
            out = jnp.where(j == jj, _sigmoid(y), out)
    o_ref[...] = out.astype(o_ref.dtype)


def _bmm(x3, w3, src, acts, out_dtype, tm):
    _, n, k = x3.shape
    nj, _, nn = w3.shape
    src = tuple(src)

    def xmap(j, i):
        idx = jnp.int32(src[0])
        for jj in range(1, nj):
            idx = jnp.where(j == jj, jnp.int32(src[jj]), idx)
        return (idx, i, 0)

    return pl.pallas_call(
        functools.partial(_bmm_kernel, acts=tuple(acts)),
        grid=(nj, n // tm),
        in_specs=[
            pl.BlockSpec((None, tm, k), xmap),
            pl.BlockSpec((None, k, nn), lambda j, i: (j, 0, 0)),
        ],
        out_specs=pl.BlockSpec((None, tm, nn), lambda j, i: (j, i, 0)),
        out_shape=jax.ShapeDtypeStruct((nj, n, nn), out_dtype),
        compiler_params=_cparams("parallel", "parallel"),
        name="bmm",
    )(x3, w3)


def _rwmix_kernel(x_ref, xp_ref, xn_ref, g_ref, mod_ref, mix_ref, o_ref, *, tpb, nct):
    i = pl.program_id(0)
    tm = x_ref.shape[0]
    g = g_ref[...]
    shift = mod_ref[0:1, :]
    scale = mod_ref[1:2, :]
    h = _modulate(x_ref[...], g, shift, scale)
    hp = _modulate(xp_ref[...], g, shift, scale)[SUBLANES - 1:SUBLANES, :]
    hn = _modulate(xn_ref[...], g, shift, scale)[0:1, :]
    it = i % tpb
    first = jnp.logical_or(it == 0, it == nct)
    last = jnp.logical_or(it == nct - 1, it == tpb - 1)
    hp = jnp.where(first, 0.0, hp)
    hn = jnp.where(last, 0.0, hn)
    rows = lax.broadcasted_iota(jnp.int32, h.shape, 0)
    prev = jnp.where(rows == 0, hp, pltpu.roll(h, 1, axis=0))
    nxt = jnp.where(rows == tm - 1, hn, pltpu.roll(h, tm - 1, axis=0))
    xx = 0.5 * (prev + nxt) - h
    for m in range(6):
        o_ref[m] = (h + xx * mix_ref[m:m + 1, :]).astype(BF16)


def _rwkv_mix(geo, x, g, mods, mix):
    n, d = x.shape
    tm = geo.tm
    r8 = tm // SUBLANES
    nblk8 = n // SUBLANES
    return pl.pallas_call(
        functools.partial(_rwmix_kernel, tpb=geo.tpb, nct=geo.nct),
        grid=(geo.ntiles,),
        in_specs=[
            pl.BlockSpec((tm, d), lambda i: (i, 0)),
            pl.BlockSpec((SUBLANES, d), lambda i: (jnp.maximum(i * r8 - 1, 0), 0)),
            pl.BlockSpec((SUBLANES, d), lambda i: (jnp.minimum((i + 1) * r8, nblk8 - 1), 0)),
            pl.BlockSpec((1, d), lambda i: (0, 0)),
            geo.mod_spec(),
            pl.BlockSpec((6, d), lambda i: (0, 0)),
        ],
        out_specs=pl.BlockSpec((6, tm, d), lambda i: (0, i, 0)),
        out_shape=jax.ShapeDtypeStruct((6, n, d), BF16),
        compiler_params=_cparams("parallel"),
        name="rwkv_mix",
    )(x, x, x, g.reshape(1, d), mods, mix)


def _head_sum(x, bd):
    hi, lo = _split2(x)
    return _dot(hi, bd) + _dot(lo, bd)


def _rwfeat_kernel(k_ref, v_ref, tl_ref, w2_ref, a2_ref, g2_ref, v2_ref, pv_ref, bd_ref,
                   *rest, has_vlora):
    if has_vlora:
        vf_ref, lw_ref, kd_ref, as_ref, kk_ref, g_ref, vo_ref = rest
    else:
        lw_ref, kd_ref, as_ref, kk_ref, g_ref = rest
    k = k_ref[...]
    tw = tl_ref[0]
    ta = tl_ref[1]
    tg = tl_ref[2]
    w0 = pv_ref[0:2, :]
    a0 = pv_ref[2:4, :]
    k_k = pv_ref[5:6, :]
    k_a = pv_ref[6:7, :]
    for z in range(2):
        sl = slice(LORA_PAD * z, LORA_PAD * (z + 1))
        lora_w = _dot(tw[:, sl], w2_ref[z])
        w = -_softplus(-(w0[z:z + 1, :] + lora_w)) - 0.5
        lw_ref[z] = -jnp.exp(w)
        a_sig = _sigmoid(a0[z:z + 1, :] + _dot(ta[:, sl], a2_ref[z]))
        as_ref[z] = a_sig.astype(BF16)
        kd_ref[z] = k * (1.0 + (a_sig - 1.0) * k_a)
    g_ref[...] = _dot(tg, g2_ref[...]).astype(BF16)
    kkr = k * k_k
    ss = _head_sum(kkr * kkr, bd_ref[...])
    kk_ref[...] = (kkr * lax.rsqrt(ss + 1e-12)).astype(BF16)
    if has_vlora:
        v = v_ref[...]
        tv = tl_ref[3]
        gate = _sigmoid(pv_ref[4:5, :] + _dot(tv[:, 0:LORA_PAD], v2_ref[...]))
        vo_ref[...] = v + (vf_ref[...] - v) * gate


def _rwkv_feat(geo, rkv, tl, w2p, a2p, g2, v2p, pvec, vfirst):
    _, n, d = rkv.shape
    tm, tc = geo.tm, 512
    has_vlora = vfirst is not None
    lt = tl.shape[2]
    ii = lax.broadcasted_iota(jnp.int32, (tc, tc), 0) // RW_HEAD
    jj = lax.broadcasted_iota(jnp.int32, (tc, tc), 1) // RW_HEAD
    bd = (ii == jj).astype(BF16)
    row = lambda i, j: (i, j)
    in_specs = [
        pl.BlockSpec((None, tm, tc), lambda i, j: (1, i, j)),
        pl.BlockSpec((None, tm, tc), lambda i, j: (2, i, j)),
        pl.BlockSpec((4, tm, lt), lambda i, j: (0, i, 0)),
        pl.BlockSpec((2, LORA_PAD, tc), lambda i, j: (0, 0, j)),
        pl.BlockSpec((2, LORA_PAD, tc), lambda i, j: (0, 0, j)),
        pl.BlockSpec((lt, tc), lambda i, j: (0, j)),
        pl.BlockSpec((LORA_PAD, tc), lambda i, j: (0, j)),
        pl.BlockSpec((SUBLANES, tc), lambda i, j: (0, j)),
        pl.BlockSpec((tc, tc), lambda i, j: (0, 0)),
    ]
    args = [rkv, rkv, tl, w2p, a2p, g2, v2p, pvec, bd]
    dir_spec = pl.BlockSpec((2, tm, tc), lambda i, j: (0, i, j))
    out_specs = [dir_spec, dir_spec, dir_spec, pl.BlockSpec((tm, tc), row), pl.BlockSpec((tm, tc), row)]
    out_shape = [jax.ShapeDtypeStruct((2, n, d), dt) for dt in (F32, F32, BF16)]
    out_shape += [jax.ShapeDtypeStruct((n, d), BF16)] * 2
    if has_vlora:
        in_specs.append(pl.BlockSpec((tm, tc), row))
        args.append(vfirst)
        out_specs.append(pl.BlockSpec((tm, tc), row))
        out_shape.append(jax.ShapeDtypeStruct((n, d), F32))
    res = pl.pallas_call(
        functools.partial(_rwfeat_kernel, has_vlora=has_vlora),
        grid=(geo.ntiles, d // tc),
        in_specs=in_specs,
        out_specs=out_specs,
        out_shape=out_shape,
        compiler_params=_cparams("parallel", "parallel"),
        name="rwkv_feat",
    )(*args)
    if has_vlora:
        lw, kd, asg, kk, g, v = res
    else:
        lw, kd, asg, kk, g = res
        v = rkv[2]
    return lw, kd, asg, kk, g, v


def _wkv_chunk_kernel(r_ref, v_ref, kk_ref, lw_ref, kd_ref, as_ref,
                      m_ref, ga_ref, rq_ref, o0_ref, pc_ref, *, npairs):
    c = WKV_CHUNK
    c2 = 2 * c
    sgn = 1 - 2 * pl.program_id(0)
    ri = lax.broadcasted_iota(jnp.int32, (c2, c2), 0)
    ci = lax.broadcasted_iota(jnp.int32, (c2, c2), 1)
    same = (ri >= c) == (ci >= c)
    tt = jnp.where(ri >= c, ri - c, ri)
    ss = jnp.where(ci >= c, ci - c, ci)
    earlier = (ss - tt) * sgn < 0
    strict = jnp.logical_and(same, earlier)
    incl = jnp.logical_and(same, jnp.logical_or(earlier, ss == tt))
    eye = (ri == ci).astype(F32)
    r64 = lax.broadcasted_iota(jnp.int32, (c, c), 0)
    c64 = lax.broadcasted_iota(jnp.int32, (c, c), 1)
    ltri = jnp.where((c64 - r64) * sgn <= 0, 1.0, 0.0).astype(BF16)
    head0 = lax.broadcasted_iota(jnp.int32, (c, LANES), 1) < RW_HEAD
    pairs = range(npairs)

    def stack(x):
        return jnp.concatenate([jnp.where(head0, x, 0.0), jnp.where(head0, 0.0, x)], axis=0)

    def dup(x):
        return jnp.concatenate([x, x], axis=0)

    def bf(x):
        return x.astype(BF16)

    lhs, rhs, a2, bp2, kp2, vst, r2 = [], [], [], [], [], [], []
    lw_all = lw_ref[...]
    cum_all = _dot_exact_lhs(ltri, lw_all)
    for p in pairs:
        sl = slice(LANES * p, LANES * (p + 1))
        lw = lw_all[:, sl]
        cum = cum_all[:, sl]
        tot = jnp.sum(lw, axis=0, keepdims=True)
        p_inv = jnp.exp(-cum)
        p_end = jnp.exp(tot - cum)
        kk = kk_ref[:, sl]
        b = kk * as_ref[:, sl]
        kd = kd_ref[:, sl]
        a2p = stack(-kk * jnp.exp(cum - lw))
        r2p = stack(r_ref[:, sl] * jnp.exp(cum))
        a2.append(bf(a2p))
        r2.append(r2p)
        lhs.append(jnp.concatenate([a2[p], bf(r2p)], axis=0))
        rhs.append(jnp.concatenate([dup(bf(b * p_inv)), dup(bf(kd * p_inv))], axis=0))
        bp2.append(bf(stack(b * p_end)))
        kp2.append(bf(stack(kd * p_end)))
        vst.append(bf(stack(v_ref[:, sl])))
        pc_ref[:, sl] = jnp.broadcast_to(jnp.exp(tot), (SUBLANES, LANES))
    gram = [_dot_nt(lhs[p], rhs[p]) for p in pairs]
    nab = [jnp.where(strict, gram[p][0:c2, 0:c2], 0.0) for p in pairs]
    nrb = [bf(jnp.where(incl, gram[p][c2:2 * c2, 0:c2], 0.0)) for p in pairs]
    nk = [bf(jnp.concatenate([jnp.where(strict, gram[p][0:c2, c2:2 * c2], 0.0),
                              jnp.where(incl, gram[p][c2:2 * c2, c2:2 * c2], 0.0)], axis=0))
          for p in pairs]
    quads = range(npairs // 2)

    def side(x0, x1):
        return jnp.concatenate([x0, x1], axis=1)

    def diag(x0, x1):
        z0 = jnp.zeros_like(x0)
        return jnp.concatenate([side(x0, z0), side(z0, x1)], axis=0)

    def diag_halves(x):
        return diag(x[:, 0:c2], x[:, c2:2 * c2])

    def unside(xs):
        return [xs[p // 2][:, c2 * (p % 2):c2 * (p % 2 + 1)] for p in pairs]

    xv = unside([_dot(side(nk[2 * q], nk[2 * q + 1]), diag(vst[2 * q], vst[2 * q + 1]))
                 for q in quads])
    tinv = [side(eye + nab[2 * q], eye + nab[2 * q + 1]) for q in quads]
    npow = [bf(side(nab[2 * q], nab[2 * q + 1])) for q in quads]
    for _ in range(int(math.log2(c)) - 1):
        npow = [bf(_dot(npow[q], diag_halves(npow[q]))) for q in quads]
        tinv = [tinv[q] + _dot(bf(tinv[q]), diag_halves(npow[q])) for q in quads]
    tinv = unside(tinv)
    y = [_dot(bf(tinv[p]), jnp.concatenate([a2[p], bf(xv[p][0:c2, :])], axis=1)) for p in pairs]
    yb = [bf(y[p]) for p in pairs]
    z = [_dot(nrb[p], yb[p]) for p in pairs]
    mg = unside([_dot(bf(side(y[2 * q].T, y[2 * q + 1].T)), diag(bp2[2 * q], bp2[2 * q + 1]))
                 for q in quads])
    vk = unside([_dot(bf(side(vst[2 * q].astype(F32).T, vst[2 * q + 1].astype(F32).T)),
                      diag(kp2[2 * q], kp2[2 * q + 1])) for q in quads])
    for p in pairs:
        sl = slice(LANES * p, LANES * (p + 1))
        m_ref[:, sl] = bf(mg[p][0:c2, :])
        ga_ref[:, sl] = mg[p][c2:2 * c2, :] + vk[p]
        rq_ref[:, sl] = bf(r2[p] + z[p][:, 0:c2])
        o0 = z[p][:, c2:2 * c2] + xv[p][c2:2 * c2, :]
        o0_ref[:, sl] = o0[0:c, :] + o0[c:c2, :]


def _wkv_scan_kernel(m_ref, ga_ref, rq_ref, o0_ref, pc_ref, o_ref, g_scr, *, npairs):
    c = WKV_CHUNK

    @pl.when(pl.program_id(2) == 0)
    def _():
        g_scr[...] = jnp.zeros_like(g_scr)

    for p in range(npairs):
        sl = slice(LANES * p, LANES * (p + 1))
        g = g_scr[p]
        g_hi, g_lo = _split2(g)
        rq = rq_ref[:, sl]
        o_st = _dot_nt(rq, g_hi) + _dot_nt(rq, g_lo)
        o_ref[:, sl] = o_st[0:c, :] + o_st[c:2 * c, :] + o0_ref[:, sl]
        m = m_ref[:, sl]
        g_scr[p] = g * pc_ref[0:1, sl] + (_dot(g_hi, m) + _dot(g_lo, m)) + ga_ref[:, sl]


WKV_CHUNK_LANES = 2048


def _wkv_bidir(r, v, kk, lw, kd, asg, batch, ctx_len):
    n, d = r.shape
    c = WKV_CHUNK
    s = n // batch
    ncs = s // c
    ncc = ctx_len // c
    nch = n // c
    lanes = min(WKV_CHUNK_LANES, d)
    ngrp = d // lanes
    shared = pl.BlockSpec((c, lanes), lambda z, i, j: (i, j))
    perdir = pl.BlockSpec((None, c, lanes), lambda z, i, j: (z, i, j))
    big = pl.BlockSpec((None, 2 * c, lanes), lambda z, i, j: (z, i, j))
    m_, ga_, rq_, o0_, pc_ = pl.pallas_call(
        functools.partial(_wkv_chunk_kernel, npairs=lanes // LANES),
        grid=(2, nch, ngrp),
        in_specs=[shared, shared, shared, perdir, perdir, perdir],
        out_specs=[big, big, big, perdir,
                   pl.BlockSpec((None, SUBLANES, lanes), lambda z, i, j: (z, i, j))],
        out_shape=[jax.ShapeDtypeStruct((2, nch * 2 * c, d), dt) for dt in (BF16, F32, BF16)]
        + [jax.ShapeDtypeStruct((2, n, d), F32),
           jax.ShapeDtypeStruct((2, nch * SUBLANES, d), F32)],
        compiler_params=_cparams("parallel", "parallel", "parallel"),
        name="wkv_chunk",
    )(r, v, kk, lw, kd, asg)

    def cmap(z, b, cc):
        back = jnp.where(cc < ncc, ncc - 1 - cc, ncs + ncc - 1 - cc)
        return (z, b * ncs + jnp.where(z == 0, cc, back), 0)

    out = pl.pallas_call(
        functools.partial(_wkv_scan_kernel, npairs=d // LANES),
        grid=(2, batch, ncs),
        in_specs=[
            pl.BlockSpec((None, 2 * c, d), cmap),
            pl.BlockSpec((None, 2 * c, d), cmap),
            pl.BlockSpec((None, 2 * c, d), cmap),
            pl.BlockSpec((None, c, d), cmap),
            pl.BlockSpec((None, SUBLANES, d), cmap),
        ],
        out_specs=pl.BlockSpec((None, c, d), cmap),
        out_shape=jax.ShapeDtypeStruct((2, n, d), F32),
        scratch_shapes=[pltpu.VMEM((d // LANES, 2 * c, 2 * c), F32)],
        compiler_params=_cparams("parallel", "parallel", "arbitrary"),
        name="wkv_scan",
    )(m_, ga_, rq_, o0_, pc_)
    return out


def _rwread_kernel(o_ref, r_ref, kd_ref, v_ref, g_ref, pv_ref, bd_ref, y_ref):
    bd = bd_ref[...]
    wkv = o_ref[0] + o_ref[1]
    inv = 1.0 / RW_HEAD
    mu = _head_sum(wkv, bd) * inv
    dev = wkv - mu
    var = _head_sum(dev * dev, bd) * inv
    y = dev * lax.rsqrt(var + GN_EPS) * pv_ref[1:2, :] + pv_ref[2:3, :]
    rk = r_ref[...] * (kd_ref[0] + kd_ref[1]) * pv_ref[0:1, :]
    y = y + _head_sum(rk, bd) * v_ref[...]
    y_ref[...] = (y * g_ref[...]).astype(BF16)


def _rwkv_readout(geo, wkv, r, kd, v, g, pvec):
    n, d = r.shape
    tm, tc = geo.tm, 512
    ii = lax.broadcasted_iota(jnp.int32, (tc, tc), 0) // RW_HEAD
    jj = lax.broadcasted_iota(jnp.int32, (tc, tc), 1) // RW_HEAD
    bd = (ii == jj).astype(BF16)
    row = pl.BlockSpec((tm, tc), lambda i, j: (i, j))
    dirs = pl.BlockSpec((2, tm, tc), lambda i, j: (0, i, j))
    return pl.pallas_call(
        _rwread_kernel,
        grid=(geo.ntiles, d // tc),
        in_specs=[dirs, row, dirs, row, row,
                  pl.BlockSpec((SUBLANES, tc), lambda i, j: (0, j)),
                  pl.BlockSpec((tc, tc), lambda i, j: (0, 0))],
        out_specs=row,
        out_shape=jax.ShapeDtypeStruct((n, d), BF16),
        compiler_params=_cparams("parallel", "parallel"),
        name="rwkv_readout",
    )(wkv, r, kd, v, g, pvec, bd)


def _mlaqkv_kernel(z_ref, qn_ref, kvn_ref, wuq_ref, wukv_ref, gq_ref, gk_ref, cos_ref, sin_ref,
                   q_ref, k_ref, v_ref, *, q_lora, kv_lora):
    z = z_ref[...]
    cq = z[:, 0:q_lora]
    ckv = z[:, q_lora:q_lora + kv_lora]
    krot = z[:, q_lora + kv_lora:q_lora + kv_lora + LANES]

    def rms(x, g):
        ms = jnp.mean(x * x, axis=-1, keepdims=True)
        return x * lax.rsqrt(ms + EPS) * g

    qf = _dot(rms(cq, qn_ref[...]).astype(BF16), wuq_ref[...])
    kvf = _dot(rms(ckv, kvn_ref[...]).astype(BF16), wukv_ref[...])
    cos = cos_ref[...]
    sin = sin_ref[...]
    half = QK_ROPE // 2
    lane = lax.broadcasted_iota(jnp.int32, cos.shape, 1)

    def rope(x):
        up = pltpu.roll(x, LANES - half, axis=1)
        dn = pltpu.roll(x, half, axis=1)
        return x * cos + jnp.where(lane < half, up, dn) * sin

    inv_w = 1.0 / (QK_NOPE + QK_ROPE)
    gq = gq_ref[...]
    gk = gk_ref[...]
    kr_ss = jnp.sum(krot * krot, axis=-1, keepdims=True)
    for h in range(MLA_HEADS):
        o = QK_PAD * h
        qh = qf[:, o:o + QK_PAD]
        rs = lax.rsqrt(jnp.sum(qh * qh, axis=-1, keepdims=True) * inv_w + EPS)
        qn = qh * rs * gq * (ATTN_SCALE * math.log2(math.e))
        q_ref[:, o:o + QK_NOPE] = qn[:, 0:QK_NOPE].astype(BF16)
        q_ref[:, o + QK_NOPE:o + QK_PAD] = rope(qn[:, QK_NOPE:QK_PAD]).astype(BF16)
        kn = kvf[:, o:o + QK_NOPE]
        rsk = lax.rsqrt((jnp.sum(kn * kn, axis=-1, keepdims=True) + kr_ss) * inv_w + EPS)
        k_ref[:, o:o + QK_NOPE] = (kn * rsk * gk[:, 0:QK_NOPE]).astype(BF16)
        k_ref[:, o + QK_NOPE:o + QK_PAD] = rope(krot * rsk * gk[:, QK_NOPE:QK_PAD]).astype(BF16)
        v_ref[:, V_HEAD * h:V_HEAD * (h + 1)] = kvf[:, o + QK_NOPE:o + QK_PAD].astype(BF16)


def _mla_qkv(geo, z, qn, kvn, wuq_p, wukv, gq_p, gk_p, cos_t, sin_t):
    n, zw = z.shape
    tm = geo.tm
    q_lora, kv_lora = qn.shape[0], kvn.shape[0]
    hq = MLA_HEADS * QK_PAD
    tpb = geo.tpb
    full = lambda i: (0, 0)
    rowmap = lambda i: (i, 0)
    return pl.pallas_call(
        functools.partial(_mlaqkv_kernel, q_lora=q_lora, kv_lora=kv_lora),
        grid=(geo.ntiles,),
        in_specs=[
            pl.BlockSpec((tm, zw), rowmap),
            pl.BlockSpec((1, q_lora), full),
            pl.BlockSpec((1, kv_lora), full),
            pl.BlockSpec((q_lora, hq), full),
            pl.BlockSpec((kv_lora, hq), full),
            pl.BlockSpec((1, QK_PAD), full),
            pl.BlockSpec((1, QK_PAD), full),
            pl.BlockSpec((tm, LANES), lambda i: (i % tpb, 0)),
            pl.BlockSpec((tm, LANES), lambda i: (i % tpb, 0)),
        ],
        out_specs=[pl.BlockSpec((tm, hq), rowmap), pl.BlockSpec((tm, hq), rowmap),
                   pl.BlockSpec((tm, MLA_HEADS * V_HEAD), rowmap)],
        out_shape=[jax.ShapeDtypeStruct((n, hq), BF16), jax.ShapeDtypeStruct((n, hq), BF16),
                   jax.ShapeDtypeStruct((n, MLA_HEADS * V_HEAD), BF16)],
        compiler_params=_cparams("parallel"),
        name="mla_qkv",
    )(z, qn.reshape(1, -1), kvn.reshape(1, -1), wuq_p, wukv, gq_p, gk_p, cos_t, sin_t)


ATTN_SUBTILES = 2


def _attn_kernel(q_ref, k_ref, vt_ref, o_ref):
    k = k_ref[...]
    vt = vt_ref[...]
    cols = q_ref.shape[0] // ATTN_SUBTILES
    subs = range(ATTN_SUBTILES)
    s = [_dot_nt(k, q_ref[cols * i:cols * (i + 1), :]) for i in subs]
    m = [jnp.max(s[i], axis=0, keepdims=True) for i in subs]
    p = [jnp.exp2(s[i] - m[i]) for i in subs]
    l = [jnp.sum(p[i], axis=0, keepdims=True) for i in subs]
    for i in subs:
        ot = _dot(vt, p[i].astype(BF16)) / l[i]
        o_ref[cols * i:cols * (i + 1), :] = ot.T.astype(BF16)


def _attn_call(q3, k3, vt4, tq, nkeys):
    b, sq, _ = q3.shape
    return pl.pallas_call(
        _attn_kernel,
        grid=(b, MLA_HEADS, sq // tq),
        in_specs=[
            pl.BlockSpec((None, tq, QK_PAD), lambda bb, h, i: (bb, i, h)),
            pl.BlockSpec((None, nkeys, QK_PAD), lambda bb, h, i: (bb, 0, h)),
            pl.BlockSpec((None, None, V_HEAD, nkeys), lambda bb, h, i: (bb, h, 0, 0)),
        ],
        out_specs=pl.BlockSpec((None, tq, V_HEAD), lambda bb, h, i: (bb, i, h)),
        out_shape=jax.ShapeDtypeStruct((b, sq, MLA_HEADS * V_HEAD), BF16),
        compiler_params=_cparams("parallel", "parallel", "parallel"),
        name="mla_attention",
    )(q3, k3, vt4)


def _mla_attention(geo, q, k, v):
    b, s, l = geo.b, geo.s, geo.l
    q3 = q.reshape(b, s, MLA_HEADS * QK_PAD)
    k3 = k.reshape(b, s, MLA_HEADS * QK_PAD)
    vt4 = v.reshape(b, s, MLA_HEADS, V_HEAD).transpose(0, 2, 3, 1)
    o_ctx = _attn_call(q3[:, :l], k3, vt4, geo.tm, l)
    tq_lat = 2 * geo.tm if geo.t % (2 * geo.tm) == 0 else geo.tm
    o_lat = _attn_call(q3[:, l:], k3, vt4, tq_lat, s)
    return jnp.concatenate([o_ctx, o_lat], axis=1).reshape(b * s, MLA_HEADS * V_HEAD)


def _topk_rows(s, k, payload=None):
    rows = lax.broadcasted_iota(jnp.int32, s.shape, 0).astype(F32)
    big = float(s.shape[0])
    vals, idxs = [], []
    for _ in range(k):
        m = jnp.max(s, axis=0, keepdims=True)
        idx = jnp.min(jnp.where(s == m, rows, big), axis=0, keepdims=True)
        hit = rows == idx
        vals.append(m)
        if payload is None:
            idxs.append(idx)
        else:
            idxs.append(jnp.sum(jnp.where(hit, payload, 0.0), axis=0, keepdims=True))
        s = jnp.where(hit, -jnp.inf, s)
    return jnp.concatenate(vals, axis=0), jnp.concatenate(idxs, axis=0)


def _peersel_kernel(q_ref, qn_ref, keys_ref, e_ref, g_ref):
    q = q_ref[...]
    ms = jnp.mean(q * q, axis=-1, keepdims=True)
    qn = q * lax.rsqrt(ms + EPS) * qn_ref[...]
    half = D_KEY // 2
    s1 = _dot3_nt(keys_ref[0], qn[:, 0:half])
    s2 = _dot3_nt(keys_ref[1], qn[:, half:D_KEY])
    t1, i1 = _topk_rows(s1, PEER_TOPK)
    t2, i2 = _topk_rows(s2, PEER_TOPK)
    k = PEER_TOPK
    sub = lax.broadcasted_iota(jnp.int32, (SUBLANES, t1.shape[1]), 0)
    cand = [t1[0:1, :] + t2]
    cidx = [i1[0:1, :] * float(N_KEYS) + i2]
    for p in range(1, k // 2):
        live = sub < k // (p + 1)
        cand.append(jnp.where(live, t1[p:p + 1, :] + t2[0:SUBLANES, :], -jnp.inf))
        cidx.append(i1[p:p + 1, :] * float(N_KEYS) + i2[0:SUBLANES, :])
    cand.append(t1[k // 2:k, :] + t2[0:1, :])
    cidx.append(i1[k // 2:k, :] * float(N_KEYS) + i2[0:1, :])
    best, eidx = _topk_rows(jnp.concatenate(cand, axis=0), k, payload=jnp.concatenate(cidx, axis=0))
    ex = jnp.exp(best - jnp.max(best, axis=0, keepdims=True))
    g_ref[...] = ex / jnp.sum(ex, axis=0, keepdims=True)
    e_ref[...] = eidx.astype(jnp.int32)


def _peer_select(qp, q_norm, keys, tm, row0, n):
    assert row0 % tm == 0 and n % tm == 0
    tile0 = row0 // tm
    return pl.pallas_call(
        _peersel_kernel,
        grid=(n // tm, PEER_HEADS),
        in_specs=[
            pl.BlockSpec((tm, D_KEY), lambda i, h: (tile0 + i, h)),
            pl.BlockSpec((1, D_KEY), lambda i, h: (0, 0)),
            pl.BlockSpec((2, N_KEYS, D_KEY // 2), lambda i, h: (0, 0, 0)),
        ],
        out_specs=[pl.BlockSpec((PEER_TOPK, tm), lambda i, h: (h, i)),
                   pl.BlockSpec((PEER_TOPK, tm), lambda i, h: (h, i))],
        out_shape=[jax.ShapeDtypeStruct((PEER_HEADS * PEER_TOPK, n), jnp.int32),
                   jax.ShapeDtypeStruct((PEER_HEADS * PEER_TOPK, n), F32)],
        compiler_params=_cparams("parallel", "parallel"),
        name="peer_select",
    )(qp, q_norm.reshape(1, D_KEY), keys)


GATHER_SLOTS = 4
SLAB_PAD = 1


def _pack_expert_table(u, v):
    ne, d = u.shape
    ub = lax.bitcast_convert_type(u.astype(BF16), jnp.uint16).astype(jnp.uint32)
    vb = lax.bitcast_convert_type(v.astype(BF16), jnp.uint16).astype(jnp.uint32)
    return ((vb << 16) | ub).reshape(ne, d // LANES, LANES)


def _peergather_kernel(idx_ref, idxn_ref, gate_ref, h_ref, x_ref, mod_ref, tab_ref, o_ref,
                       *scratch, tb):
    nsel = PEER_HEADS * PEER_TOPK
    nrow = h_ref.shape[1] // LANES
    pitch = nrow + SLAB_PAD
    ns = GATHER_SLOTS
    bufs, sem = scratch[:ns], scratch[ns]
    lane_t = lax.broadcasted_iota(jnp.int32, (nsel, tb), 1)
    g2 = mod_ref[5:6, :]
    step = pl.program_id(0)
    nsteps = pl.num_programs(0)

    def row_copy(ids_ref, t, j, slot):
        return pltpu.make_async_copy(
            tab_ref.at[ids_ref[t, j]],
            bufs[slot].at[pl.ds(j * pitch, nrow), :],
            sem.at[slot])

    def issue(ids_ref, t, slot):
        for j in range(nsel):
            row_copy(ids_ref, t, j, slot).start(priority=j % 2)

    def wait(t, slot):
        for j in range(nsel):
            row_copy(idx_ref, t, j, slot).wait()

    def packed(slot, s):
        return bufs[slot][pl.ds(s, nsel, stride=pitch), :]

    def compute(t, slot):
        hrow = h_ref[pl.ds(t, 1), :]
        acc = jnp.zeros((nsel, LANES), F32)
        for s in range(nrow):
            u = lax.bitcast_convert_type(packed(slot, s) << 16, F32)
            acc = acc + u * hrow[:, LANES * s:LANES * (s + 1)]
        dots = jnp.sum(acc, axis=-1, keepdims=True)
        gcol = jnp.sum(jnp.where(lane_t == t, gate_ref[...], 0.0), axis=-1, keepdims=True)
        coef = gcol * _gelu(dots)
        outs = []
        for s in range(nrow):
            vv = lax.bitcast_convert_type(packed(slot, s) & jnp.uint32(0xFFFF0000), F32)
            outs.append(jnp.sum(coef * vv, axis=0, keepdims=True))
        orow = jnp.concatenate(outs, axis=1)
        o_ref[pl.ds(t, 1), :] = x_ref[pl.ds(t, 1), :] + g2 * orow

    @pl.when(step == 0)
    def _():
        for s in range(ns - 1):
            issue(idx_ref, s, s)

    ngroups = tb // ns

    def body(g, carry):
        for s in range(ns):
            t = g * ns + s
            wait(t, s)
            issue(idx_ref, t + ns - 1, (s + ns - 1) % ns)
            compute(t, s)
        return carry

    lax.fori_loop(0, ngroups - 1, body, 0)
    for s in range(ns):
        t = (ngroups - 1) * ns + s
        wait(t, s)
        if s == 0:
            issue(idx_ref, tb - 1, ns - 1)
        else:
            @pl.when(step < nsteps - 1)
            def _():
                issue(idxn_ref, s - 1, s - 1)
        compute(t, s)


def _peer_gather(geo_g, eidx, gate_t, h2, x, mods, table, n):
    d = x.shape[1]
    tb = geo_g.tm
    nsel = PEER_HEADS * PEER_TOPK
    pitch = d // LANES + SLAB_PAD
    assert n % tb == 0
    nsteps = n // tb
    return pl.pallas_call(
        functools.partial(_peergather_kernel, tb=tb),
        grid=(nsteps,),
        in_specs=[
            pl.BlockSpec((tb, nsel), lambda i: (i, 0), memory_space=pltpu.SMEM),
            pl.BlockSpec((tb, nsel), lambda i: (jnp.minimum(i + 1, nsteps - 1), 0),
                         memory_space=pltpu.SMEM),
            pl.BlockSpec((nsel, tb), lambda i: (0, i)),
            pl.BlockSpec((tb, d), lambda i: (i, 0)),
            pl.BlockSpec((tb, d), lambda i: (i, 0)),
            geo_g.mod_spec(),
            pl.BlockSpec(memory_space=pl.ANY),
        ],
        out_specs=pl.BlockSpec((tb, d), lambda i: (i, 0)),
        out_shape=jax.ShapeDtypeStruct((n, d), F32),
        scratch_shapes=[pltpu.VMEM((nsel * pitch, LANES), jnp.uint32)] * GATHER_SLOTS
        + [pltpu.SemaphoreType.DMA((GATHER_SLOTS,))],
        compiler_params=_cparams("arbitrary"),
        name="peer_gather",
    )(eidx, eidx, gate_t, h2, x, mods, table)


SC_CORES = 2
SC_SUBCORES = 16
SC_LANES = 16
SC_CHUNK = 16
SC_SHARE = 0.3235

_ERF_ALPHA = (0.00022905065861350646, 0.0034082910107109506, 0.050955695062380861,
              0.18520832239976145, 1.128379143519084)
_ERF_BETA = (-1.1791602954361697e-7, 0.000023547966471313185, 0.0010179625278914885,
             0.014070470171167667, 0.11098505178285362, 0.49746925110067538, 1.0)
_ERF_CLAMP = 3.832506856900711


def _erf_rational(x):
    x = jnp.minimum(jnp.maximum(x, -_ERF_CLAMP), _ERF_CLAMP)
    x2 = x * x
    p = jnp.full_like(x, _ERF_ALPHA[0])
    for c in _ERF_ALPHA[1:]:
        p = p * x2 + c
    q = jnp.full_like(x, _ERF_BETA[0])
    for c in _ERF_BETA[1:]:
        q = q * x2 + c
    return x * p / q


def _peer_sc(eidx, gate, h, table, row0):
    n, nsel = eidx.shape
    d = h.shape[1]
    nw = SC_CORES * SC_SUBCORES
    assert n % (2 * nw) == 0 and nsel % SC_CHUNK == 0 and d % SC_LANES == 0
    tpw = n // nw
    nchunk = nsel // SC_CHUNK
    nvec = d // SC_LANES
    mesh = plsc.VectorSubcoreMesh(core_axis_name="c", subcore_axis_name="s",
                                  num_cores=SC_CORES, num_subcores=SC_SUBCORES)

    @functools.partial(
        pl.kernel, mesh=mesh,
        out_type=jax.ShapeDtypeStruct((n, d), F32),
        scratch_types=[
            pltpu.VMEM((nsel,), jnp.int32), pltpu.VMEM((nsel,), jnp.int32),
            pltpu.VMEM((nsel,), F32), pltpu.VMEM((nsel,), F32),
            pltpu.VMEM((d,), F32), pltpu.VMEM((d,), F32),
            pltpu.VMEM((d,), F32),
            pltpu.VMEM((SC_CHUNK, d), jnp.uint32),
            pltpu.VMEM((SC_CHUNK, d), jnp.uint32),
            pltpu.SemaphoreType.DMA, pltpu.SemaphoreType.DMA,
            pltpu.SemaphoreType.DMA, pltpu.SemaphoreType.DMA,
        ],
        compiler_params=pltpu.CompilerParams(needs_layout_passes=False),
        cost_estimate=pl.CostEstimate(flops=4 * n * nsel * d, transcendentals=0,
                                      bytes_accessed=4 * n * nsel * d),
        name="peer_sc",
    )
    def sc_kernel(eidx_hbm, gate_hbm, h_hbm, tab_hbm, y_hbm, idx_a, idx_b, gate_a, gate_b, h_a, h_b,
                  o_v, rows0, rows1, sem0, sem1, msem_a, msem_b):
        wid = lax.axis_index("s") * SC_CORES + lax.axis_index("c")
        lanes = lax.iota(jnp.int32, SC_LANES)
        zero = jnp.zeros((SC_LANES,), F32)
        bufs = ((rows0, sem0), (rows1, sem1))
        meta = ((idx_a, gate_a, h_a, msem_a), (idx_b, gate_b, h_b, msem_b))

        def meta_copies(tok, s):
            idx_v, gate_v, h_v, msem = meta[s]
            return (pltpu.make_async_copy(eidx_hbm.at[tok], idx_v, msem),
                    pltpu.make_async_copy(gate_hbm.at[tok], gate_v, msem),
                    pltpu.make_async_copy(h_hbm.at[row0 + tok], h_v, msem))

        def gather(s, c, slot):
            rows, sem = bufs[slot]
            return pltpu.make_async_copy(
                tab_hbm.at[meta[s][0].at[pl.ds(c * SC_CHUNK, SC_CHUNK)]], rows, sem)

        def chunk_compute(s, c, rows):
            _, gate_v, h_v, _ = meta[s]

            def dot_body(i, accs):
                hv = h_v[pl.ds(i * SC_LANES, SC_LANES)]
                out = []
                for e in range(SC_CHUNK):
                    w = rows[e, pl.ds(i * SC_LANES, SC_LANES)]
                    out.append(accs[e] + lax.bitcast_convert_type(w << 16, F32) * hv)
                return tuple(out)

            accs = lax.fori_loop(0, nvec, dot_body, (zero,) * SC_CHUNK)
            dots = zero
            for e in range(SC_CHUNK):
                dots = jnp.where(lanes == e, jnp.sum(accs[e]), dots)
            act = 0.5 * dots * (1.0 + _erf_rational(dots * (2.0 ** -0.5)))
            coef = gate_v[pl.ds(c * SC_CHUNK, SC_CHUNK)] * act
            splat = [jnp.full((SC_LANES,), jnp.sum(jnp.where(lanes == e, coef, 0.0)), F32)
                     for e in range(SC_CHUNK)]

            def acc_body(i, carry):
                o = o_v[pl.ds(i * SC_LANES, SC_LANES)]
                for e in range(SC_CHUNK):
                    w = rows[e, pl.ds(i * SC_LANES, SC_LANES)]
                    o = o + splat[e] * lax.bitcast_convert_type(w & jnp.uint32(0xFFFF0000), F32)
                o_v[pl.ds(i * SC_LANES, SC_LANES)] = o
                return carry

            lax.fori_loop(0, nvec, acc_body, 0)

        def zero_body(i, c2):
            o_v[pl.ds(i * SC_LANES, SC_LANES)] = zero
            return c2

        def token(tok, s, has_next):
            def when_next(fn):
                if isinstance(has_next, bool):
                    if has_next:
                        fn()
                else:
                    pl.when(has_next)(fn)

            def load_next():
                for cp in meta_copies(tok + 1, 1 - s):
                    cp.start()

            when_next(load_next)
            lax.fori_loop(0, nvec, zero_body, 0)
            for c in range(nchunk):
                if c + 1 < nchunk:
                    gather(s, c + 1, (c + 1) % 2).start()
                gather(s, c, c % 2).wait()
                chunk_compute(s, c, bufs[c % 2][0])

            def prefetch_next():
                for cp in meta_copies(tok + 1, 1 - s):
                    cp.wait()
                gather(1 - s, 0, 0).start()

            when_next(prefetch_next)
            pltpu.sync_copy(o_v, y_hbm.at[tok])

        base = wid * tpw
        for cp in meta_copies(base, 0):
            cp.start()
        for cp in meta_copies(base, 0):
            cp.wait()
        gather(0, 0, 0).start()

        def pair_body(g, carry):
            token(base + 2 * g, 0, True)
            token(base + 2 * g + 1, 1, g + 1 < tpw // 2)
            return carry

        lax.fori_loop(0, tpw // 2, pair_body, 0)

    return sc_kernel(eidx, gate, h, table)


def _pad_to(x, axis, size):
    pad = [(0, 0)] * x.ndim
    pad[axis] = (0, size - x.shape[axis])
    return jnp.pad(x, pad)


def _rwkv_layer(geo, xs, mods, norm1, mix, w_rkv, w_o, w0, w1, w2, a0, a1, a2, vl, g1, g2,
                k_k, k_a, r_k, ln_w, ln_b, vfirst):
    d = geo.d
    xm = _rwkv_mix(geo, xs, norm1, mods, mix)
    rkv = _bmm(xm, w_rkv.astype(BF16), (0, 2, 3), ("none",) * 3, F32, geo.tm)
    lt = g1.shape[1]
    w1c = jnp.concatenate([_pad_to(w1[0], 1, LORA_PAD), _pad_to(w1[1], 1, LORA_PAD)], axis=1)
    a1c = jnp.concatenate([_pad_to(a1[0], 1, LORA_PAD), _pad_to(a1[1], 1, LORA_PAD)], axis=1)
    if vl is None:
        v1p = jnp.zeros((d, lt), F32)
        v2p = jnp.zeros((LORA_PAD, d), F32)
        v0 = jnp.zeros((d,), F32)
    else:
        v0, v1, v2 = vl
        v1p = _pad_to(v1, 1, lt)
        v2p = _pad_to(v2, 0, LORA_PAD)
    wl1 = jnp.stack([_pad_to(w1c, 1, lt), _pad_to(a1c, 1, lt), g1, v1p]).astype(BF16)
    tl = _bmm(xm, wl1, (1, 4, 5, 3), ("tanh", "none", "sigmoid", "none"), BF16, geo.tm)
    w2p = jnp.stack([_pad_to(w2[0], 0, LORA_PAD), _pad_to(w2[1], 0, LORA_PAD)]).astype(BF16)
    a2p = jnp.stack([_pad_to(a2[0], 0, LORA_PAD), _pad_to(a2[1], 0, LORA_PAD)]).astype(BF16)
    pvec = jnp.stack([w0[0], w0[1], a0[0], a0[1], v0, k_k, k_a, jnp.zeros_like(k_k)])
    lw, kd, asg, kk, g, v = _rwkv_feat(geo, rkv, tl, w2p, a2p, g2.astype(BF16), v2p.astype(BF16),
                                      pvec, vfirst)
    r = rkv[0]
    wkv = _wkv_bidir(r, v, kk, lw, kd, asg, geo.b, geo.l)
    pv2 = _pad_to(jnp.stack([r_k, ln_w, ln_b]), 0, SUBLANES)
    y = _rwkv_readout(geo, wkv, r, kd, v, g, pv2)
    xs = _matmul_res(geo, y, w_o.astype(BF16), xs, mods, 2)
    return xs, v


def _rope_tables(geo):
    t = geo.t
    pos = jnp.arange(t)
    row = (pos // GRID_W).astype(F32)
    col = (pos % GRID_W).astype(F32)
    n_freq = QK_ROPE // 4
    inv_freq = ROPE_THETA ** (-jnp.arange(n_freq, dtype=F32) / n_freq)
    ang = jnp.concatenate([row[:, None] * inv_freq, col[:, None] * inv_freq], axis=-1)
    cos, sin = jnp.cos(ang), jnp.sin(ang)
    pad = LANES - QK_ROPE
    cos_l = jnp.concatenate([cos, cos, jnp.ones((t, pad), F32)], axis=1)
    sin_l = jnp.concatenate([-sin, sin, jnp.zeros((t, pad), F32)], axis=1)
    cos_c = jnp.ones((geo.l, LANES), F32)
    sin_c = jnp.zeros((geo.l, LANES), F32)
    return jnp.concatenate([cos_c, cos_l], axis=0), jnp.concatenate([sin_c, sin_l], axis=0)


def _mla_layer(geo, xs, mods, norm1, rope_t, w_in, q_norm, kv_norm, w_uq, w_ukv, g_q, g_k, w_o):
    q_lora, kv_lora = q_norm.shape[0], kv_norm.shape[0]
    zw = q_lora + kv_lora + LANES
    z = _mod_matmul(geo, xs, norm1, mods, _pad_to(w_in, 1, zw).astype(BF16), 0, False)
    qk = QK_NOPE + QK_ROPE
    wuq_p = _pad_to(w_uq.reshape(q_lora, MLA_HEADS, qk), 2, QK_PAD).reshape(q_lora, -1)
    gq_p = _pad_to(g_q, 0, QK_PAD).reshape(1, QK_PAD)
    gk_p = _pad_to(g_k, 0, QK_PAD).reshape(1, QK_PAD)
    q, k, v = _mla_qkv(geo, z, q_norm, kv_norm, wuq_p.astype(BF16), w_ukv.astype(BF16),
                       gq_p, gk_p, *rope_t)
    o = _mla_attention(geo, q, k, v)
    return _matmul_res(geo, o, w_o.astype(BF16), xs, mods, 2)


def _peer_layer(geo, geo_g, xs, mods, norm2, w_q, q_norm, keys, u, v):
    qp, h2 = _mod_matmul(geo, xs, norm2, mods, w_q.astype(BF16), 1, True)
    table = _pack_expert_table(u, v)
    n, d = xs.shape
    n_sc = geo.tm * round(SC_SHARE * n / geo.tm)
    if n_sc % (2 * SC_CORES * SC_SUBCORES) != 0:
        n_sc = 0
    n_tc = n - n_sc

    def select(row0, nrows):
        wide = 2 * geo.tm
        tsel = wide if row0 % wide == 0 and nrows % wide == 0 else geo.tm
        return _peer_select(qp, q_norm, keys, tsel, row0, nrows)

    if n_sc:
        eidx_sc, gate_sc = select(n_tc, n_sc)
        y_sc = _peer_sc(eidx_sc.T, gate_sc.T, h2, table.reshape(table.shape[0], d), n_tc)
        q_norm = q_norm + 0.0 * gate_sc[0, 0]
    eidx_t, gate_t = select(0, n_tc)
    out_tc = _peer_gather(geo_g, eidx_t.T, gate_t, h2, xs, mods, table, n_tc)
    if n_sc == 0:
        return out_tc
    g2, r = [], n_tc
    while r < n:
        bi, pos = divmod(r, geo.s)
        seg = int(pos >= geo.l)
        stop = min(n, bi * geo.s + (geo.s if seg else geo.l))
        g2.append(jnp.broadcast_to(mods[bi, seg, 5], (stop - r, d)))
        r = stop
    out_sc = xs[n_tc:] + jnp.concatenate(g2, axis=0) * y_sc
    return jnp.concatenate([out_tc, out_sc], axis=0)


def kernel(x, c, ctx, c_ctx, w_ada, b_ada, norm1, norm2, rw_mix, rw_wrkv, rw_wo, rw_w0, rw_w1, rw_w2, rw_a0, rw_a1, rw_a2, rw_v0, rw_v1, rw_v2, rw_g1, rw_g2, rw_kk, rw_ka, rw_rk, rw_lnw, rw_lnb, mla_win, mla_qnorm, mla_kvnorm, mla_wuq, mla_wukv, mla_gq, mla_gk, mla_wo, peer_wq, peer_qnorm, peer_keys, peer_u, peer_v):
    b, t, d = x.shape
    l = ctx.shape[1]
    depth = w_ada.shape[0]
    geo = _Geom(b, l, t, d, min(256, l))
    geo_g = _Geom(b, l, t, d, min(128, l))
    cond8 = _pad_to(jnp.concatenate([c, c_ctx[None, :]], axis=0), 0, SUBLANES)
    ada = _adaln(cond8, w_ada, b_ada).reshape(depth, SUBLANES, 6, d)
    mods_all = jnp.stack([jnp.broadcast_to(ada[:, b:b + 1], (depth, b, 6, d)), ada[:, 0:b]], axis=2)
    xs = jnp.concatenate([ctx, x], axis=1).reshape(b * (l + t), d)
    rope_t = _rope_tables(geo)
    vfirst = None
    for i in range(depth):
        j = i // 2
        mods = mods_all[i]
        if i % 2 == 0:
            vl = None if j == 0 else (rw_v0[j - 1], rw_v1[j - 1], rw_v2[j - 1])
            xs, vcur = _rwkv_layer(geo, xs, mods, norm1[i], rw_mix[j], rw_wrkv[j], rw_wo[j],
                                   rw_w0[j], rw_w1[j], rw_w2[j], rw_a0[j], rw_a1[j], rw_a2[j], vl,
                                   rw_g1[j], rw_g2[j], rw_kk[j], rw_ka[j], rw_rk[j], rw_lnw[j],
                                   rw_lnb[j], vfirst)
            if j == 0:
                vfirst = vcur
        else:
            xs = _mla_layer(geo, xs, mods, norm1[i], rope_t, mla_win[j], mla_qnorm[j],
                            mla_kvnorm[j], mla_wuq[j], mla_wukv[j], mla_gq[j], mla_gk[j], mla_wo[j])
        if i == depth - 1:
            xs = xs.reshape(b, l + t, d)[:, l:, :].reshape(b * t, d)
            geo, geo_g = _Geom(b, 0, t, d, geo.tm), _Geom(b, 0, t, d, geo_g.tm)
        xs = _peer_layer(geo, geo_g, xs, mods, norm2[i], peer_wq[i], peer_qnorm[i], peer_keys[i],
                         peer_u[i], peer_v[i])
    return xs.reshape(b, t, d)
```

```python
import functools
import math

import jax
import jax.numpy as jnp
from jax import lax
from jax.experimental import pallas as pl
from jax.experimental.pallas import tpu as pltpu
from jax.experimental.pallas import tpu_sc as plsc

F32 = jnp.float32
BF16 = jnp.bfloat16

EPS = 1e-6
GN_EPS = 64e-5
RW_HEAD = 64
WKV_CHUNK = 64
MLA_HEADS = 16
QK_NOPE = 128
QK_ROPE = 64
V_HEAD = 128
QK_PAD = 256
ROPE_THETA = 10000.0
GRID_W = 64
ATTN_SCALE = (QK_NOPE + QK_ROPE) ** -0.5
PEER_HEADS = 8
N_KEYS = 128
PEER_TOPK = 16
D_KEY = 256
LORA_PAD = 128

LANES = 128
SUBLANES = 8
VMEM_LIMIT = 56 * 1024 * 1024


def _cparams(*sem):
    return pltpu.CompilerParams(dimension_semantics=sem, vmem_limit_bytes=VMEM_LIMIT)


def _dot(a, b):
    return jnp.dot(a, b, preferred_element_type=F32)


def _dot_nt(a, b):
    return lax.dot_general(a, b, (((1,), (1,)), ((), ())), preferred_element_type=F32)


def _split2(x):
    hi = x.astype(BF16)
    lo = (x - hi.astype(F32)).astype(BF16)
    return hi, lo


def _split3(x):
    hi = x.astype(BF16)
    r1 = x - hi.astype(F32)
    mid = r1.astype(BF16)
    lo = (r1 - mid.astype(F32)).astype(BF16)
    return hi, mid, lo


def _dot3(a, b):
    ah, al = _split2(a)
    bh, bl = _split2(b)
    return _dot(ah, bh) + (_dot(ah, bl) + _dot(al, bh))


def _dot3_nt(a, b):
    ah, al = _split2(a)
    bh, bl = _split2(b)
    return _dot_nt(ah, bh) + (_dot_nt(ah, bl) + _dot_nt(al, bh))


def _dot_exact_lhs(sel, x):
    hi, mid, lo = _split3(x)
    return _dot(sel, hi) + (_dot(sel, mid) + _dot(sel, lo))


def _modulate(x, g, shift, scale):
    ms = jnp.mean(x * x, axis=-1, keepdims=True)
    return (x * lax.rsqrt(ms + EPS) * g) * (1.0 + scale) + shift


def _sigmoid(x):
    return 1.0 / (1.0 + jnp.exp(-x))


def _softplus(y):
    return jnp.maximum(y, 0.0) + jnp.log(1.0 + jnp.exp(-jnp.abs(y)))


def _erf(x):
    return lax.erf(x)


def _gelu(x):
    return 0.5 * x * (1.0 + _erf(x * (2.0 ** -0.5)))


def _ada_kernel(s_ref, w_ref, b_ref, o_ref):
    s = s_ref[...]
    s = s * _sigmoid(s)
    o_ref[...] = _dot3(s, w_ref[...]) + b_ref[...]


def _adaln(cond8, w_ada, b_ada):
    depth, d, n = w_ada.shape
    tn = 1024
    return pl.pallas_call(
        _ada_kernel,
        grid=(depth, n // tn),
        in_specs=[
            pl.BlockSpec((SUBLANES, d), lambda l, j: (0, 0)),
            pl.BlockSpec((None, d, tn), lambda l, j: (l, 0, j)),
            pl.BlockSpec((None, 1, tn), lambda l, j: (l, 0, j)),
        ],
        out_specs=pl.BlockSpec((None, SUBLANES, tn), lambda l, j: (l, 0, j)),
        out_shape=jax.ShapeDtypeStruct((depth, SUBLANES, n), F32),
        compiler_params=_cparams("parallel", "parallel"),
        name="adaln",
    )(cond8, w_ada, b_ada.reshape(depth, 1, n))


class _Geom:
    def __init__(self, batch, ctx_len, seq_len, d_model, tm):
        self.b, self.l, self.t, self.d = batch, ctx_len, seq_len, d_model
        self.s = ctx_len + seq_len
        self.n = batch * self.s
        self.tm = tm
        assert ctx_len % tm == 0 and seq_len % tm == 0
        self.tpb = self.s // tm
        self.nct = ctx_len // tm
        self.ntiles = self.n // tm

    def mod_spec(self, nlead=0):
        tpb, nct = self.tpb, self.nct

        def imap(*ids):
            i = ids[nlead]
            return (i // tpb, ((i % tpb) >= nct).astype(jnp.int32), 0, 0)

        return pl.BlockSpec((None, None, 6, self.d), imap)


def _modmm_kernel(x_ref, g_ref, mod_ref, w_ref, o_ref, *h_ref, which):
    h = _modulate(x_ref[...], g_ref[...], mod_ref[3 * which:3 * which + 1, :],
                  mod_ref[3 * which + 1:3 * which + 2, :])
    if h_ref:
        h_ref[0][...] = h
    o_ref[...] = _dot(h.astype(BF16), w_ref[...])


def _mod_matmul(geo, x, g, mods, w, which, emit_h):
    n, d = x.shape
    nn = w.shape[1]
    tm = geo.tm
    out_shape = [jax.ShapeDtypeStruct((n, nn), F32)]
    out_specs = [pl.BlockSpec((tm, nn), lambda i: (i, 0))]
    if emit_h:
        out_shape.append(jax.ShapeDtypeStruct((n, d), F32))
        out_specs.append(pl.BlockSpec((tm, d), lambda i: (i, 0)))
    res = pl.pallas_call(
        functools.partial(_modmm_kernel, which=which),
        grid=(geo.ntiles,),
        in_specs=[
            pl.BlockSpec((tm, d), lambda i: (i, 0)),
            pl.BlockSpec((1, d), lambda i: (0, 0)),
            geo.mod_spec(),
            pl.BlockSpec((d, nn), lambda i: (0, 0)),
        ],
        out_specs=out_specs,
        out_shape=out_shape,
        compiler_params=_cparams("parallel"),
        name="mod_matmul",
    )(x, g.reshape(1, d), mods, w)
    return res if emit_h else res[0]


def _mmres_kernel(y_ref, w_ref, x_ref, mod_ref, o_ref, *, gidx):
    acc = _dot(y_ref[...], w_ref[...])
    o_ref[...] = x_ref[...] + mod_ref[gidx:gidx + 1, :] * acc


def _matmul_res(geo, y, w, x, mods, gidx):
    n, k = y.shape
    d = x.shape[1]
    tm = geo.tm
    return pl.pallas_call(
        functools.partial(_mmres_kernel, gidx=gidx),
        grid=(geo.ntiles,),
        in_specs=[
            pl.BlockSpec((tm, k), lambda i: (i, 0)),
            pl.BlockSpec((k, d), lambda i: (0, 0)),
            pl.BlockSpec((tm, d), lambda i: (i, 0)),
            geo.mod_spec(),
        ],
        out_specs=pl.BlockSpec((tm, d), lambda i: (i, 0)),
        out_shape=jax.ShapeDtypeStruct((n, d), F32),
        compiler_params=_cparams("parallel"),
        name="matmul_res",
    )(y, w, x, mods)


def _bmm_kernel(x_ref, w_ref, o_ref, *, acts):
    j = pl.program_id(0)
    y = _dot(x_ref[...], w_ref[...])
    out = y
    for jj, a in enumerate(acts):
        if a == "tanh":
            out = jnp.where(j == jj, jnp.tanh(y), out)
        elif a == "sigmoid":
            out = jnp.where(j == jj, _sigmoid(y), out)
    o_ref[...] = out.astype(o_ref.dtype)


def _bmm(x3, w3, src, acts, out_dtype, tm):
    _, n, k = x3.shape
    nj, _, nn = w3.shape
    src = tuple(src)

    def xmap(j, i):
        idx = jnp.int32(src[0])
        for jj in range(1, nj):
            idx = jnp.where(j == jj, jnp.int32(src[jj]), idx)
        return (idx, i, 0)

    return pl.pallas_call(
        functools.partial(_bmm_kernel, acts=tuple(acts)),
        grid=(nj, n // tm),
        in_specs=[
            pl.BlockSpec((None, tm, k), xmap),
            pl.BlockSpec((None, k, nn), lambda j, i: (j, 0, 0)),
        ],
        out_specs=pl.BlockSpec((None, tm, nn), lambda j, i: (j, i, 0)),
        out_shape=jax.ShapeDtypeStruct((nj, n, nn), out_dtype),
        compiler_params=_cparams("parallel", "parallel"),
        name="bmm",
    )(x3, w3)


def _rwmix_kernel(x_ref, xp_ref, xn_ref, g_ref, mod_ref, mix_ref, o_ref, *, tpb, nct):
    i = pl.program_id(0)
    tm = x_ref.shape[0]
    g = g_ref[...]
    shift = mod_ref[0:1, :]
    scale = mod_ref[1:2, :]
    h = _modulate(x_ref[...], g, shift, scale)
    hp = _modulate(xp_ref[...], g, shift, scale)[SUBLANES - 1:SUBLANES, :]
    hn = _modulate(xn_ref[...], g, shift, scale)[0:1, :]
    it = i % tpb
    first = jnp.logical_or(it == 0, it == nct)
    last = jnp.logical_or(it == nct - 1, it == tpb - 1)
    hp = jnp.where(first, 0.0, hp)
    hn = jnp.where(last, 0.0, hn)
    rows = lax.broadcasted_iota(jnp.int32, h.shape, 0)
    prev = jnp.where(rows == 0, hp, pltpu.roll(h, 1, axis=0))
    nxt = jnp.where(rows == tm - 1, hn, pltpu.roll(h, tm - 1, axis=0))
    xx = 0.5 * (prev + nxt) - h
    for m in range(6):
        o_ref[m] = (h + xx * mix_ref[m:m + 1, :]).astype(BF16)


def _rwkv_mix(geo, x, g, mods, mix):
    n, d = x.shape
    tm = geo.tm
    r8 = tm // SUBLANES
    nblk8 = n // SUBLANES
    return pl.pallas_call(
        functools.partial(_rwmix_kernel, tpb=geo.tpb, nct=geo.nct),
        grid=(geo.ntiles,),
        in_specs=[
            pl.BlockSpec((tm, d), lambda i: (i, 0)),
            pl.BlockSpec((SUBLANES, d), lambda i: (jnp.maximum(i * r8 - 1, 0), 0)),
            pl.BlockSpec((SUBLANES, d), lambda i: (jnp.minimum((i + 1) * r8, nblk8 - 1), 0)),
            pl.BlockSpec((1, d), lambda i: (0, 0)),
            geo.mod_spec(),
            pl.BlockSpec((6, d), lambda i: (0, 0)),
        ],
        out_specs=pl.BlockSpec((6, tm, d), lambda i: (0, i, 0)),
        out_shape=jax.ShapeDtypeStruct((6, n, d), BF16),
        compiler_params=_cparams("parallel"),
        name="rwkv_mix",
    )(x, x, x, g.reshape(1, d), mods, mix)


def _head_sum(x, bd):
    hi, lo = _split2(x)
    return _dot(hi, bd) + _dot(lo, bd)


def _rwfeat_kernel(k_ref, v_ref, tl_ref, w2_ref, a2_ref, g2_ref, v2_ref, pv_ref, bd_ref,
                   *rest, has_vlora):
    if has_vlora:
        vf_ref, lw_ref, kd_ref, as_ref, kk_ref, g_ref, vo_ref = rest
    else:
        lw_ref, kd_ref, as_ref, kk_ref, g_ref = rest
    k = k_ref[...]
    tw = tl_ref[0]
    ta = tl_ref[1]
    tg = tl_ref[2]
    w0 = pv_ref[0:2, :]
    a0 = pv_ref[2:4, :]
    k_k = pv_ref[5:6, :]
    k_a = pv_ref[6:7, :]
    for z in range(2):
        sl = slice(LORA_PAD * z, LORA_PAD * (z + 1))
        lora_w = _dot(tw[:, sl], w2_ref[z])
        w = -_softplus(-(w0[z:z + 1, :] + lora_w)) - 0.5
        lw_ref[z] = -jnp.exp(w)
        a_sig = _sigmoid(a0[z:z + 1, :] + _dot(ta[:, sl], a2_ref[z]))
        as_ref[z] = a_sig.astype(BF16)
        kd_ref[z] = k * (1.0 + (a_sig - 1.0) * k_a)
    g_ref[...] = _dot(tg, g2_ref[...]).astype(BF16)
    kkr = k * k_k
    ss = _head_sum(kkr * kkr, bd_ref[...])
    kk_ref[...] = (kkr * lax.rsqrt(ss + 1e-12)).astype(BF16)
    if has_vlora:
        v = v_ref[...]
        tv = tl_ref[3]
        gate = _sigmoid(pv_ref[4:5, :] + _dot(tv[:, 0:LORA_PAD], v2_ref[...]))
        vo_ref[...] = v + (vf_ref[...] - v) * gate


def _rwkv_feat(geo, rkv, tl, w2p, a2p, g2, v2p, pvec, vfirst):
    _, n, d = rkv.shape
    tm, tc = geo.tm, 512
    has_vlora = vfirst is not None
    lt = tl.shape[2]
    ii = lax.broadcasted_iota(jnp.int32, (tc, tc), 0) // RW_HEAD
    jj = lax.broadcasted_iota(jnp.int32, (tc, tc), 1) // RW_HEAD
    bd = (ii == jj).astype(BF16)
    row = lambda i, j: (i, j)
    in_specs = [
        pl.BlockSpec((None, tm, tc), lambda i, j: (1, i, j)),
        pl.BlockSpec((None, tm, tc), lambda i, j: (2, i, j)),
        pl.BlockSpec((4, tm, lt), lambda i, j: (0, i, 0)),
        pl.BlockSpec((2, LORA_PAD, tc), lambda i, j: (0, 0, j)),
        pl.BlockSpec((2, LORA_PAD, tc), lambda i, j: (0, 0, j)),
        pl.BlockSpec((lt, tc), lambda i, j: (0, j)),
        pl.BlockSpec((LORA_PAD, tc), lambda i, j: (0, j)),
        pl.BlockSpec((SUBLANES, tc), lambda i, j: (0, j)),
        pl.BlockSpec((tc, tc), lambda i, j: (0, 0)),
    ]
    args = [rkv, rkv, tl, w2p, a2p, g2, v2p, pvec, bd]
    dir_spec = pl.BlockSpec((2, tm, tc), lambda i, j: (0, i, j))
    out_specs = [dir_spec, dir_spec, dir_spec, pl.BlockSpec((tm, tc), row), pl.BlockSpec((tm, tc), row)]
    out_shape = [jax.ShapeDtypeStruct((2, n, d), dt) for dt in (F32, F32, BF16)]
    out_shape += [jax.ShapeDtypeStruct((n, d), BF16)] * 2
    if has_vlora:
        in_specs.append(pl.BlockSpec((tm, tc), row))
        args.append(vfirst)
        out_specs.append(pl.BlockSpec((tm, tc), row))
        out_shape.append(jax.ShapeDtypeStruct((n, d), F32))
    res = pl.pallas_call(
        functools.partial(_rwfeat_kernel, has_vlora=has_vlora),
        grid=(geo.ntiles, d // tc),
        in_specs=in_specs,
        out_specs=out_specs,
        out_shape=out_shape,
        compiler_params=_cparams("parallel", "parallel"),
        name="rwkv_feat",
    )(*args)
    if has_vlora:
        lw, kd, asg, kk, g, v = res
    else:
        lw, kd, asg, kk, g = res
        v = rkv[2]
    return lw, kd, asg, kk, g, v


def _wkv_chunk_kernel(r_ref, v_ref, kk_ref, lw_ref, kd_ref, as_ref,
                      m_ref, ga_ref, rq_ref, o0_ref, pc_ref, *, npairs):
    c = WKV_CHUNK
    c2 = 2 * c
    sgn = 1 - 2 * pl.program_id(0)
    ri = lax.broadcasted_iota(jnp.int32, (c2, c2), 0)
    ci = lax.broadcasted_iota(jnp.int32, (c2, c2), 1)
    same = (ri >= c) == (ci >= c)
    tt = jnp.where(ri >= c, ri - c, ri)
    ss = jnp.where(ci >= c, ci - c, ci)
    earlier = (ss - tt) * sgn < 0
    strict = jnp.logical_and(same, earlier)
    incl = jnp.logical_and(same, jnp.logical_or(earlier, ss == tt))
    eye = (ri == ci).astype(F32)
    r64 = lax.broadcasted_iota(jnp.int32, (c, c), 0)
    c64 = lax.broadcasted_iota(jnp.int32, (c, c), 1)
    ltri = jnp.where((c64 - r64) * sgn <= 0, 1.0, 0.0).astype(BF16)
    head0 = lax.broadcasted_iota(jnp.int32, (c, LANES), 1) < RW_HEAD
    pairs = range(npairs)

    def stack(x):
        return jnp.concatenate([jnp.where(head0, x, 0.0), jnp.where(head0, 0.0, x)], axis=0)

    def dup(x):
        return jnp.concatenate([x, x], axis=0)

    def bf(x):
        return x.astype(BF16)

    lhs, rhs, a2, bp2, kp2, vst, r2 = [], [], [], [], [], [], []
    lw_all = lw_ref[...]
    cum_all = _dot_exact_lhs(ltri, lw_all)
    for p in pairs:
        sl = slice(LANES * p, LANES * (p + 1))
        lw = lw_all[:, sl]
        cum = cum_all[:, sl]
        tot = jnp.sum(lw, axis=0, keepdims=True)
        p_inv = jnp.exp(-cum)
        p_end = jnp.exp(tot - cum)
        kk = kk_ref[:, sl]
        b = kk * as_ref[:, sl]
        kd = kd_ref[:, sl]
        a2p = stack(-kk * jnp.exp(cum - lw))
        r2p = stack(r_ref[:, sl] * jnp.exp(cum))
        a2.append(bf(a2p))
        r2.append(r2p)
        lhs.append(jnp.concatenate([a2[p], bf(r2p)], axis=0))
        rhs.append(jnp.concatenate([dup(bf(b * p_inv)), dup(bf(kd * p_inv))], axis=0))
        bp2.append(bf(stack(b * p_end)))
        kp2.append(bf(stack(kd * p_end)))
        vst.append(bf(stack(v_ref[:, sl])))
        pc_ref[:, sl] = jnp.broadcast_to(jnp.exp(tot), (SUBLANES, LANES))
    gram = [_dot_nt(lhs[p], rhs[p]) for p in pairs]
    nab = [jnp.where(strict, gram[p][0:c2, 0:c2], 0.0) for p in pairs]
    nrb = [bf(jnp.where(incl, gram[p][c2:2 * c2, 0:c2], 0.0)) for p in pairs]
    nk = [bf(jnp.concatenate([jnp.where(strict, gram[p][0:c2, c2:2 * c2], 0.0),
                              jnp.where(incl, gram[p][c2:2 * c2, c2:2 * c2], 0.0)], axis=0))
          for p in pairs]
    quads = range(npairs // 2)

    def side(x0, x1):
        return jnp.concatenate([x0, x1], axis=1)

    def diag(x0, x1):
        z0 = jnp.zeros_like(x0)
        return jnp.concatenate([side(x0, z0), side(z0, x1)], axis=0)

    def diag_halves(x):
        return diag(x[:, 0:c2], x[:, c2:2 * c2])

    def unside(xs):
        return [xs[p // 2][:, c2 * (p % 2):c2 * (p % 2 + 1)] for p in pairs]

    xv = unside([_dot(side(nk[2 * q], nk[2 * q + 1]), diag(vst[2 * q], vst[2 * q + 1]))
                 for q in quads])
    tinv = [side(eye + nab[2 * q], eye + nab[2 * q + 1]) for q in quads]
    npow = [bf(side(nab[2 * q], nab[2 * q + 1])) for q in quads]
    for _ in range(int(math.log2(c)) - 1):
        npow = [bf(_dot(npow[q], diag_halves(npow[q]))) for q in quads]
        tinv = [tinv[q] + _dot(bf(tinv[q]), diag_halves(npow[q])) for q in quads]
    tinv = unside(tinv)
    y = [_dot(bf(tinv[p]), jnp.concatenate([a2[p], bf(xv[p][0:c2, :])], axis=1)) for p in pairs]
    yb = [bf(y[p]) for p in pairs]
    z = [_dot(nrb[p], yb[p]) for p in pairs]
    mg = unside([_dot(bf(side(y[2 * q].T, y[2 * q + 1].T)), diag(bp2[2 * q], bp2[2 * q + 1]))
                 for q in quads])
    vk = unside([_dot(bf(side(vst[2 * q].astype(F32).T, vst[2 * q + 1].astype(F32).T)),
                      diag(kp2[2 * q], kp2[2 * q + 1])) for q in quads])
    for p in pairs:
        sl = slice(LANES * p, LANES * (p + 1))
        m_ref[:, sl] = bf(mg[p][0:c2, :])
        ga_ref[:, sl] = mg[p][c2:2 * c2, :] + vk[p]
        rq_ref[:, sl] = bf(r2[p] + z[p][:, 0:c2])
        o0 = z[p][:, c2:2 * c2] + xv[p][c2:2 * c2, :]
        o0_ref[:, sl] = o0[0:c, :] + o0[c:c2, :]


def _wkv_scan_kernel(m_ref, ga_ref, rq_ref, o0_ref, pc_ref, o_ref, g_scr, *, npairs):
    c = WKV_CHUNK

    @pl.when(pl.program_id(2) == 0)
    def _():
        g_scr[...] = jnp.zeros_like(g_scr)

    for p in range(npairs):
        sl = slice(LANES * p, LANES * (p + 1))
        g = g_scr[p]
        g_hi, g_lo = _split2(g)
        rq = rq_ref[:, sl]
        o_st = _dot_nt(rq, g_hi) + _dot_nt(rq, g_lo)
        o_ref[:, sl] = o_st[0:c, :] + o_st[c:2 * c, :] + o0_ref[:, sl]
        m = m_ref[:, sl]
        g_scr[p] = g * pc_ref[0:1, sl] + (_dot(g_hi, m) + _dot(g_lo, m)) + ga_ref[:, sl]


WKV_CHUNK_LANES = 2048


def _wkv_bidir(r, v, kk, lw, kd, asg, batch, ctx_len):
    n, d = r.shape
    c = WKV_CHUNK
    s = n // batch
    ncs = s // c
    ncc = ctx_len // c
    nch = n // c
    lanes = min(WKV_CHUNK_LANES, d)
    ngrp = d // lanes
    shared = pl.BlockSpec((c, lanes), lambda z, i, j: (i, j))
    perdir = pl.BlockSpec((None, c, lanes), lambda z, i, j: (z, i, j))
    big = pl.BlockSpec((None, 2 * c, lanes), lambda z, i, j: (z, i, j))
    m_, ga_, rq_, o0_, pc_ = pl.pallas_call(
        functools.partial(_wkv_chunk_kernel, npairs=lanes // LANES),
        grid=(2, nch, ngrp),
        in_specs=[shared, shared, shared, perdir, perdir, perdir],
        out_specs=[big, big, big, perdir,
                   pl.BlockSpec((None, SUBLANES, lanes), lambda z, i, j: (z, i, j))],
        out_shape=[jax.ShapeDtypeStruct((2, nch * 2 * c, d), dt) for dt in (BF16, F32, BF16)]
        + [jax.ShapeDtypeStruct((2, n, d), F32),
           jax.ShapeDtypeStruct((2, nch * SUBLANES, d), F32)],
        compiler_params=_cparams("parallel", "parallel", "parallel"),
        name="wkv_chunk",
    )(r, v, kk, lw, kd, asg)

    def cmap(z, b, cc):
        back = jnp.where(cc < ncc, ncc - 1 - cc, ncs + ncc - 1 - cc)
        return (z, b * ncs + jnp.where(z == 0, cc, back), 0)

    out = pl.pallas_call(
        functools.partial(_wkv_scan_kernel, npairs=d // LANES),
        grid=(2, batch, ncs),
        in_specs=[
            pl.BlockSpec((None, 2 * c, d), cmap),
            pl.BlockSpec((None, 2 * c, d), cmap),
            pl.BlockSpec((None, 2 * c, d), cmap),
            pl.BlockSpec((None, c, d), cmap),
            pl.BlockSpec((None, SUBLANES, d), cmap),
        ],
        out_specs=pl.BlockSpec((None, c, d), cmap),
        out_shape=jax.ShapeDtypeStruct((2, n, d), F32),
        scratch_shapes=[pltpu.VMEM((d // LANES, 2 * c, 2 * c), F32)],
        compiler_params=_cparams("parallel", "parallel", "arbitrary"),
        name="wkv_scan",
    )(m_, ga_, rq_, o0_, pc_)
    return out


def _rwread_kernel(o_ref, r_ref, kd_ref, v_ref, g_ref, pv_ref, bd_ref, y_ref):
    bd = bd_ref[...]
    wkv = o_ref[0] + o_ref[1]
    inv = 1.0 / RW_HEAD
    mu = _head_sum(wkv, bd) * inv
    dev = wkv - mu
    var = _head_sum(dev * dev, bd) * inv
    y = dev * lax.rsqrt(var + GN_EPS) * pv_ref[1:2, :] + pv_ref[2:3, :]
    rk = r_ref[...] * (kd_ref[0] + kd_ref[1]) * pv_ref[0:1, :]
    y = y + _head_sum(rk, bd) * v_ref[...]
    y_ref[...] = (y * g_ref[...]).astype(BF16)


def _rwkv_readout(geo, wkv, r, kd, v, g, pvec):
    n, d = r.shape
    tm, tc = geo.tm, 512
    ii = lax.broadcasted_iota(jnp.int32, (tc, tc), 0) // RW_HEAD
    jj = lax.broadcasted_iota(jnp.int32, (tc, tc), 1) // RW_HEAD
    bd = (ii == jj).astype(BF16)
    row = pl.BlockSpec((tm, tc), lambda i, j: (i, j))
    dirs = pl.BlockSpec((2, tm, tc), lambda i, j: (0, i, j))
    return pl.pallas_call(
        _rwread_kernel,
        grid=(geo.ntiles, d // tc),
        in_specs=[dirs, row, dirs, row, row,
                  pl.BlockSpec((SUBLANES, tc), lambda i, j: (0, j)),
                  pl.BlockSpec((tc, tc), lambda i, j: (0, 0))],
        out_specs=row,
        out_shape=jax.ShapeDtypeStruct((n, d), BF16),
        compiler_params=_cparams("parallel", "parallel"),
        name="rwkv_readout",
    )(wkv, r, kd, v, g, pvec, bd)


def _mlaqkv_kernel(z_ref, qn_ref, kvn_ref, wuq_ref, wukv_ref, gq_ref, gk_ref, cos_ref, sin_ref,
                   q_ref, k_ref, v_ref, *, q_lora, kv_lora):
    z = z_ref[...]
    cq = z[:, 0:q_lora]
    ckv = z[:, q_lora:q_lora + kv_lora]
    krot = z[:, q_lora + kv_lora:q_lora + kv_lora + LANES]

    def rms(x, g):
        ms = jnp.mean(x * x, axis=-1, keepdims=True)
        return x * lax.rsqrt(ms + EPS) * g

    qf = _dot(rms(cq, qn_ref[...]).astype(BF16), wuq_ref[...])
    kvf = _dot(rms(ckv, kvn_ref[...]).astype(BF16), wukv_ref[...])
    cos = cos_ref[...]
    sin = sin_ref[...]
    half = QK_ROPE // 2
    lane = lax.broadcasted_iota(jnp.int32, cos.shape, 1)

    def rope(x):
        up = pltpu.roll(x, LANES - half, axis=1)
        dn = pltpu.roll(x, half, axis=1)
        return x * cos + jnp.where(lane < half, up, dn) * sin

    inv_w = 1.0 / (QK_NOPE + QK_ROPE)
    gq = gq_ref[...]
    gk = gk_ref[...]
    kr_ss = jnp.sum(krot * krot, axis=-1, keepdims=True)
    for h in range(MLA_HEADS):
        o = QK_PAD * h
        qh = qf[:, o:o + QK_PAD]
        rs = lax.rsqrt(jnp.sum(qh * qh, axis=-1, keepdims=True) * inv_w + EPS)
        qn = qh * rs * gq * (ATTN_SCALE * math.log2(math.e))
        q_ref[:, o:o + QK_NOPE] = qn[:, 0:QK_NOPE].astype(BF16)
        q_ref[:, o + QK_NOPE:o + QK_PAD] = rope(qn[:, QK_NOPE:QK_PAD]).astype(BF16)
        kn = kvf[:, o:o + QK_NOPE]
        rsk = lax.rsqrt((jnp.sum(kn * kn, axis=-1, keepdims=True) + kr_ss) * inv_w + EPS)
        k_ref[:, o:o + QK_NOPE] = (kn * rsk * gk[:, 0:QK_NOPE]).astype(BF16)
        k_ref[:, o + QK_NOPE:o + QK_PAD] = rope(krot * rsk * gk[:, QK_NOPE:QK_PAD]).astype(BF16)
        v_ref[:, V_HEAD * h:V_HEAD * (h + 1)] = kvf[:, o + QK_NOPE:o + QK_PAD].astype(BF16)


def _mla_qkv(geo, z, qn, kvn, wuq_p, wukv, gq_p, gk_p, cos_t, sin_t):
    n, zw = z.shape
    tm = geo.tm
    q_lora, kv_lora = qn.shape[0], kvn.shape[0]
    hq = MLA_HEADS * QK_PAD
    tpb = geo.tpb
    full = lambda i: (0, 0)
    rowmap = lambda i: (i, 0)
    return pl.pallas_call(
        functools.partial(_mlaqkv_kernel, q_lora=q_lora, kv_lora=kv_lora),
        grid=(geo.ntiles,),
        in_specs=[
            pl.BlockSpec((tm, zw), rowmap),
            pl.BlockSpec((1, q_lora), full),
            pl.BlockSpec((1, kv_lora), full),
            pl.BlockSpec((q_lora, hq), full),
            pl.BlockSpec((kv_lora, hq), full),
            pl.BlockSpec((1, QK_PAD), full),
            pl.BlockSpec((1, QK_PAD), full),
            pl.BlockSpec((tm, LANES), lambda i: (i % tpb, 0)),
            pl.BlockSpec((tm, LANES), lambda i: (i % tpb, 0)),
        ],
        out_specs=[pl.BlockSpec((tm, hq), rowmap), pl.BlockSpec((tm, hq), rowmap),
                   pl.BlockSpec((tm, MLA_HEADS * V_HEAD), rowmap)],
        out_shape=[jax.ShapeDtypeStruct((n, hq), BF16), jax.ShapeDtypeStruct((n, hq), BF16),
                   jax.ShapeDtypeStruct((n, MLA_HEADS * V_HEAD), BF16)],
        compiler_params=_cparams("parallel"),
        name="mla_qkv",
    )(z, qn.reshape(1, -1), kvn.reshape(1, -1), wuq_p, wukv, gq_p, gk_p, cos_t, sin_t)


ATTN_SUBTILES = 2


def _attn_kernel(q_ref, k_ref, vt_ref, o_ref):
    k = k_ref[...]
    vt = vt_ref[...]
    cols = q_ref.shape[0] // ATTN_SUBTILES
    subs = range(ATTN_SUBTILES)
    s = [_dot_nt(k, q_ref[cols * i:cols * (i + 1), :]) for i in subs]
    m = [jnp.max(s[i], axis=0, keepdims=True) for i in subs]
    p = [jnp.exp2(s[i] - m[i]) for i in subs]
    l = [jnp.sum(p[i], axis=0, keepdims=True) for i in subs]
    for i in subs:
        ot = _dot(vt, p[i].astype(BF16)) / l[i]
        o_ref[cols * i:cols * (i + 1), :] = ot.T.astype(BF16)


def _attn_call(q3, k3, vt4, tq, nkeys):
    b, sq, _ = q3.shape
    return pl.pallas_call(
        _attn_kernel,
        grid=(b, MLA_HEADS, sq // tq),
        in_specs=[
            pl.BlockSpec((None, tq, QK_PAD), lambda bb, h, i: (bb, i, h)),
            pl.BlockSpec((None, nkeys, QK_PAD), lambda bb, h, i: (bb, 0, h)),
            pl.BlockSpec((None, None, V_HEAD, nkeys), lambda bb, h, i: (bb, h, 0, 0)),
        ],
        out_specs=pl.BlockSpec((None, tq, V_HEAD), lambda bb, h, i: (bb, i, h)),
        out_shape=jax.ShapeDtypeStruct((b, sq, MLA_HEADS * V_HEAD), BF16),
        compiler_params=_cparams("parallel", "parallel", "parallel"),
        name="mla_attention",
    )(q3, k3, vt4)


def _mla_attention(geo, q, k, v):
    b, s, l = geo.b, geo.s, geo.l
    q3 = q.reshape(b, s, MLA_HEADS * QK_PAD)
    k3 = k.reshape(b, s, MLA_HEADS * QK_PAD)
    vt4 = v.reshape(b, s, MLA_HEADS, V_HEAD).transpose(0, 2, 3, 1)
    o_ctx = _attn_call(q3[:, :l], k3, vt4, geo.tm, l)
    tq_lat = 2 * geo.tm if geo.t % (2 * geo.tm) == 0 else geo.tm
    o_lat = _attn_call(q3[:, l:], k3, vt4, tq_lat, s)
    return jnp.concatenate([o_ctx, o_lat], axis=1).reshape(b * s, MLA_HEADS * V_HEAD)


def _topk_rows(s, k, payload=None):
    rows = lax.broadcasted_iota(jnp.int32, s.shape, 0).astype(F32)
    big = float(s.shape[0])
    vals, idxs = [], []
    for _ in range(k):
        m = jnp.max(s, axis=0, keepdims=True)
        idx = jnp.min(jnp.where(s == m, rows, big), axis=0, keepdims=True)
        hit = rows == idx
        vals.append(m)
        if payload is None:
            idxs.append(idx)
        else:
            idxs.append(jnp.sum(jnp.where(hit, payload, 0.0), axis=0, keepdims=True))
        s = jnp.where(hit, -jnp.inf, s)
    return jnp.concatenate(vals, axis=0), jnp.concatenate(idxs, axis=0)


def _peersel_kernel(q_ref, qn_ref, keys_ref, e_ref, g_ref):
    q = q_ref[...]
    ms = jnp.mean(q * q, axis=-1, keepdims=True)
    qn = q * lax.rsqrt(ms + EPS) * qn_ref[...]
    half = D_KEY // 2
    s1 = _dot3_nt(keys_ref[0], qn[:, 0:half])
    s2 = _dot3_nt(keys_ref[1], qn[:, half:D_KEY])
    t1, i1 = _topk_rows(s1, PEER_TOPK)
    t2, i2 = _topk_rows(s2, PEER_TOPK)
    k = PEER_TOPK
    sub = lax.broadcasted_iota(jnp.int32, (SUBLANES, t1.shape[1]), 0)
    cand = [t1[0:1, :] + t2]
    cidx = [i1[0:1, :] * float(N_KEYS) + i2]
    for p in range(1, k // 2):
        live = sub < k // (p + 1)
        cand.append(jnp.where(live, t1[p:p + 1, :] + t2[0:SUBLANES, :], -jnp.inf))
        cidx.append(i1[p:p + 1, :] * float(N_KEYS) + i2[0:SUBLANES, :])
    cand.append(t1[k // 2:k, :] + t2[0:1, :])
    cidx.append(i1[k // 2:k, :] * float(N_KEYS) + i2[0:1, :])
    best, eidx = _topk_rows(jnp.concatenate(cand, axis=0), k, payload=jnp.concatenate(cidx, axis=0))
    ex = jnp.exp(best - jnp.max(best, axis=0, keepdims=True))
    g_ref[...] = ex / jnp.sum(ex, axis=0, keepdims=True)
    e_ref[...] = eidx.astype(jnp.int32)


def _peer_select(qp, q_norm, keys, tm, row0, n):
    assert row0 % tm == 0 and n % tm == 0
    tile0 = row0 // tm
    return pl.pallas_call(
        _peersel_kernel,
        grid=(n // tm, PEER_HEADS),
        in_specs=[
            pl.BlockSpec((tm, D_KEY), lambda i, h: (tile0 + i, h)),
            pl.BlockSpec((1, D_KEY), lambda i, h: (0, 0)),
            pl.BlockSpec((2, N_KEYS, D_KEY // 2), lambda i, h: (0, 0, 0)),
        ],
        out_specs=[pl.BlockSpec((PEER_TOPK, tm), lambda i, h: (h, i)),
                   pl.BlockSpec((PEER_TOPK, tm), lambda i, h: (h, i))],
        out_shape=[jax.ShapeDtypeStruct((PEER_HEADS * PEER_TOPK, n), jnp.int32),
                   jax.ShapeDtypeStruct((PEER_HEADS * PEER_TOPK, n), F32)],
        compiler_params=_cparams("parallel", "parallel"),
        name="peer_select",
    )(qp, q_norm.reshape(1, D_KEY), keys)


GATHER_SLOTS = 4
SLAB_PAD = 1


def _pack_expert_table(u, v):
    ne, d = u.shape
    ub = lax.bitcast_convert_type(u.astype(BF16), jnp.uint16).astype(jnp.uint32)
    vb = lax.bitcast_convert_type(v.astype(BF16), jnp.uint16).astype(jnp.uint32)
    return ((vb << 16) | ub).reshape(ne, d // LANES, LANES)


def _peergather_kernel(idx_ref, idxn_ref, gate_ref, h_ref, x_ref, mod_ref, tab_ref, o_ref,
                       *scratch, tb):
    nsel = PEER_HEADS * PEER_TOPK
    nrow = h_ref.shape[1] // LANES
    pitch = nrow + SLAB_PAD
    ns = GATHER_SLOTS
    bufs, sem = scratch[:ns], scratch[ns]
    lane_t = lax.broadcasted_iota(jnp.int32, (nsel, tb), 1)
    g2 = mod_ref[5:6, :]
    step = pl.program_id(0)
    nsteps = pl.num_programs(0)

    def row_copy(ids_ref, t, j, slot):
        return pltpu.make_async_copy(
            tab_ref.at[ids_ref[t, j]],
            bufs[slot].at[pl.ds(j * pitch, nrow), :],
            sem.at[slot])

    def issue(ids_ref, t, slot):
        for j in range(nsel):
            row_copy(ids_ref, t, j, slot).start(priority=j % 2)

    def wait(t, slot):
        for j in range(nsel):
            row_copy(idx_ref, t, j, slot).wait()

    def packed(slot, s):
        return bufs[slot][pl.ds(s, nsel, stride=pitch), :]

    def compute(t, slot):
        hrow = h_ref[pl.ds(t, 1), :]
        acc = jnp.zeros((nsel, LANES), F32)
        for s in range(nrow):
            u = lax.bitcast_convert_type(packed(slot, s) << 16, F32)
            acc = acc + u * hrow[:, LANES * s:LANES * (s + 1)]
        dots = jnp.sum(acc, axis=-1, keepdims=True)
        gcol = jnp.sum(jnp.where(lane_t == t, gate_ref[...], 0.0), axis=-1, keepdims=True)
        coef = gcol * _gelu(dots)
        outs = []
        for s in range(nrow):
            vv = lax.bitcast_convert_type(packed(slot, s) & jnp.uint32(0xFFFF0000), F32)
            outs.append(jnp.sum(coef * vv, axis=0, keepdims=True))
        orow = jnp.concatenate(outs, axis=1)
        o_ref[pl.ds(t, 1), :] = x_ref[pl.ds(t, 1), :] + g2 * orow

    @pl.when(step == 0)
    def _():
        for s in range(ns - 1):
            issue(idx_ref, s, s)

    ngroups = tb // ns

    def body(g, carry):
        for s in range(ns):
            t = g * ns + s
            wait(t, s)
            issue(idx_ref, t + ns - 1, (s + ns - 1) % ns)
            compute(t, s)
        return carry

    lax.fori_loop(0, ngroups - 1, body, 0)
    for s in range(ns):
        t = (ngroups - 1) * ns + s
        wait(t, s)
        if s == 0:
            issue(idx_ref, tb - 1, ns - 1)
        else:
            @pl.when(step < nsteps - 1)
            def _():
                issue(idxn_ref, s - 1, s - 1)
        compute(t, s)


def _peer_gather(geo_g, eidx, gate_t, h2, x, mods, table, n):
    d = x.shape[1]
    tb = geo_g.tm
    nsel = PEER_HEADS * PEER_TOPK
    pitch = d // LANES + SLAB_PAD
    assert n % tb == 0
    nsteps = n // tb
    return pl.pallas_call(
        functools.partial(_peergather_kernel, tb=tb),
        grid=(nsteps,),
        in_specs=[
            pl.BlockSpec((tb, nsel), lambda i: (i, 0), memory_space=pltpu.SMEM),
            pl.BlockSpec((tb, nsel), lambda i: (jnp.minimum(i + 1, nsteps - 1), 0),
                         memory_space=pltpu.SMEM),
            pl.BlockSpec((nsel, tb), lambda i: (0, i)),
            pl.BlockSpec((tb, d), lambda i: (i, 0)),
            pl.BlockSpec((tb, d), lambda i: (i, 0)),
            geo_g.mod_spec(),
            pl.BlockSpec(memory_space=pl.ANY),
        ],
        out_specs=pl.BlockSpec((tb, d), lambda i: (i, 0)),
        out_shape=jax.ShapeDtypeStruct((n, d), F32),
        scratch_shapes=[pltpu.VMEM((nsel * pitch, LANES), jnp.uint32)] * GATHER_SLOTS
        + [pltpu.SemaphoreType.DMA((GATHER_SLOTS,))],
        compiler_params=_cparams("arbitrary"),
        name="peer_gather",
    )(eidx, eidx, gate_t, h2, x, mods, table)


SC_CORES = 2
SC_SUBCORES = 16
SC_LANES = 16
SC_CHUNK = 16
SC_UNROLL = 4
SC_SHARE = 0.3235

_ERF_ALPHA = (0.00022905065861350646, 0.0034082910107109506, 0.050955695062380861,
              0.18520832239976145, 1.128379143519084)
_ERF_BETA = (-1.1791602954361697e-7, 0.000023547966471313185, 0.0010179625278914885,
             0.014070470171167667, 0.11098505178285362, 0.49746925110067538, 1.0)
_ERF_CLAMP = 3.832506856900711


def _erf_rational(x):
    x = jnp.minimum(jnp.maximum(x, -_ERF_CLAMP), _ERF_CLAMP)
    x2 = x * x
    p = jnp.full_like(x, _ERF_ALPHA[0])
    for c in _ERF_ALPHA[1:]:
        p = p * x2 + c
    q = jnp.full_like(x, _ERF_BETA[0])
    for c in _ERF_BETA[1:]:
        q = q * x2 + c
    return x * p / q


def _peer_sc(eidx, gate, h, table, row0):
    n, nsel = eidx.shape
    d = h.shape[1]
    nw = SC_CORES * SC_SUBCORES
    assert n % (2 * nw) == 0 and nsel % SC_CHUNK == 0 and d % SC_LANES == 0
    tpw = n // nw
    nchunk = nsel // SC_CHUNK
    nvec = d // SC_LANES
    mesh = plsc.VectorSubcoreMesh(core_axis_name="c", subcore_axis_name="s",
                                  num_cores=SC_CORES, num_subcores=SC_SUBCORES)

    @functools.partial(
        pl.kernel, mesh=mesh,
        out_type=jax.ShapeDtypeStruct((n, d), F32),
        scratch_types=[
            pltpu.VMEM((nsel,), jnp.int32), pltpu.VMEM((nsel,), jnp.int32),
            pltpu.VMEM((nsel,), F32), pltpu.VMEM((nsel,), F32),
            pltpu.VMEM((d,), F32), pltpu.VMEM((d,), F32),
            pltpu.VMEM((d,), F32),
            pltpu.VMEM((SC_CHUNK, d), jnp.uint32),
            pltpu.VMEM((SC_CHUNK, d), jnp.uint32),
            pltpu.SemaphoreType.DMA, pltpu.SemaphoreType.DMA,
            pltpu.SemaphoreType.DMA, pltpu.SemaphoreType.DMA,
        ],
        compiler_params=pltpu.CompilerParams(needs_layout_passes=False),
        name="peer_sc",
    )
    def sc_kernel(eidx_hbm, gate_hbm, h_hbm, tab_hbm, y_hbm, idx_a, idx_b, gate_a, gate_b, h_a, h_b,
                  o_v, rows0, rows1, sem0, sem1, msem_a, msem_b):
        wid = lax.axis_index("s") * SC_CORES + lax.axis_index("c")
        lanes = lax.iota(jnp.int32, SC_LANES)
        zero = jnp.zeros((SC_LANES,), F32)
        bufs = ((rows0, sem0), (rows1, sem1))
        meta = ((idx_a, gate_a, h_a, msem_a), (idx_b, gate_b, h_b, msem_b))

        def meta_copies(tok, s):
            idx_v, gate_v, h_v, msem = meta[s]
            return (pltpu.make_async_copy(eidx_hbm.at[tok], idx_v, msem),
                    pltpu.make_async_copy(gate_hbm.at[tok], gate_v, msem),
                    pltpu.make_async_copy(h_hbm.at[row0 + tok], h_v, msem))

        def chunk_start(c):
            return c * SC_CHUNK if isinstance(c, int) else pl.multiple_of(c * SC_CHUNK, SC_CHUNK)

        def gather(s, c, slot):
            rows, sem = bufs[slot]
            return pltpu.make_async_copy(
                tab_hbm.at[meta[s][0].at[pl.ds(chunk_start(c), SC_CHUNK)]], rows, sem)

        def chunk_compute(s, c, rows):
            _, gate_v, h_v, _ = meta[s]

            def dot_body(i, accs):
                hv = h_v[pl.ds(i * SC_LANES, SC_LANES)]
                out = []
                for e in range(SC_CHUNK):
                    w = rows[e, pl.ds(i * SC_LANES, SC_LANES)]
                    out.append(accs[e] + lax.bitcast_convert_type(w << 16, F32) * hv)
                return tuple(out)

            accs = lax.fori_loop(0, nvec, dot_body, (zero,) * SC_CHUNK, unroll=SC_UNROLL)
            dots = zero
            for e in range(SC_CHUNK):
                dots = jnp.where(lanes == e, jnp.sum(accs[e]), dots)
            act = 0.5 * dots * (1.0 + _erf_rational(dots * (2.0 ** -0.5)))
            coef = gate_v[pl.ds(chunk_start(c), SC_CHUNK)] * act
            splat = [jnp.full((SC_LANES,), jnp.sum(jnp.where(lanes == e, coef, 0.0)), F32)
                     for e in range(SC_CHUNK)]

            def acc_body(i, carry):
                o = o_v[pl.ds(i * SC_LANES, SC_LANES)]
                for e in range(SC_CHUNK):
                    w = rows[e, pl.ds(i * SC_LANES, SC_LANES)]
                    o = o + splat[e] * lax.bitcast_convert_type(w & jnp.uint32(0xFFFF0000), F32)
                o_v[pl.ds(i * SC_LANES, SC_LANES)] = o
                return carry

            lax.fori_loop(0, nvec, acc_body, 0, unroll=SC_UNROLL)

        def zero_body(i, c2):
            o_v[pl.ds(i * SC_LANES, SC_LANES)] = zero
            return c2

        def token(tok, s, has_next):
            def when_next(fn):
                if isinstance(has_next, bool):
                    if has_next:
                        fn()
                else:
                    pl.when(has_next)(fn)

            def load_next():
                for cp in meta_copies(tok + 1, 1 - s):
                    cp.start()

            when_next(load_next)
            lax.fori_loop(0, nvec, zero_body, 0)

            def chunk_pair(cp, carry):
                c0 = 2 * cp
                gather(s, c0 + 1, 1).start()
                gather(s, c0, 0).wait()
                chunk_compute(s, c0, rows0)

                @pl.when(c0 + 2 < nchunk)
                def _():
                    gather(s, c0 + 2, 0).start()

                gather(s, c0 + 1, 1).wait()
                chunk_compute(s, c0 + 1, rows1)
                return carry

            lax.fori_loop(0, nchunk // 2, chunk_pair, 0)

            def prefetch_next():
                for cp in meta_copies(tok + 1, 1 - s):
                    cp.wait()
                gather(1 - s, 0, 0).start()

            when_next(prefetch_next)
            pltpu.sync_copy(o_v, y_hbm.at[tok])

        base = wid * tpw
        for cp in meta_copies(base, 0):
            cp.start()
        for cp in meta_copies(base, 0):
            cp.wait()
        gather(0, 0, 0).start()

        def pair_body(g, carry):
            token(base + 2 * g, 0, True)
            token(base + 2 * g + 1, 1, g + 1 < tpw // 2)
            return carry

        lax.fori_loop(0, tpw // 2, pair_body, 0)

    return sc_kernel(eidx, gate, h, table)


def _pad_to(x, axis, size):
    pad = [(0, 0)] * x.ndim
    pad[axis] = (0, size - x.shape[axis])
    return jnp.pad(x, pad)


def _rwkv_layer(geo, xs, mods, norm1, mix, w_rkv, w_o, w0, w1, w2, a0, a1, a2, vl, g1, g2,
                k_k, k_a, r_k, ln_w, ln_b, vfirst):
    d = geo.d
    xm = _rwkv_mix(geo, xs, norm1, mods, mix)
    rkv = _bmm(xm, w_rkv.astype(BF16), (0, 2, 3), ("none",) * 3, F32, geo.tm)
    lt = g1.shape[1]
    w1c = jnp.concatenate([_pad_to(w1[0], 1, LORA_PAD), _pad_to(w1[1], 1, LORA_PAD)], axis=1)
    a1c = jnp.concatenate([_pad_to(a1[0], 1, LORA_PAD), _pad_to(a1[1], 1, LORA_PAD)], axis=1)
    if vl is None:
        v1p = jnp.zeros((d, lt), F32)
        v2p = jnp.zeros((LORA_PAD, d), F32)
        v0 = jnp.zeros((d,), F32)
    else:
        v0, v1, v2 = vl
        v1p = _pad_to(v1, 1, lt)
        v2p = _pad_to(v2, 0, LORA_PAD)
    wl1 = jnp.stack([_pad_to(w1c, 1, lt), _pad_to(a1c, 1, lt), g1, v1p]).astype(BF16)
    tl = _bmm(xm, wl1, (1, 4, 5, 3), ("tanh", "none", "sigmoid", "none"), BF16, geo.tm)
    w2p = jnp.stack([_pad_to(w2[0], 0, LORA_PAD), _pad_to(w2[1], 0, LORA_PAD)]).astype(BF16)
    a2p = jnp.stack([_pad_to(a2[0], 0, LORA_PAD), _pad_to(a2[1], 0, LORA_PAD)]).astype(BF16)
    pvec = jnp.stack([w0[0], w0[1], a0[0], a0[1], v0, k_k, k_a, jnp.zeros_like(k_k)])
    lw, kd, asg, kk, g, v = _rwkv_feat(geo, rkv, tl, w2p, a2p, g2.astype(BF16), v2p.astype(BF16),
                                      pvec, vfirst)
    r = rkv[0]
    wkv = _wkv_bidir(r, v, kk, lw, kd, asg, geo.b, geo.l)
    pv2 = _pad_to(jnp.stack([r_k, ln_w, ln_b]), 0, SUBLANES)
    y = _rwkv_readout(geo, wkv, r, kd, v, g, pv2)
    xs = _matmul_res(geo, y, w_o.astype(BF16), xs, mods, 2)
    return xs, v


def _rope_tables(geo):
    t = geo.t
    pos = jnp.arange(t)
    row = (pos // GRID_W).astype(F32)
    col = (pos % GRID_W).astype(F32)
    n_freq = QK_ROPE // 4
    inv_freq = ROPE_THETA ** (-jnp.arange(n_freq, dtype=F32) / n_freq)
    ang = jnp.concatenate([row[:, None] * inv_freq, col[:, None] * inv_freq], axis=-1)
    cos, sin = jnp.cos(ang), jnp.sin(ang)
    pad = LANES - QK_ROPE
    cos_l = jnp.concatenate([cos, cos, jnp.ones((t, pad), F32)], axis=1)
    sin_l = jnp.concatenate([-sin, sin, jnp.zeros((t, pad), F32)], axis=1)
    cos_c = jnp.ones((geo.l, LANES), F32)
    sin_c = jnp.zeros((geo.l, LANES), F32)
    return jnp.concatenate([cos_c, cos_l], axis=0), jnp.concatenate([sin_c, sin_l], axis=0)


def _mla_layer(geo, xs, mods, norm1, rope_t, w_in, q_norm, kv_norm, w_uq, w_ukv, g_q, g_k, w_o):
    q_lora, kv_lora = q_norm.shape[0], kv_norm.shape[0]
    zw = q_lora + kv_lora + LANES
    z = _mod_matmul(geo, xs, norm1, mods, _pad_to(w_in, 1, zw).astype(BF16), 0, False)
    qk = QK_NOPE + QK_ROPE
    wuq_p = _pad_to(w_uq.reshape(q_lora, MLA_HEADS, qk), 2, QK_PAD).reshape(q_lora, -1)
    gq_p = _pad_to(g_q, 0, QK_PAD).reshape(1, QK_PAD)
    gk_p = _pad_to(g_k, 0, QK_PAD).reshape(1, QK_PAD)
    q, k, v = _mla_qkv(geo, z, q_norm, kv_norm, wuq_p.astype(BF16), w_ukv.astype(BF16),
                       gq_p, gk_p, *rope_t)
    o = _mla_attention(geo, q, k, v)
    return _matmul_res(geo, o, w_o.astype(BF16), xs, mods, 2)


def _peer_layer(geo, geo_g, xs, mods, norm2, w_q, q_norm, keys, u, v):
    qp, h2 = _mod_matmul(geo, xs, norm2, mods, w_q.astype(BF16), 1, True)
    table = _pack_expert_table(u, v)
    n, d = xs.shape
    n_sc = geo.tm * round(SC_SHARE * n / geo.tm)
    if n_sc % (2 * SC_CORES * SC_SUBCORES) != 0:
        n_sc = 0
    n_tc = n - n_sc

    def select(row0, nrows):
        wide = 2 * geo.tm
        tsel = wide if row0 % wide == 0 and nrows % wide == 0 else geo.tm
        return _peer_select(qp, q_norm, keys, tsel, row0, nrows)

    if n_sc:
        eidx_sc, gate_sc = select(n_tc, n_sc)
        y_sc = _peer_sc(eidx_sc.T, gate_sc.T, h2, table.reshape(table.shape[0], d), n_tc)
    eidx_t, gate_t = select(0, n_tc)
    out_tc = _peer_gather(geo_g, eidx_t.T, gate_t, h2, xs, mods, table, n_tc)
    if n_sc == 0:
        return out_tc
    g2, r = [], n_tc
    while r < n:
        bi, pos = divmod(r, geo.s)
        seg = int(pos >= geo.l)
        stop = min(n, bi * geo.s + (geo.s if seg else geo.l))
        g2.append(jnp.broadcast_to(mods[bi, seg, 5], (stop - r, d)))
        r = stop
    out_sc = xs[n_tc:] + jnp.concatenate(g2, axis=0) * y_sc
    return jnp.concatenate([out_tc, out_sc], axis=0)


def kernel(x, c, ctx, c_ctx, w_ada, b_ada, norm1, norm2, rw_mix, rw_wrkv, rw_wo, rw_w0, rw_w1, rw_w2, rw_a0, rw_a1, rw_a2, rw_v0, rw_v1, rw_v2, rw_g1, rw_g2, rw_kk, rw_ka, rw_rk, rw_lnw, rw_lnb, mla_win, mla_qnorm, mla_kvnorm, mla_wuq, mla_wukv, mla_gq, mla_gk, mla_wo, peer_wq, peer_qnorm, peer_keys, peer_u, peer_v):
    b, t, d = x.shape
    l = ctx.shape[1]
    depth = w_ada.shape[0]
    geo = _Geom(b, l, t, d, min(256, l))
    geo_g = _Geom(b, l, t, d, min(128, l))
    cond8 = _pad_to(jnp.concatenate([c, c_ctx[None, :]], axis=0), 0, SUBLANES)
    ada = _adaln(cond8, w_ada, b_ada).reshape(depth, SUBLANES, 6, d)
    mods_all = jnp.stack([jnp.broadcast_to(ada[:, b:b + 1], (depth, b, 6, d)), ada[:, 0:b]], axis=2)
    xs = jnp.concatenate([ctx, x], axis=1).reshape(b * (l + t), d)
    rope_t = _rope_tables(geo)
    vfirst = None
    for i in range(depth):
        j = i // 2
        mods = mods_all[i]
        if i % 2 == 0:
            vl = None if j == 0 else (rw_v0[j - 1], rw_v1[j - 1], rw_v2[j - 1])
            xs, vcur = _rwkv_layer(geo, xs, mods, norm1[i], rw_mix[j], rw_wrkv[j], rw_wo[j],
                                   rw_w0[j], rw_w1[j], rw_w2[j], rw_a0[j], rw_a1[j], rw_a2[j], vl,
                                   rw_g1[j], rw_g2[j], rw_kk[j], rw_ka[j], rw_rk[j], rw_lnw[j],
                                   rw_lnb[j], vfirst)
            if j == 0:
                vfirst = vcur
        else:
            xs = _mla_layer(geo, xs, mods, norm1[i], rope_t, mla_win[j], mla_qnorm[j],
                            mla_kvnorm[j], mla_wuq[j], mla_wukv[j], mla_gq[j], mla_gk[j], mla_wo[j])
        if i == depth - 1:
            xs = xs.reshape(b, l + t, d)[:, l:, :].reshape(b * t, d)
            geo, geo_g = _Geom(b, 0, t, d, geo.tm), _Geom(b, 0, t, d, geo_g.tm)
        xs = _peer_layer(geo, geo_g, xs, mods, norm2[i], peer_wq[i], peer_qnorm[i], peer_keys[i],
                         peer_u[i], peer_v[i])
    return xs.reshape(b, t, d)
```

```python
import functools
import math

import jax
import jax.numpy as jnp
from jax import lax
from jax.experimental import pallas as pl
from jax.experimental.pallas import tpu as pltpu
from jax.experimental.pallas import tpu_sc as plsc

F32 = jnp.float32
BF16 = jnp.bfloat16

EPS = 1e-6
GN_EPS = 64e-5
RW_HEAD = 64
WKV_CHUNK = 64
MLA_HEADS = 16
QK_NOPE = 128
QK_ROPE = 64
V_HEAD = 128
QK_PAD = 256
ROPE_THETA = 10000.0
GRID_W = 64
ATTN_SCALE = (QK_NOPE + QK_ROPE) ** -0.5
PEER_HEADS = 8
N_KEYS = 128
PEER_TOPK = 16
D_KEY = 256
LORA_PAD = 128

LANES = 128
SUBLANES = 8
VMEM_LIMIT = 56 * 1024 * 1024
ROW_TILE = 256
GATHER_TILE = 128
COL_TILE = 512
ADA_COLS = 1024


def _cparams(*sem):
    return pltpu.CompilerParams(dimension_semantics=sem, vmem_limit_bytes=VMEM_LIMIT)


def _dot(a, b):
    return jnp.dot(a, b, preferred_element_type=F32)


def _dot_nt(a, b):
    return lax.dot_general(a, b, (((1,), (1,)), ((), ())), preferred_element_type=F32)


def _split2(x):
    hi = x.astype(BF16)
    lo = (x - hi.astype(F32)).astype(BF16)
    return hi, lo


def _split3(x):
    hi = x.astype(BF16)
    r1 = x - hi.astype(F32)
    mid = r1.astype(BF16)
    lo = (r1 - mid.astype(F32)).astype(BF16)
    return hi, mid, lo


def _dot3(a, b):
    ah, al = _split2(a)
    bh, bl = _split2(b)
    return _dot(ah, bh) + (_dot(ah, bl) + _dot(al, bh))


def _dot3_nt(a, b):
    ah, al = _split2(a)
    bh, bl = _split2(b)
    return _dot_nt(ah, bh) + (_dot_nt(ah, bl) + _dot_nt(al, bh))


def _dot_exact_lhs(sel, x):
    hi, mid, lo = _split3(x)
    return _dot(sel, hi) + (_dot(sel, mid) + _dot(sel, lo))


def _modulate(x, g, shift, scale):
    ms = jnp.mean(x * x, axis=-1, keepdims=True)
    return (x * lax.rsqrt(ms + EPS) * g) * (1.0 + scale) + shift


def _sigmoid(x):
    return 1.0 / (1.0 + jnp.exp(-x))


def _softplus(y):
    return jnp.maximum(y, 0.0) + jnp.log(1.0 + jnp.exp(-jnp.abs(y)))


def _erf(x):
    return lax.erf(x)


def _gelu(x):
    return 0.5 * x * (1.0 + _erf(x * (2.0 ** -0.5)))


def _ada_kernel(s_ref, w_ref, b_ref, o_ref):
    s = s_ref[...]
    s = s * _sigmoid(s)
    o_ref[...] = _dot3(s, w_ref[...]) + b_ref[...]


def _adaln(cond8, w_ada, b_ada):
    depth, d, n = w_ada.shape
    tn = min(ADA_COLS, n)
    return pl.pallas_call(
        _ada_kernel,
        grid=(depth, n // tn),
        in_specs=[
            pl.BlockSpec((SUBLANES, d), lambda l, j: (0, 0)),
            pl.BlockSpec((None, d, tn), lambda l, j: (l, 0, j)),
            pl.BlockSpec((None, 1, tn), lambda l, j: (l, 0, j)),
        ],
        out_specs=pl.BlockSpec((None, SUBLANES, tn), lambda l, j: (l, 0, j)),
        out_shape=jax.ShapeDtypeStruct((depth, SUBLANES, n), F32),
        compiler_params=_cparams("parallel", "parallel"),
        name="adaln",
    )(cond8, w_ada, b_ada.reshape(depth, 1, n))


class _Geom:
    def __init__(self, batch, ctx_len, seq_len, d_model, tm):
        self.b, self.l, self.t, self.d = batch, ctx_len, seq_len, d_model
        self.s = ctx_len + seq_len
        self.n = batch * self.s
        self.tm = tm
        assert ctx_len % tm == 0 and seq_len % tm == 0
        self.tpb = self.s // tm
        self.nct = ctx_len // tm
        self.ntiles = self.n // tm

    def mod_spec(self, nlead=0):
        tpb, nct = self.tpb, self.nct

        def imap(*ids):
            i = ids[nlead]
            return (i // tpb, ((i % tpb) >= nct).astype(jnp.int32), 0, 0)

        return pl.BlockSpec((None, None, 6, self.d), imap)


def _modmm_kernel(x_ref, g_ref, mod_ref, w_ref, o_ref, *h_ref, which):
    h = _modulate(x_ref[...], g_ref[...], mod_ref[3 * which:3 * which + 1, :],
                  mod_ref[3 * which + 1:3 * which + 2, :])
    if h_ref:
        h_ref[0][...] = h
    o_ref[...] = _dot(h.astype(BF16), w_ref[...])


def _mod_matmul(geo, x, g, mods, w, which, emit_h):
    n, d = x.shape
    nn = w.shape[1]
    tm = geo.tm
    out_shape = [jax.ShapeDtypeStruct((n, nn), F32)]
    out_specs = [pl.BlockSpec((tm, nn), lambda i: (i, 0))]
    if emit_h:
        out_shape.append(jax.ShapeDtypeStruct((n, d), F32))
        out_specs.append(pl.BlockSpec((tm, d), lambda i: (i, 0)))
    res = pl.pallas_call(
        functools.partial(_modmm_kernel, which=which),
        grid=(geo.ntiles,),
        in_specs=[
            pl.BlockSpec((tm, d), lambda i: (i, 0)),
            pl.BlockSpec((1, d), lambda i: (0, 0)),
            geo.mod_spec(),
            pl.BlockSpec((d, nn), lambda i: (0, 0)),
        ],
        out_specs=out_specs,
        out_shape=out_shape,
        compiler_params=_cparams("parallel"),
        name="mod_matmul",
    )(x, g.reshape(1, d), mods, w)
    return res if emit_h else res[0]


def _mmres_kernel(y_ref, w_ref, x_ref, mod_ref, o_ref, *, gidx):
    acc = _dot(y_ref[...], w_ref[...])
    o_ref[...] = x_ref[...] + mod_ref[gidx:gidx + 1, :] * acc


def _matmul_res(geo, y, w, x, mods, gidx):
    n, k = y.shape
    d = x.shape[1]
    tm = geo.tm
    return pl.pallas_call(
        functools.partial(_mmres_kernel, gidx=gidx),
        grid=(geo.ntiles,),
        in_specs=[
            pl.BlockSpec((tm, k), lambda i: (i, 0)),
            pl.BlockSpec((k, d), lambda i: (0, 0)),
            pl.BlockSpec((tm, d), lambda i: (i, 0)),
            geo.mod_spec(),
        ],
        out_specs=pl.BlockSpec((tm, d), lambda i: (i, 0)),
        out_shape=jax.ShapeDtypeStruct((n, d), F32),
        compiler_params=_cparams("parallel"),
        name="matmul_res",
    )(y, w, x, mods)


def _bmm_kernel(x_ref, w_ref, o_ref, *, acts):
    j = pl.program_id(0)
    y = _dot(x_ref[...], w_ref[...])
    out = y
    for jj, a in enumerate(acts):
        if a == "tanh":
            out = jnp.where(j == jj, jnp.tanh(y), out)
        elif a == "sigmoid":
            out = jnp.where(j == jj, _sigmoid(y), out)
    o_ref[...] = out.astype(o_ref.dtype)


def _bmm(x3, w3, src, acts, out_dtype, tm):
    _, n, k = x3.shape
    nj, _, nn = w3.shape
    src = tuple(src)

    def xmap(j, i):
        idx = jnp.int32(src[0])
        for jj in range(1, nj):
            idx = jnp.where(j == jj, jnp.int32(src[jj]), idx)
        return (idx, i, 0)

    return pl.pallas_call(
        functools.partial(_bmm_kernel, acts=tuple(acts)),
        grid=(nj, n // tm),
        in_specs=[
            pl.BlockSpec((None, tm, k), xmap),
            pl.BlockSpec((None, k, nn), lambda j, i: (j, 0, 0)),
        ],
        out_specs=pl.BlockSpec((None, tm, nn), lambda j, i: (j, i, 0)),
        out_shape=jax.ShapeDtypeStruct((nj, n, nn), out_dtype),
        compiler_params=_cparams("parallel", "parallel"),
        name="bmm",
    )(x3, w3)


def _rwmix_kernel(x_ref, xp_ref, xn_ref, g_ref, mod_ref, mix_ref, o_ref, *, tpb, nct):
    i = pl.program_id(0)
    tm = x_ref.shape[0]
    g = g_ref[...]
    shift = mod_ref[0:1, :]
    scale = mod_ref[1:2, :]
    h = _modulate(x_ref[...], g, shift, scale)
    hp = _modulate(xp_ref[...], g, shift, scale)[SUBLANES - 1:SUBLANES, :]
    hn = _modulate(xn_ref[...], g, shift, scale)[0:1, :]
    it = i % tpb
    first = jnp.logical_or(it == 0, it == nct)
    last = jnp.logical_or(it == nct - 1, it == tpb - 1)
    hp = jnp.where(first, 0.0, hp)
    hn = jnp.where(last, 0.0, hn)
    rows = lax.broadcasted_iota(jnp.int32, h.shape, 0)
    prev = jnp.where(rows == 0, hp, pltpu.roll(h, 1, axis=0))
    nxt = jnp.where(rows == tm - 1, hn, pltpu.roll(h, tm - 1, axis=0))
    xx = 0.5 * (prev + nxt) - h
    for m in range(6):
        o_ref[m] = (h + xx * mix_ref[m:m + 1, :]).astype(BF16)


def _rwkv_mix(geo, x, g, mods, mix):
    n, d = x.shape
    tm = geo.tm
    r8 = tm // SUBLANES
    nblk8 = n // SUBLANES
    return pl.pallas_call(
        functools.partial(_rwmix_kernel, tpb=geo.tpb, nct=geo.nct),
        grid=(geo.ntiles,),
        in_specs=[
            pl.BlockSpec((tm, d), lambda i: (i, 0)),
            pl.BlockSpec((SUBLANES, d), lambda i: (jnp.maximum(i * r8 - 1, 0), 0)),
            pl.BlockSpec((SUBLANES, d), lambda i: (jnp.minimum((i + 1) * r8, nblk8 - 1), 0)),
            pl.BlockSpec((1, d), lambda i: (0, 0)),
            geo.mod_spec(),
            pl.BlockSpec((6, d), lambda i: (0, 0)),
        ],
        out_specs=pl.BlockSpec((6, tm, d), lambda i: (0, i, 0)),
        out_shape=jax.ShapeDtypeStruct((6, n, d), BF16),
        compiler_params=_cparams("parallel"),
        name="rwkv_mix",
    )(x, x, x, g.reshape(1, d), mods, mix)


def _head_sum(x, bd):
    hi, lo = _split2(x)
    return _dot(hi, bd) + _dot(lo, bd)


def _rwfeat_kernel(k_ref, v_ref, tl_ref, w2_ref, a2_ref, g2_ref, v2_ref, pv_ref, bd_ref,
                   *rest, has_vlora):
    if has_vlora:
        vf_ref, lw_ref, kd_ref, as_ref, kk_ref, g_ref, vo_ref = rest
    else:
        lw_ref, kd_ref, as_ref, kk_ref, g_ref = rest
    k = k_ref[...]
    tw = tl_ref[0]
    ta = tl_ref[1]
    tg = tl_ref[2]
    w0 = pv_ref[0:2, :]
    a0 = pv_ref[2:4, :]
    k_k = pv_ref[5:6, :]
    k_a = pv_ref[6:7, :]
    for z in range(2):
        sl = slice(LORA_PAD * z, LORA_PAD * (z + 1))
        lora_w = _dot(tw[:, sl], w2_ref[z])
        w = -_softplus(-(w0[z:z + 1, :] + lora_w)) - 0.5
        lw_ref[z] = -jnp.exp(w)
        a_sig = _sigmoid(a0[z:z + 1, :] + _dot(ta[:, sl], a2_ref[z]))
        as_ref[z] = a_sig.astype(BF16)
        kd_ref[z] = k * (1.0 + (a_sig - 1.0) * k_a)
    g_ref[...] = _dot(tg, g2_ref[...]).astype(BF16)
    kkr = k * k_k
    ss = _head_sum(kkr * kkr, bd_ref[...])
    kk_ref[...] = (kkr * lax.rsqrt(ss + 1e-12)).astype(BF16)
    if has_vlora:
        v = v_ref[...]
        tv = tl_ref[3]
        gate = _sigmoid(pv_ref[4:5, :] + _dot(tv[:, 0:LORA_PAD], v2_ref[...]))
        vo_ref[...] = v + (vf_ref[...] - v) * gate


def _rwkv_feat(geo, rkv, tl, w2p, a2p, g2, v2p, pvec, vfirst):
    _, n, d = rkv.shape
    tm, tc = geo.tm, COL_TILE
    has_vlora = vfirst is not None
    lt = tl.shape[2]
    ii = lax.broadcasted_iota(jnp.int32, (tc, tc), 0) // RW_HEAD
    jj = lax.broadcasted_iota(jnp.int32, (tc, tc), 1) // RW_HEAD
    bd = (ii == jj).astype(BF16)
    row = lambda i, j: (i, j)
    in_specs = [
        pl.BlockSpec((None, tm, tc), lambda i, j: (1, i, j)),
        pl.BlockSpec((None, tm, tc), lambda i, j: (2, i, j)),
        pl.BlockSpec((4, tm, lt), lambda i, j: (0, i, 0)),
        pl.BlockSpec((2, LORA_PAD, tc), lambda i, j: (0, 0, j)),
        pl.BlockSpec((2, LORA_PAD, tc), lambda i, j: (0, 0, j)),
        pl.BlockSpec((lt, tc), lambda i, j: (0, j)),
        pl.BlockSpec((LORA_PAD, tc), lambda i, j: (0, j)),
        pl.BlockSpec((SUBLANES, tc), lambda i, j: (0, j)),
        pl.BlockSpec((tc, tc), lambda i, j: (0, 0)),
    ]
    args = [rkv, rkv, tl, w2p, a2p, g2, v2p, pvec, bd]
    dir_spec = pl.BlockSpec((2, tm, tc), lambda i, j: (0, i, j))
    out_specs = [dir_spec, dir_spec, dir_spec, pl.BlockSpec((tm, tc), row), pl.BlockSpec((tm, tc), row)]
    out_shape = [jax.ShapeDtypeStruct((2, n, d), dt) for dt in (F32, F32, BF16)]
    out_shape += [jax.ShapeDtypeStruct((n, d), BF16)] * 2
    if has_vlora:
        in_specs.append(pl.BlockSpec((tm, tc), row))
        args.append(vfirst)
        out_specs.append(pl.BlockSpec((tm, tc), row))
        out_shape.append(jax.ShapeDtypeStruct((n, d), F32))
    res = pl.pallas_call(
        functools.partial(_rwfeat_kernel, has_vlora=has_vlora),
        grid=(geo.ntiles, d // tc),
        in_specs=in_specs,
        out_specs=out_specs,
        out_shape=out_shape,
        compiler_params=_cparams("parallel", "parallel"),
        name="rwkv_feat",
    )(*args)
    if has_vlora:
        lw, kd, asg, kk, g, v = res
    else:
        lw, kd, asg, kk, g = res
        v = rkv[2]
    return lw, kd, asg, kk, g, v


def _wkv_chunk_kernel(r_ref, v_ref, kk_ref, lw_ref, kd_ref, as_ref,
                      m_ref, ga_ref, rq_ref, o0_ref, pc_ref, *, npairs):
    c = WKV_CHUNK
    c2 = 2 * c
    sgn = 1 - 2 * pl.program_id(0)
    ri = lax.broadcasted_iota(jnp.int32, (c2, c2), 0)
    ci = lax.broadcasted_iota(jnp.int32, (c2, c2), 1)
    same = (ri >= c) == (ci >= c)
    tt = jnp.where(ri >= c, ri - c, ri)
    ss = jnp.where(ci >= c, ci - c, ci)
    earlier = (ss - tt) * sgn < 0
    strict = jnp.logical_and(same, earlier)
    incl = jnp.logical_and(same, jnp.logical_or(earlier, ss == tt))
    eye = (ri == ci).astype(F32)
    r64 = lax.broadcasted_iota(jnp.int32, (c, c), 0)
    c64 = lax.broadcasted_iota(jnp.int32, (c, c), 1)
    ltri = jnp.where((c64 - r64) * sgn <= 0, 1.0, 0.0).astype(BF16)
    head0 = lax.broadcasted_iota(jnp.int32, (c, LANES), 1) < RW_HEAD
    pairs = range(npairs)

    def stack(x):
        return jnp.concatenate([jnp.where(head0, x, 0.0), jnp.where(head0, 0.0, x)], axis=0)

    def dup(x):
        return jnp.concatenate([x, x], axis=0)

    def bf(x):
        return x.astype(BF16)

    lhs, rhs, a2, bp2, kp2, vst, r2 = [], [], [], [], [], [], []
    lw_all = lw_ref[...]
    cum_all = _dot_exact_lhs(ltri, lw_all)
    for p in pairs:
        sl = slice(LANES * p, LANES * (p + 1))
        lw = lw_all[:, sl]
        cum = cum_all[:, sl]
        tot = jnp.sum(lw, axis=0, keepdims=True)
        p_inv = jnp.exp(-cum)
        p_end = jnp.exp(tot - cum)
        kk = kk_ref[:, sl]
        b = kk * as_ref[:, sl]
        kd = kd_ref[:, sl]
        a2p = stack(-kk * jnp.exp(cum - lw))
        r2p = stack(r_ref[:, sl] * jnp.exp(cum))
        a2.append(bf(a2p))
        r2.append(r2p)
        lhs.append(jnp.concatenate([a2[p], bf(r2p)], axis=0))
        rhs.append(jnp.concatenate([dup(bf(b * p_inv)), dup(bf(kd * p_inv))], axis=0))
        bp2.append(bf(stack(b * p_end)))
        kp2.append(bf(stack(kd * p_end)))
        vst.append(bf(stack(v_ref[:, sl])))
        pc_ref[:, sl] = jnp.broadcast_to(jnp.exp(tot), (SUBLANES, LANES))
    gram = [_dot_nt(lhs[p], rhs[p]) for p in pairs]
    nab = [jnp.where(strict, gram[p][0:c2, 0:c2], 0.0) for p in pairs]
    nrb = [bf(jnp.where(incl, gram[p][c2:2 * c2, 0:c2], 0.0)) for p in pairs]
    nk = [bf(jnp.concatenate([jnp.where(strict, gram[p][0:c2, c2:2 * c2], 0.0),
                              jnp.where(incl, gram[p][c2:2 * c2, c2:2 * c2], 0.0)], axis=0))
          for p in pairs]
    quads = range(npairs // 2)

    def side(x0, x1):
        return jnp.concatenate([x0, x1], axis=1)

    def diag(x0, x1):
        z0 = jnp.zeros_like(x0)
        return jnp.concatenate([side(x0, z0), side(z0, x1)], axis=0)

    def diag_halves(x):
        return diag(x[:, 0:c2], x[:, c2:2 * c2])

    def unside(xs):
        return [xs[p // 2][:, c2 * (p % 2):c2 * (p % 2 + 1)] for p in pairs]

    xv = unside([_dot(side(nk[2 * q], nk[2 * q + 1]), diag(vst[2 * q], vst[2 * q + 1]))
                 for q in quads])
    tinv = [side(eye + nab[2 * q], eye + nab[2 * q + 1]) for q in quads]
    npow = [bf(side(nab[2 * q], nab[2 * q + 1])) for q in quads]
    for _ in range(int(math.log2(c)) - 1):
        npow = [bf(_dot(npow[q], diag_halves(npow[q]))) for q in quads]
        tinv = [tinv[q] + _dot(bf(tinv[q]), diag_halves(npow[q])) for q in quads]
    tinv = unside(tinv)
    y = [_dot(bf(tinv[p]), jnp.concatenate([a2[p], bf(xv[p][0:c2, :])], axis=1)) for p in pairs]
    yb = [bf(y[p]) for p in pairs]
    z = [_dot(nrb[p], yb[p]) for p in pairs]
    mg = unside([_dot(bf(side(y[2 * q].T, y[2 * q + 1].T)), diag(bp2[2 * q], bp2[2 * q + 1]))
                 for q in quads])
    vk = unside([_dot(bf(side(vst[2 * q].astype(F32).T, vst[2 * q + 1].astype(F32).T)),
                      diag(kp2[2 * q], kp2[2 * q + 1])) for q in quads])
    for p in pairs:
        sl = slice(LANES * p, LANES * (p + 1))
        m_ref[:, sl] = bf(mg[p][0:c2, :])
        ga_ref[:, sl] = mg[p][c2:2 * c2, :] + vk[p]
        rq_ref[:, sl] = bf(r2[p] + z[p][:, 0:c2])
        o0 = z[p][:, c2:2 * c2] + xv[p][c2:2 * c2, :]
        o0_ref[:, sl] = o0[0:c, :] + o0[c:c2, :]


def _wkv_scan_kernel(m_ref, ga_ref, rq_ref, o0_ref, pc_ref, o_ref, g_scr, *, npairs):
    c = WKV_CHUNK

    @pl.when(pl.program_id(2) == 0)
    def _():
        g_scr[...] = jnp.zeros_like(g_scr)

    for p in range(npairs):
        sl = slice(LANES * p, LANES * (p + 1))
        g = g_scr[p]
        g_hi, g_lo = _split2(g)
        o_st = _dot_nt(rq_ref[:, sl], g_hi)
        o_ref[:, sl] = o_st[0:c, :] + o_st[c:2 * c, :] + o0_ref[:, sl]
        m = m_ref[:, sl]
        g_scr[p] = g * pc_ref[0:1, sl] + (_dot(g_hi, m) + _dot(g_lo, m)) + ga_ref[:, sl]


WKV_CHUNK_LANES = 2048


def _wkv_bidir(r, v, kk, lw, kd, asg, batch, ctx_len):
    n, d = r.shape
    c = WKV_CHUNK
    s = n // batch
    ncs = s // c
    ncc = ctx_len // c
    nch = n // c
    lanes = min(WKV_CHUNK_LANES, d)
    ngrp = d // lanes
    shared = pl.BlockSpec((c, lanes), lambda z, i, j: (i, j))
    perdir = pl.BlockSpec((None, c, lanes), lambda z, i, j: (z, i, j))
    big = pl.BlockSpec((None, 2 * c, lanes), lambda z, i, j: (z, i, j))
    m_, ga_, rq_, o0_, pc_ = pl.pallas_call(
        functools.partial(_wkv_chunk_kernel, npairs=lanes // LANES),
        grid=(2, nch, ngrp),
        in_specs=[shared, shared, shared, perdir, perdir, perdir],
        out_specs=[big, big, big, perdir,
                   pl.BlockSpec((None, SUBLANES, lanes), lambda z, i, j: (z, i, j))],
        out_shape=[jax.ShapeDtypeStruct((2, nch * 2 * c, d), dt) for dt in (BF16, F32, BF16)]
        + [jax.ShapeDtypeStruct((2, n, d), F32),
           jax.ShapeDtypeStruct((2, nch * SUBLANES, d), F32)],
        compiler_params=_cparams("parallel", "parallel", "parallel"),
        name="wkv_chunk",
    )(r, v, kk, lw, kd, asg)

    def cmap(z, b, cc):
        back = jnp.where(cc < ncc, ncc - 1 - cc, ncs + ncc - 1 - cc)
        return (z, b * ncs + jnp.where(z == 0, cc, back), 0)

    out = pl.pallas_call(
        functools.partial(_wkv_scan_kernel, npairs=d // LANES),
        grid=(2, batch, ncs),
        in_specs=[
            pl.BlockSpec((None, 2 * c, d), cmap),
            pl.BlockSpec((None, 2 * c, d), cmap),
            pl.BlockSpec((None, 2 * c, d), cmap),
            pl.BlockSpec((None, c, d), cmap),
            pl.BlockSpec((None, SUBLANES, d), cmap),
        ],
        out_specs=pl.BlockSpec((None, c, d), cmap),
        out_shape=jax.ShapeDtypeStruct((2, n, d), F32),
        scratch_shapes=[pltpu.VMEM((d // LANES, 2 * c, 2 * c), F32)],
        compiler_params=_cparams("parallel", "parallel", "arbitrary"),
        name="wkv_scan",
    )(m_, ga_, rq_, o0_, pc_)
    return out


def _rwread_kernel(o_ref, r_ref, kd_ref, v_ref, g_ref, pv_ref, bd_ref, y_ref):
    bd = bd_ref[...]
    wkv = o_ref[0] + o_ref[1]
    inv = 1.0 / RW_HEAD
    mu = _head_sum(wkv, bd) * inv
    dev = wkv - mu
    var = _head_sum(dev * dev, bd) * inv
    y = dev * lax.rsqrt(var + GN_EPS) * pv_ref[1:2, :] + pv_ref[2:3, :]
    rk = r_ref[...] * (kd_ref[0] + kd_ref[1]) * pv_ref[0:1, :]
    y = y + _head_sum(rk, bd) * v_ref[...]
    y_ref[...] = (y * g_ref[...]).astype(BF16)


def _rwkv_readout(geo, wkv, r, kd, v, g, pvec):
    n, d = r.shape
    tm, tc = geo.tm, COL_TILE
    ii = lax.broadcasted_iota(jnp.int32, (tc, tc), 0) // RW_HEAD
    jj = lax.broadcasted_iota(jnp.int32, (tc, tc), 1) // RW_HEAD
    bd = (ii == jj).astype(BF16)
    row = pl.BlockSpec((tm, tc), lambda i, j: (i, j))
    dirs = pl.BlockSpec((2, tm, tc), lambda i, j: (0, i, j))
    return pl.pallas_call(
        _rwread_kernel,
        grid=(geo.ntiles, d // tc),
        in_specs=[dirs, row, dirs, row, row,
                  pl.BlockSpec((SUBLANES, tc), lambda i, j: (0, j)),
                  pl.BlockSpec((tc, tc), lambda i, j: (0, 0))],
        out_specs=row,
        out_shape=jax.ShapeDtypeStruct((n, d), BF16),
        compiler_params=_cparams("parallel", "parallel"),
        name="rwkv_readout",
    )(wkv, r, kd, v, g, pvec, bd)


def _mlaqkv_kernel(z_ref, qn_ref, kvn_ref, wuq_ref, wukv_ref, gq_ref, gk_ref, cos_ref, sin_ref,
                   q_ref, k_ref, v_ref, *, q_lora, kv_lora):
    z = z_ref[...]
    cq = z[:, 0:q_lora]
    ckv = z[:, q_lora:q_lora + kv_lora]
    krot = z[:, q_lora + kv_lora:q_lora + kv_lora + LANES]

    def rms(x, g):
        ms = jnp.mean(x * x, axis=-1, keepdims=True)
        return x * lax.rsqrt(ms + EPS) * g

    qf = _dot(rms(cq, qn_ref[...]).astype(BF16), wuq_ref[...])
    kvf = _dot(rms(ckv, kvn_ref[...]).astype(BF16), wukv_ref[...])
    cos = cos_ref[...]
    sin = sin_ref[...]
    half = QK_ROPE // 2
    lane = lax.broadcasted_iota(jnp.int32, cos.shape, 1)

    def rope(x):
        up = pltpu.roll(x, LANES - half, axis=1)
        dn = pltpu.roll(x, half, axis=1)
        return x * cos + jnp.where(lane < half, up, dn) * sin

    inv_w = 1.0 / (QK_NOPE + QK_ROPE)
    gq = gq_ref[...]
    gk = gk_ref[...]
    kr_ss = jnp.sum(krot * krot, axis=-1, keepdims=True)
    for h in range(MLA_HEADS):
        o = QK_PAD * h
        qh = qf[:, o:o + QK_PAD]
        rs = lax.rsqrt(jnp.sum(qh * qh, axis=-1, keepdims=True) * inv_w + EPS)
        qn = qh * rs * gq * (ATTN_SCALE * math.log2(math.e))
        q_ref[:, o:o + QK_NOPE] = qn[:, 0:QK_NOPE].astype(BF16)
        q_ref[:, o + QK_NOPE:o + QK_PAD] = rope(qn[:, QK_NOPE:QK_PAD]).astype(BF16)
        kn = kvf[:, o:o + QK_NOPE]
        rsk = lax.rsqrt((jnp.sum(kn * kn, axis=-1, keepdims=True) + kr_ss) * inv_w + EPS)
        k_ref[:, o:o + QK_NOPE] = (kn * rsk * gk[:, 0:QK_NOPE]).astype(BF16)
        k_ref[:, o + QK_NOPE:o + QK_PAD] = rope(krot * rsk * gk[:, QK_NOPE:QK_PAD]).astype(BF16)
        v_ref[:, V_HEAD * h:V_HEAD * (h + 1)] = kvf[:, o + QK_NOPE:o + QK_PAD].astype(BF16)


def _mla_qkv(geo, z, qn, kvn, wuq_p, wukv, gq_p, gk_p, cos_t, sin_t):
    n, zw = z.shape
    tm = geo.tm
    q_lora, kv_lora = qn.shape[0], kvn.shape[0]
    hq = MLA_HEADS * QK_PAD
    tpb = geo.tpb
    full = lambda i: (0, 0)
    rowmap = lambda i: (i, 0)
    return pl.pallas_call(
        functools.partial(_mlaqkv_kernel, q_lora=q_lora, kv_lora=kv_lora),
        grid=(geo.ntiles,),
        in_specs=[
            pl.BlockSpec((tm, zw), rowmap),
            pl.BlockSpec((1, q_lora), full),
            pl.BlockSpec((1, kv_lora), full),
            pl.BlockSpec((q_lora, hq), full),
            pl.BlockSpec((kv_lora, hq), full),
            pl.BlockSpec((1, QK_PAD), full),
            pl.BlockSpec((1, QK_PAD), full),
            pl.BlockSpec((tm, LANES), lambda i: (i % tpb, 0)),
            pl.BlockSpec((tm, LANES), lambda i: (i % tpb, 0)),
        ],
        out_specs=[pl.BlockSpec((tm, hq), rowmap), pl.BlockSpec((tm, hq), rowmap),
                   pl.BlockSpec((tm, MLA_HEADS * V_HEAD), rowmap)],
        out_shape=[jax.ShapeDtypeStruct((n, hq), BF16), jax.ShapeDtypeStruct((n, hq), BF16),
                   jax.ShapeDtypeStruct((n, MLA_HEADS * V_HEAD), BF16)],
        compiler_params=_cparams("parallel"),
        name="mla_qkv",
    )(z, qn.reshape(1, -1), kvn.reshape(1, -1), wuq_p, wukv, gq_p, gk_p, cos_t, sin_t)


ATTN_SUBTILES = 2


def _attn_kernel(q_ref, k_ref, vt_ref, o_ref):
    k = k_ref[...]
    vt = vt_ref[...]
    cols = q_ref.shape[0] // ATTN_SUBTILES
    subs = range(ATTN_SUBTILES)
    s = [_dot_nt(k, q_ref[cols * i:cols * (i + 1), :]) for i in subs]
    m = [jnp.max(s[i], axis=0, keepdims=True) for i in subs]
    p = [jnp.exp2(s[i] - m[i]) for i in subs]
    l = [jnp.sum(p[i], axis=0, keepdims=True) for i in subs]
    for i in subs:
        ot = _dot(vt, p[i].astype(BF16)) / l[i]
        o_ref[cols * i:cols * (i + 1), :] = ot.T.astype(BF16)


def _attn_call(q3, k3, vt4, tq, nkeys):
    b, sq, _ = q3.shape
    return pl.pallas_call(
        _attn_kernel,
        grid=(b, MLA_HEADS, sq // tq),
        in_specs=[
            pl.BlockSpec((None, tq, QK_PAD), lambda bb, h, i: (bb, i, h)),
            pl.BlockSpec((None, nkeys, QK_PAD), lambda bb, h, i: (bb, 0, h)),
            pl.BlockSpec((None, None, V_HEAD, nkeys), lambda bb, h, i: (bb, h, 0, 0)),
        ],
        out_specs=pl.BlockSpec((None, tq, V_HEAD), lambda bb, h, i: (bb, i, h)),
        out_shape=jax.ShapeDtypeStruct((b, sq, MLA_HEADS * V_HEAD), BF16),
        compiler_params=_cparams("parallel", "parallel", "parallel"),
        name="mla_attention",
    )(q3, k3, vt4)


def _mla_attention(geo, q, k, v):
    b, s, l = geo.b, geo.s, geo.l
    q3 = q.reshape(b, s, MLA_HEADS * QK_PAD)
    k3 = k.reshape(b, s, MLA_HEADS * QK_PAD)
    vt4 = v.reshape(b, s, MLA_HEADS, V_HEAD).transpose(0, 2, 3, 1)
    o_ctx = _attn_call(q3[:, :l], k3, vt4, geo.tm, l)
    tq_lat = 2 * geo.tm if geo.t % (2 * geo.tm) == 0 else geo.tm
    o_lat = _attn_call(q3[:, l:], k3, vt4, tq_lat, s)
    return jnp.concatenate([o_ctx, o_lat], axis=1).reshape(b * s, MLA_HEADS * V_HEAD)


def _topk_rows(s, k, payload=None):
    rows = lax.broadcasted_iota(jnp.int32, s.shape, 0).astype(F32)
    big = float(s.shape[0])
    vals, idxs = [], []
    for _ in range(k):
        m = jnp.max(s, axis=0, keepdims=True)
        idx = jnp.min(jnp.where(s == m, rows, big), axis=0, keepdims=True)
        hit = rows == idx
        vals.append(m)
        if payload is None:
            idxs.append(idx)
        else:
            idxs.append(jnp.sum(jnp.where(hit, payload, 0.0), axis=0, keepdims=True))
        s = jnp.where(hit, -jnp.inf, s)
    return jnp.concatenate(vals, axis=0), jnp.concatenate(idxs, axis=0)


def _peersel_kernel(q_ref, qn_ref, keys_ref, e_ref, g_ref):
    q = q_ref[...]
    ms = jnp.mean(q * q, axis=-1, keepdims=True)
    qn = q * lax.rsqrt(ms + EPS) * qn_ref[...]
    half = D_KEY // 2
    s1 = _dot3_nt(keys_ref[0], qn[:, 0:half])
    s2 = _dot3_nt(keys_ref[1], qn[:, half:D_KEY])
    t1, i1 = _topk_rows(s1, PEER_TOPK)
    t2, i2 = _topk_rows(s2, PEER_TOPK)
    k = PEER_TOPK
    sub = lax.broadcasted_iota(jnp.int32, (SUBLANES, t1.shape[1]), 0)
    cand = [t1[0:1, :] + t2]
    cidx = [i1[0:1, :] * float(N_KEYS) + i2]
    for p in range(1, k // 2):
        live = sub < k // (p + 1)
        cand.append(jnp.where(live, t1[p:p + 1, :] + t2[0:SUBLANES, :], -jnp.inf))
        cidx.append(i1[p:p + 1, :] * float(N_KEYS) + i2[0:SUBLANES, :])
    cand.append(t1[k // 2:k, :] + t2[0:1, :])
    cidx.append(i1[k // 2:k, :] * float(N_KEYS) + i2[0:1, :])
    best, eidx = _topk_rows(jnp.concatenate(cand, axis=0), k, payload=jnp.concatenate(cidx, axis=0))
    ex = jnp.exp(best - jnp.max(best, axis=0, keepdims=True))
    g_ref[...] = ex / jnp.sum(ex, axis=0, keepdims=True)
    e_ref[...] = eidx.astype(jnp.int32)


def _peer_select(qp, q_norm, keys, tm, row0, n):
    assert row0 % tm == 0 and n % tm == 0
    tile0 = row0 // tm
    return pl.pallas_call(
        _peersel_kernel,
        grid=(n // tm, PEER_HEADS),
        in_specs=[
            pl.BlockSpec((tm, D_KEY), lambda i, h: (tile0 + i, h)),
            pl.BlockSpec((1, D_KEY), lambda i, h: (0, 0)),
            pl.BlockSpec((2, N_KEYS, D_KEY // 2), lambda i, h: (0, 0, 0)),
        ],
        out_specs=[pl.BlockSpec((PEER_TOPK, tm), lambda i, h: (h, i)),
                   pl.BlockSpec((PEER_TOPK, tm), lambda i, h: (h, i))],
        out_shape=[jax.ShapeDtypeStruct((PEER_HEADS * PEER_TOPK, n), jnp.int32),
                   jax.ShapeDtypeStruct((PEER_HEADS * PEER_TOPK, n), F32)],
        compiler_params=_cparams("parallel", "parallel"),
        name="peer_select",
    )(qp, q_norm.reshape(1, D_KEY), keys)


GATHER_SLOTS = 4
SLAB_PAD = 1


def _pack_expert_table(u, v):
    ne, d = u.shape
    ub = lax.bitcast_convert_type(u.astype(BF16), jnp.uint16).astype(jnp.uint32)
    vb = lax.bitcast_convert_type(v.astype(BF16), jnp.uint16).astype(jnp.uint32)
    return ((vb << 16) | ub).reshape(ne, d // LANES, LANES)


def _peergather_kernel(idx_ref, idxn_ref, gate_ref, h_ref, x_ref, mod_ref, tab_ref, o_ref,
                       *scratch, tb):
    nsel = PEER_HEADS * PEER_TOPK
    nrow = h_ref.shape[1] // LANES
    pitch = nrow + SLAB_PAD
    ns = GATHER_SLOTS
    bufs, sem = scratch[:ns], scratch[ns]
    lane_t = lax.broadcasted_iota(jnp.int32, (nsel, tb), 1)
    g2 = mod_ref[5:6, :]
    step = pl.program_id(0)
    nsteps = pl.num_programs(0)

    def row_copy(ids_ref, t, j, slot):
        return pltpu.make_async_copy(
            tab_ref.at[ids_ref[t, j]],
            bufs[slot].at[pl.ds(j * pitch, nrow), :],
            sem.at[slot])

    def issue(ids_ref, t, slot):
        for j in range(nsel):
            row_copy(ids_ref, t, j, slot).start(priority=j % 2)

    def wait(t, slot):
        for j in range(nsel):
            row_copy(idx_ref, t, j, slot).wait()

    def packed(slot, s):
        return bufs[slot][pl.ds(s, nsel, stride=pitch), :]

    def compute(t, slot):
        hrow = h_ref[pl.ds(t, 1), :]
        acc = jnp.zeros((nsel, LANES), F32)
        for s in range(nrow):
            u = lax.bitcast_convert_type(packed(slot, s) << 16, F32)
            acc = acc + u * hrow[:, LANES * s:LANES * (s + 1)]
        dots = jnp.sum(acc, axis=-1, keepdims=True)
        gcol = jnp.sum(jnp.where(lane_t == t, gate_ref[...], 0.0), axis=-1, keepdims=True)
        coef = gcol * _gelu(dots)
        outs = []
        for s in range(nrow):
            vv = lax.bitcast_convert_type(packed(slot, s) & jnp.uint32(0xFFFF0000), F32)
            outs.append(jnp.sum(coef * vv, axis=0, keepdims=True))
        orow = jnp.concatenate(outs, axis=1)
        o_ref[pl.ds(t, 1), :] = x_ref[pl.ds(t, 1), :] + g2 * orow

    @pl.when(step == 0)
    def _():
        for s in range(ns - 1):
            issue(idx_ref, s, s)

    ngroups = tb // ns

    def body(g, carry):
        for s in range(ns):
            t = g * ns + s
            wait(t, s)
            issue(idx_ref, t + ns - 1, (s + ns - 1) % ns)
            compute(t, s)
        return carry

    lax.fori_loop(0, ngroups - 1, body, 0)
    for s in range(ns):
        t = (ngroups - 1) * ns + s
        wait(t, s)
        if s == 0:
            issue(idx_ref, tb - 1, ns - 1)
        else:
            @pl.when(step < nsteps - 1)
            def _():
                issue(idxn_ref, s - 1, s - 1)
        compute(t, s)


def _peer_gather(geo_g, eidx, gate_t, h2, x, mods, table, n):
    d = x.shape[1]
    tb = geo_g.tm
    nsel = PEER_HEADS * PEER_TOPK
    pitch = d // LANES + SLAB_PAD
    assert n % tb == 0
    nsteps = n // tb
    return pl.pallas_call(
        functools.partial(_peergather_kernel, tb=tb),
        grid=(nsteps,),
        in_specs=[
            pl.BlockSpec((tb, nsel), lambda i: (i, 0), memory_space=pltpu.SMEM),
            pl.BlockSpec((tb, nsel), lambda i: (jnp.minimum(i + 1, nsteps - 1), 0),
                         memory_space=pltpu.SMEM),
            pl.BlockSpec((nsel, tb), lambda i: (0, i)),
            pl.BlockSpec((tb, d), lambda i: (i, 0)),
            pl.BlockSpec((tb, d), lambda i: (i, 0)),
            geo_g.mod_spec(),
            pl.BlockSpec(memory_space=pl.ANY),
        ],
        out_specs=pl.BlockSpec((tb, d), lambda i: (i, 0)),
        out_shape=jax.ShapeDtypeStruct((n, d), F32),
        scratch_shapes=[pltpu.VMEM((nsel * pitch, LANES), jnp.uint32)] * GATHER_SLOTS
        + [pltpu.SemaphoreType.DMA((GATHER_SLOTS,))],
        compiler_params=_cparams("arbitrary"),
        name="peer_gather",
    )(eidx, eidx, gate_t, h2, x, mods, table)


SC_CORES = 2
SC_SUBCORES = 16
SC_LANES = 16
SC_CHUNK = 16
SC_SHARE = 0.3235

_ERF_ALPHA = (0.00022905065861350646, 0.0034082910107109506, 0.050955695062380861,
              0.18520832239976145, 1.128379143519084)
_ERF_BETA = (-1.1791602954361697e-7, 0.000023547966471313185, 0.0010179625278914885,
             0.014070470171167667, 0.11098505178285362, 0.49746925110067538, 1.0)
_ERF_CLAMP = 3.832506856900711


def _erf_rational(x):
    x = jnp.minimum(jnp.maximum(x, -_ERF_CLAMP), _ERF_CLAMP)
    x2 = x * x
    p = jnp.full_like(x, _ERF_ALPHA[0])
    for c in _ERF_ALPHA[1:]:
        p = p * x2 + c
    q = jnp.full_like(x, _ERF_BETA[0])
    for c in _ERF_BETA[1:]:
        q = q * x2 + c
    return x * p / q


def _peer_sc(eidx, gate, h, table, row0):
    n, nsel = eidx.shape
    d = h.shape[1]
    nw = SC_CORES * SC_SUBCORES
    assert n % (2 * nw) == 0 and nsel % SC_CHUNK == 0 and d % SC_LANES == 0
    tpw = n // nw
    nchunk = nsel // SC_CHUNK
    nvec = d // SC_LANES
    mesh = plsc.VectorSubcoreMesh(core_axis_name="c", subcore_axis_name="s",
                                  num_cores=SC_CORES, num_subcores=SC_SUBCORES)

    @functools.partial(
        pl.kernel, mesh=mesh,
        out_type=jax.ShapeDtypeStruct((n, d), F32),
        scratch_types=[
            pltpu.VMEM((nsel,), jnp.int32), pltpu.VMEM((nsel,), jnp.int32),
            pltpu.VMEM((nsel,), F32), pltpu.VMEM((nsel,), F32),
            pltpu.VMEM((d,), F32), pltpu.VMEM((d,), F32),
            pltpu.VMEM((d,), F32),
            pltpu.VMEM((SC_CHUNK, d), jnp.uint32),
            pltpu.VMEM((SC_CHUNK, d), jnp.uint32),
            pltpu.SemaphoreType.DMA, pltpu.SemaphoreType.DMA,
            pltpu.SemaphoreType.DMA, pltpu.SemaphoreType.DMA,
        ],
        compiler_params=pltpu.CompilerParams(needs_layout_passes=False),
        name="peer_sc",
    )
    def sc_kernel(eidx_hbm, gate_hbm, h_hbm, tab_hbm, y_hbm, idx_a, idx_b, gate_a, gate_b, h_a, h_b,
                  o_v, rows0, rows1, sem0, sem1, msem_a, msem_b):
        wid = lax.axis_index("s") * SC_CORES + lax.axis_index("c")
        lanes = lax.iota(jnp.int32, SC_LANES)
        zero = jnp.zeros((SC_LANES,), F32)
        bufs = ((rows0, sem0), (rows1, sem1))
        meta = ((idx_a, gate_a, h_a, msem_a), (idx_b, gate_b, h_b, msem_b))

        def meta_copies(tok, s):
            idx_v, gate_v, h_v, msem = meta[s]
            return (pltpu.make_async_copy(eidx_hbm.at[tok], idx_v, msem),
                    pltpu.make_async_copy(gate_hbm.at[tok], gate_v, msem),
                    pltpu.make_async_copy(h_hbm.at[row0 + tok], h_v, msem))

        def gather(s, c, slot):
            rows, sem = bufs[slot]
            return pltpu.make_async_copy(
                tab_hbm.at[meta[s][0].at[pl.ds(c * SC_CHUNK, SC_CHUNK)]], rows, sem)

        def chunk_compute(s, c, rows):
            _, gate_v, h_v, _ = meta[s]

            def dot_body(i, accs):
                hv = h_v[pl.ds(i * SC_LANES, SC_LANES)]
                out = []
                for e in range(SC_CHUNK):
                    w = rows[e, pl.ds(i * SC_LANES, SC_LANES)]
                    out.append(accs[e] + lax.bitcast_convert_type(w << 16, F32) * hv)
                return tuple(out)

            accs = lax.fori_loop(0, nvec, dot_body, (zero,) * SC_CHUNK)
            dots = zero
            for e in range(SC_CHUNK):
                dots = jnp.where(lanes == e, jnp.sum(accs[e]), dots)
            act = 0.5 * dots * (1.0 + _erf_rational(dots * (2.0 ** -0.5)))
            coef = gate_v[pl.ds(c * SC_CHUNK, SC_CHUNK)] * act
            splat = [jnp.full((SC_LANES,), jnp.sum(jnp.where(lanes == e, coef, 0.0)), F32)
                     for e in range(SC_CHUNK)]

            def acc_body(i, carry):
                o = o_v[pl.ds(i * SC_LANES, SC_LANES)]
                for e in range(SC_CHUNK):
                    w = rows[e, pl.ds(i * SC_LANES, SC_LANES)]
                    o = o + splat[e] * lax.bitcast_convert_type(w & jnp.uint32(0xFFFF0000), F32)
                o_v[pl.ds(i * SC_LANES, SC_LANES)] = o
                return carry

            lax.fori_loop(0, nvec, acc_body, 0)

        def zero_body(i, c2):
            o_v[pl.ds(i * SC_LANES, SC_LANES)] = zero
            return c2

        def token(tok, s, has_next):
            def when_next(fn):
                if isinstance(has_next, bool):
                    if has_next:
                        fn()
                else:
                    pl.when(has_next)(fn)

            def load_next():
                for cp in meta_copies(tok + 1, 1 - s):
                    cp.start()

            when_next(load_next)
            lax.fori_loop(0, nvec, zero_body, 0)
            for c in range(nchunk):
                if c + 1 < nchunk:
                    gather(s, c + 1, (c + 1) % 2).start()
                gather(s, c, c % 2).wait()
                chunk_compute(s, c, bufs[c % 2][0])

            def prefetch_next():
                for cp in meta_copies(tok + 1, 1 - s):
                    cp.wait()
                gather(1 - s, 0, 0).start()

            when_next(prefetch_next)
            pltpu.sync_copy(o_v, y_hbm.at[tok])

        base = wid * tpw
        for cp in meta_copies(base, 0):
            cp.start()
        for cp in meta_copies(base, 0):
            cp.wait()
        gather(0, 0, 0).start()

        def pair_body(g, carry):
            token(base + 2 * g, 0, True)
            token(base + 2 * g + 1, 1, g + 1 < tpw // 2)
            return carry

        lax.fori_loop(0, tpw // 2, pair_body, 0)

    return sc_kernel(eidx, gate, h, table)


def _pad_to(x, axis, size):
    pad = [(0, 0)] * x.ndim
    pad[axis] = (0, size - x.shape[axis])
    return jnp.pad(x, pad)


def _rwkv_layer(geo, xs, mods, norm1, mix, w_rkv, w_o, w0, w1, w2, a0, a1, a2, vl, g1, g2,
                k_k, k_a, r_k, ln_w, ln_b, vfirst):
    d = geo.d
    xm = _rwkv_mix(geo, xs, norm1, mods, mix)
    rkv = _bmm(xm, w_rkv.astype(BF16), (0, 2, 3), ("none",) * 3, F32, geo.tm)
    lt = g1.shape[1]
    w1c = jnp.concatenate([_pad_to(w1[0], 1, LORA_PAD), _pad_to(w1[1], 1, LORA_PAD)], axis=1)
    a1c = jnp.concatenate([_pad_to(a1[0], 1, LORA_PAD), _pad_to(a1[1], 1, LORA_PAD)], axis=1)
    if vl is None:
        v1p = jnp.zeros((d, lt), F32)
        v2p = jnp.zeros((LORA_PAD, d), F32)
        v0 = jnp.zeros((d,), F32)
    else:
        v0, v1, v2 = vl
        v1p = _pad_to(v1, 1, lt)
        v2p = _pad_to(v2, 0, LORA_PAD)
    wl1 = jnp.stack([_pad_to(w1c, 1, lt), _pad_to(a1c, 1, lt), g1, v1p]).astype(BF16)
    tl = _bmm(xm, wl1, (1, 4, 5, 3), ("tanh", "none", "sigmoid", "none"), BF16, geo.tm)
    w2p = jnp.stack([_pad_to(w2[0], 0, LORA_PAD), _pad_to(w2[1], 0, LORA_PAD)]).astype(BF16)
    a2p = jnp.stack([_pad_to(a2[0], 0, LORA_PAD), _pad_to(a2[1], 0, LORA_PAD)]).astype(BF16)
    pvec = jnp.stack([w0[0], w0[1], a0[0], a0[1], v0, k_k, k_a, jnp.zeros_like(k_k)])
    lw, kd, asg, kk, g, v = _rwkv_feat(geo, rkv, tl, w2p, a2p, g2.astype(BF16), v2p.astype(BF16),
                                      pvec, vfirst)
    r = rkv[0]
    wkv = _wkv_bidir(r, v, kk, lw, kd, asg, geo.b, geo.l)
    pv2 = _pad_to(jnp.stack([r_k, ln_w, ln_b]), 0, SUBLANES)
    y = _rwkv_readout(geo, wkv, r, kd, v, g, pv2)
    xs = _matmul_res(geo, y, w_o.astype(BF16), xs, mods, 2)
    return xs, v


def _rope_tables(geo):
    t = geo.t
    pos = jnp.arange(t)
    row = (pos // GRID_W).astype(F32)
    col = (pos % GRID_W).astype(F32)
    n_freq = QK_ROPE // 4
    inv_freq = ROPE_THETA ** (-jnp.arange(n_freq, dtype=F32) / n_freq)
    ang = jnp.concatenate([row[:, None] * inv_freq, col[:, None] * inv_freq], axis=-1)
    cos, sin = jnp.cos(ang), jnp.sin(ang)
    pad = LANES - QK_ROPE
    cos_l = jnp.concatenate([cos, cos, jnp.ones((t, pad), F32)], axis=1)
    sin_l = jnp.concatenate([-sin, sin, jnp.zeros((t, pad), F32)], axis=1)
    cos_c = jnp.ones((geo.l, LANES), F32)
    sin_c = jnp.zeros((geo.l, LANES), F32)
    return jnp.concatenate([cos_c, cos_l], axis=0), jnp.concatenate([sin_c, sin_l], axis=0)


def _mla_layer(geo, xs, mods, norm1, rope_t, w_in, q_norm, kv_norm, w_uq, w_ukv, g_q, g_k, w_o):
    q_lora, kv_lora = q_norm.shape[0], kv_norm.shape[0]
    zw = q_lora + kv_lora + LANES
    z = _mod_matmul(geo, xs, norm1, mods, _pad_to(w_in, 1, zw).astype(BF16), 0, False)
    qk = QK_NOPE + QK_ROPE
    wuq_p = _pad_to(w_uq.reshape(q_lora, MLA_HEADS, qk), 2, QK_PAD).reshape(q_lora, -1)
    gq_p = _pad_to(g_q, 0, QK_PAD).reshape(1, QK_PAD)
    gk_p = _pad_to(g_k, 0, QK_PAD).reshape(1, QK_PAD)
    q, k, v = _mla_qkv(geo, z, q_norm, kv_norm, wuq_p.astype(BF16), w_ukv.astype(BF16),
                       gq_p, gk_p, *rope_t)
    o = _mla_attention(geo, q, k, v)
    return _matmul_res(geo, o, w_o.astype(BF16), xs, mods, 2)


def _peer_layer(geo, geo_g, xs, mods, norm2, w_q, q_norm, keys, u, v):
    qp, h2 = _mod_matmul(geo, xs, norm2, mods, w_q.astype(BF16), 1, True)
    table = _pack_expert_table(u, v)
    n, d = xs.shape
    n_sc = geo.tm * round(SC_SHARE * n / geo.tm)
    if n_sc % (2 * SC_CORES * SC_SUBCORES) != 0:
        n_sc = 0
    n_tc = n - n_sc

    def select(row0, nrows):
        wide = 2 * geo.tm
        tsel = wide if row0 % wide == 0 and nrows % wide == 0 else geo.tm
        return _peer_select(qp, q_norm, keys, tsel, row0, nrows)

    if n_sc:
        eidx_sc, gate_sc = select(n_tc, n_sc)
        y_sc = _peer_sc(eidx_sc.T, gate_sc.T, h2, table.reshape(table.shape[0], d), n_tc)
    eidx_t, gate_t = select(0, n_tc)
    out_tc = _peer_gather(geo_g, eidx_t.T, gate_t, h2, xs, mods, table, n_tc)
    if n_sc == 0:
        return out_tc
    g2, r = [], n_tc
    while r < n:
        bi, pos = divmod(r, geo.s)
        seg = int(pos >= geo.l)
        stop = min(n, bi * geo.s + (geo.s if seg else geo.l))
        g2.append(jnp.broadcast_to(mods[bi, seg, 5], (stop - r, d)))
        r = stop
    out_sc = xs[n_tc:] + jnp.concatenate(g2, axis=0) * y_sc
    return jnp.concatenate([out_tc, out_sc], axis=0)


def kernel(x, c, ctx, c_ctx, w_ada, b_ada, norm1, norm2, rw_mix, rw_wrkv, rw_wo, rw_w0, rw_w1, rw_w2, rw_a0, rw_a1, rw_a2, rw_v0, rw_v1, rw_v2, rw_g1, rw_g2, rw_kk, rw_ka, rw_rk, rw_lnw, rw_lnb, mla_win, mla_qnorm, mla_kvnorm, mla_wuq, mla_wukv, mla_gq, mla_gk, mla_wo, peer_wq, peer_qnorm, peer_keys, peer_u, peer_v):
    b, t, d = x.shape
    l = ctx.shape[1]
    depth = w_ada.shape[0]
    geo = _Geom(b, l, t, d, min(ROW_TILE, l))
    geo_g = _Geom(b, l, t, d, min(GATHER_TILE, l))
    cond8 = _pad_to(jnp.concatenate([c, c_ctx[None, :]], axis=0), 0, SUBLANES)
    ada = _adaln(cond8, w_ada, b_ada).reshape(depth, SUBLANES, 6, d)
    mods_all = jnp.stack([jnp.broadcast_to(ada[:, b:b + 1], (depth, b, 6, d)), ada[:, 0:b]], axis=2)
    xs = jnp.concatenate([ctx, x], axis=1).reshape(b * (l + t), d)
    rope_t = _rope_tables(geo)
    vfirst = None
    for i in range(depth):
        j = i // 2
        mods = mods_all[i]
        if i % 2 == 0:
            vl = None if j == 0 else (rw_v0[j - 1], rw_v1[j - 1], rw_v2[j - 1])
            xs, vcur = _rwkv_layer(geo, xs, mods, norm1[i], rw_mix[j], rw_wrkv[j], rw_wo[j],
                                   rw_w0[j], rw_w1[j], rw_w2[j], rw_a0[j], rw_a1[j], rw_a2[j], vl,
                                   rw_g1[j], rw_g2[j], rw_kk[j], rw_ka[j], rw_rk[j], rw_lnw[j],
                                   rw_lnb[j], vfirst)
            if j == 0:
                vfirst = vcur
        else:
            xs = _mla_layer(geo, xs, mods, norm1[i], rope_t, mla_win[j], mla_qnorm[j],
                            mla_kvnorm[j], mla_wuq[j], mla_wukv[j], mla_gq[j], mla_gk[j], mla_wo[j])
        if i == depth - 1:
            xs = xs.reshape(b, l + t, d)[:, l:, :].reshape(b * t, d)
            geo, geo_g = _Geom(b, 0, t, d, geo.tm), _Geom(b, 0, t, d, geo_g.tm)
        xs = _peer_layer(geo, geo_g, xs, mods, norm2[i], peer_wq[i], peer_qnorm[i], peer_keys[i],
                         peer_u[i], peer_v[i])
    return xs.reshape(b, t, d)
```

```python
import functools
import math

import jax
import jax.numpy as jnp
from jax import lax
from jax.experimental import pallas as pl
from jax.experimental.pallas import tpu as pltpu
from jax.experimental.pallas import tpu_sc as plsc

F32 = jnp.float32
BF16 = jnp.bfloat16

EPS = 1e-6
GN_EPS = 64e-5
RW_HEAD = 64
WKV_CHUNK = 64
MLA_HEADS = 16
QK_NOPE = 128
QK_ROPE = 64
V_HEAD = 128
QK_PAD = 256
ROPE_THETA = 10000.0
GRID_W = 64
ATTN_SCALE = (QK_NOPE + QK_ROPE) ** -0.5
PEER_HEADS = 8
N_KEYS = 128
PEER_TOPK = 16
D_KEY = 256
LORA_PAD = 128

LANES = 128
SUBLANES = 8
VMEM_LIMIT = 56 * 1024 * 1024
ROW_TILE = 256
GATHER_TILE = 128
COL_TILE = 512
ADA_COLS = 1024


def _cparams(*sem):
    return pltpu.CompilerParams(dimension_semantics=sem, vmem_limit_bytes=VMEM_LIMIT)


def _dot(a, b):
    return jnp.dot(a, b, preferred_element_type=F32)


def _dot_nt(a, b):
    return lax.dot_general(a, b, (((1,), (1,)), ((), ())), preferred_element_type=F32)


def _split2(x):
    hi = x.astype(BF16)
    lo = (x - hi.astype(F32)).astype(BF16)
    return hi, lo


def _split3(x):
    hi = x.astype(BF16)
    r1 = x - hi.astype(F32)
    mid = r1.astype(BF16)
    lo = (r1 - mid.astype(F32)).astype(BF16)
    return hi, mid, lo


def _dot3(a, b):
    ah, al = _split2(a)
    bh, bl = _split2(b)
    return _dot(ah, bh) + (_dot(ah, bl) + _dot(al, bh))


def _dot3_nt(a, b):
    ah, al = _split2(a)
    bh, bl = _split2(b)
    return _dot_nt(ah, bh) + (_dot_nt(ah, bl) + _dot_nt(al, bh))


def _dot_exact_lhs(sel, x):
    hi, mid, lo = _split3(x)
    return _dot(sel, hi) + (_dot(sel, mid) + _dot(sel, lo))


def _modulate(x, g, shift, scale):
    ms = jnp.mean(x * x, axis=-1, keepdims=True)
    return (x * lax.rsqrt(ms + EPS) * g) * (1.0 + scale) + shift


def _sigmoid(x):
    return 1.0 / (1.0 + jnp.exp(-x))


def _softplus(y):
    return jnp.maximum(y, 0.0) + jnp.log(1.0 + jnp.exp(-jnp.abs(y)))


def _erf(x):
    return lax.erf(x)


def _gelu(x):
    return 0.5 * x * (1.0 + _erf(x * (2.0 ** -0.5)))


def _ada_kernel(s_ref, w_ref, b_ref, o_ref):
    s = s_ref[...]
    s = s * _sigmoid(s)
    o_ref[...] = _dot3(s, w_ref[...]) + b_ref[...]


def _adaln(cond8, w_ada, b_ada):
    depth, d, n = w_ada.shape
    tn = min(ADA_COLS, n)
    return pl.pallas_call(
        _ada_kernel,
        grid=(depth, n // tn),
        in_specs=[
            pl.BlockSpec((SUBLANES, d), lambda l, j: (0, 0)),
            pl.BlockSpec((None, d, tn), lambda l, j: (l, 0, j)),
            pl.BlockSpec((None, 1, tn), lambda l, j: (l, 0, j)),
        ],
        out_specs=pl.BlockSpec((None, SUBLANES, tn), lambda l, j: (l, 0, j)),
        out_shape=jax.ShapeDtypeStruct((depth, SUBLANES, n), F32),
        compiler_params=_cparams("parallel", "parallel"),
        name="adaln",
    )(cond8, w_ada, b_ada.reshape(depth, 1, n))


class _Geom:
    def __init__(self, batch, ctx_len, seq_len, d_model, tm):
        self.b, self.l, self.t, self.d = batch, ctx_len, seq_len, d_model
        self.s = ctx_len + seq_len
        self.n = batch * self.s
        self.tm = tm
        assert ctx_len % tm == 0 and seq_len % tm == 0
        self.tpb = self.s // tm
        self.nct = ctx_len // tm
        self.ntiles = self.n // tm

    def mod_spec(self, nlead=0):
        tpb, nct = self.tpb, self.nct

        def imap(*ids):
            i = ids[nlead]
            return (i // tpb, ((i % tpb) >= nct).astype(jnp.int32), 0, 0)

        return pl.BlockSpec((None, None, 6, self.d), imap)


def _modmm_kernel(x_ref, g_ref, mod_ref, w_ref, o_ref, *h_ref, which):
    h = _modulate(x_ref[...], g_ref[...], mod_ref[3 * which:3 * which + 1, :],
                  mod_ref[3 * which + 1:3 * which + 2, :])
    if h_ref:
        h_ref[0][...] = h
    o_ref[...] = _dot(h.astype(BF16), w_ref[...])


def _mod_matmul(geo, x, g, mods, w, which, emit_h):
    n, d = x.shape
    nn = w.shape[1]
    tm = geo.tm
    out_shape = [jax.ShapeDtypeStruct((n, nn), F32)]
    out_specs = [pl.BlockSpec((tm, nn), lambda i: (i, 0))]
    if emit_h:
        out_shape.append(jax.ShapeDtypeStruct((n, d), F32))
        out_specs.append(pl.BlockSpec((tm, d), lambda i: (i, 0)))
    res = pl.pallas_call(
        functools.partial(_modmm_kernel, which=which),
        grid=(geo.ntiles,),
        in_specs=[
            pl.BlockSpec((tm, d), lambda i: (i, 0)),
            pl.BlockSpec((1, d), lambda i: (0, 0)),
            geo.mod_spec(),
            pl.BlockSpec((d, nn), lambda i: (0, 0)),
        ],
        out_specs=out_specs,
        out_shape=out_shape,
        compiler_params=_cparams("parallel"),
        name="mod_matmul",
    )(x, g.reshape(1, d), mods, w)
    return res if emit_h else res[0]


def _mmres_kernel(y_ref, w_ref, x_ref, mod_ref, o_ref, *, gidx):
    acc = _dot(y_ref[...], w_ref[...])
    o_ref[...] = x_ref[...] + mod_ref[gidx:gidx + 1, :] * acc


def _matmul_res(geo, y, w, x, mods, gidx):
    n, k = y.shape
    d = x.shape[1]
    tm = geo.tm
    return pl.pallas_call(
        functools.partial(_mmres_kernel, gidx=gidx),
        grid=(geo.ntiles,),
        in_specs=[
            pl.BlockSpec((tm, k), lambda i: (i, 0)),
            pl.BlockSpec((k, d), lambda i: (0, 0)),
            pl.BlockSpec((tm, d), lambda i: (i, 0)),
            geo.mod_spec(),
        ],
        out_specs=pl.BlockSpec((tm, d), lambda i: (i, 0)),
        out_shape=jax.ShapeDtypeStruct((n, d), F32),
        compiler_params=_cparams("parallel"),
        name="matmul_res",
    )(y, w, x, mods)


def _bmm_kernel(x_ref, w_ref, o_ref, *, acts):
    j = pl.program_id(0)
    y = _dot(x_ref[...], w_ref[...])
    out = y
    for jj, a in enumerate(acts):
        if a == "tanh":
            out = jnp.where(j == jj, jnp.tanh(y), out)
        elif a == "sigmoid":
            out = jnp.where(j == jj, _sigmoid(y), out)
    o_ref[...] = out.astype(o_ref.dtype)


def _bmm(x3, w3, src, acts, out_dtype, tm):
    _, n, k = x3.shape
    nj, _, nn = w3.shape
    src = tuple(src)

    def xmap(j, i):
        idx = jnp.int32(src[0])
        for jj in range(1, nj):
            idx = jnp.where(j == jj, jnp.int32(src[jj]), idx)
        return (idx, i, 0)

    return pl.pallas_call(
        functools.partial(_bmm_kernel, acts=tuple(acts)),
        grid=(nj, n // tm),
        in_specs=[
            pl.BlockSpec((None, tm, k), xmap),
            pl.BlockSpec((None, k, nn), lambda j, i: (j, 0, 0)),
        ],
        out_specs=pl.BlockSpec((None, tm, nn), lambda j, i: (j, i, 0)),
        out_shape=jax.ShapeDtypeStruct((nj, n, nn), out_dtype),
        compiler_params=_cparams("parallel", "parallel"),
        name="bmm",
    )(x3, w3)


def _rwmix_kernel(x_ref, xp_ref, xn_ref, g_ref, mod_ref, mix_ref, o_ref, *, tpb, nct):
    i = pl.program_id(0)
    tm = x_ref.shape[0]
    g = g_ref[...]
    shift = mod_ref[0:1, :]
    scale = mod_ref[1:2, :]
    h = _modulate(x_ref[...], g, shift, scale)
    hp = _modulate(xp_ref[...], g, shift, scale)[SUBLANES - 1:SUBLANES, :]
    hn = _modulate(xn_ref[...], g, shift, scale)[0:1, :]
    it = i % tpb
    first = jnp.logical_or(it == 0, it == nct)
    last = jnp.logical_or(it == nct - 1, it == tpb - 1)
    hp = jnp.where(first, 0.0, hp)
    hn = jnp.where(last, 0.0, hn)
    rows = lax.broadcasted_iota(jnp.int32, h.shape, 0)
    prev = jnp.where(rows == 0, hp, pltpu.roll(h, 1, axis=0))
    nxt = jnp.where(rows == tm - 1, hn, pltpu.roll(h, tm - 1, axis=0))
    xx = 0.5 * (prev + nxt) - h
    for m in range(6):
        o_ref[m] = (h + xx * mix_ref[m:m + 1, :]).astype(BF16)


def _rwkv_mix(geo, x, g, mods, mix):
    n, d = x.shape
    tm = geo.tm
    r8 = tm // SUBLANES
    nblk8 = n // SUBLANES
    return pl.pallas_call(
        functools.partial(_rwmix_kernel, tpb=geo.tpb, nct=geo.nct),
        grid=(geo.ntiles,),
        in_specs=[
            pl.BlockSpec((tm, d), lambda i: (i, 0)),
            pl.BlockSpec((SUBLANES, d), lambda i: (jnp.maximum(i * r8 - 1, 0), 0)),
            pl.BlockSpec((SUBLANES, d), lambda i: (jnp.minimum((i + 1) * r8, nblk8 - 1), 0)),
            pl.BlockSpec((1, d), lambda i: (0, 0)),
            geo.mod_spec(),
            pl.BlockSpec((6, d), lambda i: (0, 0)),
        ],
        out_specs=pl.BlockSpec((6, tm, d), lambda i: (0, i, 0)),
        out_shape=jax.ShapeDtypeStruct((6, n, d), BF16),
        compiler_params=_cparams("parallel"),
        name="rwkv_mix",
    )(x, x, x, g.reshape(1, d), mods, mix)


def _head_sum(x, bd):
    return _dot(x.astype(BF16), bd)


def _rwfeat_kernel(k_ref, v_ref, tl_ref, w2_ref, a2_ref, g2_ref, v2_ref, pv_ref, bd_ref,
                   *rest, has_vlora):
    if has_vlora:
        vf_ref, lw_ref, kd_ref, as_ref, kk_ref, g_ref, vo_ref = rest
    else:
        lw_ref, kd_ref, as_ref, kk_ref, g_ref = rest
    k = k_ref[...]
    tw = tl_ref[0]
    ta = tl_ref[1]
    tg = tl_ref[2]
    w0 = pv_ref[0:2, :]
    a0 = pv_ref[2:4, :]
    k_k = pv_ref[5:6, :]
    k_a = pv_ref[6:7, :]
    for z in range(2):
        sl = slice(LORA_PAD * z, LORA_PAD * (z + 1))
        lora_w = _dot(tw[:, sl], w2_ref[z])
        w = -_softplus(-(w0[z:z + 1, :] + lora_w)) - 0.5
        lw_ref[z] = -jnp.exp(w)
        a_sig = _sigmoid(a0[z:z + 1, :] + _dot(ta[:, sl], a2_ref[z]))
        as_ref[z] = a_sig.astype(BF16)
        kd_ref[z] = k * (1.0 + (a_sig - 1.0) * k_a)
    g_ref[...] = _dot(tg, g2_ref[...]).astype(BF16)
    kkr = k * k_k
    ss = _head_sum(kkr * kkr, bd_ref[...])
    kk_ref[...] = (kkr * lax.rsqrt(ss + 1e-12)).astype(BF16)
    if has_vlora:
        v = v_ref[...]
        tv = tl_ref[3]
        gate = _sigmoid(pv_ref[4:5, :] + _dot(tv[:, 0:LORA_PAD], v2_ref[...]))
        vo_ref[...] = v + (vf_ref[...] - v) * gate


def _rwkv_feat(geo, rkv, tl, w2p, a2p, g2, v2p, pvec, vfirst):
    _, n, d = rkv.shape
    tm, tc = geo.tm, COL_TILE
    has_vlora = vfirst is not None
    lt = tl.shape[2]
    ii = lax.broadcasted_iota(jnp.int32, (tc, tc), 0) // RW_HEAD
    jj = lax.broadcasted_iota(jnp.int32, (tc, tc), 1) // RW_HEAD
    bd = (ii == jj).astype(BF16)
    row = lambda i, j: (i, j)
    in_specs = [
        pl.BlockSpec((None, tm, tc), lambda i, j: (1, i, j)),
        pl.BlockSpec((None, tm, tc), lambda i, j: (2, i, j)),
        pl.BlockSpec((4, tm, lt), lambda i, j: (0, i, 0)),
        pl.BlockSpec((2, LORA_PAD, tc), lambda i, j: (0, 0, j)),
        pl.BlockSpec((2, LORA_PAD, tc), lambda i, j: (0, 0, j)),
        pl.BlockSpec((lt, tc), lambda i, j: (0, j)),
        pl.BlockSpec((LORA_PAD, tc), lambda i, j: (0, j)),
        pl.BlockSpec((SUBLANES, tc), lambda i, j: (0, j)),
        pl.BlockSpec((tc, tc), lambda i, j: (0, 0)),
    ]
    args = [rkv, rkv, tl, w2p, a2p, g2, v2p, pvec, bd]
    dir_spec = pl.BlockSpec((2, tm, tc), lambda i, j: (0, i, j))
    out_specs = [dir_spec, dir_spec, dir_spec, pl.BlockSpec((tm, tc), row), pl.BlockSpec((tm, tc), row)]
    out_shape = [jax.ShapeDtypeStruct((2, n, d), dt) for dt in (F32, F32, BF16)]
    out_shape += [jax.ShapeDtypeStruct((n, d), BF16)] * 2
    if has_vlora:
        in_specs.append(pl.BlockSpec((tm, tc), row))
        args.append(vfirst)
        out_specs.append(pl.BlockSpec((tm, tc), row))
        out_shape.append(jax.ShapeDtypeStruct((n, d), F32))
    res = pl.pallas_call(
        functools.partial(_rwfeat_kernel, has_vlora=has_vlora),
        grid=(geo.ntiles, d // tc),
        in_specs=in_specs,
        out_specs=out_specs,
        out_shape=out_shape,
        compiler_params=_cparams("parallel", "parallel"),
        name="rwkv_feat",
    )(*args)
    if has_vlora:
        lw, kd, asg, kk, g, v = res
    else:
        lw, kd, asg, kk, g = res
        v = rkv[2]
    return lw, kd, asg, kk, g, v


def _wkv_chunk_kernel(r_ref, v_ref, kk_ref, lw_ref, kd_ref, as_ref,
                      m_ref, ga_ref, rq_ref, o0_ref, pc_ref, *, npairs):
    c = WKV_CHUNK
    c2 = 2 * c
    sgn = 1 - 2 * pl.program_id(0)
    ri = lax.broadcasted_iota(jnp.int32, (c2, c2), 0)
    ci = lax.broadcasted_iota(jnp.int32, (c2, c2), 1)
    same = (ri >= c) == (ci >= c)
    tt = jnp.where(ri >= c, ri - c, ri)
    ss = jnp.where(ci >= c, ci - c, ci)
    earlier = (ss - tt) * sgn < 0
    strict = jnp.logical_and(same, earlier)
    incl = jnp.logical_and(same, jnp.logical_or(earlier, ss == tt))
    eye = (ri == ci).astype(F32)
    r64 = lax.broadcasted_iota(jnp.int32, (c, c), 0)
    c64 = lax.broadcasted_iota(jnp.int32, (c, c), 1)
    ltri = jnp.where((c64 - r64) * sgn <= 0, 1.0, 0.0).astype(BF16)
    head0 = lax.broadcasted_iota(jnp.int32, (c, LANES), 1) < RW_HEAD
    pairs = range(npairs)

    def stack(x):
        return jnp.concatenate([jnp.where(head0, x, 0.0), jnp.where(head0, 0.0, x)], axis=0)

    def dup(x):
        return jnp.concatenate([x, x], axis=0)

    def bf(x):
        return x.astype(BF16)

    lhs, rhs, a2, bp2, kp2, vst, r2 = [], [], [], [], [], [], []
    lw_all = lw_ref[...]
    cum_all = _dot_exact_lhs(ltri, lw_all)
    for p in pairs:
        sl = slice(LANES * p, LANES * (p + 1))
        lw = lw_all[:, sl]
        cum = cum_all[:, sl]
        tot = jnp.sum(lw, axis=0, keepdims=True)
        p_inv = jnp.exp(-cum)
        p_end = jnp.exp(tot - cum)
        kk = kk_ref[:, sl]
        b = kk * as_ref[:, sl]
        kd = kd_ref[:, sl]
        a2p = stack(-kk * jnp.exp(cum - lw))
        r2p = stack(r_ref[:, sl] * jnp.exp(cum))
        a2.append(bf(a2p))
        r2.append(r2p)
        lhs.append(jnp.concatenate([a2[p], bf(r2p)], axis=0))
        rhs.append(jnp.concatenate([dup(bf(b * p_inv)), dup(bf(kd * p_inv))], axis=0))
        bp2.append(bf(stack(b * p_end)))
        kp2.append(bf(stack(kd * p_end)))
        vst.append(bf(stack(v_ref[:, sl])))
        pc_ref[:, sl] = jnp.broadcast_to(jnp.exp(tot), (SUBLANES, LANES))
    gram = [_dot_nt(lhs[p], rhs[p]) for p in pairs]
    nab = [jnp.where(strict, gram[p][0:c2, 0:c2], 0.0) for p in pairs]
    nrb = [bf(jnp.where(incl, gram[p][c2:2 * c2, 0:c2], 0.0)) for p in pairs]
    nk = [bf(jnp.concatenate([jnp.where(strict, gram[p][0:c2, c2:2 * c2], 0.0),
                              jnp.where(incl, gram[p][c2:2 * c2, c2:2 * c2], 0.0)], axis=0))
          for p in pairs]
    quads = range(npairs // 2)

    def side(x0, x1):
        return jnp.concatenate([x0, x1], axis=1)

    def diag(x0, x1):
        z0 = jnp.zeros_like(x0)
        return jnp.concatenate([side(x0, z0), side(z0, x1)], axis=0)

    def diag_halves(x):
        return diag(x[:, 0:c2], x[:, c2:2 * c2])

    def unside(xs):
        return [xs[p // 2][:, c2 * (p % 2):c2 * (p % 2 + 1)] for p in pairs]

    xv = unside([_dot(side(nk[2 * q], nk[2 * q + 1]), diag(vst[2 * q], vst[2 * q + 1]))
                 for q in quads])
    tinv = [side(eye + nab[2 * q], eye + nab[2 * q + 1]) for q in quads]
    npow = [bf(side(nab[2 * q], nab[2 * q + 1])) for q in quads]
    for _ in range(int(math.log2(c)) - 1):
        npow = [bf(_dot(npow[q], diag_halves(npow[q]))) for q in quads]
        tinv = [tinv[q] + _dot(bf(tinv[q]), diag_halves(npow[q])) for q in quads]
    tinv = unside(tinv)
    y = [_dot(bf(tinv[p]), jnp.concatenate([a2[p], bf(xv[p][0:c2, :])], axis=1)) for p in pairs]
    yb = [bf(y[p]) for p in pairs]
    z = [_dot(nrb[p], yb[p]) for p in pairs]
    mg = unside([_dot(bf(side(y[2 * q].T, y[2 * q + 1].T)), diag(bp2[2 * q], bp2[2 * q + 1]))
                 for q in quads])
    vk = unside([_dot(bf(side(vst[2 * q].astype(F32).T, vst[2 * q + 1].astype(F32).T)),
                      diag(kp2[2 * q], kp2[2 * q + 1])) for q in quads])
    for p in pairs:
        sl = slice(LANES * p, LANES * (p + 1))
        m_ref[:, sl] = bf(mg[p][0:c2, :])
        ga_ref[:, sl] = mg[p][c2:2 * c2, :] + vk[p]
        rq_ref[:, sl] = bf(r2[p] + z[p][:, 0:c2])
        o0 = z[p][:, c2:2 * c2] + xv[p][c2:2 * c2, :]
        o0_ref[:, sl] = o0[0:c, :] + o0[c:c2, :]


def _wkv_scan_kernel(m_ref, ga_ref, rq_ref, o0_ref, pc_ref, o_ref, g_scr, *, npairs):
    c = WKV_CHUNK

    @pl.when(pl.program_id(2) == 0)
    def _():
        g_scr[...] = jnp.zeros_like(g_scr)

    for p in range(npairs):
        sl = slice(LANES * p, LANES * (p + 1))
        g = g_scr[p]
        g_hi = g.astype(BF16)
        o_st = _dot_nt(rq_ref[:, sl], g_hi)
        o_ref[:, sl] = o_st[0:c, :] + o_st[c:2 * c, :] + o0_ref[:, sl]
        m = m_ref[:, sl]
        g_scr[p] = g * pc_ref[0:1, sl] + _dot(g_hi, m) + ga_ref[:, sl]


WKV_CHUNK_LANES = 2048


def _wkv_bidir(r, v, kk, lw, kd, asg, batch, ctx_len):
    n, d = r.shape
    c = WKV_CHUNK
    s = n // batch
    ncs = s // c
    ncc = ctx_len // c
    nch = n // c
    lanes = min(WKV_CHUNK_LANES, d)
    ngrp = d // lanes
    shared = pl.BlockSpec((c, lanes), lambda z, i, j: (i, j))
    perdir = pl.BlockSpec((None, c, lanes), lambda z, i, j: (z, i, j))
    big = pl.BlockSpec((None, 2 * c, lanes), lambda z, i, j: (z, i, j))
    m_, ga_, rq_, o0_, pc_ = pl.pallas_call(
        functools.partial(_wkv_chunk_kernel, npairs=lanes // LANES),
        grid=(2, nch, ngrp),
        in_specs=[shared, shared, shared, perdir, perdir, perdir],
        out_specs=[big, big, big, perdir,
                   pl.BlockSpec((None, SUBLANES, lanes), lambda z, i, j: (z, i, j))],
        out_shape=[jax.ShapeDtypeStruct((2, nch * 2 * c, d), dt) for dt in (BF16, F32, BF16)]
        + [jax.ShapeDtypeStruct((2, n, d), F32),
           jax.ShapeDtypeStruct((2, nch * SUBLANES, d), F32)],
        compiler_params=_cparams("parallel", "parallel", "parallel"),
        name="wkv_chunk",
    )(r, v, kk, lw, kd, asg)

    def cmap(z, b, cc):
        back = jnp.where(cc < ncc, ncc - 1 - cc, ncs + ncc - 1 - cc)
        return (z, b * ncs + jnp.where(z == 0, cc, back), 0)

    out = pl.pallas_call(
        functools.partial(_wkv_scan_kernel, npairs=d // LANES),
        grid=(2, batch, ncs),
        in_specs=[
            pl.BlockSpec((None, 2 * c, d), cmap),
            pl.BlockSpec((None, 2 * c, d), cmap),
            pl.BlockSpec((None, 2 * c, d), cmap),
            pl.BlockSpec((None, c, d), cmap),
            pl.BlockSpec((None, SUBLANES, d), cmap),
        ],
        out_specs=pl.BlockSpec((None, c, d), cmap),
        out_shape=jax.ShapeDtypeStruct((2, n, d), F32),
        scratch_shapes=[pltpu.VMEM((d // LANES, 2 * c, 2 * c), F32)],
        compiler_params=_cparams("parallel", "parallel", "arbitrary"),
        name="wkv_scan",
    )(m_, ga_, rq_, o0_, pc_)
    return out


def _rwread_kernel(o_ref, r_ref, kd_ref, v_ref, g_ref, pv_ref, bd_ref, y_ref):
    bd = bd_ref[...]
    wkv = o_ref[0] + o_ref[1]
    inv = 1.0 / RW_HEAD
    mu = _head_sum(wkv, bd) * inv
    dev = wkv - mu
    var = _head_sum(dev * dev, bd) * inv
    y = dev * lax.rsqrt(var + GN_EPS) * pv_ref[1:2, :] + pv_ref[2:3, :]
    rk = r_ref[...] * (kd_ref[0] + kd_ref[1]) * pv_ref[0:1, :]
    y = y + _head_sum(rk, bd) * v_ref[...]
    y_ref[...] = (y * g_ref[...]).astype(BF16)


def _rwkv_readout(geo, wkv, r, kd, v, g, pvec):
    n, d = r.shape
    tm, tc = geo.tm, COL_TILE
    ii = lax.broadcasted_iota(jnp.int32, (tc, tc), 0) // RW_HEAD
    jj = lax.broadcasted_iota(jnp.int32, (tc, tc), 1) // RW_HEAD
    bd = (ii == jj).astype(BF16)
    row = pl.BlockSpec((tm, tc), lambda i, j: (i, j))
    dirs = pl.BlockSpec((2, tm, tc), lambda i, j: (0, i, j))
    return pl.pallas_call(
        _rwread_kernel,
        grid=(geo.ntiles, d // tc),
        in_specs=[dirs, row, dirs, row, row,
                  pl.BlockSpec((SUBLANES, tc), lambda i, j: (0, j)),
                  pl.BlockSpec((tc, tc), lambda i, j: (0, 0))],
        out_specs=row,
        out_shape=jax.ShapeDtypeStruct((n, d), BF16),
        compiler_params=_cparams("parallel", "parallel"),
        name="rwkv_readout",
    )(wkv, r, kd, v, g, pvec, bd)


def _mlaqkv_kernel(z_ref, qn_ref, kvn_ref, wuq_ref, wukv_ref, gq_ref, gk_ref, cos_ref, sin_ref,
                   q_ref, k_ref, v_ref, *, q_lora, kv_lora):
    z = z_ref[...]
    cq = z[:, 0:q_lora]
    ckv = z[:, q_lora:q_lora + kv_lora]
    krot = z[:, q_lora + kv_lora:q_lora + kv_lora + LANES]

    def rms(x, g):
        ms = jnp.mean(x * x, axis=-1, keepdims=True)
        return x * lax.rsqrt(ms + EPS) * g

    qf = _dot(rms(cq, qn_ref[...]).astype(BF16), wuq_ref[...])
    kvf = _dot(rms(ckv, kvn_ref[...]).astype(BF16), wukv_ref[...])
    cos = cos_ref[...]
    sin = sin_ref[...]
    half = QK_ROPE // 2
    lane = lax.broadcasted_iota(jnp.int32, cos.shape, 1)

    def rope(x):
        up = pltpu.roll(x, LANES - half, axis=1)
        dn = pltpu.roll(x, half, axis=1)
        return x * cos + jnp.where(lane < half, up, dn) * sin

    inv_w = 1.0 / (QK_NOPE + QK_ROPE)
    gq = gq_ref[...]
    gk = gk_ref[...]
    kr_ss = jnp.sum(krot * krot, axis=-1, keepdims=True)
    for h in range(MLA_HEADS):
        o = QK_PAD * h
        qh = qf[:, o:o + QK_PAD]
        rs = lax.rsqrt(jnp.sum(qh * qh, axis=-1, keepdims=True) * inv_w + EPS)
        qn = qh * rs * gq * (ATTN_SCALE * math.log2(math.e))
        q_ref[:, o:o + QK_NOPE] = qn[:, 0:QK_NOPE].astype(BF16)
        q_ref[:, o + QK_NOPE:o + QK_PAD] = rope(qn[:, QK_NOPE:QK_PAD]).astype(BF16)
        kn = kvf[:, o:o + QK_NOPE]
        rsk = lax.rsqrt((jnp.sum(kn * kn, axis=-1, keepdims=True) + kr_ss) * inv_w + EPS)
        k_ref[:, o:o + QK_NOPE] = (kn * rsk * gk[:, 0:QK_NOPE]).astype(BF16)
        k_ref[:, o + QK_NOPE:o + QK_PAD] = rope(krot * rsk * gk[:, QK_NOPE:QK_PAD]).astype(BF16)
        v_ref[:, V_HEAD * h:V_HEAD * (h + 1)] = kvf[:, o + QK_NOPE:o + QK_PAD].astype(BF16)


def _mla_qkv(geo, z, qn, kvn, wuq_p, wukv, gq_p, gk_p, cos_t, sin_t):
    n, zw = z.shape
    tm = geo.tm
    q_lora, kv_lora = qn.shape[0], kvn.shape[0]
    hq = MLA_HEADS * QK_PAD
    tpb = geo.tpb
    full = lambda i: (0, 0)
    rowmap = lambda i: (i, 0)
    return pl.pallas_call(
        functools.partial(_mlaqkv_kernel, q_lora=q_lora, kv_lora=kv_lora),
        grid=(geo.ntiles,),
        in_specs=[
            pl.BlockSpec((tm, zw), rowmap),
            pl.BlockSpec((1, q_lora), full),
            pl.BlockSpec((1, kv_lora), full),
            pl.BlockSpec((q_lora, hq), full),
            pl.BlockSpec((kv_lora, hq), full),
            pl.BlockSpec((1, QK_PAD), full),
            pl.BlockSpec((1, QK_PAD), full),
            pl.BlockSpec((tm, LANES), lambda i: (i % tpb, 0)),
            pl.BlockSpec((tm, LANES), lambda i: (i % tpb, 0)),
        ],
        out_specs=[pl.BlockSpec((tm, hq), rowmap), pl.BlockSpec((tm, hq), rowmap),
                   pl.BlockSpec((tm, MLA_HEADS * V_HEAD), rowmap)],
        out_shape=[jax.ShapeDtypeStruct((n, hq), BF16), jax.ShapeDtypeStruct((n, hq), BF16),
                   jax.ShapeDtypeStruct((n, MLA_HEADS * V_HEAD), BF16)],
        compiler_params=_cparams("parallel"),
        name="mla_qkv",
    )(z, qn.reshape(1, -1), kvn.reshape(1, -1), wuq_p, wukv, gq_p, gk_p, cos_t, sin_t)


ATTN_SUBTILES = 2


def _attn_kernel(q_ref, k_ref, vt_ref, o_ref):
    k = k_ref[...]
    vt = vt_ref[...]
    cols = q_ref.shape[0] // ATTN_SUBTILES
    subs = range(ATTN_SUBTILES)
    s = [_dot_nt(k, q_ref[cols * i:cols * (i + 1), :]) for i in subs]
    m = [jnp.max(s[i], axis=0, keepdims=True) for i in subs]
    p = [jnp.exp2(s[i] - m[i]) for i in subs]
    l = [jnp.sum(p[i], axis=0, keepdims=True) for i in subs]
    for i in subs:
        ot = _dot(vt, p[i].astype(BF16)) / l[i]
        o_ref[cols * i:cols * (i + 1), :] = ot.T.astype(BF16)


def _attn_call(q3, k3, vt4, tq, nkeys):
    b, sq, _ = q3.shape
    return pl.pallas_call(
        _attn_kernel,
        grid=(b, MLA_HEADS, sq // tq),
        in_specs=[
            pl.BlockSpec((None, tq, QK_PAD), lambda bb, h, i: (bb, i, h)),
            pl.BlockSpec((None, nkeys, QK_PAD), lambda bb, h, i: (bb, 0, h)),
            pl.BlockSpec((None, None, V_HEAD, nkeys), lambda bb, h, i: (bb, h, 0, 0)),
        ],
        out_specs=pl.BlockSpec((None, tq, V_HEAD), lambda bb, h, i: (bb, i, h)),
        out_shape=jax.ShapeDtypeStruct((b, sq, MLA_HEADS * V_HEAD), BF16),
        compiler_params=_cparams("parallel", "parallel", "parallel"),
        name="mla_attention",
    )(q3, k3, vt4)


def _mla_attention(geo, q, k, v):
    b, s, l = geo.b, geo.s, geo.l
    q3 = q.reshape(b, s, MLA_HEADS * QK_PAD)
    k3 = k.reshape(b, s, MLA_HEADS * QK_PAD)
    vt4 = v.reshape(b, s, MLA_HEADS, V_HEAD).transpose(0, 2, 3, 1)
    o_ctx = _attn_call(q3[:, :l], k3, vt4, geo.tm, l)
    tq_lat = 2 * geo.tm if geo.t % (2 * geo.tm) == 0 else geo.tm
    o_lat = _attn_call(q3[:, l:], k3, vt4, tq_lat, s)
    return jnp.concatenate([o_ctx, o_lat], axis=1).reshape(b * s, MLA_HEADS * V_HEAD)


def _topk_rows(s, k, payload=None):
    rows = lax.broadcasted_iota(jnp.int32, s.shape, 0).astype(F32)
    big = float(s.shape[0])
    vals, idxs = [], []
    for _ in range(k):
        m = jnp.max(s, axis=0, keepdims=True)
        idx = jnp.min(jnp.where(s == m, rows, big), axis=0, keepdims=True)
        hit = rows == idx
        vals.append(m)
        if payload is None:
            idxs.append(idx)
        else:
            idxs.append(jnp.sum(jnp.where(hit, payload, 0.0), axis=0, keepdims=True))
        s = jnp.where(hit, -jnp.inf, s)
    return jnp.concatenate(vals, axis=0), jnp.concatenate(idxs, axis=0)


def _peersel_kernel(q_ref, qn_ref, keys_ref, e_ref, g_ref):
    q = q_ref[...]
    ms = jnp.mean(q * q, axis=-1, keepdims=True)
    qn = q * lax.rsqrt(ms + EPS) * qn_ref[...]
    half = D_KEY // 2
    s1 = _dot3_nt(keys_ref[0], qn[:, 0:half])
    s2 = _dot3_nt(keys_ref[1], qn[:, half:D_KEY])
    t1, i1 = _topk_rows(s1, PEER_TOPK)
    t2, i2 = _topk_rows(s2, PEER_TOPK)
    k = PEER_TOPK
    sub = lax.broadcasted_iota(jnp.int32, (SUBLANES, t1.shape[1]), 0)
    cand = [t1[0:1, :] + t2]
    cidx = [i1[0:1, :] * float(N_KEYS) + i2]
    for p in range(1, k // 2):
        live = sub < k // (p + 1)
        cand.append(jnp.where(live, t1[p:p + 1, :] + t2[0:SUBLANES, :], -jnp.inf))
        cidx.append(i1[p:p + 1, :] * float(N_KEYS) + i2[0:SUBLANES, :])
    cand.append(t1[k // 2:k, :] + t2[0:1, :])
    cidx.append(i1[k // 2:k, :] * float(N_KEYS) + i2[0:1, :])
    best, eidx = _topk_rows(jnp.concatenate(cand, axis=0), k, payload=jnp.concatenate(cidx, axis=0))
    ex = jnp.exp(best - jnp.max(best, axis=0, keepdims=True))
    g_ref[...] = ex / jnp.sum(ex, axis=0, keepdims=True)
    e_ref[...] = eidx.astype(jnp.int32)


def _peer_select(qp, q_norm, keys, tm, row0, n):
    assert row0 % tm == 0 and n % tm == 0
    tile0 = row0 // tm
    return pl.pallas_call(
        _peersel_kernel,
        grid=(n // tm, PEER_HEADS),
        in_specs=[
            pl.BlockSpec((tm, D_KEY), lambda i, h: (tile0 + i, h)),
            pl.BlockSpec((1, D_KEY), lambda i, h: (0, 0)),
            pl.BlockSpec((2, N_KEYS, D_KEY // 2), lambda i, h: (0, 0, 0)),
        ],
        out_specs=[pl.BlockSpec((PEER_TOPK, tm), lambda i, h: (h, i)),
                   pl.BlockSpec((PEER_TOPK, tm), lambda i, h: (h, i))],
        out_shape=[jax.ShapeDtypeStruct((PEER_HEADS * PEER_TOPK, n), jnp.int32),
                   jax.ShapeDtypeStruct((PEER_HEADS * PEER_TOPK, n), F32)],
        compiler_params=_cparams("parallel", "parallel"),
        name="peer_select",
    )(qp, q_norm.reshape(1, D_KEY), keys)


GATHER_SLOTS = 4
SLAB_PAD = 1


def _pack_expert_table(u, v):
    ne, d = u.shape
    ub = lax.bitcast_convert_type(u.astype(BF16), jnp.uint16).astype(jnp.uint32)
    vb = lax.bitcast_convert_type(v.astype(BF16), jnp.uint16).astype(jnp.uint32)
    return ((vb << 16) | ub).reshape(ne, d // LANES, LANES)


def _peergather_kernel(idx_ref, idxn_ref, gate_ref, h_ref, x_ref, mod_ref, tab_ref, o_ref,
                       *scratch, tb):
    nsel = PEER_HEADS * PEER_TOPK
    nrow = h_ref.shape[1] // LANES
    pitch = nrow + SLAB_PAD
    ns = GATHER_SLOTS
    bufs, sem = scratch[:ns], scratch[ns]
    lane_t = lax.broadcasted_iota(jnp.int32, (nsel, tb), 1)
    g2 = mod_ref[5:6, :]
    step = pl.program_id(0)
    nsteps = pl.num_programs(0)

    def row_copy(ids_ref, t, j, slot):
        return pltpu.make_async_copy(
            tab_ref.at[ids_ref[t, j]],
            bufs[slot].at[pl.ds(j * pitch, nrow), :],
            sem.at[slot])

    def issue(ids_ref, t, slot):
        for j in range(nsel):
            row_copy(ids_ref, t, j, slot).start(priority=j % 2)

    def wait(t, slot):
        for j in range(nsel):
            row_copy(idx_ref, t, j, slot).wait()

    def packed(slot, s):
        return bufs[slot][pl.ds(s, nsel, stride=pitch), :]

    def compute(t, slot):
        hrow = h_ref[pl.ds(t, 1), :]
        acc = jnp.zeros((nsel, LANES), F32)
        for s in range(nrow):
            u = lax.bitcast_convert_type(packed(slot, s) << 16, F32)
            acc = acc + u * hrow[:, LANES * s:LANES * (s + 1)]
        dots = jnp.sum(acc, axis=-1, keepdims=True)
        gcol = jnp.sum(jnp.where(lane_t == t, gate_ref[...], 0.0), axis=-1, keepdims=True)
        coef = gcol * _gelu(dots)
        outs = []
        for s in range(nrow):
            vv = lax.bitcast_convert_type(packed(slot, s) & jnp.uint32(0xFFFF0000), F32)
            outs.append(jnp.sum(coef * vv, axis=0, keepdims=True))
        orow = jnp.concatenate(outs, axis=1)
        o_ref[pl.ds(t, 1), :] = x_ref[pl.ds(t, 1), :] + g2 * orow

    @pl.when(step == 0)
    def _():
        for s in range(ns - 1):
            issue(idx_ref, s, s)

    ngroups = tb // ns

    def body(g, carry):
        for s in range(ns):
            t = g * ns + s
            wait(t, s)
            issue(idx_ref, t + ns - 1, (s + ns - 1) % ns)
            compute(t, s)
        return carry

    lax.fori_loop(0, ngroups - 1, body, 0)
    for s in range(ns):
        t = (ngroups - 1) * ns + s
        wait(t, s)
        if s == 0:
            issue(idx_ref, tb - 1, ns - 1)
        else:
            @pl.when(step < nsteps - 1)
            def _():
                issue(idxn_ref, s - 1, s - 1)
        compute(t, s)


def _peer_gather(geo_g, eidx, gate_t, h2, x, mods, table, n):
    d = x.shape[1]
    tb = geo_g.tm
    nsel = PEER_HEADS * PEER_TOPK
    pitch = d // LANES + SLAB_PAD
    assert n % tb == 0
    nsteps = n // tb
    return pl.pallas_call(
        functools.partial(_peergather_kernel, tb=tb),
        grid=(nsteps,),
        in_specs=[
            pl.BlockSpec((tb, nsel), lambda i: (i, 0), memory_space=pltpu.SMEM),
            pl.BlockSpec((tb, nsel), lambda i: (jnp.minimum(i + 1, nsteps - 1), 0),
                         memory_space=pltpu.SMEM),
            pl.BlockSpec((nsel, tb), lambda i: (0, i)),
            pl.BlockSpec((tb, d), lambda i: (i, 0)),
            pl.BlockSpec((tb, d), lambda i: (i, 0)),
            geo_g.mod_spec(),
            pl.BlockSpec(memory_space=pl.ANY),
        ],
        out_specs=pl.BlockSpec((tb, d), lambda i: (i, 0)),
        out_shape=jax.ShapeDtypeStruct((n, d), F32),
        scratch_shapes=[pltpu.VMEM((nsel * pitch, LANES), jnp.uint32)] * GATHER_SLOTS
        + [pltpu.SemaphoreType.DMA((GATHER_SLOTS,))],
        compiler_params=_cparams("arbitrary"),
        name="peer_gather",
    )(eidx, eidx, gate_t, h2, x, mods, table)


SC_CORES = 2
SC_SUBCORES = 16
SC_LANES = 16
SC_CHUNK = 16
SC_SHARE = 0.3235

_ERF_ALPHA = (0.00022905065861350646, 0.0034082910107109506, 0.050955695062380861,
              0.18520832239976145, 1.128379143519084)
_ERF_BETA = (-1.1791602954361697e-7, 0.000023547966471313185, 0.0010179625278914885,
             0.014070470171167667, 0.11098505178285362, 0.49746925110067538, 1.0)
_ERF_CLAMP = 3.832506856900711


def _erf_rational(x):
    x = jnp.minimum(jnp.maximum(x, -_ERF_CLAMP), _ERF_CLAMP)
    x2 = x * x
    p = jnp.full_like(x, _ERF_ALPHA[0])
    for c in _ERF_ALPHA[1:]:
        p = p * x2 + c
    q = jnp.full_like(x, _ERF_BETA[0])
    for c in _ERF_BETA[1:]:
        q = q * x2 + c
    return x * p / q


def _peer_sc(eidx, gate, h, table, row0):
    n, nsel = eidx.shape
    d = h.shape[1]
    nw = SC_CORES * SC_SUBCORES
    assert n % (2 * nw) == 0 and nsel % SC_CHUNK == 0 and d % SC_LANES == 0
    tpw = n // nw
    nchunk = nsel // SC_CHUNK
    nvec = d // SC_LANES
    mesh = plsc.VectorSubcoreMesh(core_axis_name="c", subcore_axis_name="s",
                                  num_cores=SC_CORES, num_subcores=SC_SUBCORES)

    @functools.partial(
        pl.kernel, mesh=mesh,
        out_type=jax.ShapeDtypeStruct((n, d), F32),
        scratch_types=[
            pltpu.VMEM((nsel,), jnp.int32), pltpu.VMEM((nsel,), jnp.int32),
            pltpu.VMEM((nsel,), F32), pltpu.VMEM((nsel,), F32),
            pltpu.VMEM((d,), F32), pltpu.VMEM((d,), F32),
            pltpu.VMEM((d,), F32),
            pltpu.VMEM((SC_CHUNK, d), jnp.uint32),
            pltpu.VMEM((SC_CHUNK, d), jnp.uint32),
            pltpu.SemaphoreType.DMA, pltpu.SemaphoreType.DMA,
            pltpu.SemaphoreType.DMA, pltpu.SemaphoreType.DMA,
        ],
        compiler_params=pltpu.CompilerParams(needs_layout_passes=False),
        name="peer_sc",
    )
    def sc_kernel(eidx_hbm, gate_hbm, h_hbm, tab_hbm, y_hbm, idx_a, idx_b, gate_a, gate_b, h_a, h_b,
                  o_v, rows0, rows1, sem0, sem1, msem_a, msem_b):
        wid = lax.axis_index("s") * SC_CORES + lax.axis_index("c")
        lanes = lax.iota(jnp.int32, SC_LANES)
        zero = jnp.zeros((SC_LANES,), F32)
        bufs = ((rows0, sem0), (rows1, sem1))
        meta = ((idx_a, gate_a, h_a, msem_a), (idx_b, gate_b, h_b, msem_b))

        def meta_copies(tok, s):
            idx_v, gate_v, h_v, msem = meta[s]
            return (pltpu.make_async_copy(eidx_hbm.at[tok], idx_v, msem),
                    pltpu.make_async_copy(gate_hbm.at[tok], gate_v, msem),
                    pltpu.make_async_copy(h_hbm.at[row0 + tok], h_v, msem))

        def gather(s, c, slot):
            rows, sem = bufs[slot]
            return pltpu.make_async_copy(
                tab_hbm.at[meta[s][0].at[pl.ds(c * SC_CHUNK, SC_CHUNK)]], rows, sem)

        def chunk_compute(s, c, rows):
            _, gate_v, h_v, _ = meta[s]

            def dot_body(i, accs):
                hv = h_v[pl.ds(i * SC_LANES, SC_LANES)]
                out = []
                for e in range(SC_CHUNK):
                    w = rows[e, pl.ds(i * SC_LANES, SC_LANES)]
                    out.append(accs[e] + lax.bitcast_convert_type(w << 16, F32) * hv)
                return tuple(out)

            accs = lax.fori_loop(0, nvec, dot_body, (zero,) * SC_CHUNK)
            dots = zero
            for e in range(SC_CHUNK):
                dots = jnp.where(lanes == e, jnp.sum(accs[e]), dots)
            act = 0.5 * dots * (1.0 + _erf_rational(dots * (2.0 ** -0.5)))
            coef = gate_v[pl.ds(c * SC_CHUNK, SC_CHUNK)] * act
            splat = [jnp.full((SC_LANES,), jnp.sum(jnp.where(lanes == e, coef, 0.0)), F32)
                     for e in range(SC_CHUNK)]

            def acc_body(i, carry):
                o = o_v[pl.ds(i * SC_LANES, SC_LANES)]
                for e in range(SC_CHUNK):
                    w = rows[e, pl.ds(i * SC_LANES, SC_LANES)]
                    o = o + splat[e] * lax.bitcast_convert_type(w & jnp.uint32(0xFFFF0000), F32)
                o_v[pl.ds(i * SC_LANES, SC_LANES)] = o
                return carry

            lax.fori_loop(0, nvec, acc_body, 0)

        def zero_body(i, c2):
            o_v[pl.ds(i * SC_LANES, SC_LANES)] = zero
            return c2

        def token(tok, s, has_next):
            def when_next(fn):
                if isinstance(has_next, bool):
                    if has_next:
                        fn()
                else:
                    pl.when(has_next)(fn)

            def load_next():
                for cp in meta_copies(tok + 1, 1 - s):
                    cp.start()

            when_next(load_next)
            lax.fori_loop(0, nvec, zero_body, 0)
            for c in range(nchunk):
                if c + 1 < nchunk:
                    gather(s, c + 1, (c + 1) % 2).start()
                gather(s, c, c % 2).wait()
                chunk_compute(s, c, bufs[c % 2][0])

            def prefetch_next():
                for cp in meta_copies(tok + 1, 1 - s):
                    cp.wait()
                gather(1 - s, 0, 0).start()

            when_next(prefetch_next)
            pltpu.sync_copy(o_v, y_hbm.at[tok])

        base = wid * tpw
        for cp in meta_copies(base, 0):
            cp.start()
        for cp in meta_copies(base, 0):
            cp.wait()
        gather(0, 0, 0).start()

        def pair_body(g, carry):
            token(base + 2 * g, 0, True)
            token(base + 2 * g + 1, 1, g + 1 < tpw // 2)
            return carry

        lax.fori_loop(0, tpw // 2, pair_body, 0)

    return sc_kernel(eidx, gate, h, table)


def _pad_to(x, axis, size):
    pad = [(0, 0)] * x.ndim
    pad[axis] = (0, size - x.shape[axis])
    return jnp.pad(x, pad)


def _rwkv_layer(geo, xs, mods, norm1, mix, w_rkv, w_o, w0, w1, w2, a0, a1, a2, vl, g1, g2,
                k_k, k_a, r_k, ln_w, ln_b, vfirst):
    d = geo.d
    xm = _rwkv_mix(geo, xs, norm1, mods, mix)
    rkv = _bmm(xm, w_rkv.astype(BF16), (0, 2, 3), ("none",) * 3, F32, geo.tm)
    lt = g1.shape[1]
    w1c = jnp.concatenate([_pad_to(w1[0], 1, LORA_PAD), _pad_to(w1[1], 1, LORA_PAD)], axis=1)
    a1c = jnp.concatenate([_pad_to(a1[0], 1, LORA_PAD), _pad_to(a1[1], 1, LORA_PAD)], axis=1)
    if vl is None:
        v1p = jnp.zeros((d, lt), F32)
        v2p = jnp.zeros((LORA_PAD, d), F32)
        v0 = jnp.zeros((d,), F32)
    else:
        v0, v1, v2 = vl
        v1p = _pad_to(v1, 1, lt)
        v2p = _pad_to(v2, 0, LORA_PAD)
    wl1 = jnp.stack([_pad_to(w1c, 1, lt), _pad_to(a1c, 1, lt), g1, v1p]).astype(BF16)
    tl = _bmm(xm, wl1, (1, 4, 5, 3), ("tanh", "none", "sigmoid", "none"), BF16, geo.tm)
    w2p = jnp.stack([_pad_to(w2[0], 0, LORA_PAD), _pad_to(w2[1], 0, LORA_PAD)]).astype(BF16)
    a2p = jnp.stack([_pad_to(a2[0], 0, LORA_PAD), _pad_to(a2[1], 0, LORA_PAD)]).astype(BF16)
    pvec = jnp.stack([w0[0], w0[1], a0[0], a0[1], v0, k_k, k_a, jnp.zeros_like(k_k)])
    lw, kd, asg, kk, g, v = _rwkv_feat(geo, rkv, tl, w2p, a2p, g2.astype(BF16), v2p.astype(BF16),
                                      pvec, vfirst)
    r = rkv[0]
    wkv = _wkv_bidir(r, v, kk, lw, kd, asg, geo.b, geo.l)
    pv2 = _pad_to(jnp.stack([r_k, ln_w, ln_b]), 0, SUBLANES)
    y = _rwkv_readout(geo, wkv, r, kd, v, g, pv2)
    xs = _matmul_res(geo, y, w_o.astype(BF16), xs, mods, 2)
    return xs, v


def _rope_tables(geo):
    t = geo.t
    pos = jnp.arange(t)
    row = (pos // GRID_W).astype(F32)
    col = (pos % GRID_W).astype(F32)
    n_freq = QK_ROPE // 4
    inv_freq = ROPE_THETA ** (-jnp.arange(n_freq, dtype=F32) / n_freq)
    ang = jnp.concatenate([row[:, None] * inv_freq, col[:, None] * inv_freq], axis=-1)
    cos, sin = jnp.cos(ang), jnp.sin(ang)
    pad = LANES - QK_ROPE
    cos_l = jnp.concatenate([cos, cos, jnp.ones((t, pad), F32)], axis=1)
    sin_l = jnp.concatenate([-sin, sin, jnp.zeros((t, pad), F32)], axis=1)
    cos_c = jnp.ones((geo.l, LANES), F32)
    sin_c = jnp.zeros((geo.l, LANES), F32)
    return jnp.concatenate([cos_c, cos_l], axis=0), jnp.concatenate([sin_c, sin_l], axis=0)


def _mla_layer(geo, xs, mods, norm1, rope_t, w_in, q_norm, kv_norm, w_uq, w_ukv, g_q, g_k, w_o):
    q_lora, kv_lora = q_norm.shape[0], kv_norm.shape[0]
    zw = q_lora + kv_lora + LANES
    z = _mod_matmul(geo, xs, norm1, mods, _pad_to(w_in, 1, zw).astype(BF16), 0, False)
    qk = QK_NOPE + QK_ROPE
    wuq_p = _pad_to(w_uq.reshape(q_lora, MLA_HEADS, qk), 2, QK_PAD).reshape(q_lora, -1)
    gq_p = _pad_to(g_q, 0, QK_PAD).reshape(1, QK_PAD)
    gk_p = _pad_to(g_k, 0, QK_PAD).reshape(1, QK_PAD)
    q, k, v = _mla_qkv(geo, z, q_norm, kv_norm, wuq_p.astype(BF16), w_ukv.astype(BF16),
                       gq_p, gk_p, *rope_t)
    o = _mla_attention(geo, q, k, v)
    return _matmul_res(geo, o, w_o.astype(BF16), xs, mods, 2)


def _peer_layer(geo, geo_g, xs, mods, norm2, w_q, q_norm, keys, u, v):
    qp, h2 = _mod_matmul(geo, xs, norm2, mods, w_q.astype(BF16), 1, True)
    table = _pack_expert_table(u, v)
    n, d = xs.shape
    n_sc = geo.tm * round(SC_SHARE * n / geo.tm)
    if n_sc % (2 * SC_CORES * SC_SUBCORES) != 0:
        n_sc = 0
    n_tc = n - n_sc

    def select(row0, nrows):
        wide = 2 * geo.tm
        tsel = wide if row0 % wide == 0 and nrows % wide == 0 else geo.tm
        return _peer_select(qp, q_norm, keys, tsel, row0, nrows)

    if n_sc:
        eidx_sc, gate_sc = select(n_tc, n_sc)
        y_sc = _peer_sc(eidx_sc.T, gate_sc.T, h2, table.reshape(table.shape[0], d), n_tc)
    eidx_t, gate_t = select(0, n_tc)
    out_tc = _peer_gather(geo_g, eidx_t.T, gate_t, h2, xs, mods, table, n_tc)
    if n_sc == 0:
        return out_tc
    g2, r = [], n_tc
    while r < n:
        bi, pos = divmod(r, geo.s)
        seg = int(pos >= geo.l)
        stop = min(n, bi * geo.s + (geo.s if seg else geo.l))
        g2.append(jnp.broadcast_to(mods[bi, seg, 5], (stop - r, d)))
        r = stop
    out_sc = xs[n_tc:] + jnp.concatenate(g2, axis=0) * y_sc
    return jnp.concatenate([out_tc, out_sc], axis=0)


def kernel(x, c, ctx, c_ctx, w_ada, b_ada, norm1, norm2, rw_mix, rw_wrkv, rw_wo, rw_w0, rw_w1, rw_w2, rw_a0, rw_a1, rw_a2, rw_v0, rw_v1, rw_v2, rw_g1, rw_g2, rw_kk, rw_ka, rw_rk, rw_lnw, rw_lnb, mla_win, mla_qnorm, mla_kvnorm, mla_wuq, mla_wukv, mla_gq, mla_gk, mla_wo, peer_wq, peer_qnorm, peer_keys, peer_u, peer_v):
    b, t, d = x.shape
    l = ctx.shape[1]
    depth = w_ada.shape[0]
    geo = _Geom(b, l, t, d, min(ROW_TILE, l))
    geo_g = _Geom(b, l, t, d, min(GATHER_TILE, l))
    cond8 = _pad_to(jnp.concatenate([c, c_ctx[None, :]], axis=0), 0, SUBLANES)
    ada = _adaln(cond8, w_ada, b_ada).reshape(depth, SUBLANES, 6, d)
    mods_all = jnp.stack([jnp.broadcast_to(ada[:, b:b + 1], (depth, b, 6, d)), ada[:, 0:b]], axis=2)
    xs = jnp.concatenate([ctx, x], axis=1).reshape(b * (l + t), d)
    rope_t = _rope_tables(geo)
    vfirst = None
    for i in range(depth):
        j = i // 2
        mods = mods_all[i]
        if i % 2 == 0:
            vl = None if j == 0 else (rw_v0[j - 1], rw_v1[j - 1], rw_v2[j - 1])
            xs, vcur = _rwkv_layer(geo, xs, mods, norm1[i], rw_mix[j], rw_wrkv[j], rw_wo[j],
                                   rw_w0[j], rw_w1[j], rw_w2[j], rw_a0[j], rw_a1[j], rw_a2[j], vl,
                                   rw_g1[j], rw_g2[j], rw_kk[j], rw_ka[j], rw_rk[j], rw_lnw[j],
                                   rw_lnb[j], vfirst)
            if j == 0:
                vfirst = vcur
        else:
            xs = _mla_layer(geo, xs, mods, norm1[i], rope_t, mla_win[j], mla_qnorm[j],
                            mla_kvnorm[j], mla_wuq[j], mla_wukv[j], mla_gq[j], mla_gk[j], mla_wo[j])
        if i == depth - 1:
            xs = xs.reshape(b, l + t, d)[:, l:, :].reshape(b * t, d)
            geo, geo_g = _Geom(b, 0, t, d, geo.tm), _Geom(b, 0, t, d, geo_g.tm)
        xs = _peer_layer(geo, geo_g, xs, mods, norm2[i], peer_wq[i], peer_qnorm[i], peer_keys[i],
                         peer_u[i], peer_v[i])
    return xs.reshape(b, t, d)
```

```python
import functools
import math

import jax
import jax.numpy as jnp
from jax import lax
from jax.experimental import pallas as pl
from jax.experimental.pallas import tpu as pltpu
from jax.experimental.pallas import tpu_sc as plsc

F32 = jnp.float32
BF16 = jnp.bfloat16

EPS = 1e-6
GN_EPS = 64e-5
RW_HEAD = 64
WKV_CHUNK = 64
MLA_HEADS = 16
QK_NOPE = 128
QK_ROPE = 64
V_HEAD = 128
QK_PAD = 256
ROPE_THETA = 10000.0
GRID_W = 64
ATTN_SCALE = (QK_NOPE + QK_ROPE) ** -0.5
PEER_HEADS = 8
N_KEYS = 128
PEER_TOPK = 16
D_KEY = 256
LORA_PAD = 128

LANES = 128
SUBLANES = 8
VMEM_LIMIT = 56 * 1024 * 1024
ROW_TILE = 256
GATHER_TILE = 128
COL_TILE = 512
ADA_COLS = 1024


def _cparams(*sem):
    return pltpu.CompilerParams(dimension_semantics=sem, vmem_limit_bytes=VMEM_LIMIT)


def _dot(a, b):
    return jnp.dot(a, b, preferred_element_type=F32)


def _dot_nt(a, b):
    return lax.dot_general(a, b, (((1,), (1,)), ((), ())), preferred_element_type=F32)


def _split2(x):
    hi = x.astype(BF16)
    lo = (x - hi.astype(F32)).astype(BF16)
    return hi, lo


def _split3(x):
    hi = x.astype(BF16)
    r1 = x - hi.astype(F32)
    mid = r1.astype(BF16)
    lo = (r1 - mid.astype(F32)).astype(BF16)
    return hi, mid, lo


def _dot3(a, b):
    ah, al = _split2(a)
    bh, bl = _split2(b)
    return _dot(ah, bh) + (_dot(ah, bl) + _dot(al, bh))


def _dot3_nt(a, b):
    ah, al = _split2(a)
    bh, bl = _split2(b)
    return _dot_nt(ah, bh) + (_dot_nt(ah, bl) + _dot_nt(al, bh))


def _dot_exact_lhs(sel, x):
    hi, mid, lo = _split3(x)
    return _dot(sel, hi) + (_dot(sel, mid) + _dot(sel, lo))


def _modulate(x, g, shift, scale):
    ms = jnp.mean(x * x, axis=-1, keepdims=True)
    return (x * lax.rsqrt(ms + EPS) * g) * (1.0 + scale) + shift


def _sigmoid(x):
    return 1.0 / (1.0 + jnp.exp(-x))


def _softplus(y):
    return jnp.maximum(y, 0.0) + jnp.log(1.0 + jnp.exp(-jnp.abs(y)))


def _erf(x):
    return lax.erf(x)


def _gelu(x):
    return 0.5 * x * (1.0 + _erf(x * (2.0 ** -0.5)))


def _ada_kernel(s_ref, w_ref, b_ref, o_ref):
    s = s_ref[...]
    s = s * _sigmoid(s)
    o_ref[...] = _dot3(s, w_ref[...]) + b_ref[...]


def _adaln(cond8, w_ada, b_ada):
    depth, d, n = w_ada.shape
    tn = min(ADA_COLS, n)
    return pl.pallas_call(
        _ada_kernel,
        grid=(depth, n // tn),
        in_specs=[
            pl.BlockSpec((SUBLANES, d), lambda l, j: (0, 0)),
            pl.BlockSpec((None, d, tn), lambda l, j: (l, 0, j)),
            pl.BlockSpec((None, 1, tn), lambda l, j: (l, 0, j)),
        ],
        out_specs=pl.BlockSpec((None, SUBLANES, tn), lambda l, j: (l, 0, j)),
        out_shape=jax.ShapeDtypeStruct((depth, SUBLANES, n), F32),
        compiler_params=_cparams("parallel", "parallel"),
        name="adaln",
    )(cond8, w_ada, b_ada.reshape(depth, 1, n))


class _Geom:
    def __init__(self, batch, ctx_len, seq_len, d_model, tm):
        self.b, self.l, self.t, self.d = batch, ctx_len, seq_len, d_model
        self.s = ctx_len + seq_len
        self.n = batch * self.s
        self.tm = tm
        assert ctx_len % tm == 0 and seq_len % tm == 0
        self.tpb = self.s // tm
        self.nct = ctx_len // tm
        self.ntiles = self.n // tm

    def mod_spec(self, nlead=0):
        tpb, nct = self.tpb, self.nct

        def imap(*ids):
            i = ids[nlead]
            return (i // tpb, ((i % tpb) >= nct).astype(jnp.int32), 0, 0)

        return pl.BlockSpec((None, None, 6, self.d), imap)


def _modmm_kernel(x_ref, g_ref, mod_ref, w_ref, o_ref, *h_ref, which):
    h = _modulate(x_ref[...], g_ref[...], mod_ref[3 * which:3 * which + 1, :],
                  mod_ref[3 * which + 1:3 * which + 2, :])
    if h_ref:
        h_ref[0][...] = h
    o_ref[...] = _dot(h.astype(BF16), w_ref[...])


def _mod_matmul(geo, x, g, mods, w, which, emit_h):
    n, d = x.shape
    nn = w.shape[1]
    tm = geo.tm
    out_shape = [jax.ShapeDtypeStruct((n, nn), F32)]
    out_specs = [pl.BlockSpec((tm, nn), lambda i: (i, 0))]
    if emit_h:
        out_shape.append(jax.ShapeDtypeStruct((n, d), F32))
        out_specs.append(pl.BlockSpec((tm, d), lambda i: (i, 0)))
    res = pl.pallas_call(
        functools.partial(_modmm_kernel, which=which),
        grid=(geo.ntiles,),
        in_specs=[
            pl.BlockSpec((tm, d), lambda i: (i, 0)),
            pl.BlockSpec((1, d), lambda i: (0, 0)),
            geo.mod_spec(),
            pl.BlockSpec((d, nn), lambda i: (0, 0)),
        ],
        out_specs=out_specs,
        out_shape=out_shape,
        compiler_params=_cparams("parallel"),
        name="mod_matmul",
    )(x, g.reshape(1, d), mods, w)
    return res if emit_h else res[0]


def _mmres_kernel(y_ref, w_ref, x_ref, mod_ref, o_ref, *, gidx):
    acc = _dot(y_ref[...], w_ref[...])
    o_ref[...] = x_ref[...] + mod_ref[gidx:gidx + 1, :] * acc


def _matmul_res(geo, y, w, x, mods, gidx):
    n, k = y.shape
    d = x.shape[1]
    tm = geo.tm
    return pl.pallas_call(
        functools.partial(_mmres_kernel, gidx=gidx),
        grid=(geo.ntiles,),
        in_specs=[
            pl.BlockSpec((tm, k), lambda i: (i, 0)),
            pl.BlockSpec((k, d), lambda i: (0, 0)),
            pl.BlockSpec((tm, d), lambda i: (i, 0)),
            geo.mod_spec(),
        ],
        out_specs=pl.BlockSpec((tm, d), lambda i: (i, 0)),
        out_shape=jax.ShapeDtypeStruct((n, d), F32),
        compiler_params=_cparams("parallel"),
        name="matmul_res",
    )(y, w, x, mods)


def _bmm_kernel(x_ref, w_ref, o_ref, *, acts):
    j = pl.program_id(0)
    y = _dot(x_ref[...], w_ref[...])
    out = y
    for jj, a in enumerate(acts):
        if a == "tanh":
            out = jnp.where(j == jj, jnp.tanh(y), out)
        elif a == "sigmoid":
            out = jnp.where(j == jj, _sigmoid(y), out)
    o_ref[...] = out.astype(o_ref.dtype)


def _bmm(x3, w3, src, acts, out_dtype, tm):
    _, n, k = x3.shape
    nj, _, nn = w3.shape
    src = tuple(src)

    def xmap(j, i):
        idx = jnp.int32(src[0])
        for jj in range(1, nj):
            idx = jnp.where(j == jj, jnp.int32(src[jj]), idx)
        return (idx, i, 0)

    return pl.pallas_call(
        functools.partial(_bmm_kernel, acts=tuple(acts)),
        grid=(nj, n // tm),
        in_specs=[
            pl.BlockSpec((None, tm, k), xmap),
            pl.BlockSpec((None, k, nn), lambda j, i: (j, 0, 0)),
        ],
        out_specs=pl.BlockSpec((None, tm, nn), lambda j, i: (j, i, 0)),
        out_shape=jax.ShapeDtypeStruct((nj, n, nn), out_dtype),
        compiler_params=_cparams("parallel", "parallel"),
        name="bmm",
    )(x3, w3)


def _rwmix_kernel(x_ref, xp_ref, xn_ref, g_ref, mod_ref, mix_ref, o_ref, *, tpb, nct):
    i = pl.program_id(0)
    tm = x_ref.shape[0]
    g = g_ref[...]
    shift = mod_ref[0:1, :]
    scale = mod_ref[1:2, :]
    h = _modulate(x_ref[...], g, shift, scale)
    hp = _modulate(xp_ref[...], g, shift, scale)[SUBLANES - 1:SUBLANES, :]
    hn = _modulate(xn_ref[...], g, shift, scale)[0:1, :]
    it = i % tpb
    first = jnp.logical_or(it == 0, it == nct)
    last = jnp.logical_or(it == nct - 1, it == tpb - 1)
    hp = jnp.where(first, 0.0, hp)
    hn = jnp.where(last, 0.0, hn)
    rows = lax.broadcasted_iota(jnp.int32, h.shape, 0)
    prev = jnp.where(rows == 0, hp, pltpu.roll(h, 1, axis=0))
    nxt = jnp.where(rows == tm - 1, hn, pltpu.roll(h, tm - 1, axis=0))
    xx = 0.5 * (prev + nxt) - h
    for m in range(6):
        o_ref[m] = (h + xx * mix_ref[m:m + 1, :]).astype(BF16)


def _rwkv_mix(geo, x, g, mods, mix):
    n, d = x.shape
    tm = geo.tm
    r8 = tm // SUBLANES
    nblk8 = n // SUBLANES
    return pl.pallas_call(
        functools.partial(_rwmix_kernel, tpb=geo.tpb, nct=geo.nct),
        grid=(geo.ntiles,),
        in_specs=[
            pl.BlockSpec((tm, d), lambda i: (i, 0)),
            pl.BlockSpec((SUBLANES, d), lambda i: (jnp.maximum(i * r8 - 1, 0), 0)),
            pl.BlockSpec((SUBLANES, d), lambda i: (jnp.minimum((i + 1) * r8, nblk8 - 1), 0)),
            pl.BlockSpec((1, d), lambda i: (0, 0)),
            geo.mod_spec(),
            pl.BlockSpec((6, d), lambda i: (0, 0)),
        ],
        out_specs=pl.BlockSpec((6, tm, d), lambda i: (0, i, 0)),
        out_shape=jax.ShapeDtypeStruct((6, n, d), BF16),
        compiler_params=_cparams("parallel"),
        name="rwkv_mix",
    )(x, x, x, g.reshape(1, d), mods, mix)


def _head_sum(x, bd):
    return _dot(x.astype(BF16), bd)


def _rwfeat_kernel(k_ref, v_ref, tl_ref, w2_ref, a2_ref, g2_ref, v2_ref, pv_ref, bd_ref,
                   *rest, has_vlora):
    if has_vlora:
        vf_ref, lw_ref, kd_ref, as_ref, kk_ref, g_ref, vo_ref = rest
    else:
        lw_ref, kd_ref, as_ref, kk_ref, g_ref = rest
    k = k_ref[...]
    tw = tl_ref[0]
    ta = tl_ref[1]
    tg = tl_ref[2]
    w0 = pv_ref[0:2, :]
    a0 = pv_ref[2:4, :]
    k_k = pv_ref[5:6, :]
    k_a = pv_ref[6:7, :]
    for z in range(2):
        sl = slice(LORA_PAD * z, LORA_PAD * (z + 1))
        lora_w = _dot(tw[:, sl], w2_ref[z])
        w = -_softplus(-(w0[z:z + 1, :] + lora_w)) - 0.5
        lw_ref[z] = -jnp.exp(w)
        a_sig = _sigmoid(a0[z:z + 1, :] + _dot(ta[:, sl], a2_ref[z]))
        as_ref[z] = a_sig.astype(BF16)
        kd_ref[z] = k * (1.0 + (a_sig - 1.0) * k_a)
    g_ref[...] = _dot(tg, g2_ref[...]).astype(BF16)
    kkr = k * k_k
    ss = _head_sum(kkr * kkr, bd_ref[...])
    kk_ref[...] = (kkr * lax.rsqrt(ss + 1e-12)).astype(BF16)
    if has_vlora:
        v = v_ref[...]
        tv = tl_ref[3]
        gate = _sigmoid(pv_ref[4:5, :] + _dot(tv[:, 0:LORA_PAD], v2_ref[...]))
        vo_ref[...] = v + (vf_ref[...] - v) * gate


def _rwkv_feat(geo, rkv, tl, w2p, a2p, g2, v2p, pvec, vfirst):
    _, n, d = rkv.shape
    tm, tc = geo.tm, COL_TILE
    has_vlora = vfirst is not None
    lt = tl.shape[2]
    ii = lax.broadcasted_iota(jnp.int32, (tc, tc), 0) // RW_HEAD
    jj = lax.broadcasted_iota(jnp.int32, (tc, tc), 1) // RW_HEAD
    bd = (ii == jj).astype(BF16)
    row = lambda i, j: (i, j)
    in_specs = [
        pl.BlockSpec((None, tm, tc), lambda i, j: (1, i, j)),
        pl.BlockSpec((None, tm, tc), lambda i, j: (2, i, j)),
        pl.BlockSpec((4, tm, lt), lambda i, j: (0, i, 0)),
        pl.BlockSpec((2, LORA_PAD, tc), lambda i, j: (0, 0, j)),
        pl.BlockSpec((2, LORA_PAD, tc), lambda i, j: (0, 0, j)),
        pl.BlockSpec((lt, tc), lambda i, j: (0, j)),
        pl.BlockSpec((LORA_PAD, tc), lambda i, j: (0, j)),
        pl.BlockSpec((SUBLANES, tc), lambda i, j: (0, j)),
        pl.BlockSpec((tc, tc), lambda i, j: (0, 0)),
    ]
    args = [rkv, rkv, tl, w2p, a2p, g2, v2p, pvec, bd]
    dir_spec = pl.BlockSpec((2, tm, tc), lambda i, j: (0, i, j))
    out_specs = [dir_spec, dir_spec, dir_spec, pl.BlockSpec((tm, tc), row), pl.BlockSpec((tm, tc), row)]
    out_shape = [jax.ShapeDtypeStruct((2, n, d), dt) for dt in (F32, F32, BF16)]
    out_shape += [jax.ShapeDtypeStruct((n, d), BF16)] * 2
    if has_vlora:
        in_specs.append(pl.BlockSpec((tm, tc), row))
        args.append(vfirst)
        out_specs.append(pl.BlockSpec((tm, tc), row))
        out_shape.append(jax.ShapeDtypeStruct((n, d), F32))
    res = pl.pallas_call(
        functools.partial(_rwfeat_kernel, has_vlora=has_vlora),
        grid=(geo.ntiles, d // tc),
        in_specs=in_specs,
        out_specs=out_specs,
        out_shape=out_shape,
        compiler_params=_cparams("parallel", "parallel"),
        name="rwkv_feat",
    )(*args)
    if has_vlora:
        lw, kd, asg, kk, g, v = res
    else:
        lw, kd, asg, kk, g = res
        v = rkv[2]
    return lw, kd, asg, kk, g, v


def _wkv_chunk_kernel(r_ref, v_ref, kk_ref, lw_ref, kd_ref, as_ref,
                      m_ref, ga_ref, rq_ref, o0_ref, pc_ref, *, npairs):
    c = WKV_CHUNK
    c2 = 2 * c
    sgn = 1 - 2 * pl.program_id(0)
    ri = lax.broadcasted_iota(jnp.int32, (c2, c2), 0)
    ci = lax.broadcasted_iota(jnp.int32, (c2, c2), 1)
    same = (ri >= c) == (ci >= c)
    tt = jnp.where(ri >= c, ri - c, ri)
    ss = jnp.where(ci >= c, ci - c, ci)
    earlier = (ss - tt) * sgn < 0
    strict = jnp.logical_and(same, earlier)
    incl = jnp.logical_and(same, jnp.logical_or(earlier, ss == tt))
    eye = (ri == ci).astype(F32)
    r64 = lax.broadcasted_iota(jnp.int32, (c, c), 0)
    c64 = lax.broadcasted_iota(jnp.int32, (c, c), 1)
    ltri = jnp.where((c64 - r64) * sgn <= 0, 1.0, 0.0).astype(BF16)
    head0 = lax.broadcasted_iota(jnp.int32, (c, LANES), 1) < RW_HEAD
    pairs = range(npairs)

    def stack(x):
        return jnp.concatenate([jnp.where(head0, x, 0.0), jnp.where(head0, 0.0, x)], axis=0)

    def dup(x):
        return jnp.concatenate([x, x], axis=0)

    def bf(x):
        return x.astype(BF16)

    lhs, rhs, a2, bp2, kp2, vst, r2 = [], [], [], [], [], [], []
    lw_all = lw_ref[...]
    cum_all = _dot_exact_lhs(ltri, lw_all)
    for p in pairs:
        sl = slice(LANES * p, LANES * (p + 1))
        lw = lw_all[:, sl]
        cum = cum_all[:, sl]
        tot = jnp.sum(lw, axis=0, keepdims=True)
        p_inv = jnp.exp(-cum)
        p_end = jnp.exp(tot - cum)
        kk = kk_ref[:, sl]
        b = kk * as_ref[:, sl]
        kd = kd_ref[:, sl]
        a2p = stack(-kk * jnp.exp(cum - lw))
        r2p = stack(r_ref[:, sl] * jnp.exp(cum))
        a2.append(bf(a2p))
        r2.append(r2p)
        lhs.append(jnp.concatenate([a2[p], bf(r2p)], axis=0))
        rhs.append(jnp.concatenate([dup(bf(b * p_inv)), dup(bf(kd * p_inv))], axis=0))
        bp2.append(bf(stack(b * p_end)))
        kp2.append(bf(stack(kd * p_end)))
        vst.append(bf(stack(v_ref[:, sl])))
        pc_ref[:, sl] = jnp.broadcast_to(jnp.exp(tot), (SUBLANES, LANES))
    gram = [_dot_nt(lhs[p], rhs[p]) for p in pairs]
    nab = [jnp.where(strict, gram[p][0:c2, 0:c2], 0.0) for p in pairs]
    nrb = [bf(jnp.where(incl, gram[p][c2:2 * c2, 0:c2], 0.0)) for p in pairs]
    nk = [bf(jnp.concatenate([jnp.where(strict, gram[p][0:c2, c2:2 * c2], 0.0),
                              jnp.where(incl, gram[p][c2:2 * c2, c2:2 * c2], 0.0)], axis=0))
          for p in pairs]
    quads = range(npairs // 2)

    def side(x0, x1):
        return jnp.concatenate([x0, x1], axis=1)

    def diag(x0, x1):
        z0 = jnp.zeros_like(x0)
        return jnp.concatenate([side(x0, z0), side(z0, x1)], axis=0)

    def diag_halves(x):
        return diag(x[:, 0:c2], x[:, c2:2 * c2])

    def unside(xs):
        return [xs[p // 2][:, c2 * (p % 2):c2 * (p % 2 + 1)] for p in pairs]

    xv = unside([_dot(side(nk[2 * q], nk[2 * q + 1]), diag(vst[2 * q], vst[2 * q + 1]))
                 for q in quads])
    tinv = [side(eye + nab[2 * q], eye + nab[2 * q + 1]) for q in quads]
    npow = [bf(side(nab[2 * q], nab[2 * q + 1])) for q in quads]
    for _ in range(int(math.log2(c)) - 1):
        npow = [bf(_dot(npow[q], diag_halves(npow[q]))) for q in quads]
        tinv = [tinv[q] + _dot(bf(tinv[q]), diag_halves(npow[q])) for q in quads]
    tinv = unside(tinv)
    y = [_dot(bf(tinv[p]), jnp.concatenate([a2[p], bf(xv[p][0:c2, :])], axis=1)) for p in pairs]
    yb = [bf(y[p]) for p in pairs]
    z = [_dot(nrb[p], yb[p]) for p in pairs]
    mg = unside([_dot(bf(side(y[2 * q].T, y[2 * q + 1].T)), diag(bp2[2 * q], bp2[2 * q + 1]))
                 for q in quads])
    vk = unside([_dot(bf(side(vst[2 * q].astype(F32).T, vst[2 * q + 1].astype(F32).T)),
                      diag(kp2[2 * q], kp2[2 * q + 1])) for q in quads])
    for p in pairs:
        sl = slice(LANES * p, LANES * (p + 1))
        m_ref[:, sl] = bf(mg[p][0:c2, :])
        ga_ref[:, sl] = bf(mg[p][c2:2 * c2, :] + vk[p])
        rq_ref[:, sl] = bf(r2[p] + z[p][:, 0:c2])
        o0 = z[p][:, c2:2 * c2] + xv[p][c2:2 * c2, :]
        o0_ref[:, sl] = bf(o0[0:c, :] + o0[c:c2, :])


def _wkv_scan_kernel(m_ref, ga_ref, rq_ref, o0_ref, pc_ref, o_ref, g_scr, *, npairs):
    c = WKV_CHUNK

    @pl.when(pl.program_id(2) == 0)
    def _():
        g_scr[...] = jnp.zeros_like(g_scr)

    for p in range(npairs):
        sl = slice(LANES * p, LANES * (p + 1))
        g = g_scr[p]
        g_hi = g.astype(BF16)
        o_st = _dot_nt(rq_ref[:, sl], g_hi)
        o_ref[:, sl] = o_st[0:c, :] + o_st[c:2 * c, :] + o0_ref[:, sl]
        m = m_ref[:, sl]
        g_scr[p] = g * pc_ref[0:1, sl] + _dot(g_hi, m) + ga_ref[:, sl]


WKV_CHUNK_LANES = 2048


def _wkv_bidir(r, v, kk, lw, kd, asg, batch, ctx_len):
    n, d = r.shape
    c = WKV_CHUNK
    s = n // batch
    ncs = s // c
    ncc = ctx_len // c
    nch = n // c
    lanes = min(WKV_CHUNK_LANES, d)
    ngrp = d // lanes
    shared = pl.BlockSpec((c, lanes), lambda z, i, j: (i, j))
    perdir = pl.BlockSpec((None, c, lanes), lambda z, i, j: (z, i, j))
    big = pl.BlockSpec((None, 2 * c, lanes), lambda z, i, j: (z, i, j))
    m_, ga_, rq_, o0_, pc_ = pl.pallas_call(
        functools.partial(_wkv_chunk_kernel, npairs=lanes // LANES),
        grid=(2, nch, ngrp),
        in_specs=[shared, shared, shared, perdir, perdir, perdir],
        out_specs=[big, big, big, perdir,
                   pl.BlockSpec((None, SUBLANES, lanes), lambda z, i, j: (z, i, j))],
        out_shape=[jax.ShapeDtypeStruct((2, nch * 2 * c, d), BF16)] * 3
        + [jax.ShapeDtypeStruct((2, n, d), BF16),
           jax.ShapeDtypeStruct((2, nch * SUBLANES, d), F32)],
        compiler_params=_cparams("parallel", "parallel", "parallel"),
        name="wkv_chunk",
    )(r, v, kk, lw, kd, asg)

    def cmap(z, b, cc):
        back = jnp.where(cc < ncc, ncc - 1 - cc, ncs + ncc - 1 - cc)
        return (z, b * ncs + jnp.where(z == 0, cc, back), 0)

    out = pl.pallas_call(
        functools.partial(_wkv_scan_kernel, npairs=d // LANES),
        grid=(2, batch, ncs),
        in_specs=[
            pl.BlockSpec((None, 2 * c, d), cmap),
            pl.BlockSpec((None, 2 * c, d), cmap),
            pl.BlockSpec((None, 2 * c, d), cmap),
            pl.BlockSpec((None, c, d), cmap),
            pl.BlockSpec((None, SUBLANES, d), cmap),
        ],
        out_specs=pl.BlockSpec((None, c, d), cmap),
        out_shape=jax.ShapeDtypeStruct((2, n, d), F32),
        scratch_shapes=[pltpu.VMEM((d // LANES, 2 * c, 2 * c), F32)],
        compiler_params=_cparams("parallel", "parallel", "arbitrary"),
        name="wkv_scan",
    )(m_, ga_, rq_, o0_, pc_)
    return out


def _rwread_kernel(o_ref, r_ref, kd_ref, v_ref, g_ref, pv_ref, bd_ref, y_ref):
    bd = bd_ref[...]
    wkv = o_ref[0] + o_ref[1]
    inv = 1.0 / RW_HEAD
    mu = _head_sum(wkv, bd) * inv
    dev = wkv - mu
    var = _head_sum(dev * dev, bd) * inv
    y = dev * lax.rsqrt(var + GN_EPS) * pv_ref[1:2, :] + pv_ref[2:3, :]
    rk = r_ref[...] * (kd_ref[0] + kd_ref[1]) * pv_ref[0:1, :]
    y = y + _head_sum(rk, bd) * v_ref[...]
    y_ref[...] = (y * g_ref[...]).astype(BF16)


def _rwkv_readout(geo, wkv, r, kd, v, g, pvec):
    n, d = r.shape
    tm, tc = geo.tm, COL_TILE
    ii = lax.broadcasted_iota(jnp.int32, (tc, tc), 0) // RW_HEAD
    jj = lax.broadcasted_iota(jnp.int32, (tc, tc), 1) // RW_HEAD
    bd = (ii == jj).astype(BF16)
    row = pl.BlockSpec((tm, tc), lambda i, j: (i, j))
    dirs = pl.BlockSpec((2, tm, tc), lambda i, j: (0, i, j))
    return pl.pallas_call(
        _rwread_kernel,
        grid=(geo.ntiles, d // tc),
        in_specs=[dirs, row, dirs, row, row,
                  pl.BlockSpec((SUBLANES, tc), lambda i, j: (0, j)),
                  pl.BlockSpec((tc, tc), lambda i, j: (0, 0))],
        out_specs=row,
        out_shape=jax.ShapeDtypeStruct((n, d), BF16),
        compiler_params=_cparams("parallel", "parallel"),
        name="rwkv_readout",
    )(wkv, r, kd, v, g, pvec, bd)


def _mlaqkv_kernel(z_ref, qn_ref, kvn_ref, wuq_ref, wukv_ref, gq_ref, gk_ref, cos_ref, sin_ref,
                   q_ref, k_ref, v_ref, *, q_lora, kv_lora):
    z = z_ref[...]
    cq = z[:, 0:q_lora]
    ckv = z[:, q_lora:q_lora + kv_lora]
    krot = z[:, q_lora + kv_lora:q_lora + kv_lora + LANES]

    def rms(x, g):
        ms = jnp.mean(x * x, axis=-1, keepdims=True)
        return x * lax.rsqrt(ms + EPS) * g

    qf = _dot(rms(cq, qn_ref[...]).astype(BF16), wuq_ref[...])
    kvf = _dot(rms(ckv, kvn_ref[...]).astype(BF16), wukv_ref[...])
    cos = cos_ref[...]
    sin = sin_ref[...]
    half = QK_ROPE // 2
    lane = lax.broadcasted_iota(jnp.int32, cos.shape, 1)

    def rope(x):
        up = pltpu.roll(x, LANES - half, axis=1)
        dn = pltpu.roll(x, half, axis=1)
        return x * cos + jnp.where(lane < half, up, dn) * sin

    inv_w = 1.0 / (QK_NOPE + QK_ROPE)
    gq = gq_ref[...]
    gk = gk_ref[...]
    kr_ss = jnp.sum(krot * krot, axis=-1, keepdims=True)
    for h in range(MLA_HEADS):
        o = QK_PAD * h
        qh = qf[:, o:o + QK_PAD]
        rs = lax.rsqrt(jnp.sum(qh * qh, axis=-1, keepdims=True) * inv_w + EPS)
        qn = qh * rs * gq * (ATTN_SCALE * math.log2(math.e))
        q_ref[:, o:o + QK_NOPE] = qn[:, 0:QK_NOPE].astype(BF16)
        q_ref[:, o + QK_NOPE:o + QK_PAD] = rope(qn[:, QK_NOPE:QK_PAD]).astype(BF16)
        kn = kvf[:, o:o + QK_NOPE]
        rsk = lax.rsqrt((jnp.sum(kn * kn, axis=-1, keepdims=True) + kr_ss) * inv_w + EPS)
        k_ref[:, o:o + QK_NOPE] = (kn * rsk * gk[:, 0:QK_NOPE]).astype(BF16)
        k_ref[:, o + QK_NOPE:o + QK_PAD] = rope(krot * rsk * gk[:, QK_NOPE:QK_PAD]).astype(BF16)
        v_ref[:, V_HEAD * h:V_HEAD * (h + 1)] = kvf[:, o + QK_NOPE:o + QK_PAD].astype(BF16)


def _mla_qkv(geo, z, qn, kvn, wuq_p, wukv, gq_p, gk_p, cos_t, sin_t):
    n, zw = z.shape
    tm = geo.tm
    q_lora, kv_lora = qn.shape[0], kvn.shape[0]
    hq = MLA_HEADS * QK_PAD
    tpb = geo.tpb
    full = lambda i: (0, 0)
    rowmap = lambda i: (i, 0)
    return pl.pallas_call(
        functools.partial(_mlaqkv_kernel, q_lora=q_lora, kv_lora=kv_lora),
        grid=(geo.ntiles,),
        in_specs=[
            pl.BlockSpec((tm, zw), rowmap),
            pl.BlockSpec((1, q_lora), full),
            pl.BlockSpec((1, kv_lora), full),
            pl.BlockSpec((q_lora, hq), full),
            pl.BlockSpec((kv_lora, hq), full),
            pl.BlockSpec((1, QK_PAD), full),
            pl.BlockSpec((1, QK_PAD), full),
            pl.BlockSpec((tm, LANES), lambda i: (i % tpb, 0)),
            pl.BlockSpec((tm, LANES), lambda i: (i % tpb, 0)),
        ],
        out_specs=[pl.BlockSpec((tm, hq), rowmap), pl.BlockSpec((tm, hq), rowmap),
                   pl.BlockSpec((tm, MLA_HEADS * V_HEAD), rowmap)],
        out_shape=[jax.ShapeDtypeStruct((n, hq), BF16), jax.ShapeDtypeStruct((n, hq), BF16),
                   jax.ShapeDtypeStruct((n, MLA_HEADS * V_HEAD), BF16)],
        compiler_params=_cparams("parallel"),
        name="mla_qkv",
    )(z, qn.reshape(1, -1), kvn.reshape(1, -1), wuq_p, wukv, gq_p, gk_p, cos_t, sin_t)


ATTN_SUBTILES = 2


def _attn_kernel(q_ref, k_ref, vt_ref, o_ref):
    k = k_ref[...]
    vt = vt_ref[...]
    cols = q_ref.shape[0] // ATTN_SUBTILES
    subs = range(ATTN_SUBTILES)
    s = [_dot_nt(k, q_ref[cols * i:cols * (i + 1), :]) for i in subs]
    m = [jnp.max(s[i], axis=0, keepdims=True) for i in subs]
    p = [jnp.exp2(s[i] - m[i]) for i in subs]
    l = [jnp.sum(p[i], axis=0, keepdims=True) for i in subs]
    for i in subs:
        ot = _dot(vt, p[i].astype(BF16)) / l[i]
        o_ref[cols * i:cols * (i + 1), :] = ot.T.astype(BF16)


def _attn_call(q3, k3, vt4, tq, nkeys):
    b, sq, _ = q3.shape
    return pl.pallas_call(
        _attn_kernel,
        grid=(b, MLA_HEADS, sq // tq),
        in_specs=[
            pl.BlockSpec((None, tq, QK_PAD), lambda bb, h, i: (bb, i, h)),
            pl.BlockSpec((None, nkeys, QK_PAD), lambda bb, h, i: (bb, 0, h)),
            pl.BlockSpec((None, None, V_HEAD, nkeys), lambda bb, h, i: (bb, h, 0, 0)),
        ],
        out_specs=pl.BlockSpec((None, tq, V_HEAD), lambda bb, h, i: (bb, i, h)),
        out_shape=jax.ShapeDtypeStruct((b, sq, MLA_HEADS * V_HEAD), BF16),
        compiler_params=_cparams("parallel", "parallel", "parallel"),
        name="mla_attention",
    )(q3, k3, vt4)


def _mla_attention(geo, q, k, v):
    b, s, l = geo.b, geo.s, geo.l
    q3 = q.reshape(b, s, MLA_HEADS * QK_PAD)
    k3 = k.reshape(b, s, MLA_HEADS * QK_PAD)
    vt4 = v.reshape(b, s, MLA_HEADS, V_HEAD).transpose(0, 2, 3, 1)
    o_ctx = _attn_call(q3[:, :l], k3, vt4, geo.tm, l)
    tq_lat = 2 * geo.tm if geo.t % (2 * geo.tm) == 0 else geo.tm
    o_lat = _attn_call(q3[:, l:], k3, vt4, tq_lat, s)
    return jnp.concatenate([o_ctx, o_lat], axis=1).reshape(b * s, MLA_HEADS * V_HEAD)


def _topk_rows(s, k, payload=None):
    rows = lax.broadcasted_iota(jnp.int32, s.shape, 0).astype(F32)
    big = float(s.shape[0])
    vals, idxs = [], []
    for _ in range(k):
        m = jnp.max(s, axis=0, keepdims=True)
        idx = jnp.min(jnp.where(s == m, rows, big), axis=0, keepdims=True)
        hit = rows == idx
        vals.append(m)
        if payload is None:
            idxs.append(idx)
        else:
            idxs.append(jnp.sum(jnp.where(hit, payload, 0.0), axis=0, keepdims=True))
        s = jnp.where(hit, -jnp.inf, s)
    return jnp.concatenate(vals, axis=0), jnp.concatenate(idxs, axis=0)


def _peersel_kernel(q_ref, qn_ref, keys_ref, e_ref, g_ref):
    q = q_ref[...]
    ms = jnp.mean(q * q, axis=-1, keepdims=True)
    qn = q * lax.rsqrt(ms + EPS) * qn_ref[...]
    half = D_KEY // 2
    s1 = _dot3_nt(keys_ref[0], qn[:, 0:half])
    s2 = _dot3_nt(keys_ref[1], qn[:, half:D_KEY])
    t1, i1 = _topk_rows(s1, PEER_TOPK)
    t2, i2 = _topk_rows(s2, PEER_TOPK)
    k = PEER_TOPK
    sub = lax.broadcasted_iota(jnp.int32, (SUBLANES, t1.shape[1]), 0)
    cand = [t1[0:1, :] + t2]
    cidx = [i1[0:1, :] * float(N_KEYS) + i2]
    for p in range(1, k // 2):
        live = sub < k // (p + 1)
        cand.append(jnp.where(live, t1[p:p + 1, :] + t2[0:SUBLANES, :], -jnp.inf))
        cidx.append(i1[p:p + 1, :] * float(N_KEYS) + i2[0:SUBLANES, :])
    cand.append(t1[k // 2:k, :] + t2[0:1, :])
    cidx.append(i1[k // 2:k, :] * float(N_KEYS) + i2[0:1, :])
    best, eidx = _topk_rows(jnp.concatenate(cand, axis=0), k, payload=jnp.concatenate(cidx, axis=0))
    ex = jnp.exp(best - jnp.max(best, axis=0, keepdims=True))
    g_ref[...] = ex / jnp.sum(ex, axis=0, keepdims=True)
    e_ref[...] = eidx.astype(jnp.int32)


def _peer_select(qp, q_norm, keys, tm, row0, n):
    assert row0 % tm == 0 and n % tm == 0
    tile0 = row0 // tm
    return pl.pallas_call(
        _peersel_kernel,
        grid=(n // tm, PEER_HEADS),
        in_specs=[
            pl.BlockSpec((tm, D_KEY), lambda i, h: (tile0 + i, h)),
            pl.BlockSpec((1, D_KEY), lambda i, h: (0, 0)),
            pl.BlockSpec((2, N_KEYS, D_KEY // 2), lambda i, h: (0, 0, 0)),
        ],
        out_specs=[pl.BlockSpec((PEER_TOPK, tm), lambda i, h: (h, i)),
                   pl.BlockSpec((PEER_TOPK, tm), lambda i, h: (h, i))],
        out_shape=[jax.ShapeDtypeStruct((PEER_HEADS * PEER_TOPK, n), jnp.int32),
                   jax.ShapeDtypeStruct((PEER_HEADS * PEER_TOPK, n), F32)],
        compiler_params=_cparams("parallel", "parallel"),
        name="peer_select",
    )(qp, q_norm.reshape(1, D_KEY), keys)


GATHER_SLOTS = 4
SLAB_PAD = 1


def _pack_expert_table(u, v):
    ne, d = u.shape
    ub = lax.bitcast_convert_type(u.astype(BF16), jnp.uint16).astype(jnp.uint32)
    vb = lax.bitcast_convert_type(v.astype(BF16), jnp.uint16).astype(jnp.uint32)
    return ((vb << 16) | ub).reshape(ne, d // LANES, LANES)


def _peergather_kernel(idx_ref, idxn_ref, gate_ref, h_ref, x_ref, mod_ref, tab_ref, o_ref,
                       *scratch, tb):
    nsel = PEER_HEADS * PEER_TOPK
    nrow = h_ref.shape[1] // LANES
    pitch = nrow + SLAB_PAD
    ns = GATHER_SLOTS
    bufs, sem = scratch[:ns], scratch[ns]
    lane_t = lax.broadcasted_iota(jnp.int32, (nsel, tb), 1)
    g2 = mod_ref[5:6, :]
    step = pl.program_id(0)
    nsteps = pl.num_programs(0)

    def row_copy(ids_ref, t, j, slot):
        return pltpu.make_async_copy(
            tab_ref.at[ids_ref[t, j]],
            bufs[slot].at[pl.ds(j * pitch, nrow), :],
            sem.at[slot])

    def issue(ids_ref, t, slot):
        for j in range(nsel):
            row_copy(ids_ref, t, j, slot).start(priority=j % 2)

    def wait(t, slot):
        for j in range(nsel):
            row_copy(idx_ref, t, j, slot).wait()

    def packed(slot, s):
        return bufs[slot][pl.ds(s, nsel, stride=pitch), :]

    def compute(t, slot):
        hrow = h_ref[pl.ds(t, 1), :]
        acc = jnp.zeros((nsel, LANES), F32)
        for s in range(nrow):
            u = lax.bitcast_convert_type(packed(slot, s) << 16, F32)
            acc = acc + u * hrow[:, LANES * s:LANES * (s + 1)]
        dots = jnp.sum(acc, axis=-1, keepdims=True)
        gcol = jnp.sum(jnp.where(lane_t == t, gate_ref[...], 0.0), axis=-1, keepdims=True)
        coef = gcol * _gelu(dots)
        outs = []
        for s in range(nrow):
            vv = lax.bitcast_convert_type(packed(slot, s) & jnp.uint32(0xFFFF0000), F32)
            outs.append(jnp.sum(coef * vv, axis=0, keepdims=True))
        orow = jnp.concatenate(outs, axis=1)
        o_ref[pl.ds(t, 1), :] = x_ref[pl.ds(t, 1), :] + g2 * orow

    @pl.when(step == 0)
    def _():
        for s in range(ns - 1):
            issue(idx_ref, s, s)

    ngroups = tb // ns

    def body(g, carry):
        for s in range(ns):
            t = g * ns + s
            wait(t, s)
            issue(idx_ref, t + ns - 1, (s + ns - 1) % ns)
            compute(t, s)
        return carry

    lax.fori_loop(0, ngroups - 1, body, 0)
    for s in range(ns):
        t = (ngroups - 1) * ns + s
        wait(t, s)
        if s == 0:
            issue(idx_ref, tb - 1, ns - 1)
        else:
            @pl.when(step < nsteps - 1)
            def _():
                issue(idxn_ref, s - 1, s - 1)
        compute(t, s)


def _peer_gather(geo_g, eidx, gate_t, h2, x, mods, table, n):
    d = x.shape[1]
    tb = geo_g.tm
    nsel = PEER_HEADS * PEER_TOPK
    pitch = d // LANES + SLAB_PAD
    assert n % tb == 0
    nsteps = n // tb
    return pl.pallas_call(
        functools.partial(_peergather_kernel, tb=tb),
        grid=(nsteps,),
        in_specs=[
            pl.BlockSpec((tb, nsel), lambda i: (i, 0), memory_space=pltpu.SMEM),
            pl.BlockSpec((tb, nsel), lambda i: (jnp.minimum(i + 1, nsteps - 1), 0),
                         memory_space=pltpu.SMEM),
            pl.BlockSpec((nsel, tb), lambda i: (0, i)),
            pl.BlockSpec((tb, d), lambda i: (i, 0)),
            pl.BlockSpec((tb, d), lambda i: (i, 0)),
            geo_g.mod_spec(),
            pl.BlockSpec(memory_space=pl.ANY),
        ],
        out_specs=pl.BlockSpec((tb, d), lambda i: (i, 0)),
        out_shape=jax.ShapeDtypeStruct((n, d), F32),
        scratch_shapes=[pltpu.VMEM((nsel * pitch, LANES), jnp.uint32)] * GATHER_SLOTS
        + [pltpu.SemaphoreType.DMA((GATHER_SLOTS,))],
        compiler_params=_cparams("arbitrary"),
        name="peer_gather",
    )(eidx, eidx, gate_t, h2, x, mods, table)


SC_CORES = 2
SC_SUBCORES = 16
SC_LANES = 16
SC_CHUNK = 16
SC_SHARE = 0.3235

_ERF_ALPHA = (0.00022905065861350646, 0.0034082910107109506, 0.050955695062380861,
              0.18520832239976145, 1.128379143519084)
_ERF_BETA = (-1.1791602954361697e-7, 0.000023547966471313185, 0.0010179625278914885,
             0.014070470171167667, 0.11098505178285362, 0.49746925110067538, 1.0)
_ERF_CLAMP = 3.832506856900711


def _erf_rational(x):
    x = jnp.minimum(jnp.maximum(x, -_ERF_CLAMP), _ERF_CLAMP)
    x2 = x * x
    p = jnp.full_like(x, _ERF_ALPHA[0])
    for c in _ERF_ALPHA[1:]:
        p = p * x2 + c
    q = jnp.full_like(x, _ERF_BETA[0])
    for c in _ERF_BETA[1:]:
        q = q * x2 + c
    return x * p / q


def _peer_sc(eidx, gate, h, table, row0):
    n, nsel = eidx.shape
    d = h.shape[1]
    nw = SC_CORES * SC_SUBCORES
    assert n % (2 * nw) == 0 and nsel % SC_CHUNK == 0 and d % SC_LANES == 0
    tpw = n // nw
    nchunk = nsel // SC_CHUNK
    nvec = d // SC_LANES
    mesh = plsc.VectorSubcoreMesh(core_axis_name="c", subcore_axis_name="s",
                                  num_cores=SC_CORES, num_subcores=SC_SUBCORES)

    @functools.partial(
        pl.kernel, mesh=mesh,
        out_type=jax.ShapeDtypeStruct((n, d), F32),
        scratch_types=[
            pltpu.VMEM((nsel,), jnp.int32), pltpu.VMEM((nsel,), jnp.int32),
            pltpu.VMEM((nsel,), F32), pltpu.VMEM((nsel,), F32),
            pltpu.VMEM((d,), F32), pltpu.VMEM((d,), F32),
            pltpu.VMEM((d,), F32),
            pltpu.VMEM((SC_CHUNK, d), jnp.uint32),
            pltpu.VMEM((SC_CHUNK, d), jnp.uint32),
            pltpu.SemaphoreType.DMA, pltpu.SemaphoreType.DMA,
            pltpu.SemaphoreType.DMA, pltpu.SemaphoreType.DMA,
        ],
        compiler_params=pltpu.CompilerParams(needs_layout_passes=False),
        name="peer_sc",
    )
    def sc_kernel(eidx_hbm, gate_hbm, h_hbm, tab_hbm, y_hbm, idx_a, idx_b, gate_a, gate_b, h_a, h_b,
                  o_v, rows0, rows1, sem0, sem1, msem_a, msem_b):
        wid = lax.axis_index("s") * SC_CORES + lax.axis_index("c")
        lanes = lax.iota(jnp.int32, SC_LANES)
        zero = jnp.zeros((SC_LANES,), F32)
        bufs = ((rows0, sem0), (rows1, sem1))
        meta = ((idx_a, gate_a, h_a, msem_a), (idx_b, gate_b, h_b, msem_b))

        def meta_copies(tok, s):
            idx_v, gate_v, h_v, msem = meta[s]
            return (pltpu.make_async_copy(eidx_hbm.at[tok], idx_v, msem),
                    pltpu.make_async_copy(gate_hbm.at[tok], gate_v, msem),
                    pltpu.make_async_copy(h_hbm.at[row0 + tok], h_v, msem))

        def gather(s, c, slot):
            rows, sem = bufs[slot]
            return pltpu.make_async_copy(
                tab_hbm.at[meta[s][0].at[pl.ds(c * SC_CHUNK, SC_CHUNK)]], rows, sem)

        def chunk_compute(s, c, rows):
            _, gate_v, h_v, _ = meta[s]

            def dot_body(i, accs):
                hv = h_v[pl.ds(i * SC_LANES, SC_LANES)]
                out = []
                for e in range(SC_CHUNK):
                    w = rows[e, pl.ds(i * SC_LANES, SC_LANES)]
                    out.append(accs[e] + lax.bitcast_convert_type(w << 16, F32) * hv)
                return tuple(out)

            accs = lax.fori_loop(0, nvec, dot_body, (zero,) * SC_CHUNK)
            dots = zero
            for e in range(SC_CHUNK):
                dots = jnp.where(lanes == e, jnp.sum(accs[e]), dots)
            act = 0.5 * dots * (1.0 + _erf_rational(dots * (2.0 ** -0.5)))
            coef = gate_v[pl.ds(c * SC_CHUNK, SC_CHUNK)] * act
            splat = [jnp.full((SC_LANES,), jnp.sum(jnp.where(lanes == e, coef, 0.0)), F32)
                     for e in range(SC_CHUNK)]

            def acc_body(i, carry):
                o = o_v[pl.ds(i * SC_LANES, SC_LANES)]
                for e in range(SC_CHUNK):
                    w = rows[e, pl.ds(i * SC_LANES, SC_LANES)]
                    o = o + splat[e] * lax.bitcast_convert_type(w & jnp.uint32(0xFFFF0000), F32)
                o_v[pl.ds(i * SC_LANES, SC_LANES)] = o
                return carry

            lax.fori_loop(0, nvec, acc_body, 0)

        def zero_body(i, c2):
            o_v[pl.ds(i * SC_LANES, SC_LANES)] = zero
            return c2

        def token(tok, s, has_next):
            def when_next(fn):
                if isinstance(has_next, bool):
                    if has_next:
                        fn()
                else:
                    pl.when(has_next)(fn)

            def load_next():
                for cp in meta_copies(tok + 1, 1 - s):
                    cp.start()

            when_next(load_next)
            lax.fori_loop(0, nvec, zero_body, 0)
            for c in range(nchunk):
                if c + 1 < nchunk:
                    gather(s, c + 1, (c + 1) % 2).start()
                gather(s, c, c % 2).wait()
                chunk_compute(s, c, bufs[c % 2][0])

            def prefetch_next():
                for cp in meta_copies(tok + 1, 1 - s):
                    cp.wait()
                gather(1 - s, 0, 0).start()

            when_next(prefetch_next)
            pltpu.sync_copy(o_v, y_hbm.at[tok])

        base = wid * tpw
        for cp in meta_copies(base, 0):
            cp.start()
        for cp in meta_copies(base, 0):
            cp.wait()
        gather(0, 0, 0).start()

        def pair_body(g, carry):
            token(base + 2 * g, 0, True)
            token(base + 2 * g + 1, 1, g + 1 < tpw // 2)
            return carry

        lax.fori_loop(0, tpw // 2, pair_body, 0)

    return sc_kernel(eidx, gate, h, table)


def _pad_to(x, axis, size):
    pad = [(0, 0)] * x.ndim
    pad[axis] = (0, size - x.shape[axis])
    return jnp.pad(x, pad)


def _rwkv_layer(geo, xs, mods, norm1, mix, w_rkv, w_o, w0, w1, w2, a0, a1, a2, vl, g1, g2,
                k_k, k_a, r_k, ln_w, ln_b, vfirst):
    d = geo.d
    xm = _rwkv_mix(geo, xs, norm1, mods, mix)
    rkv = _bmm(xm, w_rkv.astype(BF16), (0, 2, 3), ("none",) * 3, F32, geo.tm)
    lt = g1.shape[1]
    w1c = jnp.concatenate([_pad_to(w1[0], 1, LORA_PAD), _pad_to(w1[1], 1, LORA_PAD)], axis=1)
    a1c = jnp.concatenate([_pad_to(a1[0], 1, LORA_PAD), _pad_to(a1[1], 1, LORA_PAD)], axis=1)
    if vl is None:
        v1p = jnp.zeros((d, lt), F32)
        v2p = jnp.zeros((LORA_PAD, d), F32)
        v0 = jnp.zeros((d,), F32)
    else:
        v0, v1, v2 = vl
        v1p = _pad_to(v1, 1, lt)
        v2p = _pad_to(v2, 0, LORA_PAD)
    wl1 = jnp.stack([_pad_to(w1c, 1, lt), _pad_to(a1c, 1, lt), g1, v1p]).astype(BF16)
    tl = _bmm(xm, wl1, (1, 4, 5, 3), ("tanh", "none", "sigmoid", "none"), BF16, geo.tm)
    w2p = jnp.stack([_pad_to(w2[0], 0, LORA_PAD), _pad_to(w2[1], 0, LORA_PAD)]).astype(BF16)
    a2p = jnp.stack([_pad_to(a2[0], 0, LORA_PAD), _pad_to(a2[1], 0, LORA_PAD)]).astype(BF16)
    pvec = jnp.stack([w0[0], w0[1], a0[0], a0[1], v0, k_k, k_a, jnp.zeros_like(k_k)])
    lw, kd, asg, kk, g, v = _rwkv_feat(geo, rkv, tl, w2p, a2p, g2.astype(BF16), v2p.astype(BF16),
                                      pvec, vfirst)
    r = rkv[0]
    wkv = _wkv_bidir(r, v, kk, lw, kd, asg, geo.b, geo.l)
    pv2 = _pad_to(jnp.stack([r_k, ln_w, ln_b]), 0, SUBLANES)
    y = _rwkv_readout(geo, wkv, r, kd, v, g, pv2)
    xs = _matmul_res(geo, y, w_o.astype(BF16), xs, mods, 2)
    return xs, v


def _rope_tables(geo):
    t = geo.t
    pos = jnp.arange(t)
    row = (pos // GRID_W).astype(F32)
    col = (pos % GRID_W).astype(F32)
    n_freq = QK_ROPE // 4
    inv_freq = ROPE_THETA ** (-jnp.arange(n_freq, dtype=F32) / n_freq)
    ang = jnp.concatenate([row[:, None] * inv_freq, col[:, None] * inv_freq], axis=-1)
    cos, sin = jnp.cos(ang), jnp.sin(ang)
    pad = LANES - QK_ROPE
    cos_l = jnp.concatenate([cos, cos, jnp.ones((t, pad), F32)], axis=1)
    sin_l = jnp.concatenate([-sin, sin, jnp.zeros((t, pad), F32)], axis=1)
    cos_c = jnp.ones((geo.l, LANES), F32)
    sin_c = jnp.zeros((geo.l, LANES), F32)
    return jnp.concatenate([cos_c, cos_l], axis=0), jnp.concatenate([sin_c, sin_l], axis=0)


def _mla_layer(geo, xs, mods, norm1, rope_t, w_in, q_norm, kv_norm, w_uq, w_ukv, g_q, g_k, w_o):
    q_lora, kv_lora = q_norm.shape[0], kv_norm.shape[0]
    zw = q_lora + kv_lora + LANES
    z = _mod_matmul(geo, xs, norm1, mods, _pad_to(w_in, 1, zw).astype(BF16), 0, False)
    qk = QK_NOPE + QK_ROPE
    wuq_p = _pad_to(w_uq.reshape(q_lora, MLA_HEADS, qk), 2, QK_PAD).reshape(q_lora, -1)
    gq_p = _pad_to(g_q, 0, QK_PAD).reshape(1, QK_PAD)
    gk_p = _pad_to(g_k, 0, QK_PAD).reshape(1, QK_PAD)
    q, k, v = _mla_qkv(geo, z, q_norm, kv_norm, wuq_p.astype(BF16), w_ukv.astype(BF16),
                       gq_p, gk_p, *rope_t)
    o = _mla_attention(geo, q, k, v)
    return _matmul_res(geo, o, w_o.astype(BF16), xs, mods, 2)


def _peer_layer(geo, geo_g, xs, mods, norm2, w_q, q_norm, keys, u, v):
    qp, h2 = _mod_matmul(geo, xs, norm2, mods, w_q.astype(BF16), 1, True)
    table = _pack_expert_table(u, v)
    n, d = xs.shape
    n_sc = geo.tm * round(SC_SHARE * n / geo.tm)
    if n_sc % (2 * SC_CORES * SC_SUBCORES) != 0:
        n_sc = 0
    n_tc = n - n_sc

    def select(row0, nrows):
        wide = 2 * geo.tm
        tsel = wide if row0 % wide == 0 and nrows % wide == 0 else geo.tm
        return _peer_select(qp, q_norm, keys, tsel, row0, nrows)

    if n_sc:
        eidx_sc, gate_sc = select(n_tc, n_sc)
        y_sc = _peer_sc(eidx_sc.T, gate_sc.T, h2, table.reshape(table.shape[0], d), n_tc)
    eidx_t, gate_t = select(0, n_tc)
    out_tc = _peer_gather(geo_g, eidx_t.T, gate_t, h2, xs, mods, table, n_tc)
    if n_sc == 0:
        return out_tc
    g2, r = [], n_tc
    while r < n:
        bi, pos = divmod(r, geo.s)
        seg = int(pos >= geo.l)
        stop = min(n, bi * geo.s + (geo.s if seg else geo.l))
        g2.append(jnp.broadcast_to(mods[bi, seg, 5], (stop - r, d)))
        r = stop
    out_sc = xs[n_tc:] + jnp.concatenate(g2, axis=0) * y_sc
    return jnp.concatenate([out_tc, out_sc], axis=0)


def kernel(x, c, ctx, c_ctx, w_ada, b_ada, norm1, norm2, rw_mix, rw_wrkv, rw_wo, rw_w0, rw_w1, rw_w2, rw_a0, rw_a1, rw_a2, rw_v0, rw_v1, rw_v2, rw_g1, rw_g2, rw_kk, rw_ka, rw_rk, rw_lnw, rw_lnb, mla_win, mla_qnorm, mla_kvnorm, mla_wuq, mla_wukv, mla_gq, mla_gk, mla_wo, peer_wq, peer_qnorm, peer_keys, peer_u, peer_v):
    b, t, d = x.shape
    l = ctx.shape[1]
    depth = w_ada.shape[0]
    geo = _Geom(b, l, t, d, min(ROW_TILE, l))
    geo_g = _Geom(b, l, t, d, min(GATHER_TILE, l))
    cond8 = _pad_to(jnp.concatenate([c, c_ctx[None, :]], axis=0), 0, SUBLANES)
    ada = _adaln(cond8, w_ada, b_ada).reshape(depth, SUBLANES, 6, d)
    mods_all = jnp.stack([jnp.broadcast_to(ada[:, b:b + 1], (depth, b, 6, d)), ada[:, 0:b]], axis=2)
    xs = jnp.concatenate([ctx, x], axis=1).reshape(b * (l + t), d)
    rope_t = _rope_tables(geo)
    vfirst = None
    for i in range(depth):
        j = i // 2
        mods = mods_all[i]
        if i % 2 == 0:
            vl = None if j == 0 else (rw_v0[j - 1], rw_v1[j - 1], rw_v2[j - 1])
            xs, vcur = _rwkv_layer(geo, xs, mods, norm1[i], rw_mix[j], rw_wrkv[j], rw_wo[j],
                                   rw_w0[j], rw_w1[j], rw_w2[j], rw_a0[j], rw_a1[j], rw_a2[j], vl,
                                   rw_g1[j], rw_g2[j], rw_kk[j], rw_ka[j], rw_rk[j], rw_lnw[j],
                                   rw_lnb[j], vfirst)
            if j == 0:
                vfirst = vcur
        else:
            xs = _mla_layer(geo, xs, mods, norm1[i], rope_t, mla_win[j], mla_qnorm[j],
                            mla_kvnorm[j], mla_wuq[j], mla_wukv[j], mla_gq[j], mla_gk[j], mla_wo[j])
        if i == depth - 1:
            xs = xs.reshape(b, l + t, d)[:, l:, :].reshape(b * t, d)
            geo, geo_g = _Geom(b, 0, t, d, geo.tm), _Geom(b, 0, t, d, geo_g.tm)
        xs = _peer_layer(geo, geo_g, xs, mods, norm2[i], peer_wq[i], peer_qnorm[i], peer_keys[i],
                         peer_u[i], peer_v[i])
    return xs.reshape(b, t, d)
```

```python
import functools
import math

import jax
import jax.numpy as jnp
from jax import lax
from jax.experimental import pallas as pl
from jax.experimental.pallas import tpu as pltpu
from jax.experimental.pallas import tpu_sc as plsc

F32 = jnp.float32
BF16 = jnp.bfloat16

EPS = 1e-6
GN_EPS = 64e-5
RW_HEAD = 64
WKV_CHUNK = 64
MLA_HEADS = 16
QK_NOPE = 128
QK_ROPE = 64
V_HEAD = 128
QK_PAD = 256
ROPE_THETA = 10000.0
GRID_W = 64
ATTN_SCALE = (QK_NOPE + QK_ROPE) ** -0.5
PEER_HEADS = 8
N_KEYS = 128
PEER_TOPK = 16
D_KEY = 256
LORA_PAD = 128

LANES = 128
SUBLANES = 8
VMEM_LIMIT = 56 * 1024 * 1024
ROW_TILE = 256
GATHER_TILE = 128
COL_TILE = 512
ADA_COLS = 1024


def _cparams(*sem):
    return pltpu.CompilerParams(dimension_semantics=sem, vmem_limit_bytes=VMEM_LIMIT)


def _dot(a, b):
    return jnp.dot(a, b, preferred_element_type=F32)


def _dot_nt(a, b):
    return lax.dot_general(a, b, (((1,), (1,)), ((), ())), preferred_element_type=F32)


def _split2(x):
    hi = x.astype(BF16)
    lo = (x - hi.astype(F32)).astype(BF16)
    return hi, lo


def _split3(x):
    hi = x.astype(BF16)
    r1 = x - hi.astype(F32)
    mid = r1.astype(BF16)
    lo = (r1 - mid.astype(F32)).astype(BF16)
    return hi, mid, lo


def _dot3(a, b):
    ah, al = _split2(a)
    bh, bl = _split2(b)
    return _dot(ah, bh) + (_dot(ah, bl) + _dot(al, bh))


def _dot3_nt(a, b):
    ah, al = _split2(a)
    bh, bl = _split2(b)
    return _dot_nt(ah, bh) + (_dot_nt(ah, bl) + _dot_nt(al, bh))


def _dot_exact_lhs(sel, x):
    hi, mid, lo = _split3(x)
    return _dot(sel, hi) + (_dot(sel, mid) + _dot(sel, lo))


def _modulate(x, g, shift, scale):
    ms = jnp.mean(x * x, axis=-1, keepdims=True)
    return (x * lax.rsqrt(ms + EPS) * g) * (1.0 + scale) + shift


def _sigmoid(x):
    return 1.0 / (1.0 + jnp.exp(-x))


def _softplus(y):
    return jnp.maximum(y, 0.0) + jnp.log(1.0 + jnp.exp(-jnp.abs(y)))


def _erf(x):
    return lax.erf(x)


def _gelu(x):
    return 0.5 * x * (1.0 + _erf(x * (2.0 ** -0.5)))


def _ada_kernel(s_ref, w_ref, b_ref, o_ref):
    s = s_ref[...]
    s = s * _sigmoid(s)
    o_ref[...] = _dot3(s, w_ref[...]) + b_ref[...]


def _adaln(cond8, w_ada, b_ada):
    depth, d, n = w_ada.shape
    tn = min(ADA_COLS, n)
    return pl.pallas_call(
        _ada_kernel,
        grid=(depth, n // tn),
        in_specs=[
            pl.BlockSpec((SUBLANES, d), lambda l, j: (0, 0)),
            pl.BlockSpec((None, d, tn), lambda l, j: (l, 0, j)),
            pl.BlockSpec((None, 1, tn), lambda l, j: (l, 0, j)),
        ],
        out_specs=pl.BlockSpec((None, SUBLANES, tn), lambda l, j: (l, 0, j)),
        out_shape=jax.ShapeDtypeStruct((depth, SUBLANES, n), F32),
        compiler_params=_cparams("parallel", "parallel"),
        name="adaln",
    )(cond8, w_ada, b_ada.reshape(depth, 1, n))


class _Geom:
    def __init__(self, batch, ctx_len, seq_len, d_model, tm):
        self.b, self.l, self.t, self.d = batch, ctx_len, seq_len, d_model
        self.s = ctx_len + seq_len
        self.n = batch * self.s
        self.tm = tm
        assert ctx_len % tm == 0 and seq_len % tm == 0
        self.tpb = self.s // tm
        self.nct = ctx_len // tm
        self.ntiles = self.n // tm

    def mod_spec(self, nlead=0):
        tpb, nct = self.tpb, self.nct

        def imap(*ids):
            i = ids[nlead]
            return (i // tpb, ((i % tpb) >= nct).astype(jnp.int32), 0, 0)

        return pl.BlockSpec((None, None, 6, self.d), imap)


def _modmm_kernel(x_ref, g_ref, mod_ref, w_ref, o_ref, *h_ref, which):
    h = _modulate(x_ref[...], g_ref[...], mod_ref[3 * which:3 * which + 1, :],
                  mod_ref[3 * which + 1:3 * which + 2, :])
    if h_ref:
        h_ref[0][...] = h
    o_ref[...] = _dot(h.astype(BF16), w_ref[...])


def _mod_matmul(geo, x, g, mods, w, which, emit_h):
    n, d = x.shape
    nn = w.shape[1]
    tm = geo.tm
    out_shape = [jax.ShapeDtypeStruct((n, nn), F32)]
    out_specs = [pl.BlockSpec((tm, nn), lambda i: (i, 0))]
    if emit_h:
        out_shape.append(jax.ShapeDtypeStruct((n, d), F32))
        out_specs.append(pl.BlockSpec((tm, d), lambda i: (i, 0)))
    res = pl.pallas_call(
        functools.partial(_modmm_kernel, which=which),
        grid=(geo.ntiles,),
        in_specs=[
            pl.BlockSpec((tm, d), lambda i: (i, 0)),
            pl.BlockSpec((1, d), lambda i: (0, 0)),
            geo.mod_spec(),
            pl.BlockSpec((d, nn), lambda i: (0, 0)),
        ],
        out_specs=out_specs,
        out_shape=out_shape,
        compiler_params=_cparams("parallel"),
        name="mod_matmul",
    )(x, g.reshape(1, d), mods, w)
    return res if emit_h else res[0]


def _mmres_kernel(y_ref, w_ref, x_ref, mod_ref, o_ref, *, gidx):
    acc = _dot(y_ref[...], w_ref[...])
    o_ref[...] = x_ref[...] + mod_ref[gidx:gidx + 1, :] * acc


def _matmul_res(geo, y, w, x, mods, gidx):
    n, k = y.shape
    d = x.shape[1]
    tm = geo.tm
    return pl.pallas_call(
        functools.partial(_mmres_kernel, gidx=gidx),
        grid=(geo.ntiles,),
        in_specs=[
            pl.BlockSpec((tm, k), lambda i: (i, 0)),
            pl.BlockSpec((k, d), lambda i: (0, 0)),
            pl.BlockSpec((tm, d), lambda i: (i, 0)),
            geo.mod_spec(),
        ],
        out_specs=pl.BlockSpec((tm, d), lambda i: (i, 0)),
        out_shape=jax.ShapeDtypeStruct((n, d), F32),
        compiler_params=_cparams("parallel"),
        name="matmul_res",
    )(y, w, x, mods)


def _bmm_kernel(x_ref, w_ref, o_ref, *, acts):
    j = pl.program_id(0)
    y = _dot(x_ref[...], w_ref[...])
    out = y
    for jj, a in enumerate(acts):
        if a == "tanh":
            out = jnp.where(j == jj, jnp.tanh(y), out)
        elif a == "sigmoid":
            out = jnp.where(j == jj, _sigmoid(y), out)
    o_ref[...] = out.astype(o_ref.dtype)


def _bmm(x3, w3, src, acts, out_dtype, tm):
    _, n, k = x3.shape
    nj, _, nn = w3.shape
    src = tuple(src)

    def xmap(j, i):
        idx = jnp.int32(src[0])
        for jj in range(1, nj):
            idx = jnp.where(j == jj, jnp.int32(src[jj]), idx)
        return (idx, i, 0)

    return pl.pallas_call(
        functools.partial(_bmm_kernel, acts=tuple(acts)),
        grid=(nj, n // tm),
        in_specs=[
            pl.BlockSpec((None, tm, k), xmap),
            pl.BlockSpec((None, k, nn), lambda j, i: (j, 0, 0)),
        ],
        out_specs=pl.BlockSpec((None, tm, nn), lambda j, i: (j, i, 0)),
        out_shape=jax.ShapeDtypeStruct((nj, n, nn), out_dtype),
        compiler_params=_cparams("parallel", "parallel"),
        name="bmm",
    )(x3, w3)


def _rwmix_kernel(x_ref, xp_ref, xn_ref, g_ref, mod_ref, mix_ref, o_ref, *, tpb, nct):
    i = pl.program_id(0)
    tm = x_ref.shape[0]
    g = g_ref[...]
    shift = mod_ref[0:1, :]
    scale = mod_ref[1:2, :]
    h = _modulate(x_ref[...], g, shift, scale)
    hp = _modulate(xp_ref[...], g, shift, scale)[SUBLANES - 1:SUBLANES, :]
    hn = _modulate(xn_ref[...], g, shift, scale)[0:1, :]
    it = i % tpb
    first = jnp.logical_or(it == 0, it == nct)
    last = jnp.logical_or(it == nct - 1, it == tpb - 1)
    hp = jnp.where(first, 0.0, hp)
    hn = jnp.where(last, 0.0, hn)
    rows = lax.broadcasted_iota(jnp.int32, h.shape, 0)
    prev = jnp.where(rows == 0, hp, pltpu.roll(h, 1, axis=0))
    nxt = jnp.where(rows == tm - 1, hn, pltpu.roll(h, tm - 1, axis=0))
    xx = 0.5 * (prev + nxt) - h
    for m in range(6):
        o_ref[m] = (h + xx * mix_ref[m:m + 1, :]).astype(BF16)


def _rwkv_mix(geo, x, g, mods, mix):
    n, d = x.shape
    tm = geo.tm
    r8 = tm // SUBLANES
    nblk8 = n // SUBLANES
    return pl.pallas_call(
        functools.partial(_rwmix_kernel, tpb=geo.tpb, nct=geo.nct),
        grid=(geo.ntiles,),
        in_specs=[
            pl.BlockSpec((tm, d), lambda i: (i, 0)),
            pl.BlockSpec((SUBLANES, d), lambda i: (jnp.maximum(i * r8 - 1, 0), 0)),
            pl.BlockSpec((SUBLANES, d), lambda i: (jnp.minimum((i + 1) * r8, nblk8 - 1), 0)),
            pl.BlockSpec((1, d), lambda i: (0, 0)),
            geo.mod_spec(),
            pl.BlockSpec((6, d), lambda i: (0, 0)),
        ],
        out_specs=pl.BlockSpec((6, tm, d), lambda i: (0, i, 0)),
        out_shape=jax.ShapeDtypeStruct((6, n, d), BF16),
        compiler_params=_cparams("parallel"),
        name="rwkv_mix",
    )(x, x, x, g.reshape(1, d), mods, mix)


def _head_sum(x, bd):
    return _dot(x.astype(BF16), bd)


def _rwfeat_kernel(k_ref, v_ref, tl_ref, w2_ref, a2_ref, g2_ref, v2_ref, pv_ref, bd_ref,
                   *rest, has_vlora):
    if has_vlora:
        vf_ref, lw_ref, kd_ref, as_ref, kk_ref, g_ref, vo_ref = rest
    else:
        lw_ref, kd_ref, as_ref, kk_ref, g_ref = rest
    k = k_ref[...]
    tw = tl_ref[0]
    ta = tl_ref[1]
    tg = tl_ref[2]
    w0 = pv_ref[0:2, :]
    a0 = pv_ref[2:4, :]
    k_k = pv_ref[5:6, :]
    k_a = pv_ref[6:7, :]
    for z in range(2):
        sl = slice(LORA_PAD * z, LORA_PAD * (z + 1))
        lora_w = _dot(tw[:, sl], w2_ref[z])
        w = -_softplus(-(w0[z:z + 1, :] + lora_w)) - 0.5
        lw_ref[z] = -jnp.exp(w)
        a_sig = _sigmoid(a0[z:z + 1, :] + _dot(ta[:, sl], a2_ref[z]))
        as_ref[z] = a_sig.astype(BF16)
        kd_ref[z] = k * (1.0 + (a_sig - 1.0) * k_a)
    g_ref[...] = _dot(tg, g2_ref[...]).astype(BF16)
    kkr = k * k_k
    ss = _head_sum(kkr * kkr, bd_ref[...])
    kk_ref[...] = (kkr * lax.rsqrt(ss + 1e-12)).astype(BF16)
    if has_vlora:
        v = v_ref[...]
        tv = tl_ref[3]
        gate = _sigmoid(pv_ref[4:5, :] + _dot(tv[:, 0:LORA_PAD], v2_ref[...]))
        vo_ref[...] = v + (vf_ref[...] - v) * gate


def _rwkv_feat(geo, rkv, tl, w2p, a2p, g2, v2p, pvec, vfirst):
    _, n, d = rkv.shape
    tm, tc = geo.tm, COL_TILE
    has_vlora = vfirst is not None
    lt = tl.shape[2]
    ii = lax.broadcasted_iota(jnp.int32, (tc, tc), 0) // RW_HEAD
    jj = lax.broadcasted_iota(jnp.int32, (tc, tc), 1) // RW_HEAD
    bd = (ii == jj).astype(BF16)
    row = lambda i, j: (i, j)
    in_specs = [
        pl.BlockSpec((None, tm, tc), lambda i, j: (1, i, j)),
        pl.BlockSpec((None, tm, tc), lambda i, j: (2, i, j)),
        pl.BlockSpec((4, tm, lt), lambda i, j: (0, i, 0)),
        pl.BlockSpec((2, LORA_PAD, tc), lambda i, j: (0, 0, j)),
        pl.BlockSpec((2, LORA_PAD, tc), lambda i, j: (0, 0, j)),
        pl.BlockSpec((lt, tc), lambda i, j: (0, j)),
        pl.BlockSpec((LORA_PAD, tc), lambda i, j: (0, j)),
        pl.BlockSpec((SUBLANES, tc), lambda i, j: (0, j)),
        pl.BlockSpec((tc, tc), lambda i, j: (0, 0)),
    ]
    args = [rkv, rkv, tl, w2p, a2p, g2, v2p, pvec, bd]
    dir_spec = pl.BlockSpec((2, tm, tc), lambda i, j: (0, i, j))
    out_specs = [dir_spec, dir_spec, dir_spec, pl.BlockSpec((tm, tc), row), pl.BlockSpec((tm, tc), row)]
    out_shape = [jax.ShapeDtypeStruct((2, n, d), dt) for dt in (F32, F32, BF16)]
    out_shape += [jax.ShapeDtypeStruct((n, d), BF16)] * 2
    if has_vlora:
        in_specs.append(pl.BlockSpec((tm, tc), row))
        args.append(vfirst)
        out_specs.append(pl.BlockSpec((tm, tc), row))
        out_shape.append(jax.ShapeDtypeStruct((n, d), F32))
    res = pl.pallas_call(
        functools.partial(_rwfeat_kernel, has_vlora=has_vlora),
        grid=(geo.ntiles, d // tc),
        in_specs=in_specs,
        out_specs=out_specs,
        out_shape=out_shape,
        compiler_params=_cparams("parallel", "parallel"),
        name="rwkv_feat",
    )(*args)
    if has_vlora:
        lw, kd, asg, kk, g, v = res
    else:
        lw, kd, asg, kk, g = res
        v = rkv[2]
    return lw, kd, asg, kk, g, v


def _wkv_chunk_kernel(r_ref, v_ref, kk_ref, lw_ref, kd_ref, as_ref,
                      m_ref, ga_ref, rq_ref, o0_ref, pc_ref, *, npairs):
    c = WKV_CHUNK
    c2 = 2 * c
    sgn = 1 - 2 * pl.program_id(0)
    ri = lax.broadcasted_iota(jnp.int32, (c2, c2), 0)
    ci = lax.broadcasted_iota(jnp.int32, (c2, c2), 1)
    same = (ri >= c) == (ci >= c)
    tt = jnp.where(ri >= c, ri - c, ri)
    ss = jnp.where(ci >= c, ci - c, ci)
    earlier = (ss - tt) * sgn < 0
    strict = jnp.logical_and(same, earlier)
    incl = jnp.logical_and(same, jnp.logical_or(earlier, ss == tt))
    eye = (ri == ci).astype(F32)
    r64 = lax.broadcasted_iota(jnp.int32, (c, c), 0)
    c64 = lax.broadcasted_iota(jnp.int32, (c, c), 1)
    ltri = jnp.where((c64 - r64) * sgn <= 0, 1.0, 0.0).astype(BF16)
    head0 = lax.broadcasted_iota(jnp.int32, (c, LANES), 1) < RW_HEAD
    pairs = range(npairs)

    def stack(x):
        return jnp.concatenate([jnp.where(head0, x, 0.0), jnp.where(head0, 0.0, x)], axis=0)

    def dup(x):
        return jnp.concatenate([x, x], axis=0)

    def bf(x):
        return x.astype(BF16)

    lhs, rhs, a2, bp2, kp2, vst, r2 = [], [], [], [], [], [], []
    lw_all = lw_ref[...]
    cum_all = _dot_exact_lhs(ltri, lw_all)
    for p in pairs:
        sl = slice(LANES * p, LANES * (p + 1))
        lw = lw_all[:, sl]
        cum = cum_all[:, sl]
        tot = jnp.sum(lw, axis=0, keepdims=True)
        p_inv = jnp.exp(-cum)
        p_end = jnp.exp(tot - cum)
        kk = kk_ref[:, sl]
        b = kk * as_ref[:, sl]
        kd = kd_ref[:, sl]
        a2p = stack(-kk * jnp.exp(cum - lw))
        r2p = stack(r_ref[:, sl] * jnp.exp(cum))
        a2.append(bf(a2p))
        r2.append(r2p)
        lhs.append(jnp.concatenate([a2[p], bf(r2p)], axis=0))
        rhs.append(jnp.concatenate([dup(bf(b * p_inv)), dup(bf(kd * p_inv))], axis=0))
        bp2.append(bf(stack(b * p_end)))
        kp2.append(bf(stack(kd * p_end)))
        vst.append(bf(stack(v_ref[:, sl])))
        pc_ref[:, sl] = jnp.broadcast_to(jnp.exp(tot), (SUBLANES, LANES))
    gram = [_dot_nt(lhs[p], rhs[p]) for p in pairs]
    nab = [jnp.where(strict, gram[p][0:c2, 0:c2], 0.0) for p in pairs]
    nrb = [bf(jnp.where(incl, gram[p][c2:2 * c2, 0:c2], 0.0)) for p in pairs]
    nk = [bf(jnp.concatenate([jnp.where(strict, gram[p][0:c2, c2:2 * c2], 0.0),
                              jnp.where(incl, gram[p][c2:2 * c2, c2:2 * c2], 0.0)], axis=0))
          for p in pairs]
    quads = range(npairs // 2)

    def side(x0, x1):
        return jnp.concatenate([x0, x1], axis=1)

    def diag(x0, x1):
        z0 = jnp.zeros_like(x0)
        return jnp.concatenate([side(x0, z0), side(z0, x1)], axis=0)

    def diag_halves(x):
        return diag(x[:, 0:c2], x[:, c2:2 * c2])

    def unside(xs):
        return [xs[p // 2][:, c2 * (p % 2):c2 * (p % 2 + 1)] for p in pairs]

    xv = unside([_dot(side(nk[2 * q], nk[2 * q + 1]), diag(vst[2 * q], vst[2 * q + 1]))
                 for q in quads])
    tinv = [side(eye + nab[2 * q], eye + nab[2 * q + 1]) for q in quads]
    npow = [bf(side(nab[2 * q], nab[2 * q + 1])) for q in quads]
    for _ in range(int(math.log2(c)) - 1):
        npow = [bf(_dot(npow[q], diag_halves(npow[q]))) for q in quads]
        tinv = [tinv[q] + _dot(bf(tinv[q]), diag_halves(npow[q])) for q in quads]
    tinv = unside(tinv)
    y = [_dot(bf(tinv[p]), jnp.concatenate([a2[p], bf(xv[p][0:c2, :])], axis=1)) for p in pairs]
    yb = [bf(y[p]) for p in pairs]
    z = [_dot(nrb[p], yb[p]) for p in pairs]
    mg = unside([_dot(bf(side(y[2 * q].T, y[2 * q + 1].T)), diag(bp2[2 * q], bp2[2 * q + 1]))
                 for q in quads])
    vk = unside([_dot(bf(side(vst[2 * q].astype(F32).T, vst[2 * q + 1].astype(F32).T)),
                      diag(kp2[2 * q], kp2[2 * q + 1])) for q in quads])
    for p in pairs:
        sl = slice(LANES * p, LANES * (p + 1))
        m_ref[:, sl] = bf(mg[p][0:c2, :])
        ga_ref[:, sl] = bf(mg[p][c2:2 * c2, :] + vk[p])
        rq_ref[:, sl] = bf(r2[p] + z[p][:, 0:c2])
        o0 = z[p][:, c2:2 * c2] + xv[p][c2:2 * c2, :]
        o0_ref[:, sl] = bf(o0[0:c, :] + o0[c:c2, :])


def _wkv_scan_kernel(m_ref, ga_ref, rq_ref, o0_ref, pc_ref, o_ref, g_scr, *, npairs):
    c = WKV_CHUNK

    @pl.when(pl.program_id(2) == 0)
    def _():
        g_scr[...] = jnp.zeros_like(g_scr)

    for p in range(npairs):
        sl = slice(LANES * p, LANES * (p + 1))
        g = g_scr[p]
        g_hi = g.astype(BF16)
        o_st = _dot_nt(rq_ref[:, sl], g_hi)
        o_ref[:, sl] = (o_st[0:c, :] + o_st[c:2 * c, :] + o0_ref[:, sl]).astype(BF16)
        m = m_ref[:, sl]
        g_scr[p] = g * pc_ref[0:1, sl] + _dot(g_hi, m) + ga_ref[:, sl]


WKV_CHUNK_LANES = 2048


def _wkv_bidir(r, v, kk, lw, kd, asg, batch, ctx_len):
    n, d = r.shape
    c = WKV_CHUNK
    s = n // batch
    ncs = s // c
    ncc = ctx_len // c
    nch = n // c
    lanes = min(WKV_CHUNK_LANES, d)
    ngrp = d // lanes
    shared = pl.BlockSpec((c, lanes), lambda z, i, j: (i, j))
    perdir = pl.BlockSpec((None, c, lanes), lambda z, i, j: (z, i, j))
    big = pl.BlockSpec((None, 2 * c, lanes), lambda z, i, j: (z, i, j))
    m_, ga_, rq_, o0_, pc_ = pl.pallas_call(
        functools.partial(_wkv_chunk_kernel, npairs=lanes // LANES),
        grid=(2, nch, ngrp),
        in_specs=[shared, shared, shared, perdir, perdir, perdir],
        out_specs=[big, big, big, perdir,
                   pl.BlockSpec((None, SUBLANES, lanes), lambda z, i, j: (z, i, j))],
        out_shape=[jax.ShapeDtypeStruct((2, nch * 2 * c, d), BF16)] * 3
        + [jax.ShapeDtypeStruct((2, n, d), BF16),
           jax.ShapeDtypeStruct((2, nch * SUBLANES, d), F32)],
        compiler_params=_cparams("parallel", "parallel", "parallel"),
        name="wkv_chunk",
    )(r, v, kk, lw, kd, asg)

    def cmap(z, b, cc):
        back = jnp.where(cc < ncc, ncc - 1 - cc, ncs + ncc - 1 - cc)
        return (z, b * ncs + jnp.where(z == 0, cc, back), 0)

    out = pl.pallas_call(
        functools.partial(_wkv_scan_kernel, npairs=d // LANES),
        grid=(2, batch, ncs),
        in_specs=[
            pl.BlockSpec((None, 2 * c, d), cmap),
            pl.BlockSpec((None, 2 * c, d), cmap),
            pl.BlockSpec((None, 2 * c, d), cmap),
            pl.BlockSpec((None, c, d), cmap),
            pl.BlockSpec((None, SUBLANES, d), cmap),
        ],
        out_specs=pl.BlockSpec((None, c, d), cmap),
        out_shape=jax.ShapeDtypeStruct((2, n, d), BF16),
        scratch_shapes=[pltpu.VMEM((d // LANES, 2 * c, 2 * c), F32)],
        compiler_params=_cparams("parallel", "parallel", "arbitrary"),
        name="wkv_scan",
    )(m_, ga_, rq_, o0_, pc_)
    return out


def _rwread_kernel(o_ref, r_ref, kd_ref, v_ref, g_ref, pv_ref, bd_ref, y_ref):
    bd = bd_ref[...]
    wkv = o_ref[0].astype(F32) + o_ref[1].astype(F32)
    inv = 1.0 / RW_HEAD
    mu = _head_sum(wkv, bd) * inv
    dev = wkv - mu
    var = _head_sum(dev * dev, bd) * inv
    y = dev * lax.rsqrt(var + GN_EPS) * pv_ref[1:2, :] + pv_ref[2:3, :]
    rk = r_ref[...] * (kd_ref[0] + kd_ref[1]) * pv_ref[0:1, :]
    y = y + _head_sum(rk, bd) * v_ref[...]
    y_ref[...] = (y * g_ref[...]).astype(BF16)


def _rwkv_readout(geo, wkv, r, kd, v, g, pvec):
    n, d = r.shape
    tm, tc = geo.tm, COL_TILE
    ii = lax.broadcasted_iota(jnp.int32, (tc, tc), 0) // RW_HEAD
    jj = lax.broadcasted_iota(jnp.int32, (tc, tc), 1) // RW_HEAD
    bd = (ii == jj).astype(BF16)
    row = pl.BlockSpec((tm, tc), lambda i, j: (i, j))
    dirs = pl.BlockSpec((2, tm, tc), lambda i, j: (0, i, j))
    return pl.pallas_call(
        _rwread_kernel,
        grid=(geo.ntiles, d // tc),
        in_specs=[dirs, row, dirs, row, row,
                  pl.BlockSpec((SUBLANES, tc), lambda i, j: (0, j)),
                  pl.BlockSpec((tc, tc), lambda i, j: (0, 0))],
        out_specs=row,
        out_shape=jax.ShapeDtypeStruct((n, d), BF16),
        compiler_params=_cparams("parallel", "parallel"),
        name="rwkv_readout",
    )(wkv, r, kd, v, g, pvec, bd)


def _mlaqkv_kernel(z_ref, qn_ref, kvn_ref, wuq_ref, wukv_ref, gq_ref, gk_ref, cos_ref, sin_ref,
                   q_ref, k_ref, v_ref, *, q_lora, kv_lora):
    z = z_ref[...]
    cq = z[:, 0:q_lora]
    ckv = z[:, q_lora:q_lora + kv_lora]
    krot = z[:, q_lora + kv_lora:q_lora + kv_lora + LANES]

    def rms(x, g):
        ms = jnp.mean(x * x, axis=-1, keepdims=True)
        return x * lax.rsqrt(ms + EPS) * g

    qf = _dot(rms(cq, qn_ref[...]).astype(BF16), wuq_ref[...])
    kvf = _dot(rms(ckv, kvn_ref[...]).astype(BF16), wukv_ref[...])
    cos = cos_ref[...]
    sin = sin_ref[...]
    half = QK_ROPE // 2
    lane = lax.broadcasted_iota(jnp.int32, cos.shape, 1)

    def rope(x):
        up = pltpu.roll(x, LANES - half, axis=1)
        dn = pltpu.roll(x, half, axis=1)
        return x * cos + jnp.where(lane < half, up, dn) * sin

    inv_w = 1.0 / (QK_NOPE + QK_ROPE)
    gq = gq_ref[...]
    gk = gk_ref[...]
    kr_ss = jnp.sum(krot * krot, axis=-1, keepdims=True)
    for h in range(MLA_HEADS):
        o = QK_PAD * h
        qh = qf[:, o:o + QK_PAD]
        rs = lax.rsqrt(jnp.sum(qh * qh, axis=-1, keepdims=True) * inv_w + EPS)
        qn = qh * rs * gq * (ATTN_SCALE * math.log2(math.e))
        q_ref[:, o:o + QK_NOPE] = qn[:, 0:QK_NOPE].astype(BF16)
        q_ref[:, o + QK_NOPE:o + QK_PAD] = rope(qn[:, QK_NOPE:QK_PAD]).astype(BF16)
        kn = kvf[:, o:o + QK_NOPE]
        rsk = lax.rsqrt((jnp.sum(kn * kn, axis=-1, keepdims=True) + kr_ss) * inv_w + EPS)
        k_ref[:, o:o + QK_NOPE] = (kn * rsk * gk[:, 0:QK_NOPE]).astype(BF16)
        k_ref[:, o + QK_NOPE:o + QK_PAD] = rope(krot * rsk * gk[:, QK_NOPE:QK_PAD]).astype(BF16)
        v_ref[:, V_HEAD * h:V_HEAD * (h + 1)] = kvf[:, o + QK_NOPE:o + QK_PAD].astype(BF16)


def _mla_qkv(geo, z, qn, kvn, wuq_p, wukv, gq_p, gk_p, cos_t, sin_t):
    n, zw = z.shape
    tm = geo.tm
    q_lora, kv_lora = qn.shape[0], kvn.shape[0]
    hq = MLA_HEADS * QK_PAD
    tpb = geo.tpb
    full = lambda i: (0, 0)
    rowmap = lambda i: (i, 0)
    return pl.pallas_call(
        functools.partial(_mlaqkv_kernel, q_lora=q_lora, kv_lora=kv_lora),
        grid=(geo.ntiles,),
        in_specs=[
            pl.BlockSpec((tm, zw), rowmap),
            pl.BlockSpec((1, q_lora), full),
            pl.BlockSpec((1, kv_lora), full),
            pl.BlockSpec((q_lora, hq), full),
            pl.BlockSpec((kv_lora, hq), full),
            pl.BlockSpec((1, QK_PAD), full),
            pl.BlockSpec((1, QK_PAD), full),
            pl.BlockSpec((tm, LANES), lambda i: (i % tpb, 0)),
            pl.BlockSpec((tm, LANES), lambda i: (i % tpb, 0)),
        ],
        out_specs=[pl.BlockSpec((tm, hq), rowmap), pl.BlockSpec((tm, hq), rowmap),
                   pl.BlockSpec((tm, MLA_HEADS * V_HEAD), rowmap)],
        out_shape=[jax.ShapeDtypeStruct((n, hq), BF16), jax.ShapeDtypeStruct((n, hq), BF16),
                   jax.ShapeDtypeStruct((n, MLA_HEADS * V_HEAD), BF16)],
        compiler_params=_cparams("parallel"),
        name="mla_qkv",
    )(z, qn.reshape(1, -1), kvn.reshape(1, -1), wuq_p, wukv, gq_p, gk_p, cos_t, sin_t)


ATTN_SUBTILES = 2


def _attn_kernel(q_ref, k_ref, vt_ref, o_ref):
    k = k_ref[...]
    vt = vt_ref[...]
    cols = q_ref.shape[0] // ATTN_SUBTILES
    subs = range(ATTN_SUBTILES)
    s = [_dot_nt(k, q_ref[cols * i:cols * (i + 1), :]) for i in subs]
    m = [jnp.max(s[i], axis=0, keepdims=True) for i in subs]
    p = [jnp.exp2(s[i] - m[i]) for i in subs]
    l = [jnp.sum(p[i], axis=0, keepdims=True) for i in subs]
    for i in subs:
        ot = _dot(vt, p[i].astype(BF16)) / l[i]
        o_ref[cols * i:cols * (i + 1), :] = ot.T.astype(BF16)


def _attn_call(q3, k3, vt4, tq, nkeys):
    b, sq, _ = q3.shape
    return pl.pallas_call(
        _attn_kernel,
        grid=(b, MLA_HEADS, sq // tq),
        in_specs=[
            pl.BlockSpec((None, tq, QK_PAD), lambda bb, h, i: (bb, i, h)),
            pl.BlockSpec((None, nkeys, QK_PAD), lambda bb, h, i: (bb, 0, h)),
            pl.BlockSpec((None, None, V_HEAD, nkeys), lambda bb, h, i: (bb, h, 0, 0)),
        ],
        out_specs=pl.BlockSpec((None, tq, V_HEAD), lambda bb, h, i: (bb, i, h)),
        out_shape=jax.ShapeDtypeStruct((b, sq, MLA_HEADS * V_HEAD), BF16),
        compiler_params=_cparams("parallel", "parallel", "parallel"),
        name="mla_attention",
    )(q3, k3, vt4)


def _mla_attention(geo, q, k, v):
    b, s, l = geo.b, geo.s, geo.l
    q3 = q.reshape(b, s, MLA_HEADS * QK_PAD)
    k3 = k.reshape(b, s, MLA_HEADS * QK_PAD)
    vt4 = v.reshape(b, s, MLA_HEADS, V_HEAD).transpose(0, 2, 3, 1)
    o_ctx = _attn_call(q3[:, :l], k3, vt4, geo.tm, l)
    tq_lat = 2 * geo.tm if geo.t % (2 * geo.tm) == 0 else geo.tm
    o_lat = _attn_call(q3[:, l:], k3, vt4, tq_lat, s)
    return jnp.concatenate([o_ctx, o_lat], axis=1).reshape(b * s, MLA_HEADS * V_HEAD)


def _topk_rows(s, k, payload=None):
    rows = lax.broadcasted_iota(jnp.int32, s.shape, 0).astype(F32)
    big = float(s.shape[0])
    vals, idxs = [], []
    for _ in range(k):
        m = jnp.max(s, axis=0, keepdims=True)
        idx = jnp.min(jnp.where(s == m, rows, big), axis=0, keepdims=True)
        hit = rows == idx
        vals.append(m)
        if payload is None:
            idxs.append(idx)
        else:
            idxs.append(jnp.sum(jnp.where(hit, payload, 0.0), axis=0, keepdims=True))
        s = jnp.where(hit, -jnp.inf, s)
    return jnp.concatenate(vals, axis=0), jnp.concatenate(idxs, axis=0)


def _peersel_kernel(q_ref, qn_ref, keys_ref, e_ref, g_ref):
    q = q_ref[...]
    ms = jnp.mean(q * q, axis=-1, keepdims=True)
    qn = q * lax.rsqrt(ms + EPS) * qn_ref[...]
    half = D_KEY // 2
    s1 = _dot3_nt(keys_ref[0], qn[:, 0:half])
    s2 = _dot3_nt(keys_ref[1], qn[:, half:D_KEY])
    t1, i1 = _topk_rows(s1, PEER_TOPK)
    t2, i2 = _topk_rows(s2, PEER_TOPK)
    k = PEER_TOPK
    sub = lax.broadcasted_iota(jnp.int32, (SUBLANES, t1.shape[1]), 0)
    cand = [t1[0:1, :] + t2]
    cidx = [i1[0:1, :] * float(N_KEYS) + i2]
    for p in range(1, k // 2):
        live = sub < k // (p + 1)
        cand.append(jnp.where(live, t1[p:p + 1, :] + t2[0:SUBLANES, :], -jnp.inf))
        cidx.append(i1[p:p + 1, :] * float(N_KEYS) + i2[0:SUBLANES, :])
    cand.append(t1[k // 2:k, :] + t2[0:1, :])
    cidx.append(i1[k // 2:k, :] * float(N_KEYS) + i2[0:1, :])
    best, eidx = _topk_rows(jnp.concatenate(cand, axis=0), k, payload=jnp.concatenate(cidx, axis=0))
    ex = jnp.exp(best - jnp.max(best, axis=0, keepdims=True))
    g_ref[...] = ex / jnp.sum(ex, axis=0, keepdims=True)
    e_ref[...] = eidx.astype(jnp.int32)


def _peer_select(qp, q_norm, keys, tm, row0, n):
    assert row0 % tm == 0 and n % tm == 0
    tile0 = row0 // tm
    return pl.pallas_call(
        _peersel_kernel,
        grid=(n // tm, PEER_HEADS),
        in_specs=[
            pl.BlockSpec((tm, D_KEY), lambda i, h: (tile0 + i, h)),
            pl.BlockSpec((1, D_KEY), lambda i, h: (0, 0)),
            pl.BlockSpec((2, N_KEYS, D_KEY // 2), lambda i, h: (0, 0, 0)),
        ],
        out_specs=[pl.BlockSpec((PEER_TOPK, tm), lambda i, h: (h, i)),
                   pl.BlockSpec((PEER_TOPK, tm), lambda i, h: (h, i))],
        out_shape=[jax.ShapeDtypeStruct((PEER_HEADS * PEER_TOPK, n), jnp.int32),
                   jax.ShapeDtypeStruct((PEER_HEADS * PEER_TOPK, n), F32)],
        compiler_params=_cparams("parallel", "parallel"),
        name="peer_select",
    )(qp, q_norm.reshape(1, D_KEY), keys)


GATHER_SLOTS = 4
SLAB_PAD = 1


def _pack_expert_table(u, v):
    ne, d = u.shape
    ub = lax.bitcast_convert_type(u.astype(BF16), jnp.uint16).astype(jnp.uint32)
    vb = lax.bitcast_convert_type(v.astype(BF16), jnp.uint16).astype(jnp.uint32)
    return ((vb << 16) | ub).reshape(ne, d // LANES, LANES)


def _peergather_kernel(idx_ref, idxn_ref, gate_ref, h_ref, x_ref, mod_ref, tab_ref, o_ref,
                       *scratch, tb):
    nsel = PEER_HEADS * PEER_TOPK
    nrow = h_ref.shape[1] // LANES
    pitch = nrow + SLAB_PAD
    ns = GATHER_SLOTS
    bufs, sem = scratch[:ns], scratch[ns]
    lane_t = lax.broadcasted_iota(jnp.int32, (nsel, tb), 1)
    g2 = mod_ref[5:6, :]
    step = pl.program_id(0)
    nsteps = pl.num_programs(0)

    def row_copy(ids_ref, t, j, slot):
        return pltpu.make_async_copy(
            tab_ref.at[ids_ref[t, j]],
            bufs[slot].at[pl.ds(j * pitch, nrow), :],
            sem.at[slot])

    def issue(ids_ref, t, slot):
        for j in range(nsel):
            row_copy(ids_ref, t, j, slot).start(priority=j % 2)

    def wait(t, slot):
        for j in range(nsel):
            row_copy(idx_ref, t, j, slot).wait()

    def packed(slot, s):
        return bufs[slot][pl.ds(s, nsel, stride=pitch), :]

    def compute(t, slot):
        hrow = h_ref[pl.ds(t, 1), :]
        acc = jnp.zeros((nsel, LANES), F32)
        for s in range(nrow):
            u = lax.bitcast_convert_type(packed(slot, s) << 16, F32)
            acc = acc + u * hrow[:, LANES * s:LANES * (s + 1)]
        dots = jnp.sum(acc, axis=-1, keepdims=True)
        gcol = jnp.sum(jnp.where(lane_t == t, gate_ref[...], 0.0), axis=-1, keepdims=True)
        coef = gcol * _gelu(dots)
        outs = []
        for s in range(nrow):
            vv = lax.bitcast_convert_type(packed(slot, s) & jnp.uint32(0xFFFF0000), F32)
            outs.append(jnp.sum(coef * vv, axis=0, keepdims=True))
        orow = jnp.concatenate(outs, axis=1)
        o_ref[pl.ds(t, 1), :] = x_ref[pl.ds(t, 1), :] + g2 * orow

    @pl.when(step == 0)
    def _():
        for s in range(ns - 1):
            issue(idx_ref, s, s)

    ngroups = tb // ns

    def body(g, carry):
        for s in range(ns):
            t = g * ns + s
            wait(t, s)
            issue(idx_ref, t + ns - 1, (s + ns - 1) % ns)
            compute(t, s)
        return carry

    lax.fori_loop(0, ngroups - 1, body, 0)
    for s in range(ns):
        t = (ngroups - 1) * ns + s
        wait(t, s)
        if s == 0:
            issue(idx_ref, tb - 1, ns - 1)
        else:
            @pl.when(step < nsteps - 1)
            def _():
                issue(idxn_ref, s - 1, s - 1)
        compute(t, s)


def _peer_gather(geo_g, eidx, gate_t, h2, x, mods, table, n):
    d = x.shape[1]
    tb = geo_g.tm
    nsel = PEER_HEADS * PEER_TOPK
    pitch = d // LANES + SLAB_PAD
    assert n % tb == 0
    nsteps = n // tb
    return pl.pallas_call(
        functools.partial(_peergather_kernel, tb=tb),
        grid=(nsteps,),
        in_specs=[
            pl.BlockSpec((tb, nsel), lambda i: (i, 0), memory_space=pltpu.SMEM),
            pl.BlockSpec((tb, nsel), lambda i: (jnp.minimum(i + 1, nsteps - 1), 0),
                         memory_space=pltpu.SMEM),
            pl.BlockSpec((nsel, tb), lambda i: (0, i)),
            pl.BlockSpec((tb, d), lambda i: (i, 0)),
            pl.BlockSpec((tb, d), lambda i: (i, 0)),
            geo_g.mod_spec(),
            pl.BlockSpec(memory_space=pl.ANY),
        ],
        out_specs=pl.BlockSpec((tb, d), lambda i: (i, 0)),
        out_shape=jax.ShapeDtypeStruct((n, d), F32),
        scratch_shapes=[pltpu.VMEM((nsel * pitch, LANES), jnp.uint32)] * GATHER_SLOTS
        + [pltpu.SemaphoreType.DMA((GATHER_SLOTS,))],
        compiler_params=_cparams("arbitrary"),
        name="peer_gather",
    )(eidx, eidx, gate_t, h2, x, mods, table)


SC_CORES = 2
SC_SUBCORES = 16
SC_LANES = 16
SC_CHUNK = 16
SC_SHARE = 0.3235

_ERF_ALPHA = (0.00022905065861350646, 0.0034082910107109506, 0.050955695062380861,
              0.18520832239976145, 1.128379143519084)
_ERF_BETA = (-1.1791602954361697e-7, 0.000023547966471313185, 0.0010179625278914885,
             0.014070470171167667, 0.11098505178285362, 0.49746925110067538, 1.0)
_ERF_CLAMP = 3.832506856900711


def _erf_rational(x):
    x = jnp.minimum(jnp.maximum(x, -_ERF_CLAMP), _ERF_CLAMP)
    x2 = x * x
    p = jnp.full_like(x, _ERF_ALPHA[0])
    for c in _ERF_ALPHA[1:]:
        p = p * x2 + c
    q = jnp.full_like(x, _ERF_BETA[0])
    for c in _ERF_BETA[1:]:
        q = q * x2 + c
    return x * p / q


def _peer_sc(eidx, gate, h, table, row0):
    n, nsel = eidx.shape
    d = h.shape[1]
    nw = SC_CORES * SC_SUBCORES
    assert n % (2 * nw) == 0 and nsel % SC_CHUNK == 0 and d % SC_LANES == 0
    tpw = n // nw
    nchunk = nsel // SC_CHUNK
    nvec = d // SC_LANES
    mesh = plsc.VectorSubcoreMesh(core_axis_name="c", subcore_axis_name="s",
                                  num_cores=SC_CORES, num_subcores=SC_SUBCORES)

    @functools.partial(
        pl.kernel, mesh=mesh,
        out_type=jax.ShapeDtypeStruct((n, d), F32),
        scratch_types=[
            pltpu.VMEM((nsel,), jnp.int32), pltpu.VMEM((nsel,), jnp.int32),
            pltpu.VMEM((nsel,), F32), pltpu.VMEM((nsel,), F32),
            pltpu.VMEM((d,), F32), pltpu.VMEM((d,), F32),
            pltpu.VMEM((d,), F32),
            pltpu.VMEM((SC_CHUNK, d), jnp.uint32),
            pltpu.VMEM((SC_CHUNK, d), jnp.uint32),
            pltpu.SemaphoreType.DMA, pltpu.SemaphoreType.DMA,
            pltpu.SemaphoreType.DMA, pltpu.SemaphoreType.DMA,
        ],
        compiler_params=pltpu.CompilerParams(needs_layout_passes=False),
        name="peer_sc",
    )
    def sc_kernel(eidx_hbm, gate_hbm, h_hbm, tab_hbm, y_hbm, idx_a, idx_b, gate_a, gate_b, h_a, h_b,
                  o_v, rows0, rows1, sem0, sem1, msem_a, msem_b):
        wid = lax.axis_index("s") * SC_CORES + lax.axis_index("c")
        lanes = lax.iota(jnp.int32, SC_LANES)
        zero = jnp.zeros((SC_LANES,), F32)
        bufs = ((rows0, sem0), (rows1, sem1))
        meta = ((idx_a, gate_a, h_a, msem_a), (idx_b, gate_b, h_b, msem_b))

        def meta_copies(tok, s):
            idx_v, gate_v, h_v, msem = meta[s]
            return (pltpu.make_async_copy(eidx_hbm.at[tok], idx_v, msem),
                    pltpu.make_async_copy(gate_hbm.at[tok], gate_v, msem),
                    pltpu.make_async_copy(h_hbm.at[row0 + tok], h_v, msem))

        def gather(s, c, slot):
            rows, sem = bufs[slot]
            return pltpu.make_async_copy(
                tab_hbm.at[meta[s][0].at[pl.ds(c * SC_CHUNK, SC_CHUNK)]], rows, sem)

        def chunk_compute(s, c, rows):
            _, gate_v, h_v, _ = meta[s]

            def dot_body(i, accs):
                hv = h_v[pl.ds(i * SC_LANES, SC_LANES)]
                out = []
                for e in range(SC_CHUNK):
                    w = rows[e, pl.ds(i * SC_LANES, SC_LANES)]
                    out.append(accs[e] + lax.bitcast_convert_type(w << 16, F32) * hv)
                return tuple(out)

            accs = lax.fori_loop(0, nvec, dot_body, (zero,) * SC_CHUNK)
            dots = zero
            for e in range(SC_CHUNK):
                dots = jnp.where(lanes == e, jnp.sum(accs[e]), dots)
            act = 0.5 * dots * (1.0 + _erf_rational(dots * (2.0 ** -0.5)))
            coef = gate_v[pl.ds(c * SC_CHUNK, SC_CHUNK)] * act
            splat = [jnp.full((SC_LANES,), jnp.sum(jnp.where(lanes == e, coef, 0.0)), F32)
                     for e in range(SC_CHUNK)]

            def acc_body(i, carry):
                o = o_v[pl.ds(i * SC_LANES, SC_LANES)]
                for e in range(SC_CHUNK):
                    w = rows[e, pl.ds(i * SC_LANES, SC_LANES)]
                    o = o + splat[e] * lax.bitcast_convert_type(w & jnp.uint32(0xFFFF0000), F32)
                o_v[pl.ds(i * SC_LANES, SC_LANES)] = o
                return carry

            lax.fori_loop(0, nvec, acc_body, 0)

        def zero_body(i, c2):
            o_v[pl.ds(i * SC_LANES, SC_LANES)] = zero
            return c2

        def token(tok, s, has_next):
            def when_next(fn):
                if isinstance(has_next, bool):
                    if has_next:
                        fn()
                else:
                    pl.when(has_next)(fn)

            def load_next():
                for cp in meta_copies(tok + 1, 1 - s):
                    cp.start()

            when_next(load_next)
            lax.fori_loop(0, nvec, zero_body, 0)
            for c in range(nchunk):
                if c + 1 < nchunk:
                    gather(s, c + 1, (c + 1) % 2).start()
                gather(s, c, c % 2).wait()
                chunk_compute(s, c, bufs[c % 2][0])

            def prefetch_next():
                for cp in meta_copies(tok + 1, 1 - s):
                    cp.wait()
                gather(1 - s, 0, 0).start()

            when_next(prefetch_next)
            pltpu.sync_copy(o_v, y_hbm.at[tok])

        base = wid * tpw
        for cp in meta_copies(base, 0):
            cp.start()
        for cp in meta_copies(base, 0):
            cp.wait()
        gather(0, 0, 0).start()

        def pair_body(g, carry):
            token(base + 2 * g, 0, True)
            token(base + 2 * g + 1, 1, g + 1 < tpw // 2)
            return carry

        lax.fori_loop(0, tpw // 2, pair_body, 0)

    return sc_kernel(eidx, gate, h, table)


def _pad_to(x, axis, size):
    pad = [(0, 0)] * x.ndim
    pad[axis] = (0, size - x.shape[axis])
    return jnp.pad(x, pad)


def _rwkv_layer(geo, xs, mods, norm1, mix, w_rkv, w_o, w0, w1, w2, a0, a1, a2, vl, g1, g2,
                k_k, k_a, r_k, ln_w, ln_b, vfirst):
    d = geo.d
    xm = _rwkv_mix(geo, xs, norm1, mods, mix)
    rkv = _bmm(xm, w_rkv.astype(BF16), (0, 2, 3), ("none",) * 3, F32, geo.tm)
    lt = g1.shape[1]
    w1c = jnp.concatenate([_pad_to(w1[0], 1, LORA_PAD), _pad_to(w1[1], 1, LORA_PAD)], axis=1)
    a1c = jnp.concatenate([_pad_to(a1[0], 1, LORA_PAD), _pad_to(a1[1], 1, LORA_PAD)], axis=1)
    if vl is None:
        v1p = jnp.zeros((d, lt), F32)
        v2p = jnp.zeros((LORA_PAD, d), F32)
        v0 = jnp.zeros((d,), F32)
    else:
        v0, v1, v2 = vl
        v1p = _pad_to(v1, 1, lt)
        v2p = _pad_to(v2, 0, LORA_PAD)
    wl1 = jnp.stack([_pad_to(w1c, 1, lt), _pad_to(a1c, 1, lt), g1, v1p]).astype(BF16)
    tl = _bmm(xm, wl1, (1, 4, 5, 3), ("tanh", "none", "sigmoid", "none"), BF16, geo.tm)
    w2p = jnp.stack([_pad_to(w2[0], 0, LORA_PAD), _pad_to(w2[1], 0, LORA_PAD)]).astype(BF16)
    a2p = jnp.stack([_pad_to(a2[0], 0, LORA_PAD), _pad_to(a2[1], 0, LORA_PAD)]).astype(BF16)
    pvec = jnp.stack([w0[0], w0[1], a0[0], a0[1], v0, k_k, k_a, jnp.zeros_like(k_k)])
    lw, kd, asg, kk, g, v = _rwkv_feat(geo, rkv, tl, w2p, a2p, g2.astype(BF16), v2p.astype(BF16),
                                      pvec, vfirst)
    r = rkv[0]
    wkv = _wkv_bidir(r, v, kk, lw, kd, asg, geo.b, geo.l)
    pv2 = _pad_to(jnp.stack([r_k, ln_w, ln_b]), 0, SUBLANES)
    y = _rwkv_readout(geo, wkv, r, kd, v, g, pv2)
    xs = _matmul_res(geo, y, w_o.astype(BF16), xs, mods, 2)
    return xs, v


def _rope_tables(geo):
    t = geo.t
    pos = jnp.arange(t)
    row = (pos // GRID_W).astype(F32)
    col = (pos % GRID_W).astype(F32)
    n_freq = QK_ROPE // 4
    inv_freq = ROPE_THETA ** (-jnp.arange(n_freq, dtype=F32) / n_freq)
    ang = jnp.concatenate([row[:, None] * inv_freq, col[:, None] * inv_freq], axis=-1)
    cos, sin = jnp.cos(ang), jnp.sin(ang)
    pad = LANES - QK_ROPE
    cos_l = jnp.concatenate([cos, cos, jnp.ones((t, pad), F32)], axis=1)
    sin_l = jnp.concatenate([-sin, sin, jnp.zeros((t, pad), F32)], axis=1)
    cos_c = jnp.ones((geo.l, LANES), F32)
    sin_c = jnp.zeros((geo.l, LANES), F32)
    return jnp.concatenate([cos_c, cos_l], axis=0), jnp.concatenate([sin_c, sin_l], axis=0)


def _mla_layer(geo, xs, mods, norm1, rope_t, w_in, q_norm, kv_norm, w_uq, w_ukv, g_q, g_k, w_o):
    q_lora, kv_lora = q_norm.shape[0], kv_norm.shape[0]
    zw = q_lora + kv_lora + LANES
    z = _mod_matmul(geo, xs, norm1, mods, _pad_to(w_in, 1, zw).astype(BF16), 0, False)
    qk = QK_NOPE + QK_ROPE
    wuq_p = _pad_to(w_uq.reshape(q_lora, MLA_HEADS, qk), 2, QK_PAD).reshape(q_lora, -1)
    gq_p = _pad_to(g_q, 0, QK_PAD).reshape(1, QK_PAD)
    gk_p = _pad_to(g_k, 0, QK_PAD).reshape(1, QK_PAD)
    q, k, v = _mla_qkv(geo, z, q_norm, kv_norm, wuq_p.astype(BF16), w_ukv.astype(BF16),
                       gq_p, gk_p, *rope_t)
    o = _mla_attention(geo, q, k, v)
    return _matmul_res(geo, o, w_o.astype(BF16), xs, mods, 2)


def _peer_layer(geo, geo_g, xs, mods, norm2, w_q, q_norm, keys, u, v):
    qp, h2 = _mod_matmul(geo, xs, norm2, mods, w_q.astype(BF16), 1, True)
    table = _pack_expert_table(u, v)
    n, d = xs.shape
    n_sc = geo.tm * round(SC_SHARE * n / geo.tm)
    if n_sc % (2 * SC_CORES * SC_SUBCORES) != 0:
        n_sc = 0
    n_tc = n - n_sc

    def select(row0, nrows):
        wide = 2 * geo.tm
        tsel = wide if row0 % wide == 0 and nrows % wide == 0 else geo.tm
        return _peer_select(qp, q_norm, keys, tsel, row0, nrows)

    if n_sc:
        eidx_sc, gate_sc = select(n_tc, n_sc)
        y_sc = _peer_sc(eidx_sc.T, gate_sc.T, h2, table.reshape(table.shape[0], d), n_tc)
    eidx_t, gate_t = select(0, n_tc)
    out_tc = _peer_gather(geo_g, eidx_t.T, gate_t, h2, xs, mods, table, n_tc)
    if n_sc == 0:
        return out_tc
    g2, r = [], n_tc
    while r < n:
        bi, pos = divmod(r, geo.s)
        seg = int(pos >= geo.l)
        stop = min(n, bi * geo.s + (geo.s if seg else geo.l))
        g2.append(jnp.broadcast_to(mods[bi, seg, 5], (stop - r, d)))
        r = stop
    out_sc = xs[n_tc:] + jnp.concatenate(g2, axis=0) * y_sc
    return jnp.concatenate([out_tc, out_sc], axis=0)


def kernel(x, c, ctx, c_ctx, w_ada, b_ada, norm1, norm2, rw_mix, rw_wrkv, rw_wo, rw_w0, rw_w1, rw_w2, rw_a0, rw_a1, rw_a2, rw_v0, rw_v1, rw_v2, rw_g1, rw_g2, rw_kk, rw_ka, rw_rk, rw_lnw, rw_lnb, mla_win, mla_qnorm, mla_kvnorm, mla_wuq, mla_wukv, mla_gq, mla_gk, mla_wo, peer_wq, peer_qnorm, peer_keys, peer_u, peer_v):
    b, t, d = x.shape
    l = ctx.shape[1]
    depth = w_ada.shape[0]
    geo = _Geom(b, l, t, d, min(ROW_TILE, l))
    geo_g = _Geom(b, l, t, d, min(GATHER_TILE, l))
    cond8 = _pad_to(jnp.concatenate([c, c_ctx[None, :]], axis=0), 0, SUBLANES)
    ada = _adaln(cond8, w_ada, b_ada).reshape(depth, SUBLANES, 6, d)
    mods_all = jnp.stack([jnp.broadcast_to(ada[:, b:b + 1], (depth, b, 6, d)), ada[:, 0:b]], axis=2)
    xs = jnp.concatenate([ctx, x], axis=1).reshape(b * (l + t), d)
    rope_t = _rope_tables(geo)
    vfirst = None
    for i in range(depth):
        j = i // 2
        mods = mods_all[i]
        if i % 2 == 0:
            vl = None if j == 0 else (rw_v0[j - 1], rw_v1[j - 1], rw_v2[j - 1])
            xs, vcur = _rwkv_layer(geo, xs, mods, norm1[i], rw_mix[j], rw_wrkv[j], rw_wo[j],
                                   rw_w0[j], rw_w1[j], rw_w2[j], rw_a0[j], rw_a1[j], rw_a2[j], vl,
                                   rw_g1[j], rw_g2[j], rw_kk[j], rw_ka[j], rw_rk[j], rw_lnw[j],
                                   rw_lnb[j], vfirst)
            if j == 0:
                vfirst = vcur
        else:
            xs = _mla_layer(geo, xs, mods, norm1[i], rope_t, mla_win[j], mla_qnorm[j],
                            mla_kvnorm[j], mla_wuq[j], mla_wukv[j], mla_gq[j], mla_gk[j], mla_wo[j])
        if i == depth - 1:
            xs = xs.reshape(b, l + t, d)[:, l:, :].reshape(b * t, d)
            geo, geo_g = _Geom(b, 0, t, d, geo.tm), _Geom(b, 0, t, d, geo_g.tm)
        xs = _peer_layer(geo, geo_g, xs, mods, norm2[i], peer_wq[i], peer_qnorm[i], peer_keys[i],
                         peer_u[i], peer_v[i])
    return xs.reshape(b, t, d)
```

```python
import functools
import math

import jax
import jax.numpy as jnp
from jax import lax
from jax.experimental import pallas as pl
from jax.experimental.pallas import tpu as pltpu
from jax.experimental.pallas import tpu_sc as plsc

F32 = jnp.float32
BF16 = jnp.bfloat16

EPS = 1e-6
GN_EPS = 64e-5
RW_HEAD = 64
WKV_CHUNK = 64
MLA_HEADS = 16
QK_NOPE = 128
QK_ROPE = 64
V_HEAD = 128
QK_PAD = 256
ROPE_THETA = 10000.0
GRID_W = 64
ATTN_SCALE = (QK_NOPE + QK_ROPE) ** -0.5
PEER_HEADS = 8
N_KEYS = 128
PEER_TOPK = 16
D_KEY = 256
LORA_PAD = 128

LANES = 128
SUBLANES = 8
VMEM_LIMIT = 56 * 1024 * 1024
ROW_TILE = 256
GATHER_TILE = 128
COL_TILE = 512
ADA_COLS = 1024


def _cparams(*sem):
    return pltpu.CompilerParams(dimension_semantics=sem, vmem_limit_bytes=VMEM_LIMIT)


def _dot(a, b):
    return jnp.dot(a, b, preferred_element_type=F32)


def _dot_nt(a, b):
    return lax.dot_general(a, b, (((1,), (1,)), ((), ())), preferred_element_type=F32)


def _split2(x):
    hi = x.astype(BF16)
    lo = (x - hi.astype(F32)).astype(BF16)
    return hi, lo


def _split3(x):
    hi = x.astype(BF16)
    r1 = x - hi.astype(F32)
    mid = r1.astype(BF16)
    lo = (r1 - mid.astype(F32)).astype(BF16)
    return hi, mid, lo


def _dot3(a, b):
    ah, al = _split2(a)
    bh, bl = _split2(b)
    return _dot(ah, bh) + (_dot(ah, bl) + _dot(al, bh))


def _dot3_nt(a, b):
    ah, al = _split2(a)
    bh, bl = _split2(b)
    return _dot_nt(ah, bh) + (_dot_nt(ah, bl) + _dot_nt(al, bh))


def _dot_exact_lhs(sel, x):
    hi, mid, lo = _split3(x)
    return _dot(sel, hi) + (_dot(sel, mid) + _dot(sel, lo))


def _modulate(x, g, shift, scale):
    ms = jnp.mean(x * x, axis=-1, keepdims=True)
    return (x * lax.rsqrt(ms + EPS) * g) * (1.0 + scale) + shift


def _sigmoid(x):
    return 1.0 / (1.0 + jnp.exp(-x))


def _softplus(y):
    return jnp.maximum(y, 0.0) + jnp.log(1.0 + jnp.exp(-jnp.abs(y)))


def _erf(x):
    return lax.erf(x)


def _gelu(x):
    return 0.5 * x * (1.0 + _erf(x * (2.0 ** -0.5)))


def _ada_kernel(s_ref, w_ref, b_ref, o_ref):
    s = s_ref[...]
    s = s * _sigmoid(s)
    o_ref[...] = _dot3(s, w_ref[...]) + b_ref[...]


def _adaln(cond8, w_ada, b_ada):
    depth, d, n = w_ada.shape
    tn = min(ADA_COLS, n)
    return pl.pallas_call(
        _ada_kernel,
        grid=(depth, n // tn),
        in_specs=[
            pl.BlockSpec((SUBLANES, d), lambda l, j: (0, 0)),
            pl.BlockSpec((None, d, tn), lambda l, j: (l, 0, j)),
            pl.BlockSpec((None, 1, tn), lambda l, j: (l, 0, j)),
        ],
        out_specs=pl.BlockSpec((None, SUBLANES, tn), lambda l, j: (l, 0, j)),
        out_shape=jax.ShapeDtypeStruct((depth, SUBLANES, n), F32),
        compiler_params=_cparams("parallel", "parallel"),
        name="adaln",
    )(cond8, w_ada, b_ada.reshape(depth, 1, n))


class _Geom:
    def __init__(self, batch, ctx_len, seq_len, d_model, tm):
        self.b, self.l, self.t, self.d = batch, ctx_len, seq_len, d_model
        self.s = ctx_len + seq_len
        self.n = batch * self.s
        self.tm = tm
        assert ctx_len % tm == 0 and seq_len % tm == 0
        self.tpb = self.s // tm
        self.nct = ctx_len // tm
        self.ntiles = self.n // tm

    def mod_spec(self):
        tpb, nct = self.tpb, self.nct

        def imap(i):
            return (i // tpb, ((i % tpb) >= nct).astype(jnp.int32), 0, 0)

        return pl.BlockSpec((None, None, 6, self.d), imap)


def _modmm_kernel(x_ref, g_ref, mod_ref, w_ref, o_ref, *h_ref, which):
    h = _modulate(x_ref[...], g_ref[...], mod_ref[3 * which:3 * which + 1, :],
                  mod_ref[3 * which + 1:3 * which + 2, :])
    if h_ref:
        h_ref[0][...] = h
    o_ref[...] = _dot(h.astype(BF16), w_ref[...])


def _mod_matmul(geo, x, g, mods, w, which, emit_h):
    n, d = x.shape
    nn = w.shape[1]
    tm = geo.tm
    out_shape = [jax.ShapeDtypeStruct((n, nn), F32)]
    out_specs = [pl.BlockSpec((tm, nn), lambda i: (i, 0))]
    if emit_h:
        out_shape.append(jax.ShapeDtypeStruct((n, d), F32))
        out_specs.append(pl.BlockSpec((tm, d), lambda i: (i, 0)))
    res = pl.pallas_call(
        functools.partial(_modmm_kernel, which=which),
        grid=(geo.ntiles,),
        in_specs=[
            pl.BlockSpec((tm, d), lambda i: (i, 0)),
            pl.BlockSpec((1, d), lambda i: (0, 0)),
            geo.mod_spec(),
            pl.BlockSpec((d, nn), lambda i: (0, 0)),
        ],
        out_specs=out_specs,
        out_shape=out_shape,
        compiler_params=_cparams("parallel"),
        name="mod_matmul",
    )(x, g.reshape(1, d), mods, w)
    return res if emit_h else res[0]


def _mmres_kernel(y_ref, w_ref, x_ref, mod_ref, o_ref, *, gidx):
    acc = _dot(y_ref[...], w_ref[...])
    o_ref[...] = x_ref[...] + mod_ref[gidx:gidx + 1, :] * acc


def _matmul_res(geo, y, w, x, mods, gidx):
    n, k = y.shape
    d = x.shape[1]
    tm = geo.tm
    return pl.pallas_call(
        functools.partial(_mmres_kernel, gidx=gidx),
        grid=(geo.ntiles,),
        in_specs=[
            pl.BlockSpec((tm, k), lambda i: (i, 0)),
            pl.BlockSpec((k, d), lambda i: (0, 0)),
            pl.BlockSpec((tm, d), lambda i: (i, 0)),
            geo.mod_spec(),
        ],
        out_specs=pl.BlockSpec((tm, d), lambda i: (i, 0)),
        out_shape=jax.ShapeDtypeStruct((n, d), F32),
        compiler_params=_cparams("parallel"),
        name="matmul_res",
    )(y, w, x, mods)


def _bmm_kernel(x_ref, w_ref, o_ref, *, acts):
    j = pl.program_id(0)
    y = _dot(x_ref[...], w_ref[...])
    out = y
    for jj, a in enumerate(acts):
        if a == "tanh":
            out = jnp.where(j == jj, jnp.tanh(y), out)
        elif a == "sigmoid":
            out = jnp.where(j == jj, _sigmoid(y), out)
    o_ref[...] = out.astype(o_ref.dtype)


def _bmm(x3, w3, src, acts, out_dtype, tm):
    _, n, k = x3.shape
    nj, _, nn = w3.shape
    src = tuple(src)

    def xmap(j, i):
        idx = jnp.int32(src[0])
        for jj in range(1, nj):
            idx = jnp.where(j == jj, jnp.int32(src[jj]), idx)
        return (idx, i, 0)

    return pl.pallas_call(
        functools.partial(_bmm_kernel, acts=tuple(acts)),
        grid=(nj, n // tm),
        in_specs=[
            pl.BlockSpec((None, tm, k), xmap),
            pl.BlockSpec((None, k, nn), lambda j, i: (j, 0, 0)),
        ],
        out_specs=pl.BlockSpec((None, tm, nn), lambda j, i: (j, i, 0)),
        out_shape=jax.ShapeDtypeStruct((nj, n, nn), out_dtype),
        compiler_params=_cparams("parallel", "parallel"),
        name="bmm",
    )(x3, w3)


def _rwmix_kernel(x_ref, xp_ref, xn_ref, g_ref, mod_ref, mix_ref, o_ref, *, tpb, nct):
    i = pl.program_id(0)
    tm = x_ref.shape[0]
    g = g_ref[...]
    shift = mod_ref[0:1, :]
    scale = mod_ref[1:2, :]
    h = _modulate(x_ref[...], g, shift, scale)
    hp = _modulate(xp_ref[...], g, shift, scale)[SUBLANES - 1:SUBLANES, :]
    hn = _modulate(xn_ref[...], g, shift, scale)[0:1, :]
    it = i % tpb
    first = jnp.logical_or(it == 0, it == nct)
    last = jnp.logical_or(it == nct - 1, it == tpb - 1)
    hp = jnp.where(first, 0.0, hp)
    hn = jnp.where(last, 0.0, hn)
    rows = lax.broadcasted_iota(jnp.int32, h.shape, 0)
    prev = jnp.where(rows == 0, hp, pltpu.roll(h, 1, axis=0))
    nxt = jnp.where(rows == tm - 1, hn, pltpu.roll(h, tm - 1, axis=0))
    xx = 0.5 * (prev + nxt) - h
    for m in range(6):
        o_ref[m] = (h + xx * mix_ref[m:m + 1, :]).astype(BF16)


def _rwkv_mix(geo, x, g, mods, mix):
    n, d = x.shape
    tm = geo.tm
    r8 = tm // SUBLANES
    nblk8 = n // SUBLANES
    return pl.pallas_call(
        functools.partial(_rwmix_kernel, tpb=geo.tpb, nct=geo.nct),
        grid=(geo.ntiles,),
        in_specs=[
            pl.BlockSpec((tm, d), lambda i: (i, 0)),
            pl.BlockSpec((SUBLANES, d), lambda i: (jnp.maximum(i * r8 - 1, 0), 0)),
            pl.BlockSpec((SUBLANES, d), lambda i: (jnp.minimum((i + 1) * r8, nblk8 - 1), 0)),
            pl.BlockSpec((1, d), lambda i: (0, 0)),
            geo.mod_spec(),
            pl.BlockSpec((6, d), lambda i: (0, 0)),
        ],
        out_specs=pl.BlockSpec((6, tm, d), lambda i: (0, i, 0)),
        out_shape=jax.ShapeDtypeStruct((6, n, d), BF16),
        compiler_params=_cparams("parallel"),
        name="rwkv_mix",
    )(x, x, x, g.reshape(1, d), mods, mix)


def _head_sum(x, bd):
    return _dot(x.astype(BF16), bd)


def _rwfeat_kernel(k_ref, v_ref, tl_ref, w2_ref, a2_ref, g2_ref, v2_ref, pv_ref, bd_ref,
                   *rest, has_vlora):
    if has_vlora:
        vf_ref, lw_ref, kd_ref, as_ref, kk_ref, g_ref, vo_ref = rest
    else:
        lw_ref, kd_ref, as_ref, kk_ref, g_ref = rest
    k = k_ref[...]
    tw = tl_ref[0]
    ta = tl_ref[1]
    tg = tl_ref[2]
    w0 = pv_ref[0:2, :]
    a0 = pv_ref[2:4, :]
    k_k = pv_ref[5:6, :]
    k_a = pv_ref[6:7, :]
    for z in range(2):
        sl = slice(LORA_PAD * z, LORA_PAD * (z + 1))
        lora_w = _dot(tw[:, sl], w2_ref[z])
        w = -_softplus(-(w0[z:z + 1, :] + lora_w)) - 0.5
        lw_ref[z] = -jnp.exp(w)
        a_sig = _sigmoid(a0[z:z + 1, :] + _dot(ta[:, sl], a2_ref[z]))
        as_ref[z] = a_sig.astype(BF16)
        kd_ref[z] = k * (1.0 + (a_sig - 1.0) * k_a)
    g_ref[...] = _dot(tg, g2_ref[...]).astype(BF16)
    kkr = k * k_k
    ss = _head_sum(kkr * kkr, bd_ref[...])
    kk_ref[...] = (kkr * lax.rsqrt(ss + 1e-12)).astype(BF16)
    if has_vlora:
        v = v_ref[...]
        tv = tl_ref[3]
        gate = _sigmoid(pv_ref[4:5, :] + _dot(tv[:, 0:LORA_PAD], v2_ref[...]))
        vo_ref[...] = v + (vf_ref[...] - v) * gate


def _rwkv_feat(geo, rkv, tl, w2p, a2p, g2, v2p, pvec, vfirst):
    _, n, d = rkv.shape
    tm, tc = geo.tm, COL_TILE
    has_vlora = vfirst is not None
    lt = tl.shape[2]
    ii = lax.broadcasted_iota(jnp.int32, (tc, tc), 0) // RW_HEAD
    jj = lax.broadcasted_iota(jnp.int32, (tc, tc), 1) // RW_HEAD
    bd = (ii == jj).astype(BF16)
    row = lambda i, j: (i, j)
    in_specs = [
        pl.BlockSpec((None, tm, tc), lambda i, j: (1, i, j)),
        pl.BlockSpec((None, tm, tc), lambda i, j: (2, i, j)),
        pl.BlockSpec((4, tm, lt), lambda i, j: (0, i, 0)),
        pl.BlockSpec((2, LORA_PAD, tc), lambda i, j: (0, 0, j)),
        pl.BlockSpec((2, LORA_PAD, tc), lambda i, j: (0, 0, j)),
        pl.BlockSpec((lt, tc), lambda i, j: (0, j)),
        pl.BlockSpec((LORA_PAD, tc), lambda i, j: (0, j)),
        pl.BlockSpec((SUBLANES, tc), lambda i, j: (0, j)),
        pl.BlockSpec((tc, tc), lambda i, j: (0, 0)),
    ]
    args = [rkv, rkv, tl, w2p, a2p, g2, v2p, pvec, bd]
    dir_spec = pl.BlockSpec((2, tm, tc), lambda i, j: (0, i, j))
    out_specs = [dir_spec, dir_spec, dir_spec, pl.BlockSpec((tm, tc), row), pl.BlockSpec((tm, tc), row)]
    out_shape = [jax.ShapeDtypeStruct((2, n, d), dt) for dt in (F32, F32, BF16)]
    out_shape += [jax.ShapeDtypeStruct((n, d), BF16)] * 2
    if has_vlora:
        in_specs.append(pl.BlockSpec((tm, tc), row))
        args.append(vfirst)
        out_specs.append(pl.BlockSpec((tm, tc), row))
        out_shape.append(jax.ShapeDtypeStruct((n, d), F32))
    res = pl.pallas_call(
        functools.partial(_rwfeat_kernel, has_vlora=has_vlora),
        grid=(geo.ntiles, d // tc),
        in_specs=in_specs,
        out_specs=out_specs,
        out_shape=out_shape,
        compiler_params=_cparams("parallel", "parallel"),
        name="rwkv_feat",
    )(*args)
    if has_vlora:
        lw, kd, asg, kk, g, v = res
    else:
        lw, kd, asg, kk, g = res
        v = rkv[2]
    return lw, kd, asg, kk, g, v


def _wkv_chunk_kernel(r_ref, v_ref, kk_ref, lw_ref, kd_ref, as_ref,
                      m_ref, ga_ref, rq_ref, o0_ref, pc_ref, *, npairs):
    c = WKV_CHUNK
    c2 = 2 * c
    sgn = 1 - 2 * pl.program_id(0)
    ri = lax.broadcasted_iota(jnp.int32, (c2, c2), 0)
    ci = lax.broadcasted_iota(jnp.int32, (c2, c2), 1)
    same = (ri >= c) == (ci >= c)
    tt = jnp.where(ri >= c, ri - c, ri)
    ss = jnp.where(ci >= c, ci - c, ci)
    earlier = (ss - tt) * sgn < 0
    strict = jnp.logical_and(same, earlier)
    incl = jnp.logical_and(same, jnp.logical_or(earlier, ss == tt))
    eye = (ri == ci).astype(F32)
    r64 = lax.broadcasted_iota(jnp.int32, (c, c), 0)
    c64 = lax.broadcasted_iota(jnp.int32, (c, c), 1)
    ltri = jnp.where((c64 - r64) * sgn <= 0, 1.0, 0.0).astype(BF16)
    head0 = lax.broadcasted_iota(jnp.int32, (c, LANES), 1) < RW_HEAD
    pairs = range(npairs)

    def stack(x):
        return jnp.concatenate([jnp.where(head0, x, 0.0), jnp.where(head0, 0.0, x)], axis=0)

    def dup(x):
        return jnp.concatenate([x, x], axis=0)

    def bf(x):
        return x.astype(BF16)

    lhs, rhs, a2, bp2, kp2, vst, r2 = [], [], [], [], [], [], []
    lw_all = lw_ref[...]
    cum_all = _dot_exact_lhs(ltri, lw_all)
    for p in pairs:
        sl = slice(LANES * p, LANES * (p + 1))
        lw = lw_all[:, sl]
        cum = cum_all[:, sl]
        tot = jnp.sum(lw, axis=0, keepdims=True)
        p_inv = jnp.exp(-cum)
        p_end = jnp.exp(tot - cum)
        kk = kk_ref[:, sl]
        b = kk * as_ref[:, sl]
        kd = kd_ref[:, sl]
        a2p = stack(-kk * jnp.exp(cum - lw))
        r2p = stack(r_ref[:, sl] * jnp.exp(cum))
        a2.append(bf(a2p))
        r2.append(r2p)
        lhs.append(jnp.concatenate([a2[p], bf(r2p)], axis=0))
        rhs.append(jnp.concatenate([dup(bf(b * p_inv)), dup(bf(kd * p_inv))], axis=0))
        bp2.append(bf(stack(b * p_end)))
        kp2.append(bf(stack(kd * p_end)))
        vst.append(bf(stack(v_ref[:, sl])))
        pc_ref[:, sl] = jnp.broadcast_to(jnp.exp(tot), (SUBLANES, LANES))
    gram = [_dot_nt(lhs[p], rhs[p]) for p in pairs]
    nab = [jnp.where(strict, gram[p][0:c2, 0:c2], 0.0) for p in pairs]
    nrb = [bf(jnp.where(incl, gram[p][c2:2 * c2, 0:c2], 0.0)) for p in pairs]
    nk = [bf(jnp.concatenate([jnp.where(strict, gram[p][0:c2, c2:2 * c2], 0.0),
                              jnp.where(incl, gram[p][c2:2 * c2, c2:2 * c2], 0.0)], axis=0))
          for p in pairs]
    quads = range(npairs // 2)

    def side(x0, x1):
        return jnp.concatenate([x0, x1], axis=1)

    def diag(x0, x1):
        z0 = jnp.zeros_like(x0)
        return jnp.concatenate([side(x0, z0), side(z0, x1)], axis=0)

    def diag_halves(x):
        return diag(x[:, 0:c2], x[:, c2:2 * c2])

    def unside(xs):
        return [xs[p // 2][:, c2 * (p % 2):c2 * (p % 2 + 1)] for p in pairs]

    xv = unside([_dot(side(nk[2 * q], nk[2 * q + 1]), diag(vst[2 * q], vst[2 * q + 1]))
                 for q in quads])
    tinv = [side(eye + nab[2 * q], eye + nab[2 * q + 1]) for q in quads]
    npow = [bf(side(nab[2 * q], nab[2 * q + 1])) for q in quads]
    for _ in range(int(math.log2(c)) - 1):
        npow = [bf(_dot(npow[q], diag_halves(npow[q]))) for q in quads]
        tinv = [tinv[q] + _dot(bf(tinv[q]), diag_halves(npow[q])) for q in quads]
    tinv = unside(tinv)
    y = [_dot(bf(tinv[p]), jnp.concatenate([a2[p], bf(xv[p][0:c2, :])], axis=1)) for p in pairs]
    yb = [bf(y[p]) for p in pairs]
    z = [_dot(nrb[p], yb[p]) for p in pairs]
    mg = unside([_dot(bf(side(y[2 * q].T, y[2 * q + 1].T)), diag(bp2[2 * q], bp2[2 * q + 1]))
                 for q in quads])
    vk = unside([_dot(bf(side(vst[2 * q].astype(F32).T, vst[2 * q + 1].astype(F32).T)),
                      diag(kp2[2 * q], kp2[2 * q + 1])) for q in quads])
    for p in pairs:
        sl = slice(LANES * p, LANES * (p + 1))
        m_ref[:, sl] = bf(mg[p][0:c2, :])
        ga_ref[:, sl] = bf(mg[p][c2:2 * c2, :] + vk[p])
        rq_ref[:, sl] = bf(r2[p] + z[p][:, 0:c2])
        o0 = z[p][:, c2:2 * c2] + xv[p][c2:2 * c2, :]
        o0_ref[:, sl] = bf(o0[0:c, :] + o0[c:c2, :])


def _wkv_scan_kernel(m_ref, ga_ref, rq_ref, o0_ref, pc_ref, o_ref, g_scr, *, npairs):
    c = WKV_CHUNK

    @pl.when(pl.program_id(2) == 0)
    def _():
        g_scr[...] = jnp.zeros_like(g_scr)

    for p in range(npairs):
        sl = slice(LANES * p, LANES * (p + 1))
        g = g_scr[p]
        g_hi = g.astype(BF16)
        o_st = _dot_nt(rq_ref[:, sl], g_hi)
        o_ref[:, sl] = o_st[0:c, :] + o_st[c:2 * c, :] + o0_ref[:, sl]
        m = m_ref[:, sl]
        g_scr[p] = g * pc_ref[0:1, sl] + _dot(g_hi, m) + ga_ref[:, sl]


WKV_CHUNK_LANES = 2048


def _wkv_bidir(r, v, kk, lw, kd, asg, batch, ctx_len):
    n, d = r.shape
    c = WKV_CHUNK
    s = n // batch
    ncs = s // c
    ncc = ctx_len // c
    nch = n // c
    lanes = min(WKV_CHUNK_LANES, d)
    ngrp = d // lanes
    shared = pl.BlockSpec((c, lanes), lambda z, i, j: (i, j))
    perdir = pl.BlockSpec((None, c, lanes), lambda z, i, j: (z, i, j))
    big = pl.BlockSpec((None, 2 * c, lanes), lambda z, i, j: (z, i, j))
    m_, ga_, rq_, o0_, pc_ = pl.pallas_call(
        functools.partial(_wkv_chunk_kernel, npairs=lanes // LANES),
        grid=(2, nch, ngrp),
        in_specs=[shared, shared, shared, perdir, perdir, perdir],
        out_specs=[big, big, big, perdir,
                   pl.BlockSpec((None, SUBLANES, lanes), lambda z, i, j: (z, i, j))],
        out_shape=[jax.ShapeDtypeStruct((2, nch * 2 * c, d), BF16)] * 3
        + [jax.ShapeDtypeStruct((2, n, d), BF16),
           jax.ShapeDtypeStruct((2, nch * SUBLANES, d), F32)],
        compiler_params=_cparams("parallel", "parallel", "parallel"),
        name="wkv_chunk",
    )(r, v, kk, lw, kd, asg)

    def cmap(z, b, cc):
        back = jnp.where(cc < ncc, ncc - 1 - cc, ncs + ncc - 1 - cc)
        return (z, b * ncs + jnp.where(z == 0, cc, back), 0)

    out = pl.pallas_call(
        functools.partial(_wkv_scan_kernel, npairs=d // LANES),
        grid=(2, batch, ncs),
        in_specs=[
            pl.BlockSpec((None, 2 * c, d), cmap),
            pl.BlockSpec((None, 2 * c, d), cmap),
            pl.BlockSpec((None, 2 * c, d), cmap),
            pl.BlockSpec((None, c, d), cmap),
            pl.BlockSpec((None, SUBLANES, d), cmap),
        ],
        out_specs=pl.BlockSpec((None, c, d), cmap),
        out_shape=jax.ShapeDtypeStruct((2, n, d), F32),
        scratch_shapes=[pltpu.VMEM((d // LANES, 2 * c, 2 * c), F32)],
        compiler_params=_cparams("parallel", "parallel", "arbitrary"),
        name="wkv_scan",
    )(m_, ga_, rq_, o0_, pc_)
    return out


def _rwread_kernel(o_ref, r_ref, kd_ref, v_ref, g_ref, pv_ref, bd_ref, y_ref):
    bd = bd_ref[...]
    wkv = o_ref[0] + o_ref[1]
    inv = 1.0 / RW_HEAD
    mu = _head_sum(wkv, bd) * inv
    dev = wkv - mu
    var = _head_sum(dev * dev, bd) * inv
    y = dev * lax.rsqrt(var + GN_EPS) * pv_ref[1:2, :] + pv_ref[2:3, :]
    rk = r_ref[...] * (kd_ref[0] + kd_ref[1]) * pv_ref[0:1, :]
    y = y + _head_sum(rk, bd) * v_ref[...]
    y_ref[...] = (y * g_ref[...]).astype(BF16)


def _rwkv_readout(geo, wkv, r, kd, v, g, pvec):
    n, d = r.shape
    tm, tc = geo.tm, COL_TILE
    ii = lax.broadcasted_iota(jnp.int32, (tc, tc), 0) // RW_HEAD
    jj = lax.broadcasted_iota(jnp.int32, (tc, tc), 1) // RW_HEAD
    bd = (ii == jj).astype(BF16)
    row = pl.BlockSpec((tm, tc), lambda i, j: (i, j))
    dirs = pl.BlockSpec((2, tm, tc), lambda i, j: (0, i, j))
    return pl.pallas_call(
        _rwread_kernel,
        grid=(geo.ntiles, d // tc),
        in_specs=[dirs, row, dirs, row, row,
                  pl.BlockSpec((SUBLANES, tc), lambda i, j: (0, j)),
                  pl.BlockSpec((tc, tc), lambda i, j: (0, 0))],
        out_specs=row,
        out_shape=jax.ShapeDtypeStruct((n, d), BF16),
        compiler_params=_cparams("parallel", "parallel"),
        name="rwkv_readout",
    )(wkv, r, kd, v, g, pvec, bd)


def _mlaqkv_kernel(z_ref, qn_ref, kvn_ref, wuq_ref, wukv_ref, gq_ref, gk_ref, cos_ref, sin_ref,
                   q_ref, k_ref, v_ref, *, q_lora, kv_lora):
    z = z_ref[...]
    cq = z[:, 0:q_lora]
    ckv = z[:, q_lora:q_lora + kv_lora]
    krot = z[:, q_lora + kv_lora:q_lora + kv_lora + LANES]

    def rms(x, g):
        ms = jnp.mean(x * x, axis=-1, keepdims=True)
        return x * lax.rsqrt(ms + EPS) * g

    qf = _dot(rms(cq, qn_ref[...]).astype(BF16), wuq_ref[...])
    kvf = _dot(rms(ckv, kvn_ref[...]).astype(BF16), wukv_ref[...])
    cos = cos_ref[...]
    sin = sin_ref[...]
    half = QK_ROPE // 2
    lane = lax.broadcasted_iota(jnp.int32, cos.shape, 1)

    def rope(x):
        up = pltpu.roll(x, LANES - half, axis=1)
        dn = pltpu.roll(x, half, axis=1)
        return x * cos + jnp.where(lane < half, up, dn) * sin

    inv_w = 1.0 / (QK_NOPE + QK_ROPE)
    gq = gq_ref[...]
    gk = gk_ref[...]
    kr_ss = jnp.sum(krot * krot, axis=-1, keepdims=True)
    for h in range(MLA_HEADS):
        o = QK_PAD * h
        qh = qf[:, o:o + QK_PAD]
        rs = lax.rsqrt(jnp.sum(qh * qh, axis=-1, keepdims=True) * inv_w + EPS)
        qn = qh * rs * gq * (ATTN_SCALE * math.log2(math.e))
        q_ref[:, o:o + QK_NOPE] = qn[:, 0:QK_NOPE].astype(BF16)
        q_ref[:, o + QK_NOPE:o + QK_PAD] = rope(qn[:, QK_NOPE:QK_PAD]).astype(BF16)
        kn = kvf[:, o:o + QK_NOPE]
        rsk = lax.rsqrt((jnp.sum(kn * kn, axis=-1, keepdims=True) + kr_ss) * inv_w + EPS)
        k_ref[:, o:o + QK_NOPE] = (kn * rsk * gk[:, 0:QK_NOPE]).astype(BF16)
        k_ref[:, o + QK_NOPE:o + QK_PAD] = rope(krot * rsk * gk[:, QK_NOPE:QK_PAD]).astype(BF16)
        v_ref[:, V_HEAD * h:V_HEAD * (h + 1)] = kvf[:, o + QK_NOPE:o + QK_PAD].astype(BF16)


def _mla_qkv(geo, z, qn, kvn, wuq_p, wukv, gq_p, gk_p, cos_t, sin_t):
    n, zw = z.shape
    tm = geo.tm
    q_lora, kv_lora = qn.shape[0], kvn.shape[0]
    hq = MLA_HEADS * QK_PAD
    tpb = geo.tpb
    full = lambda i: (0, 0)
    rowmap = lambda i: (i, 0)
    return pl.pallas_call(
        functools.partial(_mlaqkv_kernel, q_lora=q_lora, kv_lora=kv_lora),
        grid=(geo.ntiles,),
        in_specs=[
            pl.BlockSpec((tm, zw), rowmap),
            pl.BlockSpec((1, q_lora), full),
            pl.BlockSpec((1, kv_lora), full),
            pl.BlockSpec((q_lora, hq), full),
            pl.BlockSpec((kv_lora, hq), full),
            pl.BlockSpec((1, QK_PAD), full),
            pl.BlockSpec((1, QK_PAD), full),
            pl.BlockSpec((tm, LANES), lambda i: (i % tpb, 0)),
            pl.BlockSpec((tm, LANES), lambda i: (i % tpb, 0)),
        ],
        out_specs=[pl.BlockSpec((tm, hq), rowmap), pl.BlockSpec((tm, hq), rowmap),
                   pl.BlockSpec((tm, MLA_HEADS * V_HEAD), rowmap)],
        out_shape=[jax.ShapeDtypeStruct((n, hq), BF16), jax.ShapeDtypeStruct((n, hq), BF16),
                   jax.ShapeDtypeStruct((n, MLA_HEADS * V_HEAD), BF16)],
        compiler_params=_cparams("parallel"),
        name="mla_qkv",
    )(z, qn.reshape(1, -1), kvn.reshape(1, -1), wuq_p, wukv, gq_p, gk_p, cos_t, sin_t)


ATTN_SUBTILES = 2


def _attn_kernel(q_ref, k_ref, vt_ref, o_ref):
    k = k_ref[...]
    vt = vt_ref[...]
    cols = q_ref.shape[0] // ATTN_SUBTILES
    subs = range(ATTN_SUBTILES)
    s = [_dot_nt(k, q_ref[cols * i:cols * (i + 1), :]) for i in subs]
    m = [jnp.max(s[i], axis=0, keepdims=True) for i in subs]
    p = [jnp.exp2(s[i] - m[i]) for i in subs]
    l = [jnp.sum(p[i], axis=0, keepdims=True) for i in subs]
    for i in subs:
        ot = _dot(vt, p[i].astype(BF16)) / l[i]
        o_ref[cols * i:cols * (i + 1), :] = ot.T.astype(BF16)


def _attn_call(q3, k3, vt4, tq, nkeys):
    b, sq, _ = q3.shape
    return pl.pallas_call(
        _attn_kernel,
        grid=(b, MLA_HEADS, sq // tq),
        in_specs=[
            pl.BlockSpec((None, tq, QK_PAD), lambda bb, h, i: (bb, i, h)),
            pl.BlockSpec((None, nkeys, QK_PAD), lambda bb, h, i: (bb, 0, h)),
            pl.BlockSpec((None, None, V_HEAD, nkeys), lambda bb, h, i: (bb, h, 0, 0)),
        ],
        out_specs=pl.BlockSpec((None, tq, V_HEAD), lambda bb, h, i: (bb, i, h)),
        out_shape=jax.ShapeDtypeStruct((b, sq, MLA_HEADS * V_HEAD), BF16),
        compiler_params=_cparams("parallel", "parallel", "parallel"),
        name="mla_attention",
    )(q3, k3, vt4)


def _mla_attention(geo, q, k, v):
    b, s, l = geo.b, geo.s, geo.l
    q3 = q.reshape(b, s, MLA_HEADS * QK_PAD)
    k3 = k.reshape(b, s, MLA_HEADS * QK_PAD)
    vt4 = v.reshape(b, s, MLA_HEADS, V_HEAD).transpose(0, 2, 3, 1)
    o_ctx = _attn_call(q3[:, :l], k3, vt4, geo.tm, l)
    tq_lat = 2 * geo.tm if geo.t % (2 * geo.tm) == 0 else geo.tm
    o_lat = _attn_call(q3[:, l:], k3, vt4, tq_lat, s)
    return jnp.concatenate([o_ctx, o_lat], axis=1).reshape(b * s, MLA_HEADS * V_HEAD)


def _topk_rows(s, k, payload=None):
    rows = lax.broadcasted_iota(jnp.int32, s.shape, 0).astype(F32)
    big = float(s.shape[0])
    vals, idxs = [], []
    for _ in range(k):
        m = jnp.max(s, axis=0, keepdims=True)
        idx = jnp.min(jnp.where(s == m, rows, big), axis=0, keepdims=True)
        hit = rows == idx
        vals.append(m)
        if payload is None:
            idxs.append(idx)
        else:
            idxs.append(jnp.sum(jnp.where(hit, payload, 0.0), axis=0, keepdims=True))
        s = jnp.where(hit, -jnp.inf, s)
    return jnp.concatenate(vals, axis=0), jnp.concatenate(idxs, axis=0)


def _peersel_kernel(q_ref, qn_ref, keys_ref, e_ref, g_ref):
    q = q_ref[...]
    ms = jnp.mean(q * q, axis=-1, keepdims=True)
    qn = q * lax.rsqrt(ms + EPS) * qn_ref[...]
    half = D_KEY // 2
    s1 = _dot3_nt(keys_ref[0], qn[:, 0:half])
    s2 = _dot3_nt(keys_ref[1], qn[:, half:D_KEY])
    t1, i1 = _topk_rows(s1, PEER_TOPK)
    t2, i2 = _topk_rows(s2, PEER_TOPK)
    k = PEER_TOPK
    sub = lax.broadcasted_iota(jnp.int32, (SUBLANES, t1.shape[1]), 0)
    cand = [t1[0:1, :] + t2]
    cidx = [i1[0:1, :] * float(N_KEYS) + i2]
    for p in range(1, k // 2):
        live = sub < k // (p + 1)
        cand.append(jnp.where(live, t1[p:p + 1, :] + t2[0:SUBLANES, :], -jnp.inf))
        cidx.append(i1[p:p + 1, :] * float(N_KEYS) + i2[0:SUBLANES, :])
    cand.append(t1[k // 2:k, :] + t2[0:1, :])
    cidx.append(i1[k // 2:k, :] * float(N_KEYS) + i2[0:1, :])
    best, eidx = _topk_rows(jnp.concatenate(cand, axis=0), k, payload=jnp.concatenate(cidx, axis=0))
    ex = jnp.exp(best - jnp.max(best, axis=0, keepdims=True))
    g_ref[...] = ex / jnp.sum(ex, axis=0, keepdims=True)
    e_ref[...] = eidx.astype(jnp.int32)


def _peer_select(qp, q_norm, keys, tm, row0, n):
    assert row0 % tm == 0 and n % tm == 0
    tile0 = row0 // tm
    return pl.pallas_call(
        _peersel_kernel,
        grid=(n // tm, PEER_HEADS),
        in_specs=[
            pl.BlockSpec((tm, D_KEY), lambda i, h: (tile0 + i, h)),
            pl.BlockSpec((1, D_KEY), lambda i, h: (0, 0)),
            pl.BlockSpec((2, N_KEYS, D_KEY // 2), lambda i, h: (0, 0, 0)),
        ],
        out_specs=[pl.BlockSpec((PEER_TOPK, tm), lambda i, h: (h, i)),
                   pl.BlockSpec((PEER_TOPK, tm), lambda i, h: (h, i))],
        out_shape=[jax.ShapeDtypeStruct((PEER_HEADS * PEER_TOPK, n), jnp.int32),
                   jax.ShapeDtypeStruct((PEER_HEADS * PEER_TOPK, n), F32)],
        compiler_params=_cparams("parallel", "parallel"),
        name="peer_select",
    )(qp, q_norm.reshape(1, D_KEY), keys)


GATHER_SLOTS = 4
SLAB_PAD = 1


def _pack_expert_table(u, v):
    ne, d = u.shape
    ub = lax.bitcast_convert_type(u.astype(BF16), jnp.uint16).astype(jnp.uint32)
    vb = lax.bitcast_convert_type(v.astype(BF16), jnp.uint16).astype(jnp.uint32)
    return ((vb << 16) | ub).reshape(ne, d // LANES, LANES)


def _peergather_kernel(idx_ref, idxn_ref, gate_ref, h_ref, x_ref, mod_ref, tab_ref, o_ref,
                       *scratch, tb):
    nsel = PEER_HEADS * PEER_TOPK
    nrow = h_ref.shape[1] // LANES
    pitch = nrow + SLAB_PAD
    ns = GATHER_SLOTS
    bufs, sem = scratch[:ns], scratch[ns]
    lane_t = lax.broadcasted_iota(jnp.int32, (nsel, tb), 1)
    g2 = mod_ref[5:6, :]
    step = pl.program_id(0)
    nsteps = pl.num_programs(0)

    def row_copy(ids_ref, t, j, slot):
        return pltpu.make_async_copy(
            tab_ref.at[ids_ref[t, j]],
            bufs[slot].at[pl.ds(j * pitch, nrow), :],
            sem.at[slot])

    def issue(ids_ref, t, slot):
        for j in range(nsel):
            row_copy(ids_ref, t, j, slot).start(priority=j % 2)

    def wait(t, slot):
        for j in range(nsel):
            row_copy(idx_ref, t, j, slot).wait()

    def packed(slot, s):
        return bufs[slot][pl.ds(s, nsel, stride=pitch), :]

    def compute(t, slot):
        hrow = h_ref[pl.ds(t, 1), :]
        acc = jnp.zeros((nsel, LANES), F32)
        for s in range(nrow):
            u = lax.bitcast_convert_type(packed(slot, s) << 16, F32)
            acc = acc + u * hrow[:, LANES * s:LANES * (s + 1)]
        dots = jnp.sum(acc, axis=-1, keepdims=True)
        gcol = jnp.sum(jnp.where(lane_t == t, gate_ref[...], 0.0), axis=-1, keepdims=True)
        coef = gcol * _gelu(dots)
        outs = []
        for s in range(nrow):
            vv = lax.bitcast_convert_type(packed(slot, s) & jnp.uint32(0xFFFF0000), F32)
            outs.append(jnp.sum(coef * vv, axis=0, keepdims=True))
        orow = jnp.concatenate(outs, axis=1)
        o_ref[pl.ds(t, 1), :] = x_ref[pl.ds(t, 1), :] + g2 * orow

    @pl.when(step == 0)
    def _():
        for s in range(ns - 1):
            issue(idx_ref, s, s)

    ngroups = tb // ns

    def body(g, carry):
        for s in range(ns):
            t = g * ns + s
            wait(t, s)
            issue(idx_ref, t + ns - 1, (s + ns - 1) % ns)
            compute(t, s)
        return carry

    lax.fori_loop(0, ngroups - 1, body, 0)
    for s in range(ns):
        t = (ngroups - 1) * ns + s
        wait(t, s)
        if s == 0:
            issue(idx_ref, tb - 1, ns - 1)
        else:
            @pl.when(step < nsteps - 1)
            def _():
                issue(idxn_ref, s - 1, s - 1)
        compute(t, s)


def _peer_gather(geo_g, eidx, gate_t, h2, x, mods, table, n):
    d = x.shape[1]
    tb = geo_g.tm
    nsel = PEER_HEADS * PEER_TOPK
    pitch = d // LANES + SLAB_PAD
    assert n % tb == 0
    nsteps = n // tb
    return pl.pallas_call(
        functools.partial(_peergather_kernel, tb=tb),
        grid=(nsteps,),
        in_specs=[
            pl.BlockSpec((tb, nsel), lambda i: (i, 0), memory_space=pltpu.SMEM),
            pl.BlockSpec((tb, nsel), lambda i: (jnp.minimum(i + 1, nsteps - 1), 0),
                         memory_space=pltpu.SMEM),
            pl.BlockSpec((nsel, tb), lambda i: (0, i)),
            pl.BlockSpec((tb, d), lambda i: (i, 0)),
            pl.BlockSpec((tb, d), lambda i: (i, 0)),
            geo_g.mod_spec(),
            pl.BlockSpec(memory_space=pl.ANY),
        ],
        out_specs=pl.BlockSpec((tb, d), lambda i: (i, 0)),
        out_shape=jax.ShapeDtypeStruct((n, d), F32),
        scratch_shapes=[pltpu.VMEM((nsel * pitch, LANES), jnp.uint32)] * GATHER_SLOTS
        + [pltpu.SemaphoreType.DMA((GATHER_SLOTS,))],
        compiler_params=_cparams("arbitrary"),
        name="peer_gather",
    )(eidx, eidx, gate_t, h2, x, mods, table)


SC_CORES = 2
SC_SUBCORES = 16
SC_LANES = 16
SC_CHUNK = 16
SC_SHARE = 0.3235

_ERF_ALPHA = (0.00022905065861350646, 0.0034082910107109506, 0.050955695062380861,
              0.18520832239976145, 1.128379143519084)
_ERF_BETA = (-1.1791602954361697e-7, 0.000023547966471313185, 0.0010179625278914885,
             0.014070470171167667, 0.11098505178285362, 0.49746925110067538, 1.0)
_ERF_CLAMP = 3.832506856900711


def _erf_rational(x):
    x = jnp.minimum(jnp.maximum(x, -_ERF_CLAMP), _ERF_CLAMP)
    x2 = x * x
    p = jnp.full_like(x, _ERF_ALPHA[0])
    for c in _ERF_ALPHA[1:]:
        p = p * x2 + c
    q = jnp.full_like(x, _ERF_BETA[0])
    for c in _ERF_BETA[1:]:
        q = q * x2 + c
    return x * p / q


def _peer_sc(eidx, gate, h, table, row0):
    n, nsel = eidx.shape
    d = h.shape[1]
    nw = SC_CORES * SC_SUBCORES
    assert n % (2 * nw) == 0 and nsel % SC_CHUNK == 0 and d % SC_LANES == 0
    tpw = n // nw
    nchunk = nsel // SC_CHUNK
    nvec = d // SC_LANES
    mesh = plsc.VectorSubcoreMesh(core_axis_name="c", subcore_axis_name="s",
                                  num_cores=SC_CORES, num_subcores=SC_SUBCORES)

    @functools.partial(
        pl.kernel, mesh=mesh,
        out_type=jax.ShapeDtypeStruct((n, d), F32),
        scratch_types=[
            pltpu.VMEM((nsel,), jnp.int32), pltpu.VMEM((nsel,), jnp.int32),
            pltpu.VMEM((nsel,), F32), pltpu.VMEM((nsel,), F32),
            pltpu.VMEM((d,), F32), pltpu.VMEM((d,), F32),
            pltpu.VMEM((d,), F32),
            pltpu.VMEM((SC_CHUNK, d), jnp.uint32),
            pltpu.VMEM((SC_CHUNK, d), jnp.uint32),
            pltpu.SemaphoreType.DMA, pltpu.SemaphoreType.DMA,
            pltpu.SemaphoreType.DMA, pltpu.SemaphoreType.DMA,
        ],
        compiler_params=pltpu.CompilerParams(needs_layout_passes=False),
        name="peer_sc",
    )
    def sc_kernel(eidx_hbm, gate_hbm, h_hbm, tab_hbm, y_hbm, idx_a, idx_b, gate_a, gate_b, h_a, h_b,
                  o_v, rows0, rows1, sem0, sem1, msem_a, msem_b):
        wid = lax.axis_index("s") * SC_CORES + lax.axis_index("c")
        lanes = lax.iota(jnp.int32, SC_LANES)
        zero = jnp.zeros((SC_LANES,), F32)
        bufs = ((rows0, sem0), (rows1, sem1))
        meta = ((idx_a, gate_a, h_a, msem_a), (idx_b, gate_b, h_b, msem_b))

        def meta_copies(tok, s):
            idx_v, gate_v, h_v, msem = meta[s]
            return (pltpu.make_async_copy(eidx_hbm.at[tok], idx_v, msem),
                    pltpu.make_async_copy(gate_hbm.at[tok], gate_v, msem),
                    pltpu.make_async_copy(h_hbm.at[row0 + tok], h_v, msem))

        def gather(s, c, slot):
            rows, sem = bufs[slot]
            return pltpu.make_async_copy(
                tab_hbm.at[meta[s][0].at[pl.ds(c * SC_CHUNK, SC_CHUNK)]], rows, sem)

        def chunk_compute(s, c, rows):
            _, gate_v, h_v, _ = meta[s]

            def dot_body(i, accs):
                hv = h_v[pl.ds(i * SC_LANES, SC_LANES)]
                out = []
                for e in range(SC_CHUNK):
                    w = rows[e, pl.ds(i * SC_LANES, SC_LANES)]
                    out.append(accs[e] + lax.bitcast_convert_type(w << 16, F32) * hv)
                return tuple(out)

            accs = lax.fori_loop(0, nvec, dot_body, (zero,) * SC_CHUNK)
            dots = zero
            for e in range(SC_CHUNK):
                dots = jnp.where(lanes == e, jnp.sum(accs[e]), dots)
            act = 0.5 * dots * (1.0 + _erf_rational(dots * (2.0 ** -0.5)))
            coef = gate_v[pl.ds(c * SC_CHUNK, SC_CHUNK)] * act
            splat = [jnp.full((SC_LANES,), jnp.sum(jnp.where(lanes == e, coef, 0.0)), F32)
                     for e in range(SC_CHUNK)]

            def acc_body(i, carry):
                o = o_v[pl.ds(i * SC_LANES, SC_LANES)]
                for e in range(SC_CHUNK):
                    w = rows[e, pl.ds(i * SC_LANES, SC_LANES)]
                    o = o + splat[e] * lax.bitcast_convert_type(w & jnp.uint32(0xFFFF0000), F32)
                o_v[pl.ds(i * SC_LANES, SC_LANES)] = o
                return carry

            lax.fori_loop(0, nvec, acc_body, 0)

        def zero_body(i, c2):
            o_v[pl.ds(i * SC_LANES, SC_LANES)] = zero
            return c2

        def token(tok, s, has_next):
            def when_next(fn):
                if isinstance(has_next, bool):
                    if has_next:
                        fn()
                else:
                    pl.when(has_next)(fn)

            def load_next():
                for cp in meta_copies(tok + 1, 1 - s):
                    cp.start()

            when_next(load_next)
            lax.fori_loop(0, nvec, zero_body, 0)
            for c in range(nchunk):
                if c + 1 < nchunk:
                    gather(s, c + 1, (c + 1) % 2).start()
                gather(s, c, c % 2).wait()
                chunk_compute(s, c, bufs[c % 2][0])

            def prefetch_next():
                for cp in meta_copies(tok + 1, 1 - s):
                    cp.wait()
                gather(1 - s, 0, 0).start()

            when_next(prefetch_next)
            pltpu.sync_copy(o_v, y_hbm.at[tok])

        base = wid * tpw
        for cp in meta_copies(base, 0):
            cp.start()
        for cp in meta_copies(base, 0):
            cp.wait()
        gather(0, 0, 0).start()

        def pair_body(g, carry):
            token(base + 2 * g, 0, True)
            token(base + 2 * g + 1, 1, g + 1 < tpw // 2)
            return carry

        lax.fori_loop(0, tpw // 2, pair_body, 0)

    return sc_kernel(eidx, gate, h, table)


def _pad_to(x, axis, size):
    pad = [(0, 0)] * x.ndim
    pad[axis] = (0, size - x.shape[axis])
    return jnp.pad(x, pad)


def _rwkv_layer(geo, xs, mods, norm1, mix, w_rkv, w_o, w0, w1, w2, a0, a1, a2, vl, g1, g2,
                k_k, k_a, r_k, ln_w, ln_b, vfirst):
    d = geo.d
    xm = _rwkv_mix(geo, xs, norm1, mods, mix)
    rkv = _bmm(xm, w_rkv.astype(BF16), (0, 2, 3), ("none",) * 3, F32, geo.tm)
    lt = g1.shape[1]
    w1c = jnp.concatenate([_pad_to(w1[0], 1, LORA_PAD), _pad_to(w1[1], 1, LORA_PAD)], axis=1)
    a1c = jnp.concatenate([_pad_to(a1[0], 1, LORA_PAD), _pad_to(a1[1], 1, LORA_PAD)], axis=1)
    if vl is None:
        v1p = jnp.zeros((d, lt), F32)
        v2p = jnp.zeros((LORA_PAD, d), F32)
        v0 = jnp.zeros((d,), F32)
    else:
        v0, v1, v2 = vl
        v1p = _pad_to(v1, 1, lt)
        v2p = _pad_to(v2, 0, LORA_PAD)
    wl1 = jnp.stack([_pad_to(w1c, 1, lt), _pad_to(a1c, 1, lt), g1, v1p]).astype(BF16)
    tl = _bmm(xm, wl1, (1, 4, 5, 3), ("tanh", "none", "sigmoid", "none"), BF16, geo.tm)
    w2p = jnp.stack([_pad_to(w2[0], 0, LORA_PAD), _pad_to(w2[1], 0, LORA_PAD)]).astype(BF16)
    a2p = jnp.stack([_pad_to(a2[0], 0, LORA_PAD), _pad_to(a2[1], 0, LORA_PAD)]).astype(BF16)
    pvec = jnp.stack([w0[0], w0[1], a0[0], a0[1], v0, k_k, k_a, jnp.zeros_like(k_k)])
    lw, kd, asg, kk, g, v = _rwkv_feat(geo, rkv, tl, w2p, a2p, g2.astype(BF16), v2p.astype(BF16),
                                      pvec, vfirst)
    r = rkv[0]
    wkv = _wkv_bidir(r, v, kk, lw, kd, asg, geo.b, geo.l)
    pv2 = _pad_to(jnp.stack([r_k, ln_w, ln_b]), 0, SUBLANES)
    y = _rwkv_readout(geo, wkv, r, kd, v, g, pv2)
    xs = _matmul_res(geo, y, w_o.astype(BF16), xs, mods, 2)
    return xs, v


def _rope_tables(geo):
    t = geo.t
    pos = jnp.arange(t)
    row = (pos // GRID_W).astype(F32)
    col = (pos % GRID_W).astype(F32)
    n_freq = QK_ROPE // 4
    inv_freq = ROPE_THETA ** (-jnp.arange(n_freq, dtype=F32) / n_freq)
    ang = jnp.concatenate([row[:, None] * inv_freq, col[:, None] * inv_freq], axis=-1)
    cos, sin = jnp.cos(ang), jnp.sin(ang)
    pad = LANES - QK_ROPE
    cos_l = jnp.concatenate([cos, cos, jnp.ones((t, pad), F32)], axis=1)
    sin_l = jnp.concatenate([-sin, sin, jnp.zeros((t, pad), F32)], axis=1)
    cos_c = jnp.ones((geo.l, LANES), F32)
    sin_c = jnp.zeros((geo.l, LANES), F32)
    return jnp.concatenate([cos_c, cos_l], axis=0), jnp.concatenate([sin_c, sin_l], axis=0)


def _mla_layer(geo, xs, mods, norm1, rope_t, w_in, q_norm, kv_norm, w_uq, w_ukv, g_q, g_k, w_o):
    q_lora, kv_lora = q_norm.shape[0], kv_norm.shape[0]
    zw = q_lora + kv_lora + LANES
    z = _mod_matmul(geo, xs, norm1, mods, _pad_to(w_in, 1, zw).astype(BF16), 0, False)
    qk = QK_NOPE + QK_ROPE
    wuq_p = _pad_to(w_uq.reshape(q_lora, MLA_HEADS, qk), 2, QK_PAD).reshape(q_lora, -1)
    gq_p = _pad_to(g_q, 0, QK_PAD).reshape(1, QK_PAD)
    gk_p = _pad_to(g_k, 0, QK_PAD).reshape(1, QK_PAD)
    q, k, v = _mla_qkv(geo, z, q_norm, kv_norm, wuq_p.astype(BF16), w_ukv.astype(BF16),
                       gq_p, gk_p, *rope_t)
    o = _mla_attention(geo, q, k, v)
    return _matmul_res(geo, o, w_o.astype(BF16), xs, mods, 2)


def _peer_layer(geo, geo_g, xs, mods, norm2, w_q, q_norm, keys, u, v):
    qp, h2 = _mod_matmul(geo, xs, norm2, mods, w_q.astype(BF16), 1, True)
    table = _pack_expert_table(u, v)
    n, d = xs.shape
    n_sc = geo.tm * round(SC_SHARE * n / geo.tm)
    if n_sc % (2 * SC_CORES * SC_SUBCORES) != 0:
        n_sc = 0
    n_tc = n - n_sc

    def select(row0, nrows):
        wide = 2 * geo.tm
        tsel = wide if row0 % wide == 0 and nrows % wide == 0 else geo.tm
        return _peer_select(qp, q_norm, keys, tsel, row0, nrows)

    if n_sc:
        eidx_sc, gate_sc = select(n_tc, n_sc)
        y_sc = _peer_sc(eidx_sc.T, gate_sc.T, h2, table.reshape(table.shape[0], d), n_tc)
    eidx_t, gate_t = select(0, n_tc)
    out_tc = _peer_gather(geo_g, eidx_t.T, gate_t, h2, xs, mods, table, n_tc)
    if n_sc == 0:
        return out_tc
    g2, r = [], n_tc
    while r < n:
        bi, pos = divmod(r, geo.s)
        seg = int(pos >= geo.l)
        stop = min(n, bi * geo.s + (geo.s if seg else geo.l))
        g2.append(jnp.broadcast_to(mods[bi, seg, 5], (stop - r, d)))
        r = stop
    out_sc = xs[n_tc:] + jnp.concatenate(g2, axis=0) * y_sc
    return jnp.concatenate([out_tc, out_sc], axis=0)


def kernel(x, c, ctx, c_ctx, w_ada, b_ada, norm1, norm2, rw_mix, rw_wrkv, rw_wo, rw_w0, rw_w1, rw_w2, rw_a0, rw_a1, rw_a2, rw_v0, rw_v1, rw_v2, rw_g1, rw_g2, rw_kk, rw_ka, rw_rk, rw_lnw, rw_lnb, mla_win, mla_qnorm, mla_kvnorm, mla_wuq, mla_wukv, mla_gq, mla_gk, mla_wo, peer_wq, peer_qnorm, peer_keys, peer_u, peer_v):
    b, t, d = x.shape
    l = ctx.shape[1]
    depth = w_ada.shape[0]
    geo = _Geom(b, l, t, d, min(ROW_TILE, l))
    geo_g = _Geom(b, l, t, d, min(GATHER_TILE, l))
    cond8 = _pad_to(jnp.concatenate([c, c_ctx[None, :]], axis=0), 0, SUBLANES)
    ada = _adaln(cond8, w_ada, b_ada).reshape(depth, SUBLANES, 6, d)
    mods_all = jnp.stack([jnp.broadcast_to(ada[:, b:b + 1], (depth, b, 6, d)), ada[:, 0:b]], axis=2)
    xs = jnp.concatenate([ctx, x], axis=1).reshape(b * (l + t), d)
    rope_t = _rope_tables(geo)
    vfirst = None
    for i in range(depth):
        j = i // 2
        mods = mods_all[i]
        if i % 2 == 0:
            vl = None if j == 0 else (rw_v0[j - 1], rw_v1[j - 1], rw_v2[j - 1])
            xs, vcur = _rwkv_layer(geo, xs, mods, norm1[i], rw_mix[j], rw_wrkv[j], rw_wo[j],
                                   rw_w0[j], rw_w1[j], rw_w2[j], rw_a0[j], rw_a1[j], rw_a2[j], vl,
                                   rw_g1[j], rw_g2[j], rw_kk[j], rw_ka[j], rw_rk[j], rw_lnw[j],
                                   rw_lnb[j], vfirst)
            if j == 0:
                vfirst = vcur
        else:
            xs = _mla_layer(geo, xs, mods, norm1[i], rope_t, mla_win[j], mla_qnorm[j],
                            mla_kvnorm[j], mla_wuq[j], mla_wukv[j], mla_gq[j], mla_gk[j], mla_wo[j])
        if i == depth - 1:
            xs = xs.reshape(b, l + t, d)[:, l:, :].reshape(b * t, d)
            geo, geo_g = _Geom(b, 0, t, d, geo.tm), _Geom(b, 0, t, d, geo_g.tm)
        xs = _peer_layer(geo, geo_g, xs, mods, norm2[i], peer_wq[i], peer_qnorm[i], peer_keys[i],
                         peer_u[i], peer_v[i])
    return xs.reshape(b, t, d)
```

```python
import functools
import math

import jax
import jax.numpy as jnp
from jax import lax
from jax.experimental import pallas as pl
from jax.experimental.pallas import tpu as pltpu
from jax.experimental.pallas import tpu_sc as plsc

F32 = jnp.float32
BF16 = jnp.bfloat16

EPS = 1e-6
GN_EPS = 64e-5
RW_HEAD = 64
WKV_CHUNK = 64
MLA_HEADS = 16
QK_NOPE = 128
QK_ROPE = 64
V_HEAD = 128
QK_PAD = 256
ROPE_THETA = 10000.0
GRID_W = 64
ATTN_SCALE = (QK_NOPE + QK_ROPE) ** -0.5
PEER_HEADS = 8
N_KEYS = 128
PEER_TOPK = 16
D_KEY = 256
LORA_PAD = 128

LANES = 128
SUBLANES = 8
VMEM_LIMIT = 56 * 1024 * 1024
ROW_TILE = 256
GATHER_TILE = 128
COL_TILE = 512
ADA_COLS = 1024


def _cparams(*sem):
    return pltpu.CompilerParams(dimension_semantics=sem, vmem_limit_bytes=VMEM_LIMIT)


def _dot(a, b):
    return jnp.dot(a, b, preferred_element_type=F32)


def _dot_nt(a, b):
    return lax.dot_general(a, b, (((1,), (1,)), ((), ())), preferred_element_type=F32)


def _split2(x):
    hi = x.astype(BF16)
    lo = (x - hi.astype(F32)).astype(BF16)
    return hi, lo


def _split3(x):
    hi = x.astype(BF16)
    r1 = x - hi.astype(F32)
    mid = r1.astype(BF16)
    lo = (r1 - mid.astype(F32)).astype(BF16)
    return hi, mid, lo


def _dot3(a, b):
    ah, al = _split2(a)
    bh, bl = _split2(b)
    return _dot(ah, bh) + (_dot(ah, bl) + _dot(al, bh))


def _dot3_nt(a, b):
    ah, al = _split2(a)
    bh, bl = _split2(b)
    return _dot_nt(ah, bh) + (_dot_nt(ah, bl) + _dot_nt(al, bh))


def _dot_exact_lhs(sel, x):
    hi, mid, lo = _split3(x)
    return _dot(sel, hi) + (_dot(sel, mid) + _dot(sel, lo))


def _modulate(x, g, shift, scale):
    ms = jnp.mean(x * x, axis=-1, keepdims=True)
    return (x * lax.rsqrt(ms + EPS) * g) * (1.0 + scale) + shift


def _sigmoid(x):
    return 1.0 / (1.0 + jnp.exp(-x))


def _softplus(y):
    return jnp.maximum(y, 0.0) + jnp.log(1.0 + jnp.exp(-jnp.abs(y)))


def _erf(x):
    return lax.erf(x)


def _gelu(x):
    return 0.5 * x * (1.0 + _erf(x * (2.0 ** -0.5)))


def _ada_kernel(s_ref, w_ref, b_ref, o_ref):
    s = s_ref[...]
    s = s * _sigmoid(s)
    o_ref[...] = _dot3(s, w_ref[...]) + b_ref[...]


def _adaln(cond8, w_ada, b_ada):
    depth, d, n = w_ada.shape
    tn = min(ADA_COLS, n)
    return pl.pallas_call(
        _ada_kernel,
        grid=(depth, n // tn),
        in_specs=[
            pl.BlockSpec((SUBLANES, d), lambda l, j: (0, 0)),
            pl.BlockSpec((None, d, tn), lambda l, j: (l, 0, j)),
            pl.BlockSpec((None, 1, tn), lambda l, j: (l, 0, j)),
        ],
        out_specs=pl.BlockSpec((None, SUBLANES, tn), lambda l, j: (l, 0, j)),
        out_shape=jax.ShapeDtypeStruct((depth, SUBLANES, n), F32),
        compiler_params=_cparams("parallel", "parallel"),
        name="adaln",
    )(cond8, w_ada, b_ada.reshape(depth, 1, n))


class _Geom:
    def __init__(self, batch, ctx_len, seq_len, d_model, tm):
        self.b, self.l, self.t, self.d = batch, ctx_len, seq_len, d_model
        self.s = ctx_len + seq_len
        self.n = batch * self.s
        self.tm = tm
        assert ctx_len % tm == 0 and seq_len % tm == 0
        self.tpb = self.s // tm
        self.nct = ctx_len // tm
        self.ntiles = self.n // tm

    def mod_spec(self):
        tpb, nct = self.tpb, self.nct

        def imap(i):
            return (i // tpb, ((i % tpb) >= nct).astype(jnp.int32), 0, 0)

        return pl.BlockSpec((None, None, 6, self.d), imap)


def _modmm_kernel(x_ref, g_ref, mod_ref, w_ref, o_ref, *h_ref, which):
    h = _modulate(x_ref[...], g_ref[...], mod_ref[3 * which:3 * which + 1, :],
                  mod_ref[3 * which + 1:3 * which + 2, :])
    if h_ref:
        h_ref[0][...] = h
    o_ref[...] = _dot(h.astype(BF16), w_ref[...])


def _mod_matmul(geo, x, g, mods, w, which, emit_h):
    n, d = x.shape
    nn = w.shape[1]
    tm = geo.tm
    out_shape = [jax.ShapeDtypeStruct((n, nn), F32)]
    out_specs = [pl.BlockSpec((tm, nn), lambda i: (i, 0))]
    if emit_h:
        out_shape.append(jax.ShapeDtypeStruct((n, d), F32))
        out_specs.append(pl.BlockSpec((tm, d), lambda i: (i, 0)))
    res = pl.pallas_call(
        functools.partial(_modmm_kernel, which=which),
        grid=(geo.ntiles,),
        in_specs=[
            pl.BlockSpec((tm, d), lambda i: (i, 0)),
            pl.BlockSpec((1, d), lambda i: (0, 0)),
            geo.mod_spec(),
            pl.BlockSpec((d, nn), lambda i: (0, 0)),
        ],
        out_specs=out_specs,
        out_shape=out_shape,
        compiler_params=_cparams("parallel"),
        name="mod_matmul",
    )(x, g.reshape(1, d), mods, w)
    return res if emit_h else res[0]


def _mmres_kernel(y_ref, w_ref, x_ref, mod_ref, o_ref, *, gidx):
    acc = _dot(y_ref[...], w_ref[...])
    o_ref[...] = x_ref[...] + mod_ref[gidx:gidx + 1, :] * acc


def _matmul_res(geo, y, w, x, mods, gidx):
    n, k = y.shape
    d = x.shape[1]
    tm = geo.tm
    return pl.pallas_call(
        functools.partial(_mmres_kernel, gidx=gidx),
        grid=(geo.ntiles,),
        in_specs=[
            pl.BlockSpec((tm, k), lambda i: (i, 0)),
            pl.BlockSpec((k, d), lambda i: (0, 0)),
            pl.BlockSpec((tm, d), lambda i: (i, 0)),
            geo.mod_spec(),
        ],
        out_specs=pl.BlockSpec((tm, d), lambda i: (i, 0)),
        out_shape=jax.ShapeDtypeStruct((n, d), F32),
        compiler_params=_cparams("parallel"),
        name="matmul_res",
    )(y, w, x, mods)


def _bmm_kernel(x_ref, w_ref, o_ref, *, acts):
    j = pl.program_id(0)
    y = _dot(x_ref[...], w_ref[...])
    out = y
    for jj, a in enumerate(acts):
        if a == "tanh":
            out = jnp.where(j == jj, jnp.tanh(y), out)
        elif a == "sigmoid":
            out = jnp.where(j == jj, _sigmoid(y), out)
    o_ref[...] = out.astype(o_ref.dtype)


def _bmm(x3, w3, src, acts, out_dtype, tm):
    _, n, k = x3.shape
    nj, _, nn = w3.shape
    src = tuple(src)

    def xmap(j, i):
        idx = jnp.int32(src[0])
        for jj in range(1, nj):
            idx = jnp.where(j == jj, jnp.int32(src[jj]), idx)
        return (idx, i, 0)

    return pl.pallas_call(
        functools.partial(_bmm_kernel, acts=tuple(acts)),
        grid=(nj, n // tm),
        in_specs=[
            pl.BlockSpec((None, tm, k), xmap),
            pl.BlockSpec((None, k, nn), lambda j, i: (j, 0, 0)),
        ],
        out_specs=pl.BlockSpec((None, tm, nn), lambda j, i: (j, i, 0)),
        out_shape=jax.ShapeDtypeStruct((nj, n, nn), out_dtype),
        compiler_params=_cparams("parallel", "parallel"),
        name="bmm",
    )(x3, w3)


def _rwmix_kernel(x_ref, xp_ref, xn_ref, g_ref, mod_ref, mix_ref, o_ref, *, tpb, nct):
    i = pl.program_id(0)
    tm = x_ref.shape[0]
    g = g_ref[...]
    shift = mod_ref[0:1, :]
    scale = mod_ref[1:2, :]
    h = _modulate(x_ref[...], g, shift, scale)
    hp = _modulate(xp_ref[...], g, shift, scale)[SUBLANES - 1:SUBLANES, :]
    hn = _modulate(xn_ref[...], g, shift, scale)[0:1, :]
    it = i % tpb
    first = jnp.logical_or(it == 0, it == nct)
    last = jnp.logical_or(it == nct - 1, it == tpb - 1)
    hp = jnp.where(first, 0.0, hp)
    hn = jnp.where(last, 0.0, hn)
    rows = lax.broadcasted_iota(jnp.int32, h.shape, 0)
    prev = jnp.where(rows == 0, hp, pltpu.roll(h, 1, axis=0))
    nxt = jnp.where(rows == tm - 1, hn, pltpu.roll(h, tm - 1, axis=0))
    xx = 0.5 * (prev + nxt) - h
    for m in range(6):
        o_ref[m] = (h + xx * mix_ref[m:m + 1, :]).astype(BF16)


def _rwkv_mix(geo, x, g, mods, mix):
    n, d = x.shape
    tm = geo.tm
    r8 = tm // SUBLANES
    nblk8 = n // SUBLANES
    return pl.pallas_call(
        functools.partial(_rwmix_kernel, tpb=geo.tpb, nct=geo.nct),
        grid=(geo.ntiles,),
        in_specs=[
            pl.BlockSpec((tm, d), lambda i: (i, 0)),
            pl.BlockSpec((SUBLANES, d), lambda i: (jnp.maximum(i * r8 - 1, 0), 0)),
            pl.BlockSpec((SUBLANES, d), lambda i: (jnp.minimum((i + 1) * r8, nblk8 - 1), 0)),
            pl.BlockSpec((1, d), lambda i: (0, 0)),
            geo.mod_spec(),
            pl.BlockSpec((6, d), lambda i: (0, 0)),
        ],
        out_specs=pl.BlockSpec((6, tm, d), lambda i: (0, i, 0)),
        out_shape=jax.ShapeDtypeStruct((6, n, d), BF16),
        compiler_params=_cparams("parallel"),
        name="rwkv_mix",
    )(x, x, x, g.reshape(1, d), mods, mix)


def _head_sum(x, bd):
    return _dot(x.astype(BF16), bd)


def _rwfeat_kernel(k_ref, v_ref, tl_ref, w2_ref, a2_ref, g2_ref, v2_ref, pv_ref, bd_ref,
                   *rest, has_vlora):
    if has_vlora:
        vf_ref, lw_ref, kd_ref, as_ref, kk_ref, g_ref, vo_ref = rest
    else:
        lw_ref, kd_ref, as_ref, kk_ref, g_ref = rest
    k = k_ref[...]
    tw = tl_ref[0]
    ta = tl_ref[1]
    tg = tl_ref[2]
    w0 = pv_ref[0:2, :]
    a0 = pv_ref[2:4, :]
    k_k = pv_ref[5:6, :]
    k_a = pv_ref[6:7, :]
    for z in range(2):
        sl = slice(LORA_PAD * z, LORA_PAD * (z + 1))
        lora_w = _dot(tw[:, sl], w2_ref[z])
        w = -_softplus(-(w0[z:z + 1, :] + lora_w)) - 0.5
        lw_ref[z] = -jnp.exp(w)
        a_sig = _sigmoid(a0[z:z + 1, :] + _dot(ta[:, sl], a2_ref[z]))
        as_ref[z] = a_sig.astype(BF16)
        kd_ref[z] = k * (1.0 + (a_sig - 1.0) * k_a)
    g_ref[...] = _dot(tg, g2_ref[...]).astype(BF16)
    kkr = k * k_k
    ss = _head_sum(kkr * kkr, bd_ref[...])
    kk_ref[...] = (kkr * lax.rsqrt(ss + 1e-12)).astype(BF16)
    if has_vlora:
        v = v_ref[...]
        tv = tl_ref[3]
        gate = _sigmoid(pv_ref[4:5, :] + _dot(tv[:, 0:LORA_PAD], v2_ref[...]))
        vo_ref[...] = v + (vf_ref[...] - v) * gate


def _rwkv_feat(geo, rkv, tl, w2p, a2p, g2, v2p, pvec, vfirst):
    _, n, d = rkv.shape
    tm, tc = geo.tm, COL_TILE
    has_vlora = vfirst is not None
    lt = tl.shape[2]
    ii = lax.broadcasted_iota(jnp.int32, (tc, tc), 0) // RW_HEAD
    jj = lax.broadcasted_iota(jnp.int32, (tc, tc), 1) // RW_HEAD
    bd = (ii == jj).astype(BF16)
    row = lambda i, j: (i, j)
    in_specs = [
        pl.BlockSpec((None, tm, tc), lambda i, j: (1, i, j)),
        pl.BlockSpec((None, tm, tc), lambda i, j: (2, i, j)),
        pl.BlockSpec((4, tm, lt), lambda i, j: (0, i, 0)),
        pl.BlockSpec((2, LORA_PAD, tc), lambda i, j: (0, 0, j)),
        pl.BlockSpec((2, LORA_PAD, tc), lambda i, j: (0, 0, j)),
        pl.BlockSpec((lt, tc), lambda i, j: (0, j)),
        pl.BlockSpec((LORA_PAD, tc), lambda i, j: (0, j)),
        pl.BlockSpec((SUBLANES, tc), lambda i, j: (0, j)),
        pl.BlockSpec((tc, tc), lambda i, j: (0, 0)),
    ]
    args = [rkv, rkv, tl, w2p, a2p, g2, v2p, pvec, bd]
    dir_spec = pl.BlockSpec((2, tm, tc), lambda i, j: (0, i, j))
    out_specs = [dir_spec, dir_spec, dir_spec, pl.BlockSpec((tm, tc), row), pl.BlockSpec((tm, tc), row)]
    out_shape = [jax.ShapeDtypeStruct((2, n, d), dt) for dt in (F32, F32, BF16)]
    out_shape += [jax.ShapeDtypeStruct((n, d), BF16)] * 2
    if has_vlora:
        vf_arr, vf_spec = _plane(vfirst, (tm, tc), row)
        in_specs.append(vf_spec)
        args.append(vf_arr)
        out_specs.append(pl.BlockSpec((tm, tc), row))
        out_shape.append(jax.ShapeDtypeStruct((n, d), F32))
    res = pl.pallas_call(
        functools.partial(_rwfeat_kernel, has_vlora=has_vlora),
        grid=(geo.ntiles, d // tc),
        in_specs=in_specs,
        out_specs=out_specs,
        out_shape=out_shape,
        compiler_params=_cparams("parallel", "parallel"),
        name="rwkv_feat",
    )(*args)
    if has_vlora:
        lw, kd, asg, kk, g, v = res
    else:
        lw, kd, asg, kk, g = res
        v = (rkv, 2)
    return lw, kd, asg, kk, g, v


def _wkv_chunk_kernel(r_ref, v_ref, kk_ref, lw_ref, kd_ref, as_ref,
                      m_ref, ga_ref, rq_ref, o0_ref, pc_ref, *, npairs):
    c = WKV_CHUNK
    c2 = 2 * c
    sgn = 1 - 2 * pl.program_id(0)
    ri = lax.broadcasted_iota(jnp.int32, (c2, c2), 0)
    ci = lax.broadcasted_iota(jnp.int32, (c2, c2), 1)
    same = (ri >= c) == (ci >= c)
    tt = jnp.where(ri >= c, ri - c, ri)
    ss = jnp.where(ci >= c, ci - c, ci)
    earlier = (ss - tt) * sgn < 0
    strict = jnp.logical_and(same, earlier)
    incl = jnp.logical_and(same, jnp.logical_or(earlier, ss == tt))
    eye = (ri == ci).astype(F32)
    r64 = lax.broadcasted_iota(jnp.int32, (c, c), 0)
    c64 = lax.broadcasted_iota(jnp.int32, (c, c), 1)
    ltri = jnp.where((c64 - r64) * sgn <= 0, 1.0, 0.0).astype(BF16)
    head0 = lax.broadcasted_iota(jnp.int32, (c, LANES), 1) < RW_HEAD
    pairs = range(npairs)

    def stack(x):
        return jnp.concatenate([jnp.where(head0, x, 0.0), jnp.where(head0, 0.0, x)], axis=0)

    def dup(x):
        return jnp.concatenate([x, x], axis=0)

    def bf(x):
        return x.astype(BF16)

    lhs, rhs, a2, bp2, kp2, vst, r2 = [], [], [], [], [], [], []
    lw_all = lw_ref[...]
    cum_all = _dot_exact_lhs(ltri, lw_all)
    for p in pairs:
        sl = slice(LANES * p, LANES * (p + 1))
        lw = lw_all[:, sl]
        cum = cum_all[:, sl]
        tot = jnp.sum(lw, axis=0, keepdims=True)
        p_inv = jnp.exp(-cum)
        p_end = jnp.exp(tot - cum)
        kk = kk_ref[:, sl]
        b = kk * as_ref[:, sl]
        kd = kd_ref[:, sl]
        a2p = stack(-kk * jnp.exp(cum - lw))
        r2p = stack(r_ref[:, sl] * jnp.exp(cum))
        a2.append(bf(a2p))
        r2.append(r2p)
        lhs.append(jnp.concatenate([a2[p], bf(r2p)], axis=0))
        rhs.append(jnp.concatenate([dup(bf(b * p_inv)), dup(bf(kd * p_inv))], axis=0))
        bp2.append(bf(stack(b * p_end)))
        kp2.append(bf(stack(kd * p_end)))
        vst.append(bf(stack(v_ref[:, sl])))
        pc_ref[:, sl] = jnp.broadcast_to(jnp.exp(tot), (SUBLANES, LANES))
    gram = [_dot_nt(lhs[p], rhs[p]) for p in pairs]
    nab = [jnp.where(strict, gram[p][0:c2, 0:c2], 0.0) for p in pairs]
    nrb = [bf(jnp.where(incl, gram[p][c2:2 * c2, 0:c2], 0.0)) for p in pairs]
    nk = [bf(jnp.concatenate([jnp.where(strict, gram[p][0:c2, c2:2 * c2], 0.0),
                              jnp.where(incl, gram[p][c2:2 * c2, c2:2 * c2], 0.0)], axis=0))
          for p in pairs]
    quads = range(npairs // 2)

    def side(x0, x1):
        return jnp.concatenate([x0, x1], axis=1)

    def diag(x0, x1):
        z0 = jnp.zeros_like(x0)
        return jnp.concatenate([side(x0, z0), side(z0, x1)], axis=0)

    def diag_halves(x):
        return diag(x[:, 0:c2], x[:, c2:2 * c2])

    def unside(xs):
        return [xs[p // 2][:, c2 * (p % 2):c2 * (p % 2 + 1)] for p in pairs]

    xv = unside([_dot(side(nk[2 * q], nk[2 * q + 1]), diag(vst[2 * q], vst[2 * q + 1]))
                 for q in quads])
    tinv = [side(eye + nab[2 * q], eye + nab[2 * q + 1]) for q in quads]
    npow = [bf(side(nab[2 * q], nab[2 * q + 1])) for q in quads]
    for _ in range(int(math.log2(c)) - 1):
        npow = [bf(_dot(npow[q], diag_halves(npow[q]))) for q in quads]
        tinv = [tinv[q] + _dot(bf(tinv[q]), diag_halves(npow[q])) for q in quads]
    tinv = unside(tinv)
    y = [_dot(bf(tinv[p]), jnp.concatenate([a2[p], bf(xv[p][0:c2, :])], axis=1)) for p in pairs]
    yb = [bf(y[p]) for p in pairs]
    z = [_dot(nrb[p], yb[p]) for p in pairs]
    mg = unside([_dot(bf(side(y[2 * q].T, y[2 * q + 1].T)), diag(bp2[2 * q], bp2[2 * q + 1]))
                 for q in quads])
    vk = unside([_dot(bf(side(vst[2 * q].astype(F32).T, vst[2 * q + 1].astype(F32).T)),
                      diag(kp2[2 * q], kp2[2 * q + 1])) for q in quads])
    for p in pairs:
        sl = slice(LANES * p, LANES * (p + 1))
        m_ref[:, sl] = bf(mg[p][0:c2, :])
        ga_ref[:, sl] = bf(mg[p][c2:2 * c2, :] + vk[p])
        rq_ref[:, sl] = bf(r2[p] + z[p][:, 0:c2])
        o0 = z[p][:, c2:2 * c2] + xv[p][c2:2 * c2, :]
        o0_ref[:, sl] = bf(o0[0:c, :] + o0[c:c2, :])


def _wkv_scan_kernel(m_ref, ga_ref, rq_ref, o0_ref, pc_ref, o_ref, g_scr, *, npairs):
    c = WKV_CHUNK

    @pl.when(pl.program_id(2) == 0)
    def _():
        g_scr[...] = jnp.zeros_like(g_scr)

    for p in range(npairs):
        sl = slice(LANES * p, LANES * (p + 1))
        g = g_scr[p]
        g_hi = g.astype(BF16)
        o_st = _dot_nt(rq_ref[:, sl], g_hi)
        o_ref[:, sl] = o_st[0:c, :] + o_st[c:2 * c, :] + o0_ref[:, sl]
        m = m_ref[:, sl]
        g_scr[p] = g * pc_ref[0:1, sl] + _dot(g_hi, m) + ga_ref[:, sl]


WKV_CHUNK_LANES = 2048


def _plane(x, block, imap):
    if not isinstance(x, tuple):
        return x, pl.BlockSpec(block, imap)
    stack, k = x
    return stack, pl.BlockSpec((None,) + block, lambda *ids: (k,) + tuple(imap(*ids)))


def _wkv_bidir(r, v, kk, lw, kd, asg, batch, ctx_len):
    n, d = kk.shape
    c = WKV_CHUNK
    s = n // batch
    ncs = s // c
    ncc = ctx_len // c
    nch = n // c
    lanes = min(WKV_CHUNK_LANES, d)
    ngrp = d // lanes
    rowmap = lambda z, i, j: (i, j)
    r, r_spec = _plane(r, (c, lanes), rowmap)
    v, v_spec = _plane(v, (c, lanes), rowmap)
    perdir = pl.BlockSpec((None, c, lanes), lambda z, i, j: (z, i, j))
    big = pl.BlockSpec((None, 2 * c, lanes), lambda z, i, j: (z, i, j))
    m_, ga_, rq_, o0_, pc_ = pl.pallas_call(
        functools.partial(_wkv_chunk_kernel, npairs=lanes // LANES),
        grid=(2, nch, ngrp),
        in_specs=[r_spec, v_spec, pl.BlockSpec((c, lanes), rowmap), perdir, perdir, perdir],
        out_specs=[big, big, big, perdir,
                   pl.BlockSpec((None, SUBLANES, lanes), lambda z, i, j: (z, i, j))],
        out_shape=[jax.ShapeDtypeStruct((2, nch * 2 * c, d), BF16)] * 3
        + [jax.ShapeDtypeStruct((2, n, d), BF16),
           jax.ShapeDtypeStruct((2, nch * SUBLANES, d), F32)],
        compiler_params=_cparams("parallel", "parallel", "parallel"),
        name="wkv_chunk",
    )(r, v, kk, lw, kd, asg)

    def cmap(z, b, cc):
        back = jnp.where(cc < ncc, ncc - 1 - cc, ncs + ncc - 1 - cc)
        return (z, b * ncs + jnp.where(z == 0, cc, back), 0)

    out = pl.pallas_call(
        functools.partial(_wkv_scan_kernel, npairs=d // LANES),
        grid=(2, batch, ncs),
        in_specs=[
            pl.BlockSpec((None, 2 * c, d), cmap),
            pl.BlockSpec((None, 2 * c, d), cmap),
            pl.BlockSpec((None, 2 * c, d), cmap),
            pl.BlockSpec((None, c, d), cmap),
            pl.BlockSpec((None, SUBLANES, d), cmap),
        ],
        out_specs=pl.BlockSpec((None, c, d), cmap),
        out_shape=jax.ShapeDtypeStruct((2, n, d), F32),
        scratch_shapes=[pltpu.VMEM((d // LANES, 2 * c, 2 * c), F32)],
        compiler_params=_cparams("parallel", "parallel", "arbitrary"),
        name="wkv_scan",
    )(m_, ga_, rq_, o0_, pc_)
    return out


def _rwread_kernel(o_ref, r_ref, kd_ref, v_ref, g_ref, pv_ref, bd_ref, y_ref):
    bd = bd_ref[...]
    wkv = o_ref[0] + o_ref[1]
    inv = 1.0 / RW_HEAD
    mu = _head_sum(wkv, bd) * inv
    dev = wkv - mu
    var = _head_sum(dev * dev, bd) * inv
    y = dev * lax.rsqrt(var + GN_EPS) * pv_ref[1:2, :] + pv_ref[2:3, :]
    rk = r_ref[...] * (kd_ref[0] + kd_ref[1]) * pv_ref[0:1, :]
    y = y + _head_sum(rk, bd) * v_ref[...]
    y_ref[...] = (y * g_ref[...]).astype(BF16)


def _rwkv_readout(geo, wkv, r, kd, v, g, pvec):
    n, d = g.shape
    tm, tc = geo.tm, COL_TILE
    ii = lax.broadcasted_iota(jnp.int32, (tc, tc), 0) // RW_HEAD
    jj = lax.broadcasted_iota(jnp.int32, (tc, tc), 1) // RW_HEAD
    bd = (ii == jj).astype(BF16)
    row = pl.BlockSpec((tm, tc), lambda i, j: (i, j))
    dirs = pl.BlockSpec((2, tm, tc), lambda i, j: (0, i, j))
    r, r_spec = _plane(r, (tm, tc), lambda i, j: (i, j))
    v, v_spec = _plane(v, (tm, tc), lambda i, j: (i, j))
    return pl.pallas_call(
        _rwread_kernel,
        grid=(geo.ntiles, d // tc),
        in_specs=[dirs, r_spec, dirs, v_spec, row,
                  pl.BlockSpec((SUBLANES, tc), lambda i, j: (0, j)),
                  pl.BlockSpec((tc, tc), lambda i, j: (0, 0))],
        out_specs=row,
        out_shape=jax.ShapeDtypeStruct((n, d), BF16),
        compiler_params=_cparams("parallel", "parallel"),
        name="rwkv_readout",
    )(wkv, r, kd, v, g, pvec, bd)


def _mlaqkv_kernel(z_ref, qn_ref, kvn_ref, wuq_ref, wukv_ref, gq_ref, gk_ref, cos_ref, sin_ref,
                   q_ref, k_ref, v_ref, *, q_lora, kv_lora):
    z = z_ref[...]
    cq = z[:, 0:q_lora]
    ckv = z[:, q_lora:q_lora + kv_lora]
    krot = z[:, q_lora + kv_lora:q_lora + kv_lora + LANES]

    def rms(x, g):
        ms = jnp.mean(x * x, axis=-1, keepdims=True)
        return x * lax.rsqrt(ms + EPS) * g

    qf = _dot(rms(cq, qn_ref[...]).astype(BF16), wuq_ref[...])
    kvf = _dot(rms(ckv, kvn_ref[...]).astype(BF16), wukv_ref[...])
    cos = cos_ref[...]
    sin = sin_ref[...]
    half = QK_ROPE // 2
    lane = lax.broadcasted_iota(jnp.int32, cos.shape, 1)

    def rope(x):
        up = pltpu.roll(x, LANES - half, axis=1)
        dn = pltpu.roll(x, half, axis=1)
        return x * cos + jnp.where(lane < half, up, dn) * sin

    inv_w = 1.0 / (QK_NOPE + QK_ROPE)
    gq = gq_ref[...]
    gk = gk_ref[...]
    kr_ss = jnp.sum(krot * krot, axis=-1, keepdims=True)
    for h in range(MLA_HEADS):
        o = QK_PAD * h
        qh = qf[:, o:o + QK_PAD]
        rs = lax.rsqrt(jnp.sum(qh * qh, axis=-1, keepdims=True) * inv_w + EPS)
        qn = qh * rs * gq * (ATTN_SCALE * math.log2(math.e))
        q_ref[:, o:o + QK_NOPE] = qn[:, 0:QK_NOPE].astype(BF16)
        q_ref[:, o + QK_NOPE:o + QK_PAD] = rope(qn[:, QK_NOPE:QK_PAD]).astype(BF16)
        kn = kvf[:, o:o + QK_NOPE]
        rsk = lax.rsqrt((jnp.sum(kn * kn, axis=-1, keepdims=True) + kr_ss) * inv_w + EPS)
        k_ref[:, o:o + QK_NOPE] = (kn * rsk * gk[:, 0:QK_NOPE]).astype(BF16)
        k_ref[:, o + QK_NOPE:o + QK_PAD] = rope(krot * rsk * gk[:, QK_NOPE:QK_PAD]).astype(BF16)
        v_ref[:, V_HEAD * h:V_HEAD * (h + 1)] = kvf[:, o + QK_NOPE:o + QK_PAD].astype(BF16)


def _mla_qkv(geo, z, qn, kvn, wuq_p, wukv, gq_p, gk_p, cos_t, sin_t):
    n, zw = z.shape
    tm = geo.tm
    q_lora, kv_lora = qn.shape[0], kvn.shape[0]
    hq = MLA_HEADS * QK_PAD
    tpb = geo.tpb
    full = lambda i: (0, 0)
    rowmap = lambda i: (i, 0)
    return pl.pallas_call(
        functools.partial(_mlaqkv_kernel, q_lora=q_lora, kv_lora=kv_lora),
        grid=(geo.ntiles,),
        in_specs=[
            pl.BlockSpec((tm, zw), rowmap),
            pl.BlockSpec((1, q_lora), full),
            pl.BlockSpec((1, kv_lora), full),
            pl.BlockSpec((q_lora, hq), full),
            pl.BlockSpec((kv_lora, hq), full),
            pl.BlockSpec((1, QK_PAD), full),
            pl.BlockSpec((1, QK_PAD), full),
            pl.BlockSpec((tm, LANES), lambda i: (i % tpb, 0)),
            pl.BlockSpec((tm, LANES), lambda i: (i % tpb, 0)),
        ],
        out_specs=[pl.BlockSpec((tm, hq), rowmap), pl.BlockSpec((tm, hq), rowmap),
                   pl.BlockSpec((tm, MLA_HEADS * V_HEAD), rowmap)],
        out_shape=[jax.ShapeDtypeStruct((n, hq), BF16), jax.ShapeDtypeStruct((n, hq), BF16),
                   jax.ShapeDtypeStruct((n, MLA_HEADS * V_HEAD), BF16)],
        compiler_params=_cparams("parallel"),
        name="mla_qkv",
    )(z, qn.reshape(1, -1), kvn.reshape(1, -1), wuq_p, wukv, gq_p, gk_p, cos_t, sin_t)


ATTN_SUBTILES = 2


def _attn_kernel(q_ref, k_ref, vt_ref, o_ref):
    k = k_ref[...]
    vt = vt_ref[...]
    cols = q_ref.shape[0] // ATTN_SUBTILES
    subs = range(ATTN_SUBTILES)
    s = [_dot_nt(k, q_ref[cols * i:cols * (i + 1), :]) for i in subs]
    m = [jnp.max(s[i], axis=0, keepdims=True) for i in subs]
    p = [jnp.exp2(s[i] - m[i]) for i in subs]
    l = [jnp.sum(p[i], axis=0, keepdims=True) for i in subs]
    for i in subs:
        ot = _dot(vt, p[i].astype(BF16)) / l[i]
        o_ref[cols * i:cols * (i + 1), :] = ot.T.astype(BF16)


def _attn_call(q3, k3, vt4, tq, nkeys):
    b, sq, _ = q3.shape
    return pl.pallas_call(
        _attn_kernel,
        grid=(b, MLA_HEADS, sq // tq),
        in_specs=[
            pl.BlockSpec((None, tq, QK_PAD), lambda bb, h, i: (bb, i, h)),
            pl.BlockSpec((None, nkeys, QK_PAD), lambda bb, h, i: (bb, 0, h)),
            pl.BlockSpec((None, None, V_HEAD, nkeys), lambda bb, h, i: (bb, h, 0, 0)),
        ],
        out_specs=pl.BlockSpec((None, tq, V_HEAD), lambda bb, h, i: (bb, i, h)),
        out_shape=jax.ShapeDtypeStruct((b, sq, MLA_HEADS * V_HEAD), BF16),
        compiler_params=_cparams("parallel", "parallel", "parallel"),
        name="mla_attention",
    )(q3, k3, vt4)


def _mla_attention(geo, q, k, v):
    b, s, l = geo.b, geo.s, geo.l
    q3 = q.reshape(b, s, MLA_HEADS * QK_PAD)
    k3 = k.reshape(b, s, MLA_HEADS * QK_PAD)
    vt4 = v.reshape(b, s, MLA_HEADS, V_HEAD).transpose(0, 2, 3, 1)
    o_ctx = _attn_call(q3[:, :l], k3, vt4, geo.tm, l)
    tq_lat = 2 * geo.tm if geo.t % (2 * geo.tm) == 0 else geo.tm
    o_lat = _attn_call(q3[:, l:], k3, vt4, tq_lat, s)
    return jnp.concatenate([o_ctx, o_lat], axis=1).reshape(b * s, MLA_HEADS * V_HEAD)


def _topk_rows(s, k, payload=None):
    rows = lax.broadcasted_iota(jnp.int32, s.shape, 0).astype(F32)
    big = float(s.shape[0])
    vals, idxs = [], []
    for _ in range(k):
        m = jnp.max(s, axis=0, keepdims=True)
        idx = jnp.min(jnp.where(s == m, rows, big), axis=0, keepdims=True)
        hit = rows == idx
        vals.append(m)
        if payload is None:
            idxs.append(idx)
        else:
            idxs.append(jnp.sum(jnp.where(hit, payload, 0.0), axis=0, keepdims=True))
        s = jnp.where(hit, -jnp.inf, s)
    return jnp.concatenate(vals, axis=0), jnp.concatenate(idxs, axis=0)


def _peersel_kernel(q_ref, qn_ref, keys_ref, e_ref, g_ref):
    q = q_ref[...]
    ms = jnp.mean(q * q, axis=-1, keepdims=True)
    qn = q * lax.rsqrt(ms + EPS) * qn_ref[...]
    half = D_KEY // 2
    s1 = _dot3_nt(keys_ref[0], qn[:, 0:half])
    s2 = _dot3_nt(keys_ref[1], qn[:, half:D_KEY])
    t1, i1 = _topk_rows(s1, PEER_TOPK)
    t2, i2 = _topk_rows(s2, PEER_TOPK)
    k = PEER_TOPK
    sub = lax.broadcasted_iota(jnp.int32, (SUBLANES, t1.shape[1]), 0)
    cand = [t1[0:1, :] + t2]
    cidx = [i1[0:1, :] * float(N_KEYS) + i2]
    for p in range(1, k // 2):
        live = sub < k // (p + 1)
        cand.append(jnp.where(live, t1[p:p + 1, :] + t2[0:SUBLANES, :], -jnp.inf))
        cidx.append(i1[p:p + 1, :] * float(N_KEYS) + i2[0:SUBLANES, :])
    cand.append(t1[k // 2:k, :] + t2[0:1, :])
    cidx.append(i1[k // 2:k, :] * float(N_KEYS) + i2[0:1, :])
    best, eidx = _topk_rows(jnp.concatenate(cand, axis=0), k, payload=jnp.concatenate(cidx, axis=0))
    ex = jnp.exp(best - jnp.max(best, axis=0, keepdims=True))
    g_ref[...] = ex / jnp.sum(ex, axis=0, keepdims=True)
    e_ref[...] = eidx.astype(jnp.int32)


def _peer_select(qp, q_norm, keys, tm, row0, n):
    assert row0 % tm == 0 and n % tm == 0
    tile0 = row0 // tm
    return pl.pallas_call(
        _peersel_kernel,
        grid=(n // tm, PEER_HEADS),
        in_specs=[
            pl.BlockSpec((tm, D_KEY), lambda i, h: (tile0 + i, h)),
            pl.BlockSpec((1, D_KEY), lambda i, h: (0, 0)),
            pl.BlockSpec((2, N_KEYS, D_KEY // 2), lambda i, h: (0, 0, 0)),
        ],
        out_specs=[pl.BlockSpec((PEER_TOPK, tm), lambda i, h: (h, i)),
                   pl.BlockSpec((PEER_TOPK, tm), lambda i, h: (h, i))],
        out_shape=[jax.ShapeDtypeStruct((PEER_HEADS * PEER_TOPK, n), jnp.int32),
                   jax.ShapeDtypeStruct((PEER_HEADS * PEER_TOPK, n), F32)],
        compiler_params=_cparams("parallel", "parallel"),
        name="peer_select",
    )(qp, q_norm.reshape(1, D_KEY), keys)


GATHER_SLOTS = 4
SLAB_PAD = 1


def _pack_expert_table(u, v):
    ne, d = u.shape
    ub = lax.bitcast_convert_type(u.astype(BF16), jnp.uint16).astype(jnp.uint32)
    vb = lax.bitcast_convert_type(v.astype(BF16), jnp.uint16).astype(jnp.uint32)
    return ((vb << 16) | ub).reshape(ne, d // LANES, LANES)


def _peergather_kernel(idx_ref, idxn_ref, gate_ref, h_ref, x_ref, mod_ref, tab_ref, o_ref,
                       *scratch, tb):
    nsel = PEER_HEADS * PEER_TOPK
    nrow = h_ref.shape[1] // LANES
    pitch = nrow + SLAB_PAD
    ns = GATHER_SLOTS
    bufs, sem = scratch[:ns], scratch[ns]
    lane_t = lax.broadcasted_iota(jnp.int32, (nsel, tb), 1)
    g2 = mod_ref[5:6, :]
    step = pl.program_id(0)
    nsteps = pl.num_programs(0)

    def row_copy(ids_ref, t, j, slot):
        return pltpu.make_async_copy(
            tab_ref.at[ids_ref[t, j]],
            bufs[slot].at[pl.ds(j * pitch, nrow), :],
            sem.at[slot])

    def issue(ids_ref, t, slot):
        for j in range(nsel):
            row_copy(ids_ref, t, j, slot).start(priority=j % 2)

    def wait(t, slot):
        for j in range(nsel):
            row_copy(idx_ref, t, j, slot).wait()

    def packed(slot, s):
        return bufs[slot][pl.ds(s, nsel, stride=pitch), :]

    def compute(t, slot):
        hrow = h_ref[pl.ds(t, 1), :]
        acc = jnp.zeros((nsel, LANES), F32)
        for s in range(nrow):
            u = lax.bitcast_convert_type(packed(slot, s) << 16, F32)
            acc = acc + u * hrow[:, LANES * s:LANES * (s + 1)]
        dots = jnp.sum(acc, axis=-1, keepdims=True)
        gcol = jnp.sum(jnp.where(lane_t == t, gate_ref[...], 0.0), axis=-1, keepdims=True)
        coef = gcol * _gelu(dots)
        outs = []
        for s in range(nrow):
            vv = lax.bitcast_convert_type(packed(slot, s) & jnp.uint32(0xFFFF0000), F32)
            outs.append(jnp.sum(coef * vv, axis=0, keepdims=True))
        orow = jnp.concatenate(outs, axis=1)
        o_ref[pl.ds(t, 1), :] = x_ref[pl.ds(t, 1), :] + g2 * orow

    @pl.when(step == 0)
    def _():
        for s in range(ns - 1):
            issue(idx_ref, s, s)

    ngroups = tb // ns

    def body(g, carry):
        for s in range(ns):
            t = g * ns + s
            wait(t, s)
            issue(idx_ref, t + ns - 1, (s + ns - 1) % ns)
            compute(t, s)
        return carry

    lax.fori_loop(0, ngroups - 1, body, 0)
    for s in range(ns):
        t = (ngroups - 1) * ns + s
        wait(t, s)
        if s == 0:
            issue(idx_ref, tb - 1, ns - 1)
        else:
            @pl.when(step < nsteps - 1)
            def _():
                issue(idxn_ref, s - 1, s - 1)
        compute(t, s)


def _peer_gather(geo_g, eidx, gate_t, h2, x, mods, table, n):
    d = x.shape[1]
    tb = geo_g.tm
    nsel = PEER_HEADS * PEER_TOPK
    pitch = d // LANES + SLAB_PAD
    assert n % tb == 0
    nsteps = n // tb
    return pl.pallas_call(
        functools.partial(_peergather_kernel, tb=tb),
        grid=(nsteps,),
        in_specs=[
            pl.BlockSpec((tb, nsel), lambda i: (i, 0), memory_space=pltpu.SMEM),
            pl.BlockSpec((tb, nsel), lambda i: (jnp.minimum(i + 1, nsteps - 1), 0),
                         memory_space=pltpu.SMEM),
            pl.BlockSpec((nsel, tb), lambda i: (0, i)),
            pl.BlockSpec((tb, d), lambda i: (i, 0)),
            pl.BlockSpec((tb, d), lambda i: (i, 0)),
            geo_g.mod_spec(),
            pl.BlockSpec(memory_space=pl.ANY),
        ],
        out_specs=pl.BlockSpec((tb, d), lambda i: (i, 0)),
        out_shape=jax.ShapeDtypeStruct((n, d), F32),
        scratch_shapes=[pltpu.VMEM((nsel * pitch, LANES), jnp.uint32)] * GATHER_SLOTS
        + [pltpu.SemaphoreType.DMA((GATHER_SLOTS,))],
        compiler_params=_cparams("arbitrary"),
        name="peer_gather",
    )(eidx, eidx, gate_t, h2, x, mods, table)


SC_CORES = 2
SC_SUBCORES = 16
SC_LANES = 16
SC_CHUNK = 16
SC_SHARE = 0.3235

_ERF_ALPHA = (0.00022905065861350646, 0.0034082910107109506, 0.050955695062380861,
              0.18520832239976145, 1.128379143519084)
_ERF_BETA = (-1.1791602954361697e-7, 0.000023547966471313185, 0.0010179625278914885,
             0.014070470171167667, 0.11098505178285362, 0.49746925110067538, 1.0)
_ERF_CLAMP = 3.832506856900711


def _erf_rational(x):
    x = jnp.minimum(jnp.maximum(x, -_ERF_CLAMP), _ERF_CLAMP)
    x2 = x * x
    p = jnp.full_like(x, _ERF_ALPHA[0])
    for c in _ERF_ALPHA[1:]:
        p = p * x2 + c
    q = jnp.full_like(x, _ERF_BETA[0])
    for c in _ERF_BETA[1:]:
        q = q * x2 + c
    return x * p / q


def _peer_sc(eidx, gate, h, table, row0):
    n, nsel = eidx.shape
    d = h.shape[1]
    nw = SC_CORES * SC_SUBCORES
    assert n % (2 * nw) == 0 and nsel % SC_CHUNK == 0 and d % SC_LANES == 0
    tpw = n // nw
    nchunk = nsel // SC_CHUNK
    nvec = d // SC_LANES
    mesh = plsc.VectorSubcoreMesh(core_axis_name="c", subcore_axis_name="s",
                                  num_cores=SC_CORES, num_subcores=SC_SUBCORES)

    @functools.partial(
        pl.kernel, mesh=mesh,
        out_type=jax.ShapeDtypeStruct((n, d), F32),
        scratch_types=[
            pltpu.VMEM((nsel,), jnp.int32), pltpu.VMEM((nsel,), jnp.int32),
            pltpu.VMEM((nsel,), F32), pltpu.VMEM((nsel,), F32),
            pltpu.VMEM((d,), F32), pltpu.VMEM((d,), F32),
            pltpu.VMEM((d,), F32),
            pltpu.VMEM((SC_CHUNK, d), jnp.uint32),
            pltpu.VMEM((SC_CHUNK, d), jnp.uint32),
            pltpu.SemaphoreType.DMA, pltpu.SemaphoreType.DMA,
            pltpu.SemaphoreType.DMA, pltpu.SemaphoreType.DMA,
        ],
        compiler_params=pltpu.CompilerParams(needs_layout_passes=False),
        name="peer_sc",
    )
    def sc_kernel(eidx_hbm, gate_hbm, h_hbm, tab_hbm, y_hbm, idx_a, idx_b, gate_a, gate_b, h_a, h_b,
                  o_v, rows0, rows1, sem0, sem1, msem_a, msem_b):
        wid = lax.axis_index("s") * SC_CORES + lax.axis_index("c")
        lanes = lax.iota(jnp.int32, SC_LANES)
        zero = jnp.zeros((SC_LANES,), F32)
        bufs = ((rows0, sem0), (rows1, sem1))
        meta = ((idx_a, gate_a, h_a, msem_a), (idx_b, gate_b, h_b, msem_b))

        def meta_copies(tok, s):
            idx_v, gate_v, h_v, msem = meta[s]
            return (pltpu.make_async_copy(eidx_hbm.at[tok], idx_v, msem),
                    pltpu.make_async_copy(gate_hbm.at[tok], gate_v, msem),
                    pltpu.make_async_copy(h_hbm.at[row0 + tok], h_v, msem))

        def gather(s, c, slot):
            rows, sem = bufs[slot]
            return pltpu.make_async_copy(
                tab_hbm.at[meta[s][0].at[pl.ds(c * SC_CHUNK, SC_CHUNK)]], rows, sem)

        def chunk_compute(s, c, rows):
            _, gate_v, h_v, _ = meta[s]

            def dot_body(i, accs):
                hv = h_v[pl.ds(i * SC_LANES, SC_LANES)]
                out = []
                for e in range(SC_CHUNK):
                    w = rows[e, pl.ds(i * SC_LANES, SC_LANES)]
                    out.append(accs[e] + lax.bitcast_convert_type(w << 16, F32) * hv)
                return tuple(out)

            accs = lax.fori_loop(0, nvec, dot_body, (zero,) * SC_CHUNK)
            dots = zero
            for e in range(SC_CHUNK):
                dots = jnp.where(lanes == e, jnp.sum(accs[e]), dots)
            act = 0.5 * dots * (1.0 + _erf_rational(dots * (2.0 ** -0.5)))
            coef = gate_v[pl.ds(c * SC_CHUNK, SC_CHUNK)] * act
            splat = [jnp.full((SC_LANES,), jnp.sum(jnp.where(lanes == e, coef, 0.0)), F32)
                     for e in range(SC_CHUNK)]

            def acc_body(i, carry):
                o = o_v[pl.ds(i * SC_LANES, SC_LANES)]
                for e in range(SC_CHUNK):
                    w = rows[e, pl.ds(i * SC_LANES, SC_LANES)]
                    o = o + splat[e] * lax.bitcast_convert_type(w & jnp.uint32(0xFFFF0000), F32)
                o_v[pl.ds(i * SC_LANES, SC_LANES)] = o
                return carry

            lax.fori_loop(0, nvec, acc_body, 0)

        def zero_body(i, c2):
            o_v[pl.ds(i * SC_LANES, SC_LANES)] = zero
            return c2

        def token(tok, s, has_next):
            def when_next(fn):
                if isinstance(has_next, bool):
                    if has_next:
                        fn()
                else:
                    pl.when(has_next)(fn)

            def load_next():
                for cp in meta_copies(tok + 1, 1 - s):
                    cp.start()

            when_next(load_next)
            lax.fori_loop(0, nvec, zero_body, 0)
            for c in range(nchunk):
                if c + 1 < nchunk:
                    gather(s, c + 1, (c + 1) % 2).start()
                gather(s, c, c % 2).wait()
                chunk_compute(s, c, bufs[c % 2][0])

            def prefetch_next():
                for cp in meta_copies(tok + 1, 1 - s):
                    cp.wait()
                gather(1 - s, 0, 0).start()

            when_next(prefetch_next)
            pltpu.sync_copy(o_v, y_hbm.at[tok])

        base = wid * tpw
        for cp in meta_copies(base, 0):
            cp.start()
        for cp in meta_copies(base, 0):
            cp.wait()
        gather(0, 0, 0).start()

        def pair_body(g, carry):
            token(base + 2 * g, 0, True)
            token(base + 2 * g + 1, 1, g + 1 < tpw // 2)
            return carry

        lax.fori_loop(0, tpw // 2, pair_body, 0)

    return sc_kernel(eidx, gate, h, table)


def _pad_to(x, axis, size):
    pad = [(0, 0)] * x.ndim
    pad[axis] = (0, size - x.shape[axis])
    return jnp.pad(x, pad)


def _rwkv_layer(geo, xs, mods, norm1, mix, w_rkv, w_o, w0, w1, w2, a0, a1, a2, vl, g1, g2,
                k_k, k_a, r_k, ln_w, ln_b, vfirst):
    d = geo.d
    xm = _rwkv_mix(geo, xs, norm1, mods, mix)
    rkv = _bmm(xm, w_rkv.astype(BF16), (0, 2, 3), ("none",) * 3, F32, geo.tm)
    lt = g1.shape[1]
    w1c = jnp.concatenate([_pad_to(w1[0], 1, LORA_PAD), _pad_to(w1[1], 1, LORA_PAD)], axis=1)
    a1c = jnp.concatenate([_pad_to(a1[0], 1, LORA_PAD), _pad_to(a1[1], 1, LORA_PAD)], axis=1)
    if vl is None:
        v1p = jnp.zeros((d, lt), F32)
        v2p = jnp.zeros((LORA_PAD, d), F32)
        v0 = jnp.zeros((d,), F32)
    else:
        v0, v1, v2 = vl
        v1p = _pad_to(v1, 1, lt)
        v2p = _pad_to(v2, 0, LORA_PAD)
    wl1 = jnp.stack([_pad_to(w1c, 1, lt), _pad_to(a1c, 1, lt), g1, v1p]).astype(BF16)
    tl = _bmm(xm, wl1, (1, 4, 5, 3), ("tanh", "none", "sigmoid", "none"), BF16, geo.tm)
    w2p = jnp.stack([_pad_to(w2[0], 0, LORA_PAD), _pad_to(w2[1], 0, LORA_PAD)]).astype(BF16)
    a2p = jnp.stack([_pad_to(a2[0], 0, LORA_PAD), _pad_to(a2[1], 0, LORA_PAD)]).astype(BF16)
    pvec = jnp.stack([w0[0], w0[1], a0[0], a0[1], v0, k_k, k_a, jnp.zeros_like(k_k)])
    lw, kd, asg, kk, g, v = _rwkv_feat(geo, rkv, tl, w2p, a2p, g2.astype(BF16), v2p.astype(BF16),
                                      pvec, vfirst)
    r = (rkv, 0)
    wkv = _wkv_bidir(r, v, kk, lw, kd, asg, geo.b, geo.l)
    pv2 = _pad_to(jnp.stack([r_k, ln_w, ln_b]), 0, SUBLANES)
    y = _rwkv_readout(geo, wkv, r, kd, v, g, pv2)
    xs = _matmul_res(geo, y, w_o.astype(BF16), xs, mods, 2)
    return xs, v


def _rope_tables(geo):
    t = geo.t
    pos = jnp.arange(t)
    row = (pos // GRID_W).astype(F32)
    col = (pos % GRID_W).astype(F32)
    n_freq = QK_ROPE // 4
    inv_freq = ROPE_THETA ** (-jnp.arange(n_freq, dtype=F32) / n_freq)
    ang = jnp.concatenate([row[:, None] * inv_freq, col[:, None] * inv_freq], axis=-1)
    cos, sin = jnp.cos(ang), jnp.sin(ang)
    pad = LANES - QK_ROPE
    cos_l = jnp.concatenate([cos, cos, jnp.ones((t, pad), F32)], axis=1)
    sin_l = jnp.concatenate([-sin, sin, jnp.zeros((t, pad), F32)], axis=1)
    cos_c = jnp.ones((geo.l, LANES), F32)
    sin_c = jnp.zeros((geo.l, LANES), F32)
    return jnp.concatenate([cos_c, cos_l], axis=0), jnp.concatenate([sin_c, sin_l], axis=0)


def _mla_layer(geo, xs, mods, norm1, rope_t, w_in, q_norm, kv_norm, w_uq, w_ukv, g_q, g_k, w_o):
    q_lora, kv_lora = q_norm.shape[0], kv_norm.shape[0]
    zw = q_lora + kv_lora + LANES
    z = _mod_matmul(geo, xs, norm1, mods, _pad_to(w_in, 1, zw).astype(BF16), 0, False)
    qk = QK_NOPE + QK_ROPE
    wuq_p = _pad_to(w_uq.reshape(q_lora, MLA_HEADS, qk), 2, QK_PAD).reshape(q_lora, -1)
    gq_p = _pad_to(g_q, 0, QK_PAD).reshape(1, QK_PAD)
    gk_p = _pad_to(g_k, 0, QK_PAD).reshape(1, QK_PAD)
    q, k, v = _mla_qkv(geo, z, q_norm, kv_norm, wuq_p.astype(BF16), w_ukv.astype(BF16),
                       gq_p, gk_p, *rope_t)
    o = _mla_attention(geo, q, k, v)
    return _matmul_res(geo, o, w_o.astype(BF16), xs, mods, 2)


def _peer_layer(geo, geo_g, xs, mods, norm2, w_q, q_norm, keys, u, v):
    qp, h2 = _mod_matmul(geo, xs, norm2, mods, w_q.astype(BF16), 1, True)
    table = _pack_expert_table(u, v)
    n, d = xs.shape
    n_sc = geo.tm * round(SC_SHARE * n / geo.tm)
    if n_sc % (2 * SC_CORES * SC_SUBCORES) != 0:
        n_sc = 0
    n_tc = n - n_sc

    def select(row0, nrows):
        wide = 2 * geo.tm
        tsel = wide if row0 % wide == 0 and nrows % wide == 0 else geo.tm
        return _peer_select(qp, q_norm, keys, tsel, row0, nrows)

    if n_sc:
        eidx_sc, gate_sc = select(n_tc, n_sc)
        y_sc = _peer_sc(eidx_sc.T, gate_sc.T, h2, table.reshape(table.shape[0], d), n_tc)
    eidx_t, gate_t = select(0, n_tc)
    out_tc = _peer_gather(geo_g, eidx_t.T, gate_t, h2, xs, mods, table, n_tc)
    if n_sc == 0:
        return out_tc
    g2, r = [], n_tc
    while r < n:
        bi, pos = divmod(r, geo.s)
        seg = int(pos >= geo.l)
        stop = min(n, bi * geo.s + (geo.s if seg else geo.l))
        g2.append(jnp.broadcast_to(mods[bi, seg, 5], (stop - r, d)))
        r = stop
    out_sc = xs[n_tc:] + jnp.concatenate(g2, axis=0) * y_sc
    return jnp.concatenate([out_tc, out_sc], axis=0)


def kernel(x, c, ctx, c_ctx, w_ada, b_ada, norm1, norm2, rw_mix, rw_wrkv, rw_wo, rw_w0, rw_w1, rw_w2, rw_a0, rw_a1, rw_a2, rw_v0, rw_v1, rw_v2, rw_g1, rw_g2, rw_kk, rw_ka, rw_rk, rw_lnw, rw_lnb, mla_win, mla_qnorm, mla_kvnorm, mla_wuq, mla_wukv, mla_gq, mla_gk, mla_wo, peer_wq, peer_qnorm, peer_keys, peer_u, peer_v):
    b, t, d = x.shape
    l = ctx.shape[1]
    depth = w_ada.shape[0]
    geo = _Geom(b, l, t, d, min(ROW_TILE, l))
    geo_g = _Geom(b, l, t, d, min(GATHER_TILE, l))
    cond8 = _pad_to(jnp.concatenate([c, c_ctx[None, :]], axis=0), 0, SUBLANES)
    ada = _adaln(cond8, w_ada, b_ada).reshape(depth, SUBLANES, 6, d)
    mods_all = jnp.stack([jnp.broadcast_to(ada[:, b:b + 1], (depth, b, 6, d)), ada[:, 0:b]], axis=2)
    xs = jnp.concatenate([ctx, x], axis=1).reshape(b * (l + t), d)
    rope_t = _rope_tables(geo)
    vfirst = None
    for i in range(depth):
        j = i // 2
        mods = mods_all[i]
        if i % 2 == 0:
            vl = None if j == 0 else (rw_v0[j - 1], rw_v1[j - 1], rw_v2[j - 1])
            xs, vcur = _rwkv_layer(geo, xs, mods, norm1[i], rw_mix[j], rw_wrkv[j], rw_wo[j],
                                   rw_w0[j], rw_w1[j], rw_w2[j], rw_a0[j], rw_a1[j], rw_a2[j], vl,
                                   rw_g1[j], rw_g2[j], rw_kk[j], rw_ka[j], rw_rk[j], rw_lnw[j],
                                   rw_lnb[j], vfirst)
            if j == 0:
                vfirst = vcur
        else:
            xs = _mla_layer(geo, xs, mods, norm1[i], rope_t, mla_win[j], mla_qnorm[j],
                            mla_kvnorm[j], mla_wuq[j], mla_wukv[j], mla_gq[j], mla_gk[j], mla_wo[j])
        if i == depth - 1:
            xs = xs.reshape(b, l + t, d)[:, l:, :].reshape(b * t, d)
            geo, geo_g = _Geom(b, 0, t, d, geo.tm), _Geom(b, 0, t, d, geo_g.tm)
        xs = _peer_layer(geo, geo_g, xs, mods, norm2[i], peer_wq[i], peer_qnorm[i], peer_keys[i],
                         peer_u[i], peer_v[i])
    return xs.reshape(b, t, d)
```

```python
import functools
import math

import jax
import jax.numpy as jnp
from jax import lax
from jax.experimental import pallas as pl
from jax.experimental.pallas import tpu as pltpu
from jax.experimental.pallas import tpu_sc as plsc

F32 = jnp.float32
BF16 = jnp.bfloat16

EPS = 1e-6
GN_EPS = 64e-5
RW_HEAD = 64
WKV_CHUNK = 64
MLA_HEADS = 16
QK_NOPE = 128
QK_ROPE = 64
V_HEAD = 128
QK_PAD = 256
ROPE_THETA = 10000.0
GRID_W = 64
ATTN_SCALE = (QK_NOPE + QK_ROPE) ** -0.5
PEER_HEADS = 8
N_KEYS = 128
PEER_TOPK = 16
D_KEY = 256
LORA_PAD = 128

LANES = 128
SUBLANES = 8
VMEM_LIMIT = 56 * 1024 * 1024
ROW_TILE = 256
GATHER_TILE = 128
COL_TILE = 512
ADA_COLS = 1024


def _cparams(*sem):
    return pltpu.CompilerParams(dimension_semantics=sem, vmem_limit_bytes=VMEM_LIMIT)


def _dot(a, b):
    return jnp.dot(a, b, preferred_element_type=F32)


def _dot_nt(a, b):
    return lax.dot_general(a, b, (((1,), (1,)), ((), ())), preferred_element_type=F32)


def _split2(x):
    hi = x.astype(BF16)
    lo = (x - hi.astype(F32)).astype(BF16)
    return hi, lo


def _split3(x):
    hi = x.astype(BF16)
    r1 = x - hi.astype(F32)
    mid = r1.astype(BF16)
    lo = (r1 - mid.astype(F32)).astype(BF16)
    return hi, mid, lo


def _dot3(a, b):
    ah, al = _split2(a)
    bh, bl = _split2(b)
    return _dot(ah, bh) + (_dot(ah, bl) + _dot(al, bh))


def _dot3_nt(a, b):
    ah, al = _split2(a)
    bh, bl = _split2(b)
    return _dot_nt(ah, bh) + (_dot_nt(ah, bl) + _dot_nt(al, bh))


def _dot_exact_lhs(sel, x):
    hi, mid, lo = _split3(x)
    return _dot(sel, hi) + (_dot(sel, mid) + _dot(sel, lo))


def _modulate(x, g, shift, scale):
    ms = jnp.mean(x * x, axis=-1, keepdims=True)
    return (x * lax.rsqrt(ms + EPS) * g) * (1.0 + scale) + shift


def _sigmoid(x):
    return 1.0 / (1.0 + jnp.exp(-x))


def _softplus(y):
    return jnp.maximum(y, 0.0) + jnp.log(1.0 + jnp.exp(-jnp.abs(y)))


def _erf(x):
    return lax.erf(x)


def _gelu(x):
    return 0.5 * x * (1.0 + _erf(x * (2.0 ** -0.5)))


def _ada_kernel(s_ref, w_ref, b_ref, o_ref):
    s = s_ref[...]
    s = s * _sigmoid(s)
    o_ref[...] = _dot3(s, w_ref[...]) + b_ref[...]


def _adaln(cond8, w_ada, b_ada):
    depth, d, n = w_ada.shape
    tn = min(ADA_COLS, n)
    return pl.pallas_call(
        _ada_kernel,
        grid=(depth, n // tn),
        in_specs=[
            pl.BlockSpec((SUBLANES, d), lambda l, j: (0, 0)),
            pl.BlockSpec((None, d, tn), lambda l, j: (l, 0, j)),
            pl.BlockSpec((None, 1, tn), lambda l, j: (l, 0, j)),
        ],
        out_specs=pl.BlockSpec((None, SUBLANES, tn), lambda l, j: (l, 0, j)),
        out_shape=jax.ShapeDtypeStruct((depth, SUBLANES, n), F32),
        compiler_params=_cparams("parallel", "parallel"),
        name="adaln",
    )(cond8, w_ada, b_ada.reshape(depth, 1, n))


class _Geom:
    def __init__(self, batch, ctx_len, seq_len, d_model, tm):
        self.b, self.l, self.t, self.d = batch, ctx_len, seq_len, d_model
        self.s = ctx_len + seq_len
        self.n = batch * self.s
        self.tm = tm
        assert ctx_len % tm == 0 and seq_len % tm == 0
        self.tpb = self.s // tm
        self.nct = ctx_len // tm
        self.ntiles = self.n // tm

    def mod_spec(self):
        tpb, nct = self.tpb, self.nct

        def imap(i):
            return (i // tpb, ((i % tpb) >= nct).astype(jnp.int32), 0, 0)

        return pl.BlockSpec((None, None, 6, self.d), imap)


def _modmm_kernel(x_ref, g_ref, mod_ref, w_ref, o_ref, *h_ref, which):
    h = _modulate(x_ref[...], g_ref[...], mod_ref[3 * which:3 * which + 1, :],
                  mod_ref[3 * which + 1:3 * which + 2, :])
    if h_ref:
        h_ref[0][...] = h
    o_ref[...] = _dot(h.astype(BF16), w_ref[...])


def _mod_matmul(geo, x, g, mods, w, which, emit_h):
    n, d = x.shape
    nn = w.shape[1]
    tm = geo.tm
    out_shape = [jax.ShapeDtypeStruct((n, nn), F32)]
    out_specs = [pl.BlockSpec((tm, nn), lambda i: (i, 0))]
    if emit_h:
        out_shape.append(jax.ShapeDtypeStruct((n, d), F32))
        out_specs.append(pl.BlockSpec((tm, d), lambda i: (i, 0)))
    res = pl.pallas_call(
        functools.partial(_modmm_kernel, which=which),
        grid=(geo.ntiles,),
        in_specs=[
            pl.BlockSpec((tm, d), lambda i: (i, 0)),
            pl.BlockSpec((1, d), lambda i: (0, 0)),
            geo.mod_spec(),
            pl.BlockSpec((d, nn), lambda i: (0, 0)),
        ],
        out_specs=out_specs,
        out_shape=out_shape,
        compiler_params=_cparams("parallel"),
        name="mod_matmul",
    )(x, g.reshape(1, d), mods, w)
    return res if emit_h else res[0]


def _mmres_kernel(y_ref, w_ref, x_ref, mod_ref, o_ref, *, gidx):
    acc = _dot(y_ref[...], w_ref[...])
    o_ref[...] = x_ref[...] + mod_ref[gidx:gidx + 1, :] * acc


def _matmul_res(geo, y, w, x, mods, gidx):
    n, k = y.shape
    d = x.shape[1]
    tm = geo.tm
    return pl.pallas_call(
        functools.partial(_mmres_kernel, gidx=gidx),
        grid=(geo.ntiles,),
        in_specs=[
            pl.BlockSpec((tm, k), lambda i: (i, 0)),
            pl.BlockSpec((k, d), lambda i: (0, 0)),
            pl.BlockSpec((tm, d), lambda i: (i, 0)),
            geo.mod_spec(),
        ],
        out_specs=pl.BlockSpec((tm, d), lambda i: (i, 0)),
        out_shape=jax.ShapeDtypeStruct((n, d), F32),
        compiler_params=_cparams("parallel"),
        name="matmul_res",
    )(y, w, x, mods)


def _bmm_kernel(x_ref, w_ref, o_ref, *, acts):
    j = pl.program_id(0)
    y = _dot(x_ref[...], w_ref[...])
    out = y
    for jj, a in enumerate(acts):
        if a == "tanh":
            out = jnp.where(j == jj, jnp.tanh(y), out)
        elif a == "sigmoid":
            out = jnp.where(j == jj, _sigmoid(y), out)
    o_ref[...] = out.astype(o_ref.dtype)


def _bmm(x3, w3, src, acts, out_dtype, tm):
    _, n, k = x3.shape
    nj, _, nn = w3.shape
    src = tuple(src)

    def xmap(j, i):
        idx = jnp.int32(src[0])
        for jj in range(1, nj):
            idx = jnp.where(j == jj, jnp.int32(src[jj]), idx)
        return (idx, i, 0)

    return pl.pallas_call(
        functools.partial(_bmm_kernel, acts=tuple(acts)),
        grid=(nj, n // tm),
        in_specs=[
            pl.BlockSpec((None, tm, k), xmap),
            pl.BlockSpec((None, k, nn), lambda j, i: (j, 0, 0)),
        ],
        out_specs=pl.BlockSpec((None, tm, nn), lambda j, i: (j, i, 0)),
        out_shape=jax.ShapeDtypeStruct((nj, n, nn), out_dtype),
        compiler_params=_cparams("parallel", "parallel"),
        name="bmm",
    )(x3, w3)


def _rwmix_kernel(x_ref, xp_ref, xn_ref, g_ref, mod_ref, mix_ref, o_ref, *, tpb, nct):
    i = pl.program_id(0)
    tm = x_ref.shape[0]
    g = g_ref[...]
    shift = mod_ref[0:1, :]
    scale = mod_ref[1:2, :]
    h = _modulate(x_ref[...], g, shift, scale)
    hp = _modulate(xp_ref[...], g, shift, scale)[SUBLANES - 1:SUBLANES, :]
    hn = _modulate(xn_ref[...], g, shift, scale)[0:1, :]
    it = i % tpb
    first = jnp.logical_or(it == 0, it == nct)
    last = jnp.logical_or(it == nct - 1, it == tpb - 1)
    hp = jnp.where(first, 0.0, hp)
    hn = jnp.where(last, 0.0, hn)
    rows = lax.broadcasted_iota(jnp.int32, h.shape, 0)
    prev = jnp.where(rows == 0, hp, pltpu.roll(h, 1, axis=0))
    nxt = jnp.where(rows == tm - 1, hn, pltpu.roll(h, tm - 1, axis=0))
    xx = 0.5 * (prev + nxt) - h
    for m in range(6):
        o_ref[m] = (h + xx * mix_ref[m:m + 1, :]).astype(BF16)


def _rwkv_mix(geo, x, g, mods, mix):
    n, d = x.shape
    tm = geo.tm
    r8 = tm // SUBLANES
    nblk8 = n // SUBLANES
    return pl.pallas_call(
        functools.partial(_rwmix_kernel, tpb=geo.tpb, nct=geo.nct),
        grid=(geo.ntiles,),
        in_specs=[
            pl.BlockSpec((tm, d), lambda i: (i, 0)),
            pl.BlockSpec((SUBLANES, d), lambda i: (jnp.maximum(i * r8 - 1, 0), 0)),
            pl.BlockSpec((SUBLANES, d), lambda i: (jnp.minimum((i + 1) * r8, nblk8 - 1), 0)),
            pl.BlockSpec((1, d), lambda i: (0, 0)),
            geo.mod_spec(),
            pl.BlockSpec((6, d), lambda i: (0, 0)),
        ],
        out_specs=pl.BlockSpec((6, tm, d), lambda i: (0, i, 0)),
        out_shape=jax.ShapeDtypeStruct((6, n, d), BF16),
        compiler_params=_cparams("parallel"),
        name="rwkv_mix",
    )(x, x, x, g.reshape(1, d), mods, mix)


def _head_sum(x, bd):
    return _dot(x.astype(BF16), bd)


def _rwfeat_kernel(k_ref, v_ref, tl_ref, w2_ref, a2_ref, g2_ref, v2_ref, pv_ref, bd_ref,
                   *rest, has_vlora):
    if has_vlora:
        vf_ref, lw_ref, kd_ref, as_ref, kk_ref, g_ref, vo_ref = rest
    else:
        lw_ref, kd_ref, as_ref, kk_ref, g_ref = rest
    k = k_ref[...]
    tw = tl_ref[0]
    ta = tl_ref[1]
    tg = tl_ref[2]
    w0 = pv_ref[0:2, :]
    a0 = pv_ref[2:4, :]
    k_k = pv_ref[5:6, :]
    k_a = pv_ref[6:7, :]
    for z in range(2):
        sl = slice(LORA_PAD * z, LORA_PAD * (z + 1))
        lora_w = _dot(tw[:, sl], w2_ref[z])
        w = -_softplus(-(w0[z:z + 1, :] + lora_w)) - 0.5
        lw_ref[z] = -jnp.exp(w)
        a_sig = _sigmoid(a0[z:z + 1, :] + _dot(ta[:, sl], a2_ref[z]))
        as_ref[z] = a_sig.astype(BF16)
        kd_ref[z] = k * (1.0 + (a_sig - 1.0) * k_a)
    g_ref[...] = _dot(tg, g2_ref[...]).astype(BF16)
    kkr = k * k_k
    ss = _head_sum(kkr * kkr, bd_ref[...])
    kk_ref[...] = (kkr * lax.rsqrt(ss + 1e-12)).astype(BF16)
    if has_vlora:
        v = v_ref[...]
        tv = tl_ref[3]
        gate = _sigmoid(pv_ref[4:5, :] + _dot(tv[:, 0:LORA_PAD], v2_ref[...]))
        vo_ref[...] = v + (vf_ref[...] - v) * gate


def _rwkv_feat(geo, rkv, tl, w2p, a2p, g2, v2p, pvec, vfirst):
    _, n, d = rkv.shape
    tm, tc = geo.tm, COL_TILE
    has_vlora = vfirst is not None
    lt = tl.shape[2]
    ii = lax.broadcasted_iota(jnp.int32, (tc, tc), 0) // RW_HEAD
    jj = lax.broadcasted_iota(jnp.int32, (tc, tc), 1) // RW_HEAD
    bd = (ii == jj).astype(BF16)
    row = lambda i, j: (i, j)
    in_specs = [
        pl.BlockSpec((None, tm, tc), lambda i, j: (1, i, j)),
        pl.BlockSpec((None, tm, tc), lambda i, j: (2, i, j)),
        pl.BlockSpec((4, tm, lt), lambda i, j: (0, i, 0)),
        pl.BlockSpec((2, LORA_PAD, tc), lambda i, j: (0, 0, j)),
        pl.BlockSpec((2, LORA_PAD, tc), lambda i, j: (0, 0, j)),
        pl.BlockSpec((lt, tc), lambda i, j: (0, j)),
        pl.BlockSpec((LORA_PAD, tc), lambda i, j: (0, j)),
        pl.BlockSpec((SUBLANES, tc), lambda i, j: (0, j)),
        pl.BlockSpec((tc, tc), lambda i, j: (0, 0)),
    ]
    args = [rkv, rkv, tl, w2p, a2p, g2, v2p, pvec, bd]
    dir_spec = pl.BlockSpec((2, tm, tc), lambda i, j: (0, i, j))
    out_specs = [dir_spec, dir_spec, dir_spec, pl.BlockSpec((tm, tc), row), pl.BlockSpec((tm, tc), row)]
    out_shape = [jax.ShapeDtypeStruct((2, n, d), dt) for dt in (F32, F32, BF16)]
    out_shape += [jax.ShapeDtypeStruct((n, d), BF16)] * 2
    if has_vlora:
        vf_arr, vf_spec = _plane(vfirst, (tm, tc), row)
        in_specs.append(vf_spec)
        args.append(vf_arr)
        out_specs.append(pl.BlockSpec((tm, tc), row))
        out_shape.append(jax.ShapeDtypeStruct((n, d), F32))
    res = pl.pallas_call(
        functools.partial(_rwfeat_kernel, has_vlora=has_vlora),
        grid=(geo.ntiles, d // tc),
        in_specs=in_specs,
        out_specs=out_specs,
        out_shape=out_shape,
        compiler_params=_cparams("parallel", "parallel"),
        name="rwkv_feat",
    )(*args)
    if has_vlora:
        lw, kd, asg, kk, g, v = res
    else:
        lw, kd, asg, kk, g = res
        v = (rkv, 2)
    return lw, kd, asg, kk, g, v


def _wkv_chunk_kernel(r_ref, v_ref, kk_ref, lw_ref, kd_ref, as_ref,
                      m_ref, ga_ref, rq_ref, o0_ref, pc_ref, *, npairs):
    c = WKV_CHUNK
    c2 = 2 * c
    sgn = 1 - 2 * pl.program_id(0)
    ri = lax.broadcasted_iota(jnp.int32, (c2, c2), 0)
    ci = lax.broadcasted_iota(jnp.int32, (c2, c2), 1)
    same = (ri >= c) == (ci >= c)
    tt = jnp.where(ri >= c, ri - c, ri)
    ss = jnp.where(ci >= c, ci - c, ci)
    earlier = (ss - tt) * sgn < 0
    strict = jnp.logical_and(same, earlier)
    incl = jnp.logical_and(same, jnp.logical_or(earlier, ss == tt))
    eye = (ri == ci).astype(F32)
    r64 = lax.broadcasted_iota(jnp.int32, (c, c), 0)
    c64 = lax.broadcasted_iota(jnp.int32, (c, c), 1)
    ltri = jnp.where((c64 - r64) * sgn <= 0, 1.0, 0.0).astype(BF16)
    head0 = lax.broadcasted_iota(jnp.int32, (c, LANES), 1) < RW_HEAD
    pairs = range(npairs)

    def stack(x):
        return jnp.concatenate([jnp.where(head0, x, 0.0), jnp.where(head0, 0.0, x)], axis=0)

    def dup(x):
        return jnp.concatenate([x, x], axis=0)

    def bf(x):
        return x.astype(BF16)

    lhs, rhs, a2, bp2, kp2, vst, r2 = [], [], [], [], [], [], []
    lw_all = lw_ref[...]
    cum_all = _dot_exact_lhs(ltri, lw_all)
    for p in pairs:
        sl = slice(LANES * p, LANES * (p + 1))
        lw = lw_all[:, sl]
        cum = cum_all[:, sl]
        tot = jnp.sum(lw, axis=0, keepdims=True)
        p_inv = jnp.exp(-cum)
        p_end = jnp.exp(tot - cum)
        kk = kk_ref[:, sl]
        b = kk * as_ref[:, sl]
        kd = kd_ref[:, sl]
        a2p = stack(-kk * jnp.exp(cum - lw))
        r2p = stack(r_ref[:, sl] * jnp.exp(cum))
        a2.append(bf(a2p))
        r2.append(r2p)
        lhs.append(jnp.concatenate([a2[p], bf(r2p)], axis=0))
        rhs.append(jnp.concatenate([dup(bf(b * p_inv)), dup(bf(kd * p_inv))], axis=0))
        bp2.append(bf(stack(b * p_end)))
        kp2.append(bf(stack(kd * p_end)))
        vst.append(bf(stack(v_ref[:, sl])))
        pc_ref[:, sl] = jnp.broadcast_to(jnp.exp(tot), (SUBLANES, LANES))
    gram = [_dot_nt(lhs[p], rhs[p]) for p in pairs]
    nab = [jnp.where(strict, gram[p][0:c2, 0:c2], 0.0) for p in pairs]
    nrb = [bf(jnp.where(incl, gram[p][c2:2 * c2, 0:c2], 0.0)) for p in pairs]
    nk = [bf(jnp.concatenate([jnp.where(strict, gram[p][0:c2, c2:2 * c2], 0.0),
                              jnp.where(incl, gram[p][c2:2 * c2, c2:2 * c2], 0.0)], axis=0))
          for p in pairs]
    quads = range(npairs // 2)

    def side(x0, x1):
        return jnp.concatenate([x0, x1], axis=1)

    def diag(x0, x1):
        z0 = jnp.zeros_like(x0)
        return jnp.concatenate([side(x0, z0), side(z0, x1)], axis=0)

    def diag_halves(x):
        return diag(x[:, 0:c2], x[:, c2:2 * c2])

    def unside(xs):
        return [xs[p // 2][:, c2 * (p % 2):c2 * (p % 2 + 1)] for p in pairs]

    xv = unside([_dot(side(nk[2 * q], nk[2 * q + 1]), diag(vst[2 * q], vst[2 * q + 1]))
                 for q in quads])
    tinv = [side(eye + nab[2 * q], eye + nab[2 * q + 1]) for q in quads]
    npow = [bf(side(nab[2 * q], nab[2 * q + 1])) for q in quads]
    for _ in range(int(math.log2(c)) - 1):
        npow = [bf(_dot(npow[q], diag_halves(npow[q]))) for q in quads]
        tinv = [tinv[q] + _dot(bf(tinv[q]), diag_halves(npow[q])) for q in quads]
    tinv = unside(tinv)
    y = [_dot(bf(tinv[p]), jnp.concatenate([a2[p], bf(xv[p][0:c2, :])], axis=1)) for p in pairs]
    yb = [bf(y[p]) for p in pairs]
    z = [_dot(nrb[p], yb[p]) for p in pairs]
    mg = unside([_dot(bf(side(y[2 * q].T, y[2 * q + 1].T)), diag(bp2[2 * q], bp2[2 * q + 1]))
                 for q in quads])
    vk = unside([_dot(bf(side(vst[2 * q].astype(F32).T, vst[2 * q + 1].astype(F32).T)),
                      diag(kp2[2 * q], kp2[2 * q + 1])) for q in quads])
    for p in pairs:
        sl = slice(LANES * p, LANES * (p + 1))
        m_ref[:, sl] = bf(mg[p][0:c2, :])
        ga_ref[:, sl] = bf(mg[p][c2:2 * c2, :] + vk[p])
        rq_ref[:, sl] = bf(r2[p] + z[p][:, 0:c2])
        o0 = z[p][:, c2:2 * c2] + xv[p][c2:2 * c2, :]
        o0_ref[:, sl] = bf(o0[0:c, :] + o0[c:c2, :])


def _wkv_scan_kernel(m_ref, ga_ref, rq_ref, o0_ref, pc_ref, o_ref, g_scr, *, npairs):
    c = WKV_CHUNK

    @pl.when(pl.program_id(2) == 0)
    def _():
        g_scr[...] = jnp.zeros_like(g_scr)

    for p in range(npairs):
        sl = slice(LANES * p, LANES * (p + 1))
        g = g_scr[p]
        g_hi = g.astype(BF16)
        o_st = _dot_nt(rq_ref[:, sl], g_hi)
        o_ref[:, sl] = o_st[0:c, :] + o_st[c:2 * c, :] + o0_ref[:, sl]
        m = m_ref[:, sl]
        g_scr[p] = g * pc_ref[0:1, sl] + _dot(g_hi, m) + ga_ref[:, sl]


WKV_CHUNK_LANES = 2048


def _plane(x, block, imap):
    if not isinstance(x, tuple):
        return x, pl.BlockSpec(block, imap)
    stack, k = x
    return stack, pl.BlockSpec((None,) + block, lambda *ids: (k,) + tuple(imap(*ids)))


def _wkv_bidir(r, v, kk, lw, kd, asg, batch, ctx_len):
    n, d = kk.shape
    c = WKV_CHUNK
    s = n // batch
    ncs = s // c
    ncc = ctx_len // c
    nch = n // c
    lanes = min(WKV_CHUNK_LANES, d)
    ngrp = d // lanes
    rowmap = lambda z, i, j: (i, j)
    r, r_spec = _plane(r, (c, lanes), rowmap)
    v, v_spec = _plane(v, (c, lanes), rowmap)
    perdir = pl.BlockSpec((None, c, lanes), lambda z, i, j: (z, i, j))
    big = pl.BlockSpec((None, 2 * c, lanes), lambda z, i, j: (z, i, j))
    m_, ga_, rq_, o0_, pc_ = pl.pallas_call(
        functools.partial(_wkv_chunk_kernel, npairs=lanes // LANES),
        grid=(2, nch, ngrp),
        in_specs=[r_spec, v_spec, pl.BlockSpec((c, lanes), rowmap), perdir, perdir, perdir],
        out_specs=[big, big, big, perdir,
                   pl.BlockSpec((None, SUBLANES, lanes), lambda z, i, j: (z, i, j))],
        out_shape=[jax.ShapeDtypeStruct((2, nch * 2 * c, d), BF16)] * 3
        + [jax.ShapeDtypeStruct((2, n, d), BF16),
           jax.ShapeDtypeStruct((2, nch * SUBLANES, d), F32)],
        compiler_params=_cparams("parallel", "parallel", "parallel"),
        name="wkv_chunk",
    )(r, v, kk, lw, kd, asg)

    def cmap(z, b, cc):
        back = jnp.where(cc < ncc, ncc - 1 - cc, ncs + ncc - 1 - cc)
        return (z, b * ncs + jnp.where(z == 0, cc, back), 0)

    out = pl.pallas_call(
        functools.partial(_wkv_scan_kernel, npairs=d // LANES),
        grid=(2, batch, ncs),
        in_specs=[
            pl.BlockSpec((None, 2 * c, d), cmap),
            pl.BlockSpec((None, 2 * c, d), cmap),
            pl.BlockSpec((None, 2 * c, d), cmap),
            pl.BlockSpec((None, c, d), cmap),
            pl.BlockSpec((None, SUBLANES, d), cmap),
        ],
        out_specs=pl.BlockSpec((None, c, d), cmap),
        out_shape=jax.ShapeDtypeStruct((2, n, d), F32),
        scratch_shapes=[pltpu.VMEM((d // LANES, 2 * c, 2 * c), F32)],
        compiler_params=_cparams("parallel", "parallel", "arbitrary"),
        name="wkv_scan",
    )(m_, ga_, rq_, o0_, pc_)
    return out


def _rwread_kernel(o_ref, r_ref, kd_ref, v_ref, g_ref, pv_ref, bd_ref, y_ref):
    bd = bd_ref[...]
    wkv = o_ref[0] + o_ref[1]
    inv = 1.0 / RW_HEAD
    mu = _head_sum(wkv, bd) * inv
    dev = wkv - mu
    var = _head_sum(dev * dev, bd) * inv
    y = dev * lax.rsqrt(var + GN_EPS) * pv_ref[1:2, :] + pv_ref[2:3, :]
    rk = r_ref[...] * (kd_ref[0] + kd_ref[1]) * pv_ref[0:1, :]
    y = y + _head_sum(rk, bd) * v_ref[...]
    y_ref[...] = (y * g_ref[...]).astype(BF16)


def _rwkv_readout(geo, wkv, r, kd, v, g, pvec):
    n, d = g.shape
    tm, tc = geo.tm, COL_TILE
    ii = lax.broadcasted_iota(jnp.int32, (tc, tc), 0) // RW_HEAD
    jj = lax.broadcasted_iota(jnp.int32, (tc, tc), 1) // RW_HEAD
    bd = (ii == jj).astype(BF16)
    row = pl.BlockSpec((tm, tc), lambda i, j: (i, j))
    dirs = pl.BlockSpec((2, tm, tc), lambda i, j: (0, i, j))
    r, r_spec = _plane(r, (tm, tc), lambda i, j: (i, j))
    v, v_spec = _plane(v, (tm, tc), lambda i, j: (i, j))
    return pl.pallas_call(
        _rwread_kernel,
        grid=(geo.ntiles, d // tc),
        in_specs=[dirs, r_spec, dirs, v_spec, row,
                  pl.BlockSpec((SUBLANES, tc), lambda i, j: (0, j)),
                  pl.BlockSpec((tc, tc), lambda i, j: (0, 0))],
        out_specs=row,
        out_shape=jax.ShapeDtypeStruct((n, d), BF16),
        compiler_params=_cparams("parallel", "parallel"),
        name="rwkv_readout",
    )(wkv, r, kd, v, g, pvec, bd)


def _mlaqkv_kernel(z_ref, qn_ref, kvn_ref, wuq_ref, wukv_ref, gq_ref, gk_ref, cos_ref, sin_ref,
                   q_ref, k_ref, v_ref, *, q_lora, kv_lora):
    z = z_ref[...]
    cq = z[:, 0:q_lora]
    ckv = z[:, q_lora:q_lora + kv_lora]
    krot = z[:, q_lora + kv_lora:q_lora + kv_lora + LANES]

    def rms(x, g):
        ms = jnp.mean(x * x, axis=-1, keepdims=True)
        return x * lax.rsqrt(ms + EPS) * g

    qf = _dot(rms(cq, qn_ref[...]).astype(BF16), wuq_ref[...])
    kvf = _dot(rms(ckv, kvn_ref[...]).astype(BF16), wukv_ref[...])
    cos = cos_ref[...]
    sin = sin_ref[...]
    half = QK_ROPE // 2
    lane = lax.broadcasted_iota(jnp.int32, cos.shape, 1)

    def rope(x):
        up = pltpu.roll(x, LANES - half, axis=1)
        dn = pltpu.roll(x, half, axis=1)
        return x * cos + jnp.where(lane < half, up, dn) * sin

    inv_w = 1.0 / (QK_NOPE + QK_ROPE)
    gq = gq_ref[...]
    gk = gk_ref[...]
    kr_ss = jnp.sum(krot * krot, axis=-1, keepdims=True)
    for h in range(MLA_HEADS):
        o = QK_PAD * h
        qh = qf[:, o:o + QK_PAD]
        rs = lax.rsqrt(jnp.sum(qh * qh, axis=-1, keepdims=True) * inv_w + EPS)
        qn = qh * rs * gq * (ATTN_SCALE * math.log2(math.e))
        q_ref[:, o:o + QK_NOPE] = qn[:, 0:QK_NOPE].astype(BF16)
        q_ref[:, o + QK_NOPE:o + QK_PAD] = rope(qn[:, QK_NOPE:QK_PAD]).astype(BF16)
        kn = kvf[:, o:o + QK_NOPE]
        rsk = lax.rsqrt((jnp.sum(kn * kn, axis=-1, keepdims=True) + kr_ss) * inv_w + EPS)
        k_ref[:, o:o + QK_NOPE] = (kn * rsk * gk[:, 0:QK_NOPE]).astype(BF16)
        k_ref[:, o + QK_NOPE:o + QK_PAD] = rope(krot * rsk * gk[:, QK_NOPE:QK_PAD]).astype(BF16)
        v_ref[:, V_HEAD * h:V_HEAD * (h + 1)] = kvf[:, o + QK_NOPE:o + QK_PAD].astype(BF16)


def _mla_qkv(geo, z, qn, kvn, wuq_p, wukv, gq_p, gk_p, cos_t, sin_t):
    n, zw = z.shape
    tm = geo.tm
    q_lora, kv_lora = qn.shape[0], kvn.shape[0]
    hq = MLA_HEADS * QK_PAD
    tpb = geo.tpb
    full = lambda i: (0, 0)
    rowmap = lambda i: (i, 0)
    return pl.pallas_call(
        functools.partial(_mlaqkv_kernel, q_lora=q_lora, kv_lora=kv_lora),
        grid=(geo.ntiles,),
        in_specs=[
            pl.BlockSpec((tm, zw), rowmap),
            pl.BlockSpec((1, q_lora), full),
            pl.BlockSpec((1, kv_lora), full),
            pl.BlockSpec((q_lora, hq), full),
            pl.BlockSpec((kv_lora, hq), full),
            pl.BlockSpec((1, QK_PAD), full),
            pl.BlockSpec((1, QK_PAD), full),
            pl.BlockSpec((tm, LANES), lambda i: (i % tpb, 0)),
            pl.BlockSpec((tm, LANES), lambda i: (i % tpb, 0)),
        ],
        out_specs=[pl.BlockSpec((tm, hq), rowmap), pl.BlockSpec((tm, hq), rowmap),
                   pl.BlockSpec((tm, MLA_HEADS * V_HEAD), rowmap)],
        out_shape=[jax.ShapeDtypeStruct((n, hq), BF16), jax.ShapeDtypeStruct((n, hq), BF16),
                   jax.ShapeDtypeStruct((n, MLA_HEADS * V_HEAD), BF16)],
        compiler_params=_cparams("parallel"),
        name="mla_qkv",
    )(z, qn.reshape(1, -1), kvn.reshape(1, -1), wuq_p, wukv, gq_p, gk_p, cos_t, sin_t)


ATTN_SUBTILES = 2


def _attn_kernel(q_ref, k_ref, vt_ref, o_ref):
    k = k_ref[...]
    vt = vt_ref[...]
    cols = q_ref.shape[0] // ATTN_SUBTILES
    subs = range(ATTN_SUBTILES)
    s = [_dot_nt(k, q_ref[cols * i:cols * (i + 1), :]) for i in subs]
    m = [jnp.max(s[i], axis=0, keepdims=True) for i in subs]
    p = [jnp.exp2(s[i] - m[i]) for i in subs]
    l = [jnp.sum(p[i], axis=0, keepdims=True) for i in subs]
    for i in subs:
        ot = _dot(vt, p[i].astype(BF16)) / l[i]
        o_ref[cols * i:cols * (i + 1), :] = ot.T.astype(BF16)


def _attn_call(q3, k3, vt4, tq, nkeys):
    b, sq, _ = q3.shape
    return pl.pallas_call(
        _attn_kernel,
        grid=(b, MLA_HEADS, sq // tq),
        in_specs=[
            pl.BlockSpec((None, tq, QK_PAD), lambda bb, h, i: (bb, i, h)),
            pl.BlockSpec((None, nkeys, QK_PAD), lambda bb, h, i: (bb, 0, h)),
            pl.BlockSpec((None, None, V_HEAD, nkeys), lambda bb, h, i: (bb, h, 0, 0)),
        ],
        out_specs=pl.BlockSpec((None, tq, V_HEAD), lambda bb, h, i: (bb, i, h)),
        out_shape=jax.ShapeDtypeStruct((b, sq, MLA_HEADS * V_HEAD), BF16),
        compiler_params=_cparams("parallel", "parallel", "parallel"),
        name="mla_attention",
    )(q3, k3, vt4)


def _mla_attention(geo, q, k, v):
    b, s, l = geo.b, geo.s, geo.l
    q3 = q.reshape(b, s, MLA_HEADS * QK_PAD)
    k3 = k.reshape(b, s, MLA_HEADS * QK_PAD)
    vt4 = v.reshape(b, s, MLA_HEADS, V_HEAD).transpose(0, 2, 3, 1)
    o_ctx = _attn_call(q3[:, :l], k3, vt4, geo.tm, l)
    tq_lat = 2 * geo.tm if geo.t % (2 * geo.tm) == 0 else geo.tm
    o_lat = _attn_call(q3[:, l:], k3, vt4, tq_lat, s)
    return jnp.concatenate([o_ctx, o_lat], axis=1).reshape(b * s, MLA_HEADS * V_HEAD)


def _topk_rows(s, k, payload=None):
    rows = lax.broadcasted_iota(jnp.int32, s.shape, 0).astype(F32)
    big = float(s.shape[0])
    vals, idxs = [], []
    for _ in range(k):
        m = jnp.max(s, axis=0, keepdims=True)
        idx = jnp.min(jnp.where(s == m, rows, big), axis=0, keepdims=True)
        hit = rows == idx
        vals.append(m)
        if payload is None:
            idxs.append(idx)
        else:
            idxs.append(jnp.sum(jnp.where(hit, payload, 0.0), axis=0, keepdims=True))
        s = jnp.where(hit, -jnp.inf, s)
    return jnp.concatenate(vals, axis=0), jnp.concatenate(idxs, axis=0)


def _peersel_kernel(q_ref, qn_ref, keys_ref, e_ref, g_ref):
    q = q_ref[...]
    ms = jnp.mean(q * q, axis=-1, keepdims=True)
    qn = q * lax.rsqrt(ms + EPS) * qn_ref[...]
    half = D_KEY // 2
    s1 = _dot3_nt(keys_ref[0], qn[:, 0:half])
    s2 = _dot3_nt(keys_ref[1], qn[:, half:D_KEY])
    t1, i1 = _topk_rows(s1, PEER_TOPK)
    t2, i2 = _topk_rows(s2, PEER_TOPK)
    k = PEER_TOPK
    sub = lax.broadcasted_iota(jnp.int32, (SUBLANES, t1.shape[1]), 0)
    cand = [t1[0:1, :] + t2]
    cidx = [i1[0:1, :] * float(N_KEYS) + i2]
    for p in range(1, k // 2):
        live = sub < k // (p + 1)
        cand.append(jnp.where(live, t1[p:p + 1, :] + t2[0:SUBLANES, :], -jnp.inf))
        cidx.append(i1[p:p + 1, :] * float(N_KEYS) + i2[0:SUBLANES, :])
    cand.append(t1[k // 2:k, :] + t2[0:1, :])
    cidx.append(i1[k // 2:k, :] * float(N_KEYS) + i2[0:1, :])
    best, eidx = _topk_rows(jnp.concatenate(cand, axis=0), k, payload=jnp.concatenate(cidx, axis=0))
    ex = jnp.exp(best - jnp.max(best, axis=0, keepdims=True))
    g_ref[...] = ex / jnp.sum(ex, axis=0, keepdims=True)
    e_ref[...] = eidx.astype(jnp.int32)


def _peer_select(qp, q_norm, keys, tm, row0, n):
    assert row0 % tm == 0 and n % tm == 0
    tile0 = row0 // tm
    return pl.pallas_call(
        _peersel_kernel,
        grid=(n // tm, PEER_HEADS),
        in_specs=[
            pl.BlockSpec((tm, D_KEY), lambda i, h: (tile0 + i, h)),
            pl.BlockSpec((1, D_KEY), lambda i, h: (0, 0)),
            pl.BlockSpec((2, N_KEYS, D_KEY // 2), lambda i, h: (0, 0, 0)),
        ],
        out_specs=[pl.BlockSpec((PEER_TOPK, tm), lambda i, h: (h, i)),
                   pl.BlockSpec((PEER_TOPK, tm), lambda i, h: (h, i))],
        out_shape=[jax.ShapeDtypeStruct((PEER_HEADS * PEER_TOPK, n), jnp.int32),
                   jax.ShapeDtypeStruct((PEER_HEADS * PEER_TOPK, n), F32)],
        compiler_params=_cparams("parallel", "parallel"),
        name="peer_select",
    )(qp, q_norm.reshape(1, D_KEY), keys)


GATHER_SLOTS = 4
SLAB_PAD = 1


def _pack_expert_table(u, v):
    ne, d = u.shape
    ub = lax.bitcast_convert_type(u.astype(BF16), jnp.uint16).astype(jnp.uint32)
    vb = lax.bitcast_convert_type(v.astype(BF16), jnp.uint16).astype(jnp.uint32)
    return ((vb << 16) | ub).reshape(ne, d // LANES, LANES)


def _peergather_kernel(idx_ref, idxn_ref, gate_ref, h_ref, x_ref, mod_ref, tab_ref, o_ref,
                       *scratch, tb):
    nsel = PEER_HEADS * PEER_TOPK
    nrow = h_ref.shape[1] // LANES
    pitch = nrow + SLAB_PAD
    ns = GATHER_SLOTS
    bufs, sem = scratch[:ns], scratch[ns]
    lane_t = lax.broadcasted_iota(jnp.int32, (nsel, tb), 1)
    g2 = mod_ref[5:6, :]
    step = pl.program_id(0)
    nsteps = pl.num_programs(0)

    def row_copy(ids_ref, t, j, slot):
        return pltpu.make_async_copy(
            tab_ref.at[ids_ref[t, j]],
            bufs[slot].at[pl.ds(j * pitch, nrow), :],
            sem.at[slot])

    def issue(ids_ref, t, slot):
        for j in range(nsel):
            row_copy(ids_ref, t, j, slot).start(priority=j % 2)

    def wait(t, slot):
        for j in range(nsel):
            row_copy(idx_ref, t, j, slot).wait()

    def packed(slot, s):
        return bufs[slot][pl.ds(s, nsel, stride=pitch), :]

    def compute(t, slot):
        hrow = h_ref[pl.ds(t, 1), :]
        acc = jnp.zeros((nsel, LANES), F32)
        for s in range(nrow):
            u = lax.bitcast_convert_type(packed(slot, s) << 16, F32)
            acc = acc + u * hrow[:, LANES * s:LANES * (s + 1)]
        dots = jnp.sum(acc, axis=-1, keepdims=True)
        gcol = jnp.sum(jnp.where(lane_t == t, gate_ref[...], 0.0), axis=-1, keepdims=True)
        coef = gcol * _gelu(dots)
        outs = []
        for s in range(nrow):
            vv = lax.bitcast_convert_type(packed(slot, s) & jnp.uint32(0xFFFF0000), F32)
            outs.append(jnp.sum(coef * vv, axis=0, keepdims=True))
        orow = jnp.concatenate(outs, axis=1)
        o_ref[pl.ds(t, 1), :] = x_ref[pl.ds(t, 1), :] + g2 * orow

    @pl.when(step == 0)
    def _():
        for s in range(ns - 1):
            issue(idx_ref, s, s)

    ngroups = tb // ns

    def body(g, carry):
        for s in range(ns):
            t = g * ns + s
            wait(t, s)
            issue(idx_ref, t + ns - 1, (s + ns - 1) % ns)
            compute(t, s)
        return carry

    lax.fori_loop(0, ngroups - 1, body, 0)
    for s in range(ns):
        t = (ngroups - 1) * ns + s
        wait(t, s)
        if s == 0:
            issue(idx_ref, tb - 1, ns - 1)
        else:
            @pl.when(step < nsteps - 1)
            def _():
                issue(idxn_ref, s - 1, s - 1)
        compute(t, s)


def _peer_gather(geo_g, eidx, gate_t, h2, x, mods, table, n):
    n_all, d = x.shape
    tb = geo_g.tm
    nsel = PEER_HEADS * PEER_TOPK
    pitch = d // LANES + SLAB_PAD
    assert n % tb == 0
    nsteps = n // tb
    return pl.pallas_call(
        functools.partial(_peergather_kernel, tb=tb),
        grid=(nsteps,),
        in_specs=[
            pl.BlockSpec((tb, nsel), lambda i: (i, 0), memory_space=pltpu.SMEM),
            pl.BlockSpec((tb, nsel), lambda i: (jnp.minimum(i + 1, nsteps - 1), 0),
                         memory_space=pltpu.SMEM),
            pl.BlockSpec((nsel, tb), lambda i: (0, i)),
            pl.BlockSpec((tb, d), lambda i: (i, 0)),
            pl.BlockSpec((tb, d), lambda i: (i, 0)),
            geo_g.mod_spec(),
            pl.BlockSpec(memory_space=pl.ANY),
        ],
        out_specs=pl.BlockSpec((tb, d), lambda i: (i, 0)),
        out_shape=jax.ShapeDtypeStruct((n_all, d), F32),
        scratch_shapes=[pltpu.VMEM((nsel * pitch, LANES), jnp.uint32)] * GATHER_SLOTS
        + [pltpu.SemaphoreType.DMA((GATHER_SLOTS,))],
        compiler_params=_cparams("arbitrary"),
        name="peer_gather",
    )(eidx, eidx, gate_t, h2, x, mods, table)


def _scres_kernel(full_ref, x_ref, y_ref, mod_ref, o_ref):
    del full_ref
    o_ref[...] = x_ref[...] + mod_ref[5:6, :] * y_ref[...]


def _sc_rows_residual(geo, out_full, x, y_sc, mods, row0):
    n, d = x.shape
    tm, tpb, nct = geo.tm, geo.tpb, geo.nct
    assert row0 % tm == 0 and y_sc.shape[0] % tm == 0
    tile0 = row0 // tm

    def modmap(i):
        t = tile0 + i
        return (t // tpb, ((t % tpb) >= nct).astype(jnp.int32), 0, 0)

    return pl.pallas_call(
        _scres_kernel,
        grid=(y_sc.shape[0] // tm,),
        in_specs=[
            pl.BlockSpec(memory_space=pl.ANY),
            pl.BlockSpec((tm, d), lambda i: (tile0 + i, 0)),
            pl.BlockSpec((tm, d), lambda i: (i, 0)),
            pl.BlockSpec((None, None, 6, d), modmap),
        ],
        out_specs=pl.BlockSpec((tm, d), lambda i: (tile0 + i, 0)),
        out_shape=jax.ShapeDtypeStruct((n, d), F32),
        input_output_aliases={0: 0},
        compiler_params=_cparams("parallel"),
        name="sc_rows_residual",
    )(out_full, x, y_sc, mods)


SC_CORES = 2
SC_SUBCORES = 16
SC_LANES = 16
SC_CHUNK = 16
SC_SHARE = 0.3235

_ERF_ALPHA = (0.00022905065861350646, 0.0034082910107109506, 0.050955695062380861,
              0.18520832239976145, 1.128379143519084)
_ERF_BETA = (-1.1791602954361697e-7, 0.000023547966471313185, 0.0010179625278914885,
             0.014070470171167667, 0.11098505178285362, 0.49746925110067538, 1.0)
_ERF_CLAMP = 3.832506856900711


def _erf_rational(x):
    x = jnp.minimum(jnp.maximum(x, -_ERF_CLAMP), _ERF_CLAMP)
    x2 = x * x
    p = jnp.full_like(x, _ERF_ALPHA[0])
    for c in _ERF_ALPHA[1:]:
        p = p * x2 + c
    q = jnp.full_like(x, _ERF_BETA[0])
    for c in _ERF_BETA[1:]:
        q = q * x2 + c
    return x * p / q


def _peer_sc(eidx, gate, h, table, row0):
    n, nsel = eidx.shape
    d = h.shape[1]
    nw = SC_CORES * SC_SUBCORES
    assert n % (2 * nw) == 0 and nsel % SC_CHUNK == 0 and d % SC_LANES == 0
    tpw = n // nw
    nchunk = nsel // SC_CHUNK
    nvec = d // SC_LANES
    mesh = plsc.VectorSubcoreMesh(core_axis_name="c", subcore_axis_name="s",
                                  num_cores=SC_CORES, num_subcores=SC_SUBCORES)

    @functools.partial(
        pl.kernel, mesh=mesh,
        out_type=jax.ShapeDtypeStruct((n, d), F32),
        scratch_types=[
            pltpu.VMEM((nsel,), jnp.int32), pltpu.VMEM((nsel,), jnp.int32),
            pltpu.VMEM((nsel,), F32), pltpu.VMEM((nsel,), F32),
            pltpu.VMEM((d,), F32), pltpu.VMEM((d,), F32),
            pltpu.VMEM((d,), F32),
            pltpu.VMEM((SC_CHUNK, d), jnp.uint32),
            pltpu.VMEM((SC_CHUNK, d), jnp.uint32),
            pltpu.SemaphoreType.DMA, pltpu.SemaphoreType.DMA,
            pltpu.SemaphoreType.DMA, pltpu.SemaphoreType.DMA,
        ],
        compiler_params=pltpu.CompilerParams(needs_layout_passes=False),
        name="peer_sc",
    )
    def sc_kernel(eidx_hbm, gate_hbm, h_hbm, tab_hbm, y_hbm, idx_a, idx_b, gate_a, gate_b, h_a, h_b,
                  o_v, rows0, rows1, sem0, sem1, msem_a, msem_b):
        wid = lax.axis_index("s") * SC_CORES + lax.axis_index("c")
        lanes = lax.iota(jnp.int32, SC_LANES)
        zero = jnp.zeros((SC_LANES,), F32)
        bufs = ((rows0, sem0), (rows1, sem1))
        meta = ((idx_a, gate_a, h_a, msem_a), (idx_b, gate_b, h_b, msem_b))

        def meta_copies(tok, s):
            idx_v, gate_v, h_v, msem = meta[s]
            return (pltpu.make_async_copy(eidx_hbm.at[tok], idx_v, msem),
                    pltpu.make_async_copy(gate_hbm.at[tok], gate_v, msem),
                    pltpu.make_async_copy(h_hbm.at[row0 + tok], h_v, msem))

        def gather(s, c, slot):
            rows, sem = bufs[slot]
            return pltpu.make_async_copy(
                tab_hbm.at[meta[s][0].at[pl.ds(c * SC_CHUNK, SC_CHUNK)]], rows, sem)

        def chunk_compute(s, c, rows):
            _, gate_v, h_v, _ = meta[s]

            def dot_body(i, accs):
                hv = h_v[pl.ds(i * SC_LANES, SC_LANES)]
                out = []
                for e in range(SC_CHUNK):
                    w = rows[e, pl.ds(i * SC_LANES, SC_LANES)]
                    out.append(accs[e] + lax.bitcast_convert_type(w << 16, F32) * hv)
                return tuple(out)

            accs = lax.fori_loop(0, nvec, dot_body, (zero,) * SC_CHUNK)
            dots = zero
            for e in range(SC_CHUNK):
                dots = jnp.where(lanes == e, jnp.sum(accs[e]), dots)
            act = 0.5 * dots * (1.0 + _erf_rational(dots * (2.0 ** -0.5)))
            coef = gate_v[pl.ds(c * SC_CHUNK, SC_CHUNK)] * act
            splat = [jnp.full((SC_LANES,), jnp.sum(jnp.where(lanes == e, coef, 0.0)), F32)
                     for e in range(SC_CHUNK)]

            def acc_body(i, carry):
                o = o_v[pl.ds(i * SC_LANES, SC_LANES)]
                for e in range(SC_CHUNK):
                    w = rows[e, pl.ds(i * SC_LANES, SC_LANES)]
                    o = o + splat[e] * lax.bitcast_convert_type(w & jnp.uint32(0xFFFF0000), F32)
                o_v[pl.ds(i * SC_LANES, SC_LANES)] = o
                return carry

            lax.fori_loop(0, nvec, acc_body, 0)

        def zero_body(i, c2):
            o_v[pl.ds(i * SC_LANES, SC_LANES)] = zero
            return c2

        def token(tok, s, has_next):
            def when_next(fn):
                if isinstance(has_next, bool):
                    if has_next:
                        fn()
                else:
                    pl.when(has_next)(fn)

            def load_next():
                for cp in meta_copies(tok + 1, 1 - s):
                    cp.start()

            when_next(load_next)
            lax.fori_loop(0, nvec, zero_body, 0)
            for c in range(nchunk):
                if c + 1 < nchunk:
                    gather(s, c + 1, (c + 1) % 2).start()
                gather(s, c, c % 2).wait()
                chunk_compute(s, c, bufs[c % 2][0])

            def prefetch_next():
                for cp in meta_copies(tok + 1, 1 - s):
                    cp.wait()
                gather(1 - s, 0, 0).start()

            when_next(prefetch_next)
            pltpu.sync_copy(o_v, y_hbm.at[tok])

        base = wid * tpw
        for cp in meta_copies(base, 0):
            cp.start()
        for cp in meta_copies(base, 0):
            cp.wait()
        gather(0, 0, 0).start()

        def pair_body(g, carry):
            token(base + 2 * g, 0, True)
            token(base + 2 * g + 1, 1, g + 1 < tpw // 2)
            return carry

        lax.fori_loop(0, tpw // 2, pair_body, 0)

    return sc_kernel(eidx, gate, h, table)


def _pad_to(x, axis, size):
    pad = [(0, 0)] * x.ndim
    pad[axis] = (0, size - x.shape[axis])
    return jnp.pad(x, pad)


def _rwkv_layer(geo, xs, mods, norm1, mix, w_rkv, w_o, w0, w1, w2, a0, a1, a2, vl, g1, g2,
                k_k, k_a, r_k, ln_w, ln_b, vfirst):
    d = geo.d
    xm = _rwkv_mix(geo, xs, norm1, mods, mix)
    rkv = _bmm(xm, w_rkv.astype(BF16), (0, 2, 3), ("none",) * 3, F32, geo.tm)
    lt = g1.shape[1]
    w1c = jnp.concatenate([_pad_to(w1[0], 1, LORA_PAD), _pad_to(w1[1], 1, LORA_PAD)], axis=1)
    a1c = jnp.concatenate([_pad_to(a1[0], 1, LORA_PAD), _pad_to(a1[1], 1, LORA_PAD)], axis=1)
    if vl is None:
        v1p = jnp.zeros((d, lt), F32)
        v2p = jnp.zeros((LORA_PAD, d), F32)
        v0 = jnp.zeros((d,), F32)
    else:
        v0, v1, v2 = vl
        v1p = _pad_to(v1, 1, lt)
        v2p = _pad_to(v2, 0, LORA_PAD)
    wl1 = jnp.stack([_pad_to(w1c, 1, lt), _pad_to(a1c, 1, lt), g1, v1p]).astype(BF16)
    tl = _bmm(xm, wl1, (1, 4, 5, 3), ("tanh", "none", "sigmoid", "none"), BF16, geo.tm)
    w2p = jnp.stack([_pad_to(w2[0], 0, LORA_PAD), _pad_to(w2[1], 0, LORA_PAD)]).astype(BF16)
    a2p = jnp.stack([_pad_to(a2[0], 0, LORA_PAD), _pad_to(a2[1], 0, LORA_PAD)]).astype(BF16)
    pvec = jnp.stack([w0[0], w0[1], a0[0], a0[1], v0, k_k, k_a, jnp.zeros_like(k_k)])
    lw, kd, asg, kk, g, v = _rwkv_feat(geo, rkv, tl, w2p, a2p, g2.astype(BF16), v2p.astype(BF16),
                                      pvec, vfirst)
    r = (rkv, 0)
    wkv = _wkv_bidir(r, v, kk, lw, kd, asg, geo.b, geo.l)
    pv2 = _pad_to(jnp.stack([r_k, ln_w, ln_b]), 0, SUBLANES)
    y = _rwkv_readout(geo, wkv, r, kd, v, g, pv2)
    xs = _matmul_res(geo, y, w_o.astype(BF16), xs, mods, 2)
    return xs, v


def _rope_tables(geo):
    t = geo.t
    pos = jnp.arange(t)
    row = (pos // GRID_W).astype(F32)
    col = (pos % GRID_W).astype(F32)
    n_freq = QK_ROPE // 4
    inv_freq = ROPE_THETA ** (-jnp.arange(n_freq, dtype=F32) / n_freq)
    ang = jnp.concatenate([row[:, None] * inv_freq, col[:, None] * inv_freq], axis=-1)
    cos, sin = jnp.cos(ang), jnp.sin(ang)
    pad = LANES - QK_ROPE
    cos_l = jnp.concatenate([cos, cos, jnp.ones((t, pad), F32)], axis=1)
    sin_l = jnp.concatenate([-sin, sin, jnp.zeros((t, pad), F32)], axis=1)
    cos_c = jnp.ones((geo.l, LANES), F32)
    sin_c = jnp.zeros((geo.l, LANES), F32)
    return jnp.concatenate([cos_c, cos_l], axis=0), jnp.concatenate([sin_c, sin_l], axis=0)


def _mla_layer(geo, xs, mods, norm1, rope_t, w_in, q_norm, kv_norm, w_uq, w_ukv, g_q, g_k, w_o):
    q_lora, kv_lora = q_norm.shape[0], kv_norm.shape[0]
    zw = q_lora + kv_lora + LANES
    z = _mod_matmul(geo, xs, norm1, mods, _pad_to(w_in, 1, zw).astype(BF16), 0, False)
    qk = QK_NOPE + QK_ROPE
    wuq_p = _pad_to(w_uq.reshape(q_lora, MLA_HEADS, qk), 2, QK_PAD).reshape(q_lora, -1)
    gq_p = _pad_to(g_q, 0, QK_PAD).reshape(1, QK_PAD)
    gk_p = _pad_to(g_k, 0, QK_PAD).reshape(1, QK_PAD)
    q, k, v = _mla_qkv(geo, z, q_norm, kv_norm, wuq_p.astype(BF16), w_ukv.astype(BF16),
                       gq_p, gk_p, *rope_t)
    o = _mla_attention(geo, q, k, v)
    return _matmul_res(geo, o, w_o.astype(BF16), xs, mods, 2)


def _peer_layer(geo, geo_g, xs, mods, norm2, w_q, q_norm, keys, u, v):
    qp, h2 = _mod_matmul(geo, xs, norm2, mods, w_q.astype(BF16), 1, True)
    table = _pack_expert_table(u, v)
    n, d = xs.shape
    n_sc = geo.tm * round(SC_SHARE * n / geo.tm)
    if n_sc % (2 * SC_CORES * SC_SUBCORES) != 0:
        n_sc = 0
    n_tc = n - n_sc

    def select(row0, nrows):
        wide = 2 * geo.tm
        tsel = wide if row0 % wide == 0 and nrows % wide == 0 else geo.tm
        return _peer_select(qp, q_norm, keys, tsel, row0, nrows)

    if n_sc:
        eidx_sc, gate_sc = select(n_tc, n_sc)
        y_sc = _peer_sc(eidx_sc.T, gate_sc.T, h2, table.reshape(table.shape[0], d), n_tc)
    eidx_t, gate_t = select(0, n_tc)
    out_tc = _peer_gather(geo_g, eidx_t.T, gate_t, h2, xs, mods, table, n_tc)
    if n_sc == 0:
        return out_tc
    return _sc_rows_residual(geo, out_tc, xs, y_sc, mods, n_tc)


def kernel(x, c, ctx, c_ctx, w_ada, b_ada, norm1, norm2, rw_mix, rw_wrkv, rw_wo, rw_w0, rw_w1, rw_w2, rw_a0, rw_a1, rw_a2, rw_v0, rw_v1, rw_v2, rw_g1, rw_g2, rw_kk, rw_ka, rw_rk, rw_lnw, rw_lnb, mla_win, mla_qnorm, mla_kvnorm, mla_wuq, mla_wukv, mla_gq, mla_gk, mla_wo, peer_wq, peer_qnorm, peer_keys, peer_u, peer_v):
    b, t, d = x.shape
    l = ctx.shape[1]
    depth = w_ada.shape[0]
    geo = _Geom(b, l, t, d, min(ROW_TILE, l))
    geo_g = _Geom(b, l, t, d, min(GATHER_TILE, l))
    cond8 = _pad_to(jnp.concatenate([c, c_ctx[None, :]], axis=0), 0, SUBLANES)
    ada = _adaln(cond8, w_ada, b_ada).reshape(depth, SUBLANES, 6, d)
    mods_all = jnp.stack([jnp.broadcast_to(ada[:, b:b + 1], (depth, b, 6, d)), ada[:, 0:b]], axis=2)
    xs = jnp.concatenate([ctx, x], axis=1).reshape(b * (l + t), d)
    rope_t = _rope_tables(geo)
    vfirst = None
    for i in range(depth):
        j = i // 2
        mods = mods_all[i]
        if i % 2 == 0:
            vl = None if j == 0 else (rw_v0[j - 1], rw_v1[j - 1], rw_v2[j - 1])
            xs, vcur = _rwkv_layer(geo, xs, mods, norm1[i], rw_mix[j], rw_wrkv[j], rw_wo[j],
                                   rw_w0[j], rw_w1[j], rw_w2[j], rw_a0[j], rw_a1[j], rw_a2[j], vl,
                                   rw_g1[j], rw_g2[j], rw_kk[j], rw_ka[j], rw_rk[j], rw_lnw[j],
                                   rw_lnb[j], vfirst)
            if j == 0:
                vfirst = vcur
        else:
            xs = _mla_layer(geo, xs, mods, norm1[i], rope_t, mla_win[j], mla_qnorm[j],
                            mla_kvnorm[j], mla_wuq[j], mla_wukv[j], mla_gq[j], mla_gk[j], mla_wo[j])
        if i == depth - 1:
            xs = xs.reshape(b, l + t, d)[:, l:, :].reshape(b * t, d)
            geo, geo_g = _Geom(b, 0, t, d, geo.tm), _Geom(b, 0, t, d, geo_g.tm)
        xs = _peer_layer(geo, geo_g, xs, mods, norm2[i], peer_wq[i], peer_qnorm[i], peer_keys[i],
                         peer_u[i], peer_v[i])
    return xs.reshape(b, t, d)
```

```python
import functools
import math

import jax
import jax.numpy as jnp
from jax import lax
from jax.experimental import pallas as pl
from jax.experimental.pallas import tpu as pltpu
from jax.experimental.pallas import tpu_sc as plsc

F32 = jnp.float32
BF16 = jnp.bfloat16

EPS = 1e-6
GN_EPS = 64e-5
RW_HEAD = 64
WKV_CHUNK = 64
MLA_HEADS = 16
QK_NOPE = 128
QK_ROPE = 64
V_HEAD = 128
QK_PAD = 256
ROPE_THETA = 10000.0
GRID_W = 64
ATTN_SCALE = (QK_NOPE + QK_ROPE) ** -0.5
PEER_HEADS = 8
N_KEYS = 128
PEER_TOPK = 16
D_KEY = 256
LORA_PAD = 128

LANES = 128
SUBLANES = 8
VMEM_LIMIT = 56 * 1024 * 1024
ROW_TILE = 256
GATHER_TILE = 128
COL_TILE = 512
ADA_COLS = 1024


def _cparams(*sem):
    return pltpu.CompilerParams(dimension_semantics=sem, vmem_limit_bytes=VMEM_LIMIT)


def _dot(a, b):
    return jnp.dot(a, b, preferred_element_type=F32)


def _dot_nt(a, b):
    return lax.dot_general(a, b, (((1,), (1,)), ((), ())), preferred_element_type=F32)


def _split2(x):
    hi = x.astype(BF16)
    lo = (x - hi.astype(F32)).astype(BF16)
    return hi, lo


def _split3(x):
    hi = x.astype(BF16)
    r1 = x - hi.astype(F32)
    mid = r1.astype(BF16)
    lo = (r1 - mid.astype(F32)).astype(BF16)
    return hi, mid, lo


def _dot3(a, b):
    ah, al = _split2(a)
    bh, bl = _split2(b)
    return _dot(ah, bh) + (_dot(ah, bl) + _dot(al, bh))


def _dot3_nt(a, b):
    ah, al = _split2(a)
    bh, bl = _split2(b)
    return _dot_nt(ah, bh) + (_dot_nt(ah, bl) + _dot_nt(al, bh))


def _dot_exact_lhs(sel, x):
    hi, mid, lo = _split3(x)
    return _dot(sel, hi) + (_dot(sel, mid) + _dot(sel, lo))


def _modulate(x, g, shift, scale):
    ms = jnp.mean(x * x, axis=-1, keepdims=True)
    return (x * lax.rsqrt(ms + EPS) * g) * (1.0 + scale) + shift


def _sigmoid(x):
    return 1.0 / (1.0 + jnp.exp(-x))


def _softplus(y):
    return jnp.maximum(y, 0.0) + jnp.log(1.0 + jnp.exp(-jnp.abs(y)))


def _erf(x):
    return lax.erf(x)


def _gelu(x):
    return 0.5 * x * (1.0 + _erf(x * (2.0 ** -0.5)))


def _ada_kernel(s_ref, w_ref, b_ref, o_ref):
    s = s_ref[...]
    s = s * _sigmoid(s)
    o_ref[...] = _dot3(s, w_ref[...]) + b_ref[...]


def _adaln(cond8, w_ada, b_ada):
    depth, d, n = w_ada.shape
    tn = min(ADA_COLS, n)
    return pl.pallas_call(
        _ada_kernel,
        grid=(depth, n // tn),
        in_specs=[
            pl.BlockSpec((SUBLANES, d), lambda l, j: (0, 0)),
            pl.BlockSpec((None, d, tn), lambda l, j: (l, 0, j)),
            pl.BlockSpec((None, 1, tn), lambda l, j: (l, 0, j)),
        ],
        out_specs=pl.BlockSpec((None, SUBLANES, tn), lambda l, j: (l, 0, j)),
        out_shape=jax.ShapeDtypeStruct((depth, SUBLANES, n), F32),
        compiler_params=_cparams("parallel", "parallel"),
        name="adaln",
    )(cond8, w_ada, b_ada.reshape(depth, 1, n))


class _Geom:
    def __init__(self, batch, ctx_len, seq_len, d_model, tm):
        self.b, self.l, self.t, self.d = batch, ctx_len, seq_len, d_model
        self.s = ctx_len + seq_len
        self.n = batch * self.s
        self.tm = tm
        assert ctx_len % tm == 0 and seq_len % tm == 0
        self.tpb = self.s // tm
        self.nct = ctx_len // tm
        self.ntiles = self.n // tm

    def mod_spec(self):
        tpb, nct = self.tpb, self.nct

        def imap(i):
            return (i // tpb, ((i % tpb) >= nct).astype(jnp.int32), 0, 0)

        return pl.BlockSpec((None, None, 6, self.d), imap)


def _modmm_kernel(x_ref, g_ref, mod_ref, w_ref, o_ref, *h_ref, which):
    h = _modulate(x_ref[...], g_ref[...], mod_ref[3 * which:3 * which + 1, :],
                  mod_ref[3 * which + 1:3 * which + 2, :])
    if h_ref:
        h_ref[0][...] = h
    o_ref[...] = _dot(h.astype(BF16), w_ref[...])


def _mod_matmul(geo, x, g, mods, w, which, emit_h):
    n, d = x.shape
    nn = w.shape[1]
    tm = geo.tm
    out_shape = [jax.ShapeDtypeStruct((n, nn), F32)]
    out_specs = [pl.BlockSpec((tm, nn), lambda i: (i, 0))]
    if emit_h:
        out_shape.append(jax.ShapeDtypeStruct((n, d), F32))
        out_specs.append(pl.BlockSpec((tm, d), lambda i: (i, 0)))
    res = pl.pallas_call(
        functools.partial(_modmm_kernel, which=which),
        grid=(geo.ntiles,),
        in_specs=[
            pl.BlockSpec((tm, d), lambda i: (i, 0)),
            pl.BlockSpec((1, d), lambda i: (0, 0)),
            geo.mod_spec(),
            pl.BlockSpec((d, nn), lambda i: (0, 0)),
        ],
        out_specs=out_specs,
        out_shape=out_shape,
        compiler_params=_cparams("parallel"),
        name="mod_matmul",
    )(x, g.reshape(1, d), mods, w)
    return res if emit_h else res[0]


def _mmres_kernel(y_ref, w_ref, x_ref, mod_ref, o_ref, *, gidx):
    acc = _dot(y_ref[...], w_ref[...])
    o_ref[...] = x_ref[...] + mod_ref[gidx:gidx + 1, :] * acc


def _matmul_res(geo, y, w, x, mods, gidx):
    n, k = y.shape
    d = x.shape[1]
    tm = geo.tm
    return pl.pallas_call(
        functools.partial(_mmres_kernel, gidx=gidx),
        grid=(geo.ntiles,),
        in_specs=[
            pl.BlockSpec((tm, k), lambda i: (i, 0)),
            pl.BlockSpec((k, d), lambda i: (0, 0)),
            pl.BlockSpec((tm, d), lambda i: (i, 0)),
            geo.mod_spec(),
        ],
        out_specs=pl.BlockSpec((tm, d), lambda i: (i, 0)),
        out_shape=jax.ShapeDtypeStruct((n, d), F32),
        compiler_params=_cparams("parallel"),
        name="matmul_res",
    )(y, w, x, mods)


def _bmm_kernel(x_ref, w_ref, o_ref, *, acts):
    j = pl.program_id(0)
    y = _dot(x_ref[...], w_ref[...])
    out = y
    for jj, a in enumerate(acts):
        if a == "tanh":
            out = jnp.where(j == jj, jnp.tanh(y), out)
        elif a == "sigmoid":
            out = jnp.where(j == jj, _sigmoid(y), out)
    o_ref[...] = out.astype(o_ref.dtype)


def _bmm(x3, w3, src, acts, out_dtype, tm):
    _, n, k = x3.shape
    nj, _, nn = w3.shape
    src = tuple(src)

    def xmap(j, i):
        idx = jnp.int32(src[0])
        for jj in range(1, nj):
            idx = jnp.where(j == jj, jnp.int32(src[jj]), idx)
        return (idx, i, 0)

    return pl.pallas_call(
        functools.partial(_bmm_kernel, acts=tuple(acts)),
        grid=(nj, n // tm),
        in_specs=[
            pl.BlockSpec((None, tm, k), xmap),
            pl.BlockSpec((None, k, nn), lambda j, i: (j, 0, 0)),
        ],
        out_specs=pl.BlockSpec((None, tm, nn), lambda j, i: (j, i, 0)),
        out_shape=jax.ShapeDtypeStruct((nj, n, nn), out_dtype),
        compiler_params=_cparams("parallel", "parallel"),
        name="bmm",
    )(x3, w3)


def _rwmix_kernel(x_ref, xp_ref, xn_ref, g_ref, mod_ref, mix_ref, o_ref, *, tpb, nct):
    i = pl.program_id(0)
    tm = x_ref.shape[0]
    g = g_ref[...]
    shift = mod_ref[0:1, :]
    scale = mod_ref[1:2, :]
    h = _modulate(x_ref[...], g, shift, scale)
    hp = _modulate(xp_ref[...], g, shift, scale)[SUBLANES - 1:SUBLANES, :]
    hn = _modulate(xn_ref[...], g, shift, scale)[0:1, :]
    it = i % tpb
    first = jnp.logical_or(it == 0, it == nct)
    last = jnp.logical_or(it == nct - 1, it == tpb - 1)
    hp = jnp.where(first, 0.0, hp)
    hn = jnp.where(last, 0.0, hn)
    rows = lax.broadcasted_iota(jnp.int32, h.shape, 0)
    prev = jnp.where(rows == 0, hp, pltpu.roll(h, 1, axis=0))
    nxt = jnp.where(rows == tm - 1, hn, pltpu.roll(h, tm - 1, axis=0))
    xx = 0.5 * (prev + nxt) - h
    for m in range(6):
        o_ref[m] = (h + xx * mix_ref[m:m + 1, :]).astype(BF16)


def _rwkv_mix(geo, x, g, mods, mix):
    n, d = x.shape
    tm = geo.tm
    r8 = tm // SUBLANES
    nblk8 = n // SUBLANES
    return pl.pallas_call(
        functools.partial(_rwmix_kernel, tpb=geo.tpb, nct=geo.nct),
        grid=(geo.ntiles,),
        in_specs=[
            pl.BlockSpec((tm, d), lambda i: (i, 0)),
            pl.BlockSpec((SUBLANES, d), lambda i: (jnp.maximum(i * r8 - 1, 0), 0)),
            pl.BlockSpec((SUBLANES, d), lambda i: (jnp.minimum((i + 1) * r8, nblk8 - 1), 0)),
            pl.BlockSpec((1, d), lambda i: (0, 0)),
            geo.mod_spec(),
            pl.BlockSpec((6, d), lambda i: (0, 0)),
        ],
        out_specs=pl.BlockSpec((6, tm, d), lambda i: (0, i, 0)),
        out_shape=jax.ShapeDtypeStruct((6, n, d), BF16),
        compiler_params=_cparams("parallel"),
        name="rwkv_mix",
    )(x, x, x, g.reshape(1, d), mods, mix)


def _head_sum(x, bd):
    return _dot(x.astype(BF16), bd)


def _rwfeat_kernel(k_ref, v_ref, tl_ref, w2_ref, a2_ref, g2_ref, v2_ref, pv_ref, bd_ref,
                   *rest, has_vlora):
    if has_vlora:
        vf_ref, lw_ref, kd_ref, as_ref, kk_ref, g_ref, vo_ref = rest
    else:
        lw_ref, kd_ref, as_ref, kk_ref, g_ref = rest
    k = k_ref[...]
    tw = tl_ref[0]
    ta = tl_ref[1]
    tg = tl_ref[2]
    w0 = pv_ref[0:2, :]
    a0 = pv_ref[2:4, :]
    k_k = pv_ref[5:6, :]
    k_a = pv_ref[6:7, :]
    for z in range(2):
        sl = slice(LORA_PAD * z, LORA_PAD * (z + 1))
        lora_w = _dot(tw[:, sl], w2_ref[z])
        w = -_softplus(-(w0[z:z + 1, :] + lora_w)) - 0.5
        lw_ref[z] = -jnp.exp(w)
        a_sig = _sigmoid(a0[z:z + 1, :] + _dot(ta[:, sl], a2_ref[z]))
        as_ref[z] = a_sig.astype(BF16)
        kd_ref[z] = k * (1.0 + (a_sig - 1.0) * k_a)
    g_ref[...] = _dot(tg, g2_ref[...]).astype(BF16)
    kkr = k * k_k
    ss = _head_sum(kkr * kkr, bd_ref[...])
    kk_ref[...] = (kkr * lax.rsqrt(ss + 1e-12)).astype(BF16)
    if has_vlora:
        v = v_ref[...]
        tv = tl_ref[3]
        gate = _sigmoid(pv_ref[4:5, :] + _dot(tv[:, 0:LORA_PAD], v2_ref[...]))
        vo_ref[...] = v + (vf_ref[...] - v) * gate


def _rwkv_feat(geo, rkv, tl, w2p, a2p, g2, v2p, pvec, vfirst):
    _, n, d = rkv.shape
    tm, tc = geo.tm, COL_TILE
    has_vlora = vfirst is not None
    lt = tl.shape[2]
    ii = lax.broadcasted_iota(jnp.int32, (tc, tc), 0) // RW_HEAD
    jj = lax.broadcasted_iota(jnp.int32, (tc, tc), 1) // RW_HEAD
    bd = (ii == jj).astype(BF16)
    row = lambda i, j: (i, j)
    in_specs = [
        pl.BlockSpec((None, tm, tc), lambda i, j: (1, i, j)),
        pl.BlockSpec((None, tm, tc), lambda i, j: (2, i, j)),
        pl.BlockSpec((4, tm, lt), lambda i, j: (0, i, 0)),
        pl.BlockSpec((2, LORA_PAD, tc), lambda i, j: (0, 0, j)),
        pl.BlockSpec((2, LORA_PAD, tc), lambda i, j: (0, 0, j)),
        pl.BlockSpec((lt, tc), lambda i, j: (0, j)),
        pl.BlockSpec((LORA_PAD, tc), lambda i, j: (0, j)),
        pl.BlockSpec((SUBLANES, tc), lambda i, j: (0, j)),
        pl.BlockSpec((tc, tc), lambda i, j: (0, 0)),
    ]
    args = [rkv, rkv, tl, w2p, a2p, g2, v2p, pvec, bd]
    dir_spec = pl.BlockSpec((2, tm, tc), lambda i, j: (0, i, j))
    out_specs = [dir_spec, dir_spec, dir_spec, pl.BlockSpec((tm, tc), row), pl.BlockSpec((tm, tc), row)]
    out_shape = [jax.ShapeDtypeStruct((2, n, d), dt) for dt in (F32, F32, BF16)]
    out_shape += [jax.ShapeDtypeStruct((n, d), BF16)] * 2
    if has_vlora:
        vf_arr, vf_spec = _plane(vfirst, (tm, tc), row)
        in_specs.append(vf_spec)
        args.append(vf_arr)
        out_specs.append(pl.BlockSpec((tm, tc), row))
        out_shape.append(jax.ShapeDtypeStruct((n, d), F32))
    res = pl.pallas_call(
        functools.partial(_rwfeat_kernel, has_vlora=has_vlora),
        grid=(geo.ntiles, d // tc),
        in_specs=in_specs,
        out_specs=out_specs,
        out_shape=out_shape,
        compiler_params=_cparams("parallel", "parallel"),
        name="rwkv_feat",
    )(*args)
    if has_vlora:
        lw, kd, asg, kk, g, v = res
    else:
        lw, kd, asg, kk, g = res
        v = (rkv, 2)
    return lw, kd, asg, kk, g, v


def _wkv_chunk_kernel(r_ref, v_ref, kk_ref, lw_ref, kd_ref, as_ref,
                      m_ref, ga_ref, rq_ref, o0_ref, pc_ref, *, npairs):
    c = WKV_CHUNK
    c2 = 2 * c
    sgn = 1 - 2 * pl.program_id(0)
    ri = lax.broadcasted_iota(jnp.int32, (c2, c2), 0)
    ci = lax.broadcasted_iota(jnp.int32, (c2, c2), 1)
    same = (ri >= c) == (ci >= c)
    tt = jnp.where(ri >= c, ri - c, ri)
    ss = jnp.where(ci >= c, ci - c, ci)
    earlier = (ss - tt) * sgn < 0
    strict = jnp.logical_and(same, earlier)
    incl = jnp.logical_and(same, jnp.logical_or(earlier, ss == tt))
    eye = (ri == ci).astype(F32)
    r64 = lax.broadcasted_iota(jnp.int32, (c, c), 0)
    c64 = lax.broadcasted_iota(jnp.int32, (c, c), 1)
    ltri = jnp.where((c64 - r64) * sgn <= 0, 1.0, 0.0).astype(BF16)
    head0 = lax.broadcasted_iota(jnp.int32, (c, LANES), 1) < RW_HEAD
    pairs = range(npairs)

    def stack(x):
        return jnp.concatenate([jnp.where(head0, x, 0.0), jnp.where(head0, 0.0, x)], axis=0)

    def dup(x):
        return jnp.concatenate([x, x], axis=0)

    def bf(x):
        return x.astype(BF16)

    lhs, rhs, a2, bp2, kp2, vst, r2 = [], [], [], [], [], [], []
    lw_all = lw_ref[...]
    cum_all = _dot_exact_lhs(ltri, lw_all)
    for p in pairs:
        sl = slice(LANES * p, LANES * (p + 1))
        lw = lw_all[:, sl]
        cum = cum_all[:, sl]
        tot = jnp.sum(lw, axis=0, keepdims=True)
        p_inv = jnp.exp(-cum)
        p_end = jnp.exp(tot - cum)
        kk = kk_ref[:, sl]
        b = kk * as_ref[:, sl]
        kd = kd_ref[:, sl]
        a2p = stack(-kk * jnp.exp(cum - lw))
        r2p = stack(r_ref[:, sl] * jnp.exp(cum))
        a2.append(bf(a2p))
        r2.append(r2p)
        lhs.append(jnp.concatenate([a2[p], bf(r2p)], axis=0))
        rhs.append(jnp.concatenate([dup(bf(b * p_inv)), dup(bf(kd * p_inv))], axis=0))
        bp2.append(bf(stack(b * p_end)))
        kp2.append(bf(stack(kd * p_end)))
        vst.append(bf(stack(v_ref[:, sl])))
        pc_ref[:, sl] = jnp.broadcast_to(jnp.exp(tot), (SUBLANES, LANES))
    gram = [_dot_nt(lhs[p], rhs[p]) for p in pairs]
    nab = [jnp.where(strict, gram[p][0:c2, 0:c2], 0.0) for p in pairs]
    nrb = [bf(jnp.where(incl, gram[p][c2:2 * c2, 0:c2], 0.0)) for p in pairs]
    nk = [bf(jnp.concatenate([jnp.where(strict, gram[p][0:c2, c2:2 * c2], 0.0),
                              jnp.where(incl, gram[p][c2:2 * c2, c2:2 * c2], 0.0)], axis=0))
          for p in pairs]
    quads = range(npairs // 2)

    def side(x0, x1):
        return jnp.concatenate([x0, x1], axis=1)

    def diag(x0, x1):
        z0 = jnp.zeros_like(x0)
        return jnp.concatenate([side(x0, z0), side(z0, x1)], axis=0)

    def diag_halves(x):
        return diag(x[:, 0:c2], x[:, c2:2 * c2])

    def unside(xs):
        return [xs[p // 2][:, c2 * (p % 2):c2 * (p % 2 + 1)] for p in pairs]

    xv = unside([_dot(side(nk[2 * q], nk[2 * q + 1]), diag(vst[2 * q], vst[2 * q + 1]))
                 for q in quads])
    tinv = [side(eye + nab[2 * q], eye + nab[2 * q + 1]) for q in quads]
    npow = [bf(side(nab[2 * q], nab[2 * q + 1])) for q in quads]
    for _ in range(int(math.log2(c)) - 1):
        npow = [bf(_dot(npow[q], diag_halves(npow[q]))) for q in quads]
        tinv = [tinv[q] + _dot(bf(tinv[q]), diag_halves(npow[q])) for q in quads]
    tinv = unside(tinv)
    y = [_dot(bf(tinv[p]), jnp.concatenate([a2[p], bf(xv[p][0:c2, :])], axis=1)) for p in pairs]
    yb = [bf(y[p]) for p in pairs]
    z = [_dot(nrb[p], yb[p]) for p in pairs]
    mg = unside([_dot(bf(side(y[2 * q].T, y[2 * q + 1].T)), diag(bp2[2 * q], bp2[2 * q + 1]))
                 for q in quads])
    vk = unside([_dot(bf(side(vst[2 * q].astype(F32).T, vst[2 * q + 1].astype(F32).T)),
                      diag(kp2[2 * q], kp2[2 * q + 1])) for q in quads])
    for p in pairs:
        sl = slice(LANES * p, LANES * (p + 1))
        m_ref[:, sl] = bf(mg[p][0:c2, :])
        ga_ref[:, sl] = bf(mg[p][c2:2 * c2, :] + vk[p])
        rq_ref[:, sl] = bf(r2[p] + z[p][:, 0:c2])
        o0 = z[p][:, c2:2 * c2] + xv[p][c2:2 * c2, :]
        o0_ref[:, sl] = bf(o0[0:c, :] + o0[c:c2, :])


def _wkv_scan_kernel(m_ref, ga_ref, rq_ref, o0_ref, pc_ref, o_ref, g_scr, *, npairs):
    c = WKV_CHUNK

    @pl.when(pl.program_id(2) == 0)
    def _():
        g_scr[...] = jnp.zeros_like(g_scr)

    for p in range(npairs):
        sl = slice(LANES * p, LANES * (p + 1))
        g = g_scr[p]
        g_hi = g.astype(BF16)
        o_st = _dot_nt(rq_ref[:, sl], g_hi)
        o_ref[:, sl] = o_st[0:c, :] + o_st[c:2 * c, :] + o0_ref[:, sl]
        m = m_ref[:, sl]
        g_scr[p] = g * pc_ref[0:1, sl] + _dot(g_hi, m) + ga_ref[:, sl]


WKV_CHUNK_LANES = 2048


def _plane(x, block, imap):
    if not isinstance(x, tuple):
        return x, pl.BlockSpec(block, imap)
    stack, k = x
    return stack, pl.BlockSpec((None,) + block, lambda *ids: (k,) + tuple(imap(*ids)))


def _wkv_bidir(r, v, kk, lw, kd, asg, batch, ctx_len):
    n, d = kk.shape
    c = WKV_CHUNK
    s = n // batch
    ncs = s // c
    ncc = ctx_len // c
    nch = n // c
    lanes = min(WKV_CHUNK_LANES, d)
    ngrp = d // lanes
    rowmap = lambda z, i, j: (i, j)
    r, r_spec = _plane(r, (c, lanes), rowmap)
    v, v_spec = _plane(v, (c, lanes), rowmap)
    perdir = pl.BlockSpec((None, c, lanes), lambda z, i, j: (z, i, j))
    big = pl.BlockSpec((None, 2 * c, lanes), lambda z, i, j: (z, i, j))
    m_, ga_, rq_, o0_, pc_ = pl.pallas_call(
        functools.partial(_wkv_chunk_kernel, npairs=lanes // LANES),
        grid=(2, nch, ngrp),
        in_specs=[r_spec, v_spec, pl.BlockSpec((c, lanes), rowmap), perdir, perdir, perdir],
        out_specs=[big, big, big, perdir,
                   pl.BlockSpec((None, SUBLANES, lanes), lambda z, i, j: (z, i, j))],
        out_shape=[jax.ShapeDtypeStruct((2, nch * 2 * c, d), BF16)] * 3
        + [jax.ShapeDtypeStruct((2, n, d), BF16),
           jax.ShapeDtypeStruct((2, nch * SUBLANES, d), F32)],
        compiler_params=_cparams("parallel", "parallel", "parallel"),
        name="wkv_chunk",
    )(r, v, kk, lw, kd, asg)

    def cmap(z, b, cc):
        back = jnp.where(cc < ncc, ncc - 1 - cc, ncs + ncc - 1 - cc)
        return (z, b * ncs + jnp.where(z == 0, cc, back), 0)

    out = pl.pallas_call(
        functools.partial(_wkv_scan_kernel, npairs=d // LANES),
        grid=(2, batch, ncs),
        in_specs=[
            pl.BlockSpec((None, 2 * c, d), cmap),
            pl.BlockSpec((None, 2 * c, d), cmap),
            pl.BlockSpec((None, 2 * c, d), cmap),
            pl.BlockSpec((None, c, d), cmap),
            pl.BlockSpec((None, SUBLANES, d), cmap),
        ],
        out_specs=pl.BlockSpec((None, c, d), cmap),
        out_shape=jax.ShapeDtypeStruct((2, n, d), F32),
        scratch_shapes=[pltpu.VMEM((d // LANES, 2 * c, 2 * c), F32)],
        compiler_params=_cparams("parallel", "parallel", "arbitrary"),
        name="wkv_scan",
    )(m_, ga_, rq_, o0_, pc_)
    return out


def _rwread_kernel(o_ref, r_ref, kd_ref, v_ref, g_ref, pv_ref, bd_ref, y_ref):
    bd = bd_ref[...]
    wkv = o_ref[0] + o_ref[1]
    inv = 1.0 / RW_HEAD
    mu = _head_sum(wkv, bd) * inv
    dev = wkv - mu
    var = _head_sum(dev * dev, bd) * inv
    y = dev * lax.rsqrt(var + GN_EPS) * pv_ref[1:2, :] + pv_ref[2:3, :]
    rk = r_ref[...] * (kd_ref[0] + kd_ref[1]) * pv_ref[0:1, :]
    y = y + _head_sum(rk, bd) * v_ref[...]
    y_ref[...] = (y * g_ref[...]).astype(BF16)


def _rwkv_readout(geo, wkv, r, kd, v, g, pvec):
    n, d = g.shape
    tm, tc = geo.tm, COL_TILE
    ii = lax.broadcasted_iota(jnp.int32, (tc, tc), 0) // RW_HEAD
    jj = lax.broadcasted_iota(jnp.int32, (tc, tc), 1) // RW_HEAD
    bd = (ii == jj).astype(BF16)
    row = pl.BlockSpec((tm, tc), lambda i, j: (i, j))
    dirs = pl.BlockSpec((2, tm, tc), lambda i, j: (0, i, j))
    r, r_spec = _plane(r, (tm, tc), lambda i, j: (i, j))
    v, v_spec = _plane(v, (tm, tc), lambda i, j: (i, j))
    return pl.pallas_call(
        _rwread_kernel,
        grid=(geo.ntiles, d // tc),
        in_specs=[dirs, r_spec, dirs, v_spec, row,
                  pl.BlockSpec((SUBLANES, tc), lambda i, j: (0, j)),
                  pl.BlockSpec((tc, tc), lambda i, j: (0, 0))],
        out_specs=row,
        out_shape=jax.ShapeDtypeStruct((n, d), BF16),
        compiler_params=_cparams("parallel", "parallel"),
        name="rwkv_readout",
    )(wkv, r, kd, v, g, pvec, bd)


def _mlaqkv_kernel(z_ref, qn_ref, kvn_ref, wuq_ref, wukv_ref, gq_ref, gk_ref, cos_ref, sin_ref,
                   q_ref, k_ref, v_ref, *, q_lora, kv_lora):
    z = z_ref[...]
    cq = z[:, 0:q_lora]
    ckv = z[:, q_lora:q_lora + kv_lora]
    krot = z[:, q_lora + kv_lora:q_lora + kv_lora + LANES]

    def rms(x, g):
        ms = jnp.mean(x * x, axis=-1, keepdims=True)
        return x * lax.rsqrt(ms + EPS) * g

    qf = _dot(rms(cq, qn_ref[...]).astype(BF16), wuq_ref[...])
    kvf = _dot(rms(ckv, kvn_ref[...]).astype(BF16), wukv_ref[...])
    cos = cos_ref[...]
    sin = sin_ref[...]
    half = QK_ROPE // 2
    lane = lax.broadcasted_iota(jnp.int32, cos.shape, 1)

    def rope(x):
        up = pltpu.roll(x, LANES - half, axis=1)
        dn = pltpu.roll(x, half, axis=1)
        return x * cos + jnp.where(lane < half, up, dn) * sin

    inv_w = 1.0 / (QK_NOPE + QK_ROPE)
    gq = gq_ref[...]
    gk = gk_ref[...]
    kr_ss = jnp.sum(krot * krot, axis=-1, keepdims=True)
    for h in range(MLA_HEADS):
        o = QK_PAD * h
        qh = qf[:, o:o + QK_PAD]
        rs = lax.rsqrt(jnp.sum(qh * qh, axis=-1, keepdims=True) * inv_w + EPS)
        qn = qh * rs * gq * (ATTN_SCALE * math.log2(math.e))
        q_ref[:, o:o + QK_NOPE] = qn[:, 0:QK_NOPE].astype(BF16)
        q_ref[:, o + QK_NOPE:o + QK_PAD] = rope(qn[:, QK_NOPE:QK_PAD]).astype(BF16)
        kn = kvf[:, o:o + QK_NOPE]
        rsk = lax.rsqrt((jnp.sum(kn * kn, axis=-1, keepdims=True) + kr_ss) * inv_w + EPS)
        k_ref[:, o:o + QK_NOPE] = (kn * rsk * gk[:, 0:QK_NOPE]).astype(BF16)
        k_ref[:, o + QK_NOPE:o + QK_PAD] = rope(krot * rsk * gk[:, QK_NOPE:QK_PAD]).astype(BF16)
        v_ref[:, V_HEAD * h:V_HEAD * (h + 1)] = kvf[:, o + QK_NOPE:o + QK_PAD].astype(BF16)


def _mla_qkv(geo, z, qn, kvn, wuq_p, wukv, gq_p, gk_p, cos_t, sin_t):
    n, zw = z.shape
    tm = geo.tm
    q_lora, kv_lora = qn.shape[0], kvn.shape[0]
    hq = MLA_HEADS * QK_PAD
    tpb = geo.tpb
    full = lambda i: (0, 0)
    rowmap = lambda i: (i, 0)
    return pl.pallas_call(
        functools.partial(_mlaqkv_kernel, q_lora=q_lora, kv_lora=kv_lora),
        grid=(geo.ntiles,),
        in_specs=[
            pl.BlockSpec((tm, zw), rowmap),
            pl.BlockSpec((1, q_lora), full),
            pl.BlockSpec((1, kv_lora), full),
            pl.BlockSpec((q_lora, hq), full),
            pl.BlockSpec((kv_lora, hq), full),
            pl.BlockSpec((1, QK_PAD), full),
            pl.BlockSpec((1, QK_PAD), full),
            pl.BlockSpec((tm, LANES), lambda i: (i % tpb, 0)),
            pl.BlockSpec((tm, LANES), lambda i: (i % tpb, 0)),
        ],
        out_specs=[pl.BlockSpec((tm, hq), rowmap), pl.BlockSpec((tm, hq), rowmap),
                   pl.BlockSpec((tm, MLA_HEADS * V_HEAD), rowmap)],
        out_shape=[jax.ShapeDtypeStruct((n, hq), BF16), jax.ShapeDtypeStruct((n, hq), BF16),
                   jax.ShapeDtypeStruct((n, MLA_HEADS * V_HEAD), BF16)],
        compiler_params=_cparams("parallel"),
        name="mla_qkv",
    )(z, qn.reshape(1, -1), kvn.reshape(1, -1), wuq_p, wukv, gq_p, gk_p, cos_t, sin_t)


ATTN_SUBTILES = 2


def _attn_kernel(q_ref, k_ref, vt_ref, o_ref):
    k = k_ref[...]
    vt = vt_ref[...]
    cols = q_ref.shape[0] // ATTN_SUBTILES
    subs = range(ATTN_SUBTILES)
    s = [_dot_nt(k, q_ref[cols * i:cols * (i + 1), :]) for i in subs]
    m = [jnp.max(s[i], axis=0, keepdims=True) for i in subs]
    p = [jnp.exp2(s[i] - m[i]) for i in subs]
    l = [jnp.sum(p[i], axis=0, keepdims=True) for i in subs]
    for i in subs:
        ot = _dot(vt, p[i].astype(BF16)) / l[i]
        o_ref[cols * i:cols * (i + 1), :] = ot.T.astype(BF16)


def _attn_call(q3, k3, vt4, tq, nkeys):
    b, sq, _ = q3.shape
    return pl.pallas_call(
        _attn_kernel,
        grid=(b, MLA_HEADS, sq // tq),
        in_specs=[
            pl.BlockSpec((None, tq, QK_PAD), lambda bb, h, i: (bb, i, h)),
            pl.BlockSpec((None, nkeys, QK_PAD), lambda bb, h, i: (bb, 0, h)),
            pl.BlockSpec((None, None, V_HEAD, nkeys), lambda bb, h, i: (bb, h, 0, 0)),
        ],
        out_specs=pl.BlockSpec((None, tq, V_HEAD), lambda bb, h, i: (bb, i, h)),
        out_shape=jax.ShapeDtypeStruct((b, sq, MLA_HEADS * V_HEAD), BF16),
        compiler_params=_cparams("parallel", "parallel", "parallel"),
        name="mla_attention",
    )(q3, k3, vt4)


def _mla_attention(geo, q, k, v):
    b, s, l = geo.b, geo.s, geo.l
    q3 = q.reshape(b, s, MLA_HEADS * QK_PAD)
    k3 = k.reshape(b, s, MLA_HEADS * QK_PAD)
    vt4 = v.reshape(b, s, MLA_HEADS, V_HEAD).transpose(0, 2, 3, 1)
    o_ctx = _attn_call(q3[:, :l], k3, vt4, geo.tm, l)
    tq_lat = 2 * geo.tm if geo.t % (2 * geo.tm) == 0 else geo.tm
    o_lat = _attn_call(q3[:, l:], k3, vt4, tq_lat, s)
    return jnp.concatenate([o_ctx, o_lat], axis=1).reshape(b * s, MLA_HEADS * V_HEAD)


def _topk_rows(s, k, payload=None):
    rows = lax.broadcasted_iota(jnp.int32, s.shape, 0).astype(F32)
    big = float(s.shape[0])
    vals, idxs = [], []
    for _ in range(k):
        m = jnp.max(s, axis=0, keepdims=True)
        idx = jnp.min(jnp.where(s == m, rows, big), axis=0, keepdims=True)
        hit = rows == idx
        vals.append(m)
        if payload is None:
            idxs.append(idx)
        else:
            idxs.append(jnp.sum(jnp.where(hit, payload, 0.0), axis=0, keepdims=True))
        s = jnp.where(hit, -jnp.inf, s)
    return jnp.concatenate(vals, axis=0), jnp.concatenate(idxs, axis=0)


def _peersel_kernel(q_ref, qn_ref, keys_ref, e_ref, g_ref):
    q = q_ref[...]
    ms = jnp.mean(q * q, axis=-1, keepdims=True)
    qn = q * lax.rsqrt(ms + EPS) * qn_ref[...]
    half = D_KEY // 2
    s1 = _dot3_nt(keys_ref[0], qn[:, 0:half])
    s2 = _dot3_nt(keys_ref[1], qn[:, half:D_KEY])
    t1, i1 = _topk_rows(s1, PEER_TOPK)
    t2, i2 = _topk_rows(s2, PEER_TOPK)
    k = PEER_TOPK
    sub = lax.broadcasted_iota(jnp.int32, (SUBLANES, t1.shape[1]), 0)
    cand = [t1[0:1, :] + t2]
    cidx = [i1[0:1, :] * float(N_KEYS) + i2]
    for p in range(1, k // 2):
        live = sub < k // (p + 1)
        cand.append(jnp.where(live, t1[p:p + 1, :] + t2[0:SUBLANES, :], -jnp.inf))
        cidx.append(i1[p:p + 1, :] * float(N_KEYS) + i2[0:SUBLANES, :])
    cand.append(t1[k // 2:k, :] + t2[0:1, :])
    cidx.append(i1[k // 2:k, :] * float(N_KEYS) + i2[0:1, :])
    best, eidx = _topk_rows(jnp.concatenate(cand, axis=0), k, payload=jnp.concatenate(cidx, axis=0))
    ex = jnp.exp(best - jnp.max(best, axis=0, keepdims=True))
    g_ref[...] = ex / jnp.sum(ex, axis=0, keepdims=True)
    e_ref[...] = eidx.astype(jnp.int32)


def _peer_select(qp, q_norm, keys, tm, row0, n):
    assert row0 % tm == 0 and n % tm == 0
    tile0 = row0 // tm
    return pl.pallas_call(
        _peersel_kernel,
        grid=(n // tm, PEER_HEADS),
        in_specs=[
            pl.BlockSpec((tm, D_KEY), lambda i, h: (tile0 + i, h)),
            pl.BlockSpec((1, D_KEY), lambda i, h: (0, 0)),
            pl.BlockSpec((2, N_KEYS, D_KEY // 2), lambda i, h: (0, 0, 0)),
        ],
        out_specs=[pl.BlockSpec((PEER_TOPK, tm), lambda i, h: (h, i)),
                   pl.BlockSpec((PEER_TOPK, tm), lambda i, h: (h, i))],
        out_shape=[jax.ShapeDtypeStruct((PEER_HEADS * PEER_TOPK, n), jnp.int32),
                   jax.ShapeDtypeStruct((PEER_HEADS * PEER_TOPK, n), F32)],
        compiler_params=_cparams("parallel", "parallel"),
        name="peer_select",
    )(qp, q_norm.reshape(1, D_KEY), keys)


GATHER_SLOTS = 4
SLAB_PAD = 1


def _pack_expert_table(u, v):
    ne, d = u.shape
    ub = lax.bitcast_convert_type(u.astype(BF16), jnp.uint16).astype(jnp.uint32)
    vb = lax.bitcast_convert_type(v.astype(BF16), jnp.uint16).astype(jnp.uint32)
    return ((vb << 16) | ub).reshape(ne, d // LANES, LANES)


def _peergather_kernel(idx_ref, idxn_ref, gate_ref, h_ref, x_ref, mod_ref, tab_ref, o_ref,
                       *scratch, tb):
    nsel = PEER_HEADS * PEER_TOPK
    nrow = h_ref.shape[1] // LANES
    pitch = nrow + SLAB_PAD
    ns = GATHER_SLOTS
    bufs, sem = scratch[:ns], scratch[ns]
    lane_t = lax.broadcasted_iota(jnp.int32, (nsel, tb), 1)
    g2 = mod_ref[5:6, :]
    step = pl.program_id(0)
    nsteps = pl.num_programs(0)

    def row_copy(ids_ref, t, j, slot):
        return pltpu.make_async_copy(
            tab_ref.at[ids_ref[t, j]],
            bufs[slot].at[pl.ds(j * pitch, nrow), :],
            sem.at[slot])

    def issue(ids_ref, t, slot):
        for j in range(nsel):
            row_copy(ids_ref, t, j, slot).start(priority=j % 2)

    def wait(t, slot):
        for j in range(nsel):
            row_copy(idx_ref, t, j, slot).wait()

    def packed(slot, s):
        return bufs[slot][pl.ds(s, nsel, stride=pitch), :]

    def compute(t, slot):
        hrow = h_ref[pl.ds(t, 1), :]
        acc = jnp.zeros((nsel, LANES), F32)
        for s in range(nrow):
            u = lax.bitcast_convert_type(packed(slot, s) << 16, F32)
            acc = acc + u * hrow[:, LANES * s:LANES * (s + 1)]
        dots = jnp.sum(acc, axis=-1, keepdims=True)
        gcol = jnp.sum(jnp.where(lane_t == t, gate_ref[...], 0.0), axis=-1, keepdims=True)
        coef = gcol * _gelu(dots)
        outs = []
        for s in range(nrow):
            vv = lax.bitcast_convert_type(packed(slot, s) & jnp.uint32(0xFFFF0000), F32)
            outs.append(jnp.sum(coef * vv, axis=0, keepdims=True))
        orow = jnp.concatenate(outs, axis=1)
        o_ref[pl.ds(t, 1), :] = x_ref[pl.ds(t, 1), :] + g2 * orow

    @pl.when(step == 0)
    def _():
        for s in range(ns - 1):
            issue(idx_ref, s, s)

    ngroups = tb // ns

    def body(g, carry):
        for s in range(ns):
            t = g * ns + s
            wait(t, s)
            issue(idx_ref, t + ns - 1, (s + ns - 1) % ns)
            compute(t, s)
        return carry

    lax.fori_loop(0, ngroups - 1, body, 0)
    for s in range(ns):
        t = (ngroups - 1) * ns + s
        wait(t, s)
        if s == 0:
            issue(idx_ref, tb - 1, ns - 1)
        else:
            @pl.when(step < nsteps - 1)
            def _():
                issue(idxn_ref, s - 1, s - 1)
        compute(t, s)


def _peer_gather(geo_g, eidx, gate_t, h2, x, mods, table, n):
    n_all, d = x.shape
    tb = geo_g.tm
    nsel = PEER_HEADS * PEER_TOPK
    pitch = d // LANES + SLAB_PAD
    assert n % tb == 0
    nsteps = n // tb
    return pl.pallas_call(
        functools.partial(_peergather_kernel, tb=tb),
        grid=(nsteps,),
        in_specs=[
            pl.BlockSpec((tb, nsel), lambda i: (i, 0), memory_space=pltpu.SMEM),
            pl.BlockSpec((tb, nsel), lambda i: (jnp.minimum(i + 1, nsteps - 1), 0),
                         memory_space=pltpu.SMEM),
            pl.BlockSpec((nsel, tb), lambda i: (0, i)),
            pl.BlockSpec((tb, d), lambda i: (i, 0)),
            pl.BlockSpec((tb, d), lambda i: (i, 0)),
            geo_g.mod_spec(),
            pl.BlockSpec(memory_space=pl.ANY),
        ],
        out_specs=pl.BlockSpec((tb, d), lambda i: (i, 0)),
        out_shape=jax.ShapeDtypeStruct((n_all, d), F32),
        scratch_shapes=[pltpu.VMEM((nsel * pitch, LANES), jnp.uint32)] * GATHER_SLOTS
        + [pltpu.SemaphoreType.DMA((GATHER_SLOTS,))],
        compiler_params=_cparams("arbitrary"),
        name="peer_gather",
    )(eidx, eidx, gate_t, h2, x, mods, table)


def _scres_kernel(full_ref, x_ref, y_ref, mod_ref, o_ref):
    del full_ref
    o_ref[...] = x_ref[...] + mod_ref[5:6, :] * y_ref[...]


def _sc_rows_residual(geo, out_full, x, y_sc, mods, row0):
    n, d = x.shape
    tm, tpb, nct = geo.tm, geo.tpb, geo.nct
    assert row0 % tm == 0 and y_sc.shape[0] % tm == 0
    tile0 = row0 // tm

    def modmap(i):
        t = tile0 + i
        return (t // tpb, ((t % tpb) >= nct).astype(jnp.int32), 0, 0)

    return pl.pallas_call(
        _scres_kernel,
        grid=(y_sc.shape[0] // tm,),
        in_specs=[
            pl.BlockSpec(memory_space=pl.ANY),
            pl.BlockSpec((tm, d), lambda i: (tile0 + i, 0)),
            pl.BlockSpec((tm, d), lambda i: (i, 0)),
            pl.BlockSpec((None, None, 6, d), modmap),
        ],
        out_specs=pl.BlockSpec((tm, d), lambda i: (tile0 + i, 0)),
        out_shape=jax.ShapeDtypeStruct((n, d), F32),
        input_output_aliases={0: 0},
        compiler_params=_cparams("parallel"),
        name="sc_rows_residual",
    )(out_full, x, y_sc, mods)


SC_CORES = 2
SC_SUBCORES = 16
SC_LANES = 16
SC_CHUNK = 16
SC_SHARE = 0.3235

_ERF_ALPHA = (0.00022905065861350646, 0.0034082910107109506, 0.050955695062380861,
              0.18520832239976145, 1.128379143519084)
_ERF_BETA = (-1.1791602954361697e-7, 0.000023547966471313185, 0.0010179625278914885,
             0.014070470171167667, 0.11098505178285362, 0.49746925110067538, 1.0)
_ERF_CLAMP = 3.832506856900711


def _erf_rational(x):
    x = jnp.minimum(jnp.maximum(x, -_ERF_CLAMP), _ERF_CLAMP)
    x2 = x * x
    p = jnp.full_like(x, _ERF_ALPHA[0])
    for c in _ERF_ALPHA[1:]:
        p = p * x2 + c
    q = jnp.full_like(x, _ERF_BETA[0])
    for c in _ERF_BETA[1:]:
        q = q * x2 + c
    return x * p / q


def _peer_sc(eidx, gate, h, table, row0):
    n, nsel = eidx.shape
    d = h.shape[1]
    nw = SC_CORES * SC_SUBCORES
    assert n % (2 * nw) == 0 and nsel % SC_CHUNK == 0 and d % SC_LANES == 0
    tpw = n // nw
    nchunk = nsel // SC_CHUNK
    nvec = d // SC_LANES
    mesh = plsc.VectorSubcoreMesh(core_axis_name="c", subcore_axis_name="s",
                                  num_cores=SC_CORES, num_subcores=SC_SUBCORES)

    @functools.partial(
        pl.kernel, mesh=mesh,
        out_type=jax.ShapeDtypeStruct((n, d), F32),
        scratch_types=[
            pltpu.VMEM((nsel,), jnp.int32), pltpu.VMEM((nsel,), jnp.int32),
            pltpu.VMEM((nsel,), F32), pltpu.VMEM((nsel,), F32),
            pltpu.VMEM((d,), F32), pltpu.VMEM((d,), F32),
            pltpu.VMEM((d,), F32),
            pltpu.VMEM((SC_CHUNK, d), jnp.uint32),
            pltpu.VMEM((SC_CHUNK, d), jnp.uint32),
            pltpu.SemaphoreType.DMA, pltpu.SemaphoreType.DMA,
            pltpu.SemaphoreType.DMA, pltpu.SemaphoreType.DMA,
        ],
        compiler_params=pltpu.CompilerParams(needs_layout_passes=False),
        name="peer_sc",
    )
    def sc_kernel(eidx_hbm, gate_hbm, h_hbm, tab_hbm, y_hbm, idx_a, idx_b, gate_a, gate_b, h_a, h_b,
                  o_v, rows0, rows1, sem0, sem1, msem_a, msem_b):
        wid = lax.axis_index("s") * SC_CORES + lax.axis_index("c")
        lanes = lax.iota(jnp.int32, SC_LANES)
        zero = jnp.zeros((SC_LANES,), F32)
        bufs = ((rows0, sem0), (rows1, sem1))
        meta = ((idx_a, gate_a, h_a, msem_a), (idx_b, gate_b, h_b, msem_b))

        def meta_copies(tok, s):
            idx_v, gate_v, h_v, msem = meta[s]
            return (pltpu.make_async_copy(eidx_hbm.at[tok], idx_v, msem),
                    pltpu.make_async_copy(gate_hbm.at[tok], gate_v, msem),
                    pltpu.make_async_copy(h_hbm.at[row0 + tok], h_v, msem))

        def gather(s, c, slot):
            rows, sem = bufs[slot]
            return pltpu.make_async_copy(
                tab_hbm.at[meta[s][0].at[pl.ds(c * SC_CHUNK, SC_CHUNK)]], rows, sem)

        def chunk_compute(s, c, rows):
            _, gate_v, h_v, _ = meta[s]

            def dot_body(i, accs):
                hv = h_v[pl.ds(i * SC_LANES, SC_LANES)]
                out = []
                for e in range(SC_CHUNK):
                    w = rows[e, pl.ds(i * SC_LANES, SC_LANES)]
                    out.append(accs[e] + lax.bitcast_convert_type(w << 16, F32) * hv)
                return tuple(out)

            accs = lax.fori_loop(0, nvec, dot_body, (zero,) * SC_CHUNK)
            dots = zero
            for e in range(SC_CHUNK):
                dots = jnp.where(lanes == e, jnp.sum(accs[e]), dots)
            act = 0.5 * dots * (1.0 + _erf_rational(dots * (2.0 ** -0.5)))
            coef = gate_v[pl.ds(c * SC_CHUNK, SC_CHUNK)] * act
            splat = [jnp.full((SC_LANES,), jnp.sum(jnp.where(lanes == e, coef, 0.0)), F32)
                     for e in range(SC_CHUNK)]

            def acc_body(i, carry):
                o = o_v[pl.ds(i * SC_LANES, SC_LANES)]
                for e in range(SC_CHUNK):
                    w = rows[e, pl.ds(i * SC_LANES, SC_LANES)]
                    o = o + splat[e] * lax.bitcast_convert_type(w & jnp.uint32(0xFFFF0000), F32)
                o_v[pl.ds(i * SC_LANES, SC_LANES)] = o
                return carry

            lax.fori_loop(0, nvec, acc_body, 0)

        def zero_body(i, c2):
            o_v[pl.ds(i * SC_LANES, SC_LANES)] = zero
            return c2

        def token(tok, s, has_next):
            def when_next(fn):
                if isinstance(has_next, bool):
                    if has_next:
                        fn()
                else:
                    pl.when(has_next)(fn)

            def load_next():
                for cp in meta_copies(tok + 1, 1 - s):
                    cp.start()

            when_next(load_next)
            lax.fori_loop(0, nvec, zero_body, 0)
            for c in range(nchunk):
                if c + 1 < nchunk:
                    gather(s, c + 1, (c + 1) % 2).start()
                gather(s, c, c % 2).wait()
                chunk_compute(s, c, bufs[c % 2][0])

            def prefetch_next():
                for cp in meta_copies(tok + 1, 1 - s):
                    cp.wait()
                gather(1 - s, 0, 0).start()

            when_next(prefetch_next)
            pltpu.sync_copy(o_v, y_hbm.at[tok])

        base = wid * tpw
        for cp in meta_copies(base, 0):
            cp.start()
        for cp in meta_copies(base, 0):
            cp.wait()
        gather(0, 0, 0).start()

        def pair_body(g, carry):
            token(base + 2 * g, 0, True)
            token(base + 2 * g + 1, 1, g + 1 < tpw // 2)
            return carry

        lax.fori_loop(0, tpw // 2, pair_body, 0)

    return sc_kernel(eidx, gate, h, table)


def _pad_to(x, axis, size):
    pad = [(0, 0)] * x.ndim
    pad[axis] = (0, size - x.shape[axis])
    return jnp.pad(x, pad)


def _rwkv_layer(geo, xs, mods, norm1, mix, w_rkv, w_o, w0, w1, w2, a0, a1, a2, vl, g1, g2,
                k_k, k_a, r_k, ln_w, ln_b, vfirst):
    d = geo.d
    xm = _rwkv_mix(geo, xs, norm1, mods, mix)
    rkv = _bmm(xm, w_rkv.astype(BF16), (0, 2, 3), ("none",) * 3, F32, geo.tm)
    lt = g1.shape[1]
    w1c = jnp.concatenate([_pad_to(w1[0], 1, LORA_PAD), _pad_to(w1[1], 1, LORA_PAD)], axis=1)
    a1c = jnp.concatenate([_pad_to(a1[0], 1, LORA_PAD), _pad_to(a1[1], 1, LORA_PAD)], axis=1)
    if vl is None:
        v1p = jnp.zeros((d, lt), F32)
        v2p = jnp.zeros((LORA_PAD, d), F32)
        v0 = jnp.zeros((d,), F32)
    else:
        v0, v1, v2 = vl
        v1p = _pad_to(v1, 1, lt)
        v2p = _pad_to(v2, 0, LORA_PAD)
    wl1 = jnp.stack([_pad_to(w1c, 1, lt), _pad_to(a1c, 1, lt), g1, v1p]).astype(BF16)
    tl = _bmm(xm, wl1, (1, 4, 5, 3), ("tanh", "none", "sigmoid", "none"), BF16, geo.tm)
    w2p = jnp.stack([_pad_to(w2[0], 0, LORA_PAD), _pad_to(w2[1], 0, LORA_PAD)]).astype(BF16)
    a2p = jnp.stack([_pad_to(a2[0], 0, LORA_PAD), _pad_to(a2[1], 0, LORA_PAD)]).astype(BF16)
    pvec = jnp.stack([w0[0], w0[1], a0[0], a0[1], v0, k_k, k_a, jnp.zeros_like(k_k)])
    lw, kd, asg, kk, g, v = _rwkv_feat(geo, rkv, tl, w2p, a2p, g2.astype(BF16), v2p.astype(BF16),
                                      pvec, vfirst)
    r = (rkv, 0)
    wkv = _wkv_bidir(r, v, kk, lw, kd, asg, geo.b, geo.l)
    pv2 = _pad_to(jnp.stack([r_k, ln_w, ln_b]), 0, SUBLANES)
    y = _rwkv_readout(geo, wkv, r, kd, v, g, pv2)
    xs = _matmul_res(geo, y, w_o.astype(BF16), xs, mods, 2)
    return xs, v


def _rope_tables(geo):
    t = geo.t
    pos = jnp.arange(t)
    row = (pos // GRID_W).astype(F32)
    col = (pos % GRID_W).astype(F32)
    n_freq = QK_ROPE // 4
    inv_freq = ROPE_THETA ** (-jnp.arange(n_freq, dtype=F32) / n_freq)
    ang = jnp.concatenate([row[:, None] * inv_freq, col[:, None] * inv_freq], axis=-1)
    cos, sin = jnp.cos(ang), jnp.sin(ang)
    pad = LANES - QK_ROPE
    cos_l = jnp.concatenate([cos, cos, jnp.ones((t, pad), F32)], axis=1)
    sin_l = jnp.concatenate([-sin, sin, jnp.zeros((t, pad), F32)], axis=1)
    cos_c = jnp.ones((geo.l, LANES), F32)
    sin_c = jnp.zeros((geo.l, LANES), F32)
    return jnp.concatenate([cos_c, cos_l], axis=0), jnp.concatenate([sin_c, sin_l], axis=0)


def _mla_layer(geo, xs, mods, norm1, rope_t, w_in, q_norm, kv_norm, w_uq, w_ukv, g_q, g_k, w_o):
    q_lora, kv_lora = q_norm.shape[0], kv_norm.shape[0]
    zw = q_lora + kv_lora + LANES
    z = _mod_matmul(geo, xs, norm1, mods, _pad_to(w_in, 1, zw).astype(BF16), 0, False)
    qk = QK_NOPE + QK_ROPE
    wuq_p = _pad_to(w_uq.reshape(q_lora, MLA_HEADS, qk), 2, QK_PAD).reshape(q_lora, -1)
    gq_p = _pad_to(g_q, 0, QK_PAD).reshape(1, QK_PAD)
    gk_p = _pad_to(g_k, 0, QK_PAD).reshape(1, QK_PAD)
    q, k, v = _mla_qkv(geo, z, q_norm, kv_norm, wuq_p.astype(BF16), w_ukv.astype(BF16),
                       gq_p, gk_p, *rope_t)
    o = _mla_attention(geo, q, k, v)
    return _matmul_res(geo, o, w_o.astype(BF16), xs, mods, 2)


def _peer_layer(geo, geo_g, xs, mods, norm2, w_q, q_norm, keys, u, v):
    qp, h2 = _mod_matmul(geo, xs, norm2, mods, w_q.astype(BF16), 1, True)
    table = _pack_expert_table(u, v)
    n, d = xs.shape
    unit = 2 * geo.tm
    n_sc = unit * math.ceil(SC_SHARE * n / unit) if n % unit == 0 else 0
    if n_sc % (2 * SC_CORES * SC_SUBCORES) != 0 or n_sc >= n:
        n_sc = 0
    n_tc = n - n_sc

    def select(row0, nrows):
        wide = 2 * geo.tm
        tsel = wide if row0 % wide == 0 and nrows % wide == 0 else geo.tm
        return _peer_select(qp, q_norm, keys, tsel, row0, nrows)

    if n_sc:
        eidx_sc, gate_sc = select(n_tc, n_sc)
        y_sc = _peer_sc(eidx_sc.T, gate_sc.T, h2, table.reshape(table.shape[0], d), n_tc)
    eidx_t, gate_t = select(0, n_tc)
    out_tc = _peer_gather(geo_g, eidx_t.T, gate_t, h2, xs, mods, table, n_tc)
    if n_sc == 0:
        return out_tc
    return _sc_rows_residual(geo, out_tc, xs, y_sc, mods, n_tc)


def kernel(x, c, ctx, c_ctx, w_ada, b_ada, norm1, norm2, rw_mix, rw_wrkv, rw_wo, rw_w0, rw_w1, rw_w2, rw_a0, rw_a1, rw_a2, rw_v0, rw_v1, rw_v2, rw_g1, rw_g2, rw_kk, rw_ka, rw_rk, rw_lnw, rw_lnb, mla_win, mla_qnorm, mla_kvnorm, mla_wuq, mla_wukv, mla_gq, mla_gk, mla_wo, peer_wq, peer_qnorm, peer_keys, peer_u, peer_v):
    b, t, d = x.shape
    l = ctx.shape[1]
    depth = w_ada.shape[0]
    geo = _Geom(b, l, t, d, min(ROW_TILE, l))
    geo_g = _Geom(b, l, t, d, min(GATHER_TILE, l))
    cond8 = _pad_to(jnp.concatenate([c, c_ctx[None, :]], axis=0), 0, SUBLANES)
    ada = _adaln(cond8, w_ada, b_ada).reshape(depth, SUBLANES, 6, d)
    mods_all = jnp.stack([jnp.broadcast_to(ada[:, b:b + 1], (depth, b, 6, d)), ada[:, 0:b]], axis=2)
    xs = jnp.concatenate([ctx, x], axis=1).reshape(b * (l + t), d)
    rope_t = _rope_tables(geo)
    vfirst = None
    for i in range(depth):
        j = i // 2
        mods = mods_all[i]
        if i % 2 == 0:
            vl = None if j == 0 else (rw_v0[j - 1], rw_v1[j - 1], rw_v2[j - 1])
            xs, vcur = _rwkv_layer(geo, xs, mods, norm1[i], rw_mix[j], rw_wrkv[j], rw_wo[j],
                                   rw_w0[j], rw_w1[j], rw_w2[j], rw_a0[j], rw_a1[j], rw_a2[j], vl,
                                   rw_g1[j], rw_g2[j], rw_kk[j], rw_ka[j], rw_rk[j], rw_lnw[j],
                                   rw_lnb[j], vfirst)
            if j == 0:
                vfirst = vcur
        else:
            xs = _mla_layer(geo, xs, mods, norm1[i], rope_t, mla_win[j], mla_qnorm[j],
                            mla_kvnorm[j], mla_wuq[j], mla_wukv[j], mla_gq[j], mla_gk[j], mla_wo[j])
        if i == depth - 1:
            xs = xs.reshape(b, l + t, d)[:, l:, :].reshape(b * t, d)
            geo, geo_g = _Geom(b, 0, t, d, geo.tm), _Geom(b, 0, t, d, geo_g.tm)
        xs = _peer_layer(geo, geo_g, xs, mods, norm2[i], peer_wq[i], peer_qnorm[i], peer_keys[i],
                         peer_u[i], peer_v[i])
    return xs.reshape(b, t, d)
```
